```python
import math, functools
import jax, jax.numpy as jnp
from jax import lax
import numpy as np

D_MODEL = 1024
BATCH = 4
SEQ = 4096
DEPTH = 2
DEC_BATCH = 32
DEC_SEQ = 1
PAST_LEN = 8192
PAGE_SIZE = 128

F32 = jnp.float32
EPS = 1e-6
NEG = -1e30
FORCE = 1e4
D_RNN = 512
RG_BLOCKS = 8
RG_BW = D_RNN // RG_BLOCKS
CONV_W = 4
RG_C = 8.0
N_HEADS = 8
KV_HEADS = 2
HPG = N_HEADS // KV_HEADS
HEAD_DIM = 64
L_CMP = 32
CMP_STRIDE = 16
CMP_HIDDEN = 256
SEL_BLOCK = 64
N_SELECT = 16
WINDOW = 512
Q_BLOCK = 128
D_POOL = 512
POOL_WINDOWS = (2, 4, 8, 16)
POOL_GROUPS = 4
POOL_GW = D_POOL // POOL_GROUPS
POOL_MAX = 16
REL_BUCKETS = 32
REL_MAX_DIST = 1024
D_FF = 2816
N_EXPERTS = 8
TOP_K = 2
D_FF_EXPERT = 1408
N_DENSE = (DEPTH + 1) // 2
N_MOE = DEPTH // 2
SPLITS = (D_RNN, D_RNN, N_HEADS * HEAD_DIM, 2 * KV_HEADS * HEAD_DIM, 2 * KV_HEADS * HEAD_DIM,
          2 * KV_HEADS * HEAD_DIM, 3 * N_HEADS, D_POOL, 3 * D_MODEL)
IN_COLS = sum(SPLITS)

kernel_name = 'hybrid_rglru_nsa_pool_decoder'


def rms_norm(x, g):
    xf = x.astype(F32)
    y = xf * lax.rsqrt(jnp.mean(xf * xf, axis=-1, keepdims=True) + EPS)
    return (y * g.astype(F32)).astype(x.dtype)


def norm_keys(kv, g):
    return jnp.stack([rms_norm(kv[:, :, 0], g), kv[:, :, 1]], axis=2)


def rel_bucket(dist):
    n_exact = REL_BUCKETS // 2
    d = jnp.maximum(dist, 0)
    df = jnp.maximum(d, 1).astype(F32)
    large = n_exact + (jnp.log(df / n_exact) / math.log(REL_MAX_DIST / n_exact)
                       * (REL_BUCKETS - n_exact)).astype(jnp.int32)
    return jnp.where(d < n_exact, d, jnp.minimum(large, REL_BUCKETS - 1))


def masked_probs(logits, valid):
    logits = jnp.where(valid, logits, NEG)
    m = jnp.max(logits, axis=-1, keepdims=True)
    p = jnp.where(valid, jnp.exp(logits - m), 0.0)
    return p / jnp.maximum(jnp.sum(p, axis=-1, keepdims=True), 1e-30)


def causal_conv(x, buf, w, b):
    xp = jnp.concatenate([buf.astype(x.dtype), x], axis=1)
    y = lax.conv_general_dilated(xp, w[:, None, :].astype(x.dtype), (1,), 'VALID',
                                 dimension_numbers=('NWC', 'WIO', 'NWC'),
                                 feature_group_count=x.shape[-1]) + b
    return y, xp[:, -(CONV_W - 1):]


def rg_lru(xc, h0, w_a, b_a, w_x, b_x, lam):
    B, T, _ = xc.shape
    xb = xc.reshape(B, T, RG_BLOCKS, RG_BW)
    r = jax.nn.sigmoid((jnp.einsum('btnc,ncd->btnd', xb, w_a).reshape(B, T, D_RNN) + b_a).astype(F32))
    i = jax.nn.sigmoid((jnp.einsum('btnc,ncd->btnd', xb, w_x).reshape(B, T, D_RNN) + b_x).astype(F32))
    log_a = -RG_C * r * jax.nn.softplus(-lam.astype(F32))
    a = jnp.exp(log_a)
    u = jnp.sqrt(-jnp.expm1(2.0 * log_a)) * (i * xc.astype(F32))
    u = u.at[:, 0].add(a[:, 0] * h0.astype(F32))

    def combine(lhs, rhs):
        a1, b1 = lhs
        a2, b2 = rhs
        return a1 * a2, a2 * b1 + b2

    _, h = lax.associative_scan(combine, (a, u), axis=1)
    return h, h[:, -1]


def pool_mix(xin, buf, start_pos, w_pool, scale):
    B, T, C = xin.shape
    xf = jnp.concatenate([buf.astype(xin.dtype), xin], axis=1).astype(F32)
    cs = jnp.concatenate([jnp.zeros((B, 1, C), F32), jnp.cumsum(xf, axis=1)], axis=1)
    pos = start_pos + jnp.arange(T)
    means = []
    for g, w in enumerate(POOL_WINDOWS):
        sl = slice(g * POOL_GW, (g + 1) * POOL_GW)
        s = cs[:, POOL_MAX:POOL_MAX + T, sl] - cs[:, POOL_MAX - w:POOL_MAX - w + T, sl]
        cnt = jnp.minimum(pos + 1, w).astype(F32)[None, :, None]
        means.append(s / cnt)
    mixed = (jnp.concatenate(means, axis=-1) - xf[:, POOL_MAX - 1:]).astype(xin.dtype)
    y = jnp.einsum('btgc,gcd->btgd', mixed.reshape(B, T, POOL_GROUPS, POOL_GW), w_pool)
    return y.reshape(B, T, D_POOL) * scale, xf[:, -(POOL_MAX - 1):].astype(xin.dtype)


def compress_kv(rows, w1, w2, pe, kn):
    Bx, T = rows.shape[:2]
    n_ch = T // CMP_STRIDE
    ch = rows[:, :n_ch * CMP_STRIDE].reshape(Bx, n_ch, CMP_STRIDE, 2, KV_HEADS, HEAD_DIM)
    first = jnp.einsum('bnsegd,esdf->bnegf', ch, w1[:, :CMP_STRIDE])
    second = jnp.einsum('bnsegd,esdf->bnegf', ch, w1[:, CMP_STRIDE:])
    pe_term = jnp.einsum('led,eldf->ef', pe, w1)
    h = first[:, :-1] + second[:, 1:] + pe_term[:, None, :]
    comp = jnp.einsum('bnegf,efd->bnegd', jax.nn.gelu(h), w2)
    nc = comp.shape[1]
    comp_end = jnp.arange(nc) * CMP_STRIDE + L_CMP - 1
    return rms_norm(comp[:, :, 0], kn), comp[:, :, 1], comp_end


def overlap_matrix(n_cmp, n_sel):
    c0 = jnp.arange(n_cmp)[:, None] * CMP_STRIDE
    s0 = jnp.arange(n_sel)[None, :] * SEL_BLOCK
    return ((c0 <= s0 + SEL_BLOCK - 1) & (c0 + L_CMP - 1 >= s0)).astype(F32)


def nsa_attend(q, q_pos, comp_k, comp_v, comp_end, overlap, fetch_sel, win_k, win_v, win_pos, gates, rel_bias):
    Bq, Tq = q.shape[:2]
    scale = HEAD_DIM ** -0.5
    qg = q.reshape(Bq, Tq, KV_HEADS, HPG, HEAD_DIM)
    rb = rel_bias.astype(F32)

    def head_bias(buckets):
        return jnp.moveaxis(rb[buckets], -1, 1).reshape(Tq, KV_HEADS, HPG, -1)

    dist_c = q_pos[:, None] - comp_end[None, :]
    lc = jnp.einsum('btgjd,bcgd->btgjc', qg, comp_k, preferred_element_type=F32) * scale + head_bias(rel_bucket(dist_c))
    pc = masked_probs(lc, (dist_c >= 0)[None, :, None, None, :])
    o_cmp = jnp.einsum('btgjc,bcgd->btgjd', pc, comp_v)
    ns = overlap.shape[1]
    imp = jnp.einsum('btgjc,cn->btgn', pc, overlap)
    blk = jnp.arange(ns)
    cur = (q_pos // SEL_BLOCK)[None, :, None, None]
    forced = (blk == 0) | (blk == cur) | (blk == cur - 1)
    score = jnp.where(blk > cur, -1.0, jnp.where(forced, FORCE, imp))
    n_sel = min(N_SELECT, ns)
    _, idx = lax.top_k(score, n_sel)
    kv_s = fetch_sel(idx)
    k_s = kv_s[..., 0, :].reshape(Bq, Tq, KV_HEADS, n_sel * SEL_BLOCK, HEAD_DIM)
    v_s = kv_s[..., 1, :].reshape(Bq, Tq, KV_HEADS, n_sel * SEL_BLOCK, HEAD_DIM)
    pos_s = (idx[..., None] * SEL_BLOCK + jnp.arange(SEL_BLOCK)).reshape(Bq, Tq, KV_HEADS, -1)
    dist_s = q_pos[None, :, None, None] - pos_s
    bias_s = rb.reshape(REL_BUCKETS, KV_HEADS, HPG)[rel_bucket(dist_s), jnp.arange(KV_HEADS)[:, None]]
    ls = jnp.einsum('btgjd,btgkd->btgjk', qg, k_s, preferred_element_type=F32) * scale + jnp.moveaxis(bias_s, -1, 3)
    ps = masked_probs(ls, (dist_s >= 0)[:, :, :, None, :])
    o_sel = jnp.einsum('btgjk,btgkd->btgjd', ps, v_s)
    dist_w = q_pos[:, None] - win_pos[None, :]
    lw = jnp.einsum('btgjd,bwgd->btgjw', qg, win_k, preferred_element_type=F32) * scale + head_bias(rel_bucket(dist_w))
    valid_w = (dist_w >= 0) & (dist_w <= WINDOW) & (win_pos[None, :] >= 0)
    pw = masked_probs(lw, valid_w[None, :, None, None, :])
    o_win = jnp.einsum('btgjw,bwgd->btgjd', pw, win_v)
    g = jax.nn.sigmoid(gates.astype(F32)).reshape(Bq, Tq, KV_HEADS, HPG, 3)
    o = g[..., 0:1] * o_cmp + g[..., 1:2] * o_sel + g[..., 2:3] * o_win
    return o.reshape(Bq, Tq, N_HEADS * HEAD_DIM).astype(q.dtype)


def nsa_prompt(q, kv_cmp, kv_sel, kv_win, gates, *, w1, w2, pe, kn, rel_bias):
    B, S = q.shape[:2]
    comp_k, comp_v, comp_end = compress_kv(kv_cmp, w1, w2, pe, kn)
    ns = S // SEL_BLOCK
    ovl = overlap_matrix(comp_k.shape[1], ns)
    sel_blocks = kv_sel.reshape(B, ns, SEL_BLOCK, 2, KV_HEADS, HEAD_DIM)
    win_pad = jnp.concatenate([jnp.zeros((B, WINDOW) + kv_win.shape[2:], kv_win.dtype), kv_win], axis=1)
    nqb = S // Q_BLOCK
    g_idx = jnp.arange(KV_HEADS)[None, :, None]

    def one_block(args):
        qb_q, qb_g, b, qb = args
        q_pos = qb * Q_BLOCK + jnp.arange(Q_BLOCK)
        ck = lax.dynamic_index_in_dim(comp_k, b, 0, keepdims=True)
        cv = lax.dynamic_index_in_dim(comp_v, b, 0, keepdims=True)
        blocks_b = lax.dynamic_index_in_dim(sel_blocks, b, 0, keepdims=False)

        def fetch(idx):
            return blocks_b[idx[0], :, :, g_idx][None]

        win = lax.dynamic_slice(win_pad, (b, qb * Q_BLOCK, 0, 0, 0),
                                (1, WINDOW + Q_BLOCK, 2, KV_HEADS, HEAD_DIM))
        win_pos = qb * Q_BLOCK - WINDOW + jnp.arange(WINDOW + Q_BLOCK)
        return nsa_attend(qb_q[None], q_pos, ck, cv, comp_end, ovl, fetch, win[:, :, 0], win[:, :, 1],
                          win_pos, qb_g[None], rel_bias)[0]

    qs = q.reshape(B * nqb, Q_BLOCK, N_HEADS, HEAD_DIM)
    gs = gates.reshape(B * nqb, Q_BLOCK, N_HEADS, 3)
    bi = jnp.repeat(jnp.arange(B), nqb)
    qi = jnp.tile(jnp.arange(nqb), B)
    out = lax.map(one_block, (qs, gs, bi, qi))
    return out.reshape(B, S, N_HEADS * HEAD_DIM), kv_win[:, -min(WINDOW, S):]


def nsa_sample(q, kv_cmp_new, kv_sel_new, kv_win_new, gates, *, cmp_pool, sel_pool, win_buf, page_table,
               w1, w2, pe, kn, rel_bias):
    DB, DS = q.shape[:2]
    n_pages = page_table.shape[1]
    past = n_pages * PAGE_SIZE
    q_pos = past + jnp.arange(DS)
    past_cmp = cmp_pool[page_table].reshape(DB, past, 2, KV_HEADS, HEAD_DIM)
    rows = jnp.concatenate([past_cmp, kv_cmp_new.astype(past_cmp.dtype)], axis=1)
    comp_k, comp_v, comp_end = compress_kv(rows, w1, w2, pe, kn)
    total = past + DS
    ns = -(-total // SEL_BLOCK)
    nbp = past // SEL_BLOCK
    nnb = ns - nbp
    bpp = PAGE_SIZE // SEL_BLOCK
    ovl = overlap_matrix(comp_k.shape[1], ns)
    pool_blocks = sel_pool.reshape(-1, SEL_BLOCK, 2, KV_HEADS, HEAD_DIM)
    new_pad = jnp.pad(kv_sel_new.astype(sel_pool.dtype), ((0, 0), (0, nnb * SEL_BLOCK - DS), (0, 0), (0, 0), (0, 0)))
    new_blocks = new_pad.reshape(DB, nnb, SEL_BLOCK, 2, KV_HEADS, HEAD_DIM)
    b_idx = jnp.arange(DB)[:, None, None, None]
    g_idx = jnp.arange(KV_HEADS)[None, None, :, None]

    def fetch(idx):
        ip = jnp.minimum(idx, nbp - 1)
        phys = page_table[b_idx, ip // bpp] * bpp + ip % bpp
        from_past = pool_blocks[phys, :, :, g_idx]
        from_new = new_blocks[b_idx, jnp.clip(idx - nbp, 0, nnb - 1), :, :, g_idx]
        return jnp.where((idx >= nbp)[..., None, None, None], from_new, from_past)

    wb = win_buf.shape[1]
    win = jnp.concatenate([win_buf, kv_win_new.astype(win_buf.dtype)], axis=1)
    win_pos = past - wb + jnp.arange(wb + DS)
    out = nsa_attend(q, q_pos, comp_k, comp_v, comp_end, ovl, fetch, win[:, :, 0], win[:, :, 1], win_pos, gates, rel_bias)
    return out, win[:, -wb:]


def token_mixer(x, l, P, start_pos, conv_buf, h0, pool_buf, attend):
    B, T, _ = x.shape
    xn = rms_norm(x, P['attn_norm'][l])
    z = xn @ P['w_in'][l]
    cut = np.cumsum(SPLITS)[:-1].tolist()
    x_rg, g_rg, q, kv_c, kv_s, kv_w, g_nsa, x_pool, g_br = jnp.split(z, cut, axis=-1)
    xc, conv_new = causal_conv(x_rg, conv_buf, P['conv_w'][l], P['conv_b'][l])
    h, h_last = rg_lru(xc, h0, P['rg_w_a'][l], P['rg_b_a'][l], P['rg_w_x'][l], P['rg_b_x'][l], P['rg_lambda'][l])
    y_a = (h * jax.nn.gelu(g_rg.astype(F32))).astype(x.dtype)
    q = rms_norm(q.reshape(B, T, N_HEADS, HEAD_DIM), P['q_norm'][l])
    kv_shape = (B, T, 2, KV_HEADS, HEAD_DIM)
    kv_c = kv_c.reshape(kv_shape)
    kv_s = norm_keys(kv_s.reshape(kv_shape), P['k_norm'][l, 1])
    kv_w = norm_keys(kv_w.reshape(kv_shape), P['k_norm'][l, 2])
    y_b, win_new = attend(q, kv_c, kv_s, kv_w, g_nsa.reshape(B, T, N_HEADS, 3))
    y_c, pool_new = pool_mix(x_pool, pool_buf, start_pos, P['w_pool'][l], P['pool_scale'][l])
    g = jax.nn.sigmoid(g_br.astype(F32)).reshape(B, T, 3, D_MODEL)
    merged = (g[:, :, 0] * (y_a @ P['w_br_rg'][l]) + g[:, :, 1] * (y_b @ P['w_br_attn'][l])
              + g[:, :, 2] * (y_c @ P['w_br_pool'][l]))
    out = merged.astype(x.dtype) @ P['w_out'][l]
    return x + out, (kv_c, kv_s, win_new, conv_new, h_last, pool_new)


def swiglu(x, wg, wu, wd):
    return (jax.nn.silu(x @ wg) * (x @ wu)) @ wd


def channel_mixer(x, l, P):
    xn = rms_norm(x, P['ffn_norm'][l])
    i = l // 2
    if l % 2 == 0:
        y = swiglu(xn, P['ffn_w_gate'][i], P['ffn_w_up'][i], P['ffn_w_down'][i]).astype(F32)
    else:
        logits = (xn @ P['w_router'][i]).astype(F32) + P['b_router'][i].astype(F32)
        vals, idx = lax.top_k(logits, TOP_K)
        wts = jax.nn.softmax(vals, axis=-1)
        gate = jnp.sum(jax.nn.one_hot(idx, N_EXPERTS, dtype=F32) * wts[..., None], axis=-2)
        y = jnp.zeros(x.shape, F32)
        for e in range(N_EXPERTS):
            y = y + gate[..., e:e + 1] * swiglu(xn, P['moe_w_gate'][i, e], P['moe_w_up'][i, e], P['moe_w_down'][i, e])
    return x + y.astype(x.dtype)


def setup_inputs(seed: int = 0) -> dict:
    key = jax.random.key(seed)
    keys = iter(jax.random.split(key, 64))

    def nrm(shape, s):
        return jax.random.normal(next(keys), shape, F32) * s

    n_pages = PAST_LEN // PAGE_SIZE
    n_pool = (DEC_BATCH * n_pages * 5 + 3) // 4
    w_buf = min(WINDOW, PAST_LEN)
    kv_row = (2, KV_HEADS, HEAD_DIM)
    page_table = jax.random.permutation(next(keys), n_pool)[:DEC_BATCH * n_pages].reshape(DEC_BATCH, n_pages).astype(jnp.int32)
    u = jax.random.uniform(next(keys), (DEPTH, D_RNN), F32, 0.9, 0.999)
    a_base = u ** (1.0 / RG_C)
    rg_lambda = jnp.log(a_base) - jnp.log1p(-a_base)
    return {
        'x_prompt': nrm((BATCH, SEQ, D_MODEL), 1.0),
        'x_sample': nrm((DEC_BATCH, DEC_SEQ, D_MODEL), 1.0),
        'cache_cmp_kv': nrm((DEPTH, n_pool, PAGE_SIZE) + kv_row, 1.0),
        'cache_sel_kv': nrm((DEPTH, n_pool, PAGE_SIZE) + kv_row, 1.0),
        'cache_win_kv': nrm((DEPTH, DEC_BATCH, w_buf) + kv_row, 1.0),
        'state_conv': nrm((DEPTH, DEC_BATCH, CONV_W - 1, D_RNN), 1.0),
        'state_rg_h': nrm((DEPTH, DEC_BATCH, D_RNN), 0.5),
        'state_pool': nrm((DEPTH, DEC_BATCH, POOL_MAX - 1, D_POOL), 1.0),
        'page_table': page_table,
        'attn_norm': 1.0 + nrm((DEPTH, D_MODEL), 0.05),
        'w_in': nrm((DEPTH, D_MODEL, IN_COLS), D_MODEL ** -0.5),
        'conv_w': nrm((DEPTH, CONV_W, D_RNN), CONV_W ** -0.5),
        'conv_b': nrm((DEPTH, D_RNN), 0.02),
        'rg_w_a': nrm((DEPTH, RG_BLOCKS, RG_BW, RG_BW), RG_BW ** -0.5),
        'rg_b_a': nrm((DEPTH, D_RNN), 0.02),
        'rg_w_x': nrm((DEPTH, RG_BLOCKS, RG_BW, RG_BW), RG_BW ** -0.5),
        'rg_b_x': nrm((DEPTH, D_RNN), 0.02),
        'rg_lambda': rg_lambda,
        'q_norm': 1.0 + nrm((DEPTH, HEAD_DIM), 0.05),
        'k_norm': 1.0 + nrm((DEPTH, 3, HEAD_DIM), 0.05),
        'cmp_pe': nrm((DEPTH, L_CMP, 2, HEAD_DIM), 0.1),
        'w_cmp1': nrm((DEPTH, 2, L_CMP, HEAD_DIM, CMP_HIDDEN), (L_CMP * HEAD_DIM) ** -0.5),
        'w_cmp2': nrm((DEPTH, 2, CMP_HIDDEN, HEAD_DIM), CMP_HIDDEN ** -0.5),
        'rel_bias': nrm((REL_BUCKETS, N_HEADS), 0.5),
        'w_pool': nrm((DEPTH, POOL_GROUPS, POOL_GW, POOL_GW), POOL_GW ** -0.5),
        'pool_scale': 1.0 + nrm((DEPTH, D_POOL), 0.1),
        'w_br_rg': nrm((DEPTH, D_RNN, D_MODEL), D_RNN ** -0.5),
        'w_br_attn': nrm((DEPTH, N_HEADS * HEAD_DIM, D_MODEL), (N_HEADS * HEAD_DIM) ** -0.5),
        'w_br_pool': nrm((DEPTH, D_POOL, D_MODEL), D_POOL ** -0.5),
        'w_out': nrm((DEPTH, D_MODEL, D_MODEL), D_MODEL ** -0.5),
        'ffn_norm': 1.0 + nrm((DEPTH, D_MODEL), 0.05),
        'ffn_w_gate': nrm((N_DENSE, D_MODEL, D_FF), D_MODEL ** -0.5),
        'ffn_w_up': nrm((N_DENSE, D_MODEL, D_FF), D_MODEL ** -0.5),
        'ffn_w_down': nrm((N_DENSE, D_FF, D_MODEL), D_FF ** -0.5),
        'w_router': nrm((N_MOE, D_MODEL, N_EXPERTS), D_MODEL ** -0.5),
        'b_router': nrm((N_MOE, N_EXPERTS), 0.01),
        'moe_w_gate': nrm((N_MOE, N_EXPERTS, D_MODEL, D_FF_EXPERT), D_MODEL ** -0.5),
        'moe_w_up': nrm((N_MOE, N_EXPERTS, D_MODEL, D_FF_EXPERT), D_MODEL ** -0.5),
        'moe_w_down': nrm((N_MOE, N_EXPERTS, D_FF_EXPERT, D_MODEL), D_FF_EXPERT ** -0.5),
    }


def reference(x_prompt, x_sample, cache_cmp_kv, cache_sel_kv, cache_win_kv, state_conv, state_rg_h, state_pool,
              page_table, attn_norm, w_in, conv_w, conv_b, rg_w_a, rg_b_a, rg_w_x, rg_b_x, rg_lambda, q_norm, k_norm,
              cmp_pe, w_cmp1, w_cmp2, rel_bias, w_pool, pool_scale, w_br_rg, w_br_attn, w_br_pool, w_out, ffn_norm,
              ffn_w_gate, ffn_w_up, ffn_w_down, w_router, b_router, moe_w_gate, moe_w_up, moe_w_down):
    P = dict(attn_norm=attn_norm, w_in=w_in, conv_w=conv_w, conv_b=conv_b, rg_w_a=rg_w_a, rg_b_a=rg_b_a,
             rg_w_x=rg_w_x, rg_b_x=rg_b_x, rg_lambda=rg_lambda, q_norm=q_norm, k_norm=k_norm, w_pool=w_pool,
             pool_scale=pool_scale, w_br_rg=w_br_rg, w_br_attn=w_br_attn, w_br_pool=w_br_pool, w_out=w_out,
             ffn_norm=ffn_norm, ffn_w_gate=ffn_w_gate, ffn_w_up=ffn_w_up, ffn_w_down=ffn_w_down,
             w_router=w_router, b_router=b_router, moe_w_gate=moe_w_gate, moe_w_up=moe_w_up, moe_w_down=moe_w_down)
    B = x_prompt.shape[0]
    past_len = page_table.shape[1] * PAGE_SIZE
    zeros_conv = jnp.zeros((B, CONV_W - 1, D_RNN), x_prompt.dtype)
    zeros_h = jnp.zeros((B, D_RNN), F32)
    zeros_pool = jnp.zeros((B, POOL_MAX - 1, D_POOL), x_prompt.dtype)
    y_p, y_s = x_prompt, x_sample
    p_list, s_list = [], []
    for l in range(DEPTH):
        attend_p = functools.partial(nsa_prompt, w1=w_cmp1[l], w2=w_cmp2[l], pe=cmp_pe[l], kn=k_norm[l, 0],
                                     rel_bias=rel_bias)
        y_p, st_p = token_mixer(y_p, l, P, 0, zeros_conv, zeros_h, zeros_pool, attend_p)
        y_p = channel_mixer(y_p, l, P)
        p_list.append(st_p)
        attend_s = functools.partial(nsa_sample, cmp_pool=cache_cmp_kv[l], sel_pool=cache_sel_kv[l],
                                     win_buf=cache_win_kv[l], page_table=page_table, w1=w_cmp1[l], w2=w_cmp2[l],
                                     pe=cmp_pe[l], kn=k_norm[l, 0], rel_bias=rel_bias)
        y_s, st_s = token_mixer(y_s, l, P, past_len, state_conv[l], state_rg_h[l], state_pool[l], attend_s)
        y_s = channel_mixer(y_s, l, P)
        s_list.append(st_s)
    p_cmp_kv, p_sel_kv, p_win_kv, p_conv, p_h, p_pool = [jnp.stack(a) for a in zip(*p_list)]
    s_cmp_kv, s_sel_kv, s_win_kv, s_conv, s_h, s_pool = [jnp.stack(a) for a in zip(*s_list)]
    return (y_p, y_s, p_cmp_kv, p_sel_kv, p_win_kv, p_conv, p_h, p_pool,
            s_cmp_kv, s_sel_kv, s_win_kv, s_conv, s_h, s_pool)
```

```python
import math
import functools

import jax
import jax.numpy as jnp
import numpy as np
from jax import lax
from jax.experimental import pallas as pl
from jax.experimental.pallas import tpu as pltpu

D_MODEL = 1024
PAGE_SIZE = 128
F32 = jnp.float32
BF16 = jnp.bfloat16
EPS = 1e-6
NEG = -1e30
FORCE = 1e4
D_RNN = 512
RG_BLOCKS = 8
RG_BW = D_RNN // RG_BLOCKS
CONV_W = 4
RG_C = 8.0
N_HEADS = 8
KV_HEADS = 2
HPG = N_HEADS // KV_HEADS
HEAD_DIM = 64
L_CMP = 32
CMP_STRIDE = 16
SEL_BLOCK = 64
N_SELECT = 16
WINDOW = 512
Q_BLOCK = 128
D_POOL = 512
POOL_WINDOWS = (2, 4, 8, 16)
POOL_GROUPS = 4
POOL_GW = D_POOL // POOL_GROUPS
POOL_MAX = 16
REL_BUCKETS = 32
REL_MAX_DIST = 1024
N_EXPERTS = 8
TOP_K = 2
SPLITS = (D_RNN, D_RNN, N_HEADS * HEAD_DIM, 2 * KV_HEADS * HEAD_DIM, 2 * KV_HEADS * HEAD_DIM,
          2 * KV_HEADS * HEAD_DIM, 3 * N_HEADS, D_POOL, 3 * D_MODEL)

VMEM_LIMIT_BYTES = 48 * 1024 * 1024


def _mm_kernel(x_ref, w_ref, o_ref):
    o_ref[...] = jnp.dot(x_ref[...].astype(BF16), w_ref[...].astype(BF16), preferred_element_type=F32)


def _pick(n, cands):
    for c in cands:
        if n % c == 0:
            return c
    return n


def pmm(x, w):
    M, K = x.shape
    N = w.shape[1]
    n_pad = (-N) % 128
    if n_pad:
        w = jnp.pad(w, ((0, 0), (0, n_pad)))
    Np = N + n_pad
    tm = _pick(M, (512, 256, 128))
    tn = _pick(Np, (512, 384, 256, 128))
    out = pl.pallas_call(
        _mm_kernel,
        grid=(M // tm, Np // tn),
        in_specs=[pl.BlockSpec((tm, K), lambda i, j: (i, 0)),
                  pl.BlockSpec((K, tn), lambda i, j: (0, j))],
        out_specs=pl.BlockSpec((tm, tn), lambda i, j: (i, j)),
        out_shape=jax.ShapeDtypeStruct((M, Np), F32),
        compiler_params=pltpu.CompilerParams(dimension_semantics=("parallel", "parallel"),
                                             vmem_limit_bytes=VMEM_LIMIT_BYTES),
    )(x, w)
    return out[:, :N] if n_pad else out


def mm(x, w):
    lead = x.shape[:-1]
    return pmm(x.reshape(-1, x.shape[-1]), w).reshape(lead + (w.shape[1],))


def rms_norm(x, g):
    xf = x.astype(F32)
    y = xf * lax.rsqrt(jnp.mean(xf * xf, axis=-1, keepdims=True) + EPS)
    return (y * g.astype(F32)).astype(x.dtype)


def norm_keys(kv, g):
    return jnp.stack([rms_norm(kv[:, :, 0], g), kv[:, :, 1]], axis=2)


def rel_bucket(dist):
    n_exact = REL_BUCKETS // 2
    d = jnp.maximum(dist, 0)
    df = jnp.maximum(d, 1).astype(F32)
    large = n_exact + (jnp.log(df / n_exact) / math.log(REL_MAX_DIST / n_exact)
                       * (REL_BUCKETS - n_exact)).astype(jnp.int32)
    return jnp.where(d < n_exact, d, jnp.minimum(large, REL_BUCKETS - 1))


def masked_probs(logits, valid):
    logits = jnp.where(valid, logits, NEG)
    m = jnp.max(logits, axis=-1, keepdims=True)
    p = jnp.where(valid, jnp.exp(logits - m), 0.0)
    return p / jnp.maximum(jnp.sum(p, axis=-1, keepdims=True), 1e-30)


def causal_conv(x, buf, w, b):
    xp = jnp.concatenate([buf.astype(x.dtype), x], axis=1)
    y = lax.conv_general_dilated(xp, w[:, None, :].astype(x.dtype), (1,), 'VALID',
                                 dimension_numbers=('NWC', 'WIO', 'NWC'),
                                 feature_group_count=x.shape[-1]) + b
    return y, xp[:, -(CONV_W - 1):]


def rg_lru(xc, h0, w_a, b_a, w_x, b_x, lam):
    B, T, _ = xc.shape
    xb = xc.reshape(B, T, RG_BLOCKS, RG_BW)
    r = jax.nn.sigmoid((jnp.einsum('btnc,ncd->btnd', xb, w_a).reshape(B, T, D_RNN) + b_a).astype(F32))
    i = jax.nn.sigmoid((jnp.einsum('btnc,ncd->btnd', xb, w_x).reshape(B, T, D_RNN) + b_x).astype(F32))
    log_a = -RG_C * r * jax.nn.softplus(-lam.astype(F32))
    a = jnp.exp(log_a)
    u = jnp.sqrt(-jnp.expm1(2.0 * log_a)) * (i * xc.astype(F32))
    u = u.at[:, 0].add(a[:, 0] * h0.astype(F32))

    def combine(lhs, rhs):
        a1, b1 = lhs
        a2, b2 = rhs
        return a1 * a2, a2 * b1 + b2

    _, h = lax.associative_scan(combine, (a, u), axis=1)
    return h, h[:, -1]


def pool_mix(xin, buf, start_pos, w_pool, scale):
    B, T, C = xin.shape
    xf = jnp.concatenate([buf.astype(xin.dtype), xin], axis=1).astype(F32)
    cs = jnp.concatenate([jnp.zeros((B, 1, C), F32), jnp.cumsum(xf, axis=1)], axis=1)
    pos = start_pos + jnp.arange(T)
    means = []
    for g, w in enumerate(POOL_WINDOWS):
        sl = slice(g * POOL_GW, (g + 1) * POOL_GW)
        s = cs[:, POOL_MAX:POOL_MAX + T, sl] - cs[:, POOL_MAX - w:POOL_MAX - w + T, sl]
        cnt = jnp.minimum(pos + 1, w).astype(F32)[None, :, None]
        means.append(s / cnt)
    mixed = (jnp.concatenate(means, axis=-1) - xf[:, POOL_MAX - 1:]).astype(xin.dtype)
    y = jnp.einsum('btgc,gcd->btgd', mixed.reshape(B, T, POOL_GROUPS, POOL_GW), w_pool)
    return y.reshape(B, T, D_POOL) * scale, xf[:, -(POOL_MAX - 1):].astype(xin.dtype)


def compress_kv(rows, w1, w2, pe, kn):
    Bx, T = rows.shape[:2]
    n_ch = T // CMP_STRIDE
    ch = rows[:, :n_ch * CMP_STRIDE].reshape(Bx, n_ch, CMP_STRIDE, 2, KV_HEADS, HEAD_DIM)
    first = jnp.einsum('bnsegd,esdf->bnegf', ch, w1[:, :CMP_STRIDE])
    second = jnp.einsum('bnsegd,esdf->bnegf', ch, w1[:, CMP_STRIDE:])
    pe_term = jnp.einsum('led,eldf->ef', pe, w1)
    h = first[:, :-1] + second[:, 1:] + pe_term[:, None, :]
    comp = jnp.einsum('bnegf,efd->bnegd', jax.nn.gelu(h), w2)
    nc = comp.shape[1]
    comp_end = jnp.arange(nc) * CMP_STRIDE + L_CMP - 1
    return rms_norm(comp[:, :, 0], kn), comp[:, :, 1], comp_end


def overlap_matrix(n_cmp, n_sel):
    c0 = jnp.arange(n_cmp)[:, None] * CMP_STRIDE
    s0 = jnp.arange(n_sel)[None, :] * SEL_BLOCK
    return ((c0 <= s0 + SEL_BLOCK - 1) & (c0 + L_CMP - 1 >= s0)).astype(F32)


def nsa_attend(q, q_pos, comp_k, comp_v, comp_end, overlap, fetch_sel, win_k, win_v, win_pos, gates, rel_bias):
    Bq, Tq = q.shape[:2]
    scale = HEAD_DIM ** -0.5
    qg = q.reshape(Bq, Tq, KV_HEADS, HPG, HEAD_DIM)
    rb = rel_bias.astype(F32)

    def head_bias(buckets):
        return jnp.moveaxis(rb[buckets], -1, 1).reshape(Tq, KV_HEADS, HPG, -1)

    dist_c = q_pos[:, None] - comp_end[None, :]
    lc = jnp.einsum('btgjd,bcgd->btgjc', qg, comp_k, preferred_element_type=F32) * scale + head_bias(rel_bucket(dist_c))
    pc = masked_probs(lc, (dist_c >= 0)[None, :, None, None, :])
    o_cmp = jnp.einsum('btgjc,bcgd->btgjd', pc, comp_v)
    ns = overlap.shape[1]
    imp = jnp.einsum('btgjc,cn->btgn', pc, overlap)
    blk = jnp.arange(ns)
    cur = (q_pos // SEL_BLOCK)[None, :, None, None]
    forced = (blk == 0) | (blk == cur) | (blk == cur - 1)
    score = jnp.where(blk > cur, -1.0, jnp.where(forced, FORCE, imp))
    n_sel = min(N_SELECT, ns)
    _, idx = lax.top_k(score, n_sel)
    kv_s = fetch_sel(idx)
    k_s = kv_s[..., 0, :].reshape(Bq, Tq, KV_HEADS, n_sel * SEL_BLOCK, HEAD_DIM)
    v_s = kv_s[..., 1, :].reshape(Bq, Tq, KV_HEADS, n_sel * SEL_BLOCK, HEAD_DIM)
    pos_s = (idx[..., None] * SEL_BLOCK + jnp.arange(SEL_BLOCK)).reshape(Bq, Tq, KV_HEADS, -1)
    dist_s = q_pos[None, :, None, None] - pos_s
    bias_s = rb.reshape(REL_BUCKETS, KV_HEADS, HPG)[rel_bucket(dist_s), jnp.arange(KV_HEADS)[:, None]]
    ls = jnp.einsum('btgjd,btgkd->btgjk', qg, k_s, preferred_element_type=F32) * scale + jnp.moveaxis(bias_s, -1, 3)
    ps = masked_probs(ls, (dist_s >= 0)[:, :, :, None, :])
    o_sel = jnp.einsum('btgjk,btgkd->btgjd', ps, v_s)
    dist_w = q_pos[:, None] - win_pos[None, :]
    lw = jnp.einsum('btgjd,bwgd->btgjw', qg, win_k, preferred_element_type=F32) * scale + head_bias(rel_bucket(dist_w))
    valid_w = (dist_w >= 0) & (dist_w <= WINDOW) & (win_pos[None, :] >= 0)
    pw = masked_probs(lw, valid_w[None, :, None, None, :])
    o_win = jnp.einsum('btgjw,bwgd->btgjd', pw, win_v)
    g = jax.nn.sigmoid(gates.astype(F32)).reshape(Bq, Tq, KV_HEADS, HPG, 3)
    o = g[..., 0:1] * o_cmp + g[..., 1:2] * o_sel + g[..., 2:3] * o_win
    return o.reshape(Bq, Tq, N_HEADS * HEAD_DIM).astype(q.dtype)


QB = Q_BLOCK
ROWS = HPG * QB
LANES = 128
SUBLANES = 8
N_WIN_TILES = WINDOW // QB + 1
N_SEL_BIAS = REL_MAX_DIST // QB + 2
SEL_SPAN = 4
M_INIT = -3e38


def _nt(a, b):
    return lax.dot_general(a, b, (((1,), (1,)), ((), ())), preferred_element_type=F32)


def _nsa_prompt_kernel(q_ref, ck_ref, cv_ref, ks_ref, vs_ref, kw_ref, vw_ref, gate_ref, bc_ref, tbs_ref, tbw_ref,
                       ovl_ref, eye_ref, o_ref, acc_ref, m_ref, qa_ref, comb_ref, *, n_cmp, n_blk):
    qb = pl.program_id(2)
    t0 = qb * QB
    ncp = ck_ref.shape[2]
    q = q_ref[0].reshape(ROWS, LANES)

    lc = _nt(q, ck_ref[0, 0]) + bc_ref[0].reshape(ROWS, ncp)
    tok = t0 + lax.broadcasted_iota(jnp.int32, (HPG, QB, ncp), 1).reshape(ROWS, ncp)
    col = lax.broadcasted_iota(jnp.int32, (ROWS, ncp), 1)
    valid = (tok >= col * CMP_STRIDE + (L_CMP - 1)) & (col < n_cmp)
    lc = jnp.where(valid, lc, NEG)
    mx = jnp.max(lc, axis=1, keepdims=True)
    p = jnp.where(valid, jnp.exp(lc - mx), 0.0)
    pc = p / jnp.maximum(jnp.sum(p, axis=1, keepdims=True), 1e-30)
    gates = jax.nn.sigmoid(gate_ref[0].astype(F32)).reshape(ROWS, 3)
    comb_ref[...] = gates[:, 0:1] * jnp.dot(pc.astype(BF16), cv_ref[0, 0], preferred_element_type=F32)

    pcs = pc[0:QB] + pc[QB:2 * QB] + pc[2 * QB:3 * QB] + pc[3 * QB:4 * QB]
    hi = pcs.astype(BF16)
    lo = (pcs - hi.astype(F32)).astype(BF16)
    imp = _nt(ovl_ref[...], hi) + _nt(ovl_ref[...], lo)
    blk = lax.broadcasted_iota(jnp.int32, (n_blk, QB), 0)
    cur = (t0 + lax.broadcasted_iota(jnp.int32, (n_blk, QB), 1)) // SEL_BLOCK
    forced = (blk == 0) | (blk == cur) | (blk == cur - 1)
    score = jnp.where(blk > cur, -1.0, jnp.where(forced, FORCE, imp))
    chunks = [score[r:r + SUBLANES] for r in range(0, n_blk, SUBLANES)]
    sub = lax.broadcasted_iota(jnp.int32, (SUBLANES, QB), 0)
    cnts = [jnp.zeros((SUBLANES, QB), F32) for _ in chunks]
    for m in range(n_blk):
        row = jnp.broadcast_to(score[m:m + 1, :], (SUBLANES, QB))
        for r, ch in enumerate(chunks):
            first = r * SUBLANES
            if first > m:
                beats = jnp.where(row >= ch, 1.0, 0.0)
            elif first + SUBLANES - 1 < m:
                beats = jnp.where(row > ch, 1.0, 0.0)
            else:
                beats = jnp.where(sub + first > m, jnp.where(row >= ch, 1.0, 0.0), jnp.where(row > ch, 1.0, 0.0))
            cnts[r] = cnts[r] + beats
    cnt = jnp.concatenate(cnts, axis=0)
    sel_neg = jnp.where(cnt < float(min(N_SELECT, n_blk)), 0.0, NEG)
    pieces = [jnp.zeros((HEAD_DIM, QB), F32), sel_neg]
    if n_blk < HEAD_DIM:
        pieces.append(jnp.zeros((HEAD_DIM - n_blk, QB), F32))
    placed_t = jnp.concatenate(pieces, axis=0).astype(BF16)
    placed = _nt(eye_ref[...], placed_t)
    qa = q_ref[0].astype(F32) + placed[None]
    qa_ref[...] = qa.astype(BF16).reshape(ROWS, LANES)

    m_ref[...] = jnp.full((ROWS, LANES), M_INIT, F32)
    acc_ref[...] = jnp.zeros((ROWS, LANES), F32)
    n_bias = tbs_ref.shape[1]

    def body(kk, carry):
        off = pl.multiple_of(kk * (SEL_SPAN * QB), SEL_SPAN * QB)
        s = _nt(qa_ref[...], ks_ref[0, 0, pl.ds(off, SEL_SPAN * QB), :])
        parts = []
        for u in range(SEL_SPAN):
            idx = jnp.clip(qb - (kk * SEL_SPAN + u), -1, n_bias - 2) + 1
            parts.append(s[:, u * QB:(u + 1) * QB] + tbs_ref[:, pl.ds(idx, 1)].reshape(ROWS, LANES))
        tile_max = functools.reduce(jnp.maximum, parts)
        m_old = m_ref[...]
        m_new = jnp.maximum(m_old, jnp.max(tile_max, axis=1, keepdims=True))
        alpha = jnp.exp(m_old - m_new)
        pr = jnp.concatenate([jnp.exp(x - m_new).astype(BF16) for x in parts], axis=1)
        acc_ref[...] = alpha * acc_ref[...] + jnp.dot(pr, vs_ref[0, 0, pl.ds(off, SEL_SPAN * QB), :],
                                                      preferred_element_type=F32)
        m_ref[...] = m_new
        return carry

    lax.fori_loop(0, qb // SEL_SPAN + 1, body, 0)
    acc = acc_ref[...]
    comb_ref[...] += gates[:, 1:2] * (acc / pltpu.roll(acc, HEAD_DIM, axis=1))

    lane_q = lax.broadcasted_iota(jnp.int32, (HPG, QB, LANES), 2)
    qw = (q_ref[0].astype(F32) + jnp.where(lane_q == HEAD_DIM, NEG, 0.0)).astype(BF16).reshape(ROWS, LANES)
    w_off = pl.multiple_of(t0, QB)
    sw = _nt(qw, kw_ref[0, 0, pl.ds(w_off, WINDOW + QB), :]) + tbw_ref[...].reshape(ROWS, WINDOW + QB)
    pw = jnp.exp(sw - jnp.max(sw, axis=1, keepdims=True))
    accw = jnp.dot(pw.astype(BF16), vw_ref[0, 0, pl.ds(w_off, WINDOW + QB), :], preferred_element_type=F32)
    comb_ref[...] += gates[:, 2:3] * (accw / pltpu.roll(accw, HEAD_DIM, axis=1))

    comb = comb_ref[...]
    lane = lax.broadcasted_iota(jnp.int32, (QB, LANES), 1)
    for half in range(HPG // 2):
        a = comb[(2 * half) * QB:(2 * half + 1) * QB]
        b = comb[(2 * half + 1) * QB:(2 * half + 2) * QB]
        o_ref[0, :, half * LANES:(half + 1) * LANES] = jnp.where(lane < HEAD_DIM, a, pltpu.roll(b, HEAD_DIM, axis=1))


def rel_bias_tables(rel_bias, seq):
    max_d = (N_SEL_BIAS + 1) * QB
    tab = rel_bias.astype(F32)[rel_bucket(jnp.arange(max_d))]
    i = np.arange(QB)[:, None]
    j = np.arange(QB)[None, :]

    def tiles(ks, max_valid):
        d = np.asarray(ks)[:, None, None] * QB + (i - j)[None]
        ok = (d >= 0) if max_valid is None else ((d >= 0) & (d <= max_valid))
        t = tab[np.clip(d, 0, max_d - 1)]
        t = jnp.where(jnp.asarray(ok)[..., None], t, NEG)
        return jnp.transpose(t, (3, 0, 1, 2))

    tbs = tiles(range(-1, N_SEL_BIAS), None)
    tbw = tiles(range(N_WIN_TILES - 1, -1, -1), WINDOW)
    tbw = jnp.transpose(tbw, (0, 2, 1, 3)).reshape(N_HEADS, QB, N_WIN_TILES * QB)
    ncp = seq // CMP_STRIDE
    tq = np.arange(seq)[:, None]
    dc = tq - (np.arange(ncp)[None, :] * CMP_STRIDE + L_CMP - 1)
    bc = tab[np.clip(dc, 0, max_d - 1)]
    bc = jnp.transpose(bc.reshape(seq // QB, QB, ncp, N_HEADS), (0, 3, 1, 2))
    return tbs, tbw, bc


def nsa_prompt_pallas(q, comp_k, comp_v, kv_sel, kv_win, gates, tables):
    B, S = q.shape[:2]
    tbs, tbw, bc = tables
    nqb = S // QB
    ncp = S // CMP_STRIDE
    n_cmp = comp_k.shape[1]
    n_blk = S // SEL_BLOCK
    scale = HEAD_DIM ** -0.5
    qh = (q * scale).astype(BF16).reshape(B, S, N_HEADS, HEAD_DIM).transpose(0, 2, 1, 3)
    qp = jnp.concatenate([qh, jnp.zeros_like(qh)], axis=-1)

    def pad_cmp(c):
        c = jnp.pad(c.astype(BF16), ((0, 0), (0, ncp - n_cmp), (0, 0), (0, 0))).transpose(0, 2, 1, 3)
        return jnp.concatenate([c, jnp.zeros_like(c)], axis=-1)

    ck = pad_cmp(comp_k)
    cv = pad_cmp(comp_v)
    onehot = jnp.asarray((np.arange(S)[:, None] // SEL_BLOCK == np.arange(HEAD_DIM)[None, :]).astype(np.float32), BF16)
    onehot = jnp.broadcast_to(onehot, (B, KV_HEADS, S, HEAD_DIM))
    ones = jnp.ones((B, KV_HEADS, S, HEAD_DIM), BF16)
    zeros = jnp.zeros((B, KV_HEADS, S, HEAD_DIM), BF16)

    def kv_split(kv):
        kvt = kv.astype(BF16).transpose(0, 2, 3, 1, 4)
        return kvt[:, 0], kvt[:, 1]

    k_s, v_s = kv_split(kv_sel)
    k_w, v_w = kv_split(kv_win)
    ks = jnp.concatenate([k_s, onehot], axis=-1)
    vs = jnp.concatenate([v_s, ones], axis=-1)
    marker = jnp.asarray((np.arange(LANES) == HEAD_DIM).astype(np.float32), BF16)
    kw = jnp.concatenate([jnp.broadcast_to(marker, (B, KV_HEADS, WINDOW, LANES)),
                          jnp.concatenate([k_w, zeros], axis=-1)], axis=2)
    vw = jnp.pad(jnp.concatenate([v_w, ones], axis=-1), ((0, 0), (0, 0), (WINDOW, 0), (0, 0)))
    gt = gates.transpose(0, 2, 1, 3)
    c0 = np.arange(ncp)[None, :] * CMP_STRIDE
    s0 = np.arange(n_blk)[:, None] * SEL_BLOCK
    ovl_t = ((c0 <= s0 + SEL_BLOCK - 1) & (c0 + L_CMP - 1 >= s0) & (np.arange(ncp)[None, :] < n_cmp))
    ovl_t = jnp.asarray(ovl_t.astype(np.float32), BF16)
    eye = jnp.asarray(np.eye(QB, dtype=np.float32), BF16)

    kv_spec = pl.BlockSpec((1, 1, S, LANES), lambda b, g, i: (b, g, 0, 0))
    win_spec = pl.BlockSpec((1, 1, S + WINDOW, LANES), lambda b, g, i: (b, g, 0, 0))
    cmp_spec = pl.BlockSpec((1, 1, ncp, LANES), lambda b, g, i: (b, g, 0, 0))
    return pl.pallas_call(
        functools.partial(_nsa_prompt_kernel, n_cmp=n_cmp, n_blk=n_blk),
        grid=(B, KV_HEADS, nqb),
        in_specs=[
            pl.BlockSpec((1, HPG, QB, LANES), lambda b, g, i: (b, g, i, 0)),
            cmp_spec, cmp_spec, kv_spec, kv_spec, win_spec, win_spec,
            pl.BlockSpec((1, HPG, QB, 3), lambda b, g, i: (b, g, i, 0)),
            pl.BlockSpec((1, HPG, QB, ncp), lambda b, g, i: (i, g, 0, 0)),
            pl.BlockSpec((HPG, N_SEL_BIAS + 1, QB, QB), lambda b, g, i: (g, 0, 0, 0)),
            pl.BlockSpec((HPG, QB, N_WIN_TILES * QB), lambda b, g, i: (g, 0, 0)),
            pl.BlockSpec((n_blk, ncp), lambda b, g, i: (0, 0)),
            pl.BlockSpec((QB, QB), lambda b, g, i: (0, 0)),
        ],
        out_specs=pl.BlockSpec((1, QB, HPG * HEAD_DIM), lambda b, g, i: (b, i, g)),
        out_shape=jax.ShapeDtypeStruct((B, S, N_HEADS * HEAD_DIM), F32),
        scratch_shapes=[pltpu.VMEM((ROWS, LANES), F32), pltpu.VMEM((ROWS, LANES), F32),
                        pltpu.VMEM((ROWS, LANES), BF16), pltpu.VMEM((ROWS, LANES), F32)],
        compiler_params=pltpu.CompilerParams(dimension_semantics=("parallel", "parallel", "arbitrary"),
                                             vmem_limit_bytes=VMEM_LIMIT_BYTES),
        name="nsa_prompt",
    )(qp, ck, cv, ks, vs, kw, vw, gt, bc, tbs, tbw, ovl_t, eye)


def nsa_prompt(q, kv_cmp, kv_sel, kv_win, gates, *, w1, w2, pe, kn, tables):
    B, S = q.shape[:2]
    comp_k, comp_v, _ = compress_kv(kv_cmp, w1, w2, pe, kn)
    out = nsa_prompt_pallas(q.reshape(B, S, N_HEADS * HEAD_DIM), comp_k, comp_v, kv_sel, kv_win, gates, tables)
    return out, kv_win[:, -min(WINDOW, S):]


def nsa_sample(q, kv_cmp_new, kv_sel_new, kv_win_new, gates, *, cmp_pool, sel_pool, win_buf, page_table,
               w1, w2, pe, kn, rel_bias):
    DB, DS = q.shape[:2]
    n_pages = page_table.shape[1]
    past = n_pages * PAGE_SIZE
    q_pos = past + jnp.arange(DS)
    past_cmp = cmp_pool.reshape(cmp_pool.shape[0], -1)[page_table].reshape(DB, past, 2, KV_HEADS, HEAD_DIM)
    rows = jnp.concatenate([past_cmp, kv_cmp_new.astype(past_cmp.dtype)], axis=1)
    comp_k, comp_v, comp_end = compress_kv(rows, w1, w2, pe, kn)
    total = past + DS
    ns = -(-total // SEL_BLOCK)
    nbp = past // SEL_BLOCK
    nnb = ns - nbp
    bpp = PAGE_SIZE // SEL_BLOCK
    ovl = overlap_matrix(comp_k.shape[1], ns)
    pool_blocks = sel_pool.reshape(-1, SEL_BLOCK, 2, KV_HEADS, HEAD_DIM)
    new_pad = jnp.pad(kv_sel_new.astype(sel_pool.dtype), ((0, 0), (0, nnb * SEL_BLOCK - DS), (0, 0), (0, 0), (0, 0)))
    new_blocks = new_pad.reshape(DB, nnb, SEL_BLOCK, 2, KV_HEADS, HEAD_DIM)
    b_idx = jnp.arange(DB)[:, None, None, None]
    g_idx = jnp.arange(KV_HEADS)[None, None, :, None]

    def fetch(idx):
        ip = jnp.minimum(idx, nbp - 1)
        phys = page_table[b_idx, ip // bpp] * bpp + ip % bpp
        from_past = pool_blocks[phys, :, :, g_idx]
        from_new = new_blocks[b_idx, jnp.clip(idx - nbp, 0, nnb - 1), :, :, g_idx]
        return jnp.where((idx >= nbp)[..., None, None, None], from_new, from_past)

    wb = win_buf.shape[1]
    win = jnp.concatenate([win_buf, kv_win_new.astype(win_buf.dtype)], axis=1)
    win_pos = past - wb + jnp.arange(wb + DS)
    out = nsa_attend(q, q_pos, comp_k, comp_v, comp_end, ovl, fetch, win[:, :, 0], win[:, :, 1], win_pos, gates, rel_bias)
    return out, win[:, -wb:]


def token_mixer(x, l, P, start_pos, conv_buf, h0, pool_buf, attend):
    B, T, _ = x.shape
    xn = rms_norm(x, P['attn_norm'][l])
    z = mm(xn, P['w_in'][l])
    cut = np.cumsum(SPLITS)[:-1].tolist()
    x_rg, g_rg, q, kv_c, kv_s, kv_w, g_nsa, x_pool, g_br = jnp.split(z, cut, axis=-1)
    xc, conv_new = causal_conv(x_rg, conv_buf, P['conv_w'][l], P['conv_b'][l])
    h, h_last = rg_lru(xc, h0, P['rg_w_a'][l], P['rg_b_a'][l], P['rg_w_x'][l], P['rg_b_x'][l], P['rg_lambda'][l])
    y_a = (h * jax.nn.gelu(g_rg.astype(F32))).astype(x.dtype)
    q = rms_norm(q.reshape(B, T, N_HEADS, HEAD_DIM), P['q_norm'][l])
    kv_shape = (B, T, 2, KV_HEADS, HEAD_DIM)
    kv_c = kv_c.reshape(kv_shape)
    kv_s = norm_keys(kv_s.reshape(kv_shape), P['k_norm'][l, 1])
    kv_w = norm_keys(kv_w.reshape(kv_shape), P['k_norm'][l, 2])
    y_b, win_new = attend(q, kv_c, kv_s, kv_w, g_nsa.reshape(B, T, N_HEADS, 3))
    y_c, pool_new = pool_mix(x_pool, pool_buf, start_pos, P['w_pool'][l], P['pool_scale'][l])
    g = jax.nn.sigmoid(g_br.astype(F32)).reshape(B, T, 3, D_MODEL)
    merged = (g[:, :, 0] * mm(y_a, P['w_br_rg'][l]) + g[:, :, 1] * mm(y_b, P['w_br_attn'][l])
              + g[:, :, 2] * mm(y_c, P['w_br_pool'][l]))
    out = mm(merged.astype(x.dtype), P['w_out'][l])
    return x + out, (kv_c, kv_s, win_new, conv_new, h_last, pool_new)


def swiglu(x, wg, wu, wd):
    return mm(jax.nn.silu(mm(x, wg)) * mm(x, wu), wd)


def channel_mixer(x, l, P):
    xn = rms_norm(x, P['ffn_norm'][l])
    i = l // 2
    if l % 2 == 0:
        y = swiglu(xn, P['ffn_w_gate'][i], P['ffn_w_up'][i], P['ffn_w_down'][i]).astype(F32)
    else:
        logits = (xn @ P['w_router'][i]).astype(F32) + P['b_router'][i].astype(F32)
        vals, idx = lax.top_k(logits, TOP_K)
        wts = jax.nn.softmax(vals, axis=-1)
        gate = jnp.sum(jax.nn.one_hot(idx, N_EXPERTS, dtype=F32) * wts[..., None], axis=-2)
        y = jnp.zeros(x.shape, F32)
        for e in range(N_EXPERTS):
            y = y + gate[..., e:e + 1] * swiglu(xn, P['moe_w_gate'][i, e], P['moe_w_up'][i, e], P['moe_w_down'][i, e])
    return x + y.astype(x.dtype)


def kernel(x_prompt, x_sample, cache_cmp_kv, cache_sel_kv, cache_win_kv, state_conv, state_rg_h, state_pool,
           page_table, attn_norm, w_in, conv_w, conv_b, rg_w_a, rg_b_a, rg_w_x, rg_b_x, rg_lambda, q_norm, k_norm,
           cmp_pe, w_cmp1, w_cmp2, rel_bias, w_pool, pool_scale, w_br_rg, w_br_attn, w_br_pool, w_out, ffn_norm,
           ffn_w_gate, ffn_w_up, ffn_w_down, w_router, b_router, moe_w_gate, moe_w_up, moe_w_down):
    P = dict(attn_norm=attn_norm, w_in=w_in, conv_w=conv_w, conv_b=conv_b, rg_w_a=rg_w_a, rg_b_a=rg_b_a,
             rg_w_x=rg_w_x, rg_b_x=rg_b_x, rg_lambda=rg_lambda, q_norm=q_norm, k_norm=k_norm, w_pool=w_pool,
             pool_scale=pool_scale, w_br_rg=w_br_rg, w_br_attn=w_br_attn, w_br_pool=w_br_pool, w_out=w_out,
             ffn_norm=ffn_norm, ffn_w_gate=ffn_w_gate, ffn_w_up=ffn_w_up, ffn_w_down=ffn_w_down,
             w_router=w_router, b_router=b_router, moe_w_gate=moe_w_gate, moe_w_up=moe_w_up, moe_w_down=moe_w_down)
    depth = w_in.shape[0]
    B = x_prompt.shape[0]
    past_len = page_table.shape[1] * PAGE_SIZE
    zeros_conv = jnp.zeros((B, CONV_W - 1, D_RNN), x_prompt.dtype)
    zeros_h = jnp.zeros((B, D_RNN), F32)
    zeros_pool = jnp.zeros((B, POOL_MAX - 1, D_POOL), x_prompt.dtype)
    y_p, y_s = x_prompt, x_sample
    tables = rel_bias_tables(rel_bias, x_prompt.shape[1])
    p_list, s_list = [], []
    for l in range(depth):
        attend_p = functools.partial(nsa_prompt, w1=w_cmp1[l], w2=w_cmp2[l], pe=cmp_pe[l], kn=k_norm[l, 0],
                                     tables=tables)
        y_p, st_p = token_mixer(y_p, l, P, 0, zeros_conv, zeros_h, zeros_pool, attend_p)
        y_p = channel_mixer(y_p, l, P)
        p_list.append(st_p)
        attend_s = functools.partial(nsa_sample, cmp_pool=cache_cmp_kv[l], sel_pool=cache_sel_kv[l],
                                     win_buf=cache_win_kv[l], page_table=page_table, w1=w_cmp1[l], w2=w_cmp2[l],
                                     pe=cmp_pe[l], kn=k_norm[l, 0], rel_bias=rel_bias)
        y_s, st_s = token_mixer(y_s, l, P, past_len, state_conv[l], state_rg_h[l], state_pool[l], attend_s)
        y_s = channel_mixer(y_s, l, P)
        s_list.append(st_s)
    p_cmp_kv, p_sel_kv, p_win_kv, p_conv, p_h, p_pool = [jnp.stack(a) for a in zip(*p_list)]
    s_cmp_kv, s_sel_kv, s_win_kv, s_conv, s_h, s_pool = [jnp.stack(a) for a in zip(*s_list)]
    return (y_p, y_s, p_cmp_kv, p_sel_kv, p_win_kv, p_conv, p_h, p_pool,
            s_cmp_kv, s_sel_kv, s_win_kv, s_conv, s_h, s_pool)
```

```python
import math
import functools

import jax
import jax.numpy as jnp
import numpy as np
from jax import lax
from jax.experimental import pallas as pl
from jax.experimental.pallas import tpu as pltpu

D_MODEL = 1024
PAGE_SIZE = 128
F32 = jnp.float32
BF16 = jnp.bfloat16
EPS = 1e-6
NEG = -1e30
FORCE = 1e4
D_RNN = 512
RG_BLOCKS = 8
RG_BW = D_RNN // RG_BLOCKS
CONV_W = 4
RG_C = 8.0
N_HEADS = 8
KV_HEADS = 2
HPG = N_HEADS // KV_HEADS
HEAD_DIM = 64
L_CMP = 32
CMP_STRIDE = 16
SEL_BLOCK = 64
N_SELECT = 16
WINDOW = 512
Q_BLOCK = 128
D_POOL = 512
POOL_WINDOWS = (2, 4, 8, 16)
POOL_GROUPS = 4
POOL_GW = D_POOL // POOL_GROUPS
POOL_MAX = 16
REL_BUCKETS = 32
REL_MAX_DIST = 1024
N_EXPERTS = 8
TOP_K = 2
KV_ROW = 2 * KV_HEADS * HEAD_DIM
SPLITS = (D_RNN, D_RNN, N_HEADS * HEAD_DIM, KV_ROW, KV_ROW, KV_ROW, 3 * N_HEADS, D_POOL, 3 * D_MODEL)

VMEM_LIMIT_BYTES = 52 * 1024 * 1024
LANES = 128
SUBLANES = 8
M_INIT = -3e38


def _pick(n, cands):
    for c in cands:
        if n % c == 0:
            return c
    return n


def _nt(a, b):
    return lax.dot_general(a, b, (((1,), (1,)), ((), ())), preferred_element_type=F32)


def _rms_rows(x, g):
    return x * lax.rsqrt(jnp.mean(x * x, axis=-1, keepdims=True) + EPS) * g


def rms_norm(x, g):
    xf = x.astype(F32)
    y = xf * lax.rsqrt(jnp.mean(xf * xf, axis=-1, keepdims=True) + EPS)
    return (y * g.astype(F32)).astype(x.dtype)


def rel_bucket(dist):
    n_exact = REL_BUCKETS // 2
    d = jnp.maximum(dist, 0)
    df = jnp.maximum(d, 1).astype(F32)
    large = n_exact + (jnp.log(df / n_exact) / math.log(REL_MAX_DIST / n_exact)
                       * (REL_BUCKETS - n_exact)).astype(jnp.int32)
    return jnp.where(d < n_exact, d, jnp.minimum(large, REL_BUCKETS - 1))


def masked_probs(logits, valid):
    logits = jnp.where(valid, logits, NEG)
    m = jnp.max(logits, axis=-1, keepdims=True)
    p = jnp.where(valid, jnp.exp(logits - m), 0.0)
    return p / jnp.maximum(jnp.sum(p, axis=-1, keepdims=True), 1e-30)


def causal_conv(x, buf, w, b):
    xp = jnp.concatenate([buf.astype(x.dtype), x], axis=1)
    y = lax.conv_general_dilated(xp, w[:, None, :].astype(x.dtype), (1,), 'VALID',
                                 dimension_numbers=('NWC', 'WIO', 'NWC'),
                                 feature_group_count=x.shape[-1]) + b
    return y, xp[:, -(CONV_W - 1):]


def rg_lru(xc, h0, w_a, b_a, w_x, b_x, lam):
    B, T, _ = xc.shape
    xb = xc.reshape(B, T, RG_BLOCKS, RG_BW)
    r = jax.nn.sigmoid((jnp.einsum('btnc,ncd->btnd', xb, w_a).reshape(B, T, D_RNN) + b_a).astype(F32))
    i = jax.nn.sigmoid((jnp.einsum('btnc,ncd->btnd', xb, w_x).reshape(B, T, D_RNN) + b_x).astype(F32))
    log_a = -RG_C * r * jax.nn.softplus(-lam.astype(F32))
    a = jnp.exp(log_a)
    u = jnp.sqrt(-jnp.expm1(2.0 * log_a)) * (i * xc.astype(F32))
    u = u.at[:, 0].add(a[:, 0] * h0.astype(F32))

    def combine(lhs, rhs):
        a1, b1 = lhs
        a2, b2 = rhs
        return a1 * a2, a2 * b1 + b2

    _, h = lax.associative_scan(combine, (a, u), axis=1)
    return h, h[:, -1]


def pool_mix(xin, buf, start_pos, w_pool, scale):
    B, T, C = xin.shape
    xf = jnp.concatenate([buf.astype(xin.dtype), xin], axis=1).astype(F32)
    cs = jnp.concatenate([jnp.zeros((B, 1, C), F32), jnp.cumsum(xf, axis=1)], axis=1)
    pos = start_pos + jnp.arange(T)
    means = []
    for g, w in enumerate(POOL_WINDOWS):
        sl = slice(g * POOL_GW, (g + 1) * POOL_GW)
        s = cs[:, POOL_MAX:POOL_MAX + T, sl] - cs[:, POOL_MAX - w:POOL_MAX - w + T, sl]
        cnt = jnp.minimum(pos + 1, w).astype(F32)[None, :, None]
        means.append(s / cnt)
    mixed = (jnp.concatenate(means, axis=-1) - xf[:, POOL_MAX - 1:]).astype(xin.dtype)
    y = jnp.einsum('btgc,gcd->btgd', mixed.reshape(B, T, POOL_GROUPS, POOL_GW), w_pool)
    return y.reshape(B, T, D_POOL) * scale, xf[:, -(POOL_MAX - 1):].astype(xin.dtype)


def compress_kv(rows, w1, w2, pe, kn):
    Bx, T = rows.shape[:2]
    n_ch = T // CMP_STRIDE
    ch = rows[:, :n_ch * CMP_STRIDE].reshape(Bx, n_ch, CMP_STRIDE, 2, KV_HEADS, HEAD_DIM)
    first = jnp.einsum('bnsegd,esdf->bnegf', ch, w1[:, :CMP_STRIDE])
    second = jnp.einsum('bnsegd,esdf->bnegf', ch, w1[:, CMP_STRIDE:])
    pe_term = jnp.einsum('led,eldf->ef', pe, w1)
    h = first[:, :-1] + second[:, 1:] + pe_term[:, None, :]
    comp = jnp.einsum('bnegf,efd->bnegd', jax.nn.gelu(h), w2)
    nc = comp.shape[1]
    comp_end = jnp.arange(nc) * CMP_STRIDE + L_CMP - 1
    return rms_norm(comp[:, :, 0], kn), comp[:, :, 1], comp_end


def overlap_matrix(n_cmp, n_sel):
    c0 = jnp.arange(n_cmp)[:, None] * CMP_STRIDE
    s0 = jnp.arange(n_sel)[None, :] * SEL_BLOCK
    return ((c0 <= s0 + SEL_BLOCK - 1) & (c0 + L_CMP - 1 >= s0)).astype(F32)


def nsa_attend(q, q_pos, comp_k, comp_v, comp_end, overlap, fetch_sel, win_k, win_v, win_pos, gates, rel_bias):
    Bq, Tq = q.shape[:2]
    scale = HEAD_DIM ** -0.5
    qg = q.reshape(Bq, Tq, KV_HEADS, HPG, HEAD_DIM)
    rb = rel_bias.astype(F32)

    def head_bias(buckets):
        return jnp.moveaxis(rb[buckets], -1, 1).reshape(Tq, KV_HEADS, HPG, -1)

    dist_c = q_pos[:, None] - comp_end[None, :]
    lc = jnp.einsum('btgjd,bcgd->btgjc', qg, comp_k, preferred_element_type=F32) * scale + head_bias(rel_bucket(dist_c))
    pc = masked_probs(lc, (dist_c >= 0)[None, :, None, None, :])
    o_cmp = jnp.einsum('btgjc,bcgd->btgjd', pc, comp_v)
    ns = overlap.shape[1]
    imp = jnp.einsum('btgjc,cn->btgn', pc, overlap)
    blk = jnp.arange(ns)
    cur = (q_pos // SEL_BLOCK)[None, :, None, None]
    forced = (blk == 0) | (blk == cur) | (blk == cur - 1)
    score = jnp.where(blk > cur, -1.0, jnp.where(forced, FORCE, imp))
    n_sel = min(N_SELECT, ns)
    _, idx = lax.top_k(score, n_sel)
    kv_s = fetch_sel(idx)
    k_s = kv_s[..., 0, :].reshape(Bq, Tq, KV_HEADS, n_sel * SEL_BLOCK, HEAD_DIM)
    v_s = kv_s[..., 1, :].reshape(Bq, Tq, KV_HEADS, n_sel * SEL_BLOCK, HEAD_DIM)
    pos_s = (idx[..., None] * SEL_BLOCK + jnp.arange(SEL_BLOCK)).reshape(Bq, Tq, KV_HEADS, -1)
    dist_s = q_pos[None, :, None, None] - pos_s
    bias_s = rb.reshape(REL_BUCKETS, KV_HEADS, HPG)[rel_bucket(dist_s), jnp.arange(KV_HEADS)[:, None]]
    ls = jnp.einsum('btgjd,btgkd->btgjk', qg, k_s, preferred_element_type=F32) * scale + jnp.moveaxis(bias_s, -1, 3)
    ps = masked_probs(ls, (dist_s >= 0)[:, :, :, None, :])
    o_sel = jnp.einsum('btgjk,btgkd->btgjd', ps, v_s)
    dist_w = q_pos[:, None] - win_pos[None, :]
    lw = jnp.einsum('btgjd,bwgd->btgjw', qg, win_k, preferred_element_type=F32) * scale + head_bias(rel_bucket(dist_w))
    valid_w = (dist_w >= 0) & (dist_w <= WINDOW) & (win_pos[None, :] >= 0)
    pw = masked_probs(lw, valid_w[None, :, None, None, :])
    o_win = jnp.einsum('btgjw,bwgd->btgjd', pw, win_v)
    g = jax.nn.sigmoid(gates.astype(F32)).reshape(Bq, Tq, KV_HEADS, HPG, 3)
    o = g[..., 0:1] * o_cmp + g[..., 1:2] * o_sel + g[..., 2:3] * o_win
    return o.reshape(Bq, Tq, N_HEADS * HEAD_DIM).astype(q.dtype)


def nsa_sample(q, kv_cmp_new, kv_sel_new, kv_win_new, gates, *, cmp_pool, sel_pool, win_buf, page_table,
               w1, w2, pe, kn, rel_bias):
    DB, DS = q.shape[:2]
    n_pages = page_table.shape[1]
    past = n_pages * PAGE_SIZE
    q_pos = past + jnp.arange(DS)
    past_cmp = cmp_pool.reshape(cmp_pool.shape[0], -1)[page_table].reshape(DB, past, 2, KV_HEADS, HEAD_DIM)
    rows = jnp.concatenate([past_cmp, kv_cmp_new.astype(past_cmp.dtype)], axis=1)
    comp_k, comp_v, comp_end = compress_kv(rows, w1, w2, pe, kn)
    total = past + DS
    ns = -(-total // SEL_BLOCK)
    nbp = past // SEL_BLOCK
    nnb = ns - nbp
    bpp = PAGE_SIZE // SEL_BLOCK
    ovl = overlap_matrix(comp_k.shape[1], ns)
    pool_blocks = sel_pool.reshape(-1, SEL_BLOCK, 2, KV_HEADS, HEAD_DIM)
    new_pad = jnp.pad(kv_sel_new.astype(sel_pool.dtype), ((0, 0), (0, nnb * SEL_BLOCK - DS), (0, 0), (0, 0), (0, 0)))
    new_blocks = new_pad.reshape(DB, nnb, SEL_BLOCK, 2, KV_HEADS, HEAD_DIM)
    b_idx = jnp.arange(DB)[:, None, None, None]
    g_idx = jnp.arange(KV_HEADS)[None, None, :, None]

    def fetch(idx):
        ip = jnp.minimum(idx, nbp - 1)
        phys = page_table[b_idx, ip // bpp] * bpp + ip % bpp
        from_past = pool_blocks[phys, :, :, g_idx]
        from_new = new_blocks[b_idx, jnp.clip(idx - nbp, 0, nnb - 1), :, :, g_idx]
        return jnp.where((idx >= nbp)[..., None, None, None], from_new, from_past)

    wb = win_buf.shape[1]
    win = jnp.concatenate([win_buf, kv_win_new.astype(win_buf.dtype)], axis=1)
    win_pos = past - wb + jnp.arange(wb + DS)
    out = nsa_attend(q, q_pos, comp_k, comp_v, comp_end, ovl, fetch, win[:, :, 0], win[:, :, 1], win_pos, gates, rel_bias)
    return out, win[:, -wb:]


PROJ_TN = 512
ZA_COLS = 3 * D_MODEL + 2 * D_RNN + D_POOL
N_ZA = ZA_COLS // PROJ_TN
Q_COLS = N_HEADS * HEAD_DIM
GN_COLS = 2 * LANES
PROJ_COLS = ZA_COLS + Q_COLS + 3 * KV_ROW + GN_COLS
ZA_XRG, ZA_GRG, ZA_XPOOL = 3 * D_MODEL, 3 * D_MODEL + D_RNN, 3 * D_MODEL + 2 * D_RNN


def _proj_kernel(x_ref, g_ref, w_ref, seg_ref, ng_ref, nm_ref, za_ref, q_ref, kvc_ref, kvs_ref, kvw_ref, gn_ref, xn_ref):
    j = pl.program_id(1)

    @pl.when(j == 0)
    def _():
        xn_ref[...] = _rms_rows(x_ref[...], g_ref[...]).astype(BF16)

    acc = jnp.dot(xn_ref[...], w_ref[...], preferred_element_type=F32)

    @pl.when(j < N_ZA)
    def _():
        za_ref[...] = acc

    @pl.when(j >= N_ZA)
    def _():
        sq = acc * acc
        hi = sq.astype(BF16)
        lo = (sq - hi.astype(F32)).astype(BF16)
        ss = (jnp.dot(hi, seg_ref[...], preferred_element_type=F32)
              + jnp.dot(lo, seg_ref[...], preferred_element_type=F32))
        normed = acc * lax.rsqrt(ss * (1.0 / HEAD_DIM) + EPS) * ng_ref[...]
        y = jnp.where(nm_ref[...] > 0.5, normed, acc)

        @pl.when(j == N_ZA)
        def _():
            q_ref[...] = y.astype(BF16)

        @pl.when(j == N_ZA + 1)
        def _():
            kvc_ref[...] = y[:, :KV_ROW]
            kvs_ref[...] = y[:, KV_ROW:]

        @pl.when(j == N_ZA + 2)
        def _():
            kvw_ref[...] = y[:, :KV_ROW]
            gn_ref[...] = y[:, KV_ROW:]


def pack_projection(w_in, q_gain, ks_gain, kw_gain):
    cut = np.cumsum(SPLITS)[:-1].tolist()
    x_rg, g_rg, q, kv_c, kv_s, kv_w, g_nsa, x_pool, g_br = jnp.split(w_in, cut, axis=1)
    per_group = 3 * HPG
    gn = jnp.zeros((w_in.shape[0], GN_COLS), w_in.dtype)
    for g in range(KV_HEADS):
        gn = gn.at[:, g * LANES:g * LANES + per_group].set(g_nsa[:, g * per_group:(g + 1) * per_group])
    w = jnp.concatenate([g_br, x_rg, g_rg, x_pool, q, kv_c, kv_s, kv_w, gn], axis=1).astype(BF16)
    ones_v = jnp.ones((KV_HEADS * HEAD_DIM,), F32)
    zeros_v = jnp.zeros((KV_HEADS * HEAD_DIM,), F32)
    gain = jnp.concatenate([jnp.ones((ZA_COLS,), F32), jnp.tile(q_gain.astype(F32), N_HEADS) * HEAD_DIM ** -0.5,
                            jnp.ones((KV_ROW,), F32),
                            jnp.tile(ks_gain.astype(F32), KV_HEADS), ones_v,
                            jnp.tile(kw_gain.astype(F32), KV_HEADS), ones_v,
                            jnp.ones((GN_COLS,), F32)])
    mask = jnp.concatenate([jnp.zeros((ZA_COLS,), F32), jnp.ones((Q_COLS,), F32), jnp.zeros((KV_ROW,), F32),
                            ones_v, zeros_v, ones_v, zeros_v, jnp.zeros((GN_COLS,), F32)])
    return w, gain[None], mask[None]


def projection(x, norm_gain, packed):
    w, gain, mask = packed
    T = x.shape[0]
    tm = _pick(T, (1024, 512, 256, 128))
    seg = jnp.asarray((np.arange(PROJ_TN)[:, None] // HEAD_DIM == np.arange(PROJ_TN)[None, :] // HEAD_DIM)
                      .astype(np.float32), BF16)
    row = lambda i, j: (i, 0)
    return pl.pallas_call(
        _proj_kernel,
        grid=(T // tm, PROJ_COLS // PROJ_TN),
        in_specs=[pl.BlockSpec((tm, D_MODEL), row),
                  pl.BlockSpec((1, D_MODEL), lambda i, j: (0, 0)),
                  pl.BlockSpec((D_MODEL, PROJ_TN), lambda i, j: (0, j)),
                  pl.BlockSpec((PROJ_TN, PROJ_TN), lambda i, j: (0, 0)),
                  pl.BlockSpec((1, PROJ_TN), lambda i, j: (0, j)),
                  pl.BlockSpec((1, PROJ_TN), lambda i, j: (0, j))],
        out_specs=[pl.BlockSpec((tm, PROJ_TN), lambda i, j: (i, jnp.minimum(j, N_ZA - 1))),
                   pl.BlockSpec((tm, Q_COLS), row),
                   pl.BlockSpec((tm, KV_ROW), row), pl.BlockSpec((tm, KV_ROW), row), pl.BlockSpec((tm, KV_ROW), row),
                   pl.BlockSpec((tm, GN_COLS), row)],
        out_shape=[jax.ShapeDtypeStruct((T, ZA_COLS), F32), jax.ShapeDtypeStruct((T, Q_COLS), BF16),
                   jax.ShapeDtypeStruct((T, KV_ROW), F32), jax.ShapeDtypeStruct((T, KV_ROW), F32),
                   jax.ShapeDtypeStruct((T, KV_ROW), F32), jax.ShapeDtypeStruct((T, GN_COLS), F32)],
        scratch_shapes=[pltpu.VMEM((tm, D_MODEL), BF16)],
        compiler_params=pltpu.CompilerParams(dimension_semantics=("parallel", "arbitrary"),
                                             vmem_limit_bytes=VMEM_LIMIT_BYTES),
        name="projection",
    )(x, norm_gain.astype(F32)[None], w, seg, gain, mask)


QB = Q_BLOCK
ROWS = HPG * QB
N_WIN_TILES = WINDOW // QB + 1
N_SEL_BIAS = REL_MAX_DIST // QB + 2
SEL_SPAN = 4
KV_CHUNK = 512


def _nsa_prompt_kernel(q_ref, ck_ref, cv_ref, kvs_ref, kvw_ref, gate_ref, bc_ref, tbs_ref, tbw_ref,
                       ovl_ref, eye_ref, pq_ref, pk_ref, pv_ref, onehot_ref, o_ref,
                       acc_ref, m_ref, qa_ref, comb_ref, ks_ref, vs_ref, kw_ref, vw_ref, *, n_cmp, n_blk):
    qb = pl.program_id(2)
    t0 = qb * QB
    ncp = ck_ref.shape[2]
    seq = kvs_ref.shape[1]
    lane_row = lax.broadcasted_iota(jnp.int32, (1, LANES), 1)

    @pl.when(qb == 0)
    def _():
        ones_hi = jnp.where(lane_row >= HEAD_DIM, 1.0, 0.0)
        kw_ref[0:WINDOW, :] = jnp.broadcast_to(jnp.where(lane_row == HEAD_DIM, 1.0, 0.0), (WINDOW, LANES)).astype(BF16)
        vw_ref[0:WINDOW, :] = jnp.zeros((WINDOW, LANES), BF16)

        def stage(c, carry):
            r = pl.multiple_of(c * KV_CHUNK, KV_CHUNK)
            sel = kvs_ref[0, pl.ds(r, KV_CHUNK), :].astype(BF16)
            win = kvw_ref[0, pl.ds(r, KV_CHUNK), :].astype(BF16)
            ks_ref[pl.ds(r, KV_CHUNK), :] = (jnp.dot(sel, pk_ref[0], preferred_element_type=F32)
                                             + onehot_ref[pl.ds(r, KV_CHUNK), :].astype(F32)).astype(BF16)
            vs_ref[pl.ds(r, KV_CHUNK), :] = (jnp.dot(sel, pv_ref[0], preferred_element_type=F32) + ones_hi).astype(BF16)
            kw_ref[pl.ds(WINDOW + r, KV_CHUNK), :] = jnp.dot(win, pk_ref[0], preferred_element_type=F32).astype(BF16)
            vw_ref[pl.ds(WINDOW + r, KV_CHUNK), :] = (jnp.dot(win, pv_ref[0], preferred_element_type=F32)
                                                      + ones_hi).astype(BF16)
            return carry

        lax.fori_loop(0, seq // KV_CHUNK, stage, 0)

    q4 = jnp.dot(q_ref[0], pq_ref[...], preferred_element_type=F32)
    q3 = jnp.concatenate([q4[:, j * LANES:(j + 1) * LANES] for j in range(HPG)], axis=0)
    q = q3.astype(BF16)
    sig = jax.nn.sigmoid(gate_ref[0])
    gates = [jnp.concatenate([sig[:, 3 * j + c:3 * j + c + 1] for j in range(HPG)], axis=0) for c in range(3)]

    lc = _nt(q, ck_ref[0, 0]) + bc_ref[0].reshape(ROWS, ncp)
    tok = t0 + lax.broadcasted_iota(jnp.int32, (HPG, QB, ncp), 1).reshape(ROWS, ncp)
    col = lax.broadcasted_iota(jnp.int32, (ROWS, ncp), 1)
    valid = (tok >= col * CMP_STRIDE + (L_CMP - 1)) & (col < n_cmp)
    lc = jnp.where(valid, lc, NEG)
    mx = jnp.max(lc, axis=1, keepdims=True)
    p = jnp.where(valid, jnp.exp(lc - mx), 0.0)
    pc = p / jnp.maximum(jnp.sum(p, axis=1, keepdims=True), 1e-30)
    comb_ref[...] = gates[0] * jnp.dot(pc.astype(BF16), cv_ref[0, 0], preferred_element_type=F32)

    pcs = pc[0:QB] + pc[QB:2 * QB] + pc[2 * QB:3 * QB] + pc[3 * QB:4 * QB]
    hi = pcs.astype(BF16)
    lo = (pcs - hi.astype(F32)).astype(BF16)
    imp = _nt(ovl_ref[...], hi) + _nt(ovl_ref[...], lo)
    blk = lax.broadcasted_iota(jnp.int32, (n_blk, QB), 0)
    cur = (t0 + lax.broadcasted_iota(jnp.int32, (n_blk, QB), 1)) // SEL_BLOCK
    forced = (blk == 0) | (blk == cur) | (blk == cur - 1)
    score = jnp.where(blk > cur, -1.0, jnp.where(forced, FORCE, imp))
    chunks = [score[r:r + SUBLANES] for r in range(0, n_blk, SUBLANES)]
    sub = lax.broadcasted_iota(jnp.int32, (SUBLANES, QB), 0)
    cnts = [jnp.zeros((SUBLANES, QB), F32) for _ in chunks]
    for m in range(n_blk):
        row = jnp.broadcast_to(score[m:m + 1, :], (SUBLANES, QB))
        for r, ch in enumerate(chunks):
            first = r * SUBLANES
            if first > m:
                beats = jnp.where(row >= ch, 1.0, 0.0)
            elif first + SUBLANES - 1 < m:
                beats = jnp.where(row > ch, 1.0, 0.0)
            else:
                beats = jnp.where(sub + first > m, jnp.where(row >= ch, 1.0, 0.0), jnp.where(row > ch, 1.0, 0.0))
            cnts[r] = cnts[r] + beats
    cnt = jnp.concatenate(cnts, axis=0)
    sel_neg = jnp.where(cnt < float(min(N_SELECT, n_blk)), 0.0, NEG)
    pieces = [jnp.zeros((HEAD_DIM, QB), F32), sel_neg]
    if n_blk < HEAD_DIM:
        pieces.append(jnp.zeros((HEAD_DIM - n_blk, QB), F32))
    placed_t = jnp.concatenate(pieces, axis=0).astype(BF16)
    placed = _nt(eye_ref[...], placed_t)
    qa_ref[...] = (q3 + jnp.concatenate([placed] * HPG, axis=0)).astype(BF16)

    m_ref[...] = jnp.full((ROWS, LANES), M_INIT, F32)
    acc_ref[...] = jnp.zeros((ROWS, LANES), F32)
    n_bias = tbs_ref.shape[1]

    def body(kk, carry):
        off = pl.multiple_of(kk * (SEL_SPAN * QB), SEL_SPAN * QB)
        s = _nt(qa_ref[...], ks_ref[pl.ds(off, SEL_SPAN * QB), :])
        parts = []
        for u in range(SEL_SPAN):
            idx = jnp.clip(qb - (kk * SEL_SPAN + u), -1, n_bias - 2) + 1
            parts.append(s[:, u * QB:(u + 1) * QB] + tbs_ref[:, pl.ds(idx, 1)].reshape(ROWS, LANES))
        tile_max = functools.reduce(jnp.maximum, parts)
        m_old = m_ref[...]
        m_new = jnp.maximum(m_old, jnp.max(tile_max, axis=1, keepdims=True))
        alpha = jnp.exp(m_old - m_new)
        pr = jnp.concatenate([jnp.exp(x - m_new).astype(BF16) for x in parts], axis=1)
        acc_ref[...] = alpha * acc_ref[...] + jnp.dot(pr, vs_ref[pl.ds(off, SEL_SPAN * QB), :],
                                                      preferred_element_type=F32)
        m_ref[...] = m_new
        return carry

    lax.fori_loop(0, qb // SEL_SPAN + 1, body, 0)
    acc = acc_ref[...]
    comb_ref[...] += gates[1] * (acc / pltpu.roll(acc, HEAD_DIM, axis=1))

    qw = (q3 + jnp.where(lane_row == HEAD_DIM, NEG, 0.0)).astype(BF16)
    w_off = pl.multiple_of(t0, QB)
    sw = _nt(qw, kw_ref[pl.ds(w_off, WINDOW + QB), :]) + tbw_ref[...].reshape(ROWS, WINDOW + QB)
    pw = jnp.exp(sw - jnp.max(sw, axis=1, keepdims=True))
    accw = jnp.dot(pw.astype(BF16), vw_ref[pl.ds(w_off, WINDOW + QB), :], preferred_element_type=F32)
    comb_ref[...] += gates[2] * (accw / pltpu.roll(accw, HEAD_DIM, axis=1))

    comb = comb_ref[...]
    lane = lax.broadcasted_iota(jnp.int32, (QB, LANES), 1)
    for half in range(HPG // 2):
        a = comb[(2 * half) * QB:(2 * half + 1) * QB]
        b = comb[(2 * half + 1) * QB:(2 * half + 2) * QB]
        o_ref[0, :, half * LANES:(half + 1) * LANES] = jnp.where(lane < HEAD_DIM, a,
                                                                 pltpu.roll(b, HEAD_DIM, axis=1)).astype(o_ref.dtype)


def rel_bias_tables(rel_bias, seq):
    max_d = (N_SEL_BIAS + 1) * QB
    tab = rel_bias.astype(F32)[rel_bucket(jnp.arange(max_d))]
    i = np.arange(QB)[:, None]
    j = np.arange(QB)[None, :]

    def tiles(ks, max_valid):
        d = np.asarray(ks)[:, None, None] * QB + (i - j)[None]
        ok = (d >= 0) if max_valid is None else ((d >= 0) & (d <= max_valid))
        t = tab[np.clip(d, 0, max_d - 1)]
        t = jnp.where(jnp.asarray(ok)[..., None], t, NEG)
        return jnp.transpose(t, (3, 0, 1, 2))

    tbs = tiles(range(-1, N_SEL_BIAS), None)
    tbw = tiles(range(N_WIN_TILES - 1, -1, -1), WINDOW)
    tbw = jnp.transpose(tbw, (0, 2, 1, 3)).reshape(N_HEADS, QB, N_WIN_TILES * QB)
    nqb = seq // QB
    ncp = seq // CMP_STRIDE
    per_qb = QB // CMP_STRIDE
    width = ncp + per_qb * (nqb - 1)
    d = i - CMP_STRIDE * np.arange(width)[None, :] + QB * (nqb - 1) - (L_CMP - 1)
    v = jnp.transpose(tab[np.clip(d, 0, max_d - 1)], (2, 0, 1))
    bc = jnp.stack([v[:, :, per_qb * (nqb - 1 - b):per_qb * (nqb - 1 - b) + ncp] for b in range(nqb)])
    return tbs, tbw, bc


def nsa_prompt_pallas(q, comp_k, comp_v, kv_sel, kv_win, gn, tables):
    B, S = q.shape[:2]
    assert S % (SEL_SPAN * QB) == 0 and S % KV_CHUNK == 0 and S // SEL_BLOCK <= HEAD_DIM
    tbs, tbw, bc = tables
    nqb = S // QB
    ncp = S // CMP_STRIDE
    n_cmp = comp_k.shape[1]
    n_blk = S // SEL_BLOCK

    def pad_cmp(c):
        c = jnp.pad(c.astype(BF16), ((0, 0), (0, ncp - n_cmp), (0, 0), (0, 0))).transpose(0, 2, 1, 3)
        return jnp.concatenate([c, jnp.zeros_like(c)], axis=-1)

    ck = pad_cmp(comp_k)
    cv = pad_cmp(comp_v)
    c0 = np.arange(ncp)[None, :] * CMP_STRIDE
    s0 = np.arange(n_blk)[:, None] * SEL_BLOCK
    ovl_t = ((c0 <= s0 + SEL_BLOCK - 1) & (c0 + L_CMP - 1 >= s0) & (np.arange(ncp)[None, :] < n_cmp))
    ovl_t = jnp.asarray(ovl_t.astype(np.float32), BF16)
    eye = jnp.asarray(np.eye(QB, dtype=np.float32), BF16)
    gw = HPG * HEAD_DIM
    pq = np.zeros((gw, HPG * LANES), np.float32)
    pq[np.arange(gw), (np.arange(gw) // HEAD_DIM) * LANES + np.arange(gw) % HEAD_DIM] = 1.0
    pk = np.zeros((KV_HEADS, KV_ROW, LANES), np.float32)
    pv = np.zeros((KV_HEADS, KV_ROW, LANES), np.float32)
    for g in range(KV_HEADS):
        pk[g, g * HEAD_DIM + np.arange(HEAD_DIM), np.arange(HEAD_DIM)] = 1.0
        pv[g, (KV_HEADS + g) * HEAD_DIM + np.arange(HEAD_DIM), np.arange(HEAD_DIM)] = 1.0
    onehot = np.zeros((S, LANES), np.float32)
    onehot[np.arange(S), HEAD_DIM + np.arange(S) // SEL_BLOCK] = 1.0

    kv_spec = pl.BlockSpec((1, S, KV_ROW), lambda b, g, i: (b, 0, 0))
    cmp_spec = pl.BlockSpec((1, 1, ncp, LANES), lambda b, g, i: (b, g, 0, 0))
    const2 = lambda b, g, i: (0, 0)
    return pl.pallas_call(
        functools.partial(_nsa_prompt_kernel, n_cmp=n_cmp, n_blk=n_blk),
        grid=(B, KV_HEADS, nqb),
        in_specs=[
            pl.BlockSpec((1, QB, gw), lambda b, g, i: (b, i, g)),
            cmp_spec, cmp_spec, kv_spec, kv_spec,
            pl.BlockSpec((1, QB, LANES), lambda b, g, i: (b, i, g)),
            pl.BlockSpec((1, HPG, QB, ncp), lambda b, g, i: (i, g, 0, 0)),
            pl.BlockSpec((HPG, N_SEL_BIAS + 1, QB, QB), lambda b, g, i: (g, 0, 0, 0)),
            pl.BlockSpec((HPG, QB, N_WIN_TILES * QB), lambda b, g, i: (g, 0, 0)),
            pl.BlockSpec((n_blk, ncp), const2),
            pl.BlockSpec((QB, QB), const2),
            pl.BlockSpec((gw, HPG * LANES), const2),
            pl.BlockSpec((1, KV_ROW, LANES), lambda b, g, i: (g, 0, 0)),
            pl.BlockSpec((1, KV_ROW, LANES), lambda b, g, i: (g, 0, 0)),
            pl.BlockSpec((S, LANES), const2),
        ],
        out_specs=pl.BlockSpec((1, QB, gw), lambda b, g, i: (b, i, g)),
        out_shape=jax.ShapeDtypeStruct((B, S, N_HEADS * HEAD_DIM), BF16),
        scratch_shapes=[pltpu.VMEM((ROWS, LANES), F32), pltpu.VMEM((ROWS, LANES), F32),
                        pltpu.VMEM((ROWS, LANES), BF16), pltpu.VMEM((ROWS, LANES), F32),
                        pltpu.VMEM((S, LANES), BF16), pltpu.VMEM((S, LANES), BF16),
                        pltpu.VMEM((S + WINDOW, LANES), BF16), pltpu.VMEM((S + WINDOW, LANES), BF16)],
        compiler_params=pltpu.CompilerParams(dimension_semantics=("parallel", "parallel", "arbitrary"),
                                             vmem_limit_bytes=VMEM_LIMIT_BYTES),
        name="nsa_prompt",
    )(q, ck, cv, kv_sel, kv_win, gn, bc, tbs, tbw, ovl_t, eye, jnp.asarray(pq, BF16), jnp.asarray(pk, BF16),
      jnp.asarray(pv, BF16), jnp.asarray(onehot, BF16))


def _merge_kernel(ya_ref, yb_ref, yc_ref, ga_ref, gb_ref, gc_ref, x_ref, wa_ref, wb_ref, wc_ref, wo_ref, o_ref):
    def branch(y_ref, g_ref, w_ref):
        return jax.nn.sigmoid(g_ref[...]) * jnp.dot(y_ref[...], w_ref[...], preferred_element_type=F32)

    merged = branch(ya_ref, ga_ref, wa_ref) + branch(yb_ref, gb_ref, wb_ref) + branch(yc_ref, gc_ref, wc_ref)
    o_ref[...] = x_ref[...] + jnp.dot(merged.astype(BF16), wo_ref[...], preferred_element_type=F32)


def merge(ya, yb, yc, za, x, wa, wb, wc, wo):
    T = x.shape[0]
    tm = _pick(T, (512, 256, 128))
    y_spec = pl.BlockSpec((tm, ya.shape[1]), lambda i: (i, 0))
    w_spec = pl.BlockSpec((ya.shape[1], D_MODEL), lambda i: (0, 0))
    return pl.pallas_call(
        _merge_kernel,
        grid=(T // tm,),
        in_specs=[y_spec, y_spec, y_spec,
                  pl.BlockSpec((tm, D_MODEL), lambda i: (i, 0)), pl.BlockSpec((tm, D_MODEL), lambda i: (i, 1)),
                  pl.BlockSpec((tm, D_MODEL), lambda i: (i, 2)),
                  pl.BlockSpec((tm, D_MODEL), lambda i: (i, 0)),
                  w_spec, w_spec, w_spec, pl.BlockSpec((D_MODEL, D_MODEL), lambda i: (0, 0))],
        out_specs=pl.BlockSpec((tm, D_MODEL), lambda i: (i, 0)),
        out_shape=jax.ShapeDtypeStruct((T, D_MODEL), F32),
        compiler_params=pltpu.CompilerParams(dimension_semantics=("parallel",), vmem_limit_bytes=VMEM_LIMIT_BYTES),
        name="merge",
    )(ya, yb, yc, za, za, za, x, wa, wb, wc, wo)


def _ffn_kernel(*refs, routed):
    if routed:
        h_ref, g_ref, wr_ref, br_ref, wg_ref, wu_ref, wd_ref, o_ref, xn_ref, gate_ref = refs
    else:
        h_ref, g_ref, wg_ref, wu_ref, wd_ref, o_ref, xn_ref = refs
    e = pl.program_id(1)
    f = pl.program_id(2)

    @pl.when((e == 0) & (f == 0))
    def _():
        h = h_ref[...]
        xb = _rms_rows(h, g_ref[...]).astype(BF16)
        xn_ref[...] = xb
        o_ref[...] = h
        if routed:
            logits = jnp.dot(xb, wr_ref[...], preferred_element_type=F32) + br_ref[...]
            lane = lax.broadcasted_iota(jnp.int32, logits.shape, 1)
            m1 = jnp.max(logits, axis=1, keepdims=True)
            i1 = jnp.min(jnp.where(logits == m1, lane, LANES), axis=1, keepdims=True)
            rest = jnp.where(lane == i1, M_INIT, logits)
            m2 = jnp.max(rest, axis=1, keepdims=True)
            i2 = jnp.min(jnp.where(rest == m2, lane, LANES), axis=1, keepdims=True)
            r = jnp.exp(m2 - m1)
            gate_ref[...] = jnp.where(lane == i1, 1.0 / (1.0 + r), 0.0) + jnp.where(lane == i2, r / (1.0 + r), 0.0)

    xb = xn_ref[...]
    a = jnp.dot(xb, wg_ref[0], preferred_element_type=F32)
    u = jnp.dot(xb, wu_ref[0], preferred_element_type=F32)
    t = (a * jax.nn.sigmoid(a)) * u
    y = jnp.dot(t.astype(BF16), wd_ref[0], preferred_element_type=F32)
    if routed:
        lane = lax.broadcasted_iota(jnp.int32, gate_ref.shape, 1)
        y = jnp.sum(jnp.where(lane == e, gate_ref[...], 0.0), axis=1, keepdims=True) * y
    o_ref[...] += y


def channel_mixer(h, norm_gain, wg, wu, wd, router=None):
    T = h.shape[0]
    E, _, F = wg.shape
    tm = _pick(T, (512, 256, 128))
    tf = _pick(F, (1408, 1024, 512, 256, 128))
    routed = router is not None
    row = lambda i, e, f: (i, 0)
    in_specs = [pl.BlockSpec((tm, D_MODEL), row), pl.BlockSpec((1, D_MODEL), lambda i, e, f: (0, 0))]
    args = [h, norm_gain.astype(F32)[None]]
    scratch = [pltpu.VMEM((tm, D_MODEL), BF16)]
    if routed:
        in_specs += [pl.BlockSpec((D_MODEL, LANES), lambda i, e, f: (0, 0)), pl.BlockSpec((1, LANES), lambda i, e, f: (0, 0))]
        args += list(router)
        scratch.append(pltpu.VMEM((tm, LANES), F32))
    in_specs += [pl.BlockSpec((1, D_MODEL, tf), lambda i, e, f: (e, 0, f)),
                 pl.BlockSpec((1, D_MODEL, tf), lambda i, e, f: (e, 0, f)),
                 pl.BlockSpec((1, tf, D_MODEL), lambda i, e, f: (e, f, 0))]
    args += [wg, wu, wd]
    return pl.pallas_call(
        functools.partial(_ffn_kernel, routed=routed),
        grid=(T // tm, E, F // tf),
        in_specs=in_specs,
        out_specs=pl.BlockSpec((tm, D_MODEL), row),
        out_shape=jax.ShapeDtypeStruct((T, D_MODEL), F32),
        scratch_shapes=scratch,
        compiler_params=pltpu.CompilerParams(dimension_semantics=("parallel", "arbitrary", "arbitrary"),
                                             vmem_limit_bytes=VMEM_LIMIT_BYTES),
        name="moe" if routed else "ffn",
    )(*args)


def token_mixer(x, l, P, packed, start_pos, conv_buf, h0, pool_buf, attend):
    B, T, _ = x.shape
    x2 = x.reshape(B * T, D_MODEL)
    za, q, kv_c, kv_s, kv_w, gn = projection(x2, P['attn_norm'][l], packed)
    x_rg = za[:, ZA_XRG:ZA_XRG + D_RNN].reshape(B, T, D_RNN)
    g_rg = za[:, ZA_GRG:ZA_GRG + D_RNN].reshape(B, T, D_RNN)
    x_pool = za[:, ZA_XPOOL:ZA_XPOOL + D_POOL].reshape(B, T, D_POOL)
    xc, conv_new = causal_conv(x_rg, conv_buf, P['conv_w'][l], P['conv_b'][l])
    h, h_last = rg_lru(xc, h0, P['rg_w_a'][l], P['rg_b_a'][l], P['rg_w_x'][l], P['rg_b_x'][l], P['rg_lambda'][l])
    y_a = (h * jax.nn.gelu(g_rg.astype(F32))).astype(BF16)
    kv_shape = (B, T, 2, KV_HEADS, HEAD_DIM)
    y_b, win_new = attend(q.reshape(B, T, Q_COLS), kv_c.reshape(kv_shape), kv_s.reshape(B, T, KV_ROW),
                          kv_w.reshape(B, T, KV_ROW), gn.reshape(B, T, GN_COLS))
    y_c, pool_new = pool_mix(x_pool, pool_buf, start_pos, P['w_pool'][l], P['pool_scale'][l])
    out = merge(y_a.reshape(B * T, D_RNN), y_b.reshape(B * T, Q_COLS).astype(BF16), y_c.reshape(B * T, D_POOL).astype(BF16),
                za, x2, P['w_br_rg'][l].astype(BF16), P['w_br_attn'][l].astype(BF16), P['w_br_pool'][l].astype(BF16),
                P['w_out'][l].astype(BF16))
    return out.reshape(B, T, D_MODEL), (kv_c.reshape(kv_shape), kv_s.reshape(kv_shape), win_new, conv_new, h_last, pool_new)


def nsa_prompt(q, kv_cmp, kv_sel, kv_win, gn, *, w1, w2, pe, kn, tables):
    B, S = q.shape[:2]
    comp_k, comp_v, _ = compress_kv(kv_cmp, w1, w2, pe, kn)
    out = nsa_prompt_pallas(q, comp_k, comp_v, kv_sel, kv_win, gn, tables)
    return out, kv_win[:, -min(WINDOW, S):].reshape(B, min(WINDOW, S), 2, KV_HEADS, HEAD_DIM)


def nsa_sample_from_packed(q, kv_cmp, kv_sel, kv_win, gn, **kw):
    B, T = q.shape[:2]
    kv_shape = (B, T, 2, KV_HEADS, HEAD_DIM)
    per_group = 3 * HPG
    gates = jnp.concatenate([gn[..., g * LANES:g * LANES + per_group] for g in range(KV_HEADS)], axis=-1)
    q_f = (q.astype(F32) * HEAD_DIM ** 0.5).reshape(B, T, N_HEADS, HEAD_DIM)
    return nsa_sample(q_f, kv_cmp, kv_sel.reshape(kv_shape), kv_win.reshape(kv_shape),
                      gates.reshape(B, T, N_HEADS, 3), **kw)


def ffn_layer(x, l, P, W):
    B, T, _ = x.shape
    i = l // 2
    if l % 2 == 0:
        y = channel_mixer(x.reshape(B * T, D_MODEL), P['ffn_norm'][l], W['ffn_g'][i], W['ffn_u'][i], W['ffn_d'][i])
    else:
        y = channel_mixer(x.reshape(B * T, D_MODEL), P['ffn_norm'][l], W['moe_g'][i], W['moe_u'][i], W['moe_d'][i],
                          router=W['router'][i])
    return y.reshape(B, T, D_MODEL)


def kernel(x_prompt, x_sample, cache_cmp_kv, cache_sel_kv, cache_win_kv, state_conv, state_rg_h, state_pool,
           page_table, attn_norm, w_in, conv_w, conv_b, rg_w_a, rg_b_a, rg_w_x, rg_b_x, rg_lambda, q_norm, k_norm,
           cmp_pe, w_cmp1, w_cmp2, rel_bias, w_pool, pool_scale, w_br_rg, w_br_attn, w_br_pool, w_out, ffn_norm,
           ffn_w_gate, ffn_w_up, ffn_w_down, w_router, b_router, moe_w_gate, moe_w_up, moe_w_down):
    P = dict(attn_norm=attn_norm, conv_w=conv_w, conv_b=conv_b, rg_w_a=rg_w_a, rg_b_a=rg_b_a,
             rg_w_x=rg_w_x, rg_b_x=rg_b_x, rg_lambda=rg_lambda, w_pool=w_pool,
             pool_scale=pool_scale, w_br_rg=w_br_rg, w_br_attn=w_br_attn, w_br_pool=w_br_pool, w_out=w_out,
             ffn_norm=ffn_norm)
    depth = w_in.shape[0]
    n_moe = w_router.shape[0]
    pad_e = LANES - N_EXPERTS
    W = dict(
        ffn_g=[w[None].astype(BF16) for w in ffn_w_gate], ffn_u=[w[None].astype(BF16) for w in ffn_w_up],
        ffn_d=[w[None].astype(BF16) for w in ffn_w_down],
        moe_g=[w.astype(BF16) for w in moe_w_gate], moe_u=[w.astype(BF16) for w in moe_w_up],
        moe_d=[w.astype(BF16) for w in moe_w_down],
        router=[(jnp.pad(w_router[i], ((0, 0), (0, pad_e))).astype(BF16),
                 jnp.pad(b_router[i].astype(F32), (0, pad_e), constant_values=NEG)[None]) for i in range(n_moe)])
    B = x_prompt.shape[0]
    past_len = page_table.shape[1] * PAGE_SIZE
    zeros_conv = jnp.zeros((B, CONV_W - 1, D_RNN), x_prompt.dtype)
    zeros_h = jnp.zeros((B, D_RNN), F32)
    zeros_pool = jnp.zeros((B, POOL_MAX - 1, D_POOL), x_prompt.dtype)
    y_p, y_s = x_prompt, x_sample
    tables = rel_bias_tables(rel_bias, x_prompt.shape[1])
    p_list, s_list = [], []
    for l in range(depth):
        packed = pack_projection(w_in[l], q_norm[l], k_norm[l, 1], k_norm[l, 2])
        attend_p = functools.partial(nsa_prompt, w1=w_cmp1[l], w2=w_cmp2[l], pe=cmp_pe[l], kn=k_norm[l, 0],
                                     tables=tables)
        y_p, st_p = token_mixer(y_p, l, P, packed, 0, zeros_conv, zeros_h, zeros_pool, attend_p)
        y_p = ffn_layer(y_p, l, P, W)
        p_list.append(st_p)
        attend_s = functools.partial(nsa_sample_from_packed, cmp_pool=cache_cmp_kv[l], sel_pool=cache_sel_kv[l],
                                     win_buf=cache_win_kv[l], page_table=page_table, w1=w_cmp1[l], w2=w_cmp2[l],
                                     pe=cmp_pe[l], kn=k_norm[l, 0], rel_bias=rel_bias)
        y_s, st_s = token_mixer(y_s, l, P, packed, past_len, state_conv[l], state_rg_h[l], state_pool[l], attend_s)
        y_s = ffn_layer(y_s, l, P, W)
        s_list.append(st_s)
    p_cmp_kv, p_sel_kv, p_win_kv, p_conv, p_h, p_pool = [jnp.stack(a) for a in zip(*p_list)]
    s_cmp_kv, s_sel_kv, s_win_kv, s_conv, s_h, s_pool = [jnp.stack(a) for a in zip(*s_list)]
    return (y_p, y_s, p_cmp_kv, p_sel_kv, p_win_kv, p_conv, p_h, p_pool,
            s_cmp_kv, s_sel_kv, s_win_kv, s_conv, s_h, s_pool)
```

```python
import math
import functools

import jax
import jax.numpy as jnp
import numpy as np
from jax import lax
from jax.experimental import pallas as pl
from jax.experimental.pallas import tpu as pltpu

D_MODEL = 1024
PAGE_SIZE = 128
F32 = jnp.float32
BF16 = jnp.bfloat16
EPS = 1e-6
NEG = -1e30
FORCE = 1e4
D_RNN = 512
RG_BLOCKS = 8
RG_BW = D_RNN // RG_BLOCKS
CONV_W = 4
RG_C = 8.0
N_HEADS = 8
KV_HEADS = 2
HPG = N_HEADS // KV_HEADS
HEAD_DIM = 64
L_CMP = 32
CMP_STRIDE = 16
CMP_HIDDEN = 256
SEL_BLOCK = 64
N_SELECT = 16
WINDOW = 512
Q_BLOCK = 128
D_POOL = 512
POOL_WINDOWS = (2, 4, 8, 16)
POOL_GROUPS = 4
POOL_GW = D_POOL // POOL_GROUPS
POOL_MAX = 16
REL_BUCKETS = 32
REL_MAX_DIST = 1024
N_EXPERTS = 8
TOP_K = 2
KV_ROW = 2 * KV_HEADS * HEAD_DIM
SPLITS = (D_RNN, D_RNN, N_HEADS * HEAD_DIM, KV_ROW, KV_ROW, KV_ROW, 3 * N_HEADS, D_POOL, 3 * D_MODEL)

VMEM_LIMIT_BYTES = 52 * 1024 * 1024
LANES = 128
SUBLANES = 8
M_INIT = -3e38


def _pick(n, cands):
    for c in cands:
        if n % c == 0:
            return c
    return n


def _nt(a, b):
    return lax.dot_general(a, b, (((1,), (1,)), ((), ())), preferred_element_type=F32)


def _rms_rows(x, g):
    return x * lax.rsqrt(jnp.mean(x * x, axis=-1, keepdims=True) + EPS) * g


def rms_norm(x, g):
    xf = x.astype(F32)
    y = xf * lax.rsqrt(jnp.mean(xf * xf, axis=-1, keepdims=True) + EPS)
    return (y * g.astype(F32)).astype(x.dtype)


def rel_bucket(dist):
    n_exact = REL_BUCKETS // 2
    d = jnp.maximum(dist, 0)
    df = jnp.maximum(d, 1).astype(F32)
    large = n_exact + (jnp.log(df / n_exact) / math.log(REL_MAX_DIST / n_exact)
                       * (REL_BUCKETS - n_exact)).astype(jnp.int32)
    return jnp.where(d < n_exact, d, jnp.minimum(large, REL_BUCKETS - 1))


def masked_probs(logits, valid):
    logits = jnp.where(valid, logits, NEG)
    m = jnp.max(logits, axis=-1, keepdims=True)
    p = jnp.where(valid, jnp.exp(logits - m), 0.0)
    return p / jnp.maximum(jnp.sum(p, axis=-1, keepdims=True), 1e-30)


def causal_conv(x, buf, w, b):
    xp = jnp.concatenate([buf.astype(x.dtype), x], axis=1)
    y = lax.conv_general_dilated(xp, w[:, None, :].astype(x.dtype), (1,), 'VALID',
                                 dimension_numbers=('NWC', 'WIO', 'NWC'),
                                 feature_group_count=x.shape[-1]) + b
    return y, xp[:, -(CONV_W - 1):]


def rg_lru(xc, h0, w_a, b_a, w_x, b_x, lam):
    B, T, _ = xc.shape
    xb = xc.reshape(B, T, RG_BLOCKS, RG_BW)
    r = jax.nn.sigmoid((jnp.einsum('btnc,ncd->btnd', xb, w_a).reshape(B, T, D_RNN) + b_a).astype(F32))
    i = jax.nn.sigmoid((jnp.einsum('btnc,ncd->btnd', xb, w_x).reshape(B, T, D_RNN) + b_x).astype(F32))
    log_a = -RG_C * r * jax.nn.softplus(-lam.astype(F32))
    a = jnp.exp(log_a)
    u = jnp.sqrt(-jnp.expm1(2.0 * log_a)) * (i * xc.astype(F32))
    u = u.at[:, 0].add(a[:, 0] * h0.astype(F32))

    def combine(lhs, rhs):
        a1, b1 = lhs
        a2, b2 = rhs
        return a1 * a2, a2 * b1 + b2

    _, h = lax.associative_scan(combine, (a, u), axis=1)
    return h, h[:, -1]


def pool_mix(xin, buf, start_pos, w_pool, scale):
    B, T, C = xin.shape
    xf = jnp.concatenate([buf.astype(xin.dtype), xin], axis=1).astype(F32)
    cs = jnp.concatenate([jnp.zeros((B, 1, C), F32), jnp.cumsum(xf, axis=1)], axis=1)
    pos = start_pos + jnp.arange(T)
    means = []
    for g, w in enumerate(POOL_WINDOWS):
        sl = slice(g * POOL_GW, (g + 1) * POOL_GW)
        s = cs[:, POOL_MAX:POOL_MAX + T, sl] - cs[:, POOL_MAX - w:POOL_MAX - w + T, sl]
        cnt = jnp.minimum(pos + 1, w).astype(F32)[None, :, None]
        means.append(s / cnt)
    mixed = (jnp.concatenate(means, axis=-1) - xf[:, POOL_MAX - 1:]).astype(xin.dtype)
    y = jnp.einsum('btgc,gcd->btgd', mixed.reshape(B, T, POOL_GROUPS, POOL_GW), w_pool)
    return y.reshape(B, T, D_POOL) * scale, xf[:, -(POOL_MAX - 1):].astype(xin.dtype)


def compress_kv(rows, w1, w2, pe, kn):
    Bx, T = rows.shape[:2]
    n_ch = T // CMP_STRIDE
    ch = rows[:, :n_ch * CMP_STRIDE].reshape(Bx, n_ch, CMP_STRIDE, 2, KV_HEADS, HEAD_DIM)
    first = jnp.einsum('bnsegd,esdf->bnegf', ch, w1[:, :CMP_STRIDE])
    second = jnp.einsum('bnsegd,esdf->bnegf', ch, w1[:, CMP_STRIDE:])
    pe_term = jnp.einsum('led,eldf->ef', pe, w1)
    h = first[:, :-1] + second[:, 1:] + pe_term[:, None, :]
    comp = jnp.einsum('bnegf,efd->bnegd', jax.nn.gelu(h), w2)
    nc = comp.shape[1]
    comp_end = jnp.arange(nc) * CMP_STRIDE + L_CMP - 1
    return rms_norm(comp[:, :, 0], kn), comp[:, :, 1], comp_end


def overlap_matrix(n_cmp, n_sel):
    c0 = jnp.arange(n_cmp)[:, None] * CMP_STRIDE
    s0 = jnp.arange(n_sel)[None, :] * SEL_BLOCK
    return ((c0 <= s0 + SEL_BLOCK - 1) & (c0 + L_CMP - 1 >= s0)).astype(F32)


def nsa_attend(q, q_pos, comp_k, comp_v, comp_end, overlap, fetch_sel, win_k, win_v, win_pos, gates, rel_bias):
    Bq, Tq = q.shape[:2]
    scale = HEAD_DIM ** -0.5
    qg = q.reshape(Bq, Tq, KV_HEADS, HPG, HEAD_DIM)
    rb = rel_bias.astype(F32)

    def head_bias(buckets):
        return jnp.moveaxis(rb[buckets], -1, 1).reshape(Tq, KV_HEADS, HPG, -1)

    dist_c = q_pos[:, None] - comp_end[None, :]
    lc = jnp.einsum('btgjd,bcgd->btgjc', qg, comp_k, preferred_element_type=F32) * scale + head_bias(rel_bucket(dist_c))
    pc = masked_probs(lc, (dist_c >= 0)[None, :, None, None, :])
    o_cmp = jnp.einsum('btgjc,bcgd->btgjd', pc, comp_v)
    ns = overlap.shape[1]
    imp = jnp.einsum('btgjc,cn->btgn', pc, overlap)
    blk = jnp.arange(ns)
    cur = (q_pos // SEL_BLOCK)[None, :, None, None]
    forced = (blk == 0) | (blk == cur) | (blk == cur - 1)
    score = jnp.where(blk > cur, -1.0, jnp.where(forced, FORCE, imp))
    n_sel = min(N_SELECT, ns)
    _, idx = lax.top_k(score, n_sel)
    kv_s = fetch_sel(idx)
    k_s = kv_s[..., 0, :].reshape(Bq, Tq, KV_HEADS, n_sel * SEL_BLOCK, HEAD_DIM)
    v_s = kv_s[..., 1, :].reshape(Bq, Tq, KV_HEADS, n_sel * SEL_BLOCK, HEAD_DIM)
    pos_s = (idx[..., None] * SEL_BLOCK + jnp.arange(SEL_BLOCK)).reshape(Bq, Tq, KV_HEADS, -1)
    dist_s = q_pos[None, :, None, None] - pos_s
    bias_s = rb.reshape(REL_BUCKETS, KV_HEADS, HPG)[rel_bucket(dist_s), jnp.arange(KV_HEADS)[:, None]]
    ls = jnp.einsum('btgjd,btgkd->btgjk', qg, k_s, preferred_element_type=F32) * scale + jnp.moveaxis(bias_s, -1, 3)
    ps = masked_probs(ls, (dist_s >= 0)[:, :, :, None, :])
    o_sel = jnp.einsum('btgjk,btgkd->btgjd', ps, v_s)
    dist_w = q_pos[:, None] - win_pos[None, :]
    lw = jnp.einsum('btgjd,bwgd->btgjw', qg, win_k, preferred_element_type=F32) * scale + head_bias(rel_bucket(dist_w))
    valid_w = (dist_w >= 0) & (dist_w <= WINDOW) & (win_pos[None, :] >= 0)
    pw = masked_probs(lw, valid_w[None, :, None, None, :])
    o_win = jnp.einsum('btgjw,bwgd->btgjd', pw, win_v)
    g = jax.nn.sigmoid(gates.astype(F32)).reshape(Bq, Tq, KV_HEADS, HPG, 3)
    o = g[..., 0:1] * o_cmp + g[..., 1:2] * o_sel + g[..., 2:3] * o_win
    return o.reshape(Bq, Tq, N_HEADS * HEAD_DIM).astype(q.dtype)


def nsa_sample(q, comp_k, comp_v, kv_sel_new, kv_win_new, gates, *, sel_pool, win_buf, page_table, rel_bias):
    DB, DS = q.shape[:2]
    n_pages = page_table.shape[1]
    past = n_pages * PAGE_SIZE
    q_pos = past + jnp.arange(DS)
    comp_end = jnp.arange(comp_k.shape[1]) * CMP_STRIDE + L_CMP - 1
    total = past + DS
    ns = -(-total // SEL_BLOCK)
    nbp = past // SEL_BLOCK
    nnb = ns - nbp
    bpp = PAGE_SIZE // SEL_BLOCK
    ovl = overlap_matrix(comp_k.shape[1], ns)
    pool_blocks = sel_pool.reshape(-1, SEL_BLOCK, 2, KV_HEADS, HEAD_DIM)
    new_pad = jnp.pad(kv_sel_new.astype(sel_pool.dtype), ((0, 0), (0, nnb * SEL_BLOCK - DS), (0, 0), (0, 0), (0, 0)))
    new_blocks = new_pad.reshape(DB, nnb, SEL_BLOCK, 2, KV_HEADS, HEAD_DIM)
    b_idx = jnp.arange(DB)[:, None, None, None]
    g_idx = jnp.arange(KV_HEADS)[None, None, :, None]

    def fetch(idx):
        ip = jnp.minimum(idx, nbp - 1)
        phys = page_table[b_idx, ip // bpp] * bpp + ip % bpp
        from_past = pool_blocks[phys, :, :, g_idx]
        from_new = new_blocks[b_idx, jnp.clip(idx - nbp, 0, nnb - 1), :, :, g_idx]
        return jnp.where((idx >= nbp)[..., None, None, None], from_new, from_past)

    wb = win_buf.shape[1]
    win = jnp.concatenate([win_buf, kv_win_new.astype(win_buf.dtype)], axis=1)
    win_pos = past - wb + jnp.arange(wb + DS)
    out = nsa_attend(q, q_pos, comp_k, comp_v, comp_end, ovl, fetch, win[:, :, 0], win[:, :, 1], win_pos, gates, rel_bias)
    return out, win[:, -wb:]


PROJ_TN = 512
ZA_COLS = 3 * D_MODEL + 2 * D_RNN + D_POOL
N_ZA = ZA_COLS // PROJ_TN
Q_COLS = N_HEADS * HEAD_DIM
GN_COLS = 2 * LANES
PROJ_COLS = ZA_COLS + Q_COLS + 3 * KV_ROW + GN_COLS
ZA_XRG, ZA_GRG, ZA_XPOOL = 3 * D_MODEL, 3 * D_MODEL + D_RNN, 3 * D_MODEL + 2 * D_RNN


def _proj_kernel(x_ref, g_ref, w_ref, seg_ref, ng_ref, nm_ref, za_ref, q_ref, kvc_ref, kvs_ref, kvw_ref, gn_ref, xn_ref):
    j = pl.program_id(1)

    @pl.when(j == 0)
    def _():
        xn_ref[...] = _rms_rows(x_ref[...], g_ref[...]).astype(BF16)

    acc = jnp.dot(xn_ref[...], w_ref[...], preferred_element_type=F32)

    @pl.when(j < N_ZA)
    def _():
        za_ref[...] = acc

    @pl.when(j >= N_ZA)
    def _():
        sq = acc * acc
        hi = sq.astype(BF16)
        lo = (sq - hi.astype(F32)).astype(BF16)
        ss = (jnp.dot(hi, seg_ref[...], preferred_element_type=F32)
              + jnp.dot(lo, seg_ref[...], preferred_element_type=F32))
        normed = acc * lax.rsqrt(ss * (1.0 / HEAD_DIM) + EPS) * ng_ref[...]
        y = jnp.where(nm_ref[...] > 0.5, normed, acc)

        @pl.when(j == N_ZA)
        def _():
            q_ref[...] = y.astype(BF16)

        @pl.when(j == N_ZA + 1)
        def _():
            kvc_ref[...] = y[:, :KV_ROW]
            kvs_ref[...] = y[:, KV_ROW:]

        @pl.when(j == N_ZA + 2)
        def _():
            kvw_ref[...] = y[:, :KV_ROW]
            gn_ref[...] = y[:, KV_ROW:]


def pack_projection(w_in, q_gain, ks_gain, kw_gain):
    cut = np.cumsum(SPLITS)[:-1].tolist()
    x_rg, g_rg, q, kv_c, kv_s, kv_w, g_nsa, x_pool, g_br = jnp.split(w_in, cut, axis=1)
    per_group = 3 * HPG
    gn = jnp.zeros((w_in.shape[0], GN_COLS), w_in.dtype)
    for g in range(KV_HEADS):
        gn = gn.at[:, g * LANES:g * LANES + per_group].set(g_nsa[:, g * per_group:(g + 1) * per_group])
    w = jnp.concatenate([g_br, x_rg, g_rg, x_pool, q, kv_c, kv_s, kv_w, gn], axis=1).astype(BF16)
    ones_v = jnp.ones((KV_HEADS * HEAD_DIM,), F32)
    zeros_v = jnp.zeros((KV_HEADS * HEAD_DIM,), F32)
    gain = jnp.concatenate([jnp.ones((ZA_COLS,), F32), jnp.tile(q_gain.astype(F32), N_HEADS) * HEAD_DIM ** -0.5,
                            jnp.ones((KV_ROW,), F32),
                            jnp.tile(ks_gain.astype(F32), KV_HEADS), ones_v,
                            jnp.tile(kw_gain.astype(F32), KV_HEADS), ones_v,
                            jnp.ones((GN_COLS,), F32)])
    mask = jnp.concatenate([jnp.zeros((ZA_COLS,), F32), jnp.ones((Q_COLS,), F32), jnp.zeros((KV_ROW,), F32),
                            ones_v, zeros_v, ones_v, zeros_v, jnp.zeros((GN_COLS,), F32)])
    return w, gain[None], mask[None]


def projection(x, norm_gain, packed):
    w, gain, mask = packed
    T = x.shape[0]
    tm = _pick(T, (1024, 512, 256, 128))
    seg = jnp.asarray((np.arange(PROJ_TN)[:, None] // HEAD_DIM == np.arange(PROJ_TN)[None, :] // HEAD_DIM)
                      .astype(np.float32), BF16)
    row = lambda i, j: (i, 0)
    return pl.pallas_call(
        _proj_kernel,
        grid=(T // tm, PROJ_COLS // PROJ_TN),
        in_specs=[pl.BlockSpec((tm, D_MODEL), row),
                  pl.BlockSpec((1, D_MODEL), lambda i, j: (0, 0)),
                  pl.BlockSpec((D_MODEL, PROJ_TN), lambda i, j: (0, j)),
                  pl.BlockSpec((PROJ_TN, PROJ_TN), lambda i, j: (0, 0)),
                  pl.BlockSpec((1, PROJ_TN), lambda i, j: (0, j)),
                  pl.BlockSpec((1, PROJ_TN), lambda i, j: (0, j))],
        out_specs=[pl.BlockSpec((tm, PROJ_TN), lambda i, j: (i, jnp.minimum(j, N_ZA - 1))),
                   pl.BlockSpec((tm, Q_COLS), row),
                   pl.BlockSpec((tm, KV_ROW), row), pl.BlockSpec((tm, KV_ROW), row), pl.BlockSpec((tm, KV_ROW), row),
                   pl.BlockSpec((tm, GN_COLS), row)],
        out_shape=[jax.ShapeDtypeStruct((T, ZA_COLS), F32), jax.ShapeDtypeStruct((T, Q_COLS), BF16),
                   jax.ShapeDtypeStruct((T, KV_ROW), F32), jax.ShapeDtypeStruct((T, KV_ROW), F32),
                   jax.ShapeDtypeStruct((T, KV_ROW), F32), jax.ShapeDtypeStruct((T, GN_COLS), F32)],
        scratch_shapes=[pltpu.VMEM((tm, D_MODEL), BF16)],
        compiler_params=pltpu.CompilerParams(dimension_semantics=("parallel", "arbitrary"),
                                             vmem_limit_bytes=VMEM_LIMIT_BYTES),
        name="projection",
    )(x, norm_gain.astype(F32)[None], w, seg, gain, mask)


CMP_PAIRS = CMP_STRIDE // 2
CMP_GW = KV_HEADS * CMP_HIDDEN


def _compress_kernel(*refs, n_pages, prefetch):
    refs = refs[1:] if prefetch else refs
    pages = refs[:n_pages]
    w1_ref, pe_ref, w2_ref, kn_ref, ck_ref, cv_ref = refs[n_pages:]
    cpp = pages[0].shape[1] // (2 * CMP_STRIDE)
    m = n_pages * cpp
    for e, out_ref in enumerate((ck_ref, cv_ref)):
        acc = jnp.zeros((m, 2 * CMP_GW), F32)
        for p in range(CMP_PAIRS):
            def rows(s):
                return jnp.concatenate([pg[0, pl.ds(2 * s + e, cpp, stride=2 * CMP_STRIDE), :]
                                        for pg in pages], axis=0)
            a = jnp.concatenate([rows(2 * p), rows(2 * p + 1)], axis=1).astype(BF16)
            acc = acc + jnp.dot(a, w1_ref[e, p], preferred_element_type=F32)
        h = acc[:, :CMP_GW] + pltpu.roll(acc[:, CMP_GW:], m - 1, axis=0) + pe_ref[e]
        gl = jax.nn.gelu(h).astype(BF16)
        for g in range(KV_HEADS):
            c = jnp.dot(gl, w2_ref[e, g], preferred_element_type=F32)
            if e == 0:
                c = c * lax.rsqrt(jnp.sum(c * c, axis=1, keepdims=True) * (1.0 / HEAD_DIM) + EPS) * kn_ref[...]
            out_ref[0, g] = c.astype(out_ref.dtype)


def pack_compress(w1, w2, pe, kn):
    halves = w1.reshape(2, 2, CMP_STRIDE, HEAD_DIM, CMP_HIDDEN)
    eye = jnp.eye(KV_HEADS, dtype=w1.dtype)
    bd = jnp.einsum('ehsdf,gk->esgdhkf', halves, eye)
    w1p = bd.reshape(2, CMP_PAIRS, 2 * KV_HEADS * HEAD_DIM, 2 * CMP_GW).astype(BF16)
    pe_term = jnp.einsum('led,eldf->ef', pe, w1)
    pe_t = jnp.tile(pe_term, (1, KV_HEADS))[:, None, :].astype(F32)
    w2p = jnp.zeros((2, KV_HEADS, CMP_GW, LANES), w2.dtype)
    for g in range(KV_HEADS):
        w2p = w2p.at[:, g, g * CMP_HIDDEN:(g + 1) * CMP_HIDDEN, :HEAD_DIM].set(w2)
    knp = jnp.concatenate([kn.astype(F32), jnp.zeros((LANES - HEAD_DIM,), F32)])[None]
    return w1p, pe_t, w2p.astype(BF16), knp


def compress(rows, packed, page_table=None):
    w1p, pe_t, w2p, knp = packed
    page = rows.shape[1]
    rows = rows.reshape(rows.shape[0], 2 * page, LANES)
    if page_table is None:
        bx, n_pages = rows.shape[0], 1
        page_specs = [pl.BlockSpec((1, 2 * page, LANES), lambda b: (b, 0, 0))]
        const = lambda nd: (lambda b: (0,) * nd)
        out_map = lambda b: (b, 0, 0, 0)
        prefetch = 0
    else:
        bx, n_pages = page_table.shape
        page_specs = [pl.BlockSpec((1, 2 * page, LANES), lambda b, pt, k=k: (pt[b, k], 0, 0)) for k in range(n_pages)]
        const = lambda nd: (lambda b, pt: (0,) * nd)
        out_map = lambda b, pt: (b, 0, 0, 0)
        prefetch = 1
    m = n_pages * page // CMP_STRIDE
    in_specs = page_specs + [pl.BlockSpec(w1p.shape, const(4)), pl.BlockSpec(pe_t.shape, const(3)),
                             pl.BlockSpec(w2p.shape, const(4)), pl.BlockSpec(knp.shape, const(2))]
    out_spec = pl.BlockSpec((1, KV_HEADS, m, LANES), out_map)
    out_shape = jax.ShapeDtypeStruct((bx, KV_HEADS, m, LANES), BF16)
    grid_spec = pltpu.PrefetchScalarGridSpec(num_scalar_prefetch=prefetch, grid=(bx,), in_specs=in_specs,
                                             out_specs=[out_spec, out_spec])
    args = ([page_table] if prefetch else []) + [rows] * n_pages + [w1p, pe_t, w2p, knp]
    return pl.pallas_call(
        functools.partial(_compress_kernel, n_pages=n_pages, prefetch=prefetch),
        grid_spec=grid_spec,
        out_shape=[out_shape, out_shape],
        compiler_params=pltpu.CompilerParams(dimension_semantics=("parallel",), vmem_limit_bytes=VMEM_LIMIT_BYTES),
        name="compress",
    )(*args)


MIX_TT = 512
CONV_HALO = SUBLANES
SCAN_UNROLL = 4


def _log1p(y):
    u = 1.0 + y
    return jnp.where(u == 1.0, y, jnp.log(u) * (y / jnp.where(u == 1.0, 1.0, u - 1.0)))


def _neg_expm1(x):
    t = jnp.tanh(0.5 * x)
    return -2.0 * t / (1.0 - t)


def _softplus(x):
    return jnp.maximum(x, 0.0) + _log1p(jnp.exp(-jnp.abs(x)))


def _rglru_coeffs(xc, wa_ref, ba_ref, wx_ref, bx_ref, lam_ref):
    xb = xc.astype(BF16)
    r = jax.nn.sigmoid(jnp.dot(xb, wa_ref[...], preferred_element_type=F32) + ba_ref[...])
    i = jax.nn.sigmoid(jnp.dot(xb, wx_ref[...], preferred_element_type=F32) + bx_ref[...])
    log_a = -RG_C * r * _softplus(-lam_ref[...])
    return jnp.exp(log_a), jnp.sqrt(_neg_expm1(2.0 * log_a)) * (i * xc)


def _pool_project(sums_minus, wp_ref, scale_ref):
    return jnp.dot(sums_minus.astype(BF16), wp_ref[...], preferred_element_type=F32) * scale_ref[...]


def _mixer_seq_kernel(xrg_ref, grg_ref, xpool_ref, conv0_ref, h0_ref, pool0_ref, cw_ref, cb_ref, wa_ref, ba_ref,
                      wx_ref, bx_ref, lam_ref, wp_ref, ps_ref, ya_ref, yc_ref, convn_ref, hn_ref, pooln_ref,
                      xe_ref, pe_ref, a_ref, u_ref, h_ref, carry_ref, *, start_pos):
    i = pl.program_id(1)
    tt = xrg_ref.shape[0]

    @pl.when(i == 0)
    def _():
        xe_ref[0:CONV_HALO, :] = conv0_ref[0]
        pe_ref[0:POOL_MAX, :] = pool0_ref[0]
        carry_ref[...] = h0_ref[0]

    xe_ref[CONV_HALO:CONV_HALO + tt, :] = xrg_ref[...]
    xc = cb_ref[...] + sum(cw_ref[k:k + 1, :] * xe_ref[CONV_HALO - (CONV_W - 1) + k:CONV_HALO - (CONV_W - 1) + k + tt, :]
                           for k in range(CONV_W))
    a, u = _rglru_coeffs(xc, wa_ref, ba_ref, wx_ref, bx_ref, lam_ref)
    a_ref[...] = a
    u_ref[...] = u

    row = lax.broadcasted_iota(jnp.int32, (SUBLANES, D_RNN), 0)

    def block(j, carry):
        r0 = pl.multiple_of(j * SUBLANES, SUBLANES)
        ab = a_ref[pl.ds(r0, SUBLANES), :]
        ub = u_ref[pl.ds(r0, SUBLANES), :]
        for d in (1, 2, 4):
            a_sh = jnp.where(row >= d, pltpu.roll(ab, d, axis=0), 1.0)
            u_sh = jnp.where(row >= d, pltpu.roll(ub, d, axis=0), 0.0)
            ub = ab * u_sh + ub
            ab = ab * a_sh
        hb = ab * carry + ub
        h_ref[pl.ds(r0, SUBLANES), :] = hb
        return jnp.broadcast_to(hb[SUBLANES - 1:SUBLANES, :], (SUBLANES, D_RNN))

    carry = lax.fori_loop(0, tt // SUBLANES, block, carry_ref[...], unroll=SCAN_UNROLL)
    carry_ref[...] = carry
    ya_ref[...] = (h_ref[...] * jax.nn.gelu(grg_ref[...])).astype(ya_ref.dtype)

    pe_ref[POOL_MAX:POOL_MAX + tt, :] = xpool_ref[...]
    pos = start_pos + i * tt + lax.broadcasted_iota(jnp.int32, (tt, 1), 0)
    parts = []
    for g, w in enumerate(POOL_WINDOWS):
        lanes = slice(g * POOL_GW, (g + 1) * POOL_GW)
        s = sum(pe_ref[POOL_MAX - k:POOL_MAX - k + tt, lanes] for k in range(w))
        cnt = jnp.minimum(pos + 1, w).astype(F32)
        parts.append(s / cnt - pe_ref[POOL_MAX:POOL_MAX + tt, lanes])
    yc_ref[...] = _pool_project(jnp.concatenate(parts, axis=1), wp_ref, ps_ref).astype(yc_ref.dtype)

    @pl.when(i == pl.num_programs(1) - 1)
    def _():
        convn_ref[0] = xe_ref[CONV_HALO + tt - (CONV_W - 1):CONV_HALO + tt, :]
        hn_ref[0] = carry[0:1, :]
        pooln_ref[0] = pe_ref[POOL_MAX + tt - (POOL_MAX - 1):POOL_MAX + tt, :]

    xe_ref[0:CONV_HALO, :] = xe_ref[tt:tt + CONV_HALO, :]
    pe_ref[0:POOL_MAX, :] = pe_ref[tt:tt + POOL_MAX, :]


def pack_mixer(conv_w, conv_b, w_a, b_a, w_x, b_x, lam, w_pool, scale):
    def block_diag(w):
        n, c, d = w.shape
        return jnp.einsum('ncd,nm->ncmd', w, jnp.eye(n, dtype=w.dtype)).reshape(n * c, n * d).astype(BF16)

    row = lambda v: v.astype(F32)[None]
    return (conv_w.astype(F32), row(conv_b), block_diag(w_a), row(b_a), block_diag(w_x), row(b_x), row(lam),
            block_diag(w_pool), row(scale))


def mixer_seq(za, batch, seq, packed, conv0, h0, pool0, start_pos):
    tt = min(MIX_TT, seq)
    nt = seq // tt
    conv_pad = jnp.pad(conv0.astype(F32), ((0, 0), (CONV_HALO - (CONV_W - 1), 0), (0, 0)))
    pool_pad = jnp.pad(pool0.astype(F32), ((0, 0), (1, 0), (0, 0)))
    h_pad = jnp.broadcast_to(h0.astype(F32)[:, None, :], (batch, SUBLANES, D_RNN))
    col = lambda c: pl.BlockSpec((tt, D_RNN), lambda b, i: (b * nt + i, c))
    state = lambda rows: pl.BlockSpec((1, rows, D_RNN), lambda b, i: (b, 0, 0))
    full = lambda a: pl.BlockSpec(a.shape, lambda b, i: (0,) * a.ndim)
    out_rows = pl.BlockSpec((tt, D_RNN), lambda b, i: (b * nt + i, 0))
    return pl.pallas_call(
        functools.partial(_mixer_seq_kernel, start_pos=start_pos),
        grid=(batch, nt),
        in_specs=[col(ZA_XRG // D_RNN), col(ZA_GRG // D_RNN), col(ZA_XPOOL // D_RNN), state(CONV_HALO), state(SUBLANES),
                  state(POOL_MAX)] + [full(a) for a in packed],
        out_specs=[out_rows, out_rows, state(CONV_W - 1), state(1), state(POOL_MAX - 1)],
        out_shape=[jax.ShapeDtypeStruct((batch * seq, D_RNN), BF16), jax.ShapeDtypeStruct((batch * seq, D_POOL), BF16),
                   jax.ShapeDtypeStruct((batch, CONV_W - 1, D_RNN), F32), jax.ShapeDtypeStruct((batch, 1, D_RNN), F32),
                   jax.ShapeDtypeStruct((batch, POOL_MAX - 1, D_POOL), F32)],
        scratch_shapes=[pltpu.VMEM((CONV_HALO + tt, D_RNN), F32), pltpu.VMEM((POOL_MAX + tt, D_POOL), F32),
                        pltpu.VMEM((tt, D_RNN), F32), pltpu.VMEM((tt, D_RNN), F32), pltpu.VMEM((tt, D_RNN), F32),
                        pltpu.VMEM((SUBLANES, D_RNN), F32)],
        compiler_params=pltpu.CompilerParams(dimension_semantics=("parallel", "arbitrary"),
                                             vmem_limit_bytes=VMEM_LIMIT_BYTES),
        name="mixer_seq",
    )(za, za, za, conv_pad, h_pad, pool_pad, *packed)


def _mixer_step_kernel(xrg_ref, grg_ref, xpool_ref, conv_ref, h0_ref, pool_ref, cw_ref, cb_ref, wa_ref, ba_ref,
                       wx_ref, bx_ref, lam_ref, wp_ref, ps_ref, ya_ref, yc_ref, hn_ref, *, start_pos):
    x = xrg_ref[...]
    xc = cb_ref[...] + cw_ref[CONV_W - 1:CONV_W, :] * x + sum(cw_ref[k:k + 1, :] * conv_ref[k] for k in range(CONV_W - 1))
    a, u = _rglru_coeffs(xc, wa_ref, ba_ref, wx_ref, bx_ref, lam_ref)
    h = a * h0_ref[...] + u
    hn_ref[...] = h
    ya_ref[...] = (h * jax.nn.gelu(grg_ref[...])).astype(ya_ref.dtype)
    xp = xpool_ref[...]
    parts = []
    for g, w in enumerate(POOL_WINDOWS):
        lanes = slice(g * POOL_GW, (g + 1) * POOL_GW)
        s = xp[:, lanes] + sum(pool_ref[POOL_MAX - 1 - k][:, lanes] for k in range(1, w))
        parts.append(s / float(min(start_pos + 1, w)) - xp[:, lanes])
    yc_ref[...] = _pool_project(jnp.concatenate(parts, axis=1), wp_ref, ps_ref).astype(yc_ref.dtype)


def mixer_step(za, packed, conv_state, h0, pool_state, start_pos):
    batch = za.shape[0]
    conv_t = jnp.swapaxes(conv_state.astype(F32), 0, 1)
    pool_t = jnp.swapaxes(pool_state.astype(F32), 0, 1)
    col = lambda c: pl.BlockSpec((batch, D_RNN), lambda i: (0, c))
    full = lambda a: pl.BlockSpec(a.shape, lambda i: (0,) * a.ndim)
    rows = pl.BlockSpec((batch, D_RNN), lambda i: (0, 0))
    return pl.pallas_call(
        functools.partial(_mixer_step_kernel, start_pos=start_pos),
        grid=(1,),
        in_specs=[col(ZA_XRG // D_RNN), col(ZA_GRG // D_RNN), col(ZA_XPOOL // D_RNN), full(conv_t), rows, full(pool_t)]
        + [full(a) for a in packed],
        out_specs=[rows, rows, rows],
        out_shape=[jax.ShapeDtypeStruct((batch, D_RNN), BF16), jax.ShapeDtypeStruct((batch, D_POOL), BF16),
                   jax.ShapeDtypeStruct((batch, D_RNN), F32)],
        compiler_params=pltpu.CompilerParams(vmem_limit_bytes=VMEM_LIMIT_BYTES),
        name="mixer_step",
    )(za, za, za, conv_t, h0.astype(F32), pool_t, *packed)


QB = Q_BLOCK
ROWS = HPG * QB
N_WIN_TILES = WINDOW // QB + 1
N_SEL_BIAS = REL_MAX_DIST // QB + 2
SEL_SPAN = 4
KV_CHUNK = 512


def _nsa_prompt_kernel(q_ref, ck_ref, cv_ref, kvs_ref, kvw_ref, gate_ref, bc_ref, tbs_ref, tbw_ref,
                       ovl_ref, eye_ref, pq_ref, pk_ref, pv_ref, onehot_ref, o_ref,
                       acc_ref, m_ref, qa_ref, comb_ref, ks_ref, vs_ref, kw_ref, vw_ref, *, n_cmp, n_blk):
    qb = pl.program_id(2)
    t0 = qb * QB
    ncp = ck_ref.shape[2]
    seq = kvs_ref.shape[1]
    lane_row = lax.broadcasted_iota(jnp.int32, (1, LANES), 1)

    @pl.when(qb == 0)
    def _():
        ones_hi = jnp.where(lane_row >= HEAD_DIM, 1.0, 0.0)
        kw_ref[0:WINDOW, :] = jnp.broadcast_to(jnp.where(lane_row == HEAD_DIM, 1.0, 0.0), (WINDOW, LANES)).astype(BF16)
        vw_ref[0:WINDOW, :] = jnp.zeros((WINDOW, LANES), BF16)

        def stage(c, carry):
            r = pl.multiple_of(c * KV_CHUNK, KV_CHUNK)
            sel = kvs_ref[0, pl.ds(r, KV_CHUNK), :].astype(BF16)
            win = kvw_ref[0, pl.ds(r, KV_CHUNK), :].astype(BF16)
            ks_ref[pl.ds(r, KV_CHUNK), :] = (jnp.dot(sel, pk_ref[0], preferred_element_type=F32)
                                             + onehot_ref[pl.ds(r, KV_CHUNK), :].astype(F32)).astype(BF16)
            vs_ref[pl.ds(r, KV_CHUNK), :] = (jnp.dot(sel, pv_ref[0], preferred_element_type=F32) + ones_hi).astype(BF16)
            kw_ref[pl.ds(WINDOW + r, KV_CHUNK), :] = jnp.dot(win, pk_ref[0], preferred_element_type=F32).astype(BF16)
            vw_ref[pl.ds(WINDOW + r, KV_CHUNK), :] = (jnp.dot(win, pv_ref[0], preferred_element_type=F32)
                                                      + ones_hi).astype(BF16)
            return carry

        lax.fori_loop(0, seq // KV_CHUNK, stage, 0)

    q4 = jnp.dot(q_ref[0], pq_ref[...], preferred_element_type=F32)
    q3 = jnp.concatenate([q4[:, j * LANES:(j + 1) * LANES] for j in range(HPG)], axis=0)
    q = q3.astype(BF16)
    sig = jax.nn.sigmoid(gate_ref[0])
    gates = [jnp.concatenate([sig[:, 3 * j + c:3 * j + c + 1] for j in range(HPG)], axis=0) for c in range(3)]

    lc = _nt(q, ck_ref[0, 0]) + bc_ref[0].reshape(ROWS, ncp)
    tok = t0 + lax.broadcasted_iota(jnp.int32, (HPG, QB, ncp), 1).reshape(ROWS, ncp)
    col = lax.broadcasted_iota(jnp.int32, (ROWS, ncp), 1)
    valid = (tok >= col * CMP_STRIDE + (L_CMP - 1)) & (col < n_cmp)
    lc = jnp.where(valid, lc, NEG)
    mx = jnp.max(lc, axis=1, keepdims=True)
    p = jnp.where(valid, jnp.exp(lc - mx), 0.0)
    pc = p / jnp.maximum(jnp.sum(p, axis=1, keepdims=True), 1e-30)
    comb_ref[...] = gates[0] * jnp.dot(pc.astype(BF16), cv_ref[0, 0], preferred_element_type=F32)

    pcs = pc[0:QB] + pc[QB:2 * QB] + pc[2 * QB:3 * QB] + pc[3 * QB:4 * QB]
    hi = pcs.astype(BF16)
    lo = (pcs - hi.astype(F32)).astype(BF16)
    imp = _nt(ovl_ref[...], hi) + _nt(ovl_ref[...], lo)
    blk = lax.broadcasted_iota(jnp.int32, (n_blk, QB), 0)
    cur = (t0 + lax.broadcasted_iota(jnp.int32, (n_blk, QB), 1)) // SEL_BLOCK
    forced = (blk == 0) | (blk == cur) | (blk == cur - 1)
    score = jnp.where(blk > cur, -1.0, jnp.where(forced, FORCE, imp))
    chunks = [score[r:r + SUBLANES] for r in range(0, n_blk, SUBLANES)]
    sub = lax.broadcasted_iota(jnp.int32, (SUBLANES, QB), 0)
    cnts = [jnp.zeros((SUBLANES, QB), F32) for _ in chunks]
    for m in range(n_blk):
        row = jnp.broadcast_to(score[m:m + 1, :], (SUBLANES, QB))
        for r, ch in enumerate(chunks):
            first = r * SUBLANES
            if first > m:
                beats = jnp.where(row >= ch, 1.0, 0.0)
            elif first + SUBLANES - 1 < m:
                beats = jnp.where(row > ch, 1.0, 0.0)
            else:
                beats = jnp.where(sub + first > m, jnp.where(row >= ch, 1.0, 0.0), jnp.where(row > ch, 1.0, 0.0))
            cnts[r] = cnts[r] + beats
    cnt = jnp.concatenate(cnts, axis=0)
    sel_neg = jnp.where(cnt < float(min(N_SELECT, n_blk)), 0.0, NEG)
    pieces = [jnp.zeros((HEAD_DIM, QB), F32), sel_neg]
    if n_blk < HEAD_DIM:
        pieces.append(jnp.zeros((HEAD_DIM - n_blk, QB), F32))
    placed_t = jnp.concatenate(pieces, axis=0).astype(BF16)
    placed = _nt(eye_ref[...], placed_t)
    qa_ref[...] = (q3 + jnp.concatenate([placed] * HPG, axis=0)).astype(BF16)

    m_ref[...] = jnp.full((ROWS, LANES), M_INIT, F32)
    acc_ref[...] = jnp.zeros((ROWS, LANES), F32)
    n_bias = tbs_ref.shape[1]

    def body(kk, carry):
        off = pl.multiple_of(kk * (SEL_SPAN * QB), SEL_SPAN * QB)
        s = _nt(qa_ref[...], ks_ref[pl.ds(off, SEL_SPAN * QB), :])
        parts = []
        for u in range(SEL_SPAN):
            idx = jnp.clip(qb - (kk * SEL_SPAN + u), -1, n_bias - 2) + 1
            parts.append(s[:, u * QB:(u + 1) * QB] + tbs_ref[:, pl.ds(idx, 1)].reshape(ROWS, LANES))
        tile_max = functools.reduce(jnp.maximum, parts)
        m_old = m_ref[...]
        m_new = jnp.maximum(m_old, jnp.max(tile_max, axis=1, keepdims=True))
        alpha = jnp.exp(m_old - m_new)
        pr = jnp.concatenate([jnp.exp(x - m_new).astype(BF16) for x in parts], axis=1)
        acc_ref[...] = alpha * acc_ref[...] + jnp.dot(pr, vs_ref[pl.ds(off, SEL_SPAN * QB), :],
                                                      preferred_element_type=F32)
        m_ref[...] = m_new
        return carry

    lax.fori_loop(0, qb // SEL_SPAN + 1, body, 0)
    acc = acc_ref[...]
    comb_ref[...] += gates[1] * (acc / pltpu.roll(acc, HEAD_DIM, axis=1))

    qw = (q3 + jnp.where(lane_row == HEAD_DIM, NEG, 0.0)).astype(BF16)
    w_off = pl.multiple_of(t0, QB)
    sw = _nt(qw, kw_ref[pl.ds(w_off, WINDOW + QB), :]) + tbw_ref[...].reshape(ROWS, WINDOW + QB)
    pw = jnp.exp(sw - jnp.max(sw, axis=1, keepdims=True))
    accw = jnp.dot(pw.astype(BF16), vw_ref[pl.ds(w_off, WINDOW + QB), :], preferred_element_type=F32)
    comb_ref[...] += gates[2] * (accw / pltpu.roll(accw, HEAD_DIM, axis=1))

    comb = comb_ref[...]
    lane = lax.broadcasted_iota(jnp.int32, (QB, LANES), 1)
    for half in range(HPG // 2):
        a = comb[(2 * half) * QB:(2 * half + 1) * QB]
        b = comb[(2 * half + 1) * QB:(2 * half + 2) * QB]
        o_ref[0, :, half * LANES:(half + 1) * LANES] = jnp.where(lane < HEAD_DIM, a,
                                                                 pltpu.roll(b, HEAD_DIM, axis=1)).astype(o_ref.dtype)


def rel_bias_tables(rel_bias, seq):
    max_d = (N_SEL_BIAS + 1) * QB
    tab = rel_bias.astype(F32)[rel_bucket(jnp.arange(max_d))]
    i = np.arange(QB)[:, None]
    j = np.arange(QB)[None, :]

    def tiles(ks, max_valid):
        d = np.asarray(ks)[:, None, None] * QB + (i - j)[None]
        ok = (d >= 0) if max_valid is None else ((d >= 0) & (d <= max_valid))
        t = tab[np.clip(d, 0, max_d - 1)]
        t = jnp.where(jnp.asarray(ok)[..., None], t, NEG)
        return jnp.transpose(t, (3, 0, 1, 2))

    tbs = tiles(range(-1, N_SEL_BIAS), None)
    tbw = tiles(range(N_WIN_TILES - 1, -1, -1), WINDOW)
    tbw = jnp.transpose(tbw, (0, 2, 1, 3)).reshape(N_HEADS, QB, N_WIN_TILES * QB)
    nqb = seq // QB
    ncp = seq // CMP_STRIDE
    per_qb = QB // CMP_STRIDE
    width = ncp + per_qb * (nqb - 1)
    d = i - CMP_STRIDE * np.arange(width)[None, :] + QB * (nqb - 1) - (L_CMP - 1)
    v = jnp.transpose(tab[np.clip(d, 0, max_d - 1)], (2, 0, 1))
    bc = jnp.stack([v[:, :, per_qb * (nqb - 1 - b):per_qb * (nqb - 1 - b) + ncp] for b in range(nqb)])
    return tbs, tbw, bc


def nsa_prompt_pallas(q, ck, cv, kv_sel, kv_win, gn, tables):
    B, S = q.shape[:2]
    assert S % (SEL_SPAN * QB) == 0 and S % KV_CHUNK == 0 and S // SEL_BLOCK <= HEAD_DIM
    tbs, tbw, bc = tables
    nqb = S // QB
    ncp = S // CMP_STRIDE
    n_cmp = ncp - 1
    n_blk = S // SEL_BLOCK
    c0 = np.arange(ncp)[None, :] * CMP_STRIDE
    s0 = np.arange(n_blk)[:, None] * SEL_BLOCK
    ovl_t = ((c0 <= s0 + SEL_BLOCK - 1) & (c0 + L_CMP - 1 >= s0) & (np.arange(ncp)[None, :] < n_cmp))
    ovl_t = jnp.asarray(ovl_t.astype(np.float32), BF16)
    eye = jnp.asarray(np.eye(QB, dtype=np.float32), BF16)
    gw = HPG * HEAD_DIM
    pq = np.zeros((gw, HPG * LANES), np.float32)
    pq[np.arange(gw), (np.arange(gw) // HEAD_DIM) * LANES + np.arange(gw) % HEAD_DIM] = 1.0
    pk = np.zeros((KV_HEADS, KV_ROW, LANES), np.float32)
    pv = np.zeros((KV_HEADS, KV_ROW, LANES), np.float32)
    for g in range(KV_HEADS):
        pk[g, g * HEAD_DIM + np.arange(HEAD_DIM), np.arange(HEAD_DIM)] = 1.0
        pv[g, (KV_HEADS + g) * HEAD_DIM + np.arange(HEAD_DIM), np.arange(HEAD_DIM)] = 1.0
    onehot = np.zeros((S, LANES), np.float32)
    onehot[np.arange(S), HEAD_DIM + np.arange(S) // SEL_BLOCK] = 1.0

    kv_spec = pl.BlockSpec((1, S, KV_ROW), lambda b, g, i: (b, 0, 0))
    cmp_spec = pl.BlockSpec((1, 1, ncp, LANES), lambda b, g, i: (b, g, 0, 0))
    const2 = lambda b, g, i: (0, 0)
    return pl.pallas_call(
        functools.partial(_nsa_prompt_kernel, n_cmp=n_cmp, n_blk=n_blk),
        grid=(B, KV_HEADS, nqb),
        in_specs=[
            pl.BlockSpec((1, QB, gw), lambda b, g, i: (b, i, g)),
            cmp_spec, cmp_spec, kv_spec, kv_spec,
            pl.BlockSpec((1, QB, LANES), lambda b, g, i: (b, i, g)),
            pl.BlockSpec((1, HPG, QB, ncp), lambda b, g, i: (i, g, 0, 0)),
            pl.BlockSpec((HPG, N_SEL_BIAS + 1, QB, QB), lambda b, g, i: (g, 0, 0, 0)),
            pl.BlockSpec((HPG, QB, N_WIN_TILES * QB), lambda b, g, i: (g, 0, 0)),
            pl.BlockSpec((n_blk, ncp), const2),
            pl.BlockSpec((QB, QB), const2),
            pl.BlockSpec((gw, HPG * LANES), const2),
            pl.BlockSpec((1, KV_ROW, LANES), lambda b, g, i: (g, 0, 0)),
            pl.BlockSpec((1, KV_ROW, LANES), lambda b, g, i: (g, 0, 0)),
            pl.BlockSpec((S, LANES), const2),
        ],
        out_specs=pl.BlockSpec((1, QB, gw), lambda b, g, i: (b, i, g)),
        out_shape=jax.ShapeDtypeStruct((B, S, N_HEADS * HEAD_DIM), BF16),
        scratch_shapes=[pltpu.VMEM((ROWS, LANES), F32), pltpu.VMEM((ROWS, LANES), F32),
                        pltpu.VMEM((ROWS, LANES), BF16), pltpu.VMEM((ROWS, LANES), F32),
                        pltpu.VMEM((S, LANES), BF16), pltpu.VMEM((S, LANES), BF16),
                        pltpu.VMEM((S + WINDOW, LANES), BF16), pltpu.VMEM((S + WINDOW, LANES), BF16)],
        compiler_params=pltpu.CompilerParams(dimension_semantics=("parallel", "parallel", "arbitrary"),
                                             vmem_limit_bytes=VMEM_LIMIT_BYTES),
        name="nsa_prompt",
    )(q, ck, cv, kv_sel, kv_win, gn, bc, tbs, tbw, ovl_t, eye, jnp.asarray(pq, BF16), jnp.asarray(pk, BF16),
      jnp.asarray(pv, BF16), jnp.asarray(onehot, BF16))


def _merge_kernel(ya_ref, yb_ref, yc_ref, ga_ref, gb_ref, gc_ref, x_ref, wa_ref, wb_ref, wc_ref, wo_ref, o_ref):
    def branch(y_ref, g_ref, w_ref):
        return jax.nn.sigmoid(g_ref[...]) * jnp.dot(y_ref[...], w_ref[...], preferred_element_type=F32)

    merged = branch(ya_ref, ga_ref, wa_ref) + branch(yb_ref, gb_ref, wb_ref) + branch(yc_ref, gc_ref, wc_ref)
    o_ref[...] = x_ref[...] + jnp.dot(merged.astype(BF16), wo_ref[...], preferred_element_type=F32)


def merge(ya, yb, yc, za, x, wa, wb, wc, wo):
    T = x.shape[0]
    tm = _pick(T, (512, 256, 128))
    y_spec = pl.BlockSpec((tm, ya.shape[1]), lambda i: (i, 0))
    w_spec = pl.BlockSpec((ya.shape[1], D_MODEL), lambda i: (0, 0))
    return pl.pallas_call(
        _merge_kernel,
        grid=(T // tm,),
        in_specs=[y_spec, y_spec, y_spec,
                  pl.BlockSpec((tm, D_MODEL), lambda i: (i, 0)), pl.BlockSpec((tm, D_MODEL), lambda i: (i, 1)),
                  pl.BlockSpec((tm, D_MODEL), lambda i: (i, 2)),
                  pl.BlockSpec((tm, D_MODEL), lambda i: (i, 0)),
                  w_spec, w_spec, w_spec, pl.BlockSpec((D_MODEL, D_MODEL), lambda i: (0, 0))],
        out_specs=pl.BlockSpec((tm, D_MODEL), lambda i: (i, 0)),
        out_shape=jax.ShapeDtypeStruct((T, D_MODEL), F32),
        compiler_params=pltpu.CompilerParams(dimension_semantics=("parallel",), vmem_limit_bytes=VMEM_LIMIT_BYTES),
        name="merge",
    )(ya, yb, yc, za, za, za, x, wa, wb, wc, wo)


def _ffn_kernel(*refs, routed):
    if routed:
        h_ref, g_ref, wr_ref, br_ref, wg_ref, wu_ref, wd_ref, o_ref, xn_ref, gate_ref = refs
    else:
        h_ref, g_ref, wg_ref, wu_ref, wd_ref, o_ref, xn_ref = refs
    e = pl.program_id(1)
    f = pl.program_id(2)

    @pl.when((e == 0) & (f == 0))
    def _():
        h = h_ref[...]
        xb = _rms_rows(h, g_ref[...]).astype(BF16)
        xn_ref[...] = xb
        o_ref[...] = h
        if routed:
            logits = jnp.dot(xb, wr_ref[...], preferred_element_type=F32) + br_ref[...]
            lane = lax.broadcasted_iota(jnp.int32, logits.shape, 1)
            m1 = jnp.max(logits, axis=1, keepdims=True)
            i1 = jnp.min(jnp.where(logits == m1, lane, LANES), axis=1, keepdims=True)
            rest = jnp.where(lane == i1, M_INIT, logits)
            m2 = jnp.max(rest, axis=1, keepdims=True)
            i2 = jnp.min(jnp.where(rest == m2, lane, LANES), axis=1, keepdims=True)
            r = jnp.exp(m2 - m1)
            gate_ref[...] = jnp.where(lane == i1, 1.0 / (1.0 + r), 0.0) + jnp.where(lane == i2, r / (1.0 + r), 0.0)

    xb = xn_ref[...]
    a = jnp.dot(xb, wg_ref[0], preferred_element_type=F32)
    u = jnp.dot(xb, wu_ref[0], preferred_element_type=F32)
    t = (a * jax.nn.sigmoid(a)) * u
    y = jnp.dot(t.astype(BF16), wd_ref[0], preferred_element_type=F32)
    if routed:
        lane = lax.broadcasted_iota(jnp.int32, gate_ref.shape, 1)
        y = jnp.sum(jnp.where(lane == e, gate_ref[...], 0.0), axis=1, keepdims=True) * y
    o_ref[...] += y


def channel_mixer(h, norm_gain, wg, wu, wd, router=None):
    T = h.shape[0]
    E, _, F = wg.shape
    tm = _pick(T, (512, 256, 128))
    tf = _pick(F, (1408, 1024, 512, 256, 128))
    routed = router is not None
    row = lambda i, e, f: (i, 0)
    in_specs = [pl.BlockSpec((tm, D_MODEL), row), pl.BlockSpec((1, D_MODEL), lambda i, e, f: (0, 0))]
    args = [h, norm_gain.astype(F32)[None]]
    scratch = [pltpu.VMEM((tm, D_MODEL), BF16)]
    if routed:
        in_specs += [pl.BlockSpec((D_MODEL, LANES), lambda i, e, f: (0, 0)), pl.BlockSpec((1, LANES), lambda i, e, f: (0, 0))]
        args += list(router)
        scratch.append(pltpu.VMEM((tm, LANES), F32))
    in_specs += [pl.BlockSpec((1, D_MODEL, tf), lambda i, e, f: (e, 0, f)),
                 pl.BlockSpec((1, D_MODEL, tf), lambda i, e, f: (e, 0, f)),
                 pl.BlockSpec((1, tf, D_MODEL), lambda i, e, f: (e, f, 0))]
    args += [wg, wu, wd]
    return pl.pallas_call(
        functools.partial(_ffn_kernel, routed=routed),
        grid=(T // tm, E, F // tf),
        in_specs=in_specs,
        out_specs=pl.BlockSpec((tm, D_MODEL), row),
        out_shape=jax.ShapeDtypeStruct((T, D_MODEL), F32),
        scratch_shapes=scratch,
        compiler_params=pltpu.CompilerParams(dimension_semantics=("parallel", "arbitrary", "arbitrary"),
                                             vmem_limit_bytes=VMEM_LIMIT_BYTES),
        name="moe" if routed else "ffn",
    )(*args)


def mixer_tail(x2, za, y_a, y_b, y_c, l, P):
    return merge(y_a, y_b, y_c, za, x2, P['w_br_rg'][l].astype(BF16), P['w_br_attn'][l].astype(BF16),
                 P['w_br_pool'][l].astype(BF16), P['w_out'][l].astype(BF16))


def prompt_mixer(x, l, P, packed, tables):
    B, T, _ = x.shape
    x2 = x.reshape(B * T, D_MODEL)
    za, q, kv_c, kv_s, kv_w, gn = projection(x2, P['attn_norm'][l], packed['proj'])
    zeros = lambda rows: jnp.zeros((B, rows, D_RNN), F32)
    y_a, y_c, conv_new, h_last, pool_new = mixer_seq(za, B, T, packed['mix'], zeros(CONV_W - 1), jnp.zeros((B, D_RNN), F32),
                                                     zeros(POOL_MAX - 1), 0)
    ck, cv = compress(kv_c.reshape(B, T, KV_ROW), packed['cmp'])
    y_b = nsa_prompt_pallas(q.reshape(B, T, Q_COLS), ck, cv, kv_s.reshape(B, T, KV_ROW), kv_w.reshape(B, T, KV_ROW),
                            gn.reshape(B, T, GN_COLS), tables)
    out = mixer_tail(x2, za, y_a, y_b.reshape(B * T, Q_COLS), y_c, l, P)
    kv_shape = (B, T, 2, KV_HEADS, HEAD_DIM)
    state = (kv_c.reshape(kv_shape), kv_s.reshape(kv_shape), kv_w.reshape(kv_shape)[:, -min(WINDOW, T):], conv_new,
             h_last[:, 0], pool_new)
    return out.reshape(B, T, D_MODEL), state


def sample_mixer(x, l, P, packed, past_len, conv_state, h0, pool_state, cmp_pool, sel_pool, win_buf, page_table, rel_bias):
    B, T, _ = x.shape
    x2 = x.reshape(B * T, D_MODEL)
    za, q, kv_c, kv_s, kv_w, gn = projection(x2, P['attn_norm'][l], packed['proj'])
    y_a, y_c, h_new = mixer_step(za, packed['mix'], conv_state, h0, pool_state, past_len)
    ck, cv = compress(cmp_pool.reshape(cmp_pool.shape[0], PAGE_SIZE, KV_ROW), packed['cmp'], page_table)
    n_cmp = ck.shape[2] - 1
    unpad = lambda c: jnp.transpose(c[:, :, :n_cmp, :HEAD_DIM].astype(F32), (0, 2, 1, 3))
    kv_shape = (B, T, 2, KV_HEADS, HEAD_DIM)
    per_group = 3 * HPG
    gates = jnp.concatenate([gn[:, g * LANES:g * LANES + per_group] for g in range(KV_HEADS)], axis=-1)
    q_f = (q.astype(F32) * HEAD_DIM ** 0.5).reshape(B, T, N_HEADS, HEAD_DIM)
    y_b, win_new = nsa_sample(q_f, unpad(ck), unpad(cv), kv_s.reshape(kv_shape), kv_w.reshape(kv_shape),
                              gates.reshape(B, T, N_HEADS, 3), sel_pool=sel_pool, win_buf=win_buf,
                              page_table=page_table, rel_bias=rel_bias)
    out = mixer_tail(x2, za, y_a, y_b.reshape(B * T, Q_COLS).astype(BF16), y_c, l, P)
    conv_new = jnp.concatenate([conv_state[:, 1:], za[:, None, ZA_XRG:ZA_XRG + D_RNN]], axis=1)
    pool_new = jnp.concatenate([pool_state[:, 1:], za[:, None, ZA_XPOOL:ZA_XPOOL + D_POOL]], axis=1)
    state = (kv_c.reshape(kv_shape), kv_s.reshape(kv_shape), win_new, conv_new, h_new, pool_new)
    return out.reshape(B, T, D_MODEL), state


def ffn_layer(x, l, P, W):
    B, T, _ = x.shape
    i = l // 2
    if l % 2 == 0:
        y = channel_mixer(x.reshape(B * T, D_MODEL), P['ffn_norm'][l], W['ffn_g'][i], W['ffn_u'][i], W['ffn_d'][i])
    else:
        y = channel_mixer(x.reshape(B * T, D_MODEL), P['ffn_norm'][l], W['moe_g'][i], W['moe_u'][i], W['moe_d'][i],
                          router=W['router'][i])
    return y.reshape(B, T, D_MODEL)


def kernel(x_prompt, x_sample, cache_cmp_kv, cache_sel_kv, cache_win_kv, state_conv, state_rg_h, state_pool,
           page_table, attn_norm, w_in, conv_w, conv_b, rg_w_a, rg_b_a, rg_w_x, rg_b_x, rg_lambda, q_norm, k_norm,
           cmp_pe, w_cmp1, w_cmp2, rel_bias, w_pool, pool_scale, w_br_rg, w_br_attn, w_br_pool, w_out, ffn_norm,
           ffn_w_gate, ffn_w_up, ffn_w_down, w_router, b_router, moe_w_gate, moe_w_up, moe_w_down):
    P = dict(attn_norm=attn_norm, conv_w=conv_w, conv_b=conv_b, rg_w_a=rg_w_a, rg_b_a=rg_b_a,
             rg_w_x=rg_w_x, rg_b_x=rg_b_x, rg_lambda=rg_lambda, w_pool=w_pool,
             pool_scale=pool_scale, w_br_rg=w_br_rg, w_br_attn=w_br_attn, w_br_pool=w_br_pool, w_out=w_out,
             ffn_norm=ffn_norm)
    depth = w_in.shape[0]
    n_moe = w_router.shape[0]
    pad_e = LANES - N_EXPERTS
    W = dict(
        ffn_g=[w[None].astype(BF16) for w in ffn_w_gate], ffn_u=[w[None].astype(BF16) for w in ffn_w_up],
        ffn_d=[w[None].astype(BF16) for w in ffn_w_down],
        moe_g=[w.astype(BF16) for w in moe_w_gate], moe_u=[w.astype(BF16) for w in moe_w_up],
        moe_d=[w.astype(BF16) for w in moe_w_down],
        router=[(jnp.pad(w_router[i], ((0, 0), (0, pad_e))).astype(BF16),
                 jnp.pad(b_router[i].astype(F32), (0, pad_e), constant_values=NEG)[None]) for i in range(n_moe)])
    past_len = page_table.shape[1] * PAGE_SIZE
    y_p, y_s = x_prompt, x_sample
    tables = rel_bias_tables(rel_bias, x_prompt.shape[1])
    p_list, s_list = [], []
    for l in range(depth):
        packed = dict(proj=pack_projection(w_in[l], q_norm[l], k_norm[l, 1], k_norm[l, 2]),
                      cmp=pack_compress(w_cmp1[l], w_cmp2[l], cmp_pe[l], k_norm[l, 0]),
                      mix=pack_mixer(conv_w[l], conv_b[l], rg_w_a[l], rg_b_a[l], rg_w_x[l], rg_b_x[l], rg_lambda[l],
                                     w_pool[l], pool_scale[l]))
        y_p, st_p = prompt_mixer(y_p, l, P, packed, tables)
        y_p = ffn_layer(y_p, l, P, W)
        p_list.append(st_p)
        y_s, st_s = sample_mixer(y_s, l, P, packed, past_len, state_conv[l], state_rg_h[l], state_pool[l],
                                 cache_cmp_kv[l], cache_sel_kv[l], cache_win_kv[l], page_table, rel_bias)
        y_s = ffn_layer(y_s, l, P, W)
        s_list.append(st_s)
    p_cmp_kv, p_sel_kv, p_win_kv, p_conv, p_h, p_pool = [jnp.stack(a) for a in zip(*p_list)]
    s_cmp_kv, s_sel_kv, s_win_kv, s_conv, s_h, s_pool = [jnp.stack(a) for a in zip(*s_list)]
    return (y_p, y_s, p_cmp_kv, p_sel_kv, p_win_kv, p_conv, p_h, p_pool,
            s_cmp_kv, s_sel_kv, s_win_kv, s_conv, s_h, s_pool)
```

```python
import math
import functools

import jax
import jax.numpy as jnp
import numpy as np
from jax import lax
from jax.experimental import pallas as pl
from jax.experimental.pallas import tpu as pltpu

D_MODEL = 1024
PAGE_SIZE = 128
F32 = jnp.float32
BF16 = jnp.bfloat16
EPS = 1e-6
NEG = -1e30
FORCE = 1e4
D_RNN = 512
RG_BLOCKS = 8
RG_BW = D_RNN // RG_BLOCKS
CONV_W = 4
RG_C = 8.0
N_HEADS = 8
KV_HEADS = 2
HPG = N_HEADS // KV_HEADS
HEAD_DIM = 64
L_CMP = 32
CMP_STRIDE = 16
CMP_HIDDEN = 256
SEL_BLOCK = 64
N_SELECT = 16
WINDOW = 512
Q_BLOCK = 128
D_POOL = 512
POOL_WINDOWS = (2, 4, 8, 16)
POOL_GROUPS = 4
POOL_GW = D_POOL // POOL_GROUPS
POOL_MAX = 16
REL_BUCKETS = 32
REL_MAX_DIST = 1024
N_EXPERTS = 8
TOP_K = 2
KV_ROW = 2 * KV_HEADS * HEAD_DIM
SPLITS = (D_RNN, D_RNN, N_HEADS * HEAD_DIM, KV_ROW, KV_ROW, KV_ROW, 3 * N_HEADS, D_POOL, 3 * D_MODEL)

VMEM_LIMIT_BYTES = 52 * 1024 * 1024
LANES = 128
SUBLANES = 8
M_INIT = -3e38


def _pick(n, cands):
    for c in cands:
        if n % c == 0:
            return c
    return n


def _nt(a, b):
    return lax.dot_general(a, b, (((1,), (1,)), ((), ())), preferred_element_type=F32)


def _rms_rows(x, g):
    return x * lax.rsqrt(jnp.mean(x * x, axis=-1, keepdims=True) + EPS) * g


def rms_norm(x, g):
    xf = x.astype(F32)
    y = xf * lax.rsqrt(jnp.mean(xf * xf, axis=-1, keepdims=True) + EPS)
    return (y * g.astype(F32)).astype(x.dtype)


def rel_bucket(dist):
    n_exact = REL_BUCKETS // 2
    d = jnp.maximum(dist, 0)
    df = jnp.maximum(d, 1).astype(F32)
    large = n_exact + (jnp.log(df / n_exact) / math.log(REL_MAX_DIST / n_exact)
                       * (REL_BUCKETS - n_exact)).astype(jnp.int32)
    return jnp.where(d < n_exact, d, jnp.minimum(large, REL_BUCKETS - 1))


def masked_probs(logits, valid):
    logits = jnp.where(valid, logits, NEG)
    m = jnp.max(logits, axis=-1, keepdims=True)
    p = jnp.where(valid, jnp.exp(logits - m), 0.0)
    return p / jnp.maximum(jnp.sum(p, axis=-1, keepdims=True), 1e-30)


def causal_conv(x, buf, w, b):
    xp = jnp.concatenate([buf.astype(x.dtype), x], axis=1)
    y = lax.conv_general_dilated(xp, w[:, None, :].astype(x.dtype), (1,), 'VALID',
                                 dimension_numbers=('NWC', 'WIO', 'NWC'),
                                 feature_group_count=x.shape[-1]) + b
    return y, xp[:, -(CONV_W - 1):]


def rg_lru(xc, h0, w_a, b_a, w_x, b_x, lam):
    B, T, _ = xc.shape
    xb = xc.reshape(B, T, RG_BLOCKS, RG_BW)
    r = jax.nn.sigmoid((jnp.einsum('btnc,ncd->btnd', xb, w_a).reshape(B, T, D_RNN) + b_a).astype(F32))
    i = jax.nn.sigmoid((jnp.einsum('btnc,ncd->btnd', xb, w_x).reshape(B, T, D_RNN) + b_x).astype(F32))
    log_a = -RG_C * r * jax.nn.softplus(-lam.astype(F32))
    a = jnp.exp(log_a)
    u = jnp.sqrt(-jnp.expm1(2.0 * log_a)) * (i * xc.astype(F32))
    u = u.at[:, 0].add(a[:, 0] * h0.astype(F32))

    def combine(lhs, rhs):
        a1, b1 = lhs
        a2, b2 = rhs
        return a1 * a2, a2 * b1 + b2

    _, h = lax.associative_scan(combine, (a, u), axis=1)
    return h, h[:, -1]


def pool_mix(xin, buf, start_pos, w_pool, scale):
    B, T, C = xin.shape
    xf = jnp.concatenate([buf.astype(xin.dtype), xin], axis=1).astype(F32)
    cs = jnp.concatenate([jnp.zeros((B, 1, C), F32), jnp.cumsum(xf, axis=1)], axis=1)
    pos = start_pos + jnp.arange(T)
    means = []
    for g, w in enumerate(POOL_WINDOWS):
        sl = slice(g * POOL_GW, (g + 1) * POOL_GW)
        s = cs[:, POOL_MAX:POOL_MAX + T, sl] - cs[:, POOL_MAX - w:POOL_MAX - w + T, sl]
        cnt = jnp.minimum(pos + 1, w).astype(F32)[None, :, None]
        means.append(s / cnt)
    mixed = (jnp.concatenate(means, axis=-1) - xf[:, POOL_MAX - 1:]).astype(xin.dtype)
    y = jnp.einsum('btgc,gcd->btgd', mixed.reshape(B, T, POOL_GROUPS, POOL_GW), w_pool)
    return y.reshape(B, T, D_POOL) * scale, xf[:, -(POOL_MAX - 1):].astype(xin.dtype)


def compress_kv(rows, w1, w2, pe, kn):
    Bx, T = rows.shape[:2]
    n_ch = T // CMP_STRIDE
    ch = rows[:, :n_ch * CMP_STRIDE].reshape(Bx, n_ch, CMP_STRIDE, 2, KV_HEADS, HEAD_DIM)
    first = jnp.einsum('bnsegd,esdf->bnegf', ch, w1[:, :CMP_STRIDE])
    second = jnp.einsum('bnsegd,esdf->bnegf', ch, w1[:, CMP_STRIDE:])
    pe_term = jnp.einsum('led,eldf->ef', pe, w1)
    h = first[:, :-1] + second[:, 1:] + pe_term[:, None, :]
    comp = jnp.einsum('bnegf,efd->bnegd', jax.nn.gelu(h), w2)
    nc = comp.shape[1]
    comp_end = jnp.arange(nc) * CMP_STRIDE + L_CMP - 1
    return rms_norm(comp[:, :, 0], kn), comp[:, :, 1], comp_end


def overlap_matrix(n_cmp, n_sel):
    c0 = jnp.arange(n_cmp)[:, None] * CMP_STRIDE
    s0 = jnp.arange(n_sel)[None, :] * SEL_BLOCK
    return ((c0 <= s0 + SEL_BLOCK - 1) & (c0 + L_CMP - 1 >= s0)).astype(F32)


def nsa_attend(q, q_pos, comp_k, comp_v, comp_end, overlap, fetch_sel, win_k, win_v, win_pos, gates, rel_bias):
    Bq, Tq = q.shape[:2]
    scale = HEAD_DIM ** -0.5
    qg = q.reshape(Bq, Tq, KV_HEADS, HPG, HEAD_DIM)
    rb = rel_bias.astype(F32)

    def head_bias(buckets):
        return jnp.moveaxis(rb[buckets], -1, 1).reshape(Tq, KV_HEADS, HPG, -1)

    dist_c = q_pos[:, None] - comp_end[None, :]
    lc = jnp.einsum('btgjd,bcgd->btgjc', qg, comp_k, preferred_element_type=F32) * scale + head_bias(rel_bucket(dist_c))
    pc = masked_probs(lc, (dist_c >= 0)[None, :, None, None, :])
    o_cmp = jnp.einsum('btgjc,bcgd->btgjd', pc, comp_v)
    ns = overlap.shape[1]
    imp = jnp.einsum('btgjc,cn->btgn', pc, overlap)
    blk = jnp.arange(ns)
    cur = (q_pos // SEL_BLOCK)[None, :, None, None]
    forced = (blk == 0) | (blk == cur) | (blk == cur - 1)
    score = jnp.where(blk > cur, -1.0, jnp.where(forced, FORCE, imp))
    n_sel = min(N_SELECT, ns)
    _, idx = lax.top_k(score, n_sel)
    kv_s = fetch_sel(idx)
    k_s = kv_s[..., 0, :].reshape(Bq, Tq, KV_HEADS, n_sel * SEL_BLOCK, HEAD_DIM)
    v_s = kv_s[..., 1, :].reshape(Bq, Tq, KV_HEADS, n_sel * SEL_BLOCK, HEAD_DIM)
    pos_s = (idx[..., None] * SEL_BLOCK + jnp.arange(SEL_BLOCK)).reshape(Bq, Tq, KV_HEADS, -1)
    dist_s = q_pos[None, :, None, None] - pos_s
    bias_s = rb.reshape(REL_BUCKETS, KV_HEADS, HPG)[rel_bucket(dist_s), jnp.arange(KV_HEADS)[:, None]]
    ls = jnp.einsum('btgjd,btgkd->btgjk', qg, k_s, preferred_element_type=F32) * scale + jnp.moveaxis(bias_s, -1, 3)
    ps = masked_probs(ls, (dist_s >= 0)[:, :, :, None, :])
    o_sel = jnp.einsum('btgjk,btgkd->btgjd', ps, v_s)
    dist_w = q_pos[:, None] - win_pos[None, :]
    lw = jnp.einsum('btgjd,bwgd->btgjw', qg, win_k, preferred_element_type=F32) * scale + head_bias(rel_bucket(dist_w))
    valid_w = (dist_w >= 0) & (dist_w <= WINDOW) & (win_pos[None, :] >= 0)
    pw = masked_probs(lw, valid_w[None, :, None, None, :])
    o_win = jnp.einsum('btgjw,bwgd->btgjd', pw, win_v)
    g = jax.nn.sigmoid(gates.astype(F32)).reshape(Bq, Tq, KV_HEADS, HPG, 3)
    o = g[..., 0:1] * o_cmp + g[..., 1:2] * o_sel + g[..., 2:3] * o_win
    return o.reshape(Bq, Tq, N_HEADS * HEAD_DIM).astype(q.dtype)


def nsa_sample(q, comp_k, comp_v, kv_sel_new, kv_win_new, gates, *, sel_pool, win_buf, page_table, rel_bias):
    DB, DS = q.shape[:2]
    n_pages = page_table.shape[1]
    past = n_pages * PAGE_SIZE
    q_pos = past + jnp.arange(DS)
    comp_end = jnp.arange(comp_k.shape[1]) * CMP_STRIDE + L_CMP - 1
    total = past + DS
    ns = -(-total // SEL_BLOCK)
    nbp = past // SEL_BLOCK
    nnb = ns - nbp
    bpp = PAGE_SIZE // SEL_BLOCK
    ovl = overlap_matrix(comp_k.shape[1], ns)
    pool_blocks = sel_pool.reshape(-1, SEL_BLOCK, 2, KV_HEADS, HEAD_DIM)
    new_pad = jnp.pad(kv_sel_new.astype(sel_pool.dtype), ((0, 0), (0, nnb * SEL_BLOCK - DS), (0, 0), (0, 0), (0, 0)))
    new_blocks = new_pad.reshape(DB, nnb, SEL_BLOCK, 2, KV_HEADS, HEAD_DIM)
    b_idx = jnp.arange(DB)[:, None, None, None]
    g_idx = jnp.arange(KV_HEADS)[None, None, :, None]

    def fetch(idx):
        ip = jnp.minimum(idx, nbp - 1)
        phys = page_table[b_idx, ip // bpp] * bpp + ip % bpp
        from_past = pool_blocks[phys, :, :, g_idx]
        from_new = new_blocks[b_idx, jnp.clip(idx - nbp, 0, nnb - 1), :, :, g_idx]
        return jnp.where((idx >= nbp)[..., None, None, None], from_new, from_past)

    wb = win_buf.shape[1]
    win = jnp.concatenate([win_buf, kv_win_new.astype(win_buf.dtype)], axis=1)
    win_pos = past - wb + jnp.arange(wb + DS)
    out = nsa_attend(q, q_pos, comp_k, comp_v, comp_end, ovl, fetch, win[:, :, 0], win[:, :, 1], win_pos, gates, rel_bias)
    return out, win[:, -wb:]


PROJ_TN = 512
ZA_COLS = 3 * D_MODEL + 2 * D_RNN + D_POOL
N_ZA = ZA_COLS // PROJ_TN
Q_COLS = N_HEADS * HEAD_DIM
GN_COLS = 2 * LANES
PROJ_COLS = ZA_COLS + Q_COLS + 3 * KV_ROW + GN_COLS
ZA_XRG, ZA_GRG, ZA_XPOOL = 3 * D_MODEL, 3 * D_MODEL + D_RNN, 3 * D_MODEL + 2 * D_RNN


def _proj_kernel(x_ref, g_ref, w_ref, seg_ref, ng_ref, nm_ref, za_ref, q_ref, kvc_ref, kvs_ref, kvw_ref, gn_ref, xn_ref):
    j = pl.program_id(1)

    @pl.when(j == 0)
    def _():
        xn_ref[...] = _rms_rows(x_ref[...], g_ref[...]).astype(BF16)

    acc = jnp.dot(xn_ref[...], w_ref[...], preferred_element_type=F32)

    @pl.when(j < N_ZA)
    def _():
        za_ref[...] = acc

    @pl.when(j >= N_ZA)
    def _():
        sq = acc * acc
        hi = sq.astype(BF16)
        lo = (sq - hi.astype(F32)).astype(BF16)
        ss = (jnp.dot(hi, seg_ref[...], preferred_element_type=F32)
              + jnp.dot(lo, seg_ref[...], preferred_element_type=F32))
        normed = acc * lax.rsqrt(ss * (1.0 / HEAD_DIM) + EPS) * ng_ref[...]
        y = jnp.where(nm_ref[...] > 0.5, normed, acc)

        @pl.when(j == N_ZA)
        def _():
            q_ref[...] = y.astype(BF16)

        @pl.when(j == N_ZA + 1)
        def _():
            kvc_ref[...] = y[:, :KV_ROW]
            kvs_ref[...] = y[:, KV_ROW:]

        @pl.when(j == N_ZA + 2)
        def _():
            kvw_ref[...] = y[:, :KV_ROW]
            gn_ref[...] = y[:, KV_ROW:]


def pack_projection(w_in, q_gain, ks_gain, kw_gain):
    cut = np.cumsum(SPLITS)[:-1].tolist()
    x_rg, g_rg, q, kv_c, kv_s, kv_w, g_nsa, x_pool, g_br = jnp.split(w_in, cut, axis=1)
    per_group = 3 * HPG
    gn = jnp.zeros((w_in.shape[0], GN_COLS), w_in.dtype)
    for g in range(KV_HEADS):
        gn = gn.at[:, g * LANES:g * LANES + per_group].set(g_nsa[:, g * per_group:(g + 1) * per_group])
    w = jnp.concatenate([g_br, x_rg, g_rg, x_pool, q, kv_c, kv_s, kv_w, gn], axis=1).astype(BF16)
    ones_v = jnp.ones((KV_HEADS * HEAD_DIM,), F32)
    zeros_v = jnp.zeros((KV_HEADS * HEAD_DIM,), F32)
    gain = jnp.concatenate([jnp.ones((ZA_COLS,), F32), jnp.tile(q_gain.astype(F32), N_HEADS) * HEAD_DIM ** -0.5,
                            jnp.ones((KV_ROW,), F32),
                            jnp.tile(ks_gain.astype(F32), KV_HEADS), ones_v,
                            jnp.tile(kw_gain.astype(F32), KV_HEADS), ones_v,
                            jnp.ones((GN_COLS,), F32)])
    mask = jnp.concatenate([jnp.zeros((ZA_COLS,), F32), jnp.ones((Q_COLS,), F32), jnp.zeros((KV_ROW,), F32),
                            ones_v, zeros_v, ones_v, zeros_v, jnp.zeros((GN_COLS,), F32)])
    return w, gain[None], mask[None]


def projection(x, norm_gain, packed):
    w, gain, mask = packed
    T = x.shape[0]
    tm = _pick(T, (1024, 512, 256, 128))
    seg = jnp.asarray((np.arange(PROJ_TN)[:, None] // HEAD_DIM == np.arange(PROJ_TN)[None, :] // HEAD_DIM)
                      .astype(np.float32), BF16)
    row = lambda i, j: (i, 0)
    return pl.pallas_call(
        _proj_kernel,
        grid=(T // tm, PROJ_COLS // PROJ_TN),
        in_specs=[pl.BlockSpec((tm, D_MODEL), row),
                  pl.BlockSpec((1, D_MODEL), lambda i, j: (0, 0)),
                  pl.BlockSpec((D_MODEL, PROJ_TN), lambda i, j: (0, j)),
                  pl.BlockSpec((PROJ_TN, PROJ_TN), lambda i, j: (0, 0)),
                  pl.BlockSpec((1, PROJ_TN), lambda i, j: (0, j)),
                  pl.BlockSpec((1, PROJ_TN), lambda i, j: (0, j))],
        out_specs=[pl.BlockSpec((tm, PROJ_TN), lambda i, j: (i, jnp.minimum(j, N_ZA - 1))),
                   pl.BlockSpec((tm, Q_COLS), row),
                   pl.BlockSpec((tm, KV_ROW), row), pl.BlockSpec((tm, KV_ROW), row), pl.BlockSpec((tm, KV_ROW), row),
                   pl.BlockSpec((tm, GN_COLS), row)],
        out_shape=[jax.ShapeDtypeStruct((T, ZA_COLS), F32), jax.ShapeDtypeStruct((T, Q_COLS), BF16),
                   jax.ShapeDtypeStruct((T, KV_ROW), F32), jax.ShapeDtypeStruct((T, KV_ROW), F32),
                   jax.ShapeDtypeStruct((T, KV_ROW), F32), jax.ShapeDtypeStruct((T, GN_COLS), F32)],
        scratch_shapes=[pltpu.VMEM((tm, D_MODEL), BF16)],
        compiler_params=pltpu.CompilerParams(dimension_semantics=("parallel", "arbitrary"),
                                             vmem_limit_bytes=VMEM_LIMIT_BYTES),
        name="projection",
    )(x, norm_gain.astype(F32)[None], w, seg, gain, mask)


CMP_PAIRS = CMP_STRIDE // 2
CMP_GW = KV_HEADS * CMP_HIDDEN


def _compress_kernel(*refs, n_pages, prefetch):
    refs = refs[1:] if prefetch else refs
    pages = refs[:n_pages]
    w1_ref, pe_ref, w2_ref, kn_ref, ck_ref, cv_ref = refs[n_pages:]
    cpp = pages[0].shape[1] // (2 * CMP_STRIDE)
    m = n_pages * cpp
    for e, out_ref in enumerate((ck_ref, cv_ref)):
        acc = jnp.zeros((m, 2 * CMP_GW), F32)
        for p in range(CMP_PAIRS):
            def rows(s):
                return jnp.concatenate([pg[0, pl.ds(2 * s + e, cpp, stride=2 * CMP_STRIDE), :]
                                        for pg in pages], axis=0)
            a = jnp.concatenate([rows(2 * p), rows(2 * p + 1)], axis=1).astype(BF16)
            acc = acc + jnp.dot(a, w1_ref[e, p], preferred_element_type=F32)
        h = acc[:, :CMP_GW] + pltpu.roll(acc[:, CMP_GW:], m - 1, axis=0) + pe_ref[e]
        gl = jax.nn.gelu(h).astype(BF16)
        for g in range(KV_HEADS):
            c = jnp.dot(gl, w2_ref[e, g], preferred_element_type=F32)
            if e == 0:
                c = c * lax.rsqrt(jnp.sum(c * c, axis=1, keepdims=True) * (1.0 / HEAD_DIM) + EPS) * kn_ref[...]
            out_ref[0, g] = c.astype(out_ref.dtype)


def pack_compress(w1, w2, pe, kn):
    halves = w1.reshape(2, 2, CMP_STRIDE, HEAD_DIM, CMP_HIDDEN)
    eye = jnp.eye(KV_HEADS, dtype=w1.dtype)
    bd = jnp.einsum('ehsdf,gk->esgdhkf', halves, eye)
    w1p = bd.reshape(2, CMP_PAIRS, 2 * KV_HEADS * HEAD_DIM, 2 * CMP_GW).astype(BF16)
    pe_term = jnp.einsum('led,eldf->ef', pe, w1)
    pe_t = jnp.tile(pe_term, (1, KV_HEADS))[:, None, :].astype(F32)
    w2p = jnp.zeros((2, KV_HEADS, CMP_GW, LANES), w2.dtype)
    for g in range(KV_HEADS):
        w2p = w2p.at[:, g, g * CMP_HIDDEN:(g + 1) * CMP_HIDDEN, :HEAD_DIM].set(w2)
    knp = jnp.concatenate([kn.astype(F32), jnp.zeros((LANES - HEAD_DIM,), F32)])[None]
    return w1p, pe_t, w2p.astype(BF16), knp


def compress(rows, packed, page_table=None):
    w1p, pe_t, w2p, knp = packed
    page = rows.shape[1]
    rows = rows.reshape(rows.shape[0], 2 * page, LANES)
    if page_table is None:
        bx, n_pages = rows.shape[0], 1
        page_specs = [pl.BlockSpec((1, 2 * page, LANES), lambda b: (b, 0, 0))]
        const = lambda nd: (lambda b: (0,) * nd)
        out_map = lambda b: (b, 0, 0, 0)
        prefetch = 0
    else:
        bx, n_pages = page_table.shape
        page_specs = [pl.BlockSpec((1, 2 * page, LANES), lambda b, pt, k=k: (pt[b, k], 0, 0)) for k in range(n_pages)]
        const = lambda nd: (lambda b, pt: (0,) * nd)
        out_map = lambda b, pt: (b, 0, 0, 0)
        prefetch = 1
    m = n_pages * page // CMP_STRIDE
    in_specs = page_specs + [pl.BlockSpec(w1p.shape, const(4)), pl.BlockSpec(pe_t.shape, const(3)),
                             pl.BlockSpec(w2p.shape, const(4)), pl.BlockSpec(knp.shape, const(2))]
    out_spec = pl.BlockSpec((1, KV_HEADS, m, LANES), out_map)
    out_shape = jax.ShapeDtypeStruct((bx, KV_HEADS, m, LANES), BF16)
    grid_spec = pltpu.PrefetchScalarGridSpec(num_scalar_prefetch=prefetch, grid=(bx,), in_specs=in_specs,
                                             out_specs=[out_spec, out_spec])
    args = ([page_table] if prefetch else []) + [rows] * n_pages + [w1p, pe_t, w2p, knp]
    return pl.pallas_call(
        functools.partial(_compress_kernel, n_pages=n_pages, prefetch=prefetch),
        grid_spec=grid_spec,
        out_shape=[out_shape, out_shape],
        compiler_params=pltpu.CompilerParams(dimension_semantics=("parallel",), vmem_limit_bytes=VMEM_LIMIT_BYTES),
        name="compress",
    )(*args)


MIX_TT = 512
CONV_HALO = SUBLANES
SCAN_UNROLL = 4


def _log1p(y):
    u = 1.0 + y
    return jnp.where(u == 1.0, y, jnp.log(u) * (y / jnp.where(u == 1.0, 1.0, u - 1.0)))


def _neg_expm1(x):
    t = jnp.tanh(0.5 * x)
    return -2.0 * t / (1.0 - t)


def _softplus(x):
    return jnp.maximum(x, 0.0) + _log1p(jnp.exp(-jnp.abs(x)))


def _rglru_coeffs(xc, wa_ref, ba_ref, wx_ref, bx_ref, lam_ref):
    xb = xc.astype(BF16)
    r = jax.nn.sigmoid(jnp.dot(xb, wa_ref[...], preferred_element_type=F32) + ba_ref[...])
    i = jax.nn.sigmoid(jnp.dot(xb, wx_ref[...], preferred_element_type=F32) + bx_ref[...])
    log_a = -RG_C * r * _softplus(-lam_ref[...])
    return jnp.exp(log_a), jnp.sqrt(_neg_expm1(2.0 * log_a)) * (i * xc)


def _pool_project(sums_minus, wp_ref, scale_ref):
    return jnp.dot(sums_minus.astype(BF16), wp_ref[...], preferred_element_type=F32) * scale_ref[...]


def _mixer_seq_kernel(xrg_ref, grg_ref, xpool_ref, conv0_ref, h0_ref, pool0_ref, cw_ref, cb_ref, wa_ref, ba_ref,
                      wx_ref, bx_ref, lam_ref, wp_ref, ps_ref, ya_ref, yc_ref, convn_ref, hn_ref, pooln_ref,
                      xe_ref, pe_ref, a_ref, u_ref, h_ref, carry_ref, *, start_pos):
    i = pl.program_id(1)
    tt = xrg_ref.shape[0]

    @pl.when(i == 0)
    def _():
        xe_ref[0:CONV_HALO, :] = conv0_ref[0]
        pe_ref[0:POOL_MAX, :] = pool0_ref[0]
        carry_ref[...] = h0_ref[0]

    xe_ref[CONV_HALO:CONV_HALO + tt, :] = xrg_ref[...]
    xc = cb_ref[...] + sum(cw_ref[k:k + 1, :] * xe_ref[CONV_HALO - (CONV_W - 1) + k:CONV_HALO - (CONV_W - 1) + k + tt, :]
                           for k in range(CONV_W))
    a, u = _rglru_coeffs(xc, wa_ref, ba_ref, wx_ref, bx_ref, lam_ref)
    a_ref[...] = a
    u_ref[...] = u

    row = lax.broadcasted_iota(jnp.int32, (SUBLANES, D_RNN), 0)

    def block(j, carry):
        r0 = pl.multiple_of(j * SUBLANES, SUBLANES)
        ab = a_ref[pl.ds(r0, SUBLANES), :]
        ub = u_ref[pl.ds(r0, SUBLANES), :]
        for d in (1, 2, 4):
            a_sh = jnp.where(row >= d, pltpu.roll(ab, d, axis=0), 1.0)
            u_sh = jnp.where(row >= d, pltpu.roll(ub, d, axis=0), 0.0)
            ub = ab * u_sh + ub
            ab = ab * a_sh
        hb = ab * carry + ub
        h_ref[pl.ds(r0, SUBLANES), :] = hb
        return jnp.broadcast_to(hb[SUBLANES - 1:SUBLANES, :], (SUBLANES, D_RNN))

    carry = lax.fori_loop(0, tt // SUBLANES, block, carry_ref[...], unroll=SCAN_UNROLL)
    carry_ref[...] = carry
    ya_ref[...] = (h_ref[...] * jax.nn.gelu(grg_ref[...])).astype(ya_ref.dtype)

    pe_ref[POOL_MAX:POOL_MAX + tt, :] = xpool_ref[...]
    pos = start_pos + i * tt + lax.broadcasted_iota(jnp.int32, (tt, 1), 0)
    parts = []
    for g, w in enumerate(POOL_WINDOWS):
        lanes = slice(g * POOL_GW, (g + 1) * POOL_GW)
        s = sum(pe_ref[POOL_MAX - k:POOL_MAX - k + tt, lanes] for k in range(w))
        cnt = jnp.minimum(pos + 1, w).astype(F32)
        parts.append(s / cnt - pe_ref[POOL_MAX:POOL_MAX + tt, lanes])
    yc_ref[...] = _pool_project(jnp.concatenate(parts, axis=1), wp_ref, ps_ref).astype(yc_ref.dtype)

    @pl.when(i == pl.num_programs(1) - 1)
    def _():
        convn_ref[0] = xe_ref[CONV_HALO + tt - (CONV_W - 1):CONV_HALO + tt, :]
        hn_ref[0] = carry[0:1, :]
        pooln_ref[0] = pe_ref[POOL_MAX + tt - (POOL_MAX - 1):POOL_MAX + tt, :]

    xe_ref[0:CONV_HALO, :] = xe_ref[tt:tt + CONV_HALO, :]
    pe_ref[0:POOL_MAX, :] = pe_ref[tt:tt + POOL_MAX, :]


def pack_mixer(conv_w, conv_b, w_a, b_a, w_x, b_x, lam, w_pool, scale):
    def block_diag(w):
        n, c, d = w.shape
        return jnp.einsum('ncd,nm->ncmd', w, jnp.eye(n, dtype=w.dtype)).reshape(n * c, n * d).astype(BF16)

    row = lambda v: v.astype(F32)[None]
    return (conv_w.astype(F32), row(conv_b), block_diag(w_a), row(b_a), block_diag(w_x), row(b_x), row(lam),
            block_diag(w_pool), row(scale))


def mixer_seq(za, batch, seq, packed, conv0, h0, pool0, start_pos):
    tt = min(MIX_TT, seq)
    nt = seq // tt
    conv_pad = jnp.pad(conv0.astype(F32), ((0, 0), (CONV_HALO - (CONV_W - 1), 0), (0, 0)))
    pool_pad = jnp.pad(pool0.astype(F32), ((0, 0), (1, 0), (0, 0)))
    h_pad = jnp.broadcast_to(h0.astype(F32)[:, None, :], (batch, SUBLANES, D_RNN))
    col = lambda c: pl.BlockSpec((tt, D_RNN), lambda b, i: (b * nt + i, c))
    state = lambda rows: pl.BlockSpec((1, rows, D_RNN), lambda b, i: (b, 0, 0))
    full = lambda a: pl.BlockSpec(a.shape, lambda b, i: (0,) * a.ndim)
    out_rows = pl.BlockSpec((tt, D_RNN), lambda b, i: (b * nt + i, 0))
    return pl.pallas_call(
        functools.partial(_mixer_seq_kernel, start_pos=start_pos),
        grid=(batch, nt),
        in_specs=[col(ZA_XRG // D_RNN), col(ZA_GRG // D_RNN), col(ZA_XPOOL // D_RNN), state(CONV_HALO), state(SUBLANES),
                  state(POOL_MAX)] + [full(a) for a in packed],
        out_specs=[out_rows, out_rows, state(CONV_W - 1), state(1), state(POOL_MAX - 1)],
        out_shape=[jax.ShapeDtypeStruct((batch * seq, D_RNN), BF16), jax.ShapeDtypeStruct((batch * seq, D_POOL), BF16),
                   jax.ShapeDtypeStruct((batch, CONV_W - 1, D_RNN), F32), jax.ShapeDtypeStruct((batch, 1, D_RNN), F32),
                   jax.ShapeDtypeStruct((batch, POOL_MAX - 1, D_POOL), F32)],
        scratch_shapes=[pltpu.VMEM((CONV_HALO + tt, D_RNN), F32), pltpu.VMEM((POOL_MAX + tt, D_POOL), F32),
                        pltpu.VMEM((tt, D_RNN), F32), pltpu.VMEM((tt, D_RNN), F32), pltpu.VMEM((tt, D_RNN), F32),
                        pltpu.VMEM((SUBLANES, D_RNN), F32)],
        compiler_params=pltpu.CompilerParams(dimension_semantics=("parallel", "arbitrary"),
                                             vmem_limit_bytes=VMEM_LIMIT_BYTES),
        name="mixer_seq",
    )(za, za, za, conv_pad, h_pad, pool_pad, *packed)


def _mixer_step_kernel(xrg_ref, grg_ref, xpool_ref, conv_ref, h0_ref, pool_ref, cw_ref, cb_ref, wa_ref, ba_ref,
                       wx_ref, bx_ref, lam_ref, wp_ref, ps_ref, ya_ref, yc_ref, hn_ref, *, start_pos):
    x = xrg_ref[...]
    xc = cb_ref[...] + cw_ref[CONV_W - 1:CONV_W, :] * x + sum(cw_ref[k:k + 1, :] * conv_ref[k] for k in range(CONV_W - 1))
    a, u = _rglru_coeffs(xc, wa_ref, ba_ref, wx_ref, bx_ref, lam_ref)
    h = a * h0_ref[...] + u
    hn_ref[...] = h
    ya_ref[...] = (h * jax.nn.gelu(grg_ref[...])).astype(ya_ref.dtype)
    xp = xpool_ref[...]
    parts = []
    for g, w in enumerate(POOL_WINDOWS):
        lanes = slice(g * POOL_GW, (g + 1) * POOL_GW)
        s = xp[:, lanes] + sum(pool_ref[POOL_MAX - 1 - k][:, lanes] for k in range(1, w))
        parts.append(s / float(min(start_pos + 1, w)) - xp[:, lanes])
    yc_ref[...] = _pool_project(jnp.concatenate(parts, axis=1), wp_ref, ps_ref).astype(yc_ref.dtype)


def mixer_step(za, packed, conv_state, h0, pool_state, start_pos):
    batch = za.shape[0]
    conv_t = jnp.swapaxes(conv_state.astype(F32), 0, 1)
    pool_t = jnp.swapaxes(pool_state.astype(F32), 0, 1)
    col = lambda c: pl.BlockSpec((batch, D_RNN), lambda i: (0, c))
    full = lambda a: pl.BlockSpec(a.shape, lambda i: (0,) * a.ndim)
    rows = pl.BlockSpec((batch, D_RNN), lambda i: (0, 0))
    return pl.pallas_call(
        functools.partial(_mixer_step_kernel, start_pos=start_pos),
        grid=(1,),
        in_specs=[col(ZA_XRG // D_RNN), col(ZA_GRG // D_RNN), col(ZA_XPOOL // D_RNN), full(conv_t), rows, full(pool_t)]
        + [full(a) for a in packed],
        out_specs=[rows, rows, rows],
        out_shape=[jax.ShapeDtypeStruct((batch, D_RNN), BF16), jax.ShapeDtypeStruct((batch, D_POOL), BF16),
                   jax.ShapeDtypeStruct((batch, D_RNN), F32)],
        compiler_params=pltpu.CompilerParams(vmem_limit_bytes=VMEM_LIMIT_BYTES),
        name="mixer_step",
    )(za, za, za, conv_t, h0.astype(F32), pool_t, *packed)


QB = Q_BLOCK
ROWS = HPG * QB
N_WIN_TILES = WINDOW // QB + 1
N_SEL_BIAS = REL_MAX_DIST // QB + 2
SEL_SPAN = 4
KV_CHUNK = 512


def _nsa_prompt_kernel(q_ref, ck_ref, cv_ref, kvs_ref, kvw_ref, gate_ref, bc_ref, tbs_ref, tbw_ref,
                       ovl_ref, eye_ref, pq_ref, pk_ref, pv_ref, onehot_ref, o_ref,
                       acc_ref, m_ref, qa_ref, comb_ref, ks_ref, vs_ref, kw_ref, vw_ref, *, n_cmp, n_blk):
    qb = pl.program_id(2)
    t0 = qb * QB
    ncp = ck_ref.shape[2]
    seq = kvs_ref.shape[1]
    lane_row = lax.broadcasted_iota(jnp.int32, (1, LANES), 1)

    @pl.when(qb == 0)
    def _():
        ones_hi = jnp.where(lane_row >= HEAD_DIM, 1.0, 0.0)
        kw_ref[0:WINDOW, :] = jnp.broadcast_to(jnp.where(lane_row == HEAD_DIM, 1.0, 0.0), (WINDOW, LANES)).astype(BF16)
        vw_ref[0:WINDOW, :] = jnp.zeros((WINDOW, LANES), BF16)

        def stage(c, carry):
            r = pl.multiple_of(c * KV_CHUNK, KV_CHUNK)
            sel = kvs_ref[0, pl.ds(r, KV_CHUNK), :].astype(BF16)
            win = kvw_ref[0, pl.ds(r, KV_CHUNK), :].astype(BF16)
            ks_ref[pl.ds(r, KV_CHUNK), :] = (jnp.dot(sel, pk_ref[0], preferred_element_type=F32)
                                             + onehot_ref[pl.ds(r, KV_CHUNK), :].astype(F32)).astype(BF16)
            vs_ref[pl.ds(r, KV_CHUNK), :] = (jnp.dot(sel, pv_ref[0], preferred_element_type=F32) + ones_hi).astype(BF16)
            kw_ref[pl.ds(WINDOW + r, KV_CHUNK), :] = jnp.dot(win, pk_ref[0], preferred_element_type=F32).astype(BF16)
            vw_ref[pl.ds(WINDOW + r, KV_CHUNK), :] = (jnp.dot(win, pv_ref[0], preferred_element_type=F32)
                                                      + ones_hi).astype(BF16)
            return carry

        lax.fori_loop(0, seq // KV_CHUNK, stage, 0)

    q4 = jnp.dot(q_ref[0], pq_ref[...], preferred_element_type=F32)
    q3 = jnp.concatenate([q4[:, j * LANES:(j + 1) * LANES] for j in range(HPG)], axis=0)
    q = q3.astype(BF16)
    sig = jax.nn.sigmoid(gate_ref[0])
    gates = [jnp.concatenate([sig[:, 3 * j + c:3 * j + c + 1] for j in range(HPG)], axis=0) for c in range(3)]

    lc = _nt(q, ck_ref[0, 0]) + bc_ref[0].reshape(ROWS, ncp)
    tok = t0 + lax.broadcasted_iota(jnp.int32, (HPG, QB, ncp), 1).reshape(ROWS, ncp)
    col = lax.broadcasted_iota(jnp.int32, (ROWS, ncp), 1)
    valid = (tok >= col * CMP_STRIDE + (L_CMP - 1)) & (col < n_cmp)
    lc = jnp.where(valid, lc, NEG)
    mx = jnp.max(lc, axis=1, keepdims=True)
    p = jnp.where(valid, jnp.exp(lc - mx), 0.0)
    pc = p / jnp.maximum(jnp.sum(p, axis=1, keepdims=True), 1e-30)
    comb_ref[...] = gates[0] * jnp.dot(pc.astype(BF16), cv_ref[0, 0], preferred_element_type=F32)

    pcs = pc[0:QB] + pc[QB:2 * QB] + pc[2 * QB:3 * QB] + pc[3 * QB:4 * QB]
    hi = pcs.astype(BF16)
    lo = (pcs - hi.astype(F32)).astype(BF16)
    imp = _nt(ovl_ref[...], hi) + _nt(ovl_ref[...], lo)
    blk = lax.broadcasted_iota(jnp.int32, (n_blk, QB), 0)
    cur = (t0 + lax.broadcasted_iota(jnp.int32, (n_blk, QB), 1)) // SEL_BLOCK
    forced = (blk == 0) | (blk == cur) | (blk == cur - 1)
    score = jnp.where(blk > cur, -1.0, jnp.where(forced, FORCE, imp))
    chunks = [score[r:r + SUBLANES] for r in range(0, n_blk, SUBLANES)]
    sub = lax.broadcasted_iota(jnp.int32, (SUBLANES, QB), 0)
    cnts = [jnp.zeros((SUBLANES, QB), F32) for _ in chunks]
    for m in range(n_blk):
        row = jnp.broadcast_to(score[m:m + 1, :], (SUBLANES, QB))
        for r, ch in enumerate(chunks):
            first = r * SUBLANES
            if first > m:
                beats = jnp.where(row >= ch, 1.0, 0.0)
            elif first + SUBLANES - 1 < m:
                beats = jnp.where(row > ch, 1.0, 0.0)
            else:
                beats = jnp.where(sub + first > m, jnp.where(row >= ch, 1.0, 0.0), jnp.where(row > ch, 1.0, 0.0))
            cnts[r] = cnts[r] + beats
    cnt = jnp.concatenate(cnts, axis=0)
    sel_neg = jnp.where(cnt < float(min(N_SELECT, n_blk)), 0.0, NEG)
    pieces = [jnp.zeros((HEAD_DIM, QB), F32), sel_neg]
    if n_blk < HEAD_DIM:
        pieces.append(jnp.zeros((HEAD_DIM - n_blk, QB), F32))
    placed_t = jnp.concatenate(pieces, axis=0).astype(BF16)
    placed = _nt(eye_ref[...], placed_t)
    qa_ref[...] = (q3 + jnp.concatenate([placed] * HPG, axis=0)).astype(BF16)

    m_ref[...] = jnp.full((ROWS, LANES), M_INIT, F32)
    acc_ref[...] = jnp.zeros((ROWS, LANES), F32)
    n_bias = tbs_ref.shape[1]

    def body(kk, carry):
        off = pl.multiple_of(kk * (SEL_SPAN * QB), SEL_SPAN * QB)
        s = _nt(qa_ref[...], ks_ref[pl.ds(off, SEL_SPAN * QB), :])
        parts = []
        for u in range(SEL_SPAN):
            idx = jnp.clip(qb - (kk * SEL_SPAN + u), -1, n_bias - 2) + 1
            parts.append(s[:, u * QB:(u + 1) * QB] + tbs_ref[:, pl.ds(idx, 1)].reshape(ROWS, LANES))
        tile_max = functools.reduce(jnp.maximum, parts)
        m_old = m_ref[...]
        m_new = jnp.maximum(m_old, jnp.max(tile_max, axis=1, keepdims=True))
        alpha = jnp.exp(m_old - m_new)
        pr = jnp.concatenate([jnp.exp(x - m_new).astype(BF16) for x in parts], axis=1)
        acc_ref[...] = alpha * acc_ref[...] + jnp.dot(pr, vs_ref[pl.ds(off, SEL_SPAN * QB), :],
                                                      preferred_element_type=F32)
        m_ref[...] = m_new
        return carry

    lax.fori_loop(0, qb // SEL_SPAN + 1, body, 0)
    acc = acc_ref[...]
    comb_ref[...] += gates[1] * (acc / pltpu.roll(acc, HEAD_DIM, axis=1))

    qw = (q3 + jnp.where(lane_row == HEAD_DIM, NEG, 0.0)).astype(BF16)
    w_off = pl.multiple_of(t0, QB)
    sw = _nt(qw, kw_ref[pl.ds(w_off, WINDOW + QB), :]) + tbw_ref[...].reshape(ROWS, WINDOW + QB)
    pw = jnp.exp(sw - jnp.max(sw, axis=1, keepdims=True))
    accw = jnp.dot(pw.astype(BF16), vw_ref[pl.ds(w_off, WINDOW + QB), :], preferred_element_type=F32)
    comb_ref[...] += gates[2] * (accw / pltpu.roll(accw, HEAD_DIM, axis=1))

    comb = comb_ref[...]
    lane = lax.broadcasted_iota(jnp.int32, (QB, LANES), 1)
    for half in range(HPG // 2):
        a = comb[(2 * half) * QB:(2 * half + 1) * QB]
        b = comb[(2 * half + 1) * QB:(2 * half + 2) * QB]
        o_ref[0, :, half * LANES:(half + 1) * LANES] = jnp.where(lane < HEAD_DIM, a,
                                                                 pltpu.roll(b, HEAD_DIM, axis=1)).astype(o_ref.dtype)


def rel_bias_tables(rel_bias, seq):
    max_d = (N_SEL_BIAS + 1) * QB
    tab = rel_bias.astype(F32)[rel_bucket(jnp.arange(max_d))]
    i = np.arange(QB)[:, None]
    j = np.arange(QB)[None, :]

    def tiles(ks, max_valid):
        d = np.asarray(ks)[:, None, None] * QB + (i - j)[None]
        ok = (d >= 0) if max_valid is None else ((d >= 0) & (d <= max_valid))
        t = tab[np.clip(d, 0, max_d - 1)]
        t = jnp.where(jnp.asarray(ok)[..., None], t, NEG)
        return jnp.transpose(t, (3, 0, 1, 2))

    tbs = tiles(range(-1, N_SEL_BIAS), None)
    tbw = tiles(range(N_WIN_TILES - 1, -1, -1), WINDOW)
    tbw = jnp.transpose(tbw, (0, 2, 1, 3)).reshape(N_HEADS, QB, N_WIN_TILES * QB)
    nqb = seq // QB
    ncp = seq // CMP_STRIDE
    per_qb = QB // CMP_STRIDE
    width = ncp + per_qb * (nqb - 1)
    d = i - CMP_STRIDE * np.arange(width)[None, :] + QB * (nqb - 1) - (L_CMP - 1)
    v = jnp.transpose(tab[np.clip(d, 0, max_d - 1)], (2, 0, 1))
    bc = jnp.stack([v[:, :, per_qb * (nqb - 1 - b):per_qb * (nqb - 1 - b) + ncp] for b in range(nqb)])
    return tbs, tbw, bc


def nsa_prompt_pallas(q, ck, cv, kv_sel, kv_win, gn, tables):
    B, S = q.shape[:2]
    assert S % (SEL_SPAN * QB) == 0 and S % KV_CHUNK == 0 and S // SEL_BLOCK <= HEAD_DIM
    tbs, tbw, bc = tables
    nqb = S // QB
    ncp = S // CMP_STRIDE
    n_cmp = ncp - 1
    n_blk = S // SEL_BLOCK
    c0 = np.arange(ncp)[None, :] * CMP_STRIDE
    s0 = np.arange(n_blk)[:, None] * SEL_BLOCK
    ovl_t = ((c0 <= s0 + SEL_BLOCK - 1) & (c0 + L_CMP - 1 >= s0) & (np.arange(ncp)[None, :] < n_cmp))
    ovl_t = jnp.asarray(ovl_t.astype(np.float32), BF16)
    eye = jnp.asarray(np.eye(QB, dtype=np.float32), BF16)
    gw = HPG * HEAD_DIM
    pq = np.zeros((gw, HPG * LANES), np.float32)
    pq[np.arange(gw), (np.arange(gw) // HEAD_DIM) * LANES + np.arange(gw) % HEAD_DIM] = 1.0
    pk = np.zeros((KV_HEADS, KV_ROW, LANES), np.float32)
    pv = np.zeros((KV_HEADS, KV_ROW, LANES), np.float32)
    for g in range(KV_HEADS):
        pk[g, g * HEAD_DIM + np.arange(HEAD_DIM), np.arange(HEAD_DIM)] = 1.0
        pv[g, (KV_HEADS + g) * HEAD_DIM + np.arange(HEAD_DIM), np.arange(HEAD_DIM)] = 1.0
    onehot = np.zeros((S, LANES), np.float32)
    onehot[np.arange(S), HEAD_DIM + np.arange(S) // SEL_BLOCK] = 1.0

    kv_spec = pl.BlockSpec((1, S, KV_ROW), lambda b, g, i: (b, 0, 0))
    cmp_spec = pl.BlockSpec((1, 1, ncp, LANES), lambda b, g, i: (b, g, 0, 0))
    const2 = lambda b, g, i: (0, 0)
    return pl.pallas_call(
        functools.partial(_nsa_prompt_kernel, n_cmp=n_cmp, n_blk=n_blk),
        grid=(B, KV_HEADS, nqb),
        in_specs=[
            pl.BlockSpec((1, QB, gw), lambda b, g, i: (b, i, g)),
            cmp_spec, cmp_spec, kv_spec, kv_spec,
            pl.BlockSpec((1, QB, LANES), lambda b, g, i: (b, i, g)),
            pl.BlockSpec((1, HPG, QB, ncp), lambda b, g, i: (i, g, 0, 0)),
            pl.BlockSpec((HPG, N_SEL_BIAS + 1, QB, QB), lambda b, g, i: (g, 0, 0, 0)),
            pl.BlockSpec((HPG, QB, N_WIN_TILES * QB), lambda b, g, i: (g, 0, 0)),
            pl.BlockSpec((n_blk, ncp), const2),
            pl.BlockSpec((QB, QB), const2),
            pl.BlockSpec((gw, HPG * LANES), const2),
            pl.BlockSpec((1, KV_ROW, LANES), lambda b, g, i: (g, 0, 0)),
            pl.BlockSpec((1, KV_ROW, LANES), lambda b, g, i: (g, 0, 0)),
            pl.BlockSpec((S, LANES), const2),
        ],
        out_specs=pl.BlockSpec((1, QB, gw), lambda b, g, i: (b, i, g)),
        out_shape=jax.ShapeDtypeStruct((B, S, N_HEADS * HEAD_DIM), BF16),
        scratch_shapes=[pltpu.VMEM((ROWS, LANES), F32), pltpu.VMEM((ROWS, LANES), F32),
                        pltpu.VMEM((ROWS, LANES), BF16), pltpu.VMEM((ROWS, LANES), F32),
                        pltpu.VMEM((S, LANES), BF16), pltpu.VMEM((S, LANES), BF16),
                        pltpu.VMEM((S + WINDOW, LANES), BF16), pltpu.VMEM((S + WINDOW, LANES), BF16)],
        compiler_params=pltpu.CompilerParams(dimension_semantics=("parallel", "parallel", "arbitrary"),
                                             vmem_limit_bytes=VMEM_LIMIT_BYTES),
        name="nsa_prompt",
    )(q, ck, cv, kv_sel, kv_win, gn, bc, tbs, tbw, ovl_t, eye, jnp.asarray(pq, BF16), jnp.asarray(pk, BF16),
      jnp.asarray(pv, BF16), jnp.asarray(onehot, BF16))


STEP_ROWS = SUBLANES
STAGE_PAGES = 8
RANK_LANES = 2 * LANES


def _split3(x):
    hi = x.astype(BF16)
    r1 = x - hi.astype(F32)
    mid = r1.astype(BF16)
    return hi, mid, (r1 - mid.astype(F32)).astype(BF16)


def _nsa_step_kernel(*refs, n_pages, n_cmp):
    pages = refs[1:1 + n_pages]
    (q_ref, ck_ref, cv_ref, kvs_ref, kvw_ref, win_ref, gn_ref, bcs_ref, bsel_ref, bwin_ref, ovl_ref, oh_ref,
     pq_ref, pk_ref, pv_ref, o_ref, kc_ref, vs_ref, kw_ref, vw_ref) = refs[1 + n_pages:]
    page = pages[0].shape[1]
    past = n_pages * page
    wb = win_ref.shape[1]
    ncp = ck_ref.shape[2]
    lane_row = lax.broadcasted_iota(jnp.int32, (1, LANES), 1)
    ones_hi = jnp.where(lane_row >= HEAD_DIM, 1.0, 0.0)
    ones_hi2 = jnp.concatenate([ones_hi] * KV_HEADS, axis=1)
    first_row = lax.broadcasted_iota(jnp.int32, (LANES, 1), 0) == 0

    @pl.when(pl.program_id(0) == 0)
    def _():
        for g in range(KV_HEADS):
            kc_ref[g, 0:past, LANES:2 * LANES] = oh_ref[...]
            kc_ref[g, past:past + LANES, LANES:2 * LANES] = jnp.zeros((LANES, LANES), BF16)

    def place(rows_bf16, r0, nrows, k_dst, v_dst):
        k2 = jnp.dot(rows_bf16, pk_ref[...], preferred_element_type=F32)
        v2 = jnp.dot(rows_bf16, pv_ref[...], preferred_element_type=F32) + ones_hi2
        for g in range(KV_HEADS):
            k_dst[g, r0:r0 + nrows, 0:LANES] = k2[:, g * LANES:(g + 1) * LANES].astype(BF16)
            v_dst[g, r0:r0 + nrows, :] = v2[:, g * LANES:(g + 1) * LANES].astype(BF16)

    for c in range(n_pages // STAGE_PAGES):
        blk = jnp.concatenate([pg[0] for pg in pages[c * STAGE_PAGES:(c + 1) * STAGE_PAGES]], axis=0).astype(BF16)
        place(blk, c * STAGE_PAGES * page, STAGE_PAGES * page, kc_ref, vs_ref)
    place(jnp.where(first_row, kvs_ref[0], 0.0).astype(BF16), past, LANES, kc_ref, vs_ref)
    place(win_ref[0].astype(BF16), 0, wb, kw_ref, vw_ref)
    place(jnp.where(first_row, kvw_ref[0], 0.0).astype(BF16), wb, LANES, kw_ref, vw_ref)

    sig = jax.nn.sigmoid(gn_ref[0])
    lane = lax.broadcasted_iota(jnp.int32, (1, LANES), 1)
    rr = lax.broadcasted_iota(jnp.int32, (RANK_LANES, RANK_LANES), 0)
    cc = lax.broadcasted_iota(jnp.int32, (RANK_LANES, RANK_LANES), 1)
    ones_sq = jnp.ones((RANK_LANES, RANK_LANES), BF16)
    pad_rows = jnp.zeros((STEP_ROWS - HPG, LANES), F32)
    for g in range(KV_HEADS):
        gw = HPG * HEAD_DIM
        q4 = jnp.dot(q_ref[0][:, g * gw:(g + 1) * gw], pq_ref[...], preferred_element_type=F32)
        q8 = jnp.concatenate([q4[:, j * LANES:(j + 1) * LANES] for j in range(HPG)] + [pad_rows], axis=0)
        gates = [jnp.concatenate([sig[:, g * LANES + 3 * j + c:g * LANES + 3 * j + c + 1] for j in range(HPG)]
                                 + [pad_rows[:, 0:1]], axis=0) for c in range(3)]

        lc = _nt(q8.astype(BF16), ck_ref[0, g]) + bcs_ref[g]
        col = lax.broadcasted_iota(jnp.int32, (STEP_ROWS, ncp), 1)
        valid = col < n_cmp
        lc = jnp.where(valid, lc, NEG)
        p = jnp.where(valid, jnp.exp(lc - jnp.max(lc, axis=1, keepdims=True)), 0.0)
        pc = p / jnp.maximum(jnp.sum(p, axis=1, keepdims=True), 1e-30)
        comb = gates[0] * jnp.dot(pc.astype(BF16), cv_ref[0, g], preferred_element_type=F32)

        pcs = jnp.broadcast_to(jnp.sum(pc[0:HPG], axis=0, keepdims=True), (STEP_ROWS, ncp))
        hi = pcs.astype(BF16)
        lo = (pcs - hi.astype(F32)).astype(BF16)
        imp = (jnp.dot(hi, ovl_ref[...], preferred_element_type=F32)
               + jnp.dot(lo, ovl_ref[...], preferred_element_type=F32))[0:1]
        blk_id = lax.broadcasted_iota(jnp.int32, (1, RANK_LANES), 1)
        cur = past // SEL_BLOCK
        forced = (blk_id == 0) | (blk_id == cur) | (blk_id == cur - 1)
        score = jnp.where(blk_id > cur, -1.0, jnp.where(forced, FORCE, imp))
        s_n = jnp.broadcast_to(score, (RANK_LANES, RANK_LANES))
        diag = jnp.where(rr == cc, s_n, 0.0)
        s_m = sum(_nt(part, ones_sq) for part in _split3(diag))
        beats = jnp.where(rr < cc, jnp.where(s_m >= s_n, 1.0, 0.0), jnp.where(s_m > s_n, 1.0, 0.0))
        rank = jnp.sum(beats, axis=0, keepdims=True)
        sel_neg = jnp.where(rank < float(N_SELECT), 0.0, NEG)

        qa = jnp.concatenate([q8, jnp.broadcast_to(sel_neg[:, :LANES], (STEP_ROWS, LANES))], axis=1).astype(BF16)
        s_past = _nt(qa, kc_ref[g, 0:past, :]) + bsel_ref[g, :, 0:past]
        s_new = (_nt(qa, kc_ref[g, past:past + LANES, :]) + bsel_ref[g, :, past:past + LANES]
                 + sel_neg[:, LANES:LANES + 1])
        mx = jnp.maximum(jnp.max(s_past, axis=1, keepdims=True), jnp.max(s_new, axis=1, keepdims=True))
        acc = (jnp.dot(jnp.exp(s_past - mx).astype(BF16), vs_ref[g, 0:past, :], preferred_element_type=F32)
               + jnp.dot(jnp.exp(s_new - mx).astype(BF16), vs_ref[g, past:past + LANES, :], preferred_element_type=F32))
        comb = comb + gates[1] * (acc / pltpu.roll(acc, HEAD_DIM, axis=1))

        sw = _nt(q8.astype(BF16), kw_ref[g]) + bwin_ref[g]
        pw = jnp.exp(sw - jnp.max(sw, axis=1, keepdims=True))
        accw = jnp.dot(pw.astype(BF16), vw_ref[g], preferred_element_type=F32)
        comb = comb + gates[2] * (accw / pltpu.roll(accw, HEAD_DIM, axis=1))

        for half in range(HPG // 2):
            a = comb[2 * half:2 * half + 1]
            b = comb[2 * half + 1:2 * half + 2]
            o_ref[0, :, g * gw + half * LANES:g * gw + (half + 1) * LANES] = jnp.where(
                lane < HEAD_DIM, a, pltpu.roll(b, HEAD_DIM, axis=1)).astype(o_ref.dtype)


def step_bias_tables(rel_bias, past, wb, ncp):
    max_d = (N_SEL_BIAS + 1) * QB
    tab = rel_bias.astype(F32)[rel_bucket(jnp.arange(max_d))]

    def rows(dist, ok):
        t = jnp.where(jnp.asarray(ok)[:, None], tab[np.clip(dist, 0, max_d - 1)], NEG)
        t = jnp.transpose(t).reshape(KV_HEADS, HPG, -1)
        return jnp.pad(t, ((0, 0), (0, STEP_ROWS - HPG), (0, 0)))

    d_cmp = past - (np.arange(ncp) * CMP_STRIDE + L_CMP - 1)
    d_sel = past - np.arange(past + LANES)
    d_win = past - (past - wb + np.arange(wb + LANES))
    return (rows(d_cmp, d_cmp >= 0), rows(d_sel, d_sel >= 0), rows(d_win, (d_win >= 0) & (d_win <= WINDOW)))


def nsa_step(q, ck, cv, kvs_new, kvw_new, sel_pool, win_buf, gn, page_table, tables):
    B, n_pages = page_table.shape
    page = sel_pool.shape[1]
    past = n_pages * page
    wb = win_buf.shape[1]
    ncp = ck.shape[2]
    n_blk = past // SEL_BLOCK + 1
    assert n_pages % STAGE_PAGES == 0 and n_blk <= LANES + 1 and page == LANES and wb % SUBLANES == 0
    bcs, bsel, bwin = tables
    c0 = np.arange(ncp)[:, None] * CMP_STRIDE
    s0 = np.arange(RANK_LANES)[None, :] * SEL_BLOCK
    ovl = ((c0 <= s0 + SEL_BLOCK - 1) & (c0 + L_CMP - 1 >= s0) & (np.arange(ncp)[:, None] < ncp - 1)
           & (np.arange(RANK_LANES)[None, :] < n_blk))
    onehot = np.zeros((past, LANES), np.float32)
    onehot[np.arange(past), np.arange(past) // SEL_BLOCK] = 1.0
    gw = HPG * HEAD_DIM
    pq = np.zeros((gw, HPG * LANES), np.float32)
    pq[np.arange(gw), (np.arange(gw) // HEAD_DIM) * LANES + np.arange(gw) % HEAD_DIM] = 1.0
    pk = np.zeros((KV_ROW, KV_HEADS * LANES), np.float32)
    pv = np.zeros((KV_ROW, KV_HEADS * LANES), np.float32)
    for g in range(KV_HEADS):
        pk[g * HEAD_DIM + np.arange(HEAD_DIM), g * LANES + np.arange(HEAD_DIM)] = 1.0
        pv[(KV_HEADS + g) * HEAD_DIM + np.arange(HEAD_DIM), g * LANES + np.arange(HEAD_DIM)] = 1.0
    consts = [jnp.asarray(ovl.astype(np.float32), BF16), jnp.asarray(onehot, BF16), jnp.asarray(pq, BF16),
              jnp.asarray(pk, BF16), jnp.asarray(pv, BF16)]

    row3 = lambda a: a.reshape(B, 1, a.shape[-1])
    per_b = lambda shape: pl.BlockSpec((1,) + shape, lambda b, pt: (b,) + (0,) * len(shape))
    full = lambda a: pl.BlockSpec(a.shape, lambda b, pt: (0,) * a.ndim)
    page_specs = [pl.BlockSpec((1, page, KV_ROW), lambda b, pt, k=k: (pt[b, k], 0, 0)) for k in range(n_pages)]
    in_specs = page_specs + [per_b((1, Q_COLS)), per_b((KV_HEADS, ncp, LANES)), per_b((KV_HEADS, ncp, LANES)),
                             per_b((1, KV_ROW)), per_b((1, KV_ROW)), per_b((wb, KV_ROW)), per_b((1, GN_COLS)),
                             full(bcs), full(bsel), full(bwin)] + [full(c) for c in consts]
    grid_spec = pltpu.PrefetchScalarGridSpec(
        num_scalar_prefetch=1, grid=(B,), in_specs=in_specs, out_specs=per_b((1, Q_COLS)),
        scratch_shapes=[pltpu.VMEM((KV_HEADS, past + LANES, 2 * LANES), BF16), pltpu.VMEM((KV_HEADS, past + LANES, LANES), BF16),
                        pltpu.VMEM((KV_HEADS, wb + LANES, LANES), BF16), pltpu.VMEM((KV_HEADS, wb + LANES, LANES), BF16)])
    out = pl.pallas_call(
        functools.partial(_nsa_step_kernel, n_pages=n_pages, n_cmp=ncp - 1),
        grid_spec=grid_spec,
        out_shape=jax.ShapeDtypeStruct((B, 1, Q_COLS), BF16),
        compiler_params=pltpu.CompilerParams(dimension_semantics=("arbitrary",), vmem_limit_bytes=VMEM_LIMIT_BYTES),
        name="nsa_step",
    )(page_table, *([sel_pool] * n_pages), row3(q), ck, cv, row3(kvs_new), row3(kvw_new), win_buf, row3(gn),
      bcs, bsel, bwin, *consts)
    return out.reshape(B, Q_COLS)


def _merge_kernel(ya_ref, yb_ref, yc_ref, ga_ref, gb_ref, gc_ref, x_ref, wa_ref, wb_ref, wc_ref, wo_ref, o_ref):
    def branch(y_ref, g_ref, w_ref):
        return jax.nn.sigmoid(g_ref[...]) * jnp.dot(y_ref[...], w_ref[...], preferred_element_type=F32)

    merged = branch(ya_ref, ga_ref, wa_ref) + branch(yb_ref, gb_ref, wb_ref) + branch(yc_ref, gc_ref, wc_ref)
    o_ref[...] = x_ref[...] + jnp.dot(merged.astype(BF16), wo_ref[...], preferred_element_type=F32)


def merge(ya, yb, yc, za, x, wa, wb, wc, wo):
    T = x.shape[0]
    tm = _pick(T, (512, 256, 128))
    y_spec = pl.BlockSpec((tm, ya.shape[1]), lambda i: (i, 0))
    w_spec = pl.BlockSpec((ya.shape[1], D_MODEL), lambda i: (0, 0))
    return pl.pallas_call(
        _merge_kernel,
        grid=(T // tm,),
        in_specs=[y_spec, y_spec, y_spec,
                  pl.BlockSpec((tm, D_MODEL), lambda i: (i, 0)), pl.BlockSpec((tm, D_MODEL), lambda i: (i, 1)),
                  pl.BlockSpec((tm, D_MODEL), lambda i: (i, 2)),
                  pl.BlockSpec((tm, D_MODEL), lambda i: (i, 0)),
                  w_spec, w_spec, w_spec, pl.BlockSpec((D_MODEL, D_MODEL), lambda i: (0, 0))],
        out_specs=pl.BlockSpec((tm, D_MODEL), lambda i: (i, 0)),
        out_shape=jax.ShapeDtypeStruct((T, D_MODEL), F32),
        compiler_params=pltpu.CompilerParams(dimension_semantics=("parallel",), vmem_limit_bytes=VMEM_LIMIT_BYTES),
        name="merge",
    )(ya, yb, yc, za, za, za, x, wa, wb, wc, wo)


def _ffn_kernel(*refs, routed):
    if routed:
        h_ref, g_ref, wr_ref, br_ref, wg_ref, wu_ref, wd_ref, o_ref, xn_ref, gate_ref = refs
    else:
        h_ref, g_ref, wg_ref, wu_ref, wd_ref, o_ref, xn_ref = refs
    e = pl.program_id(1)
    f = pl.program_id(2)

    @pl.when((e == 0) & (f == 0))
    def _():
        h = h_ref[...]
        xb = _rms_rows(h, g_ref[...]).astype(BF16)
        xn_ref[...] = xb
        o_ref[...] = h
        if routed:
            logits = jnp.dot(xb, wr_ref[...], preferred_element_type=F32) + br_ref[...]
            lane = lax.broadcasted_iota(jnp.int32, logits.shape, 1)
            m1 = jnp.max(logits, axis=1, keepdims=True)
            i1 = jnp.min(jnp.where(logits == m1, lane, LANES), axis=1, keepdims=True)
            rest = jnp.where(lane == i1, M_INIT, logits)
            m2 = jnp.max(rest, axis=1, keepdims=True)
            i2 = jnp.min(jnp.where(rest == m2, lane, LANES), axis=1, keepdims=True)
            r = jnp.exp(m2 - m1)
            gate_ref[...] = jnp.where(lane == i1, 1.0 / (1.0 + r), 0.0) + jnp.where(lane == i2, r / (1.0 + r), 0.0)

    xb = xn_ref[...]
    a = jnp.dot(xb, wg_ref[0], preferred_element_type=F32)
    u = jnp.dot(xb, wu_ref[0], preferred_element_type=F32)
    t = (a * jax.nn.sigmoid(a)) * u
    y = jnp.dot(t.astype(BF16), wd_ref[0], preferred_element_type=F32)
    if routed:
        lane = lax.broadcasted_iota(jnp.int32, gate_ref.shape, 1)
        y = jnp.sum(jnp.where(lane == e, gate_ref[...], 0.0), axis=1, keepdims=True) * y
    o_ref[...] += y


def channel_mixer(h, norm_gain, wg, wu, wd, router=None):
    T = h.shape[0]
    E, _, F = wg.shape
    tm = _pick(T, (512, 256, 128))
    tf = _pick(F, (1408, 1024, 512, 256, 128))
    routed = router is not None
    row = lambda i, e, f: (i, 0)
    in_specs = [pl.BlockSpec((tm, D_MODEL), row), pl.BlockSpec((1, D_MODEL), lambda i, e, f: (0, 0))]
    args = [h, norm_gain.astype(F32)[None]]
    scratch = [pltpu.VMEM((tm, D_MODEL), BF16)]
    if routed:
        in_specs += [pl.BlockSpec((D_MODEL, LANES), lambda i, e, f: (0, 0)), pl.BlockSpec((1, LANES), lambda i, e, f: (0, 0))]
        args += list(router)
        scratch.append(pltpu.VMEM((tm, LANES), F32))
    in_specs += [pl.BlockSpec((1, D_MODEL, tf), lambda i, e, f: (e, 0, f)),
                 pl.BlockSpec((1, D_MODEL, tf), lambda i, e, f: (e, 0, f)),
                 pl.BlockSpec((1, tf, D_MODEL), lambda i, e, f: (e, f, 0))]
    args += [wg, wu, wd]
    return pl.pallas_call(
        functools.partial(_ffn_kernel, routed=routed),
        grid=(T // tm, E, F // tf),
        in_specs=in_specs,
        out_specs=pl.BlockSpec((tm, D_MODEL), row),
        out_shape=jax.ShapeDtypeStruct((T, D_MODEL), F32),
        scratch_shapes=scratch,
        compiler_params=pltpu.CompilerParams(dimension_semantics=("parallel", "arbitrary", "arbitrary"),
                                             vmem_limit_bytes=VMEM_LIMIT_BYTES),
        name="moe" if routed else "ffn",
    )(*args)


def mixer_tail(x2, za, y_a, y_b, y_c, l, P):
    return merge(y_a, y_b, y_c, za, x2, P['w_br_rg'][l].astype(BF16), P['w_br_attn'][l].astype(BF16),
                 P['w_br_pool'][l].astype(BF16), P['w_out'][l].astype(BF16))


def prompt_mixer(x, l, P, packed, tables):
    B, T, _ = x.shape
    x2 = x.reshape(B * T, D_MODEL)
    za, q, kv_c, kv_s, kv_w, gn = projection(x2, P['attn_norm'][l], packed['proj'])
    zeros = lambda rows: jnp.zeros((B, rows, D_RNN), F32)
    y_a, y_c, conv_new, h_last, pool_new = mixer_seq(za, B, T, packed['mix'], zeros(CONV_W - 1), jnp.zeros((B, D_RNN), F32),
                                                     zeros(POOL_MAX - 1), 0)
    ck, cv = compress(kv_c.reshape(B, T, KV_ROW), packed['cmp'])
    y_b = nsa_prompt_pallas(q.reshape(B, T, Q_COLS), ck, cv, kv_s.reshape(B, T, KV_ROW), kv_w.reshape(B, T, KV_ROW),
                            gn.reshape(B, T, GN_COLS), tables)
    out = mixer_tail(x2, za, y_a, y_b.reshape(B * T, Q_COLS), y_c, l, P)
    kv_shape = (B, T, 2, KV_HEADS, HEAD_DIM)
    state = (kv_c.reshape(kv_shape), kv_s.reshape(kv_shape), kv_w.reshape(kv_shape)[:, -min(WINDOW, T):], conv_new,
             h_last[:, 0], pool_new)
    return out.reshape(B, T, D_MODEL), state


def sample_mixer(x, l, P, packed, past_len, conv_state, h0, pool_state, cmp_pool, sel_pool, win_buf, page_table,
                 step_tables):
    B, T, _ = x.shape
    x2 = x.reshape(B * T, D_MODEL)
    za, q, kv_c, kv_s, kv_w, gn = projection(x2, P['attn_norm'][l], packed['proj'])
    y_a, y_c, h_new = mixer_step(za, packed['mix'], conv_state, h0, pool_state, past_len)
    ck, cv = compress(cmp_pool.reshape(cmp_pool.shape[0], PAGE_SIZE, KV_ROW), packed['cmp'], page_table)
    kv_shape = (B, T, 2, KV_HEADS, HEAD_DIM)
    wb = win_buf.shape[1]
    y_b = nsa_step(q, ck, cv, kv_s, kv_w, sel_pool.reshape(sel_pool.shape[0], PAGE_SIZE, KV_ROW),
                   win_buf.reshape(B, wb, KV_ROW), gn, page_table, step_tables)
    win_new = jnp.concatenate([win_buf, kv_w.reshape(kv_shape)], axis=1)[:, -wb:]
    out = mixer_tail(x2, za, y_a, y_b, y_c, l, P)
    conv_new = jnp.concatenate([conv_state[:, 1:], za[:, None, ZA_XRG:ZA_XRG + D_RNN]], axis=1)
    pool_new = jnp.concatenate([pool_state[:, 1:], za[:, None, ZA_XPOOL:ZA_XPOOL + D_POOL]], axis=1)
    state = (kv_c.reshape(kv_shape), kv_s.reshape(kv_shape), win_new, conv_new, h_new, pool_new)
    return out.reshape(B, T, D_MODEL), state


def ffn_layer(x, l, P, W):
    B, T, _ = x.shape
    i = l // 2
    if l % 2 == 0:
        y = channel_mixer(x.reshape(B * T, D_MODEL), P['ffn_norm'][l], W['ffn_g'][i], W['ffn_u'][i], W['ffn_d'][i])
    else:
        y = channel_mixer(x.reshape(B * T, D_MODEL), P['ffn_norm'][l], W['moe_g'][i], W['moe_u'][i], W['moe_d'][i],
                          router=W['router'][i])
    return y.reshape(B, T, D_MODEL)


def kernel(x_prompt, x_sample, cache_cmp_kv, cache_sel_kv, cache_win_kv, state_conv, state_rg_h, state_pool,
           page_table, attn_norm, w_in, conv_w, conv_b, rg_w_a, rg_b_a, rg_w_x, rg_b_x, rg_lambda, q_norm, k_norm,
           cmp_pe, w_cmp1, w_cmp2, rel_bias, w_pool, pool_scale, w_br_rg, w_br_attn, w_br_pool, w_out, ffn_norm,
           ffn_w_gate, ffn_w_up, ffn_w_down, w_router, b_router, moe_w_gate, moe_w_up, moe_w_down):
    P = dict(attn_norm=attn_norm, conv_w=conv_w, conv_b=conv_b, rg_w_a=rg_w_a, rg_b_a=rg_b_a,
             rg_w_x=rg_w_x, rg_b_x=rg_b_x, rg_lambda=rg_lambda, w_pool=w_pool,
             pool_scale=pool_scale, w_br_rg=w_br_rg, w_br_attn=w_br_attn, w_br_pool=w_br_pool, w_out=w_out,
             ffn_norm=ffn_norm)
    depth = w_in.shape[0]
    n_moe = w_router.shape[0]
    pad_e = LANES - N_EXPERTS
    W = dict(
        ffn_g=[w[None].astype(BF16) for w in ffn_w_gate], ffn_u=[w[None].astype(BF16) for w in ffn_w_up],
        ffn_d=[w[None].astype(BF16) for w in ffn_w_down],
        moe_g=[w.astype(BF16) for w in moe_w_gate], moe_u=[w.astype(BF16) for w in moe_w_up],
        moe_d=[w.astype(BF16) for w in moe_w_down],
        router=[(jnp.pad(w_router[i], ((0, 0), (0, pad_e))).astype(BF16),
                 jnp.pad(b_router[i].astype(F32), (0, pad_e), constant_values=NEG)[None]) for i in range(n_moe)])
    past_len = page_table.shape[1] * PAGE_SIZE
    y_p, y_s = x_prompt, x_sample
    tables = rel_bias_tables(rel_bias, x_prompt.shape[1])
    step_tables = step_bias_tables(rel_bias, past_len, cache_win_kv.shape[2], past_len // CMP_STRIDE)
    p_list, s_list = [], []
    for l in range(depth):
        packed = dict(proj=pack_projection(w_in[l], q_norm[l], k_norm[l, 1], k_norm[l, 2]),
                      cmp=pack_compress(w_cmp1[l], w_cmp2[l], cmp_pe[l], k_norm[l, 0]),
                      mix=pack_mixer(conv_w[l], conv_b[l], rg_w_a[l], rg_b_a[l], rg_w_x[l], rg_b_x[l], rg_lambda[l],
                                     w_pool[l], pool_scale[l]))
        y_p, st_p = prompt_mixer(y_p, l, P, packed, tables)
        y_p = ffn_layer(y_p, l, P, W)
        p_list.append(st_p)
        y_s, st_s = sample_mixer(y_s, l, P, packed, past_len, state_conv[l], state_rg_h[l], state_pool[l],
                                 cache_cmp_kv[l], cache_sel_kv[l], cache_win_kv[l], page_table, step_tables)
        y_s = ffn_layer(y_s, l, P, W)
        s_list.append(st_s)
    p_cmp_kv, p_sel_kv, p_win_kv, p_conv, p_h, p_pool = [jnp.stack(a) for a in zip(*p_list)]
    s_cmp_kv, s_sel_kv, s_win_kv, s_conv, s_h, s_pool = [jnp.stack(a) for a in zip(*s_list)]
    return (y_p, y_s, p_cmp_kv, p_sel_kv, p_win_kv, p_conv, p_h, p_pool,
            s_cmp_kv, s_sel_kv, s_win_kv, s_conv, s_h, s_pool)
```

```python
import math
import functools

import jax
import jax.numpy as jnp
import numpy as np
from jax import lax
from jax.experimental import pallas as pl
from jax.experimental.pallas import tpu as pltpu

D_MODEL = 1024
PAGE_SIZE = 128
F32 = jnp.float32
BF16 = jnp.bfloat16
EPS = 1e-6
NEG = -1e30
FORCE = 1e4
D_RNN = 512
RG_BLOCKS = 8
RG_BW = D_RNN // RG_BLOCKS
CONV_W = 4
RG_C = 8.0
N_HEADS = 8
KV_HEADS = 2
HPG = N_HEADS // KV_HEADS
HEAD_DIM = 64
L_CMP = 32
CMP_STRIDE = 16
CMP_HIDDEN = 256
SEL_BLOCK = 64
N_SELECT = 16
WINDOW = 512
Q_BLOCK = 128
D_POOL = 512
POOL_WINDOWS = (2, 4, 8, 16)
POOL_GROUPS = 4
POOL_GW = D_POOL // POOL_GROUPS
POOL_MAX = 16
REL_BUCKETS = 32
REL_MAX_DIST = 1024
N_EXPERTS = 8
TOP_K = 2
KV_ROW = 2 * KV_HEADS * HEAD_DIM
SPLITS = (D_RNN, D_RNN, N_HEADS * HEAD_DIM, KV_ROW, KV_ROW, KV_ROW, 3 * N_HEADS, D_POOL, 3 * D_MODEL)

VMEM_LIMIT_BYTES = 52 * 1024 * 1024
LANES = 128
SUBLANES = 8
M_INIT = -3e38


def _pick(n, cands):
    for c in cands:
        if n % c == 0:
            return c
    return n


def _nt(a, b):
    return lax.dot_general(a, b, (((1,), (1,)), ((), ())), preferred_element_type=F32)


def _rms_rows(x, g):
    return x * lax.rsqrt(jnp.mean(x * x, axis=-1, keepdims=True) + EPS) * g


def rms_norm(x, g):
    xf = x.astype(F32)
    y = xf * lax.rsqrt(jnp.mean(xf * xf, axis=-1, keepdims=True) + EPS)
    return (y * g.astype(F32)).astype(x.dtype)


def rel_bucket(dist):
    n_exact = REL_BUCKETS // 2
    d = jnp.maximum(dist, 0)
    df = jnp.maximum(d, 1).astype(F32)
    large = n_exact + (jnp.log(df / n_exact) / math.log(REL_MAX_DIST / n_exact)
                       * (REL_BUCKETS - n_exact)).astype(jnp.int32)
    return jnp.where(d < n_exact, d, jnp.minimum(large, REL_BUCKETS - 1))


def masked_probs(logits, valid):
    logits = jnp.where(valid, logits, NEG)
    m = jnp.max(logits, axis=-1, keepdims=True)
    p = jnp.where(valid, jnp.exp(logits - m), 0.0)
    return p / jnp.maximum(jnp.sum(p, axis=-1, keepdims=True), 1e-30)


def causal_conv(x, buf, w, b):
    xp = jnp.concatenate([buf.astype(x.dtype), x], axis=1)
    y = lax.conv_general_dilated(xp, w[:, None, :].astype(x.dtype), (1,), 'VALID',
                                 dimension_numbers=('NWC', 'WIO', 'NWC'),
                                 feature_group_count=x.shape[-1]) + b
    return y, xp[:, -(CONV_W - 1):]


def rg_lru(xc, h0, w_a, b_a, w_x, b_x, lam):
    B, T, _ = xc.shape
    xb = xc.reshape(B, T, RG_BLOCKS, RG_BW)
    r = jax.nn.sigmoid((jnp.einsum('btnc,ncd->btnd', xb, w_a).reshape(B, T, D_RNN) + b_a).astype(F32))
    i = jax.nn.sigmoid((jnp.einsum('btnc,ncd->btnd', xb, w_x).reshape(B, T, D_RNN) + b_x).astype(F32))
    log_a = -RG_C * r * jax.nn.softplus(-lam.astype(F32))
    a = jnp.exp(log_a)
    u = jnp.sqrt(-jnp.expm1(2.0 * log_a)) * (i * xc.astype(F32))
    u = u.at[:, 0].add(a[:, 0] * h0.astype(F32))

    def combine(lhs, rhs):
        a1, b1 = lhs
        a2, b2 = rhs
        return a1 * a2, a2 * b1 + b2

    _, h = lax.associative_scan(combine, (a, u), axis=1)
    return h, h[:, -1]


def pool_mix(xin, buf, start_pos, w_pool, scale):
    B, T, C = xin.shape
    xf = jnp.concatenate([buf.astype(xin.dtype), xin], axis=1).astype(F32)
    cs = jnp.concatenate([jnp.zeros((B, 1, C), F32), jnp.cumsum(xf, axis=1)], axis=1)
    pos = start_pos + jnp.arange(T)
    means = []
    for g, w in enumerate(POOL_WINDOWS):
        sl = slice(g * POOL_GW, (g + 1) * POOL_GW)
        s = cs[:, POOL_MAX:POOL_MAX + T, sl] - cs[:, POOL_MAX - w:POOL_MAX - w + T, sl]
        cnt = jnp.minimum(pos + 1, w).astype(F32)[None, :, None]
        means.append(s / cnt)
    mixed = (jnp.concatenate(means, axis=-1) - xf[:, POOL_MAX - 1:]).astype(xin.dtype)
    y = jnp.einsum('btgc,gcd->btgd', mixed.reshape(B, T, POOL_GROUPS, POOL_GW), w_pool)
    return y.reshape(B, T, D_POOL) * scale, xf[:, -(POOL_MAX - 1):].astype(xin.dtype)


def compress_kv(rows, w1, w2, pe, kn):
    Bx, T = rows.shape[:2]
    n_ch = T // CMP_STRIDE
    ch = rows[:, :n_ch * CMP_STRIDE].reshape(Bx, n_ch, CMP_STRIDE, 2, KV_HEADS, HEAD_DIM)
    first = jnp.einsum('bnsegd,esdf->bnegf', ch, w1[:, :CMP_STRIDE])
    second = jnp.einsum('bnsegd,esdf->bnegf', ch, w1[:, CMP_STRIDE:])
    pe_term = jnp.einsum('led,eldf->ef', pe, w1)
    h = first[:, :-1] + second[:, 1:] + pe_term[:, None, :]
    comp = jnp.einsum('bnegf,efd->bnegd', jax.nn.gelu(h), w2)
    nc = comp.shape[1]
    comp_end = jnp.arange(nc) * CMP_STRIDE + L_CMP - 1
    return rms_norm(comp[:, :, 0], kn), comp[:, :, 1], comp_end


def overlap_matrix(n_cmp, n_sel):
    c0 = jnp.arange(n_cmp)[:, None] * CMP_STRIDE
    s0 = jnp.arange(n_sel)[None, :] * SEL_BLOCK
    return ((c0 <= s0 + SEL_BLOCK - 1) & (c0 + L_CMP - 1 >= s0)).astype(F32)


def nsa_attend(q, q_pos, comp_k, comp_v, comp_end, overlap, fetch_sel, win_k, win_v, win_pos, gates, rel_bias):
    Bq, Tq = q.shape[:2]
    scale = HEAD_DIM ** -0.5
    qg = q.reshape(Bq, Tq, KV_HEADS, HPG, HEAD_DIM)
    rb = rel_bias.astype(F32)

    def head_bias(buckets):
        return jnp.moveaxis(rb[buckets], -1, 1).reshape(Tq, KV_HEADS, HPG, -1)

    dist_c = q_pos[:, None] - comp_end[None, :]
    lc = jnp.einsum('btgjd,bcgd->btgjc', qg, comp_k, preferred_element_type=F32) * scale + head_bias(rel_bucket(dist_c))
    pc = masked_probs(lc, (dist_c >= 0)[None, :, None, None, :])
    o_cmp = jnp.einsum('btgjc,bcgd->btgjd', pc, comp_v)
    ns = overlap.shape[1]
    imp = jnp.einsum('btgjc,cn->btgn', pc, overlap)
    blk = jnp.arange(ns)
    cur = (q_pos // SEL_BLOCK)[None, :, None, None]
    forced = (blk == 0) | (blk == cur) | (blk == cur - 1)
    score = jnp.where(blk > cur, -1.0, jnp.where(forced, FORCE, imp))
    n_sel = min(N_SELECT, ns)
    _, idx = lax.top_k(score, n_sel)
    kv_s = fetch_sel(idx)
    k_s = kv_s[..., 0, :].reshape(Bq, Tq, KV_HEADS, n_sel * SEL_BLOCK, HEAD_DIM)
    v_s = kv_s[..., 1, :].reshape(Bq, Tq, KV_HEADS, n_sel * SEL_BLOCK, HEAD_DIM)
    pos_s = (idx[..., None] * SEL_BLOCK + jnp.arange(SEL_BLOCK)).reshape(Bq, Tq, KV_HEADS, -1)
    dist_s = q_pos[None, :, None, None] - pos_s
    bias_s = rb.reshape(REL_BUCKETS, KV_HEADS, HPG)[rel_bucket(dist_s), jnp.arange(KV_HEADS)[:, None]]
    ls = jnp.einsum('btgjd,btgkd->btgjk', qg, k_s, preferred_element_type=F32) * scale + jnp.moveaxis(bias_s, -1, 3)
    ps = masked_probs(ls, (dist_s >= 0)[:, :, :, None, :])
    o_sel = jnp.einsum('btgjk,btgkd->btgjd', ps, v_s)
    dist_w = q_pos[:, None] - win_pos[None, :]
    lw = jnp.einsum('btgjd,bwgd->btgjw', qg, win_k, preferred_element_type=F32) * scale + head_bias(rel_bucket(dist_w))
    valid_w = (dist_w >= 0) & (dist_w <= WINDOW) & (win_pos[None, :] >= 0)
    pw = masked_probs(lw, valid_w[None, :, None, None, :])
    o_win = jnp.einsum('btgjw,bwgd->btgjd', pw, win_v)
    g = jax.nn.sigmoid(gates.astype(F32)).reshape(Bq, Tq, KV_HEADS, HPG, 3)
    o = g[..., 0:1] * o_cmp + g[..., 1:2] * o_sel + g[..., 2:3] * o_win
    return o.reshape(Bq, Tq, N_HEADS * HEAD_DIM).astype(q.dtype)


def nsa_sample(q, comp_k, comp_v, kv_sel_new, kv_win_new, gates, *, sel_pool, win_buf, page_table, rel_bias):
    DB, DS = q.shape[:2]
    n_pages = page_table.shape[1]
    past = n_pages * PAGE_SIZE
    q_pos = past + jnp.arange(DS)
    comp_end = jnp.arange(comp_k.shape[1]) * CMP_STRIDE + L_CMP - 1
    total = past + DS
    ns = -(-total // SEL_BLOCK)
    nbp = past // SEL_BLOCK
    nnb = ns - nbp
    bpp = PAGE_SIZE // SEL_BLOCK
    ovl = overlap_matrix(comp_k.shape[1], ns)
    pool_blocks = sel_pool.reshape(-1, SEL_BLOCK, 2, KV_HEADS, HEAD_DIM)
    new_pad = jnp.pad(kv_sel_new.astype(sel_pool.dtype), ((0, 0), (0, nnb * SEL_BLOCK - DS), (0, 0), (0, 0), (0, 0)))
    new_blocks = new_pad.reshape(DB, nnb, SEL_BLOCK, 2, KV_HEADS, HEAD_DIM)
    b_idx = jnp.arange(DB)[:, None, None, None]
    g_idx = jnp.arange(KV_HEADS)[None, None, :, None]

    def fetch(idx):
        ip = jnp.minimum(idx, nbp - 1)
        phys = page_table[b_idx, ip // bpp] * bpp + ip % bpp
        from_past = pool_blocks[phys, :, :, g_idx]
        from_new = new_blocks[b_idx, jnp.clip(idx - nbp, 0, nnb - 1), :, :, g_idx]
        return jnp.where((idx >= nbp)[..., None, None, None], from_new, from_past)

    wb = win_buf.shape[1]
    win = jnp.concatenate([win_buf, kv_win_new.astype(win_buf.dtype)], axis=1)
    win_pos = past - wb + jnp.arange(wb + DS)
    out = nsa_attend(q, q_pos, comp_k, comp_v, comp_end, ovl, fetch, win[:, :, 0], win[:, :, 1], win_pos, gates, rel_bias)
    return out, win[:, -wb:]


PROJ_TN = 512
ZA_COLS = 3 * D_MODEL + 2 * D_RNN + D_POOL
N_ZA = ZA_COLS // PROJ_TN
Q_COLS = N_HEADS * HEAD_DIM
GN_COLS = 2 * LANES
PROJ_COLS = ZA_COLS + Q_COLS + 3 * KV_ROW + GN_COLS
ZA_XRG, ZA_GRG, ZA_XPOOL = 3 * D_MODEL, 3 * D_MODEL + D_RNN, 3 * D_MODEL + 2 * D_RNN


def _proj_kernel(x_ref, g_ref, w_ref, seg_ref, ng_ref, nm_ref, za_ref, q_ref, kvc_ref, kvs_ref, kvw_ref, gn_ref, xn_ref):
    j = pl.program_id(1)

    @pl.when(j == 0)
    def _():
        xn_ref[...] = _rms_rows(x_ref[...], g_ref[...]).astype(BF16)

    acc = jnp.dot(xn_ref[...], w_ref[...], preferred_element_type=F32)

    @pl.when(j < N_ZA)
    def _():
        za_ref[...] = acc

    @pl.when(j >= N_ZA)
    def _():
        sq = acc * acc
        hi = sq.astype(BF16)
        lo = (sq - hi.astype(F32)).astype(BF16)
        ss = (jnp.dot(hi, seg_ref[...], preferred_element_type=F32)
              + jnp.dot(lo, seg_ref[...], preferred_element_type=F32))
        normed = acc * lax.rsqrt(ss * (1.0 / HEAD_DIM) + EPS) * ng_ref[...]
        y = jnp.where(nm_ref[...] > 0.5, normed, acc)

        @pl.when(j == N_ZA)
        def _():
            q_ref[...] = y.astype(BF16)

        @pl.when(j == N_ZA + 1)
        def _():
            kvc_ref[...] = y[:, :KV_ROW]
            kvs_ref[...] = y[:, KV_ROW:]

        @pl.when(j == N_ZA + 2)
        def _():
            kvw_ref[...] = y[:, :KV_ROW]
            gn_ref[...] = y[:, KV_ROW:]


def pack_projection(w_in, q_gain, ks_gain, kw_gain):
    cut = np.cumsum(SPLITS)[:-1].tolist()
    x_rg, g_rg, q, kv_c, kv_s, kv_w, g_nsa, x_pool, g_br = jnp.split(w_in, cut, axis=1)
    per_group = 3 * HPG
    gn = jnp.zeros((w_in.shape[0], GN_COLS), w_in.dtype)
    for g in range(KV_HEADS):
        gn = gn.at[:, g * LANES:g * LANES + per_group].set(g_nsa[:, g * per_group:(g + 1) * per_group])
    w = jnp.concatenate([g_br, x_rg, g_rg, x_pool, q, kv_c, kv_s, kv_w, gn], axis=1).astype(BF16)
    ones_v = jnp.ones((KV_HEADS * HEAD_DIM,), F32)
    zeros_v = jnp.zeros((KV_HEADS * HEAD_DIM,), F32)
    gain = jnp.concatenate([jnp.ones((ZA_COLS,), F32), jnp.tile(q_gain.astype(F32), N_HEADS) * HEAD_DIM ** -0.5,
                            jnp.ones((KV_ROW,), F32),
                            jnp.tile(ks_gain.astype(F32), KV_HEADS), ones_v,
                            jnp.tile(kw_gain.astype(F32), KV_HEADS), ones_v,
                            jnp.ones((GN_COLS,), F32)])
    mask = jnp.concatenate([jnp.zeros((ZA_COLS,), F32), jnp.ones((Q_COLS,), F32), jnp.zeros((KV_ROW,), F32),
                            ones_v, zeros_v, ones_v, zeros_v, jnp.zeros((GN_COLS,), F32)])
    return w, gain[None], mask[None]


def projection(x, norm_gain, packed):
    w, gain, mask = packed
    T = x.shape[0]
    tm = _pick(T, (1024, 512, 256, 128))
    seg = jnp.asarray((np.arange(PROJ_TN)[:, None] // HEAD_DIM == np.arange(PROJ_TN)[None, :] // HEAD_DIM)
                      .astype(np.float32), BF16)
    row = lambda i, j: (i, 0)
    return pl.pallas_call(
        _proj_kernel,
        grid=(T // tm, PROJ_COLS // PROJ_TN),
        in_specs=[pl.BlockSpec((tm, D_MODEL), row),
                  pl.BlockSpec((1, D_MODEL), lambda i, j: (0, 0)),
                  pl.BlockSpec((D_MODEL, PROJ_TN), lambda i, j: (0, j)),
                  pl.BlockSpec((PROJ_TN, PROJ_TN), lambda i, j: (0, 0)),
                  pl.BlockSpec((1, PROJ_TN), lambda i, j: (0, j)),
                  pl.BlockSpec((1, PROJ_TN), lambda i, j: (0, j))],
        out_specs=[pl.BlockSpec((tm, PROJ_TN), lambda i, j: (i, jnp.minimum(j, N_ZA - 1))),
                   pl.BlockSpec((tm, Q_COLS), row),
                   pl.BlockSpec((tm, KV_ROW), row), pl.BlockSpec((tm, KV_ROW), row), pl.BlockSpec((tm, KV_ROW), row),
                   pl.BlockSpec((tm, GN_COLS), row)],
        out_shape=[jax.ShapeDtypeStruct((T, ZA_COLS), F32), jax.ShapeDtypeStruct((T, Q_COLS), BF16),
                   jax.ShapeDtypeStruct((T, KV_ROW), F32), jax.ShapeDtypeStruct((T, KV_ROW), F32),
                   jax.ShapeDtypeStruct((T, KV_ROW), F32), jax.ShapeDtypeStruct((T, GN_COLS), F32)],
        scratch_shapes=[pltpu.VMEM((tm, D_MODEL), BF16)],
        compiler_params=pltpu.CompilerParams(dimension_semantics=("parallel", "arbitrary"),
                                             vmem_limit_bytes=VMEM_LIMIT_BYTES),
        name="projection",
    )(x, norm_gain.astype(F32)[None], w, seg, gain, mask)


CMP_PAIRS = CMP_STRIDE // 2
CMP_GW = KV_HEADS * CMP_HIDDEN


def _compress_kernel(*refs, n_pages, prefetch):
    refs = refs[1:] if prefetch else refs
    pages = refs[:n_pages]
    w1_ref, pe_ref, w2_ref, kn_ref, ck_ref, cv_ref = refs[n_pages:]
    cpp = pages[0].shape[1] // (2 * CMP_STRIDE)
    m = n_pages * cpp
    for e, out_ref in enumerate((ck_ref, cv_ref)):
        acc = jnp.zeros((m, 2 * CMP_GW), F32)
        for p in range(CMP_PAIRS):
            def rows(s):
                return jnp.concatenate([pg[0, pl.ds(2 * s + e, cpp, stride=2 * CMP_STRIDE), :]
                                        for pg in pages], axis=0)
            a = jnp.concatenate([rows(2 * p), rows(2 * p + 1)], axis=1).astype(BF16)
            acc = acc + jnp.dot(a, w1_ref[e, p], preferred_element_type=F32)
        h = acc[:, :CMP_GW] + pltpu.roll(acc[:, CMP_GW:], m - 1, axis=0) + pe_ref[e]
        gl = jax.nn.gelu(h).astype(BF16)
        for g in range(KV_HEADS):
            c = jnp.dot(gl, w2_ref[e, g], preferred_element_type=F32)
            if e == 0:
                c = c * lax.rsqrt(jnp.sum(c * c, axis=1, keepdims=True) * (1.0 / HEAD_DIM) + EPS) * kn_ref[...]
            out_ref[0, g] = c.astype(out_ref.dtype)


def pack_compress(w1, w2, pe, kn):
    halves = w1.reshape(2, 2, CMP_STRIDE, HEAD_DIM, CMP_HIDDEN)
    eye = jnp.eye(KV_HEADS, dtype=w1.dtype)
    bd = jnp.einsum('ehsdf,gk->esgdhkf', halves, eye)
    w1p = bd.reshape(2, CMP_PAIRS, 2 * KV_HEADS * HEAD_DIM, 2 * CMP_GW).astype(BF16)
    pe_term = jnp.einsum('led,eldf->ef', pe, w1)
    pe_t = jnp.tile(pe_term, (1, KV_HEADS))[:, None, :].astype(F32)
    w2p = jnp.zeros((2, KV_HEADS, CMP_GW, LANES), w2.dtype)
    for g in range(KV_HEADS):
        w2p = w2p.at[:, g, g * CMP_HIDDEN:(g + 1) * CMP_HIDDEN, :HEAD_DIM].set(w2)
    knp = jnp.concatenate([kn.astype(F32), jnp.zeros((LANES - HEAD_DIM,), F32)])[None]
    return w1p, pe_t, w2p.astype(BF16), knp


def compress(rows, packed, page_table=None):
    w1p, pe_t, w2p, knp = packed
    page = rows.shape[1]
    rows = rows.reshape(rows.shape[0], 2 * page, LANES)
    if page_table is None:
        bx, n_pages = rows.shape[0], 1
        page_specs = [pl.BlockSpec((1, 2 * page, LANES), lambda b: (b, 0, 0))]
        const = lambda nd: (lambda b: (0,) * nd)
        out_map = lambda b: (b, 0, 0, 0)
        prefetch = 0
    else:
        bx, n_pages = page_table.shape
        page_specs = [pl.BlockSpec((1, 2 * page, LANES), lambda b, pt, k=k: (pt[b, k], 0, 0)) for k in range(n_pages)]
        const = lambda nd: (lambda b, pt: (0,) * nd)
        out_map = lambda b, pt: (b, 0, 0, 0)
        prefetch = 1
    m = n_pages * page // CMP_STRIDE
    in_specs = page_specs + [pl.BlockSpec(w1p.shape, const(4)), pl.BlockSpec(pe_t.shape, const(3)),
                             pl.BlockSpec(w2p.shape, const(4)), pl.BlockSpec(knp.shape, const(2))]
    out_spec = pl.BlockSpec((1, KV_HEADS, m, LANES), out_map)
    out_shape = jax.ShapeDtypeStruct((bx, KV_HEADS, m, LANES), BF16)
    grid_spec = pltpu.PrefetchScalarGridSpec(num_scalar_prefetch=prefetch, grid=(bx,), in_specs=in_specs,
                                             out_specs=[out_spec, out_spec])
    args = ([page_table] if prefetch else []) + [rows] * n_pages + [w1p, pe_t, w2p, knp]
    return pl.pallas_call(
        functools.partial(_compress_kernel, n_pages=n_pages, prefetch=prefetch),
        grid_spec=grid_spec,
        out_shape=[out_shape, out_shape],
        compiler_params=pltpu.CompilerParams(dimension_semantics=("parallel",), vmem_limit_bytes=VMEM_LIMIT_BYTES),
        name="compress",
    )(*args)


MIX_TT = 512
CONV_HALO = SUBLANES
SCAN_UNROLL = 4


def _log1p(y):
    u = 1.0 + y
    return jnp.where(u == 1.0, y, jnp.log(u) * (y / jnp.where(u == 1.0, 1.0, u - 1.0)))


def _neg_expm1(x):
    t = jnp.tanh(0.5 * x)
    return -2.0 * t / (1.0 - t)


def _softplus(x):
    return jnp.maximum(x, 0.0) + _log1p(jnp.exp(-jnp.abs(x)))


def _rglru_coeffs(xc, wa_ref, ba_ref, wx_ref, bx_ref, lam_ref):
    xb = xc.astype(BF16)
    r = jax.nn.sigmoid(jnp.dot(xb, wa_ref[...], preferred_element_type=F32) + ba_ref[...])
    i = jax.nn.sigmoid(jnp.dot(xb, wx_ref[...], preferred_element_type=F32) + bx_ref[...])
    log_a = -RG_C * r * _softplus(-lam_ref[...])
    return jnp.exp(log_a), jnp.sqrt(_neg_expm1(2.0 * log_a)) * (i * xc)


def _pool_project(sums_minus, wp_ref, scale_ref):
    return jnp.dot(sums_minus.astype(BF16), wp_ref[...], preferred_element_type=F32) * scale_ref[...]


def _mixer_seq_kernel(xrg_ref, grg_ref, xpool_ref, conv0_ref, h0_ref, pool0_ref, cw_ref, cb_ref, wa_ref, ba_ref,
                      wx_ref, bx_ref, lam_ref, wp_ref, ps_ref, ya_ref, yc_ref, convn_ref, hn_ref, pooln_ref,
                      xe_ref, pe_ref, a_ref, u_ref, h_ref, carry_ref, *, start_pos):
    i = pl.program_id(1)
    tt = xrg_ref.shape[0]

    @pl.when(i == 0)
    def _():
        xe_ref[0:CONV_HALO, :] = conv0_ref[0]
        pe_ref[0:POOL_MAX, :] = pool0_ref[0]
        carry_ref[...] = h0_ref[0]

    xe_ref[CONV_HALO:CONV_HALO + tt, :] = xrg_ref[...]
    xc = cb_ref[...] + sum(cw_ref[k:k + 1, :] * xe_ref[CONV_HALO - (CONV_W - 1) + k:CONV_HALO - (CONV_W - 1) + k + tt, :]
                           for k in range(CONV_W))
    a, u = _rglru_coeffs(xc, wa_ref, ba_ref, wx_ref, bx_ref, lam_ref)
    a_ref[...] = a
    u_ref[...] = u

    row = lax.broadcasted_iota(jnp.int32, (SUBLANES, D_RNN), 0)

    def block(j, carry):
        r0 = pl.multiple_of(j * SUBLANES, SUBLANES)
        ab = a_ref[pl.ds(r0, SUBLANES), :]
        ub = u_ref[pl.ds(r0, SUBLANES), :]
        for d in (1, 2, 4):
            a_sh = jnp.where(row >= d, pltpu.roll(ab, d, axis=0), 1.0)
            u_sh = jnp.where(row >= d, pltpu.roll(ub, d, axis=0), 0.0)
            ub = ab * u_sh + ub
            ab = ab * a_sh
        hb = ab * carry + ub
        h_ref[pl.ds(r0, SUBLANES), :] = hb
        return jnp.broadcast_to(hb[SUBLANES - 1:SUBLANES, :], (SUBLANES, D_RNN))

    carry = lax.fori_loop(0, tt // SUBLANES, block, carry_ref[...], unroll=SCAN_UNROLL)
    carry_ref[...] = carry
    ya_ref[...] = (h_ref[...] * jax.nn.gelu(grg_ref[...])).astype(ya_ref.dtype)

    pe_ref[POOL_MAX:POOL_MAX + tt, :] = xpool_ref[...]
    pos = start_pos + i * tt + lax.broadcasted_iota(jnp.int32, (tt, 1), 0)
    parts = []
    for g, w in enumerate(POOL_WINDOWS):
        lanes = slice(g * POOL_GW, (g + 1) * POOL_GW)
        s = sum(pe_ref[POOL_MAX - k:POOL_MAX - k + tt, lanes] for k in range(w))
        cnt = jnp.minimum(pos + 1, w).astype(F32)
        parts.append(s / cnt - pe_ref[POOL_MAX:POOL_MAX + tt, lanes])
    yc_ref[...] = _pool_project(jnp.concatenate(parts, axis=1), wp_ref, ps_ref).astype(yc_ref.dtype)

    @pl.when(i == pl.num_programs(1) - 1)
    def _():
        convn_ref[0] = xe_ref[CONV_HALO + tt - (CONV_W - 1):CONV_HALO + tt, :]
        hn_ref[0] = carry[0:1, :]
        pooln_ref[0] = pe_ref[POOL_MAX + tt - (POOL_MAX - 1):POOL_MAX + tt, :]

    xe_ref[0:CONV_HALO, :] = xe_ref[tt:tt + CONV_HALO, :]
    pe_ref[0:POOL_MAX, :] = pe_ref[tt:tt + POOL_MAX, :]


def pack_mixer(conv_w, conv_b, w_a, b_a, w_x, b_x, lam, w_pool, scale):
    def block_diag(w):
        n, c, d = w.shape
        return jnp.einsum('ncd,nm->ncmd', w, jnp.eye(n, dtype=w.dtype)).reshape(n * c, n * d).astype(BF16)

    row = lambda v: v.astype(F32)[None]
    return (conv_w.astype(F32), row(conv_b), block_diag(w_a), row(b_a), block_diag(w_x), row(b_x), row(lam),
            block_diag(w_pool), row(scale))


def mixer_seq(za, batch, seq, packed, conv0, h0, pool0, start_pos):
    tt = min(MIX_TT, seq)
    nt = seq // tt
    conv_pad = jnp.pad(conv0.astype(F32), ((0, 0), (CONV_HALO - (CONV_W - 1), 0), (0, 0)))
    pool_pad = jnp.pad(pool0.astype(F32), ((0, 0), (1, 0), (0, 0)))
    h_pad = jnp.broadcast_to(h0.astype(F32)[:, None, :], (batch, SUBLANES, D_RNN))
    col = lambda c: pl.BlockSpec((tt, D_RNN), lambda b, i: (b * nt + i, c))
    state = lambda rows: pl.BlockSpec((1, rows, D_RNN), lambda b, i: (b, 0, 0))
    full = lambda a: pl.BlockSpec(a.shape, lambda b, i: (0,) * a.ndim)
    out_rows = pl.BlockSpec((tt, D_RNN), lambda b, i: (b * nt + i, 0))
    return pl.pallas_call(
        functools.partial(_mixer_seq_kernel, start_pos=start_pos),
        grid=(batch, nt),
        in_specs=[col(ZA_XRG // D_RNN), col(ZA_GRG // D_RNN), col(ZA_XPOOL // D_RNN), state(CONV_HALO), state(SUBLANES),
                  state(POOL_MAX)] + [full(a) for a in packed],
        out_specs=[out_rows, out_rows, state(CONV_W - 1), state(1), state(POOL_MAX - 1)],
        out_shape=[jax.ShapeDtypeStruct((batch * seq, D_RNN), BF16), jax.ShapeDtypeStruct((batch * seq, D_POOL), BF16),
                   jax.ShapeDtypeStruct((batch, CONV_W - 1, D_RNN), F32), jax.ShapeDtypeStruct((batch, 1, D_RNN), F32),
                   jax.ShapeDtypeStruct((batch, POOL_MAX - 1, D_POOL), F32)],
        scratch_shapes=[pltpu.VMEM((CONV_HALO + tt, D_RNN), F32), pltpu.VMEM((POOL_MAX + tt, D_POOL), F32),
                        pltpu.VMEM((tt, D_RNN), F32), pltpu.VMEM((tt, D_RNN), F32), pltpu.VMEM((tt, D_RNN), F32),
                        pltpu.VMEM((SUBLANES, D_RNN), F32)],
        compiler_params=pltpu.CompilerParams(dimension_semantics=("parallel", "arbitrary"),
                                             vmem_limit_bytes=VMEM_LIMIT_BYTES),
        name="mixer_seq",
    )(za, za, za, conv_pad, h_pad, pool_pad, *packed)


def _mixer_step_kernel(xrg_ref, grg_ref, xpool_ref, conv_ref, h0_ref, pool_ref, cw_ref, cb_ref, wa_ref, ba_ref,
                       wx_ref, bx_ref, lam_ref, wp_ref, ps_ref, ya_ref, yc_ref, hn_ref, *, start_pos):
    x = xrg_ref[...]
    xc = cb_ref[...] + cw_ref[CONV_W - 1:CONV_W, :] * x + sum(cw_ref[k:k + 1, :] * conv_ref[k] for k in range(CONV_W - 1))
    a, u = _rglru_coeffs(xc, wa_ref, ba_ref, wx_ref, bx_ref, lam_ref)
    h = a * h0_ref[...] + u
    hn_ref[...] = h
    ya_ref[...] = (h * jax.nn.gelu(grg_ref[...])).astype(ya_ref.dtype)
    xp = xpool_ref[...]
    parts = []
    for g, w in enumerate(POOL_WINDOWS):
        lanes = slice(g * POOL_GW, (g + 1) * POOL_GW)
        s = xp[:, lanes] + sum(pool_ref[POOL_MAX - 1 - k][:, lanes] for k in range(1, w))
        parts.append(s / float(min(start_pos + 1, w)) - xp[:, lanes])
    yc_ref[...] = _pool_project(jnp.concatenate(parts, axis=1), wp_ref, ps_ref).astype(yc_ref.dtype)


def mixer_step(za, packed, conv_state, h0, pool_state, start_pos):
    batch = za.shape[0]
    conv_t = jnp.swapaxes(conv_state.astype(F32), 0, 1)
    pool_t = jnp.swapaxes(pool_state.astype(F32), 0, 1)
    col = lambda c: pl.BlockSpec((batch, D_RNN), lambda i: (0, c))
    full = lambda a: pl.BlockSpec(a.shape, lambda i: (0,) * a.ndim)
    rows = pl.BlockSpec((batch, D_RNN), lambda i: (0, 0))
    return pl.pallas_call(
        functools.partial(_mixer_step_kernel, start_pos=start_pos),
        grid=(1,),
        in_specs=[col(ZA_XRG // D_RNN), col(ZA_GRG // D_RNN), col(ZA_XPOOL // D_RNN), full(conv_t), rows, full(pool_t)]
        + [full(a) for a in packed],
        out_specs=[rows, rows, rows],
        out_shape=[jax.ShapeDtypeStruct((batch, D_RNN), BF16), jax.ShapeDtypeStruct((batch, D_POOL), BF16),
                   jax.ShapeDtypeStruct((batch, D_RNN), F32)],
        compiler_params=pltpu.CompilerParams(vmem_limit_bytes=VMEM_LIMIT_BYTES),
        name="mixer_step",
    )(za, za, za, conv_t, h0.astype(F32), pool_t, *packed)


QB = Q_BLOCK
ROWS = HPG * QB
N_WIN_TILES = WINDOW // QB + 1
N_SEL_BIAS = REL_MAX_DIST // QB + 2
SEL_SPAN = 4
KV_CHUNK = 512


def _nsa_prompt_kernel(q_ref, ck_ref, cv_ref, kvs_ref, kvw_ref, gate_ref, bc_ref, tbs_ref, tbw_ref,
                       ovl_ref, eye_ref, pq_ref, pk_ref, pv_ref, onehot_ref, o_ref,
                       acc_ref, m_ref, qa_ref, comb_ref, ks_ref, vs_ref, kw_ref, vw_ref, cnt_ref, *, n_cmp, n_blk):
    qb = pl.program_id(2)
    t0 = qb * QB
    ncp = ck_ref.shape[2]
    seq = kvs_ref.shape[1]
    lane_row = lax.broadcasted_iota(jnp.int32, (1, LANES), 1)

    @pl.when(qb == 0)
    def _():
        ones_hi = jnp.where(lane_row >= HEAD_DIM, 1.0, 0.0)
        kw_ref[0:WINDOW, :] = jnp.broadcast_to(jnp.where(lane_row == HEAD_DIM, 1.0, 0.0), (WINDOW, LANES)).astype(BF16)
        vw_ref[0:WINDOW, :] = jnp.zeros((WINDOW, LANES), BF16)

        def stage(c, carry):
            r = pl.multiple_of(c * KV_CHUNK, KV_CHUNK)
            sel = kvs_ref[0, pl.ds(r, KV_CHUNK), :].astype(BF16)
            win = kvw_ref[0, pl.ds(r, KV_CHUNK), :].astype(BF16)
            ks_ref[pl.ds(r, KV_CHUNK), :] = (jnp.dot(sel, pk_ref[0], preferred_element_type=F32)
                                             + onehot_ref[pl.ds(r, KV_CHUNK), :].astype(F32)).astype(BF16)
            vs_ref[pl.ds(r, KV_CHUNK), :] = (jnp.dot(sel, pv_ref[0], preferred_element_type=F32) + ones_hi).astype(BF16)
            kw_ref[pl.ds(WINDOW + r, KV_CHUNK), :] = jnp.dot(win, pk_ref[0], preferred_element_type=F32).astype(BF16)
            vw_ref[pl.ds(WINDOW + r, KV_CHUNK), :] = (jnp.dot(win, pv_ref[0], preferred_element_type=F32)
                                                      + ones_hi).astype(BF16)
            return carry

        lax.fori_loop(0, seq // KV_CHUNK, stage, 0)

    q4 = jnp.dot(q_ref[0], pq_ref[...], preferred_element_type=F32)
    q3 = jnp.concatenate([q4[:, j * LANES:(j + 1) * LANES] for j in range(HPG)], axis=0)
    q = q3.astype(BF16)
    sig = jax.nn.sigmoid(gate_ref[0])
    gates = [jnp.concatenate([sig[:, 3 * j + c:3 * j + c + 1] for j in range(HPG)], axis=0) for c in range(3)]

    lc = _nt(q, ck_ref[0, 0]) + bc_ref[0].reshape(ROWS, ncp)
    tok = t0 + lax.broadcasted_iota(jnp.int32, (HPG, QB, ncp), 1).reshape(ROWS, ncp)
    col = lax.broadcasted_iota(jnp.int32, (ROWS, ncp), 1)
    valid = (tok >= col * CMP_STRIDE + (L_CMP - 1)) & (col < n_cmp)
    lc = jnp.where(valid, lc, NEG)
    mx = jnp.max(lc, axis=1, keepdims=True)
    p = jnp.where(valid, jnp.exp(lc - mx), 0.0)
    pc = p / jnp.maximum(jnp.sum(p, axis=1, keepdims=True), 1e-30)
    comb_ref[...] = gates[0] * jnp.dot(pc.astype(BF16), cv_ref[0, 0], preferred_element_type=F32)

    pcs = pc[0:QB] + pc[QB:2 * QB] + pc[2 * QB:3 * QB] + pc[3 * QB:4 * QB]
    hi = pcs.astype(BF16)
    lo = (pcs - hi.astype(F32)).astype(BF16)
    imp = _nt(ovl_ref[...], hi) + _nt(ovl_ref[...], lo)
    blk = lax.broadcasted_iota(jnp.int32, (n_blk, QB), 0)
    cur = (t0 + lax.broadcasted_iota(jnp.int32, (n_blk, QB), 1)) // SEL_BLOCK
    forced = (blk == 0) | (blk == cur) | (blk == cur - 1)
    score = jnp.where(blk > cur, -1.0, jnp.where(forced, FORCE, imp))
    chunks = [score[r:r + SUBLANES] for r in range(0, n_blk, SUBLANES)]
    sub = lax.broadcasted_iota(jnp.int32, (SUBLANES, QB), 0)
    cnt_ref[...] = jnp.zeros((n_blk, QB), F32)
    last_blk = (t0 + QB - 1) // SEL_BLOCK
    for mc in range(0, n_blk, SUBLANES):
        @pl.when(mc <= last_blk)
        def _(mc=mc):
            for r, ch in enumerate(chunks):
                first = r * SUBLANES
                part = jnp.zeros((SUBLANES, QB), F32)
                for m in range(mc, mc + SUBLANES):
                    row = jnp.broadcast_to(score[m:m + 1, :], (SUBLANES, QB))
                    if first > m:
                        beats = jnp.where(row >= ch, 1.0, 0.0)
                    elif first + SUBLANES - 1 < m:
                        beats = jnp.where(row > ch, 1.0, 0.0)
                    else:
                        beats = jnp.where(sub + first > m, jnp.where(row >= ch, 1.0, 0.0), jnp.where(row > ch, 1.0, 0.0))
                    part = part + beats
                cnt_ref[first:first + SUBLANES, :] += part
    sel_neg = jnp.where(cnt_ref[...] < float(min(N_SELECT, n_blk)), 0.0, NEG)
    pieces = [jnp.zeros((HEAD_DIM, QB), F32), sel_neg]
    if n_blk < HEAD_DIM:
        pieces.append(jnp.zeros((HEAD_DIM - n_blk, QB), F32))
    placed_t = jnp.concatenate(pieces, axis=0).astype(BF16)
    placed = _nt(eye_ref[...], placed_t)
    qa_ref[...] = (q3 + jnp.concatenate([placed] * HPG, axis=0)).astype(BF16)

    m_ref[...] = jnp.full((ROWS, LANES), M_INIT, F32)
    acc_ref[...] = jnp.zeros((ROWS, LANES), F32)
    n_bias = tbs_ref.shape[1]

    def body(kk, carry):
        off = pl.multiple_of(kk * (SEL_SPAN * QB), SEL_SPAN * QB)
        s = _nt(qa_ref[...], ks_ref[pl.ds(off, SEL_SPAN * QB), :])
        parts = []
        for u in range(SEL_SPAN):
            idx = jnp.clip(qb - (kk * SEL_SPAN + u), -1, n_bias - 2) + 1
            parts.append(s[:, u * QB:(u + 1) * QB] + tbs_ref[:, pl.ds(idx, 1)].reshape(ROWS, LANES))
        tile_max = functools.reduce(jnp.maximum, parts)
        m_old = m_ref[...]
        m_new = jnp.maximum(m_old, jnp.max(tile_max, axis=1, keepdims=True))
        alpha = jnp.exp(m_old - m_new)
        pr = jnp.concatenate([jnp.exp(x - m_new).astype(BF16) for x in parts], axis=1)
        acc_ref[...] = alpha * acc_ref[...] + jnp.dot(pr, vs_ref[pl.ds(off, SEL_SPAN * QB), :],
                                                      preferred_element_type=F32)
        m_ref[...] = m_new
        return carry

    lax.fori_loop(0, qb // SEL_SPAN + 1, body, 0)
    acc = acc_ref[...]
    comb_ref[...] += gates[1] * (acc / pltpu.roll(acc, HEAD_DIM, axis=1))

    qw = (q3 + jnp.where(lane_row == HEAD_DIM, NEG, 0.0)).astype(BF16)
    w_off = pl.multiple_of(t0, QB)
    sw = _nt(qw, kw_ref[pl.ds(w_off, WINDOW + QB), :]) + tbw_ref[...].reshape(ROWS, WINDOW + QB)
    pw = jnp.exp(sw - jnp.max(sw, axis=1, keepdims=True))
    accw = jnp.dot(pw.astype(BF16), vw_ref[pl.ds(w_off, WINDOW + QB), :], preferred_element_type=F32)
    comb_ref[...] += gates[2] * (accw / pltpu.roll(accw, HEAD_DIM, axis=1))

    comb = comb_ref[...]
    lane = lax.broadcasted_iota(jnp.int32, (QB, LANES), 1)
    for half in range(HPG // 2):
        a = comb[(2 * half) * QB:(2 * half + 1) * QB]
        b = comb[(2 * half + 1) * QB:(2 * half + 2) * QB]
        o_ref[0, :, half * LANES:(half + 1) * LANES] = jnp.where(lane < HEAD_DIM, a,
                                                                 pltpu.roll(b, HEAD_DIM, axis=1)).astype(o_ref.dtype)


def _bias_by_distance(rel_bias):
    max_d = (N_SEL_BIAS + 1) * QB
    return jnp.transpose(rel_bias.astype(F32)[rel_bucket(jnp.arange(max_d))])


def _ext(tab_t, lo, hi, ok_lo=None, ok_hi=None):
    n_heads, depth = tab_t.shape
    y = np.arange(lo, hi)
    parts = [jnp.broadcast_to(tab_t[:, :1], (n_heads, int(np.sum(y < 0)))), tab_t[:, max(lo, 0):max(min(hi, depth), 0)],
             jnp.broadcast_to(tab_t[:, -1:], (n_heads, int(np.sum(y > depth - 1))))]
    arr = jnp.concatenate(parts, axis=1)
    ok = np.ones(y.shape, bool)
    if ok_lo is not None:
        ok &= y >= ok_lo
    if ok_hi is not None:
        ok &= y <= ok_hi
    return jnp.where(jnp.asarray(ok)[None], arr, NEG)


def _toeplitz_tiles(ext, lo, ks):
    w = jnp.stack([ext[:, QB * k - (QB - 1) - lo:QB * k - (QB - 1) - lo + 2 * QB] for k in ks], axis=1)
    skew = jnp.tile(w, (1, 1, QB + 1))[..., :QB * (2 * QB + 1)].reshape(w.shape[:2] + (QB, 2 * QB + 1))[..., :QB]
    return skew[..., ::-1]


def rel_bias_tables(rel_bias, seq):
    tab_t = _bias_by_distance(rel_bias)
    max_d = tab_t.shape[1]
    lo = -(2 * QB - 1)
    tbs = _toeplitz_tiles(_ext(tab_t, lo, max_d, 0, None), lo, range(-1, N_SEL_BIAS))
    tbw = _toeplitz_tiles(_ext(tab_t, lo, max_d, 0, WINDOW), lo, range(N_WIN_TILES - 1, -1, -1))
    tbw = jnp.transpose(tbw, (0, 2, 1, 3)).reshape(N_HEADS, QB, N_WIN_TILES * QB)
    nqb = seq // QB
    ncp = seq // CMP_STRIDE
    per_qb = QB // CMP_STRIDE
    width = ncp + per_qb * (nqb - 1)
    c0 = QB * (nqb - 1) - (L_CMP - 1)
    ext = _ext(tab_t, c0 - CMP_STRIDE * (width - 1), c0 + QB)
    rows = []
    for a in range(per_qb):
        for r in range(CMP_STRIDE):
            f = ext[:, r::CMP_STRIDE][:, :width + per_qb - 1][:, ::-1]
            rows.append(f[:, per_qb - 1 - a:per_qb - 1 - a + width])
    v = jnp.stack(rows, axis=1)
    bc = jnp.stack([v[:, :, per_qb * (nqb - 1 - b):per_qb * (nqb - 1 - b) + ncp] for b in range(nqb)])
    return tbs, tbw, bc


def nsa_prompt_pallas(q, ck, cv, kv_sel, kv_win, gn, tables):
    B, S = q.shape[:2]
    assert S % (SEL_SPAN * QB) == 0 and S % KV_CHUNK == 0 and S // SEL_BLOCK <= HEAD_DIM
    tbs, tbw, bc = tables
    nqb = S // QB
    ncp = S // CMP_STRIDE
    n_cmp = ncp - 1
    n_blk = S // SEL_BLOCK
    c0 = np.arange(ncp)[None, :] * CMP_STRIDE
    s0 = np.arange(n_blk)[:, None] * SEL_BLOCK
    ovl_t = ((c0 <= s0 + SEL_BLOCK - 1) & (c0 + L_CMP - 1 >= s0) & (np.arange(ncp)[None, :] < n_cmp))
    ovl_t = jnp.asarray(ovl_t.astype(np.float32), BF16)
    eye = jnp.asarray(np.eye(QB, dtype=np.float32), BF16)
    gw = HPG * HEAD_DIM
    pq = np.zeros((gw, HPG * LANES), np.float32)
    pq[np.arange(gw), (np.arange(gw) // HEAD_DIM) * LANES + np.arange(gw) % HEAD_DIM] = 1.0
    pk = np.zeros((KV_HEADS, KV_ROW, LANES), np.float32)
    pv = np.zeros((KV_HEADS, KV_ROW, LANES), np.float32)
    for g in range(KV_HEADS):
        pk[g, g * HEAD_DIM + np.arange(HEAD_DIM), np.arange(HEAD_DIM)] = 1.0
        pv[g, (KV_HEADS + g) * HEAD_DIM + np.arange(HEAD_DIM), np.arange(HEAD_DIM)] = 1.0
    onehot = np.zeros((S, LANES), np.float32)
    onehot[np.arange(S), HEAD_DIM + np.arange(S) // SEL_BLOCK] = 1.0

    kv_spec = pl.BlockSpec((1, S, KV_ROW), lambda b, g, i: (b, 0, 0))
    cmp_spec = pl.BlockSpec((1, 1, ncp, LANES), lambda b, g, i: (b, g, 0, 0))
    const2 = lambda b, g, i: (0, 0)
    return pl.pallas_call(
        functools.partial(_nsa_prompt_kernel, n_cmp=n_cmp, n_blk=n_blk),
        grid=(B, KV_HEADS, nqb),
        in_specs=[
            pl.BlockSpec((1, QB, gw), lambda b, g, i: (b, i, g)),
            cmp_spec, cmp_spec, kv_spec, kv_spec,
            pl.BlockSpec((1, QB, LANES), lambda b, g, i: (b, i, g)),
            pl.BlockSpec((1, HPG, QB, ncp), lambda b, g, i: (i, g, 0, 0)),
            pl.BlockSpec((HPG, N_SEL_BIAS + 1, QB, QB), lambda b, g, i: (g, 0, 0, 0)),
            pl.BlockSpec((HPG, QB, N_WIN_TILES * QB), lambda b, g, i: (g, 0, 0)),
            pl.BlockSpec((n_blk, ncp), const2),
            pl.BlockSpec((QB, QB), const2),
            pl.BlockSpec((gw, HPG * LANES), const2),
            pl.BlockSpec((1, KV_ROW, LANES), lambda b, g, i: (g, 0, 0)),
            pl.BlockSpec((1, KV_ROW, LANES), lambda b, g, i: (g, 0, 0)),
            pl.BlockSpec((S, LANES), const2),
        ],
        out_specs=pl.BlockSpec((1, QB, gw), lambda b, g, i: (b, i, g)),
        out_shape=jax.ShapeDtypeStruct((B, S, N_HEADS * HEAD_DIM), BF16),
        scratch_shapes=[pltpu.VMEM((ROWS, LANES), F32), pltpu.VMEM((ROWS, LANES), F32),
                        pltpu.VMEM((ROWS, LANES), BF16), pltpu.VMEM((ROWS, LANES), F32),
                        pltpu.VMEM((S, LANES), BF16), pltpu.VMEM((S, LANES), BF16),
                        pltpu.VMEM((S + WINDOW, LANES), BF16), pltpu.VMEM((S + WINDOW, LANES), BF16),
                        pltpu.VMEM((n_blk, QB), F32)],
        compiler_params=pltpu.CompilerParams(dimension_semantics=("parallel", "parallel", "arbitrary"),
                                             vmem_limit_bytes=VMEM_LIMIT_BYTES),
        name="nsa_prompt",
    )(q, ck, cv, kv_sel, kv_win, gn, bc, tbs, tbw, ovl_t, eye, jnp.asarray(pq, BF16), jnp.asarray(pk, BF16),
      jnp.asarray(pv, BF16), jnp.asarray(onehot, BF16))


STEP_ROWS = SUBLANES
STAGE_PAGES = 8
RANK_LANES = 2 * LANES


def _split3(x):
    hi = x.astype(BF16)
    r1 = x - hi.astype(F32)
    mid = r1.astype(BF16)
    return hi, mid, (r1 - mid.astype(F32)).astype(BF16)


def _nsa_step_kernel(*refs, n_pages, n_cmp):
    pages = refs[1:1 + n_pages]
    (q_ref, ck_ref, cv_ref, kvs_ref, kvw_ref, win_ref, gn_ref, bcs_ref, bsel_ref, bwin_ref, ovl_ref, oh_ref,
     pq_ref, pk_ref, pv_ref, o_ref, kc_ref, vs_ref, kw_ref, vw_ref) = refs[1 + n_pages:]
    page = pages[0].shape[1]
    past = n_pages * page
    wb = win_ref.shape[1]
    ncp = ck_ref.shape[2]
    lane_row = lax.broadcasted_iota(jnp.int32, (1, LANES), 1)
    ones_hi = jnp.where(lane_row >= HEAD_DIM, 1.0, 0.0)
    ones_hi2 = jnp.concatenate([ones_hi] * KV_HEADS, axis=1)
    first_row = lax.broadcasted_iota(jnp.int32, (LANES, 1), 0) == 0

    @pl.when(pl.program_id(0) == 0)
    def _():
        for g in range(KV_HEADS):
            kc_ref[g, 0:past, LANES:2 * LANES] = oh_ref[...]
            kc_ref[g, past:past + LANES, LANES:2 * LANES] = jnp.zeros((LANES, LANES), BF16)

    def place(rows_bf16, r0, nrows, k_dst, v_dst):
        k2 = jnp.dot(rows_bf16, pk_ref[...], preferred_element_type=F32)
        v2 = jnp.dot(rows_bf16, pv_ref[...], preferred_element_type=F32) + ones_hi2
        for g in range(KV_HEADS):
            k_dst[g, r0:r0 + nrows, 0:LANES] = k2[:, g * LANES:(g + 1) * LANES].astype(BF16)
            v_dst[g, r0:r0 + nrows, :] = v2[:, g * LANES:(g + 1) * LANES].astype(BF16)

    for c in range(n_pages // STAGE_PAGES):
        blk = jnp.concatenate([pg[0] for pg in pages[c * STAGE_PAGES:(c + 1) * STAGE_PAGES]], axis=0).astype(BF16)
        place(blk, c * STAGE_PAGES * page, STAGE_PAGES * page, kc_ref, vs_ref)
    place(jnp.where(first_row, kvs_ref[0], 0.0).astype(BF16), past, LANES, kc_ref, vs_ref)
    place(win_ref[0].astype(BF16), 0, wb, kw_ref, vw_ref)
    place(jnp.where(first_row, kvw_ref[0], 0.0).astype(BF16), wb, LANES, kw_ref, vw_ref)

    sig = jax.nn.sigmoid(gn_ref[0])
    lane = lax.broadcasted_iota(jnp.int32, (1, LANES), 1)
    rr = lax.broadcasted_iota(jnp.int32, (RANK_LANES, RANK_LANES), 0)
    cc = lax.broadcasted_iota(jnp.int32, (RANK_LANES, RANK_LANES), 1)
    ones_sq = jnp.ones((RANK_LANES, RANK_LANES), BF16)
    pad_rows = jnp.zeros((STEP_ROWS - HPG, LANES), F32)
    for g in range(KV_HEADS):
        gw = HPG * HEAD_DIM
        q4 = jnp.dot(q_ref[0][:, g * gw:(g + 1) * gw], pq_ref[...], preferred_element_type=F32)
        q8 = jnp.concatenate([q4[:, j * LANES:(j + 1) * LANES] for j in range(HPG)] + [pad_rows], axis=0)
        gates = [jnp.concatenate([sig[:, g * LANES + 3 * j + c:g * LANES + 3 * j + c + 1] for j in range(HPG)]
                                 + [pad_rows[:, 0:1]], axis=0) for c in range(3)]

        lc = _nt(q8.astype(BF16), ck_ref[0, g]) + bcs_ref[g]
        col = lax.broadcasted_iota(jnp.int32, (STEP_ROWS, ncp), 1)
        valid = col < n_cmp
        lc = jnp.where(valid, lc, NEG)
        p = jnp.where(valid, jnp.exp(lc - jnp.max(lc, axis=1, keepdims=True)), 0.0)
        pc = p / jnp.maximum(jnp.sum(p, axis=1, keepdims=True), 1e-30)
        comb = gates[0] * jnp.dot(pc.astype(BF16), cv_ref[0, g], preferred_element_type=F32)

        pcs = jnp.broadcast_to(jnp.sum(pc[0:HPG], axis=0, keepdims=True), (STEP_ROWS, ncp))
        hi = pcs.astype(BF16)
        lo = (pcs - hi.astype(F32)).astype(BF16)
        imp = (jnp.dot(hi, ovl_ref[...], preferred_element_type=F32)
               + jnp.dot(lo, ovl_ref[...], preferred_element_type=F32))[0:1]
        blk_id = lax.broadcasted_iota(jnp.int32, (1, RANK_LANES), 1)
        cur = past // SEL_BLOCK
        forced = (blk_id == 0) | (blk_id == cur) | (blk_id == cur - 1)
        score = jnp.where(blk_id > cur, -1.0, jnp.where(forced, FORCE, imp))
        s_n = jnp.broadcast_to(score, (RANK_LANES, RANK_LANES))
        diag = jnp.where(rr == cc, s_n, 0.0)
        s_m = sum(_nt(part, ones_sq) for part in _split3(diag))
        beats = jnp.where(rr < cc, jnp.where(s_m >= s_n, 1.0, 0.0), jnp.where(s_m > s_n, 1.0, 0.0))
        rank = jnp.sum(beats, axis=0, keepdims=True)
        sel_neg = jnp.where(rank < float(N_SELECT), 0.0, NEG)

        qa = jnp.concatenate([q8, jnp.broadcast_to(sel_neg[:, :LANES], (STEP_ROWS, LANES))], axis=1).astype(BF16)
        s_past = _nt(qa, kc_ref[g, 0:past, :]) + bsel_ref[g, :, 0:past]
        s_new = (_nt(qa, kc_ref[g, past:past + LANES, :]) + bsel_ref[g, :, past:past + LANES]
                 + sel_neg[:, LANES:LANES + 1])
        mx = jnp.maximum(jnp.max(s_past, axis=1, keepdims=True), jnp.max(s_new, axis=1, keepdims=True))
        acc = (jnp.dot(jnp.exp(s_past - mx).astype(BF16), vs_ref[g, 0:past, :], preferred_element_type=F32)
               + jnp.dot(jnp.exp(s_new - mx).astype(BF16), vs_ref[g, past:past + LANES, :], preferred_element_type=F32))
        comb = comb + gates[1] * (acc / pltpu.roll(acc, HEAD_DIM, axis=1))

        sw = _nt(q8.astype(BF16), kw_ref[g]) + bwin_ref[g]
        pw = jnp.exp(sw - jnp.max(sw, axis=1, keepdims=True))
        accw = jnp.dot(pw.astype(BF16), vw_ref[g], preferred_element_type=F32)
        comb = comb + gates[2] * (accw / pltpu.roll(accw, HEAD_DIM, axis=1))

        for half in range(HPG // 2):
            a = comb[2 * half:2 * half + 1]
            b = comb[2 * half + 1:2 * half + 2]
            o_ref[0, :, g * gw + half * LANES:g * gw + (half + 1) * LANES] = jnp.where(
                lane < HEAD_DIM, a, pltpu.roll(b, HEAD_DIM, axis=1)).astype(o_ref.dtype)


def step_bias_tables(rel_bias, past, wb, ncp):
    tab_t = _bias_by_distance(rel_bias)

    def rows(t):
        return jnp.pad(t.reshape(KV_HEADS, HPG, -1), ((0, 0), (0, STEP_ROWS - HPG), (0, 0)))

    d_first = past - (L_CMP - 1)
    cmp_t = _ext(tab_t, d_first - CMP_STRIDE * (ncp - 1), d_first + 1, 0, None)[:, ::CMP_STRIDE][:, ::-1]
    sel_t = _ext(tab_t, -(LANES - 1), past + 1, 0, None)[:, ::-1]
    win_t = _ext(tab_t, -(LANES - 1), wb + 1, 0, WINDOW)[:, ::-1]
    return rows(cmp_t), rows(sel_t), rows(win_t)


def nsa_step(q, ck, cv, kvs_new, kvw_new, sel_pool, win_buf, gn, page_table, tables):
    B, n_pages = page_table.shape
    page = sel_pool.shape[1]
    past = n_pages * page
    wb = win_buf.shape[1]
    ncp = ck.shape[2]
    n_blk = past // SEL_BLOCK + 1
    assert n_pages % STAGE_PAGES == 0 and n_blk <= LANES + 1 and page == LANES and wb % SUBLANES == 0
    bcs, bsel, bwin = tables
    c0 = np.arange(ncp)[:, None] * CMP_STRIDE
    s0 = np.arange(RANK_LANES)[None, :] * SEL_BLOCK
    ovl = ((c0 <= s0 + SEL_BLOCK - 1) & (c0 + L_CMP - 1 >= s0) & (np.arange(ncp)[:, None] < ncp - 1)
           & (np.arange(RANK_LANES)[None, :] < n_blk))
    onehot = np.zeros((past, LANES), np.float32)
    onehot[np.arange(past), np.arange(past) // SEL_BLOCK] = 1.0
    gw = HPG * HEAD_DIM
    pq = np.zeros((gw, HPG * LANES), np.float32)
    pq[np.arange(gw), (np.arange(gw) // HEAD_DIM) * LANES + np.arange(gw) % HEAD_DIM] = 1.0
    pk = np.zeros((KV_ROW, KV_HEADS * LANES), np.float32)
    pv = np.zeros((KV_ROW, KV_HEADS * LANES), np.float32)
    for g in range(KV_HEADS):
        pk[g * HEAD_DIM + np.arange(HEAD_DIM), g * LANES + np.arange(HEAD_DIM)] = 1.0
        pv[(KV_HEADS + g) * HEAD_DIM + np.arange(HEAD_DIM), g * LANES + np.arange(HEAD_DIM)] = 1.0
    consts = [jnp.asarray(ovl.astype(np.float32), BF16), jnp.asarray(onehot, BF16), jnp.asarray(pq, BF16),
              jnp.asarray(pk, BF16), jnp.asarray(pv, BF16)]

    row3 = lambda a: a.reshape(B, 1, a.shape[-1])
    per_b = lambda shape: pl.BlockSpec((1,) + shape, lambda b, pt: (b,) + (0,) * len(shape))
    full = lambda a: pl.BlockSpec(a.shape, lambda b, pt: (0,) * a.ndim)
    page_specs = [pl.BlockSpec((1, page, KV_ROW), lambda b, pt, k=k: (pt[b, k], 0, 0)) for k in range(n_pages)]
    in_specs = page_specs + [per_b((1, Q_COLS)), per_b((KV_HEADS, ncp, LANES)), per_b((KV_HEADS, ncp, LANES)),
                             per_b((1, KV_ROW)), per_b((1, KV_ROW)), per_b((wb, KV_ROW)), per_b((1, GN_COLS)),
                             full(bcs), full(bsel), full(bwin)] + [full(c) for c in consts]
    grid_spec = pltpu.PrefetchScalarGridSpec(
        num_scalar_prefetch=1, grid=(B,), in_specs=in_specs, out_specs=per_b((1, Q_COLS)),
        scratch_shapes=[pltpu.VMEM((KV_HEADS, past + LANES, 2 * LANES), BF16), pltpu.VMEM((KV_HEADS, past + LANES, LANES), BF16),
                        pltpu.VMEM((KV_HEADS, wb + LANES, LANES), BF16), pltpu.VMEM((KV_HEADS, wb + LANES, LANES), BF16)])
    out = pl.pallas_call(
        functools.partial(_nsa_step_kernel, n_pages=n_pages, n_cmp=ncp - 1),
        grid_spec=grid_spec,
        out_shape=jax.ShapeDtypeStruct((B, 1, Q_COLS), BF16),
        compiler_params=pltpu.CompilerParams(dimension_semantics=("arbitrary",), vmem_limit_bytes=VMEM_LIMIT_BYTES),
        name="nsa_step",
    )(page_table, *([sel_pool] * n_pages), row3(q), ck, cv, row3(kvs_new), row3(kvw_new), win_buf, row3(gn),
      bcs, bsel, bwin, *consts)
    return out.reshape(B, Q_COLS)


def _merge_kernel(ya_ref, yb_ref, yc_ref, ga_ref, gb_ref, gc_ref, x_ref, wa_ref, wb_ref, wc_ref, wo_ref, o_ref):
    def branch(y_ref, g_ref, w_ref):
        return jax.nn.sigmoid(g_ref[...]) * jnp.dot(y_ref[...], w_ref[...], preferred_element_type=F32)

    merged = branch(ya_ref, ga_ref, wa_ref) + branch(yb_ref, gb_ref, wb_ref) + branch(yc_ref, gc_ref, wc_ref)
    o_ref[...] = x_ref[...] + jnp.dot(merged.astype(BF16), wo_ref[...], preferred_element_type=F32)


def merge(ya, yb, yc, za, x, wa, wb, wc, wo):
    T = x.shape[0]
    tm = _pick(T, (512, 256, 128))
    y_spec = pl.BlockSpec((tm, ya.shape[1]), lambda i: (i, 0))
    w_spec = pl.BlockSpec((ya.shape[1], D_MODEL), lambda i: (0, 0))
    return pl.pallas_call(
        _merge_kernel,
        grid=(T // tm,),
        in_specs=[y_spec, y_spec, y_spec,
                  pl.BlockSpec((tm, D_MODEL), lambda i: (i, 0)), pl.BlockSpec((tm, D_MODEL), lambda i: (i, 1)),
                  pl.BlockSpec((tm, D_MODEL), lambda i: (i, 2)),
                  pl.BlockSpec((tm, D_MODEL), lambda i: (i, 0)),
                  w_spec, w_spec, w_spec, pl.BlockSpec((D_MODEL, D_MODEL), lambda i: (0, 0))],
        out_specs=pl.BlockSpec((tm, D_MODEL), lambda i: (i, 0)),
        out_shape=jax.ShapeDtypeStruct((T, D_MODEL), F32),
        compiler_params=pltpu.CompilerParams(dimension_semantics=("parallel",), vmem_limit_bytes=VMEM_LIMIT_BYTES),
        name="merge",
    )(ya, yb, yc, za, za, za, x, wa, wb, wc, wo)


def _ffn_kernel(*refs, routed):
    if routed:
        h_ref, g_ref, wr_ref, br_ref, wg_ref, wu_ref, wd_ref, o_ref, xn_ref, gate_ref = refs
    else:
        h_ref, g_ref, wg_ref, wu_ref, wd_ref, o_ref, xn_ref = refs
    e = pl.program_id(1)
    f = pl.program_id(2)

    @pl.when((e == 0) & (f == 0))
    def _():
        h = h_ref[...]
        xb = _rms_rows(h, g_ref[...]).astype(BF16)
        xn_ref[...] = xb
        o_ref[...] = h
        if routed:
            logits = jnp.dot(xb, wr_ref[...], preferred_element_type=F32) + br_ref[...]
            lane = lax.broadcasted_iota(jnp.int32, logits.shape, 1)
            m1 = jnp.max(logits, axis=1, keepdims=True)
            i1 = jnp.min(jnp.where(logits == m1, lane, LANES), axis=1, keepdims=True)
            rest = jnp.where(lane == i1, M_INIT, logits)
            m2 = jnp.max(rest, axis=1, keepdims=True)
            i2 = jnp.min(jnp.where(rest == m2, lane, LANES), axis=1, keepdims=True)
            r = jnp.exp(m2 - m1)
            gate_ref[...] = jnp.where(lane == i1, 1.0 / (1.0 + r), 0.0) + jnp.where(lane == i2, r / (1.0 + r), 0.0)

    xb = xn_ref[...]
    a = jnp.dot(xb, wg_ref[0], preferred_element_type=F32)
    u = jnp.dot(xb, wu_ref[0], preferred_element_type=F32)
    t = (a * jax.nn.sigmoid(a)) * u
    y = jnp.dot(t.astype(BF16), wd_ref[0], preferred_element_type=F32)
    if routed:
        lane = lax.broadcasted_iota(jnp.int32, gate_ref.shape, 1)
        y = jnp.sum(jnp.where(lane == e, gate_ref[...], 0.0), axis=1, keepdims=True) * y
    o_ref[...] += y


def channel_mixer(h, norm_gain, wg, wu, wd, router=None):
    T = h.shape[0]
    E, _, F = wg.shape
    tm = _pick(T, (512, 256, 128))
    tf = _pick(F, (1408, 1024, 512, 256, 128))
    routed = router is not None
    row = lambda i, e, f: (i, 0)
    in_specs = [pl.BlockSpec((tm, D_MODEL), row), pl.BlockSpec((1, D_MODEL), lambda i, e, f: (0, 0))]
    args = [h, norm_gain.astype(F32)[None]]
    scratch = [pltpu.VMEM((tm, D_MODEL), BF16)]
    if routed:
        in_specs += [pl.BlockSpec((D_MODEL, LANES), lambda i, e, f: (0, 0)), pl.BlockSpec((1, LANES), lambda i, e, f: (0, 0))]
        args += list(router)
        scratch.append(pltpu.VMEM((tm, LANES), F32))
    in_specs += [pl.BlockSpec((1, D_MODEL, tf), lambda i, e, f: (e, 0, f)),
                 pl.BlockSpec((1, D_MODEL, tf), lambda i, e, f: (e, 0, f)),
                 pl.BlockSpec((1, tf, D_MODEL), lambda i, e, f: (e, f, 0))]
    args += [wg, wu, wd]
    return pl.pallas_call(
        functools.partial(_ffn_kernel, routed=routed),
        grid=(T // tm, E, F // tf),
        in_specs=in_specs,
        out_specs=pl.BlockSpec((tm, D_MODEL), row),
        out_shape=jax.ShapeDtypeStruct((T, D_MODEL), F32),
        scratch_shapes=scratch,
        compiler_params=pltpu.CompilerParams(dimension_semantics=("parallel", "arbitrary", "arbitrary"),
                                             vmem_limit_bytes=VMEM_LIMIT_BYTES),
        name="moe" if routed else "ffn",
    )(*args)


def mixer_tail(x2, za, y_a, y_b, y_c, l, P):
    return merge(y_a, y_b, y_c, za, x2, P['w_br_rg'][l].astype(BF16), P['w_br_attn'][l].astype(BF16),
                 P['w_br_pool'][l].astype(BF16), P['w_out'][l].astype(BF16))


def prompt_mixer(x, l, P, packed, tables):
    B, T, _ = x.shape
    x2 = x.reshape(B * T, D_MODEL)
    za, q, kv_c, kv_s, kv_w, gn = projection(x2, P['attn_norm'][l], packed['proj'])
    zeros = lambda rows: jnp.zeros((B, rows, D_RNN), F32)
    y_a, y_c, conv_new, h_last, pool_new = mixer_seq(za, B, T, packed['mix'], zeros(CONV_W - 1), jnp.zeros((B, D_RNN), F32),
                                                     zeros(POOL_MAX - 1), 0)
    ck, cv = compress(kv_c.reshape(B, T, KV_ROW), packed['cmp'])
    y_b = nsa_prompt_pallas(q.reshape(B, T, Q_COLS), ck, cv, kv_s.reshape(B, T, KV_ROW), kv_w.reshape(B, T, KV_ROW),
                            gn.reshape(B, T, GN_COLS), tables)
    out = mixer_tail(x2, za, y_a, y_b.reshape(B * T, Q_COLS), y_c, l, P)
    kv_shape = (B, T, 2, KV_HEADS, HEAD_DIM)
    state = (kv_c.reshape(kv_shape), kv_s.reshape(kv_shape), kv_w.reshape(kv_shape)[:, -min(WINDOW, T):], conv_new,
             h_last[:, 0], pool_new)
    return out.reshape(B, T, D_MODEL), state


def sample_mixer(x, l, P, packed, past_len, conv_state, h0, pool_state, cmp_pool, sel_pool, win_buf, page_table,
                 step_tables):
    B, T, _ = x.shape
    x2 = x.reshape(B * T, D_MODEL)
    za, q, kv_c, kv_s, kv_w, gn = projection(x2, P['attn_norm'][l], packed['proj'])
    y_a, y_c, h_new = mixer_step(za, packed['mix'], conv_state, h0, pool_state, past_len)
    ck, cv = compress(cmp_pool.reshape(cmp_pool.shape[0], PAGE_SIZE, KV_ROW), packed['cmp'], page_table)
    kv_shape = (B, T, 2, KV_HEADS, HEAD_DIM)
    wb = win_buf.shape[1]
    y_b = nsa_step(q, ck, cv, kv_s, kv_w, sel_pool.reshape(sel_pool.shape[0], PAGE_SIZE, KV_ROW),
                   win_buf.reshape(B, wb, KV_ROW), gn, page_table, step_tables)
    win_new = jnp.concatenate([win_buf, kv_w.reshape(kv_shape)], axis=1)[:, -wb:]
    out = mixer_tail(x2, za, y_a, y_b, y_c, l, P)
    conv_new = jnp.concatenate([conv_state[:, 1:], za[:, None, ZA_XRG:ZA_XRG + D_RNN]], axis=1)
    pool_new = jnp.concatenate([pool_state[:, 1:], za[:, None, ZA_XPOOL:ZA_XPOOL + D_POOL]], axis=1)
    state = (kv_c.reshape(kv_shape), kv_s.reshape(kv_shape), win_new, conv_new, h_new, pool_new)
    return out.reshape(B, T, D_MODEL), state


def ffn_layer(x, l, P, W):
    B, T, _ = x.shape
    i = l // 2
    if l % 2 == 0:
        y = channel_mixer(x.reshape(B * T, D_MODEL), P['ffn_norm'][l], W['ffn_g'][i], W['ffn_u'][i], W['ffn_d'][i])
    else:
        y = channel_mixer(x.reshape(B * T, D_MODEL), P['ffn_norm'][l], W['moe_g'][i], W['moe_u'][i], W['moe_d'][i],
                          router=W['router'][i])
    return y.reshape(B, T, D_MODEL)


def kernel(x_prompt, x_sample, cache_cmp_kv, cache_sel_kv, cache_win_kv, state_conv, state_rg_h, state_pool,
           page_table, attn_norm, w_in, conv_w, conv_b, rg_w_a, rg_b_a, rg_w_x, rg_b_x, rg_lambda, q_norm, k_norm,
           cmp_pe, w_cmp1, w_cmp2, rel_bias, w_pool, pool_scale, w_br_rg, w_br_attn, w_br_pool, w_out, ffn_norm,
           ffn_w_gate, ffn_w_up, ffn_w_down, w_router, b_router, moe_w_gate, moe_w_up, moe_w_down):
    P = dict(attn_norm=attn_norm, conv_w=conv_w, conv_b=conv_b, rg_w_a=rg_w_a, rg_b_a=rg_b_a,
             rg_w_x=rg_w_x, rg_b_x=rg_b_x, rg_lambda=rg_lambda, w_pool=w_pool,
             pool_scale=pool_scale, w_br_rg=w_br_rg, w_br_attn=w_br_attn, w_br_pool=w_br_pool, w_out=w_out,
             ffn_norm=ffn_norm)
    depth = w_in.shape[0]
    n_moe = w_router.shape[0]
    pad_e = LANES - N_EXPERTS
    W = dict(
        ffn_g=[w[None].astype(BF16) for w in ffn_w_gate], ffn_u=[w[None].astype(BF16) for w in ffn_w_up],
        ffn_d=[w[None].astype(BF16) for w in ffn_w_down],
        moe_g=[w.astype(BF16) for w in moe_w_gate], moe_u=[w.astype(BF16) for w in moe_w_up],
        moe_d=[w.astype(BF16) for w in moe_w_down],
        router=[(jnp.pad(w_router[i], ((0, 0), (0, pad_e))).astype(BF16),
                 jnp.pad(b_router[i].astype(F32), (0, pad_e), constant_values=NEG)[None]) for i in range(n_moe)])
    past_len = page_table.shape[1] * PAGE_SIZE
    y_p, y_s = x_prompt, x_sample
    tables = rel_bias_tables(rel_bias, x_prompt.shape[1])
    step_tables = step_bias_tables(rel_bias, past_len, cache_win_kv.shape[2], past_len // CMP_STRIDE)
    p_list, s_list = [], []
    for l in range(depth):
        packed = dict(proj=pack_projection(w_in[l], q_norm[l], k_norm[l, 1], k_norm[l, 2]),
                      cmp=pack_compress(w_cmp1[l], w_cmp2[l], cmp_pe[l], k_norm[l, 0]),
                      mix=pack_mixer(conv_w[l], conv_b[l], rg_w_a[l], rg_b_a[l], rg_w_x[l], rg_b_x[l], rg_lambda[l],
                                     w_pool[l], pool_scale[l]))
        y_p, st_p = prompt_mixer(y_p, l, P, packed, tables)
        y_p = ffn_layer(y_p, l, P, W)
        p_list.append(st_p)
        y_s, st_s = sample_mixer(y_s, l, P, packed, past_len, state_conv[l], state_rg_h[l], state_pool[l],
                                 cache_cmp_kv[l], cache_sel_kv[l], cache_win_kv[l], page_table, step_tables)
        y_s = ffn_layer(y_s, l, P, W)
        s_list.append(st_s)
    p_cmp_kv, p_sel_kv, p_win_kv, p_conv, p_h, p_pool = [jnp.stack(a) for a in zip(*p_list)]
    s_cmp_kv, s_sel_kv, s_win_kv, s_conv, s_h, s_pool = [jnp.stack(a) for a in zip(*s_list)]
    return (y_p, y_s, p_cmp_kv, p_sel_kv, p_win_kv, p_conv, p_h, p_pool,
            s_cmp_kv, s_sel_kv, s_win_kv, s_conv, s_h, s_pool)
```

```python
import math
import functools

import jax
import jax.numpy as jnp
import numpy as np
from jax import lax
from jax.experimental import pallas as pl
from jax.experimental.pallas import tpu as pltpu

D_MODEL = 1024
PAGE_SIZE = 128
F32 = jnp.float32
BF16 = jnp.bfloat16
EPS = 1e-6
NEG = -1e30
FORCE = 1e4
D_RNN = 512
RG_BLOCKS = 8
RG_BW = D_RNN // RG_BLOCKS
CONV_W = 4
RG_C = 8.0
N_HEADS = 8
KV_HEADS = 2
HPG = N_HEADS // KV_HEADS
HEAD_DIM = 64
L_CMP = 32
CMP_STRIDE = 16
CMP_HIDDEN = 256
SEL_BLOCK = 64
N_SELECT = 16
WINDOW = 512
Q_BLOCK = 128
D_POOL = 512
POOL_WINDOWS = (2, 4, 8, 16)
POOL_GROUPS = 4
POOL_GW = D_POOL // POOL_GROUPS
POOL_MAX = 16
REL_BUCKETS = 32
REL_MAX_DIST = 1024
N_EXPERTS = 8
TOP_K = 2
KV_ROW = 2 * KV_HEADS * HEAD_DIM
SPLITS = (D_RNN, D_RNN, N_HEADS * HEAD_DIM, KV_ROW, KV_ROW, KV_ROW, 3 * N_HEADS, D_POOL, 3 * D_MODEL)

VMEM_LIMIT_BYTES = 52 * 1024 * 1024
LANES = 128
SUBLANES = 8
M_INIT = -3e38


def _pick(n, cands):
    for c in cands:
        if n % c == 0:
            return c
    return n


def _nt(a, b):
    return lax.dot_general(a, b, (((1,), (1,)), ((), ())), preferred_element_type=F32)


def _rms_rows(x, g):
    return x * lax.rsqrt(jnp.mean(x * x, axis=-1, keepdims=True) + EPS) * g


def rms_norm(x, g):
    xf = x.astype(F32)
    y = xf * lax.rsqrt(jnp.mean(xf * xf, axis=-1, keepdims=True) + EPS)
    return (y * g.astype(F32)).astype(x.dtype)


def rel_bucket(dist):
    n_exact = REL_BUCKETS // 2
    d = jnp.maximum(dist, 0)
    df = jnp.maximum(d, 1).astype(F32)
    large = n_exact + (jnp.log(df / n_exact) / math.log(REL_MAX_DIST / n_exact)
                       * (REL_BUCKETS - n_exact)).astype(jnp.int32)
    return jnp.where(d < n_exact, d, jnp.minimum(large, REL_BUCKETS - 1))


def masked_probs(logits, valid):
    logits = jnp.where(valid, logits, NEG)
    m = jnp.max(logits, axis=-1, keepdims=True)
    p = jnp.where(valid, jnp.exp(logits - m), 0.0)
    return p / jnp.maximum(jnp.sum(p, axis=-1, keepdims=True), 1e-30)


def causal_conv(x, buf, w, b):
    xp = jnp.concatenate([buf.astype(x.dtype), x], axis=1)
    y = lax.conv_general_dilated(xp, w[:, None, :].astype(x.dtype), (1,), 'VALID',
                                 dimension_numbers=('NWC', 'WIO', 'NWC'),
                                 feature_group_count=x.shape[-1]) + b
    return y, xp[:, -(CONV_W - 1):]


def rg_lru(xc, h0, w_a, b_a, w_x, b_x, lam):
    B, T, _ = xc.shape
    xb = xc.reshape(B, T, RG_BLOCKS, RG_BW)
    r = jax.nn.sigmoid((jnp.einsum('btnc,ncd->btnd', xb, w_a).reshape(B, T, D_RNN) + b_a).astype(F32))
    i = jax.nn.sigmoid((jnp.einsum('btnc,ncd->btnd', xb, w_x).reshape(B, T, D_RNN) + b_x).astype(F32))
    log_a = -RG_C * r * jax.nn.softplus(-lam.astype(F32))
    a = jnp.exp(log_a)
    u = jnp.sqrt(-jnp.expm1(2.0 * log_a)) * (i * xc.astype(F32))
    u = u.at[:, 0].add(a[:, 0] * h0.astype(F32))

    def combine(lhs, rhs):
        a1, b1 = lhs
        a2, b2 = rhs
        return a1 * a2, a2 * b1 + b2

    _, h = lax.associative_scan(combine, (a, u), axis=1)
    return h, h[:, -1]


def pool_mix(xin, buf, start_pos, w_pool, scale):
    B, T, C = xin.shape
    xf = jnp.concatenate([buf.astype(xin.dtype), xin], axis=1).astype(F32)
    cs = jnp.concatenate([jnp.zeros((B, 1, C), F32), jnp.cumsum(xf, axis=1)], axis=1)
    pos = start_pos + jnp.arange(T)
    means = []
    for g, w in enumerate(POOL_WINDOWS):
        sl = slice(g * POOL_GW, (g + 1) * POOL_GW)
        s = cs[:, POOL_MAX:POOL_MAX + T, sl] - cs[:, POOL_MAX - w:POOL_MAX - w + T, sl]
        cnt = jnp.minimum(pos + 1, w).astype(F32)[None, :, None]
        means.append(s / cnt)
    mixed = (jnp.concatenate(means, axis=-1) - xf[:, POOL_MAX - 1:]).astype(xin.dtype)
    y = jnp.einsum('btgc,gcd->btgd', mixed.reshape(B, T, POOL_GROUPS, POOL_GW), w_pool)
    return y.reshape(B, T, D_POOL) * scale, xf[:, -(POOL_MAX - 1):].astype(xin.dtype)


def compress_kv(rows, w1, w2, pe, kn):
    Bx, T = rows.shape[:2]
    n_ch = T // CMP_STRIDE
    ch = rows[:, :n_ch * CMP_STRIDE].reshape(Bx, n_ch, CMP_STRIDE, 2, KV_HEADS, HEAD_DIM)
    first = jnp.einsum('bnsegd,esdf->bnegf', ch, w1[:, :CMP_STRIDE])
    second = jnp.einsum('bnsegd,esdf->bnegf', ch, w1[:, CMP_STRIDE:])
    pe_term = jnp.einsum('led,eldf->ef', pe, w1)
    h = first[:, :-1] + second[:, 1:] + pe_term[:, None, :]
    comp = jnp.einsum('bnegf,efd->bnegd', jax.nn.gelu(h), w2)
    nc = comp.shape[1]
    comp_end = jnp.arange(nc) * CMP_STRIDE + L_CMP - 1
    return rms_norm(comp[:, :, 0], kn), comp[:, :, 1], comp_end


def overlap_matrix(n_cmp, n_sel):
    c0 = jnp.arange(n_cmp)[:, None] * CMP_STRIDE
    s0 = jnp.arange(n_sel)[None, :] * SEL_BLOCK
    return ((c0 <= s0 + SEL_BLOCK - 1) & (c0 + L_CMP - 1 >= s0)).astype(F32)


def nsa_attend(q, q_pos, comp_k, comp_v, comp_end, overlap, fetch_sel, win_k, win_v, win_pos, gates, rel_bias):
    Bq, Tq = q.shape[:2]
    scale = HEAD_DIM ** -0.5
    qg = q.reshape(Bq, Tq, KV_HEADS, HPG, HEAD_DIM)
    rb = rel_bias.astype(F32)

    def head_bias(buckets):
        return jnp.moveaxis(rb[buckets], -1, 1).reshape(Tq, KV_HEADS, HPG, -1)

    dist_c = q_pos[:, None] - comp_end[None, :]
    lc = jnp.einsum('btgjd,bcgd->btgjc', qg, comp_k, preferred_element_type=F32) * scale + head_bias(rel_bucket(dist_c))
    pc = masked_probs(lc, (dist_c >= 0)[None, :, None, None, :])
    o_cmp = jnp.einsum('btgjc,bcgd->btgjd', pc, comp_v)
    ns = overlap.shape[1]
    imp = jnp.einsum('btgjc,cn->btgn', pc, overlap)
    blk = jnp.arange(ns)
    cur = (q_pos // SEL_BLOCK)[None, :, None, None]
    forced = (blk == 0) | (blk == cur) | (blk == cur - 1)
    score = jnp.where(blk > cur, -1.0, jnp.where(forced, FORCE, imp))
    n_sel = min(N_SELECT, ns)
    _, idx = lax.top_k(score, n_sel)
    kv_s = fetch_sel(idx)
    k_s = kv_s[..., 0, :].reshape(Bq, Tq, KV_HEADS, n_sel * SEL_BLOCK, HEAD_DIM)
    v_s = kv_s[..., 1, :].reshape(Bq, Tq, KV_HEADS, n_sel * SEL_BLOCK, HEAD_DIM)
    pos_s = (idx[..., None] * SEL_BLOCK + jnp.arange(SEL_BLOCK)).reshape(Bq, Tq, KV_HEADS, -1)
    dist_s = q_pos[None, :, None, None] - pos_s
    bias_s = rb.reshape(REL_BUCKETS, KV_HEADS, HPG)[rel_bucket(dist_s), jnp.arange(KV_HEADS)[:, None]]
    ls = jnp.einsum('btgjd,btgkd->btgjk', qg, k_s, preferred_element_type=F32) * scale + jnp.moveaxis(bias_s, -1, 3)
    ps = masked_probs(ls, (dist_s >= 0)[:, :, :, None, :])
    o_sel = jnp.einsum('btgjk,btgkd->btgjd', ps, v_s)
    dist_w = q_pos[:, None] - win_pos[None, :]
    lw = jnp.einsum('btgjd,bwgd->btgjw', qg, win_k, preferred_element_type=F32) * scale + head_bias(rel_bucket(dist_w))
    valid_w = (dist_w >= 0) & (dist_w <= WINDOW) & (win_pos[None, :] >= 0)
    pw = masked_probs(lw, valid_w[None, :, None, None, :])
    o_win = jnp.einsum('btgjw,bwgd->btgjd', pw, win_v)
    g = jax.nn.sigmoid(gates.astype(F32)).reshape(Bq, Tq, KV_HEADS, HPG, 3)
    o = g[..., 0:1] * o_cmp + g[..., 1:2] * o_sel + g[..., 2:3] * o_win
    return o.reshape(Bq, Tq, N_HEADS * HEAD_DIM).astype(q.dtype)


def nsa_sample(q, comp_k, comp_v, kv_sel_new, kv_win_new, gates, *, sel_pool, win_buf, page_table, rel_bias):
    DB, DS = q.shape[:2]
    n_pages = page_table.shape[1]
    past = n_pages * PAGE_SIZE
    q_pos = past + jnp.arange(DS)
    comp_end = jnp.arange(comp_k.shape[1]) * CMP_STRIDE + L_CMP - 1
    total = past + DS
    ns = -(-total // SEL_BLOCK)
    nbp = past // SEL_BLOCK
    nnb = ns - nbp
    bpp = PAGE_SIZE // SEL_BLOCK
    ovl = overlap_matrix(comp_k.shape[1], ns)
    pool_blocks = sel_pool.reshape(-1, SEL_BLOCK, 2, KV_HEADS, HEAD_DIM)
    new_pad = jnp.pad(kv_sel_new.astype(sel_pool.dtype), ((0, 0), (0, nnb * SEL_BLOCK - DS), (0, 0), (0, 0), (0, 0)))
    new_blocks = new_pad.reshape(DB, nnb, SEL_BLOCK, 2, KV_HEADS, HEAD_DIM)
    b_idx = jnp.arange(DB)[:, None, None, None]
    g_idx = jnp.arange(KV_HEADS)[None, None, :, None]

    def fetch(idx):
        ip = jnp.minimum(idx, nbp - 1)
        phys = page_table[b_idx, ip // bpp] * bpp + ip % bpp
        from_past = pool_blocks[phys, :, :, g_idx]
        from_new = new_blocks[b_idx, jnp.clip(idx - nbp, 0, nnb - 1), :, :, g_idx]
        return jnp.where((idx >= nbp)[..., None, None, None], from_new, from_past)

    wb = win_buf.shape[1]
    win = jnp.concatenate([win_buf, kv_win_new.astype(win_buf.dtype)], axis=1)
    win_pos = past - wb + jnp.arange(wb + DS)
    out = nsa_attend(q, q_pos, comp_k, comp_v, comp_end, ovl, fetch, win[:, :, 0], win[:, :, 1], win_pos, gates, rel_bias)
    return out, win[:, -wb:]


PROJ_TN = 512
ZA_COLS = 3 * D_MODEL + 2 * D_RNN + D_POOL
N_ZA = ZA_COLS // PROJ_TN
Q_COLS = N_HEADS * HEAD_DIM
GN_COLS = 2 * LANES
PROJ_COLS = ZA_COLS + Q_COLS + 3 * KV_ROW + GN_COLS
ZA_XRG, ZA_GRG, ZA_XPOOL = 3 * D_MODEL, 3 * D_MODEL + D_RNN, 3 * D_MODEL + 2 * D_RNN


def _proj_kernel(x_ref, g_ref, w_ref, seg_ref, ng_ref, nm_ref, za_ref, q_ref, kvc_ref, kvs_ref, kvw_ref, gn_ref, xn_ref):
    j = pl.program_id(1)

    @pl.when(j == 0)
    def _():
        xn_ref[...] = _rms_rows(x_ref[...], g_ref[...]).astype(BF16)

    acc = jnp.dot(xn_ref[...], w_ref[...], preferred_element_type=F32)

    @pl.when(j < N_ZA)
    def _():
        za_ref[...] = acc

    @pl.when(j >= N_ZA)
    def _():
        sq = acc * acc
        hi = sq.astype(BF16)
        lo = (sq - hi.astype(F32)).astype(BF16)
        ss = (jnp.dot(hi, seg_ref[...], preferred_element_type=F32)
              + jnp.dot(lo, seg_ref[...], preferred_element_type=F32))
        normed = acc * lax.rsqrt(ss * (1.0 / HEAD_DIM) + EPS) * ng_ref[...]
        y = jnp.where(nm_ref[...] > 0.5, normed, acc)

        @pl.when(j == N_ZA)
        def _():
            q_ref[...] = y.astype(BF16)

        @pl.when(j == N_ZA + 1)
        def _():
            kvc_ref[...] = y[:, :KV_ROW]
            kvs_ref[...] = y[:, KV_ROW:]

        @pl.when(j == N_ZA + 2)
        def _():
            kvw_ref[...] = y[:, :KV_ROW]
            gn_ref[...] = y[:, KV_ROW:]


def pack_projection(w_in, q_gain, ks_gain, kw_gain):
    cut = np.cumsum(SPLITS)[:-1].tolist()
    x_rg, g_rg, q, kv_c, kv_s, kv_w, g_nsa, x_pool, g_br = jnp.split(w_in, cut, axis=1)
    per_group = 3 * HPG
    gn = jnp.zeros((w_in.shape[0], GN_COLS), w_in.dtype)
    for g in range(KV_HEADS):
        gn = gn.at[:, g * LANES:g * LANES + per_group].set(g_nsa[:, g * per_group:(g + 1) * per_group])
    w = jnp.concatenate([g_br, x_rg, g_rg, x_pool, q, kv_c, kv_s, kv_w, gn], axis=1).astype(BF16)
    ones_v = jnp.ones((KV_HEADS * HEAD_DIM,), F32)
    zeros_v = jnp.zeros((KV_HEADS * HEAD_DIM,), F32)
    gain = jnp.concatenate([jnp.ones((ZA_COLS,), F32), jnp.tile(q_gain.astype(F32), N_HEADS) * HEAD_DIM ** -0.5,
                            jnp.ones((KV_ROW,), F32),
                            jnp.tile(ks_gain.astype(F32), KV_HEADS), ones_v,
                            jnp.tile(kw_gain.astype(F32), KV_HEADS), ones_v,
                            jnp.ones((GN_COLS,), F32)])
    mask = jnp.concatenate([jnp.zeros((ZA_COLS,), F32), jnp.ones((Q_COLS,), F32), jnp.zeros((KV_ROW,), F32),
                            ones_v, zeros_v, ones_v, zeros_v, jnp.zeros((GN_COLS,), F32)])
    return w, gain[None], mask[None]


def projection(x, norm_gain, packed):
    w, gain, mask = packed
    T = x.shape[0]
    tm = _pick(T, (1024, 512, 256, 128))
    seg = jnp.asarray((np.arange(PROJ_TN)[:, None] // HEAD_DIM == np.arange(PROJ_TN)[None, :] // HEAD_DIM)
                      .astype(np.float32), BF16)
    row = lambda i, j: (i, 0)
    return pl.pallas_call(
        _proj_kernel,
        grid=(T // tm, PROJ_COLS // PROJ_TN),
        in_specs=[pl.BlockSpec((tm, D_MODEL), row),
                  pl.BlockSpec((1, D_MODEL), lambda i, j: (0, 0)),
                  pl.BlockSpec((D_MODEL, PROJ_TN), lambda i, j: (0, j)),
                  pl.BlockSpec((PROJ_TN, PROJ_TN), lambda i, j: (0, 0)),
                  pl.BlockSpec((1, PROJ_TN), lambda i, j: (0, j)),
                  pl.BlockSpec((1, PROJ_TN), lambda i, j: (0, j))],
        out_specs=[pl.BlockSpec((tm, PROJ_TN), lambda i, j: (i, jnp.minimum(j, N_ZA - 1))),
                   pl.BlockSpec((tm, Q_COLS), row),
                   pl.BlockSpec((tm, KV_ROW), row), pl.BlockSpec((tm, KV_ROW), row), pl.BlockSpec((tm, KV_ROW), row),
                   pl.BlockSpec((tm, GN_COLS), row)],
        out_shape=[jax.ShapeDtypeStruct((T, ZA_COLS), F32), jax.ShapeDtypeStruct((T, Q_COLS), BF16),
                   jax.ShapeDtypeStruct((T, KV_ROW), F32), jax.ShapeDtypeStruct((T, KV_ROW), F32),
                   jax.ShapeDtypeStruct((T, KV_ROW), F32), jax.ShapeDtypeStruct((T, GN_COLS), F32)],
        scratch_shapes=[pltpu.VMEM((tm, D_MODEL), BF16)],
        compiler_params=pltpu.CompilerParams(dimension_semantics=("parallel", "arbitrary"),
                                             vmem_limit_bytes=VMEM_LIMIT_BYTES),
        name="projection",
    )(x, norm_gain.astype(F32)[None], w, seg, gain, mask)


CMP_PAIRS = CMP_STRIDE // 2
CMP_GW = KV_HEADS * CMP_HIDDEN


def _compress_kernel(*refs, n_pages, prefetch):
    refs = refs[1:] if prefetch else refs
    pages = refs[:n_pages]
    w1_ref, pe_ref, w2_ref, kn_ref, ck_ref, cv_ref = refs[n_pages:]
    lead = (0,) * (len(pages[0].shape) - 2)
    cpp = pages[0].shape[-2] // (2 * CMP_STRIDE)
    m = n_pages * cpp
    for e, out_ref in enumerate((ck_ref, cv_ref)):
        acc = jnp.zeros((m, 2 * CMP_GW), F32)
        for p in range(CMP_PAIRS):
            def rows(s):
                return jnp.concatenate([pg[lead + (pl.ds(2 * s + e, cpp, stride=2 * CMP_STRIDE), slice(None))]
                                        for pg in pages], axis=0)
            a = jnp.concatenate([rows(2 * p), rows(2 * p + 1)], axis=1).astype(BF16)
            acc = acc + jnp.dot(a, w1_ref[e, p], preferred_element_type=F32)
        h = acc[:, :CMP_GW] + pltpu.roll(acc[:, CMP_GW:], m - 1, axis=0) + pe_ref[e]
        gl = jax.nn.gelu(h).astype(BF16)
        for g in range(KV_HEADS):
            c = jnp.dot(gl, w2_ref[e, g], preferred_element_type=F32)
            if e == 0:
                c = c * lax.rsqrt(jnp.sum(c * c, axis=1, keepdims=True) * (1.0 / HEAD_DIM) + EPS) * kn_ref[...]
            out_ref[0, g] = c.astype(out_ref.dtype)


def pack_compress(w1, w2, pe, kn):
    halves = w1.reshape(2, 2, CMP_STRIDE, HEAD_DIM, CMP_HIDDEN)
    eye = jnp.eye(KV_HEADS, dtype=w1.dtype)
    bd = jnp.einsum('ehsdf,gk->esgdhkf', halves, eye)
    w1p = bd.reshape(2, CMP_PAIRS, 2 * KV_HEADS * HEAD_DIM, 2 * CMP_GW).astype(BF16)
    pe_term = jnp.einsum('led,eldf->ef', pe, w1)
    pe_t = jnp.tile(pe_term, (1, KV_HEADS))[:, None, :].astype(F32)
    w2p = jnp.zeros((2, KV_HEADS, CMP_GW, LANES), w2.dtype)
    for g in range(KV_HEADS):
        w2p = w2p.at[:, g, g * CMP_HIDDEN:(g + 1) * CMP_HIDDEN, :HEAD_DIM].set(w2)
    knp = jnp.concatenate([kn.astype(F32), jnp.zeros((LANES - HEAD_DIM,), F32)])[None]
    return w1p, pe_t, w2p.astype(BF16), knp


def compress(rows, packed, page_table=None, layer=0):
    w1p, pe_t, w2p, knp = packed
    page = rows.shape[-2] // 2
    if page_table is None:
        bx, n_pages = rows.shape[0], 1
        page_specs = [pl.BlockSpec((1, 2 * page, LANES), lambda b: (b, 0, 0))]
        const = lambda nd: (lambda b: (0,) * nd)
        out_map = lambda b: (b, 0, 0, 0)
        prefetch = 0
    else:
        bx, n_pages = page_table.shape
        page_specs = [pl.BlockSpec((1, 1, 2 * page, LANES), lambda b, pt, k=k: (layer, pt[b, k], 0, 0))
                      for k in range(n_pages)]
        const = lambda nd: (lambda b, pt: (0,) * nd)
        out_map = lambda b, pt: (b, 0, 0, 0)
        prefetch = 1
    m = n_pages * page // CMP_STRIDE
    in_specs = page_specs + [pl.BlockSpec(w1p.shape, const(4)), pl.BlockSpec(pe_t.shape, const(3)),
                             pl.BlockSpec(w2p.shape, const(4)), pl.BlockSpec(knp.shape, const(2))]
    out_spec = pl.BlockSpec((1, KV_HEADS, m, LANES), out_map)
    out_shape = jax.ShapeDtypeStruct((bx, KV_HEADS, m, LANES), BF16)
    grid_spec = pltpu.PrefetchScalarGridSpec(num_scalar_prefetch=prefetch, grid=(bx,), in_specs=in_specs,
                                             out_specs=[out_spec, out_spec])
    args = ([page_table] if prefetch else []) + [rows] * n_pages + [w1p, pe_t, w2p, knp]
    return pl.pallas_call(
        functools.partial(_compress_kernel, n_pages=n_pages, prefetch=prefetch),
        grid_spec=grid_spec,
        out_shape=[out_shape, out_shape],
        compiler_params=pltpu.CompilerParams(dimension_semantics=("parallel",), vmem_limit_bytes=VMEM_LIMIT_BYTES),
        name="compress",
    )(*args)


MIX_TT = 512
CONV_HALO = SUBLANES
SCAN_UNROLL = 4


def _log1p(y):
    u = 1.0 + y
    return jnp.where(u == 1.0, y, jnp.log(u) * (y / jnp.where(u == 1.0, 1.0, u - 1.0)))


def _neg_expm1(x):
    t = jnp.tanh(0.5 * x)
    return -2.0 * t / (1.0 - t)


def _softplus(x):
    return jnp.maximum(x, 0.0) + _log1p(jnp.exp(-jnp.abs(x)))


def _rglru_coeffs(xc, wa_ref, ba_ref, wx_ref, bx_ref, lam_ref):
    xb = xc.astype(BF16)
    r = jax.nn.sigmoid(jnp.dot(xb, wa_ref[...], preferred_element_type=F32) + ba_ref[...])
    i = jax.nn.sigmoid(jnp.dot(xb, wx_ref[...], preferred_element_type=F32) + bx_ref[...])
    log_a = -RG_C * r * _softplus(-lam_ref[...])
    return jnp.exp(log_a), jnp.sqrt(_neg_expm1(2.0 * log_a)) * (i * xc)


def _pool_project(sums_minus, wp_ref, scale_ref):
    return jnp.dot(sums_minus.astype(BF16), wp_ref[...], preferred_element_type=F32) * scale_ref[...]


def _mixer_seq_kernel(xrg_ref, grg_ref, xpool_ref, conv0_ref, h0_ref, pool0_ref, cw_ref, cb_ref, wa_ref, ba_ref,
                      wx_ref, bx_ref, lam_ref, wp_ref, ps_ref, ya_ref, yc_ref, convn_ref, hn_ref, pooln_ref,
                      xe_ref, pe_ref, a_ref, u_ref, h_ref, carry_ref, *, start_pos):
    i = pl.program_id(1)
    tt = xrg_ref.shape[0]

    @pl.when(i == 0)
    def _():
        xe_ref[0:CONV_HALO, :] = conv0_ref[0]
        pe_ref[0:POOL_MAX, :] = pool0_ref[0]
        carry_ref[...] = h0_ref[0]

    xe_ref[CONV_HALO:CONV_HALO + tt, :] = xrg_ref[...]
    xc = cb_ref[...] + sum(cw_ref[k:k + 1, :] * xe_ref[CONV_HALO - (CONV_W - 1) + k:CONV_HALO - (CONV_W - 1) + k + tt, :]
                           for k in range(CONV_W))
    a, u = _rglru_coeffs(xc, wa_ref, ba_ref, wx_ref, bx_ref, lam_ref)
    a_ref[...] = a
    u_ref[...] = u

    row = lax.broadcasted_iota(jnp.int32, (SUBLANES, D_RNN), 0)

    def block(j, carry):
        r0 = pl.multiple_of(j * SUBLANES, SUBLANES)
        ab = a_ref[pl.ds(r0, SUBLANES), :]
        ub = u_ref[pl.ds(r0, SUBLANES), :]
        for d in (1, 2, 4):
            a_sh = jnp.where(row >= d, pltpu.roll(ab, d, axis=0), 1.0)
            u_sh = jnp.where(row >= d, pltpu.roll(ub, d, axis=0), 0.0)
            ub = ab * u_sh + ub
            ab = ab * a_sh
        hb = ab * carry + ub
        h_ref[pl.ds(r0, SUBLANES), :] = hb
        return jnp.broadcast_to(hb[SUBLANES - 1:SUBLANES, :], (SUBLANES, D_RNN))

    carry = lax.fori_loop(0, tt // SUBLANES, block, carry_ref[...], unroll=SCAN_UNROLL)
    carry_ref[...] = carry
    ya_ref[...] = (h_ref[...] * jax.nn.gelu(grg_ref[...])).astype(ya_ref.dtype)

    pe_ref[POOL_MAX:POOL_MAX + tt, :] = xpool_ref[...]
    pos = start_pos + i * tt + lax.broadcasted_iota(jnp.int32, (tt, 1), 0)
    parts = []
    for g, w in enumerate(POOL_WINDOWS):
        lanes = slice(g * POOL_GW, (g + 1) * POOL_GW)
        s = sum(pe_ref[POOL_MAX - k:POOL_MAX - k + tt, lanes] for k in range(w))
        cnt = jnp.minimum(pos + 1, w).astype(F32)
        parts.append(s / cnt - pe_ref[POOL_MAX:POOL_MAX + tt, lanes])
    yc_ref[...] = _pool_project(jnp.concatenate(parts, axis=1), wp_ref, ps_ref).astype(yc_ref.dtype)

    @pl.when(i == pl.num_programs(1) - 1)
    def _():
        convn_ref[0] = xe_ref[CONV_HALO + tt - (CONV_W - 1):CONV_HALO + tt, :]
        hn_ref[0] = carry[0:1, :]
        pooln_ref[0] = pe_ref[POOL_MAX + tt - (POOL_MAX - 1):POOL_MAX + tt, :]

    xe_ref[0:CONV_HALO, :] = xe_ref[tt:tt + CONV_HALO, :]
    pe_ref[0:POOL_MAX, :] = pe_ref[tt:tt + POOL_MAX, :]


def pack_mixer(conv_w, conv_b, w_a, b_a, w_x, b_x, lam, w_pool, scale):
    def block_diag(w):
        n, c, d = w.shape
        return jnp.einsum('ncd,nm->ncmd', w, jnp.eye(n, dtype=w.dtype)).reshape(n * c, n * d).astype(BF16)

    row = lambda v: v.astype(F32)[None]
    return (conv_w.astype(F32), row(conv_b), block_diag(w_a), row(b_a), block_diag(w_x), row(b_x), row(lam),
            block_diag(w_pool), row(scale))


def mixer_seq(za, batch, seq, packed, conv0, h0, pool0, start_pos):
    tt = min(MIX_TT, seq)
    nt = seq // tt
    conv_pad = jnp.pad(conv0.astype(F32), ((0, 0), (CONV_HALO - (CONV_W - 1), 0), (0, 0)))
    pool_pad = jnp.pad(pool0.astype(F32), ((0, 0), (1, 0), (0, 0)))
    h_pad = jnp.broadcast_to(h0.astype(F32)[:, None, :], (batch, SUBLANES, D_RNN))
    col = lambda c: pl.BlockSpec((tt, D_RNN), lambda b, i: (b * nt + i, c))
    state = lambda rows: pl.BlockSpec((1, rows, D_RNN), lambda b, i: (b, 0, 0))
    full = lambda a: pl.BlockSpec(a.shape, lambda b, i: (0,) * a.ndim)
    out_rows = pl.BlockSpec((tt, D_RNN), lambda b, i: (b * nt + i, 0))
    return pl.pallas_call(
        functools.partial(_mixer_seq_kernel, start_pos=start_pos),
        grid=(batch, nt),
        in_specs=[col(ZA_XRG // D_RNN), col(ZA_GRG // D_RNN), col(ZA_XPOOL // D_RNN), state(CONV_HALO), state(SUBLANES),
                  state(POOL_MAX)] + [full(a) for a in packed],
        out_specs=[out_rows, out_rows, state(CONV_W - 1), state(1), state(POOL_MAX - 1)],
        out_shape=[jax.ShapeDtypeStruct((batch * seq, D_RNN), BF16), jax.ShapeDtypeStruct((batch * seq, D_POOL), BF16),
                   jax.ShapeDtypeStruct((batch, CONV_W - 1, D_RNN), F32), jax.ShapeDtypeStruct((batch, 1, D_RNN), F32),
                   jax.ShapeDtypeStruct((batch, POOL_MAX - 1, D_POOL), F32)],
        scratch_shapes=[pltpu.VMEM((CONV_HALO + tt, D_RNN), F32), pltpu.VMEM((POOL_MAX + tt, D_POOL), F32),
                        pltpu.VMEM((tt, D_RNN), F32), pltpu.VMEM((tt, D_RNN), F32), pltpu.VMEM((tt, D_RNN), F32),
                        pltpu.VMEM((SUBLANES, D_RNN), F32)],
        compiler_params=pltpu.CompilerParams(dimension_semantics=("parallel", "arbitrary"),
                                             vmem_limit_bytes=VMEM_LIMIT_BYTES),
        name="mixer_seq",
    )(za, za, za, conv_pad, h_pad, pool_pad, *packed)


def _mixer_step_kernel(xrg_ref, grg_ref, xpool_ref, conv_ref, h0_ref, pool_ref, cw_ref, cb_ref, wa_ref, ba_ref,
                       wx_ref, bx_ref, lam_ref, wp_ref, ps_ref, ya_ref, yc_ref, hn_ref, *, start_pos):
    x = xrg_ref[...]
    xc = cb_ref[...] + cw_ref[CONV_W - 1:CONV_W, :] * x + sum(cw_ref[k:k + 1, :] * conv_ref[k] for k in range(CONV_W - 1))
    a, u = _rglru_coeffs(xc, wa_ref, ba_ref, wx_ref, bx_ref, lam_ref)
    h = a * h0_ref[...] + u
    hn_ref[...] = h
    ya_ref[...] = (h * jax.nn.gelu(grg_ref[...])).astype(ya_ref.dtype)
    xp = xpool_ref[...]
    parts = []
    for g, w in enumerate(POOL_WINDOWS):
        lanes = slice(g * POOL_GW, (g + 1) * POOL_GW)
        s = xp[:, lanes] + sum(pool_ref[POOL_MAX - 1 - k][:, lanes] for k in range(1, w))
        parts.append(s / float(min(start_pos + 1, w)) - xp[:, lanes])
    yc_ref[...] = _pool_project(jnp.concatenate(parts, axis=1), wp_ref, ps_ref).astype(yc_ref.dtype)


def mixer_step(za, packed, conv_state, h0, pool_state, start_pos):
    batch = za.shape[0]
    conv_t = jnp.swapaxes(conv_state.astype(F32), 0, 1)
    pool_t = jnp.swapaxes(pool_state.astype(F32), 0, 1)
    col = lambda c: pl.BlockSpec((batch, D_RNN), lambda i: (0, c))
    full = lambda a: pl.BlockSpec(a.shape, lambda i: (0,) * a.ndim)
    rows = pl.BlockSpec((batch, D_RNN), lambda i: (0, 0))
    return pl.pallas_call(
        functools.partial(_mixer_step_kernel, start_pos=start_pos),
        grid=(1,),
        in_specs=[col(ZA_XRG // D_RNN), col(ZA_GRG // D_RNN), col(ZA_XPOOL // D_RNN), full(conv_t), rows, full(pool_t)]
        + [full(a) for a in packed],
        out_specs=[rows, rows, rows],
        out_shape=[jax.ShapeDtypeStruct((batch, D_RNN), BF16), jax.ShapeDtypeStruct((batch, D_POOL), BF16),
                   jax.ShapeDtypeStruct((batch, D_RNN), F32)],
        compiler_params=pltpu.CompilerParams(vmem_limit_bytes=VMEM_LIMIT_BYTES),
        name="mixer_step",
    )(za, za, za, conv_t, h0.astype(F32), pool_t, *packed)


QB = Q_BLOCK
ROWS = HPG * QB
N_WIN_TILES = WINDOW // QB + 1
N_SEL_BIAS = REL_MAX_DIST // QB + 2
SEL_SPAN = 4
KV_CHUNK = 512


def _nsa_prompt_kernel(q_ref, ck_ref, cv_ref, kvs_ref, kvw_ref, gate_ref, bc_ref, tbs_ref, tbw_ref,
                       ovl_ref, eye_ref, pq_ref, pk_ref, pv_ref, onehot_ref, o_ref,
                       acc_ref, m_ref, qa_ref, comb_ref, ks_ref, vs_ref, kw_ref, vw_ref, cnt_ref, *, n_cmp, n_blk):
    qb = pl.program_id(2)
    t0 = qb * QB
    ncp = ck_ref.shape[2]
    seq = kvs_ref.shape[1]
    lane_row = lax.broadcasted_iota(jnp.int32, (1, LANES), 1)

    @pl.when(qb == 0)
    def _():
        ones_hi = jnp.where(lane_row >= HEAD_DIM, 1.0, 0.0)
        kw_ref[0:WINDOW, :] = jnp.broadcast_to(jnp.where(lane_row == HEAD_DIM, 1.0, 0.0), (WINDOW, LANES)).astype(BF16)
        vw_ref[0:WINDOW, :] = jnp.zeros((WINDOW, LANES), BF16)

        def stage(c, carry):
            r = pl.multiple_of(c * KV_CHUNK, KV_CHUNK)
            sel = kvs_ref[0, pl.ds(r, KV_CHUNK), :].astype(BF16)
            win = kvw_ref[0, pl.ds(r, KV_CHUNK), :].astype(BF16)
            ks_ref[pl.ds(r, KV_CHUNK), :] = (jnp.dot(sel, pk_ref[0], preferred_element_type=F32)
                                             + onehot_ref[pl.ds(r, KV_CHUNK), :].astype(F32)).astype(BF16)
            vs_ref[pl.ds(r, KV_CHUNK), :] = (jnp.dot(sel, pv_ref[0], preferred_element_type=F32) + ones_hi).astype(BF16)
            kw_ref[pl.ds(WINDOW + r, KV_CHUNK), :] = jnp.dot(win, pk_ref[0], preferred_element_type=F32).astype(BF16)
            vw_ref[pl.ds(WINDOW + r, KV_CHUNK), :] = (jnp.dot(win, pv_ref[0], preferred_element_type=F32)
                                                      + ones_hi).astype(BF16)
            return carry

        lax.fori_loop(0, seq // KV_CHUNK, stage, 0)

    q4 = jnp.dot(q_ref[0], pq_ref[...], preferred_element_type=F32)
    q3 = jnp.concatenate([q4[:, j * LANES:(j + 1) * LANES] for j in range(HPG)], axis=0)
    q = q3.astype(BF16)
    sig = jax.nn.sigmoid(gate_ref[0])
    gates = [jnp.concatenate([sig[:, 3 * j + c:3 * j + c + 1] for j in range(HPG)], axis=0) for c in range(3)]

    lc = _nt(q, ck_ref[0, 0]) + bc_ref[0].reshape(ROWS, ncp)
    tok = t0 + lax.broadcasted_iota(jnp.int32, (HPG, QB, ncp), 1).reshape(ROWS, ncp)
    col = lax.broadcasted_iota(jnp.int32, (ROWS, ncp), 1)
    valid = (tok >= col * CMP_STRIDE + (L_CMP - 1)) & (col < n_cmp)
    lc = jnp.where(valid, lc, NEG)
    mx = jnp.max(lc, axis=1, keepdims=True)
    p = jnp.where(valid, jnp.exp(lc - mx), 0.0)
    pc = p / jnp.maximum(jnp.sum(p, axis=1, keepdims=True), 1e-30)
    comb_ref[...] = gates[0] * jnp.dot(pc.astype(BF16), cv_ref[0, 0], preferred_element_type=F32)

    pcs = pc[0:QB] + pc[QB:2 * QB] + pc[2 * QB:3 * QB] + pc[3 * QB:4 * QB]
    hi = pcs.astype(BF16)
    lo = (pcs - hi.astype(F32)).astype(BF16)
    imp = _nt(ovl_ref[...], hi) + _nt(ovl_ref[...], lo)
    blk = lax.broadcasted_iota(jnp.int32, (n_blk, QB), 0)
    cur = (t0 + lax.broadcasted_iota(jnp.int32, (n_blk, QB), 1)) // SEL_BLOCK
    forced = (blk == 0) | (blk == cur) | (blk == cur - 1)
    score = jnp.where(blk > cur, -1.0, jnp.where(forced, FORCE, imp))
    chunks = [score[r:r + SUBLANES] for r in range(0, n_blk, SUBLANES)]
    sub = lax.broadcasted_iota(jnp.int32, (SUBLANES, QB), 0)
    cnt_ref[...] = jnp.zeros((n_blk, QB), F32)
    last_blk = (t0 + QB - 1) // SEL_BLOCK
    for mc in range(0, n_blk, SUBLANES):
        @pl.when(mc <= last_blk)
        def _(mc=mc):
            for r, ch in enumerate(chunks):
                first = r * SUBLANES
                part = jnp.zeros((SUBLANES, QB), F32)
                for m in range(mc, mc + SUBLANES):
                    row = jnp.broadcast_to(score[m:m + 1, :], (SUBLANES, QB))
                    if first > m:
                        beats = jnp.where(row >= ch, 1.0, 0.0)
                    elif first + SUBLANES - 1 < m:
                        beats = jnp.where(row > ch, 1.0, 0.0)
                    else:
                        beats = jnp.where(sub + first > m, jnp.where(row >= ch, 1.0, 0.0), jnp.where(row > ch, 1.0, 0.0))
                    part = part + beats
                cnt_ref[first:first + SUBLANES, :] += part
    sel_neg = jnp.where(cnt_ref[...] < float(min(N_SELECT, n_blk)), 0.0, NEG)
    pieces = [jnp.zeros((HEAD_DIM, QB), F32), sel_neg]
    if n_blk < HEAD_DIM:
        pieces.append(jnp.zeros((HEAD_DIM - n_blk, QB), F32))
    placed_t = jnp.concatenate(pieces, axis=0).astype(BF16)
    placed = _nt(eye_ref[...], placed_t)
    qa_ref[...] = (q3 + jnp.concatenate([placed] * HPG, axis=0)).astype(BF16)

    m_ref[...] = jnp.full((ROWS, LANES), M_INIT, F32)
    acc_ref[...] = jnp.zeros((ROWS, LANES), F32)
    n_bias = tbs_ref.shape[1]

    span = SEL_SPAN * QB
    n_spans = qb // SEL_SPAN + 1

    def body(kk, carry):
        off = pl.multiple_of(kk * span, span)
        s = _nt(qa_ref[...], ks_ref[pl.ds(off, span), :])
        parts = []
        for u in range(SEL_SPAN):
            idx = jnp.clip(qb - (kk * SEL_SPAN + u), -1, n_bias - 2) + 1
            parts.append(s[:, u * QB:(u + 1) * QB] + tbs_ref[:, pl.ds(idx, 1)].reshape(ROWS, LANES))
        tile_max = functools.reduce(jnp.maximum, parts)
        m_old = m_ref[...]
        m_new = jnp.maximum(m_old, jnp.max(tile_max, axis=1, keepdims=True))
        alpha = jnp.exp(m_old - m_new)
        pr = jnp.concatenate([jnp.exp(x - m_new).astype(BF16) for x in parts], axis=1)
        acc_ref[...] = alpha * acc_ref[...] + jnp.dot(pr, vs_ref[pl.ds(off, SEL_SPAN * QB), :],
                                                      preferred_element_type=F32)
        m_ref[...] = m_new
        return carry

    lax.fori_loop(0, n_spans, body, 0)
    acc = acc_ref[...]
    comb_ref[...] += gates[1] * (acc / pltpu.roll(acc, HEAD_DIM, axis=1))

    qw = (q3 + jnp.where(lane_row == HEAD_DIM, NEG, 0.0)).astype(BF16)
    w_off = pl.multiple_of(t0, QB)
    sw = _nt(qw, kw_ref[pl.ds(w_off, WINDOW + QB), :]) + tbw_ref[...].reshape(ROWS, WINDOW + QB)
    pw = jnp.exp(sw - jnp.max(sw, axis=1, keepdims=True))
    accw = jnp.dot(pw.astype(BF16), vw_ref[pl.ds(w_off, WINDOW + QB), :], preferred_element_type=F32)
    comb_ref[...] += gates[2] * (accw / pltpu.roll(accw, HEAD_DIM, axis=1))

    comb = comb_ref[...]
    lane = lax.broadcasted_iota(jnp.int32, (QB, LANES), 1)
    for half in range(HPG // 2):
        a = comb[(2 * half) * QB:(2 * half + 1) * QB]
        b = comb[(2 * half + 1) * QB:(2 * half + 2) * QB]
        o_ref[0, :, half * LANES:(half + 1) * LANES] = jnp.where(lane < HEAD_DIM, a,
                                                                 pltpu.roll(b, HEAD_DIM, axis=1)).astype(o_ref.dtype)


def _bias_by_distance(rel_bias):
    max_d = (N_SEL_BIAS + 1) * QB
    return jnp.transpose(rel_bias.astype(F32)[rel_bucket(jnp.arange(max_d))])


def _ext(tab_t, lo, hi, ok_lo=None, ok_hi=None):
    n_heads, depth = tab_t.shape
    y = np.arange(lo, hi)
    parts = [jnp.broadcast_to(tab_t[:, :1], (n_heads, int(np.sum(y < 0)))), tab_t[:, max(lo, 0):max(min(hi, depth), 0)],
             jnp.broadcast_to(tab_t[:, -1:], (n_heads, int(np.sum(y > depth - 1))))]
    arr = jnp.concatenate(parts, axis=1)
    ok = np.ones(y.shape, bool)
    if ok_lo is not None:
        ok &= y >= ok_lo
    if ok_hi is not None:
        ok &= y <= ok_hi
    return jnp.where(jnp.asarray(ok)[None], arr, NEG)


def _toeplitz_tiles(ext, lo, ks):
    w = jnp.stack([ext[:, QB * k - (QB - 1) - lo:QB * k - (QB - 1) - lo + 2 * QB] for k in ks], axis=1)
    skew = jnp.tile(w, (1, 1, QB + 1))[..., :QB * (2 * QB + 1)].reshape(w.shape[:2] + (QB, 2 * QB + 1))[..., :QB]
    return skew[..., ::-1]


def rel_bias_tables(rel_bias, seq):
    tab_t = _bias_by_distance(rel_bias)
    max_d = tab_t.shape[1]
    lo = -(2 * QB - 1)
    tbs = _toeplitz_tiles(_ext(tab_t, lo, max_d, 0, None), lo, range(-1, N_SEL_BIAS))
    tbw = _toeplitz_tiles(_ext(tab_t, lo, max_d, 0, WINDOW), lo, range(N_WIN_TILES - 1, -1, -1))
    tbw = jnp.transpose(tbw, (0, 2, 1, 3)).reshape(N_HEADS, QB, N_WIN_TILES * QB)
    nqb = seq // QB
    ncp = seq // CMP_STRIDE
    per_qb = QB // CMP_STRIDE
    width = ncp + per_qb * (nqb - 1)
    c0 = QB * (nqb - 1) - (L_CMP - 1)
    ext = _ext(tab_t, c0 - CMP_STRIDE * (width - 1), c0 + QB)
    rows = []
    for a in range(per_qb):
        for r in range(CMP_STRIDE):
            f = ext[:, r::CMP_STRIDE][:, :width + per_qb - 1][:, ::-1]
            rows.append(f[:, per_qb - 1 - a:per_qb - 1 - a + width])
    v = jnp.stack(rows, axis=1)
    bc = jnp.stack([v[:, :, per_qb * (nqb - 1 - b):per_qb * (nqb - 1 - b) + ncp] for b in range(nqb)])
    return tbs, tbw, bc


def nsa_prompt_pallas(q, ck, cv, kv_sel, kv_win, gn, tables):
    B, S = q.shape[:2]
    assert S % (SEL_SPAN * QB) == 0 and S % KV_CHUNK == 0 and S // SEL_BLOCK <= HEAD_DIM
    tbs, tbw, bc = tables
    nqb = S // QB
    ncp = S // CMP_STRIDE
    n_cmp = ncp - 1
    n_blk = S // SEL_BLOCK
    c0 = np.arange(ncp)[None, :] * CMP_STRIDE
    s0 = np.arange(n_blk)[:, None] * SEL_BLOCK
    ovl_t = ((c0 <= s0 + SEL_BLOCK - 1) & (c0 + L_CMP - 1 >= s0) & (np.arange(ncp)[None, :] < n_cmp))
    ovl_t = jnp.asarray(ovl_t.astype(np.float32), BF16)
    eye = jnp.asarray(np.eye(QB, dtype=np.float32), BF16)
    gw = HPG * HEAD_DIM
    pq = np.zeros((gw, HPG * LANES), np.float32)
    pq[np.arange(gw), (np.arange(gw) // HEAD_DIM) * LANES + np.arange(gw) % HEAD_DIM] = 1.0
    pk = np.zeros((KV_HEADS, KV_ROW, LANES), np.float32)
    pv = np.zeros((KV_HEADS, KV_ROW, LANES), np.float32)
    for g in range(KV_HEADS):
        pk[g, g * HEAD_DIM + np.arange(HEAD_DIM), np.arange(HEAD_DIM)] = 1.0
        pv[g, (KV_HEADS + g) * HEAD_DIM + np.arange(HEAD_DIM), np.arange(HEAD_DIM)] = 1.0
    onehot = np.zeros((S, LANES), np.float32)
    onehot[np.arange(S), HEAD_DIM + np.arange(S) // SEL_BLOCK] = 1.0

    kv_spec = pl.BlockSpec((1, S, KV_ROW), lambda b, g, i: (b, 0, 0))
    cmp_spec = pl.BlockSpec((1, 1, ncp, LANES), lambda b, g, i: (b, g, 0, 0))
    const2 = lambda b, g, i: (0, 0)
    return pl.pallas_call(
        functools.partial(_nsa_prompt_kernel, n_cmp=n_cmp, n_blk=n_blk),
        grid=(B, KV_HEADS, nqb),
        in_specs=[
            pl.BlockSpec((1, QB, gw), lambda b, g, i: (b, i, g)),
            cmp_spec, cmp_spec, kv_spec, kv_spec,
            pl.BlockSpec((1, QB, LANES), lambda b, g, i: (b, i, g)),
            pl.BlockSpec((1, HPG, QB, ncp), lambda b, g, i: (i, g, 0, 0)),
            pl.BlockSpec((HPG, N_SEL_BIAS + 1, QB, QB), lambda b, g, i: (g, 0, 0, 0)),
            pl.BlockSpec((HPG, QB, N_WIN_TILES * QB), lambda b, g, i: (g, 0, 0)),
            pl.BlockSpec((n_blk, ncp), const2),
            pl.BlockSpec((QB, QB), const2),
            pl.BlockSpec((gw, HPG * LANES), const2),
            pl.BlockSpec((1, KV_ROW, LANES), lambda b, g, i: (g, 0, 0)),
            pl.BlockSpec((1, KV_ROW, LANES), lambda b, g, i: (g, 0, 0)),
            pl.BlockSpec((S, LANES), const2),
        ],
        out_specs=pl.BlockSpec((1, QB, gw), lambda b, g, i: (b, i, g)),
        out_shape=jax.ShapeDtypeStruct((B, S, N_HEADS * HEAD_DIM), BF16),
        scratch_shapes=[pltpu.VMEM((ROWS, LANES), F32), pltpu.VMEM((ROWS, LANES), F32),
                        pltpu.VMEM((ROWS, LANES), BF16), pltpu.VMEM((ROWS, LANES), F32),
                        pltpu.VMEM((S, LANES), BF16), pltpu.VMEM((S, LANES), BF16),
                        pltpu.VMEM((S + WINDOW, LANES), BF16), pltpu.VMEM((S + WINDOW, LANES), BF16),
                        pltpu.VMEM((n_blk, QB), F32)],
        compiler_params=pltpu.CompilerParams(dimension_semantics=("parallel", "parallel", "arbitrary"),
                                             vmem_limit_bytes=VMEM_LIMIT_BYTES),
        name="nsa_prompt",
    )(q, ck, cv, kv_sel, kv_win, gn, bc, tbs, tbw, ovl_t, eye, jnp.asarray(pq, BF16), jnp.asarray(pk, BF16),
      jnp.asarray(pv, BF16), jnp.asarray(onehot, BF16))


STEP_ROWS = SUBLANES
STAGE_PAGES = 8
RANK_LANES = 2 * LANES


def _split3(x):
    hi = x.astype(BF16)
    r1 = x - hi.astype(F32)
    mid = r1.astype(BF16)
    return hi, mid, (r1 - mid.astype(F32)).astype(BF16)


def _nsa_step_kernel(*refs, n_pages, n_cmp):
    pages = refs[1:1 + n_pages]
    (q_ref, ck_ref, cv_ref, kvs_ref, kvw_ref, win_ref, gn_ref, bcs_ref, bsel_ref, bwin_ref, ovl_ref, oh_ref,
     pq_ref, pk_ref, pv_ref, o_ref, kc_ref, vs_ref, kw_ref, vw_ref) = refs[1 + n_pages:]
    page = pages[0].shape[2]
    past = n_pages * page
    wb = win_ref.shape[2]
    ncp = ck_ref.shape[2]
    lane_row = lax.broadcasted_iota(jnp.int32, (1, LANES), 1)
    ones_hi = jnp.where(lane_row >= HEAD_DIM, 1.0, 0.0)
    ones_hi2 = jnp.concatenate([ones_hi] * KV_HEADS, axis=1)
    first_row = lax.broadcasted_iota(jnp.int32, (LANES, 1), 0) == 0

    @pl.when(pl.program_id(0) == 0)
    def _():
        for g in range(KV_HEADS):
            kc_ref[g, 0:past, LANES:2 * LANES] = oh_ref[...]
            kc_ref[g, past:past + LANES, LANES:2 * LANES] = jnp.zeros((LANES, LANES), BF16)

    def place(rows_bf16, r0, nrows, k_dst, v_dst):
        k2 = jnp.dot(rows_bf16, pk_ref[...], preferred_element_type=F32)
        v2 = jnp.dot(rows_bf16, pv_ref[...], preferred_element_type=F32) + ones_hi2
        for g in range(KV_HEADS):
            k_dst[g, r0:r0 + nrows, 0:LANES] = k2[:, g * LANES:(g + 1) * LANES].astype(BF16)
            v_dst[g, r0:r0 + nrows, :] = v2[:, g * LANES:(g + 1) * LANES].astype(BF16)

    for c in range(n_pages // STAGE_PAGES):
        blk = jnp.concatenate([pg[0, 0] for pg in pages[c * STAGE_PAGES:(c + 1) * STAGE_PAGES]], axis=0).astype(BF16)
        place(blk, c * STAGE_PAGES * page, STAGE_PAGES * page, kc_ref, vs_ref)
    place(jnp.where(first_row, kvs_ref[0], 0.0).astype(BF16), past, LANES, kc_ref, vs_ref)
    place(win_ref[0, 0].astype(BF16), 0, wb, kw_ref, vw_ref)
    place(jnp.where(first_row, kvw_ref[0], 0.0).astype(BF16), wb, LANES, kw_ref, vw_ref)

    sig = jax.nn.sigmoid(gn_ref[0])
    lane = lax.broadcasted_iota(jnp.int32, (1, LANES), 1)
    rr = lax.broadcasted_iota(jnp.int32, (RANK_LANES, RANK_LANES), 0)
    cc = lax.broadcasted_iota(jnp.int32, (RANK_LANES, RANK_LANES), 1)
    ones_sq = jnp.ones((RANK_LANES, RANK_LANES), BF16)
    pad_rows = jnp.zeros((STEP_ROWS - HPG, LANES), F32)
    for g in range(KV_HEADS):
        gw = HPG * HEAD_DIM
        q4 = jnp.dot(q_ref[0][:, g * gw:(g + 1) * gw], pq_ref[...], preferred_element_type=F32)
        q8 = jnp.concatenate([q4[:, j * LANES:(j + 1) * LANES] for j in range(HPG)] + [pad_rows], axis=0)
        gates = [jnp.concatenate([sig[:, g * LANES + 3 * j + c:g * LANES + 3 * j + c + 1] for j in range(HPG)]
                                 + [pad_rows[:, 0:1]], axis=0) for c in range(3)]

        lc = _nt(q8.astype(BF16), ck_ref[0, g]) + bcs_ref[g]
        col = lax.broadcasted_iota(jnp.int32, (STEP_ROWS, ncp), 1)
        valid = col < n_cmp
        lc = jnp.where(valid, lc, NEG)
        p = jnp.where(valid, jnp.exp(lc - jnp.max(lc, axis=1, keepdims=True)), 0.0)
        pc = p / jnp.maximum(jnp.sum(p, axis=1, keepdims=True), 1e-30)
        comb = gates[0] * jnp.dot(pc.astype(BF16), cv_ref[0, g], preferred_element_type=F32)

        pcs = jnp.broadcast_to(jnp.sum(pc[0:HPG], axis=0, keepdims=True), (STEP_ROWS, ncp))
        hi = pcs.astype(BF16)
        lo = (pcs - hi.astype(F32)).astype(BF16)
        imp = (jnp.dot(hi, ovl_ref[...], preferred_element_type=F32)
               + jnp.dot(lo, ovl_ref[...], preferred_element_type=F32))[0:1]
        blk_id = lax.broadcasted_iota(jnp.int32, (1, RANK_LANES), 1)
        cur = past // SEL_BLOCK
        forced = (blk_id == 0) | (blk_id == cur) | (blk_id == cur - 1)
        score = jnp.where(blk_id > cur, -1.0, jnp.where(forced, FORCE, imp))
        s_n = jnp.broadcast_to(score, (RANK_LANES, RANK_LANES))
        diag = jnp.where(rr == cc, s_n, 0.0)
        s_m = sum(_nt(part, ones_sq) for part in _split3(diag))
        beats = jnp.where(rr < cc, jnp.where(s_m >= s_n, 1.0, 0.0), jnp.where(s_m > s_n, 1.0, 0.0))
        rank = jnp.sum(beats, axis=0, keepdims=True)
        sel_neg = jnp.where(rank < float(N_SELECT), 0.0, NEG)

        qa = jnp.concatenate([q8, jnp.broadcast_to(sel_neg[:, :LANES], (STEP_ROWS, LANES))], axis=1).astype(BF16)
        s_past = _nt(qa, kc_ref[g, 0:past, :]) + bsel_ref[g, :, 0:past]
        s_new = (_nt(qa, kc_ref[g, past:past + LANES, :]) + bsel_ref[g, :, past:past + LANES]
                 + sel_neg[:, LANES:LANES + 1])
        mx = jnp.maximum(jnp.max(s_past, axis=1, keepdims=True), jnp.max(s_new, axis=1, keepdims=True))
        acc = (jnp.dot(jnp.exp(s_past - mx).astype(BF16), vs_ref[g, 0:past, :], preferred_element_type=F32)
               + jnp.dot(jnp.exp(s_new - mx).astype(BF16), vs_ref[g, past:past + LANES, :], preferred_element_type=F32))
        comb = comb + gates[1] * (acc / pltpu.roll(acc, HEAD_DIM, axis=1))

        sw = _nt(q8.astype(BF16), kw_ref[g]) + bwin_ref[g]
        pw = jnp.exp(sw - jnp.max(sw, axis=1, keepdims=True))
        accw = jnp.dot(pw.astype(BF16), vw_ref[g], preferred_element_type=F32)
        comb = comb + gates[2] * (accw / pltpu.roll(accw, HEAD_DIM, axis=1))

        for half in range(HPG // 2):
            a = comb[2 * half:2 * half + 1]
            b = comb[2 * half + 1:2 * half + 2]
            o_ref[0, :, g * gw + half * LANES:g * gw + (half + 1) * LANES] = jnp.where(
                lane < HEAD_DIM, a, pltpu.roll(b, HEAD_DIM, axis=1)).astype(o_ref.dtype)


def step_bias_tables(rel_bias, past, wb, ncp):
    tab_t = _bias_by_distance(rel_bias)

    def rows(t):
        return jnp.pad(t.reshape(KV_HEADS, HPG, -1), ((0, 0), (0, STEP_ROWS - HPG), (0, 0)))

    d_first = past - (L_CMP - 1)
    cmp_t = _ext(tab_t, d_first - CMP_STRIDE * (ncp - 1), d_first + 1, 0, None)[:, ::CMP_STRIDE][:, ::-1]
    sel_t = _ext(tab_t, -(LANES - 1), past + 1, 0, None)[:, ::-1]
    win_t = _ext(tab_t, -(LANES - 1), wb + 1, 0, WINDOW)[:, ::-1]
    return rows(cmp_t), rows(sel_t), rows(win_t)


def nsa_step(q, ck, cv, kvs_new, kvw_new, sel_pool, win_buf, gn, page_table, tables, layer):
    B, n_pages = page_table.shape
    page = sel_pool.shape[2]
    past = n_pages * page
    wb = win_buf.shape[2]
    ncp = ck.shape[2]
    n_blk = past // SEL_BLOCK + 1
    assert n_pages % STAGE_PAGES == 0 and n_blk <= LANES + 1 and page == LANES and wb % SUBLANES == 0
    bcs, bsel, bwin = tables
    c0 = np.arange(ncp)[:, None] * CMP_STRIDE
    s0 = np.arange(RANK_LANES)[None, :] * SEL_BLOCK
    ovl = ((c0 <= s0 + SEL_BLOCK - 1) & (c0 + L_CMP - 1 >= s0) & (np.arange(ncp)[:, None] < ncp - 1)
           & (np.arange(RANK_LANES)[None, :] < n_blk))
    onehot = np.zeros((past, LANES), np.float32)
    onehot[np.arange(past), np.arange(past) // SEL_BLOCK] = 1.0
    gw = HPG * HEAD_DIM
    pq = np.zeros((gw, HPG * LANES), np.float32)
    pq[np.arange(gw), (np.arange(gw) // HEAD_DIM) * LANES + np.arange(gw) % HEAD_DIM] = 1.0
    pk = np.zeros((KV_ROW, KV_HEADS * LANES), np.float32)
    pv = np.zeros((KV_ROW, KV_HEADS * LANES), np.float32)
    for g in range(KV_HEADS):
        pk[g * HEAD_DIM + np.arange(HEAD_DIM), g * LANES + np.arange(HEAD_DIM)] = 1.0
        pv[(KV_HEADS + g) * HEAD_DIM + np.arange(HEAD_DIM), g * LANES + np.arange(HEAD_DIM)] = 1.0
    consts = [jnp.asarray(ovl.astype(np.float32), BF16), jnp.asarray(onehot, BF16), jnp.asarray(pq, BF16),
              jnp.asarray(pk, BF16), jnp.asarray(pv, BF16)]

    row3 = lambda a: a.reshape(B, 1, a.shape[-1])
    per_b = lambda shape: pl.BlockSpec((1,) + shape, lambda b, pt: (b,) + (0,) * len(shape))
    full = lambda a: pl.BlockSpec(a.shape, lambda b, pt: (0,) * a.ndim)
    page_specs = [pl.BlockSpec((1, 1, page, KV_ROW), lambda b, pt, k=k: (layer, pt[b, k], 0, 0)) for k in range(n_pages)]
    win_spec = pl.BlockSpec((1, 1, wb, KV_ROW), lambda b, pt: (layer, b, 0, 0))
    in_specs = page_specs + [per_b((1, Q_COLS)), per_b((KV_HEADS, ncp, LANES)), per_b((KV_HEADS, ncp, LANES)),
                             per_b((1, KV_ROW)), per_b((1, KV_ROW)), win_spec, per_b((1, GN_COLS)),
                             full(bcs), full(bsel), full(bwin)] + [full(c) for c in consts]
    grid_spec = pltpu.PrefetchScalarGridSpec(
        num_scalar_prefetch=1, grid=(B,), in_specs=in_specs, out_specs=per_b((1, Q_COLS)),
        scratch_shapes=[pltpu.VMEM((KV_HEADS, past + LANES, 2 * LANES), BF16), pltpu.VMEM((KV_HEADS, past + LANES, LANES), BF16),
                        pltpu.VMEM((KV_HEADS, wb + LANES, LANES), BF16), pltpu.VMEM((KV_HEADS, wb + LANES, LANES), BF16)])
    out = pl.pallas_call(
        functools.partial(_nsa_step_kernel, n_pages=n_pages, n_cmp=ncp - 1),
        grid_spec=grid_spec,
        out_shape=jax.ShapeDtypeStruct((B, 1, Q_COLS), BF16),
        compiler_params=pltpu.CompilerParams(dimension_semantics=("arbitrary",), vmem_limit_bytes=VMEM_LIMIT_BYTES),
        name="nsa_step",
    )(page_table, *([sel_pool] * n_pages), row3(q), ck, cv, row3(kvs_new), row3(kvw_new), win_buf, row3(gn),
      bcs, bsel, bwin, *consts)
    return out.reshape(B, Q_COLS)


def _merge_kernel(ya_ref, yb_ref, yc_ref, ga_ref, gb_ref, gc_ref, x_ref, wa_ref, wb_ref, wc_ref, wo_ref, o_ref):
    def branch(y_ref, g_ref, w_ref):
        return jax.nn.sigmoid(g_ref[...]) * jnp.dot(y_ref[...], w_ref[...], preferred_element_type=F32)

    merged = branch(ya_ref, ga_ref, wa_ref) + branch(yb_ref, gb_ref, wb_ref) + branch(yc_ref, gc_ref, wc_ref)
    o_ref[...] = x_ref[...] + jnp.dot(merged.astype(BF16), wo_ref[...], preferred_element_type=F32)


def merge(ya, yb, yc, za, x, wa, wb, wc, wo):
    T = x.shape[0]
    tm = _pick(T, (512, 256, 128))
    y_spec = pl.BlockSpec((tm, ya.shape[1]), lambda i: (i, 0))
    w_spec = pl.BlockSpec((ya.shape[1], D_MODEL), lambda i: (0, 0))
    return pl.pallas_call(
        _merge_kernel,
        grid=(T // tm,),
        in_specs=[y_spec, y_spec, y_spec,
                  pl.BlockSpec((tm, D_MODEL), lambda i: (i, 0)), pl.BlockSpec((tm, D_MODEL), lambda i: (i, 1)),
                  pl.BlockSpec((tm, D_MODEL), lambda i: (i, 2)),
                  pl.BlockSpec((tm, D_MODEL), lambda i: (i, 0)),
                  w_spec, w_spec, w_spec, pl.BlockSpec((D_MODEL, D_MODEL), lambda i: (0, 0))],
        out_specs=pl.BlockSpec((tm, D_MODEL), lambda i: (i, 0)),
        out_shape=jax.ShapeDtypeStruct((T, D_MODEL), F32),
        compiler_params=pltpu.CompilerParams(dimension_semantics=("parallel",), vmem_limit_bytes=VMEM_LIMIT_BYTES),
        name="merge",
    )(ya, yb, yc, za, za, za, x, wa, wb, wc, wo)


def _ffn_kernel(*refs, routed):
    if routed:
        h_ref, g_ref, wr_ref, br_ref, wg_ref, wu_ref, wd_ref, o_ref, xn_ref, gate_ref = refs
    else:
        h_ref, g_ref, wg_ref, wu_ref, wd_ref, o_ref, xn_ref = refs
    e = pl.program_id(1)
    f = pl.program_id(2)

    @pl.when((e == 0) & (f == 0))
    def _():
        h = h_ref[...]
        xb = _rms_rows(h, g_ref[...]).astype(BF16)
        xn_ref[...] = xb
        o_ref[...] = h
        if routed:
            logits = jnp.dot(xb, wr_ref[...], preferred_element_type=F32) + br_ref[...]
            lane = lax.broadcasted_iota(jnp.int32, logits.shape, 1)
            m1 = jnp.max(logits, axis=1, keepdims=True)
            i1 = jnp.min(jnp.where(logits == m1, lane, LANES), axis=1, keepdims=True)
            rest = jnp.where(lane == i1, M_INIT, logits)
            m2 = jnp.max(rest, axis=1, keepdims=True)
            i2 = jnp.min(jnp.where(rest == m2, lane, LANES), axis=1, keepdims=True)
            r = jnp.exp(m2 - m1)
            gate_ref[...] = jnp.where(lane == i1, 1.0 / (1.0 + r), 0.0) + jnp.where(lane == i2, r / (1.0 + r), 0.0)

    xb = xn_ref[...]
    a = jnp.dot(xb, wg_ref[0], preferred_element_type=F32)
    u = jnp.dot(xb, wu_ref[0], preferred_element_type=F32)
    t = (a * jax.nn.sigmoid(a)) * u
    y = jnp.dot(t.astype(BF16), wd_ref[0], preferred_element_type=F32)
    if routed:
        lane = lax.broadcasted_iota(jnp.int32, gate_ref.shape, 1)
        y = jnp.sum(jnp.where(lane == e, gate_ref[...], 0.0), axis=1, keepdims=True) * y
    o_ref[...] += y


def channel_mixer(h, norm_gain, wg, wu, wd, router=None):
    T = h.shape[0]
    E, _, F = wg.shape
    tm = _pick(T, (512, 256, 128))
    tf = _pick(F, (1408, 1024, 512, 256, 128))
    routed = router is not None
    row = lambda i, e, f: (i, 0)
    in_specs = [pl.BlockSpec((tm, D_MODEL), row), pl.BlockSpec((1, D_MODEL), lambda i, e, f: (0, 0))]
    args = [h, norm_gain.astype(F32)[None]]
    scratch = [pltpu.VMEM((tm, D_MODEL), BF16)]
    if routed:
        in_specs += [pl.BlockSpec((D_MODEL, LANES), lambda i, e, f: (0, 0)), pl.BlockSpec((1, LANES), lambda i, e, f: (0, 0))]
        args += list(router)
        scratch.append(pltpu.VMEM((tm, LANES), F32))
    in_specs += [pl.BlockSpec((1, D_MODEL, tf), lambda i, e, f: (e, 0, f)),
                 pl.BlockSpec((1, D_MODEL, tf), lambda i, e, f: (e, 0, f)),
                 pl.BlockSpec((1, tf, D_MODEL), lambda i, e, f: (e, f, 0))]
    args += [wg, wu, wd]
    return pl.pallas_call(
        functools.partial(_ffn_kernel, routed=routed),
        grid=(T // tm, E, F // tf),
        in_specs=in_specs,
        out_specs=pl.BlockSpec((tm, D_MODEL), row),
        out_shape=jax.ShapeDtypeStruct((T, D_MODEL), F32),
        scratch_shapes=scratch,
        compiler_params=pltpu.CompilerParams(dimension_semantics=("parallel", "arbitrary", "arbitrary"),
                                             vmem_limit_bytes=VMEM_LIMIT_BYTES),
        name="moe" if routed else "ffn",
    )(*args)


def mixer_tail(x2, za, y_a, y_b, y_c, l, P):
    return merge(y_a, y_b, y_c, za, x2, P['w_br_rg'][l].astype(BF16), P['w_br_attn'][l].astype(BF16),
                 P['w_br_pool'][l].astype(BF16), P['w_out'][l].astype(BF16))


def prompt_mixer(x, l, P, packed, tables):
    B, T, _ = x.shape
    x2 = x.reshape(B * T, D_MODEL)
    za, q, kv_c, kv_s, kv_w, gn = projection(x2, P['attn_norm'][l], packed['proj'])
    zeros = lambda rows: jnp.zeros((B, rows, D_RNN), F32)
    y_a, y_c, conv_new, h_last, pool_new = mixer_seq(za, B, T, packed['mix'], zeros(CONV_W - 1), jnp.zeros((B, D_RNN), F32),
                                                     zeros(POOL_MAX - 1), 0)
    ck, cv = compress(kv_c.reshape(B, 2 * T, LANES), packed['cmp'])
    y_b = nsa_prompt_pallas(q.reshape(B, T, Q_COLS), ck, cv, kv_s.reshape(B, T, KV_ROW), kv_w.reshape(B, T, KV_ROW),
                            gn.reshape(B, T, GN_COLS), tables)
    out = mixer_tail(x2, za, y_a, y_b.reshape(B * T, Q_COLS), y_c, l, P)
    kv_shape = (B, T, 2, KV_HEADS, HEAD_DIM)
    state = (kv_c.reshape(kv_shape), kv_s.reshape(kv_shape), kv_w.reshape(kv_shape)[:, -min(WINDOW, T):], conv_new,
             h_last[:, 0], pool_new)
    return out.reshape(B, T, D_MODEL), state


def sample_mixer(x, l, P, packed, past_len, conv_state, h0, pool_state, caches, win_buf, page_table, step_tables):
    B, T, _ = x.shape
    x2 = x.reshape(B * T, D_MODEL)
    za, q, kv_c, kv_s, kv_w, gn = projection(x2, P['attn_norm'][l], packed['proj'])
    y_a, y_c, h_new = mixer_step(za, packed['mix'], conv_state, h0, pool_state, past_len)
    cmp_pools, sel_pools, win_bufs = caches
    ck, cv = compress(cmp_pools, packed['cmp'], page_table, layer=l)
    kv_shape = (B, T, 2, KV_HEADS, HEAD_DIM)
    wb = win_buf.shape[1]
    y_b = nsa_step(q, ck, cv, kv_s, kv_w, sel_pools, win_bufs, gn, page_table, step_tables, l)
    win_new = jnp.concatenate([win_buf[:, 1:], kv_w.reshape(kv_shape)], axis=1)
    out = mixer_tail(x2, za, y_a, y_b, y_c, l, P)
    conv_new = jnp.concatenate([conv_state[:, 1:], za[:, None, ZA_XRG:ZA_XRG + D_RNN]], axis=1)
    pool_new = jnp.concatenate([pool_state[:, 1:], za[:, None, ZA_XPOOL:ZA_XPOOL + D_POOL]], axis=1)
    state = (kv_c.reshape(kv_shape), kv_s.reshape(kv_shape), win_new, conv_new, h_new, pool_new)
    return out.reshape(B, T, D_MODEL), state


def ffn_layer(x, l, P, W):
    B, T, _ = x.shape
    i = l // 2
    if l % 2 == 0:
        y = channel_mixer(x.reshape(B * T, D_MODEL), P['ffn_norm'][l], W['ffn_g'][i], W['ffn_u'][i], W['ffn_d'][i])
    else:
        y = channel_mixer(x.reshape(B * T, D_MODEL), P['ffn_norm'][l], W['moe_g'][i], W['moe_u'][i], W['moe_d'][i],
                          router=W['router'][i])
    return y.reshape(B, T, D_MODEL)


def kernel(x_prompt, x_sample, cache_cmp_kv, cache_sel_kv, cache_win_kv, state_conv, state_rg_h, state_pool,
           page_table, attn_norm, w_in, conv_w, conv_b, rg_w_a, rg_b_a, rg_w_x, rg_b_x, rg_lambda, q_norm, k_norm,
           cmp_pe, w_cmp1, w_cmp2, rel_bias, w_pool, pool_scale, w_br_rg, w_br_attn, w_br_pool, w_out, ffn_norm,
           ffn_w_gate, ffn_w_up, ffn_w_down, w_router, b_router, moe_w_gate, moe_w_up, moe_w_down):
    P = dict(attn_norm=attn_norm, conv_w=conv_w, conv_b=conv_b, rg_w_a=rg_w_a, rg_b_a=rg_b_a,
             rg_w_x=rg_w_x, rg_b_x=rg_b_x, rg_lambda=rg_lambda, w_pool=w_pool,
             pool_scale=pool_scale, w_br_rg=w_br_rg, w_br_attn=w_br_attn, w_br_pool=w_br_pool, w_out=w_out,
             ffn_norm=ffn_norm)
    depth = w_in.shape[0]
    n_moe = w_router.shape[0]
    pad_e = LANES - N_EXPERTS
    W = dict(
        ffn_g=[w[None].astype(BF16) for w in ffn_w_gate], ffn_u=[w[None].astype(BF16) for w in ffn_w_up],
        ffn_d=[w[None].astype(BF16) for w in ffn_w_down],
        moe_g=[w.astype(BF16) for w in moe_w_gate], moe_u=[w.astype(BF16) for w in moe_w_up],
        moe_d=[w.astype(BF16) for w in moe_w_down],
        router=[(jnp.pad(w_router[i], ((0, 0), (0, pad_e))).astype(BF16),
                 jnp.pad(b_router[i].astype(F32), (0, pad_e), constant_values=NEG)[None]) for i in range(n_moe)])
    past_len = page_table.shape[1] * PAGE_SIZE
    y_p, y_s = x_prompt, x_sample
    tables = rel_bias_tables(rel_bias, x_prompt.shape[1])
    step_tables = step_bias_tables(rel_bias, past_len, cache_win_kv.shape[2], past_len // CMP_STRIDE)
    n_pool = cache_cmp_kv.shape[1]
    caches = (cache_cmp_kv.reshape(depth, n_pool, 2 * PAGE_SIZE, LANES),
              cache_sel_kv.reshape(depth, n_pool, PAGE_SIZE, KV_ROW),
              cache_win_kv.reshape(depth, cache_win_kv.shape[1], cache_win_kv.shape[2], KV_ROW))
    p_list, s_list = [], []
    for l in range(depth):
        packed = dict(proj=pack_projection(w_in[l], q_norm[l], k_norm[l, 1], k_norm[l, 2]),
                      cmp=pack_compress(w_cmp1[l], w_cmp2[l], cmp_pe[l], k_norm[l, 0]),
                      mix=pack_mixer(conv_w[l], conv_b[l], rg_w_a[l], rg_b_a[l], rg_w_x[l], rg_b_x[l], rg_lambda[l],
                                     w_pool[l], pool_scale[l]))
        y_p, st_p = prompt_mixer(y_p, l, P, packed, tables)
        y_p = ffn_layer(y_p, l, P, W)
        p_list.append(st_p)
        y_s, st_s = sample_mixer(y_s, l, P, packed, past_len, state_conv[l], state_rg_h[l], state_pool[l],
                                 caches, cache_win_kv[l], page_table, step_tables)
        y_s = ffn_layer(y_s, l, P, W)
        s_list.append(st_s)
    p_cmp_kv, p_sel_kv, p_win_kv, p_conv, p_h, p_pool = [jnp.stack(a) for a in zip(*p_list)]
    s_cmp_kv, s_sel_kv, s_win_kv, s_conv, s_h, s_pool = [jnp.stack(a) for a in zip(*s_list)]
    return (y_p, y_s, p_cmp_kv, p_sel_kv, p_win_kv, p_conv, p_h, p_pool,
            s_cmp_kv, s_sel_kv, s_win_kv, s_conv, s_h, s_pool)
```

```python
import math
import functools

import jax
import jax.numpy as jnp
import numpy as np
from jax import lax
from jax.experimental import pallas as pl
from jax.experimental.pallas import tpu as pltpu

D_MODEL = 1024
PAGE_SIZE = 128
F32 = jnp.float32
BF16 = jnp.bfloat16
EPS = 1e-6
NEG = -1e30
FORCE = 1e4
D_RNN = 512
RG_BLOCKS = 8
RG_BW = D_RNN // RG_BLOCKS
CONV_W = 4
RG_C = 8.0
N_HEADS = 8
KV_HEADS = 2
HPG = N_HEADS // KV_HEADS
HEAD_DIM = 64
L_CMP = 32
CMP_STRIDE = 16
CMP_HIDDEN = 256
SEL_BLOCK = 64
N_SELECT = 16
WINDOW = 512
Q_BLOCK = 128
D_POOL = 512
POOL_WINDOWS = (2, 4, 8, 16)
POOL_GROUPS = 4
POOL_GW = D_POOL // POOL_GROUPS
POOL_MAX = 16
REL_BUCKETS = 32
REL_MAX_DIST = 1024
N_EXPERTS = 8
TOP_K = 2
KV_ROW = 2 * KV_HEADS * HEAD_DIM
SPLITS = (D_RNN, D_RNN, N_HEADS * HEAD_DIM, KV_ROW, KV_ROW, KV_ROW, 3 * N_HEADS, D_POOL, 3 * D_MODEL)

VMEM_LIMIT_BYTES = 52 * 1024 * 1024
LANES = 128
SUBLANES = 8
M_INIT = -3e38


def _pick(n, cands):
    for c in cands:
        if n % c == 0:
            return c
    return n


def _nt(a, b):
    return lax.dot_general(a, b, (((1,), (1,)), ((), ())), preferred_element_type=F32)


def _rms_rows(x, g):
    return x * lax.rsqrt(jnp.mean(x * x, axis=-1, keepdims=True) + EPS) * g


def rms_norm(x, g):
    xf = x.astype(F32)
    y = xf * lax.rsqrt(jnp.mean(xf * xf, axis=-1, keepdims=True) + EPS)
    return (y * g.astype(F32)).astype(x.dtype)


def rel_bucket(dist):
    n_exact = REL_BUCKETS // 2
    d = jnp.maximum(dist, 0)
    df = jnp.maximum(d, 1).astype(F32)
    large = n_exact + (jnp.log(df / n_exact) / math.log(REL_MAX_DIST / n_exact)
                       * (REL_BUCKETS - n_exact)).astype(jnp.int32)
    return jnp.where(d < n_exact, d, jnp.minimum(large, REL_BUCKETS - 1))


def masked_probs(logits, valid):
    logits = jnp.where(valid, logits, NEG)
    m = jnp.max(logits, axis=-1, keepdims=True)
    p = jnp.where(valid, jnp.exp(logits - m), 0.0)
    return p / jnp.maximum(jnp.sum(p, axis=-1, keepdims=True), 1e-30)


def causal_conv(x, buf, w, b):
    xp = jnp.concatenate([buf.astype(x.dtype), x], axis=1)
    y = lax.conv_general_dilated(xp, w[:, None, :].astype(x.dtype), (1,), 'VALID',
                                 dimension_numbers=('NWC', 'WIO', 'NWC'),
                                 feature_group_count=x.shape[-1]) + b
    return y, xp[:, -(CONV_W - 1):]


def rg_lru(xc, h0, w_a, b_a, w_x, b_x, lam):
    B, T, _ = xc.shape
    xb = xc.reshape(B, T, RG_BLOCKS, RG_BW)
    r = jax.nn.sigmoid((jnp.einsum('btnc,ncd->btnd', xb, w_a).reshape(B, T, D_RNN) + b_a).astype(F32))
    i = jax.nn.sigmoid((jnp.einsum('btnc,ncd->btnd', xb, w_x).reshape(B, T, D_RNN) + b_x).astype(F32))
    log_a = -RG_C * r * jax.nn.softplus(-lam.astype(F32))
    a = jnp.exp(log_a)
    u = jnp.sqrt(-jnp.expm1(2.0 * log_a)) * (i * xc.astype(F32))
    u = u.at[:, 0].add(a[:, 0] * h0.astype(F32))

    def combine(lhs, rhs):
        a1, b1 = lhs
        a2, b2 = rhs
        return a1 * a2, a2 * b1 + b2

    _, h = lax.associative_scan(combine, (a, u), axis=1)
    return h, h[:, -1]


def pool_mix(xin, buf, start_pos, w_pool, scale):
    B, T, C = xin.shape
    xf = jnp.concatenate([buf.astype(xin.dtype), xin], axis=1).astype(F32)
    cs = jnp.concatenate([jnp.zeros((B, 1, C), F32), jnp.cumsum(xf, axis=1)], axis=1)
    pos = start_pos + jnp.arange(T)
    means = []
    for g, w in enumerate(POOL_WINDOWS):
        sl = slice(g * POOL_GW, (g + 1) * POOL_GW)
        s = cs[:, POOL_MAX:POOL_MAX + T, sl] - cs[:, POOL_MAX - w:POOL_MAX - w + T, sl]
        cnt = jnp.minimum(pos + 1, w).astype(F32)[None, :, None]
        means.append(s / cnt)
    mixed = (jnp.concatenate(means, axis=-1) - xf[:, POOL_MAX - 1:]).astype(xin.dtype)
    y = jnp.einsum('btgc,gcd->btgd', mixed.reshape(B, T, POOL_GROUPS, POOL_GW), w_pool)
    return y.reshape(B, T, D_POOL) * scale, xf[:, -(POOL_MAX - 1):].astype(xin.dtype)


def compress_kv(rows, w1, w2, pe, kn):
    Bx, T = rows.shape[:2]
    n_ch = T // CMP_STRIDE
    ch = rows[:, :n_ch * CMP_STRIDE].reshape(Bx, n_ch, CMP_STRIDE, 2, KV_HEADS, HEAD_DIM)
    first = jnp.einsum('bnsegd,esdf->bnegf', ch, w1[:, :CMP_STRIDE])
    second = jnp.einsum('bnsegd,esdf->bnegf', ch, w1[:, CMP_STRIDE:])
    pe_term = jnp.einsum('led,eldf->ef', pe, w1)
    h = first[:, :-1] + second[:, 1:] + pe_term[:, None, :]
    comp = jnp.einsum('bnegf,efd->bnegd', jax.nn.gelu(h), w2)
    nc = comp.shape[1]
    comp_end = jnp.arange(nc) * CMP_STRIDE + L_CMP - 1
    return rms_norm(comp[:, :, 0], kn), comp[:, :, 1], comp_end


def overlap_matrix(n_cmp, n_sel):
    c0 = jnp.arange(n_cmp)[:, None] * CMP_STRIDE
    s0 = jnp.arange(n_sel)[None, :] * SEL_BLOCK
    return ((c0 <= s0 + SEL_BLOCK - 1) & (c0 + L_CMP - 1 >= s0)).astype(F32)


def nsa_attend(q, q_pos, comp_k, comp_v, comp_end, overlap, fetch_sel, win_k, win_v, win_pos, gates, rel_bias):
    Bq, Tq = q.shape[:2]
    scale = HEAD_DIM ** -0.5
    qg = q.reshape(Bq, Tq, KV_HEADS, HPG, HEAD_DIM)
    rb = rel_bias.astype(F32)

    def head_bias(buckets):
        return jnp.moveaxis(rb[buckets], -1, 1).reshape(Tq, KV_HEADS, HPG, -1)

    dist_c = q_pos[:, None] - comp_end[None, :]
    lc = jnp.einsum('btgjd,bcgd->btgjc', qg, comp_k, preferred_element_type=F32) * scale + head_bias(rel_bucket(dist_c))
    pc = masked_probs(lc, (dist_c >= 0)[None, :, None, None, :])
    o_cmp = jnp.einsum('btgjc,bcgd->btgjd', pc, comp_v)
    ns = overlap.shape[1]
    imp = jnp.einsum('btgjc,cn->btgn', pc, overlap)
    blk = jnp.arange(ns)
    cur = (q_pos // SEL_BLOCK)[None, :, None, None]
    forced = (blk == 0) | (blk == cur) | (blk == cur - 1)
    score = jnp.where(blk > cur, -1.0, jnp.where(forced, FORCE, imp))
    n_sel = min(N_SELECT, ns)
    _, idx = lax.top_k(score, n_sel)
    kv_s = fetch_sel(idx)
    k_s = kv_s[..., 0, :].reshape(Bq, Tq, KV_HEADS, n_sel * SEL_BLOCK, HEAD_DIM)
    v_s = kv_s[..., 1, :].reshape(Bq, Tq, KV_HEADS, n_sel * SEL_BLOCK, HEAD_DIM)
    pos_s = (idx[..., None] * SEL_BLOCK + jnp.arange(SEL_BLOCK)).reshape(Bq, Tq, KV_HEADS, -1)
    dist_s = q_pos[None, :, None, None] - pos_s
    bias_s = rb.reshape(REL_BUCKETS, KV_HEADS, HPG)[rel_bucket(dist_s), jnp.arange(KV_HEADS)[:, None]]
    ls = jnp.einsum('btgjd,btgkd->btgjk', qg, k_s, preferred_element_type=F32) * scale + jnp.moveaxis(bias_s, -1, 3)
    ps = masked_probs(ls, (dist_s >= 0)[:, :, :, None, :])
    o_sel = jnp.einsum('btgjk,btgkd->btgjd', ps, v_s)
    dist_w = q_pos[:, None] - win_pos[None, :]
    lw = jnp.einsum('btgjd,bwgd->btgjw', qg, win_k, preferred_element_type=F32) * scale + head_bias(rel_bucket(dist_w))
    valid_w = (dist_w >= 0) & (dist_w <= WINDOW) & (win_pos[None, :] >= 0)
    pw = masked_probs(lw, valid_w[None, :, None, None, :])
    o_win = jnp.einsum('btgjw,bwgd->btgjd', pw, win_v)
    g = jax.nn.sigmoid(gates.astype(F32)).reshape(Bq, Tq, KV_HEADS, HPG, 3)
    o = g[..., 0:1] * o_cmp + g[..., 1:2] * o_sel + g[..., 2:3] * o_win
    return o.reshape(Bq, Tq, N_HEADS * HEAD_DIM).astype(q.dtype)


def nsa_sample(q, comp_k, comp_v, kv_sel_new, kv_win_new, gates, *, sel_pool, win_buf, page_table, rel_bias):
    DB, DS = q.shape[:2]
    n_pages = page_table.shape[1]
    past = n_pages * PAGE_SIZE
    q_pos = past + jnp.arange(DS)
    comp_end = jnp.arange(comp_k.shape[1]) * CMP_STRIDE + L_CMP - 1
    total = past + DS
    ns = -(-total // SEL_BLOCK)
    nbp = past // SEL_BLOCK
    nnb = ns - nbp
    bpp = PAGE_SIZE // SEL_BLOCK
    ovl = overlap_matrix(comp_k.shape[1], ns)
    pool_blocks = sel_pool.reshape(-1, SEL_BLOCK, 2, KV_HEADS, HEAD_DIM)
    new_pad = jnp.pad(kv_sel_new.astype(sel_pool.dtype), ((0, 0), (0, nnb * SEL_BLOCK - DS), (0, 0), (0, 0), (0, 0)))
    new_blocks = new_pad.reshape(DB, nnb, SEL_BLOCK, 2, KV_HEADS, HEAD_DIM)
    b_idx = jnp.arange(DB)[:, None, None, None]
    g_idx = jnp.arange(KV_HEADS)[None, None, :, None]

    def fetch(idx):
        ip = jnp.minimum(idx, nbp - 1)
        phys = page_table[b_idx, ip // bpp] * bpp + ip % bpp
        from_past = pool_blocks[phys, :, :, g_idx]
        from_new = new_blocks[b_idx, jnp.clip(idx - nbp, 0, nnb - 1), :, :, g_idx]
        return jnp.where((idx >= nbp)[..., None, None, None], from_new, from_past)

    wb = win_buf.shape[1]
    win = jnp.concatenate([win_buf, kv_win_new.astype(win_buf.dtype)], axis=1)
    win_pos = past - wb + jnp.arange(wb + DS)
    out = nsa_attend(q, q_pos, comp_k, comp_v, comp_end, ovl, fetch, win[:, :, 0], win[:, :, 1], win_pos, gates, rel_bias)
    return out, win[:, -wb:]


PROJ_TN = 512
ZA_COLS = 3 * D_MODEL + 2 * D_RNN + D_POOL
N_ZA = ZA_COLS // PROJ_TN
Q_COLS = N_HEADS * HEAD_DIM
GN_COLS = 2 * LANES
PROJ_COLS = ZA_COLS + Q_COLS + 3 * KV_ROW + GN_COLS
ZA_XRG, ZA_GRG, ZA_XPOOL = 3 * D_MODEL, 3 * D_MODEL + D_RNN, 3 * D_MODEL + 2 * D_RNN


def _proj_kernel(x_ref, g_ref, w_ref, seg_ref, ng_ref, nm_ref, za_ref, q_ref, kvc_ref, kvs_ref, kvw_ref, gn_ref, xn_ref):
    j = pl.program_id(1)

    @pl.when(j == 0)
    def _():
        xn_ref[...] = _rms_rows(x_ref[...], g_ref[...]).astype(BF16)

    acc = jnp.dot(xn_ref[...], w_ref[...], preferred_element_type=F32)

    @pl.when(j < N_ZA)
    def _():
        za_ref[...] = acc

    @pl.when(j >= N_ZA)
    def _():
        sq = acc * acc
        hi = sq.astype(BF16)
        lo = (sq - hi.astype(F32)).astype(BF16)
        ss = (jnp.dot(hi, seg_ref[...], preferred_element_type=F32)
              + jnp.dot(lo, seg_ref[...], preferred_element_type=F32))
        normed = acc * lax.rsqrt(ss * (1.0 / HEAD_DIM) + EPS) * ng_ref[...]
        y = jnp.where(nm_ref[...] > 0.5, normed, acc)

        @pl.when(j == N_ZA)
        def _():
            q_ref[...] = y.astype(BF16)

        @pl.when(j == N_ZA + 1)
        def _():
            kvc_ref[...] = y[:, :KV_ROW]
            kvs_ref[...] = y[:, KV_ROW:]

        @pl.when(j == N_ZA + 2)
        def _():
            kvw_ref[...] = y[:, :KV_ROW]
            gn_ref[...] = y[:, KV_ROW:]


def pack_projection(w_in, q_gain, ks_gain, kw_gain):
    cut = np.cumsum(SPLITS)[:-1].tolist()
    x_rg, g_rg, q, kv_c, kv_s, kv_w, g_nsa, x_pool, g_br = jnp.split(w_in, cut, axis=1)
    per_group = 3 * HPG
    gn = jnp.zeros((w_in.shape[0], GN_COLS), w_in.dtype)
    for g in range(KV_HEADS):
        gn = gn.at[:, g * LANES:g * LANES + per_group].set(g_nsa[:, g * per_group:(g + 1) * per_group])
    w = jnp.concatenate([g_br, x_rg, g_rg, x_pool, q, kv_c, kv_s, kv_w, gn], axis=1).astype(BF16)
    ones_v = jnp.ones((KV_HEADS * HEAD_DIM,), F32)
    zeros_v = jnp.zeros((KV_HEADS * HEAD_DIM,), F32)
    gain = jnp.concatenate([jnp.ones((ZA_COLS,), F32), jnp.tile(q_gain.astype(F32), N_HEADS) * HEAD_DIM ** -0.5,
                            jnp.ones((KV_ROW,), F32),
                            jnp.tile(ks_gain.astype(F32), KV_HEADS), ones_v,
                            jnp.tile(kw_gain.astype(F32), KV_HEADS), ones_v,
                            jnp.ones((GN_COLS,), F32)])
    mask = jnp.concatenate([jnp.zeros((ZA_COLS,), F32), jnp.ones((Q_COLS,), F32), jnp.zeros((KV_ROW,), F32),
                            ones_v, zeros_v, ones_v, zeros_v, jnp.zeros((GN_COLS,), F32)])
    return w, gain[None], mask[None]


def projection(x, norm_gain, packed):
    w, gain, mask = packed
    T = x.shape[0]
    tm = _pick(T, (1024, 512, 256, 128))
    seg = jnp.asarray((np.arange(PROJ_TN)[:, None] // HEAD_DIM == np.arange(PROJ_TN)[None, :] // HEAD_DIM)
                      .astype(np.float32), BF16)
    row = lambda i, j: (i, 0)
    return pl.pallas_call(
        _proj_kernel,
        grid=(T // tm, PROJ_COLS // PROJ_TN),
        in_specs=[pl.BlockSpec((tm, D_MODEL), row),
                  pl.BlockSpec((1, D_MODEL), lambda i, j: (0, 0)),
                  pl.BlockSpec((D_MODEL, PROJ_TN), lambda i, j: (0, j)),
                  pl.BlockSpec((PROJ_TN, PROJ_TN), lambda i, j: (0, 0)),
                  pl.BlockSpec((1, PROJ_TN), lambda i, j: (0, j)),
                  pl.BlockSpec((1, PROJ_TN), lambda i, j: (0, j))],
        out_specs=[pl.BlockSpec((tm, PROJ_TN), lambda i, j: (i, jnp.minimum(j, N_ZA - 1))),
                   pl.BlockSpec((tm, Q_COLS), row),
                   pl.BlockSpec((tm, KV_ROW), row), pl.BlockSpec((tm, KV_ROW), row), pl.BlockSpec((tm, KV_ROW), row),
                   pl.BlockSpec((tm, GN_COLS), row)],
        out_shape=[jax.ShapeDtypeStruct((T, ZA_COLS), F32), jax.ShapeDtypeStruct((T, Q_COLS), BF16),
                   jax.ShapeDtypeStruct((T, KV_ROW), F32), jax.ShapeDtypeStruct((T, KV_ROW), F32),
                   jax.ShapeDtypeStruct((T, KV_ROW), F32), jax.ShapeDtypeStruct((T, GN_COLS), F32)],
        scratch_shapes=[pltpu.VMEM((tm, D_MODEL), BF16)],
        compiler_params=pltpu.CompilerParams(dimension_semantics=("parallel", "arbitrary"),
                                             vmem_limit_bytes=VMEM_LIMIT_BYTES),
        name="projection",
    )(x, norm_gain.astype(F32)[None], w, seg, gain, mask)


CMP_PAIRS = CMP_STRIDE // 2
CMP_GW = KV_HEADS * CMP_HIDDEN


def _compress_kernel(*refs, n_pages, paged):
    if paged:
        pages = refs[1:1 + n_pages]
        eye_ref, w1_ref, pe_ref, w2_ref, kn_ref, ck_ref, cv_ref, half_ref = refs[1 + n_pages:]
        page = pages[0].shape[-1]
        m = n_pages * page // CMP_STRIDE
        for k, pg in enumerate(pages):
            for e in range(2):
                slab = pg[0, 0, e].reshape(KV_HEADS * HEAD_DIM, page).astype(BF16)
                half_ref[e, k * page:(k + 1) * page, :] = _nt(eye_ref[...], slab)
    else:
        rows_ref, w1_ref, pe_ref, w2_ref, kn_ref, ck_ref, cv_ref = refs
        m = rows_ref.shape[1] // (2 * CMP_STRIDE)
    for e, out_ref in enumerate((ck_ref, cv_ref)):
        acc = jnp.zeros((m, 2 * CMP_GW), F32)
        for p in range(CMP_PAIRS):
            def rows(s):
                if paged:
                    return half_ref[e, pl.ds(s, m, stride=CMP_STRIDE), :]
                return rows_ref[0, pl.ds(2 * s + e, m, stride=2 * CMP_STRIDE), :]
            a = jnp.concatenate([rows(2 * p), rows(2 * p + 1)], axis=1).astype(BF16)
            acc = acc + jnp.dot(a, w1_ref[e, p], preferred_element_type=F32)
        h = acc[:, :CMP_GW] + pltpu.roll(acc[:, CMP_GW:], m - 1, axis=0) + pe_ref[e]
        gl = jax.nn.gelu(h).astype(BF16)
        for g in range(KV_HEADS):
            c = jnp.dot(gl, w2_ref[e, g], preferred_element_type=F32)
            if e == 0:
                c = c * lax.rsqrt(jnp.sum(c * c, axis=1, keepdims=True) * (1.0 / HEAD_DIM) + EPS) * kn_ref[...]
            out_ref[0, g] = c.astype(out_ref.dtype)


def pack_compress(w1, w2, pe, kn):
    halves = w1.reshape(2, 2, CMP_STRIDE, HEAD_DIM, CMP_HIDDEN)
    eye = jnp.eye(KV_HEADS, dtype=w1.dtype)
    bd = jnp.einsum('ehsdf,gk->esgdhkf', halves, eye)
    w1p = bd.reshape(2, CMP_PAIRS, 2 * KV_HEADS * HEAD_DIM, 2 * CMP_GW).astype(BF16)
    pe_term = jnp.einsum('led,eldf->ef', pe, w1)
    pe_t = jnp.tile(pe_term, (1, KV_HEADS))[:, None, :].astype(F32)
    w2p = jnp.zeros((2, KV_HEADS, CMP_GW, LANES), w2.dtype)
    for g in range(KV_HEADS):
        w2p = w2p.at[:, g, g * CMP_HIDDEN:(g + 1) * CMP_HIDDEN, :HEAD_DIM].set(w2)
    knp = jnp.concatenate([kn.astype(F32), jnp.zeros((LANES - HEAD_DIM,), F32)])[None]
    return w1p, pe_t, w2p.astype(BF16), knp


def compress(rows, packed, page_table=None, layer=0):
    w1p, pe_t, w2p, knp = packed
    paged = page_table is not None
    if not paged:
        bx, n_pages = rows.shape[0], 1
        m = rows.shape[1] // (2 * CMP_STRIDE)
        data_specs = [pl.BlockSpec((1,) + rows.shape[1:], lambda b: (b, 0, 0))]
        const = lambda nd: (lambda b: (0,) * nd)
        out_map = lambda b: (b, 0, 0, 0)
        data, scratch = [rows], []
    else:
        bx, n_pages = page_table.shape
        page = rows.shape[-1]
        assert page == LANES and KV_HEADS * HEAD_DIM == LANES
        m = n_pages * page // CMP_STRIDE
        eye = jnp.asarray(np.eye(LANES, dtype=np.float32), BF16)
        const = lambda nd: (lambda b, pt: (0,) * nd)
        data_specs = [pl.BlockSpec((1, 1) + rows.shape[2:], lambda b, pt, k=k: (layer, pt[b, k], 0, 0, 0, 0))
                      for k in range(n_pages)] + [pl.BlockSpec(eye.shape, const(2))]
        out_map = lambda b, pt: (b, 0, 0, 0)
        data, scratch = [page_table] + [rows] * n_pages + [eye], [pltpu.VMEM((2, n_pages * page, LANES), F32)]
    in_specs = data_specs + [pl.BlockSpec(w1p.shape, const(4)), pl.BlockSpec(pe_t.shape, const(3)),
                             pl.BlockSpec(w2p.shape, const(4)), pl.BlockSpec(knp.shape, const(2))]
    out_spec = pl.BlockSpec((1, KV_HEADS, m, LANES), out_map)
    out_shape = jax.ShapeDtypeStruct((bx, KV_HEADS, m, LANES), BF16)
    grid_spec = pltpu.PrefetchScalarGridSpec(num_scalar_prefetch=int(paged), grid=(bx,), in_specs=in_specs,
                                             out_specs=[out_spec, out_spec], scratch_shapes=scratch)
    args = data + [w1p, pe_t, w2p, knp]
    return pl.pallas_call(
        functools.partial(_compress_kernel, n_pages=n_pages, paged=paged),
        grid_spec=grid_spec,
        out_shape=[out_shape, out_shape],
        compiler_params=pltpu.CompilerParams(dimension_semantics=("parallel",), vmem_limit_bytes=VMEM_LIMIT_BYTES),
        name="compress",
    )(*args)


MIX_TT = 512
CONV_HALO = SUBLANES
SCAN_UNROLL = 4


def _log1p(y):
    u = 1.0 + y
    return jnp.where(u == 1.0, y, jnp.log(u) * (y / jnp.where(u == 1.0, 1.0, u - 1.0)))


def _neg_expm1(x):
    t = jnp.tanh(0.5 * x)
    return -2.0 * t / (1.0 - t)


def _softplus(x):
    return jnp.maximum(x, 0.0) + _log1p(jnp.exp(-jnp.abs(x)))


def _rglru_coeffs(xc, wa_ref, ba_ref, wx_ref, bx_ref, lam_ref):
    xb = xc.astype(BF16)
    r = jax.nn.sigmoid(jnp.dot(xb, wa_ref[...], preferred_element_type=F32) + ba_ref[...])
    i = jax.nn.sigmoid(jnp.dot(xb, wx_ref[...], preferred_element_type=F32) + bx_ref[...])
    log_a = -RG_C * r * _softplus(-lam_ref[...])
    return jnp.exp(log_a), jnp.sqrt(_neg_expm1(2.0 * log_a)) * (i * xc)


def _pool_project(sums_minus, wp_ref, scale_ref):
    return jnp.dot(sums_minus.astype(BF16), wp_ref[...], preferred_element_type=F32) * scale_ref[...]


def _mixer_seq_kernel(xrg_ref, grg_ref, xpool_ref, conv0_ref, h0_ref, pool0_ref, cw_ref, cb_ref, wa_ref, ba_ref,
                      wx_ref, bx_ref, lam_ref, wp_ref, ps_ref, ya_ref, yc_ref, convn_ref, hn_ref, pooln_ref,
                      xe_ref, pe_ref, a_ref, u_ref, h_ref, carry_ref, *, start_pos):
    i = pl.program_id(1)
    tt = xrg_ref.shape[0]

    @pl.when(i == 0)
    def _():
        xe_ref[0:CONV_HALO, :] = conv0_ref[0]
        pe_ref[0:POOL_MAX, :] = pool0_ref[0]
        carry_ref[...] = h0_ref[0]

    xe_ref[CONV_HALO:CONV_HALO + tt, :] = xrg_ref[...]
    xc = cb_ref[...] + sum(cw_ref[k:k + 1, :] * xe_ref[CONV_HALO - (CONV_W - 1) + k:CONV_HALO - (CONV_W - 1) + k + tt, :]
                           for k in range(CONV_W))
    a, u = _rglru_coeffs(xc, wa_ref, ba_ref, wx_ref, bx_ref, lam_ref)
    a_ref[...] = a
    u_ref[...] = u

    row = lax.broadcasted_iota(jnp.int32, (SUBLANES, D_RNN), 0)

    def block(j, carry):
        r0 = pl.multiple_of(j * SUBLANES, SUBLANES)
        ab = a_ref[pl.ds(r0, SUBLANES), :]
        ub = u_ref[pl.ds(r0, SUBLANES), :]
        for d in (1, 2, 4):
            a_sh = jnp.where(row >= d, pltpu.roll(ab, d, axis=0), 1.0)
            u_sh = jnp.where(row >= d, pltpu.roll(ub, d, axis=0), 0.0)
            ub = ab * u_sh + ub
            ab = ab * a_sh
        hb = ab * carry + ub
        h_ref[pl.ds(r0, SUBLANES), :] = hb
        return jnp.broadcast_to(hb[SUBLANES - 1:SUBLANES, :], (SUBLANES, D_RNN))

    carry = lax.fori_loop(0, tt // SUBLANES, block, carry_ref[...], unroll=SCAN_UNROLL)
    carry_ref[...] = carry
    ya_ref[...] = (h_ref[...] * jax.nn.gelu(grg_ref[...])).astype(ya_ref.dtype)

    pe_ref[POOL_MAX:POOL_MAX + tt, :] = xpool_ref[...]
    pos = start_pos + i * tt + lax.broadcasted_iota(jnp.int32, (tt, 1), 0)
    parts = []
    for g, w in enumerate(POOL_WINDOWS):
        lanes = slice(g * POOL_GW, (g + 1) * POOL_GW)
        s = sum(pe_ref[POOL_MAX - k:POOL_MAX - k + tt, lanes] for k in range(w))
        cnt = jnp.minimum(pos + 1, w).astype(F32)
        parts.append(s / cnt - pe_ref[POOL_MAX:POOL_MAX + tt, lanes])
    yc_ref[...] = _pool_project(jnp.concatenate(parts, axis=1), wp_ref, ps_ref).astype(yc_ref.dtype)

    @pl.when(i == pl.num_programs(1) - 1)
    def _():
        convn_ref[0] = xe_ref[CONV_HALO + tt - (CONV_W - 1):CONV_HALO + tt, :]
        hn_ref[0] = carry[0:1, :]
        pooln_ref[0] = pe_ref[POOL_MAX + tt - (POOL_MAX - 1):POOL_MAX + tt, :]

    xe_ref[0:CONV_HALO, :] = xe_ref[tt:tt + CONV_HALO, :]
    pe_ref[0:POOL_MAX, :] = pe_ref[tt:tt + POOL_MAX, :]


def pack_mixer(conv_w, conv_b, w_a, b_a, w_x, b_x, lam, w_pool, scale):
    def block_diag(w):
        n, c, d = w.shape
        return jnp.einsum('ncd,nm->ncmd', w, jnp.eye(n, dtype=w.dtype)).reshape(n * c, n * d).astype(BF16)

    row = lambda v: v.astype(F32)[None]
    return (conv_w.astype(F32), row(conv_b), block_diag(w_a), row(b_a), block_diag(w_x), row(b_x), row(lam),
            block_diag(w_pool), row(scale))


def mixer_seq(za, batch, seq, packed, conv0, h0, pool0, start_pos):
    tt = min(MIX_TT, seq)
    nt = seq // tt
    conv_pad = jnp.pad(conv0.astype(F32), ((0, 0), (CONV_HALO - (CONV_W - 1), 0), (0, 0)))
    pool_pad = jnp.pad(pool0.astype(F32), ((0, 0), (1, 0), (0, 0)))
    h_pad = jnp.broadcast_to(h0.astype(F32)[:, None, :], (batch, SUBLANES, D_RNN))
    col = lambda c: pl.BlockSpec((tt, D_RNN), lambda b, i: (b * nt + i, c))
    state = lambda rows: pl.BlockSpec((1, rows, D_RNN), lambda b, i: (b, 0, 0))
    full = lambda a: pl.BlockSpec(a.shape, lambda b, i: (0,) * a.ndim)
    out_rows = pl.BlockSpec((tt, D_RNN), lambda b, i: (b * nt + i, 0))
    return pl.pallas_call(
        functools.partial(_mixer_seq_kernel, start_pos=start_pos),
        grid=(batch, nt),
        in_specs=[col(ZA_XRG // D_RNN), col(ZA_GRG // D_RNN), col(ZA_XPOOL // D_RNN), state(CONV_HALO), state(SUBLANES),
                  state(POOL_MAX)] + [full(a) for a in packed],
        out_specs=[out_rows, out_rows, state(CONV_W - 1), state(1), state(POOL_MAX - 1)],
        out_shape=[jax.ShapeDtypeStruct((batch * seq, D_RNN), BF16), jax.ShapeDtypeStruct((batch * seq, D_POOL), BF16),
                   jax.ShapeDtypeStruct((batch, CONV_W - 1, D_RNN), F32), jax.ShapeDtypeStruct((batch, 1, D_RNN), F32),
                   jax.ShapeDtypeStruct((batch, POOL_MAX - 1, D_POOL), F32)],
        scratch_shapes=[pltpu.VMEM((CONV_HALO + tt, D_RNN), F32), pltpu.VMEM((POOL_MAX + tt, D_POOL), F32),
                        pltpu.VMEM((tt, D_RNN), F32), pltpu.VMEM((tt, D_RNN), F32), pltpu.VMEM((tt, D_RNN), F32),
                        pltpu.VMEM((SUBLANES, D_RNN), F32)],
        compiler_params=pltpu.CompilerParams(dimension_semantics=("parallel", "arbitrary"),
                                             vmem_limit_bytes=VMEM_LIMIT_BYTES),
        name="mixer_seq",
    )(za, za, za, conv_pad, h_pad, pool_pad, *packed)


def _mixer_step_kernel(xrg_ref, grg_ref, xpool_ref, conv_ref, h0_ref, pool_ref, cw_ref, cb_ref, wa_ref, ba_ref,
                       wx_ref, bx_ref, lam_ref, wp_ref, ps_ref, ya_ref, yc_ref, hn_ref, *, start_pos):
    x = xrg_ref[...]
    xc = cb_ref[...] + cw_ref[CONV_W - 1:CONV_W, :] * x + sum(cw_ref[k:k + 1, :] * conv_ref[k] for k in range(CONV_W - 1))
    a, u = _rglru_coeffs(xc, wa_ref, ba_ref, wx_ref, bx_ref, lam_ref)
    h = a * h0_ref[...] + u
    hn_ref[...] = h
    ya_ref[...] = (h * jax.nn.gelu(grg_ref[...])).astype(ya_ref.dtype)
    xp = xpool_ref[...]
    parts = []
    for g, w in enumerate(POOL_WINDOWS):
        lanes = slice(g * POOL_GW, (g + 1) * POOL_GW)
        s = xp[:, lanes] + sum(pool_ref[POOL_MAX - 1 - k][:, lanes] for k in range(1, w))
        parts.append(s / float(min(start_pos + 1, w)) - xp[:, lanes])
    yc_ref[...] = _pool_project(jnp.concatenate(parts, axis=1), wp_ref, ps_ref).astype(yc_ref.dtype)


def mixer_step(za, packed, conv_state, h0, pool_state, start_pos):
    batch = za.shape[0]
    conv_t = jnp.swapaxes(conv_state.astype(F32), 0, 1)
    pool_t = jnp.swapaxes(pool_state.astype(F32), 0, 1)
    col = lambda c: pl.BlockSpec((batch, D_RNN), lambda i: (0, c))
    full = lambda a: pl.BlockSpec(a.shape, lambda i: (0,) * a.ndim)
    rows = pl.BlockSpec((batch, D_RNN), lambda i: (0, 0))
    return pl.pallas_call(
        functools.partial(_mixer_step_kernel, start_pos=start_pos),
        grid=(1,),
        in_specs=[col(ZA_XRG // D_RNN), col(ZA_GRG // D_RNN), col(ZA_XPOOL // D_RNN), full(conv_t), rows, full(pool_t)]
        + [full(a) for a in packed],
        out_specs=[rows, rows, rows],
        out_shape=[jax.ShapeDtypeStruct((batch, D_RNN), BF16), jax.ShapeDtypeStruct((batch, D_POOL), BF16),
                   jax.ShapeDtypeStruct((batch, D_RNN), F32)],
        compiler_params=pltpu.CompilerParams(vmem_limit_bytes=VMEM_LIMIT_BYTES),
        name="mixer_step",
    )(za, za, za, conv_t, h0.astype(F32), pool_t, *packed)


QB = Q_BLOCK
ROWS = HPG * QB
N_WIN_TILES = WINDOW // QB + 1
N_SEL_BIAS = REL_MAX_DIST // QB + 2
SEL_SPAN = 4
KV_CHUNK = 512


def _nsa_prompt_kernel(q_ref, ck_ref, cv_ref, kvs_ref, kvw_ref, gate_ref, bc_ref, tbs_ref, tbw_ref,
                       ovl_ref, eye_ref, pq_ref, pk_ref, pv_ref, onehot_ref, o_ref,
                       acc_ref, m_ref, qa_ref, comb_ref, ks_ref, vs_ref, kw_ref, vw_ref, cnt_ref, *, n_cmp, n_blk):
    qb = pl.program_id(2)
    t0 = qb * QB
    ncp = ck_ref.shape[2]
    seq = kvs_ref.shape[1]
    lane_row = lax.broadcasted_iota(jnp.int32, (1, LANES), 1)

    @pl.when(qb == 0)
    def _():
        ones_hi = jnp.where(lane_row >= HEAD_DIM, 1.0, 0.0)
        kw_ref[0:WINDOW, :] = jnp.broadcast_to(jnp.where(lane_row == HEAD_DIM, 1.0, 0.0), (WINDOW, LANES)).astype(BF16)
        vw_ref[0:WINDOW, :] = jnp.zeros((WINDOW, LANES), BF16)

        def stage(c, carry):
            r = pl.multiple_of(c * KV_CHUNK, KV_CHUNK)
            sel = kvs_ref[0, pl.ds(r, KV_CHUNK), :].astype(BF16)
            win = kvw_ref[0, pl.ds(r, KV_CHUNK), :].astype(BF16)
            ks_ref[pl.ds(r, KV_CHUNK), :] = (jnp.dot(sel, pk_ref[0], preferred_element_type=F32)
                                             + onehot_ref[pl.ds(r, KV_CHUNK), :].astype(F32)).astype(BF16)
            vs_ref[pl.ds(r, KV_CHUNK), :] = (jnp.dot(sel, pv_ref[0], preferred_element_type=F32) + ones_hi).astype(BF16)
            kw_ref[pl.ds(WINDOW + r, KV_CHUNK), :] = jnp.dot(win, pk_ref[0], preferred_element_type=F32).astype(BF16)
            vw_ref[pl.ds(WINDOW + r, KV_CHUNK), :] = (jnp.dot(win, pv_ref[0], preferred_element_type=F32)
                                                      + ones_hi).astype(BF16)
            return carry

        lax.fori_loop(0, seq // KV_CHUNK, stage, 0)

    q4 = jnp.dot(q_ref[0], pq_ref[...], preferred_element_type=F32)
    q3 = jnp.concatenate([q4[:, j * LANES:(j + 1) * LANES] for j in range(HPG)], axis=0)
    q = q3.astype(BF16)
    sig = jax.nn.sigmoid(gate_ref[0])
    gates = [jnp.concatenate([sig[:, 3 * j + c:3 * j + c + 1] for j in range(HPG)], axis=0) for c in range(3)]

    lc = _nt(q, ck_ref[0, 0]) + bc_ref[0].reshape(ROWS, ncp)
    tok = t0 + lax.broadcasted_iota(jnp.int32, (HPG, QB, ncp), 1).reshape(ROWS, ncp)
    col = lax.broadcasted_iota(jnp.int32, (ROWS, ncp), 1)
    valid = (tok >= col * CMP_STRIDE + (L_CMP - 1)) & (col < n_cmp)
    lc = jnp.where(valid, lc, NEG)
    mx = jnp.max(lc, axis=1, keepdims=True)
    p = jnp.where(valid, jnp.exp(lc - mx), 0.0)
    pc = p / jnp.maximum(jnp.sum(p, axis=1, keepdims=True), 1e-30)
    comb_ref[...] = gates[0] * jnp.dot(pc.astype(BF16), cv_ref[0, 0], preferred_element_type=F32)

    pcs = pc[0:QB] + pc[QB:2 * QB] + pc[2 * QB:3 * QB] + pc[3 * QB:4 * QB]
    hi = pcs.astype(BF16)
    lo = (pcs - hi.astype(F32)).astype(BF16)
    imp = _nt(ovl_ref[...], hi) + _nt(ovl_ref[...], lo)
    blk = lax.broadcasted_iota(jnp.int32, (n_blk, QB), 0)
    cur = (t0 + lax.broadcasted_iota(jnp.int32, (n_blk, QB), 1)) // SEL_BLOCK
    forced = (blk == 0) | (blk == cur) | (blk == cur - 1)
    score = jnp.where(blk > cur, -1.0, jnp.where(forced, FORCE, imp))
    chunks = [score[r:r + SUBLANES] for r in range(0, n_blk, SUBLANES)]
    sub = lax.broadcasted_iota(jnp.int32, (SUBLANES, QB), 0)
    cnt_ref[...] = jnp.zeros((n_blk, QB), F32)
    last_blk = (t0 + QB - 1) // SEL_BLOCK
    for mc in range(0, n_blk, SUBLANES):
        @pl.when(mc <= last_blk)
        def _(mc=mc):
            for r, ch in enumerate(chunks):
                first = r * SUBLANES
                part = jnp.zeros((SUBLANES, QB), F32)
                for m in range(mc, mc + SUBLANES):
                    row = jnp.broadcast_to(score[m:m + 1, :], (SUBLANES, QB))
                    if first > m:
                        beats = jnp.where(row >= ch, 1.0, 0.0)
                    elif first + SUBLANES - 1 < m:
                        beats = jnp.where(row > ch, 1.0, 0.0)
                    else:
                        beats = jnp.where(sub + first > m, jnp.where(row >= ch, 1.0, 0.0), jnp.where(row > ch, 1.0, 0.0))
                    part = part + beats
                cnt_ref[first:first + SUBLANES, :] += part
    sel_neg = jnp.where(cnt_ref[...] < float(min(N_SELECT, n_blk)), 0.0, NEG)
    pieces = [jnp.zeros((HEAD_DIM, QB), F32), sel_neg]
    if n_blk < HEAD_DIM:
        pieces.append(jnp.zeros((HEAD_DIM - n_blk, QB), F32))
    placed_t = jnp.concatenate(pieces, axis=0).astype(BF16)
    placed = _nt(eye_ref[...], placed_t)
    qa_ref[...] = (q3 + jnp.concatenate([placed] * HPG, axis=0)).astype(BF16)

    m_ref[...] = jnp.full((ROWS, LANES), M_INIT, F32)
    acc_ref[...] = jnp.zeros((ROWS, LANES), F32)
    n_bias = tbs_ref.shape[1]

    span = SEL_SPAN * QB
    n_spans = qb // SEL_SPAN + 1

    def body(kk, carry):
        off = pl.multiple_of(kk * span, span)
        s = _nt(qa_ref[...], ks_ref[pl.ds(off, span), :])
        parts = []
        for u in range(SEL_SPAN):
            idx = jnp.clip(qb - (kk * SEL_SPAN + u), -1, n_bias - 2) + 1
            parts.append(s[:, u * QB:(u + 1) * QB] + tbs_ref[:, pl.ds(idx, 1)].reshape(ROWS, LANES))
        tile_max = functools.reduce(jnp.maximum, parts)
        m_old = m_ref[...]
        m_new = jnp.maximum(m_old, jnp.max(tile_max, axis=1, keepdims=True))
        alpha = jnp.exp(m_old - m_new)
        pr = jnp.concatenate([jnp.exp(x - m_new).astype(BF16) for x in parts], axis=1)
        acc_ref[...] = alpha * acc_ref[...] + jnp.dot(pr, vs_ref[pl.ds(off, SEL_SPAN * QB), :],
                                                      preferred_element_type=F32)
        m_ref[...] = m_new
        return carry

    lax.fori_loop(0, n_spans, body, 0)
    acc = acc_ref[...]
    comb_ref[...] += gates[1] * (acc / pltpu.roll(acc, HEAD_DIM, axis=1))

    qw = (q3 + jnp.where(lane_row == HEAD_DIM, NEG, 0.0)).astype(BF16)
    w_off = pl.multiple_of(t0, QB)
    sw = _nt(qw, kw_ref[pl.ds(w_off, WINDOW + QB), :]) + tbw_ref[...].reshape(ROWS, WINDOW + QB)
    pw = jnp.exp(sw - jnp.max(sw, axis=1, keepdims=True))
    accw = jnp.dot(pw.astype(BF16), vw_ref[pl.ds(w_off, WINDOW + QB), :], preferred_element_type=F32)
    comb_ref[...] += gates[2] * (accw / pltpu.roll(accw, HEAD_DIM, axis=1))

    comb = comb_ref[...]
    lane = lax.broadcasted_iota(jnp.int32, (QB, LANES), 1)
    for half in range(HPG // 2):
        a = comb[(2 * half) * QB:(2 * half + 1) * QB]
        b = comb[(2 * half + 1) * QB:(2 * half + 2) * QB]
        o_ref[0, :, half * LANES:(half + 1) * LANES] = jnp.where(lane < HEAD_DIM, a,
                                                                 pltpu.roll(b, HEAD_DIM, axis=1)).astype(o_ref.dtype)


def _bias_by_distance(rel_bias):
    max_d = (N_SEL_BIAS + 1) * QB
    return jnp.transpose(rel_bias.astype(F32)[rel_bucket(jnp.arange(max_d))])


def _ext(tab_t, lo, hi, ok_lo=None, ok_hi=None):
    n_heads, depth = tab_t.shape
    y = np.arange(lo, hi)
    parts = [jnp.broadcast_to(tab_t[:, :1], (n_heads, int(np.sum(y < 0)))), tab_t[:, max(lo, 0):max(min(hi, depth), 0)],
             jnp.broadcast_to(tab_t[:, -1:], (n_heads, int(np.sum(y > depth - 1))))]
    arr = jnp.concatenate(parts, axis=1)
    ok = np.ones(y.shape, bool)
    if ok_lo is not None:
        ok &= y >= ok_lo
    if ok_hi is not None:
        ok &= y <= ok_hi
    return jnp.where(jnp.asarray(ok)[None], arr, NEG)


def _toeplitz_tiles(ext, lo, ks):
    w = jnp.stack([ext[:, QB * k - (QB - 1) - lo:QB * k - (QB - 1) - lo + 2 * QB] for k in ks], axis=1)
    skew = jnp.tile(w, (1, 1, QB + 1))[..., :QB * (2 * QB + 1)].reshape(w.shape[:2] + (QB, 2 * QB + 1))[..., :QB]
    return skew[..., ::-1]


def rel_bias_tables(rel_bias, seq):
    tab_t = _bias_by_distance(rel_bias)
    max_d = tab_t.shape[1]
    lo = -(2 * QB - 1)
    tbs = _toeplitz_tiles(_ext(tab_t, lo, max_d, 0, None), lo, range(-1, N_SEL_BIAS))
    tbw = _toeplitz_tiles(_ext(tab_t, lo, max_d, 0, WINDOW), lo, range(N_WIN_TILES - 1, -1, -1))
    tbw = jnp.transpose(tbw, (0, 2, 1, 3)).reshape(N_HEADS, QB, N_WIN_TILES * QB)
    nqb = seq // QB
    ncp = seq // CMP_STRIDE
    per_qb = QB // CMP_STRIDE
    width = ncp + per_qb * (nqb - 1)
    c0 = QB * (nqb - 1) - (L_CMP - 1)
    ext = _ext(tab_t, c0 - CMP_STRIDE * (width - 1), c0 + QB)
    rows = []
    for a in range(per_qb):
        for r in range(CMP_STRIDE):
            f = ext[:, r::CMP_STRIDE][:, :width + per_qb - 1][:, ::-1]
            rows.append(f[:, per_qb - 1 - a:per_qb - 1 - a + width])
    v = jnp.stack(rows, axis=1)
    bc = jnp.stack([v[:, :, per_qb * (nqb - 1 - b):per_qb * (nqb - 1 - b) + ncp] for b in range(nqb)])
    return tbs, tbw, bc


def nsa_prompt_pallas(q, ck, cv, kv_sel, kv_win, gn, tables):
    B, S = q.shape[:2]
    assert S % (SEL_SPAN * QB) == 0 and S % KV_CHUNK == 0 and S // SEL_BLOCK <= HEAD_DIM
    tbs, tbw, bc = tables
    nqb = S // QB
    ncp = S // CMP_STRIDE
    n_cmp = ncp - 1
    n_blk = S // SEL_BLOCK
    c0 = np.arange(ncp)[None, :] * CMP_STRIDE
    s0 = np.arange(n_blk)[:, None] * SEL_BLOCK
    ovl_t = ((c0 <= s0 + SEL_BLOCK - 1) & (c0 + L_CMP - 1 >= s0) & (np.arange(ncp)[None, :] < n_cmp))
    ovl_t = jnp.asarray(ovl_t.astype(np.float32), BF16)
    eye = jnp.asarray(np.eye(QB, dtype=np.float32), BF16)
    gw = HPG * HEAD_DIM
    pq = np.zeros((gw, HPG * LANES), np.float32)
    pq[np.arange(gw), (np.arange(gw) // HEAD_DIM) * LANES + np.arange(gw) % HEAD_DIM] = 1.0
    pk = np.zeros((KV_HEADS, KV_ROW, LANES), np.float32)
    pv = np.zeros((KV_HEADS, KV_ROW, LANES), np.float32)
    for g in range(KV_HEADS):
        pk[g, g * HEAD_DIM + np.arange(HEAD_DIM), np.arange(HEAD_DIM)] = 1.0
        pv[g, (KV_HEADS + g) * HEAD_DIM + np.arange(HEAD_DIM), np.arange(HEAD_DIM)] = 1.0
    onehot = np.zeros((S, LANES), np.float32)
    onehot[np.arange(S), HEAD_DIM + np.arange(S) // SEL_BLOCK] = 1.0

    kv_spec = pl.BlockSpec((1, S, KV_ROW), lambda b, g, i: (b, 0, 0))
    cmp_spec = pl.BlockSpec((1, 1, ncp, LANES), lambda b, g, i: (b, g, 0, 0))
    const2 = lambda b, g, i: (0, 0)
    return pl.pallas_call(
        functools.partial(_nsa_prompt_kernel, n_cmp=n_cmp, n_blk=n_blk),
        grid=(B, KV_HEADS, nqb),
        in_specs=[
            pl.BlockSpec((1, QB, gw), lambda b, g, i: (b, i, g)),
            cmp_spec, cmp_spec, kv_spec, kv_spec,
            pl.BlockSpec((1, QB, LANES), lambda b, g, i: (b, i, g)),
            pl.BlockSpec((1, HPG, QB, ncp), lambda b, g, i: (i, g, 0, 0)),
            pl.BlockSpec((HPG, N_SEL_BIAS + 1, QB, QB), lambda b, g, i: (g, 0, 0, 0)),
            pl.BlockSpec((HPG, QB, N_WIN_TILES * QB), lambda b, g, i: (g, 0, 0)),
            pl.BlockSpec((n_blk, ncp), const2),
            pl.BlockSpec((QB, QB), const2),
            pl.BlockSpec((gw, HPG * LANES), const2),
            pl.BlockSpec((1, KV_ROW, LANES), lambda b, g, i: (g, 0, 0)),
            pl.BlockSpec((1, KV_ROW, LANES), lambda b, g, i: (g, 0, 0)),
            pl.BlockSpec((S, LANES), const2),
        ],
        out_specs=pl.BlockSpec((1, QB, gw), lambda b, g, i: (b, i, g)),
        out_shape=jax.ShapeDtypeStruct((B, S, N_HEADS * HEAD_DIM), BF16),
        scratch_shapes=[pltpu.VMEM((ROWS, LANES), F32), pltpu.VMEM((ROWS, LANES), F32),
                        pltpu.VMEM((ROWS, LANES), BF16), pltpu.VMEM((ROWS, LANES), F32),
                        pltpu.VMEM((S, LANES), BF16), pltpu.VMEM((S, LANES), BF16),
                        pltpu.VMEM((S + WINDOW, LANES), BF16), pltpu.VMEM((S + WINDOW, LANES), BF16),
                        pltpu.VMEM((n_blk, QB), F32)],
        compiler_params=pltpu.CompilerParams(dimension_semantics=("parallel", "parallel", "arbitrary"),
                                             vmem_limit_bytes=VMEM_LIMIT_BYTES),
        name="nsa_prompt",
    )(q, ck, cv, kv_sel, kv_win, gn, bc, tbs, tbw, ovl_t, eye, jnp.asarray(pq, BF16), jnp.asarray(pk, BF16),
      jnp.asarray(pv, BF16), jnp.asarray(onehot, BF16))


STEP_ROWS = SUBLANES
STAGE_PAGES = 8
RANK_LANES = 2 * LANES


def _split3(x):
    hi = x.astype(BF16)
    r1 = x - hi.astype(F32)
    mid = r1.astype(BF16)
    return hi, mid, (r1 - mid.astype(F32)).astype(BF16)


def _nsa_step_kernel(*refs, n_pages, n_cmp):
    pages = refs[1:1 + n_pages]
    (q_ref, ck_ref, cv_ref, kvs_ref, kvw_ref, win_ref, gn_ref, bcs_ref, bsel_ref, bwin_ref, ovl_ref, oh_ref,
     pq_ref, pk_ref, pv_ref, o_ref, kc_ref, vs_ref, kw_ref, vw_ref) = refs[1 + n_pages:]
    page = pages[0].shape[-1]
    past = n_pages * page
    wb = win_ref.shape[-1]
    ncp = ck_ref.shape[2]
    first_row = lax.broadcasted_iota(jnp.int32, (LANES, 1), 0) == 0

    @pl.when(pl.program_id(0) == 0)
    def _():
        for g in range(KV_HEADS):
            kc_ref[g, HEAD_DIM:LANES, :] = jnp.zeros((LANES - HEAD_DIM, past + LANES), BF16)
            kc_ref[g, LANES:2 * LANES, 0:past] = oh_ref[...]
            kc_ref[g, LANES:2 * LANES, past:past + LANES] = jnp.zeros((LANES, LANES), BF16)
            vs_ref[g, HEAD_DIM:LANES, :] = jnp.ones((LANES - HEAD_DIM, past + LANES), BF16)
            kw_ref[g, HEAD_DIM:LANES, :] = jnp.zeros((LANES - HEAD_DIM, wb + LANES), BF16)
            vw_ref[g, HEAD_DIM:LANES, :] = jnp.ones((LANES - HEAD_DIM, wb + LANES), BF16)

    def place_new(row_ref, c0, k_dst, v_dst):
        new = jnp.where(first_row, row_ref[0], 0.0).astype(BF16)
        for g in range(KV_HEADS):
            k_dst[g, 0:HEAD_DIM, c0:c0 + LANES] = _nt(pk_ref[g], new)[0:HEAD_DIM].astype(BF16)
            v_dst[g, 0:HEAD_DIM, c0:c0 + LANES] = _nt(pv_ref[g], new)[0:HEAD_DIM].astype(BF16)

    for k, pg in enumerate(pages):
        for g in range(KV_HEADS):
            kc_ref[g, 0:HEAD_DIM, k * page:(k + 1) * page] = pg[0, 0, 0, g].astype(BF16)
            vs_ref[g, 0:HEAD_DIM, k * page:(k + 1) * page] = pg[0, 0, 1, g].astype(BF16)
    place_new(kvs_ref, past, kc_ref, vs_ref)
    for g in range(KV_HEADS):
        kw_ref[g, 0:HEAD_DIM, 0:wb] = win_ref[0, 0, 0, g].astype(BF16)
        vw_ref[g, 0:HEAD_DIM, 0:wb] = win_ref[0, 0, 1, g].astype(BF16)
    place_new(kvw_ref, wb, kw_ref, vw_ref)

    sig = jax.nn.sigmoid(gn_ref[0])
    lane = lax.broadcasted_iota(jnp.int32, (1, LANES), 1)
    rr = lax.broadcasted_iota(jnp.int32, (RANK_LANES, RANK_LANES), 0)
    cc = lax.broadcasted_iota(jnp.int32, (RANK_LANES, RANK_LANES), 1)
    ones_sq = jnp.ones((RANK_LANES, RANK_LANES), BF16)
    pad_rows = jnp.zeros((STEP_ROWS - HPG, LANES), F32)
    for g in range(KV_HEADS):
        gw = HPG * HEAD_DIM
        q4 = jnp.dot(q_ref[0][:, g * gw:(g + 1) * gw], pq_ref[...], preferred_element_type=F32)
        q8 = jnp.concatenate([q4[:, j * LANES:(j + 1) * LANES] for j in range(HPG)] + [pad_rows], axis=0)
        gates = [jnp.concatenate([sig[:, g * LANES + 3 * j + c:g * LANES + 3 * j + c + 1] for j in range(HPG)]
                                 + [pad_rows[:, 0:1]], axis=0) for c in range(3)]

        lc = _nt(q8.astype(BF16), ck_ref[0, g]) + bcs_ref[g]
        col = lax.broadcasted_iota(jnp.int32, (STEP_ROWS, ncp), 1)
        valid = col < n_cmp
        lc = jnp.where(valid, lc, NEG)
        p = jnp.where(valid, jnp.exp(lc - jnp.max(lc, axis=1, keepdims=True)), 0.0)
        pc = p / jnp.maximum(jnp.sum(p, axis=1, keepdims=True), 1e-30)
        comb = gates[0] * jnp.dot(pc.astype(BF16), cv_ref[0, g], preferred_element_type=F32)

        pcs = jnp.broadcast_to(jnp.sum(pc[0:HPG], axis=0, keepdims=True), (STEP_ROWS, ncp))
        hi = pcs.astype(BF16)
        lo = (pcs - hi.astype(F32)).astype(BF16)
        imp = (jnp.dot(hi, ovl_ref[...], preferred_element_type=F32)
               + jnp.dot(lo, ovl_ref[...], preferred_element_type=F32))[0:1]
        blk_id = lax.broadcasted_iota(jnp.int32, (1, RANK_LANES), 1)
        cur = past // SEL_BLOCK
        forced = (blk_id == 0) | (blk_id == cur) | (blk_id == cur - 1)
        score = jnp.where(blk_id > cur, -1.0, jnp.where(forced, FORCE, imp))
        s_n = jnp.broadcast_to(score, (RANK_LANES, RANK_LANES))
        diag = jnp.where(rr == cc, s_n, 0.0)
        s_m = sum(_nt(part, ones_sq) for part in _split3(diag))
        beats = jnp.where(rr < cc, jnp.where(s_m >= s_n, 1.0, 0.0), jnp.where(s_m > s_n, 1.0, 0.0))
        rank = jnp.sum(beats, axis=0, keepdims=True)
        sel_neg = jnp.where(rank < float(N_SELECT), 0.0, NEG)

        qa = jnp.concatenate([q8, jnp.broadcast_to(sel_neg[:, :LANES], (STEP_ROWS, LANES))], axis=1).astype(BF16)
        s_past = jnp.dot(qa, kc_ref[g, :, 0:past], preferred_element_type=F32) + bsel_ref[g, :, 0:past]
        s_new = (jnp.dot(qa, kc_ref[g, :, past:past + LANES], preferred_element_type=F32)
                 + bsel_ref[g, :, past:past + LANES] + sel_neg[:, LANES:LANES + 1])
        mx = jnp.maximum(jnp.max(s_past, axis=1, keepdims=True), jnp.max(s_new, axis=1, keepdims=True))
        acc = (_nt(jnp.exp(s_past - mx).astype(BF16), vs_ref[g, :, 0:past])
               + _nt(jnp.exp(s_new - mx).astype(BF16), vs_ref[g, :, past:past + LANES]))
        comb = comb + gates[1] * (acc / pltpu.roll(acc, HEAD_DIM, axis=1))

        sw = jnp.dot(q8.astype(BF16), kw_ref[g], preferred_element_type=F32) + bwin_ref[g]
        pw = jnp.exp(sw - jnp.max(sw, axis=1, keepdims=True))
        accw = _nt(pw.astype(BF16), vw_ref[g])
        comb = comb + gates[2] * (accw / pltpu.roll(accw, HEAD_DIM, axis=1))

        for half in range(HPG // 2):
            a = comb[2 * half:2 * half + 1]
            b = comb[2 * half + 1:2 * half + 2]
            o_ref[0, :, g * gw + half * LANES:g * gw + (half + 1) * LANES] = jnp.where(
                lane < HEAD_DIM, a, pltpu.roll(b, HEAD_DIM, axis=1)).astype(o_ref.dtype)


def step_bias_tables(rel_bias, past, wb, ncp):
    tab_t = _bias_by_distance(rel_bias)

    def rows(t):
        return jnp.pad(t.reshape(KV_HEADS, HPG, -1), ((0, 0), (0, STEP_ROWS - HPG), (0, 0)))

    d_first = past - (L_CMP - 1)
    cmp_t = _ext(tab_t, d_first - CMP_STRIDE * (ncp - 1), d_first + 1, 0, None)[:, ::CMP_STRIDE][:, ::-1]
    sel_t = _ext(tab_t, -(LANES - 1), past + 1, 0, None)[:, ::-1]
    win_t = _ext(tab_t, -(LANES - 1), wb + 1, 0, WINDOW)[:, ::-1]
    return rows(cmp_t), rows(sel_t), rows(win_t)


def nsa_step(q, ck, cv, kvs_new, kvw_new, sel_pool, win_buf, gn, page_table, tables, layer):
    B, n_pages = page_table.shape
    page = sel_pool.shape[-1]
    past = n_pages * page
    wb = win_buf.shape[-1]
    ncp = ck.shape[2]
    n_blk = past // SEL_BLOCK + 1
    assert n_blk <= LANES + 1 and page == LANES and wb % LANES == 0
    bcs, bsel, bwin = tables
    c0 = np.arange(ncp)[:, None] * CMP_STRIDE
    s0 = np.arange(RANK_LANES)[None, :] * SEL_BLOCK
    ovl = ((c0 <= s0 + SEL_BLOCK - 1) & (c0 + L_CMP - 1 >= s0) & (np.arange(ncp)[:, None] < ncp - 1)
           & (np.arange(RANK_LANES)[None, :] < n_blk))
    onehot = np.zeros((LANES, past), np.float32)
    onehot[np.arange(past) // SEL_BLOCK, np.arange(past)] = 1.0
    gw = HPG * HEAD_DIM
    pq = np.zeros((gw, HPG * LANES), np.float32)
    pq[np.arange(gw), (np.arange(gw) // HEAD_DIM) * LANES + np.arange(gw) % HEAD_DIM] = 1.0
    pk = np.zeros((KV_HEADS, LANES, KV_ROW), np.float32)
    pv = np.zeros((KV_HEADS, LANES, KV_ROW), np.float32)
    for g in range(KV_HEADS):
        pk[g, np.arange(HEAD_DIM), g * HEAD_DIM + np.arange(HEAD_DIM)] = 1.0
        pv[g, np.arange(HEAD_DIM), (KV_HEADS + g) * HEAD_DIM + np.arange(HEAD_DIM)] = 1.0
    consts = [jnp.asarray(ovl.astype(np.float32), BF16), jnp.asarray(onehot, BF16), jnp.asarray(pq, BF16),
              jnp.asarray(pk, BF16), jnp.asarray(pv, BF16)]

    row3 = lambda a: a.reshape(B, 1, a.shape[-1])
    per_b = lambda shape: pl.BlockSpec((1,) + shape, lambda b, pt: (b,) + (0,) * len(shape))
    full = lambda a: pl.BlockSpec(a.shape, lambda b, pt: (0,) * a.ndim)
    slab = (2, KV_HEADS, HEAD_DIM)
    page_specs = [pl.BlockSpec((1, 1) + slab + (page,), lambda b, pt, k=k: (layer, pt[b, k], 0, 0, 0, 0))
                  for k in range(n_pages)]
    win_spec = pl.BlockSpec((1, 1) + slab + (wb,), lambda b, pt: (layer, b, 0, 0, 0, 0))
    in_specs = page_specs + [per_b((1, Q_COLS)), per_b((KV_HEADS, ncp, LANES)), per_b((KV_HEADS, ncp, LANES)),
                             per_b((1, KV_ROW)), per_b((1, KV_ROW)), win_spec, per_b((1, GN_COLS)),
                             full(bcs), full(bsel), full(bwin)] + [full(c) for c in consts]
    grid_spec = pltpu.PrefetchScalarGridSpec(
        num_scalar_prefetch=1, grid=(B,), in_specs=in_specs, out_specs=per_b((1, Q_COLS)),
        scratch_shapes=[pltpu.VMEM((KV_HEADS, 2 * LANES, past + LANES), BF16), pltpu.VMEM((KV_HEADS, LANES, past + LANES), BF16),
                        pltpu.VMEM((KV_HEADS, LANES, wb + LANES), BF16), pltpu.VMEM((KV_HEADS, LANES, wb + LANES), BF16)])
    out = pl.pallas_call(
        functools.partial(_nsa_step_kernel, n_pages=n_pages, n_cmp=ncp - 1),
        grid_spec=grid_spec,
        out_shape=jax.ShapeDtypeStruct((B, 1, Q_COLS), BF16),
        compiler_params=pltpu.CompilerParams(dimension_semantics=("arbitrary",), vmem_limit_bytes=VMEM_LIMIT_BYTES),
        name="nsa_step",
    )(page_table, *([sel_pool] * n_pages), row3(q), ck, cv, row3(kvs_new), row3(kvw_new), win_buf, row3(gn),
      bcs, bsel, bwin, *consts)
    return out.reshape(B, Q_COLS)


def _merge_kernel(ya_ref, yb_ref, yc_ref, ga_ref, gb_ref, gc_ref, x_ref, wa_ref, wb_ref, wc_ref, wo_ref, o_ref):
    def branch(y_ref, g_ref, w_ref):
        return jax.nn.sigmoid(g_ref[...]) * jnp.dot(y_ref[...], w_ref[...], preferred_element_type=F32)

    merged = branch(ya_ref, ga_ref, wa_ref) + branch(yb_ref, gb_ref, wb_ref) + branch(yc_ref, gc_ref, wc_ref)
    o_ref[...] = x_ref[...] + jnp.dot(merged.astype(BF16), wo_ref[...], preferred_element_type=F32)


def merge(ya, yb, yc, za, x, wa, wb, wc, wo):
    T = x.shape[0]
    tm = _pick(T, (512, 256, 128))
    y_spec = pl.BlockSpec((tm, ya.shape[1]), lambda i: (i, 0))
    w_spec = pl.BlockSpec((ya.shape[1], D_MODEL), lambda i: (0, 0))
    return pl.pallas_call(
        _merge_kernel,
        grid=(T // tm,),
        in_specs=[y_spec, y_spec, y_spec,
                  pl.BlockSpec((tm, D_MODEL), lambda i: (i, 0)), pl.BlockSpec((tm, D_MODEL), lambda i: (i, 1)),
                  pl.BlockSpec((tm, D_MODEL), lambda i: (i, 2)),
                  pl.BlockSpec((tm, D_MODEL), lambda i: (i, 0)),
                  w_spec, w_spec, w_spec, pl.BlockSpec((D_MODEL, D_MODEL), lambda i: (0, 0))],
        out_specs=pl.BlockSpec((tm, D_MODEL), lambda i: (i, 0)),
        out_shape=jax.ShapeDtypeStruct((T, D_MODEL), F32),
        compiler_params=pltpu.CompilerParams(dimension_semantics=("parallel",), vmem_limit_bytes=VMEM_LIMIT_BYTES),
        name="merge",
    )(ya, yb, yc, za, za, za, x, wa, wb, wc, wo)


def _ffn_kernel(*refs, routed):
    if routed:
        h_ref, g_ref, wr_ref, br_ref, wg_ref, wu_ref, wd_ref, o_ref, xn_ref, gate_ref = refs
    else:
        h_ref, g_ref, wg_ref, wu_ref, wd_ref, o_ref, xn_ref = refs
    e = pl.program_id(1)
    f = pl.program_id(2)

    @pl.when((e == 0) & (f == 0))
    def _():
        h = h_ref[...]
        xb = _rms_rows(h, g_ref[...]).astype(BF16)
        xn_ref[...] = xb
        o_ref[...] = h
        if routed:
            logits = jnp.dot(xb, wr_ref[...], preferred_element_type=F32) + br_ref[...]
            lane = lax.broadcasted_iota(jnp.int32, logits.shape, 1)
            m1 = jnp.max(logits, axis=1, keepdims=True)
            i1 = jnp.min(jnp.where(logits == m1, lane, LANES), axis=1, keepdims=True)
            rest = jnp.where(lane == i1, M_INIT, logits)
            m2 = jnp.max(rest, axis=1, keepdims=True)
            i2 = jnp.min(jnp.where(rest == m2, lane, LANES), axis=1, keepdims=True)
            r = jnp.exp(m2 - m1)
            gate_ref[...] = jnp.where(lane == i1, 1.0 / (1.0 + r), 0.0) + jnp.where(lane == i2, r / (1.0 + r), 0.0)

    xb = xn_ref[...]
    a = jnp.dot(xb, wg_ref[0], preferred_element_type=F32)
    u = jnp.dot(xb, wu_ref[0], preferred_element_type=F32)
    t = (a * jax.nn.sigmoid(a)) * u
    y = jnp.dot(t.astype(BF16), wd_ref[0], preferred_element_type=F32)
    if routed:
        lane = lax.broadcasted_iota(jnp.int32, gate_ref.shape, 1)
        y = jnp.sum(jnp.where(lane == e, gate_ref[...], 0.0), axis=1, keepdims=True) * y
    o_ref[...] += y


def channel_mixer(h, norm_gain, wg, wu, wd, router=None):
    T = h.shape[0]
    E, _, F = wg.shape
    tm = _pick(T, (512, 256, 128))
    tf = _pick(F, (1408, 1024, 512, 256, 128))
    routed = router is not None
    row = lambda i, e, f: (i, 0)
    in_specs = [pl.BlockSpec((tm, D_MODEL), row), pl.BlockSpec((1, D_MODEL), lambda i, e, f: (0, 0))]
    args = [h, norm_gain.astype(F32)[None]]
    scratch = [pltpu.VMEM((tm, D_MODEL), BF16)]
    if routed:
        in_specs += [pl.BlockSpec((D_MODEL, LANES), lambda i, e, f: (0, 0)), pl.BlockSpec((1, LANES), lambda i, e, f: (0, 0))]
        args += list(router)
        scratch.append(pltpu.VMEM((tm, LANES), F32))
    in_specs += [pl.BlockSpec((1, D_MODEL, tf), lambda i, e, f: (e, 0, f)),
                 pl.BlockSpec((1, D_MODEL, tf), lambda i, e, f: (e, 0, f)),
                 pl.BlockSpec((1, tf, D_MODEL), lambda i, e, f: (e, f, 0))]
    args += [wg, wu, wd]
    return pl.pallas_call(
        functools.partial(_ffn_kernel, routed=routed),
        grid=(T // tm, E, F // tf),
        in_specs=in_specs,
        out_specs=pl.BlockSpec((tm, D_MODEL), row),
        out_shape=jax.ShapeDtypeStruct((T, D_MODEL), F32),
        scratch_shapes=scratch,
        compiler_params=pltpu.CompilerParams(dimension_semantics=("parallel", "arbitrary", "arbitrary"),
                                             vmem_limit_bytes=VMEM_LIMIT_BYTES),
        name="moe" if routed else "ffn",
    )(*args)


def mixer_tail(x2, za, y_a, y_b, y_c, l, P):
    return merge(y_a, y_b, y_c, za, x2, P['w_br_rg'][l].astype(BF16), P['w_br_attn'][l].astype(BF16),
                 P['w_br_pool'][l].astype(BF16), P['w_out'][l].astype(BF16))


def prompt_mixer(x, l, P, packed, tables):
    B, T, _ = x.shape
    x2 = x.reshape(B * T, D_MODEL)
    za, q, kv_c, kv_s, kv_w, gn = projection(x2, P['attn_norm'][l], packed['proj'])
    zeros = lambda rows: jnp.zeros((B, rows, D_RNN), F32)
    y_a, y_c, conv_new, h_last, pool_new = mixer_seq(za, B, T, packed['mix'], zeros(CONV_W - 1), jnp.zeros((B, D_RNN), F32),
                                                     zeros(POOL_MAX - 1), 0)
    ck, cv = compress(kv_c.reshape(B, 2 * T, LANES), packed['cmp'])
    y_b = nsa_prompt_pallas(q.reshape(B, T, Q_COLS), ck, cv, kv_s.reshape(B, T, KV_ROW), kv_w.reshape(B, T, KV_ROW),
                            gn.reshape(B, T, GN_COLS), tables)
    out = mixer_tail(x2, za, y_a, y_b.reshape(B * T, Q_COLS), y_c, l, P)
    kv_shape = (B, T, 2, KV_HEADS, HEAD_DIM)
    state = (kv_c.reshape(kv_shape), kv_s.reshape(kv_shape), kv_w.reshape(kv_shape)[:, -min(WINDOW, T):], conv_new,
             h_last[:, 0], pool_new)
    return out.reshape(B, T, D_MODEL), state


def sample_mixer(x, l, P, packed, past_len, conv_state, h0, pool_state, caches, win_buf, page_table, step_tables):
    B, T, _ = x.shape
    x2 = x.reshape(B * T, D_MODEL)
    za, q, kv_c, kv_s, kv_w, gn = projection(x2, P['attn_norm'][l], packed['proj'])
    y_a, y_c, h_new = mixer_step(za, packed['mix'], conv_state, h0, pool_state, past_len)
    cmp_pools, sel_pools, win_bufs = caches
    ck, cv = compress(cmp_pools, packed['cmp'], page_table, layer=l)
    kv_shape = (B, T, 2, KV_HEADS, HEAD_DIM)
    wb = win_buf.shape[1]
    y_b = nsa_step(q, ck, cv, kv_s, kv_w, sel_pools, win_bufs, gn, page_table, step_tables, l)
    win_new = jnp.concatenate([win_buf[:, 1:], kv_w.reshape(kv_shape)], axis=1)
    out = mixer_tail(x2, za, y_a, y_b, y_c, l, P)
    conv_new = jnp.concatenate([conv_state[:, 1:], za[:, None, ZA_XRG:ZA_XRG + D_RNN]], axis=1)
    pool_new = jnp.concatenate([pool_state[:, 1:], za[:, None, ZA_XPOOL:ZA_XPOOL + D_POOL]], axis=1)
    state = (kv_c.reshape(kv_shape), kv_s.reshape(kv_shape), win_new, conv_new, h_new, pool_new)
    return out.reshape(B, T, D_MODEL), state


def ffn_layer(x, l, P, W):
    B, T, _ = x.shape
    i = l // 2
    if l % 2 == 0:
        y = channel_mixer(x.reshape(B * T, D_MODEL), P['ffn_norm'][l], W['ffn_g'][i], W['ffn_u'][i], W['ffn_d'][i])
    else:
        y = channel_mixer(x.reshape(B * T, D_MODEL), P['ffn_norm'][l], W['moe_g'][i], W['moe_u'][i], W['moe_d'][i],
                          router=W['router'][i])
    return y.reshape(B, T, D_MODEL)


def kernel(x_prompt, x_sample, cache_cmp_kv, cache_sel_kv, cache_win_kv, state_conv, state_rg_h, state_pool,
           page_table, attn_norm, w_in, conv_w, conv_b, rg_w_a, rg_b_a, rg_w_x, rg_b_x, rg_lambda, q_norm, k_norm,
           cmp_pe, w_cmp1, w_cmp2, rel_bias, w_pool, pool_scale, w_br_rg, w_br_attn, w_br_pool, w_out, ffn_norm,
           ffn_w_gate, ffn_w_up, ffn_w_down, w_router, b_router, moe_w_gate, moe_w_up, moe_w_down):
    P = dict(attn_norm=attn_norm, conv_w=conv_w, conv_b=conv_b, rg_w_a=rg_w_a, rg_b_a=rg_b_a,
             rg_w_x=rg_w_x, rg_b_x=rg_b_x, rg_lambda=rg_lambda, w_pool=w_pool,
             pool_scale=pool_scale, w_br_rg=w_br_rg, w_br_attn=w_br_attn, w_br_pool=w_br_pool, w_out=w_out,
             ffn_norm=ffn_norm)
    depth = w_in.shape[0]
    n_moe = w_router.shape[0]
    pad_e = LANES - N_EXPERTS
    W = dict(
        ffn_g=[w[None].astype(BF16) for w in ffn_w_gate], ffn_u=[w[None].astype(BF16) for w in ffn_w_up],
        ffn_d=[w[None].astype(BF16) for w in ffn_w_down],
        moe_g=[w.astype(BF16) for w in moe_w_gate], moe_u=[w.astype(BF16) for w in moe_w_up],
        moe_d=[w.astype(BF16) for w in moe_w_down],
        router=[(jnp.pad(w_router[i], ((0, 0), (0, pad_e))).astype(BF16),
                 jnp.pad(b_router[i].astype(F32), (0, pad_e), constant_values=NEG)[None]) for i in range(n_moe)])
    past_len = page_table.shape[1] * PAGE_SIZE
    y_p, y_s = x_prompt, x_sample
    tables = rel_bias_tables(rel_bias, x_prompt.shape[1])
    step_tables = step_bias_tables(rel_bias, past_len, cache_win_kv.shape[2], past_len // CMP_STRIDE)
    positions_minor = (0, 1, 3, 4, 5, 2)
    caches = tuple(jnp.transpose(c, positions_minor) for c in (cache_cmp_kv, cache_sel_kv, cache_win_kv))
    p_list, s_list = [], []
    for l in range(depth):
        packed = dict(proj=pack_projection(w_in[l], q_norm[l], k_norm[l, 1], k_norm[l, 2]),
                      cmp=pack_compress(w_cmp1[l], w_cmp2[l], cmp_pe[l], k_norm[l, 0]),
                      mix=pack_mixer(conv_w[l], conv_b[l], rg_w_a[l], rg_b_a[l], rg_w_x[l], rg_b_x[l], rg_lambda[l],
                                     w_pool[l], pool_scale[l]))
        y_p, st_p = prompt_mixer(y_p, l, P, packed, tables)
        y_p = ffn_layer(y_p, l, P, W)
        p_list.append(st_p)
        y_s, st_s = sample_mixer(y_s, l, P, packed, past_len, state_conv[l], state_rg_h[l], state_pool[l],
                                 caches, cache_win_kv[l], page_table, step_tables)
        y_s = ffn_layer(y_s, l, P, W)
        s_list.append(st_s)
    p_cmp_kv, p_sel_kv, p_win_kv, p_conv, p_h, p_pool = [jnp.stack(a) for a in zip(*p_list)]
    s_cmp_kv, s_sel_kv, s_win_kv, s_conv, s_h, s_pool = [jnp.stack(a) for a in zip(*s_list)]
    return (y_p, y_s, p_cmp_kv, p_sel_kv, p_win_kv, p_conv, p_h, p_pool,
            s_cmp_kv, s_sel_kv, s_win_kv, s_conv, s_h, s_pool)
```

```python
import math
import functools

import jax
import jax.numpy as jnp
import numpy as np
from jax import lax
from jax.experimental import pallas as pl
from jax.experimental.pallas import tpu as pltpu

D_MODEL = 1024
PAGE_SIZE = 128
F32 = jnp.float32
BF16 = jnp.bfloat16
EPS = 1e-6
NEG = -1e30
FORCE = 1e4
D_RNN = 512
RG_BLOCKS = 8
RG_BW = D_RNN // RG_BLOCKS
CONV_W = 4
RG_C = 8.0
N_HEADS = 8
KV_HEADS = 2
HPG = N_HEADS // KV_HEADS
HEAD_DIM = 64
L_CMP = 32
CMP_STRIDE = 16
CMP_HIDDEN = 256
SEL_BLOCK = 64
N_SELECT = 16
WINDOW = 512
Q_BLOCK = 128
D_POOL = 512
POOL_WINDOWS = (2, 4, 8, 16)
POOL_GROUPS = 4
POOL_GW = D_POOL // POOL_GROUPS
POOL_MAX = 16
REL_BUCKETS = 32
REL_MAX_DIST = 1024
N_EXPERTS = 8
TOP_K = 2
KV_ROW = 2 * KV_HEADS * HEAD_DIM
SPLITS = (D_RNN, D_RNN, N_HEADS * HEAD_DIM, KV_ROW, KV_ROW, KV_ROW, 3 * N_HEADS, D_POOL, 3 * D_MODEL)

VMEM_LIMIT_BYTES = 52 * 1024 * 1024
LANES = 128
SUBLANES = 8
M_INIT = -3e38


def _pick(n, cands):
    for c in cands:
        if n % c == 0:
            return c
    return n


def _nt(a, b):
    return lax.dot_general(a, b, (((1,), (1,)), ((), ())), preferred_element_type=F32)


def _rms_rows(x, g):
    return x * lax.rsqrt(jnp.mean(x * x, axis=-1, keepdims=True) + EPS) * g


def rms_norm(x, g):
    xf = x.astype(F32)
    y = xf * lax.rsqrt(jnp.mean(xf * xf, axis=-1, keepdims=True) + EPS)
    return (y * g.astype(F32)).astype(x.dtype)


def rel_bucket(dist):
    n_exact = REL_BUCKETS // 2
    d = jnp.maximum(dist, 0)
    df = jnp.maximum(d, 1).astype(F32)
    large = n_exact + (jnp.log(df / n_exact) / math.log(REL_MAX_DIST / n_exact)
                       * (REL_BUCKETS - n_exact)).astype(jnp.int32)
    return jnp.where(d < n_exact, d, jnp.minimum(large, REL_BUCKETS - 1))


def masked_probs(logits, valid):
    logits = jnp.where(valid, logits, NEG)
    m = jnp.max(logits, axis=-1, keepdims=True)
    p = jnp.where(valid, jnp.exp(logits - m), 0.0)
    return p / jnp.maximum(jnp.sum(p, axis=-1, keepdims=True), 1e-30)


def causal_conv(x, buf, w, b):
    xp = jnp.concatenate([buf.astype(x.dtype), x], axis=1)
    y = lax.conv_general_dilated(xp, w[:, None, :].astype(x.dtype), (1,), 'VALID',
                                 dimension_numbers=('NWC', 'WIO', 'NWC'),
                                 feature_group_count=x.shape[-1]) + b
    return y, xp[:, -(CONV_W - 1):]


def rg_lru(xc, h0, w_a, b_a, w_x, b_x, lam):
    B, T, _ = xc.shape
    xb = xc.reshape(B, T, RG_BLOCKS, RG_BW)
    r = jax.nn.sigmoid((jnp.einsum('btnc,ncd->btnd', xb, w_a).reshape(B, T, D_RNN) + b_a).astype(F32))
    i = jax.nn.sigmoid((jnp.einsum('btnc,ncd->btnd', xb, w_x).reshape(B, T, D_RNN) + b_x).astype(F32))
    log_a = -RG_C * r * jax.nn.softplus(-lam.astype(F32))
    a = jnp.exp(log_a)
    u = jnp.sqrt(-jnp.expm1(2.0 * log_a)) * (i * xc.astype(F32))
    u = u.at[:, 0].add(a[:, 0] * h0.astype(F32))

    def combine(lhs, rhs):
        a1, b1 = lhs
        a2, b2 = rhs
        return a1 * a2, a2 * b1 + b2

    _, h = lax.associative_scan(combine, (a, u), axis=1)
    return h, h[:, -1]


def pool_mix(xin, buf, start_pos, w_pool, scale):
    B, T, C = xin.shape
    xf = jnp.concatenate([buf.astype(xin.dtype), xin], axis=1).astype(F32)
    cs = jnp.concatenate([jnp.zeros((B, 1, C), F32), jnp.cumsum(xf, axis=1)], axis=1)
    pos = start_pos + jnp.arange(T)
    means = []
    for g, w in enumerate(POOL_WINDOWS):
        sl = slice(g * POOL_GW, (g + 1) * POOL_GW)
        s = cs[:, POOL_MAX:POOL_MAX + T, sl] - cs[:, POOL_MAX - w:POOL_MAX - w + T, sl]
        cnt = jnp.minimum(pos + 1, w).astype(F32)[None, :, None]
        means.append(s / cnt)
    mixed = (jnp.concatenate(means, axis=-1) - xf[:, POOL_MAX - 1:]).astype(xin.dtype)
    y = jnp.einsum('btgc,gcd->btgd', mixed.reshape(B, T, POOL_GROUPS, POOL_GW), w_pool)
    return y.reshape(B, T, D_POOL) * scale, xf[:, -(POOL_MAX - 1):].astype(xin.dtype)


def compress_kv(rows, w1, w2, pe, kn):
    Bx, T = rows.shape[:2]
    n_ch = T // CMP_STRIDE
    ch = rows[:, :n_ch * CMP_STRIDE].reshape(Bx, n_ch, CMP_STRIDE, 2, KV_HEADS, HEAD_DIM)
    first = jnp.einsum('bnsegd,esdf->bnegf', ch, w1[:, :CMP_STRIDE])
    second = jnp.einsum('bnsegd,esdf->bnegf', ch, w1[:, CMP_STRIDE:])
    pe_term = jnp.einsum('led,eldf->ef', pe, w1)
    h = first[:, :-1] + second[:, 1:] + pe_term[:, None, :]
    comp = jnp.einsum('bnegf,efd->bnegd', jax.nn.gelu(h), w2)
    nc = comp.shape[1]
    comp_end = jnp.arange(nc) * CMP_STRIDE + L_CMP - 1
    return rms_norm(comp[:, :, 0], kn), comp[:, :, 1], comp_end


def overlap_matrix(n_cmp, n_sel):
    c0 = jnp.arange(n_cmp)[:, None] * CMP_STRIDE
    s0 = jnp.arange(n_sel)[None, :] * SEL_BLOCK
    return ((c0 <= s0 + SEL_BLOCK - 1) & (c0 + L_CMP - 1 >= s0)).astype(F32)


def nsa_attend(q, q_pos, comp_k, comp_v, comp_end, overlap, fetch_sel, win_k, win_v, win_pos, gates, rel_bias):
    Bq, Tq = q.shape[:2]
    scale = HEAD_DIM ** -0.5
    qg = q.reshape(Bq, Tq, KV_HEADS, HPG, HEAD_DIM)
    rb = rel_bias.astype(F32)

    def head_bias(buckets):
        return jnp.moveaxis(rb[buckets], -1, 1).reshape(Tq, KV_HEADS, HPG, -1)

    dist_c = q_pos[:, None] - comp_end[None, :]
    lc = jnp.einsum('btgjd,bcgd->btgjc', qg, comp_k, preferred_element_type=F32) * scale + head_bias(rel_bucket(dist_c))
    pc = masked_probs(lc, (dist_c >= 0)[None, :, None, None, :])
    o_cmp = jnp.einsum('btgjc,bcgd->btgjd', pc, comp_v)
    ns = overlap.shape[1]
    imp = jnp.einsum('btgjc,cn->btgn', pc, overlap)
    blk = jnp.arange(ns)
    cur = (q_pos // SEL_BLOCK)[None, :, None, None]
    forced = (blk == 0) | (blk == cur) | (blk == cur - 1)
    score = jnp.where(blk > cur, -1.0, jnp.where(forced, FORCE, imp))
    n_sel = min(N_SELECT, ns)
    _, idx = lax.top_k(score, n_sel)
    kv_s = fetch_sel(idx)
    k_s = kv_s[..., 0, :].reshape(Bq, Tq, KV_HEADS, n_sel * SEL_BLOCK, HEAD_DIM)
    v_s = kv_s[..., 1, :].reshape(Bq, Tq, KV_HEADS, n_sel * SEL_BLOCK, HEAD_DIM)
    pos_s = (idx[..., None] * SEL_BLOCK + jnp.arange(SEL_BLOCK)).reshape(Bq, Tq, KV_HEADS, -1)
    dist_s = q_pos[None, :, None, None] - pos_s
    bias_s = rb.reshape(REL_BUCKETS, KV_HEADS, HPG)[rel_bucket(dist_s), jnp.arange(KV_HEADS)[:, None]]
    ls = jnp.einsum('btgjd,btgkd->btgjk', qg, k_s, preferred_element_type=F32) * scale + jnp.moveaxis(bias_s, -1, 3)
    ps = masked_probs(ls, (dist_s >= 0)[:, :, :, None, :])
    o_sel = jnp.einsum('btgjk,btgkd->btgjd', ps, v_s)
    dist_w = q_pos[:, None] - win_pos[None, :]
    lw = jnp.einsum('btgjd,bwgd->btgjw', qg, win_k, preferred_element_type=F32) * scale + head_bias(rel_bucket(dist_w))
    valid_w = (dist_w >= 0) & (dist_w <= WINDOW) & (win_pos[None, :] >= 0)
    pw = masked_probs(lw, valid_w[None, :, None, None, :])
    o_win = jnp.einsum('btgjw,bwgd->btgjd', pw, win_v)
    g = jax.nn.sigmoid(gates.astype(F32)).reshape(Bq, Tq, KV_HEADS, HPG, 3)
    o = g[..., 0:1] * o_cmp + g[..., 1:2] * o_sel + g[..., 2:3] * o_win
    return o.reshape(Bq, Tq, N_HEADS * HEAD_DIM).astype(q.dtype)


def nsa_sample(q, comp_k, comp_v, kv_sel_new, kv_win_new, gates, *, sel_pool, win_buf, page_table, rel_bias):
    DB, DS = q.shape[:2]
    n_pages = page_table.shape[1]
    past = n_pages * PAGE_SIZE
    q_pos = past + jnp.arange(DS)
    comp_end = jnp.arange(comp_k.shape[1]) * CMP_STRIDE + L_CMP - 1
    total = past + DS
    ns = -(-total // SEL_BLOCK)
    nbp = past // SEL_BLOCK
    nnb = ns - nbp
    bpp = PAGE_SIZE // SEL_BLOCK
    ovl = overlap_matrix(comp_k.shape[1], ns)
    pool_blocks = sel_pool.reshape(-1, SEL_BLOCK, 2, KV_HEADS, HEAD_DIM)
    new_pad = jnp.pad(kv_sel_new.astype(sel_pool.dtype), ((0, 0), (0, nnb * SEL_BLOCK - DS), (0, 0), (0, 0), (0, 0)))
    new_blocks = new_pad.reshape(DB, nnb, SEL_BLOCK, 2, KV_HEADS, HEAD_DIM)
    b_idx = jnp.arange(DB)[:, None, None, None]
    g_idx = jnp.arange(KV_HEADS)[None, None, :, None]

    def fetch(idx):
        ip = jnp.minimum(idx, nbp - 1)
        phys = page_table[b_idx, ip // bpp] * bpp + ip % bpp
        from_past = pool_blocks[phys, :, :, g_idx]
        from_new = new_blocks[b_idx, jnp.clip(idx - nbp, 0, nnb - 1), :, :, g_idx]
        return jnp.where((idx >= nbp)[..., None, None, None], from_new, from_past)

    wb = win_buf.shape[1]
    win = jnp.concatenate([win_buf, kv_win_new.astype(win_buf.dtype)], axis=1)
    win_pos = past - wb + jnp.arange(wb + DS)
    out = nsa_attend(q, q_pos, comp_k, comp_v, comp_end, ovl, fetch, win[:, :, 0], win[:, :, 1], win_pos, gates, rel_bias)
    return out, win[:, -wb:]


PROJ_TN = 512
ZA_COLS = 3 * D_MODEL + 2 * D_RNN + D_POOL
N_ZA = ZA_COLS // PROJ_TN
Q_COLS = N_HEADS * HEAD_DIM
GN_COLS = 2 * LANES
PROJ_COLS = ZA_COLS + Q_COLS + 3 * KV_ROW + GN_COLS
ZA_XRG, ZA_GRG, ZA_XPOOL = 3 * D_MODEL, 3 * D_MODEL + D_RNN, 3 * D_MODEL + 2 * D_RNN


def _proj_kernel(x_ref, g_ref, w_ref, seg_ref, ng_ref, nm_ref, za_ref, q_ref, kvc_ref, kvs_ref, kvw_ref, gn_ref, xn_ref):
    j = pl.program_id(1)

    @pl.when(j == 0)
    def _():
        xn_ref[...] = _rms_rows(x_ref[...], g_ref[...]).astype(BF16)

    acc = jnp.dot(xn_ref[...], w_ref[...], preferred_element_type=F32)

    @pl.when(j < N_ZA)
    def _():
        za_ref[...] = acc

    @pl.when(j >= N_ZA)
    def _():
        sq = acc * acc
        hi = sq.astype(BF16)
        lo = (sq - hi.astype(F32)).astype(BF16)
        ss = (jnp.dot(hi, seg_ref[...], preferred_element_type=F32)
              + jnp.dot(lo, seg_ref[...], preferred_element_type=F32))
        normed = acc * lax.rsqrt(ss * (1.0 / HEAD_DIM) + EPS) * ng_ref[...]
        y = jnp.where(nm_ref[...] > 0.5, normed, acc)

        @pl.when(j == N_ZA)
        def _():
            q_ref[...] = y.astype(BF16)

        @pl.when(j == N_ZA + 1)
        def _():
            kvc_ref[...] = y[:, :KV_ROW]
            kvs_ref[...] = y[:, KV_ROW:]

        @pl.when(j == N_ZA + 2)
        def _():
            kvw_ref[...] = y[:, :KV_ROW]
            gn_ref[...] = y[:, KV_ROW:]


def pack_projection(w_in, q_gain, ks_gain, kw_gain):
    cut = np.cumsum(SPLITS)[:-1].tolist()
    x_rg, g_rg, q, kv_c, kv_s, kv_w, g_nsa, x_pool, g_br = jnp.split(w_in, cut, axis=1)
    per_group = 3 * HPG
    gn = jnp.zeros((w_in.shape[0], GN_COLS), w_in.dtype)
    for g in range(KV_HEADS):
        gn = gn.at[:, g * LANES:g * LANES + per_group].set(g_nsa[:, g * per_group:(g + 1) * per_group])
    w = jnp.concatenate([g_br, x_rg, g_rg, x_pool, q, kv_c, kv_s, kv_w, gn], axis=1).astype(BF16)
    ones_v = jnp.ones((KV_HEADS * HEAD_DIM,), F32)
    zeros_v = jnp.zeros((KV_HEADS * HEAD_DIM,), F32)
    gain = jnp.concatenate([jnp.ones((ZA_COLS,), F32), jnp.tile(q_gain.astype(F32), N_HEADS) * HEAD_DIM ** -0.5,
                            jnp.ones((KV_ROW,), F32),
                            jnp.tile(ks_gain.astype(F32), KV_HEADS), ones_v,
                            jnp.tile(kw_gain.astype(F32), KV_HEADS), ones_v,
                            jnp.ones((GN_COLS,), F32)])
    mask = jnp.concatenate([jnp.zeros((ZA_COLS,), F32), jnp.ones((Q_COLS,), F32), jnp.zeros((KV_ROW,), F32),
                            ones_v, zeros_v, ones_v, zeros_v, jnp.zeros((GN_COLS,), F32)])
    return w, gain[None], mask[None]


def projection(x, norm_gain, packed):
    w, gain, mask = packed
    T = x.shape[0]
    tm = _pick(T, (1024, 512, 256, 128))
    seg = jnp.asarray((np.arange(PROJ_TN)[:, None] // HEAD_DIM == np.arange(PROJ_TN)[None, :] // HEAD_DIM)
                      .astype(np.float32), BF16)
    row = lambda i, j: (i, 0)
    return pl.pallas_call(
        _proj_kernel,
        grid=(T // tm, PROJ_COLS // PROJ_TN),
        in_specs=[pl.BlockSpec((tm, D_MODEL), row),
                  pl.BlockSpec((1, D_MODEL), lambda i, j: (0, 0)),
                  pl.BlockSpec((D_MODEL, PROJ_TN), lambda i, j: (0, j)),
                  pl.BlockSpec((PROJ_TN, PROJ_TN), lambda i, j: (0, 0)),
                  pl.BlockSpec((1, PROJ_TN), lambda i, j: (0, j)),
                  pl.BlockSpec((1, PROJ_TN), lambda i, j: (0, j))],
        out_specs=[pl.BlockSpec((tm, PROJ_TN), lambda i, j: (i, jnp.minimum(j, N_ZA - 1))),
                   pl.BlockSpec((tm, Q_COLS), row),
                   pl.BlockSpec((tm, KV_ROW), row), pl.BlockSpec((tm, KV_ROW), row), pl.BlockSpec((tm, KV_ROW), row),
                   pl.BlockSpec((tm, GN_COLS), row)],
        out_shape=[jax.ShapeDtypeStruct((T, ZA_COLS), F32), jax.ShapeDtypeStruct((T, Q_COLS), BF16),
                   jax.ShapeDtypeStruct((T, KV_ROW), F32), jax.ShapeDtypeStruct((T, KV_ROW), F32),
                   jax.ShapeDtypeStruct((T, KV_ROW), F32), jax.ShapeDtypeStruct((T, GN_COLS), F32)],
        scratch_shapes=[pltpu.VMEM((tm, D_MODEL), BF16)],
        compiler_params=pltpu.CompilerParams(dimension_semantics=("parallel", "arbitrary"),
                                             vmem_limit_bytes=VMEM_LIMIT_BYTES),
        name="projection",
    )(x, norm_gain.astype(F32)[None], w, seg, gain, mask)


CMP_PAIRS = CMP_STRIDE // 2
CMP_GW = KV_HEADS * CMP_HIDDEN


def _compress_kernel(*refs, n_pages, paged):
    if paged:
        pages = refs[1:1 + n_pages]
        eye_ref, w1_ref, pe_ref, w2_ref, kn_ref, ck_ref, cv_ref, half_ref = refs[1 + n_pages:]
        page = pages[0].shape[-1]
        m = n_pages * page // CMP_STRIDE
        for k, pg in enumerate(pages):
            for e in range(2):
                slab = pg[0, 0, e].reshape(KV_HEADS * HEAD_DIM, page).astype(BF16)
                half_ref[e, k * page:(k + 1) * page, :] = _nt(eye_ref[...], slab)
    else:
        rows_ref, w1_ref, pe_ref, w2_ref, kn_ref, ck_ref, cv_ref = refs
        m = rows_ref.shape[1] // (2 * CMP_STRIDE)
    for e, out_ref in enumerate((ck_ref, cv_ref)):
        acc = jnp.zeros((m, 2 * CMP_GW), F32)
        for p in range(CMP_PAIRS):
            def rows(s):
                if paged:
                    return half_ref[e, pl.ds(s, m, stride=CMP_STRIDE), :]
                return rows_ref[0, pl.ds(2 * s + e, m, stride=2 * CMP_STRIDE), :]
            a = jnp.concatenate([rows(2 * p), rows(2 * p + 1)], axis=1).astype(BF16)
            acc = acc + jnp.dot(a, w1_ref[e, p], preferred_element_type=F32)
        h = acc[:, :CMP_GW] + pltpu.roll(acc[:, CMP_GW:], m - 1, axis=0) + pe_ref[e]
        gl = jax.nn.gelu(h).astype(BF16)
        for g in range(KV_HEADS):
            c = jnp.dot(gl, w2_ref[e, g], preferred_element_type=F32)
            if e == 0:
                c = c * lax.rsqrt(jnp.sum(c * c, axis=1, keepdims=True) * (1.0 / HEAD_DIM) + EPS) * kn_ref[...]
            out_ref[0, g] = c.astype(out_ref.dtype)


def pack_compress(w1, w2, pe, kn):
    halves = w1.reshape(2, 2, CMP_STRIDE, HEAD_DIM, CMP_HIDDEN)
    eye = jnp.eye(KV_HEADS, dtype=w1.dtype)
    bd = jnp.einsum('ehsdf,gk->esgdhkf', halves, eye)
    w1p = bd.reshape(2, CMP_PAIRS, 2 * KV_HEADS * HEAD_DIM, 2 * CMP_GW).astype(BF16)
    pe_term = jnp.einsum('led,eldf->ef', pe, w1)
    pe_t = jnp.tile(pe_term, (1, KV_HEADS))[:, None, :].astype(F32)
    w2p = jnp.zeros((2, KV_HEADS, CMP_GW, LANES), w2.dtype)
    for g in range(KV_HEADS):
        w2p = w2p.at[:, g, g * CMP_HIDDEN:(g + 1) * CMP_HIDDEN, :HEAD_DIM].set(w2)
    knp = jnp.concatenate([kn.astype(F32), jnp.zeros((LANES - HEAD_DIM,), F32)])[None]
    return w1p, pe_t, w2p.astype(BF16), knp


def compress(rows, packed, page_table=None, layer=0):
    w1p, pe_t, w2p, knp = packed
    paged = page_table is not None
    if not paged:
        bx, n_pages = rows.shape[0], 1
        m = rows.shape[1] // (2 * CMP_STRIDE)
        data_specs = [pl.BlockSpec((1,) + rows.shape[1:], lambda b: (b, 0, 0))]
        const = lambda nd: (lambda b: (0,) * nd)
        out_map = lambda b: (b, 0, 0, 0)
        data, scratch = [rows], []
    else:
        bx, n_pages = page_table.shape
        page = rows.shape[-1]
        assert page == LANES and KV_HEADS * HEAD_DIM == LANES
        m = n_pages * page // CMP_STRIDE
        eye = jnp.asarray(np.eye(LANES, dtype=np.float32), BF16)
        const = lambda nd: (lambda b, pt: (0,) * nd)
        data_specs = [pl.BlockSpec((1, 1) + rows.shape[2:], lambda b, pt, k=k: (layer, pt[b, k], 0, 0, 0, 0))
                      for k in range(n_pages)] + [pl.BlockSpec(eye.shape, const(2))]
        out_map = lambda b, pt: (b, 0, 0, 0)
        data, scratch = [page_table] + [rows] * n_pages + [eye], [pltpu.VMEM((2, n_pages * page, LANES), F32)]
    in_specs = data_specs + [pl.BlockSpec(w1p.shape, const(4)), pl.BlockSpec(pe_t.shape, const(3)),
                             pl.BlockSpec(w2p.shape, const(4)), pl.BlockSpec(knp.shape, const(2))]
    out_spec = pl.BlockSpec((1, KV_HEADS, m, LANES), out_map)
    out_shape = jax.ShapeDtypeStruct((bx, KV_HEADS, m, LANES), BF16)
    grid_spec = pltpu.PrefetchScalarGridSpec(num_scalar_prefetch=int(paged), grid=(bx,), in_specs=in_specs,
                                             out_specs=[out_spec, out_spec], scratch_shapes=scratch)
    args = data + [w1p, pe_t, w2p, knp]
    return pl.pallas_call(
        functools.partial(_compress_kernel, n_pages=n_pages, paged=paged),
        grid_spec=grid_spec,
        out_shape=[out_shape, out_shape],
        compiler_params=pltpu.CompilerParams(dimension_semantics=("parallel",), vmem_limit_bytes=VMEM_LIMIT_BYTES),
        name="compress",
    )(*args)


MIX_TT = 512
CONV_HALO = SUBLANES
SCAN_UNROLL = 4


def _log1p(y):
    u = 1.0 + y
    return jnp.where(u == 1.0, y, jnp.log(u) * (y / jnp.where(u == 1.0, 1.0, u - 1.0)))


def _neg_expm1(x):
    t = jnp.tanh(0.5 * x)
    return -2.0 * t / (1.0 - t)


def _softplus(x):
    return jnp.maximum(x, 0.0) + _log1p(jnp.exp(-jnp.abs(x)))


def _rglru_coeffs(xc, wa_ref, ba_ref, wx_ref, bx_ref, lam_ref):
    xb = xc.astype(BF16)
    r = jax.nn.sigmoid(jnp.dot(xb, wa_ref[...], preferred_element_type=F32) + ba_ref[...])
    i = jax.nn.sigmoid(jnp.dot(xb, wx_ref[...], preferred_element_type=F32) + bx_ref[...])
    log_a = -RG_C * r * _softplus(-lam_ref[...])
    return jnp.exp(log_a), jnp.sqrt(_neg_expm1(2.0 * log_a)) * (i * xc)


def _pool_project(sums_minus, wp_ref, scale_ref):
    return jnp.dot(sums_minus.astype(BF16), wp_ref[...], preferred_element_type=F32) * scale_ref[...]


def _mixer_seq_kernel(xrg_ref, grg_ref, xpool_ref, conv0_ref, h0_ref, pool0_ref, cw_ref, cb_ref, wa_ref, ba_ref,
                      wx_ref, bx_ref, lam_ref, wp_ref, ps_ref, ya_ref, yc_ref, convn_ref, hn_ref, pooln_ref,
                      xe_ref, pe_ref, a_ref, u_ref, h_ref, carry_ref, *, start_pos):
    i = pl.program_id(1)
    tt = xrg_ref.shape[0]

    @pl.when(i == 0)
    def _():
        xe_ref[0:CONV_HALO, :] = conv0_ref[0]
        pe_ref[0:POOL_MAX, :] = pool0_ref[0]
        carry_ref[...] = h0_ref[0]

    xe_ref[CONV_HALO:CONV_HALO + tt, :] = xrg_ref[...]
    xc = cb_ref[...] + sum(cw_ref[k:k + 1, :] * xe_ref[CONV_HALO - (CONV_W - 1) + k:CONV_HALO - (CONV_W - 1) + k + tt, :]
                           for k in range(CONV_W))
    a, u = _rglru_coeffs(xc, wa_ref, ba_ref, wx_ref, bx_ref, lam_ref)
    a_ref[...] = a
    u_ref[...] = u

    row = lax.broadcasted_iota(jnp.int32, (SUBLANES, D_RNN), 0)

    def block(j, carry):
        r0 = pl.multiple_of(j * SUBLANES, SUBLANES)
        ab = a_ref[pl.ds(r0, SUBLANES), :]
        ub = u_ref[pl.ds(r0, SUBLANES), :]
        for d in (1, 2, 4):
            a_sh = jnp.where(row >= d, pltpu.roll(ab, d, axis=0), 1.0)
            u_sh = jnp.where(row >= d, pltpu.roll(ub, d, axis=0), 0.0)
            ub = ab * u_sh + ub
            ab = ab * a_sh
        hb = ab * carry + ub
        h_ref[pl.ds(r0, SUBLANES), :] = hb
        return jnp.broadcast_to(hb[SUBLANES - 1:SUBLANES, :], (SUBLANES, D_RNN))

    carry = lax.fori_loop(0, tt // SUBLANES, block, carry_ref[...], unroll=SCAN_UNROLL)
    carry_ref[...] = carry
    ya_ref[...] = (h_ref[...] * jax.nn.gelu(grg_ref[...])).astype(ya_ref.dtype)

    pe_ref[POOL_MAX:POOL_MAX + tt, :] = xpool_ref[...]
    pos = start_pos + i * tt + lax.broadcasted_iota(jnp.int32, (tt, 1), 0)
    parts = []
    for g, w in enumerate(POOL_WINDOWS):
        lanes = slice(g * POOL_GW, (g + 1) * POOL_GW)
        s = sum(pe_ref[POOL_MAX - k:POOL_MAX - k + tt, lanes] for k in range(w))
        cnt = jnp.minimum(pos + 1, w).astype(F32)
        parts.append(s / cnt - pe_ref[POOL_MAX:POOL_MAX + tt, lanes])
    yc_ref[...] = _pool_project(jnp.concatenate(parts, axis=1), wp_ref, ps_ref).astype(yc_ref.dtype)

    @pl.when(i == pl.num_programs(1) - 1)
    def _():
        convn_ref[0] = xe_ref[CONV_HALO + tt - (CONV_W - 1):CONV_HALO + tt, :]
        hn_ref[0] = carry[0:1, :]
        pooln_ref[0] = pe_ref[POOL_MAX + tt - (POOL_MAX - 1):POOL_MAX + tt, :]

    xe_ref[0:CONV_HALO, :] = xe_ref[tt:tt + CONV_HALO, :]
    pe_ref[0:POOL_MAX, :] = pe_ref[tt:tt + POOL_MAX, :]


def pack_mixer(conv_w, conv_b, w_a, b_a, w_x, b_x, lam, w_pool, scale):
    def block_diag(w):
        n, c, d = w.shape
        return jnp.einsum('ncd,nm->ncmd', w, jnp.eye(n, dtype=w.dtype)).reshape(n * c, n * d).astype(BF16)

    row = lambda v: v.astype(F32)[None]
    return (conv_w.astype(F32), row(conv_b), block_diag(w_a), row(b_a), block_diag(w_x), row(b_x), row(lam),
            block_diag(w_pool), row(scale))


def mixer_seq(za, batch, seq, packed, conv0, h0, pool0, start_pos):
    tt = min(MIX_TT, seq)
    nt = seq // tt
    conv_pad = jnp.pad(conv0.astype(F32), ((0, 0), (CONV_HALO - (CONV_W - 1), 0), (0, 0)))
    pool_pad = jnp.pad(pool0.astype(F32), ((0, 0), (1, 0), (0, 0)))
    h_pad = jnp.broadcast_to(h0.astype(F32)[:, None, :], (batch, SUBLANES, D_RNN))
    col = lambda c: pl.BlockSpec((tt, D_RNN), lambda b, i: (b * nt + i, c))
    state = lambda rows: pl.BlockSpec((1, rows, D_RNN), lambda b, i: (b, 0, 0))
    full = lambda a: pl.BlockSpec(a.shape, lambda b, i: (0,) * a.ndim)
    out_rows = pl.BlockSpec((tt, D_RNN), lambda b, i: (b * nt + i, 0))
    return pl.pallas_call(
        functools.partial(_mixer_seq_kernel, start_pos=start_pos),
        grid=(batch, nt),
        in_specs=[col(ZA_XRG // D_RNN), col(ZA_GRG // D_RNN), col(ZA_XPOOL // D_RNN), state(CONV_HALO), state(SUBLANES),
                  state(POOL_MAX)] + [full(a) for a in packed],
        out_specs=[out_rows, out_rows, state(CONV_W - 1), state(1), state(POOL_MAX - 1)],
        out_shape=[jax.ShapeDtypeStruct((batch * seq, D_RNN), BF16), jax.ShapeDtypeStruct((batch * seq, D_POOL), BF16),
                   jax.ShapeDtypeStruct((batch, CONV_W - 1, D_RNN), F32), jax.ShapeDtypeStruct((batch, 1, D_RNN), F32),
                   jax.ShapeDtypeStruct((batch, POOL_MAX - 1, D_POOL), F32)],
        scratch_shapes=[pltpu.VMEM((CONV_HALO + tt, D_RNN), F32), pltpu.VMEM((POOL_MAX + tt, D_POOL), F32),
                        pltpu.VMEM((tt, D_RNN), F32), pltpu.VMEM((tt, D_RNN), F32), pltpu.VMEM((tt, D_RNN), F32),
                        pltpu.VMEM((SUBLANES, D_RNN), F32)],
        compiler_params=pltpu.CompilerParams(dimension_semantics=("parallel", "arbitrary"),
                                             vmem_limit_bytes=VMEM_LIMIT_BYTES),
        name="mixer_seq",
    )(za, za, za, conv_pad, h_pad, pool_pad, *packed)


def _mixer_step_kernel(xrg_ref, grg_ref, xpool_ref, conv_ref, h0_ref, pool_ref, cw_ref, cb_ref, wa_ref, ba_ref,
                       wx_ref, bx_ref, lam_ref, wp_ref, ps_ref, ya_ref, yc_ref, hn_ref, *, start_pos):
    x = xrg_ref[...]
    xc = cb_ref[...] + cw_ref[CONV_W - 1:CONV_W, :] * x + sum(cw_ref[k:k + 1, :] * conv_ref[k] for k in range(CONV_W - 1))
    a, u = _rglru_coeffs(xc, wa_ref, ba_ref, wx_ref, bx_ref, lam_ref)
    h = a * h0_ref[...] + u
    hn_ref[...] = h
    ya_ref[...] = (h * jax.nn.gelu(grg_ref[...])).astype(ya_ref.dtype)
    xp = xpool_ref[...]
    parts = []
    for g, w in enumerate(POOL_WINDOWS):
        lanes = slice(g * POOL_GW, (g + 1) * POOL_GW)
        s = xp[:, lanes] + sum(pool_ref[POOL_MAX - 1 - k][:, lanes] for k in range(1, w))
        parts.append(s / float(min(start_pos + 1, w)) - xp[:, lanes])
    yc_ref[...] = _pool_project(jnp.concatenate(parts, axis=1), wp_ref, ps_ref).astype(yc_ref.dtype)


def mixer_step(za, packed, conv_state, h0, pool_state, start_pos):
    batch = za.shape[0]
    conv_t = jnp.swapaxes(conv_state.astype(F32), 0, 1)
    pool_t = jnp.swapaxes(pool_state.astype(F32), 0, 1)
    col = lambda c: pl.BlockSpec((batch, D_RNN), lambda i: (0, c))
    full = lambda a: pl.BlockSpec(a.shape, lambda i: (0,) * a.ndim)
    rows = pl.BlockSpec((batch, D_RNN), lambda i: (0, 0))
    return pl.pallas_call(
        functools.partial(_mixer_step_kernel, start_pos=start_pos),
        grid=(1,),
        in_specs=[col(ZA_XRG // D_RNN), col(ZA_GRG // D_RNN), col(ZA_XPOOL // D_RNN), full(conv_t), rows, full(pool_t)]
        + [full(a) for a in packed],
        out_specs=[rows, rows, rows],
        out_shape=[jax.ShapeDtypeStruct((batch, D_RNN), BF16), jax.ShapeDtypeStruct((batch, D_POOL), BF16),
                   jax.ShapeDtypeStruct((batch, D_RNN), F32)],
        compiler_params=pltpu.CompilerParams(vmem_limit_bytes=VMEM_LIMIT_BYTES),
        name="mixer_step",
    )(za, za, za, conv_t, h0.astype(F32), pool_t, *packed)


QB = Q_BLOCK
ROWS = HPG * QB
N_WIN_TILES = WINDOW // QB + 1
N_SEL_BIAS = REL_MAX_DIST // QB + 2
SEL_SPAN = 4
KV_CHUNK = 512


def _nsa_prompt_kernel(q_ref, ck_ref, cv_ref, kvs_ref, kvw_ref, gate_ref, bc_ref, tbs_ref, tbw_ref,
                       ovl_ref, eye_ref, pq_ref, pk_ref, pv_ref, onehot_ref, o_ref,
                       acc_ref, m_ref, qa_ref, comb_ref, ks_ref, vs_ref, kw_ref, vw_ref, cnt_ref, *, n_cmp, n_blk):
    qb = pl.program_id(2)
    t0 = qb * QB
    ncp = ck_ref.shape[2]
    seq = kvs_ref.shape[1]
    lane_row = lax.broadcasted_iota(jnp.int32, (1, LANES), 1)

    @pl.when(qb == 0)
    def _():
        ones_hi = jnp.where(lane_row >= HEAD_DIM, 1.0, 0.0)
        kw_ref[0:WINDOW, :] = jnp.broadcast_to(jnp.where(lane_row == HEAD_DIM, 1.0, 0.0), (WINDOW, LANES)).astype(BF16)
        vw_ref[0:WINDOW, :] = jnp.zeros((WINDOW, LANES), BF16)

        def stage(c, carry):
            r = pl.multiple_of(c * KV_CHUNK, KV_CHUNK)
            sel = kvs_ref[0, pl.ds(r, KV_CHUNK), :].astype(BF16)
            win = kvw_ref[0, pl.ds(r, KV_CHUNK), :].astype(BF16)
            ks_ref[pl.ds(r, KV_CHUNK), :] = (jnp.dot(sel, pk_ref[0], preferred_element_type=F32)
                                             + onehot_ref[pl.ds(r, KV_CHUNK), :].astype(F32)).astype(BF16)
            vs_ref[pl.ds(r, KV_CHUNK), :] = (jnp.dot(sel, pv_ref[0], preferred_element_type=F32) + ones_hi).astype(BF16)
            kw_ref[pl.ds(WINDOW + r, KV_CHUNK), :] = jnp.dot(win, pk_ref[0], preferred_element_type=F32).astype(BF16)
            vw_ref[pl.ds(WINDOW + r, KV_CHUNK), :] = (jnp.dot(win, pv_ref[0], preferred_element_type=F32)
                                                      + ones_hi).astype(BF16)
            return carry

        lax.fori_loop(0, seq // KV_CHUNK, stage, 0)

    q4 = jnp.dot(q_ref[0], pq_ref[...], preferred_element_type=F32)
    q3 = jnp.concatenate([q4[:, j * LANES:(j + 1) * LANES] for j in range(HPG)], axis=0)
    q = q3.astype(BF16)
    sig = jax.nn.sigmoid(gate_ref[0])
    gates = [jnp.concatenate([sig[:, 3 * j + c:3 * j + c + 1] for j in range(HPG)], axis=0) for c in range(3)]

    lc = _nt(q, ck_ref[0, 0]) + bc_ref[0].reshape(ROWS, ncp)
    tok = t0 + lax.broadcasted_iota(jnp.int32, (HPG, QB, ncp), 1).reshape(ROWS, ncp)
    col = lax.broadcasted_iota(jnp.int32, (ROWS, ncp), 1)
    valid = (tok >= col * CMP_STRIDE + (L_CMP - 1)) & (col < n_cmp)
    lc = jnp.where(valid, lc, NEG)
    mx = jnp.max(lc, axis=1, keepdims=True)
    p = jnp.where(valid, jnp.exp(lc - mx), 0.0)
    pc = p / jnp.maximum(jnp.sum(p, axis=1, keepdims=True), 1e-30)
    comb_ref[...] = gates[0] * jnp.dot(pc.astype(BF16), cv_ref[0, 0], preferred_element_type=F32)

    pcs = pc[0:QB] + pc[QB:2 * QB] + pc[2 * QB:3 * QB] + pc[3 * QB:4 * QB]
    hi = pcs.astype(BF16)
    lo = (pcs - hi.astype(F32)).astype(BF16)
    imp = _nt(ovl_ref[...], hi) + _nt(ovl_ref[...], lo)
    blk = lax.broadcasted_iota(jnp.int32, (n_blk, QB), 0)
    cur = (t0 + lax.broadcasted_iota(jnp.int32, (n_blk, QB), 1)) // SEL_BLOCK
    forced = (blk == 0) | (blk == cur) | (blk == cur - 1)
    score = jnp.where(blk > cur, -1.0, jnp.where(forced, FORCE, imp))
    chunks = [score[r:r + SUBLANES] for r in range(0, n_blk, SUBLANES)]
    sub = lax.broadcasted_iota(jnp.int32, (SUBLANES, QB), 0)
    cnt_ref[...] = jnp.zeros((n_blk, QB), F32)
    last_blk = (t0 + QB - 1) // SEL_BLOCK
    for mc in range(0, n_blk, SUBLANES):
        @pl.when(mc <= last_blk)
        def _(mc=mc):
            for r, ch in enumerate(chunks):
                first = r * SUBLANES
                part = jnp.zeros((SUBLANES, QB), F32)
                for m in range(mc, mc + SUBLANES):
                    row = jnp.broadcast_to(score[m:m + 1, :], (SUBLANES, QB))
                    if first > m:
                        beats = jnp.where(row >= ch, 1.0, 0.0)
                    elif first + SUBLANES - 1 < m:
                        beats = jnp.where(row > ch, 1.0, 0.0)
                    else:
                        beats = jnp.where(sub + first > m, jnp.where(row >= ch, 1.0, 0.0), jnp.where(row > ch, 1.0, 0.0))
                    part = part + beats
                cnt_ref[first:first + SUBLANES, :] += part
    sel_neg = jnp.where(cnt_ref[...] < float(min(N_SELECT, n_blk)), 0.0, NEG)
    pieces = [jnp.zeros((HEAD_DIM, QB), F32), sel_neg]
    if n_blk < HEAD_DIM:
        pieces.append(jnp.zeros((HEAD_DIM - n_blk, QB), F32))
    placed_t = jnp.concatenate(pieces, axis=0).astype(BF16)
    placed = _nt(eye_ref[...], placed_t)
    qa_ref[...] = (q3 + jnp.concatenate([placed] * HPG, axis=0)).astype(BF16)

    m_ref[...] = jnp.full((ROWS, LANES), M_INIT, F32)
    acc_ref[...] = jnp.zeros((ROWS, LANES), F32)
    n_bias = tbs_ref.shape[1]

    span = SEL_SPAN * QB
    n_spans = qb // SEL_SPAN + 1

    def body(kk, carry):
        off = pl.multiple_of(kk * span, span)
        s = _nt(qa_ref[...], ks_ref[pl.ds(off, span), :])
        parts = []
        for u in range(SEL_SPAN):
            idx = jnp.clip(qb - (kk * SEL_SPAN + u), -1, n_bias - 2) + 1
            parts.append(s[:, u * QB:(u + 1) * QB] + tbs_ref[:, pl.ds(idx, 1)].reshape(ROWS, LANES))
        tile_max = functools.reduce(jnp.maximum, parts)
        m_old = m_ref[...]
        m_new = jnp.maximum(m_old, jnp.max(tile_max, axis=1, keepdims=True))
        alpha = jnp.exp(m_old - m_new)
        pr = jnp.concatenate([jnp.exp(x - m_new).astype(BF16) for x in parts], axis=1)
        acc_ref[...] = alpha * acc_ref[...] + jnp.dot(pr, vs_ref[pl.ds(off, SEL_SPAN * QB), :],
                                                      preferred_element_type=F32)
        m_ref[...] = m_new
        return carry

    lax.fori_loop(0, n_spans, body, 0)
    acc = acc_ref[...]
    comb_ref[...] += gates[1] * (acc / pltpu.roll(acc, HEAD_DIM, axis=1))

    qw = (q3 + jnp.where(lane_row == HEAD_DIM, NEG, 0.0)).astype(BF16)
    w_off = pl.multiple_of(t0, QB)
    sw = _nt(qw, kw_ref[pl.ds(w_off, WINDOW + QB), :]) + tbw_ref[...].reshape(ROWS, WINDOW + QB)
    pw = jnp.exp(sw - jnp.max(sw, axis=1, keepdims=True))
    accw = jnp.dot(pw.astype(BF16), vw_ref[pl.ds(w_off, WINDOW + QB), :], preferred_element_type=F32)
    comb_ref[...] += gates[2] * (accw / pltpu.roll(accw, HEAD_DIM, axis=1))

    comb = comb_ref[...]
    lane = lax.broadcasted_iota(jnp.int32, (QB, LANES), 1)
    for half in range(HPG // 2):
        a = comb[(2 * half) * QB:(2 * half + 1) * QB]
        b = comb[(2 * half + 1) * QB:(2 * half + 2) * QB]
        o_ref[0, :, half * LANES:(half + 1) * LANES] = jnp.where(lane < HEAD_DIM, a,
                                                                 pltpu.roll(b, HEAD_DIM, axis=1)).astype(o_ref.dtype)


def _bias_by_distance(rel_bias):
    max_d = (N_SEL_BIAS + 1) * QB
    return jnp.transpose(rel_bias.astype(F32)[rel_bucket(jnp.arange(max_d))])


def _ext(tab_t, lo, hi, ok_lo=None, ok_hi=None):
    n_heads, depth = tab_t.shape
    y = np.arange(lo, hi)
    parts = [jnp.broadcast_to(tab_t[:, :1], (n_heads, int(np.sum(y < 0)))), tab_t[:, max(lo, 0):max(min(hi, depth), 0)],
             jnp.broadcast_to(tab_t[:, -1:], (n_heads, int(np.sum(y > depth - 1))))]
    arr = jnp.concatenate(parts, axis=1)
    ok = np.ones(y.shape, bool)
    if ok_lo is not None:
        ok &= y >= ok_lo
    if ok_hi is not None:
        ok &= y <= ok_hi
    return jnp.where(jnp.asarray(ok)[None], arr, NEG)


def _toeplitz_tiles(ext, lo, ks):
    w = jnp.stack([ext[:, QB * k - (QB - 1) - lo:QB * k - (QB - 1) - lo + 2 * QB] for k in ks], axis=1)
    skew = jnp.tile(w, (1, 1, QB + 1))[..., :QB * (2 * QB + 1)].reshape(w.shape[:2] + (QB, 2 * QB + 1))[..., :QB]
    return skew[..., ::-1]


def rel_bias_tables(rel_bias, seq):
    tab_t = _bias_by_distance(rel_bias)
    max_d = tab_t.shape[1]
    lo = -(2 * QB - 1)
    tbs = _toeplitz_tiles(_ext(tab_t, lo, max_d, 0, None), lo, range(-1, N_SEL_BIAS))
    tbw = _toeplitz_tiles(_ext(tab_t, lo, max_d, 0, WINDOW), lo, range(N_WIN_TILES - 1, -1, -1))
    tbw = jnp.transpose(tbw, (0, 2, 1, 3)).reshape(N_HEADS, QB, N_WIN_TILES * QB)
    nqb = seq // QB
    ncp = seq // CMP_STRIDE
    per_qb = QB // CMP_STRIDE
    width = ncp + per_qb * (nqb - 1)
    c0 = QB * (nqb - 1) - (L_CMP - 1)
    ext = _ext(tab_t, c0 - CMP_STRIDE * (width - 1), c0 + QB)
    n_z = width + per_qb - 1
    f = ext[:, :CMP_STRIDE * n_z].reshape(N_HEADS, n_z, CMP_STRIDE)[:, ::-1]
    v = jnp.concatenate([jnp.swapaxes(f[:, per_qb - 1 - a:per_qb - 1 - a + width], 1, 2) for a in range(per_qb)],
                        axis=1)
    bc = jnp.stack([v[:, :, per_qb * (nqb - 1 - b):per_qb * (nqb - 1 - b) + ncp] for b in range(nqb)])
    return tbs, tbw, bc


def nsa_prompt_pallas(q, ck, cv, kv_sel, kv_win, gn, tables):
    B, S = q.shape[:2]
    assert S % (SEL_SPAN * QB) == 0 and S % KV_CHUNK == 0 and S // SEL_BLOCK <= HEAD_DIM
    tbs, tbw, bc = tables
    nqb = S // QB
    ncp = S // CMP_STRIDE
    n_cmp = ncp - 1
    n_blk = S // SEL_BLOCK
    c0 = np.arange(ncp)[None, :] * CMP_STRIDE
    s0 = np.arange(n_blk)[:, None] * SEL_BLOCK
    ovl_t = ((c0 <= s0 + SEL_BLOCK - 1) & (c0 + L_CMP - 1 >= s0) & (np.arange(ncp)[None, :] < n_cmp))
    ovl_t = jnp.asarray(ovl_t.astype(np.float32), BF16)
    eye = jnp.asarray(np.eye(QB, dtype=np.float32), BF16)
    gw = HPG * HEAD_DIM
    pq = np.zeros((gw, HPG * LANES), np.float32)
    pq[np.arange(gw), (np.arange(gw) // HEAD_DIM) * LANES + np.arange(gw) % HEAD_DIM] = 1.0
    pk = np.zeros((KV_HEADS, KV_ROW, LANES), np.float32)
    pv = np.zeros((KV_HEADS, KV_ROW, LANES), np.float32)
    for g in range(KV_HEADS):
        pk[g, g * HEAD_DIM + np.arange(HEAD_DIM), np.arange(HEAD_DIM)] = 1.0
        pv[g, (KV_HEADS + g) * HEAD_DIM + np.arange(HEAD_DIM), np.arange(HEAD_DIM)] = 1.0
    onehot = np.zeros((S, LANES), np.float32)
    onehot[np.arange(S), HEAD_DIM + np.arange(S) // SEL_BLOCK] = 1.0

    kv_spec = pl.BlockSpec((1, S, KV_ROW), lambda b, g, i: (b, 0, 0))
    cmp_spec = pl.BlockSpec((1, 1, ncp, LANES), lambda b, g, i: (b, g, 0, 0))
    const2 = lambda b, g, i: (0, 0)
    return pl.pallas_call(
        functools.partial(_nsa_prompt_kernel, n_cmp=n_cmp, n_blk=n_blk),
        grid=(B, KV_HEADS, nqb),
        in_specs=[
            pl.BlockSpec((1, QB, gw), lambda b, g, i: (b, i, g)),
            cmp_spec, cmp_spec, kv_spec, kv_spec,
            pl.BlockSpec((1, QB, LANES), lambda b, g, i: (b, i, g)),
            pl.BlockSpec((1, HPG, QB, ncp), lambda b, g, i: (i, g, 0, 0)),
            pl.BlockSpec((HPG, N_SEL_BIAS + 1, QB, QB), lambda b, g, i: (g, 0, 0, 0)),
            pl.BlockSpec((HPG, QB, N_WIN_TILES * QB), lambda b, g, i: (g, 0, 0)),
            pl.BlockSpec((n_blk, ncp), const2),
            pl.BlockSpec((QB, QB), const2),
            pl.BlockSpec((gw, HPG * LANES), const2),
            pl.BlockSpec((1, KV_ROW, LANES), lambda b, g, i: (g, 0, 0)),
            pl.BlockSpec((1, KV_ROW, LANES), lambda b, g, i: (g, 0, 0)),
            pl.BlockSpec((S, LANES), const2),
        ],
        out_specs=pl.BlockSpec((1, QB, gw), lambda b, g, i: (b, i, g)),
        out_shape=jax.ShapeDtypeStruct((B, S, N_HEADS * HEAD_DIM), BF16),
        scratch_shapes=[pltpu.VMEM((ROWS, LANES), F32), pltpu.VMEM((ROWS, LANES), F32),
                        pltpu.VMEM((ROWS, LANES), BF16), pltpu.VMEM((ROWS, LANES), F32),
                        pltpu.VMEM((S, LANES), BF16), pltpu.VMEM((S, LANES), BF16),
                        pltpu.VMEM((S + WINDOW, LANES), BF16), pltpu.VMEM((S + WINDOW, LANES), BF16),
                        pltpu.VMEM((n_blk, QB), F32)],
        compiler_params=pltpu.CompilerParams(dimension_semantics=("parallel", "parallel", "arbitrary"),
                                             vmem_limit_bytes=VMEM_LIMIT_BYTES),
        name="nsa_prompt",
    )(q, ck, cv, kv_sel, kv_win, gn, bc, tbs, tbw, ovl_t, eye, jnp.asarray(pq, BF16), jnp.asarray(pk, BF16),
      jnp.asarray(pv, BF16), jnp.asarray(onehot, BF16))


STEP_ROWS = SUBLANES
STAGE_PAGES = 8
RANK_LANES = 2 * LANES


def _split3(x):
    hi = x.astype(BF16)
    r1 = x - hi.astype(F32)
    mid = r1.astype(BF16)
    return hi, mid, (r1 - mid.astype(F32)).astype(BF16)


def _nsa_step_kernel(*refs, n_pages, n_cmp):
    pages = refs[1:1 + n_pages]
    (q_ref, ck_ref, cv_ref, kvs_ref, kvw_ref, win_ref, gn_ref, bcs_ref, bsel_ref, bwin_ref, ovl_ref, oh_ref,
     pq_ref, pk_ref, pv_ref, o_ref, kc_ref, vs_ref, kw_ref, vw_ref) = refs[1 + n_pages:]
    page = pages[0].shape[-1]
    past = n_pages * page
    wb = win_ref.shape[-1]
    ncp = ck_ref.shape[2]
    first_row = lax.broadcasted_iota(jnp.int32, (LANES, 1), 0) == 0

    @pl.when(pl.program_id(0) == 0)
    def _():
        for g in range(KV_HEADS):
            kc_ref[g, HEAD_DIM:LANES, :] = jnp.zeros((LANES - HEAD_DIM, past + LANES), BF16)
            kc_ref[g, LANES:2 * LANES, 0:past] = oh_ref[...]
            kc_ref[g, LANES:2 * LANES, past:past + LANES] = jnp.zeros((LANES, LANES), BF16)
            vs_ref[g, HEAD_DIM:LANES, :] = jnp.ones((LANES - HEAD_DIM, past + LANES), BF16)
            kw_ref[g, HEAD_DIM:LANES, :] = jnp.zeros((LANES - HEAD_DIM, wb + LANES), BF16)
            vw_ref[g, HEAD_DIM:LANES, :] = jnp.ones((LANES - HEAD_DIM, wb + LANES), BF16)

    def place_new(row_ref, c0, k_dst, v_dst):
        new = jnp.where(first_row, row_ref[0], 0.0).astype(BF16)
        for g in range(KV_HEADS):
            k_dst[g, 0:HEAD_DIM, c0:c0 + LANES] = _nt(pk_ref[g], new)[0:HEAD_DIM].astype(BF16)
            v_dst[g, 0:HEAD_DIM, c0:c0 + LANES] = _nt(pv_ref[g], new)[0:HEAD_DIM].astype(BF16)

    for k, pg in enumerate(pages):
        for g in range(KV_HEADS):
            kc_ref[g, 0:HEAD_DIM, k * page:(k + 1) * page] = pg[0, 0, 0, g].astype(BF16)
            vs_ref[g, 0:HEAD_DIM, k * page:(k + 1) * page] = pg[0, 0, 1, g].astype(BF16)
    place_new(kvs_ref, past, kc_ref, vs_ref)
    for g in range(KV_HEADS):
        kw_ref[g, 0:HEAD_DIM, 0:wb] = win_ref[0, 0, 0, g].astype(BF16)
        vw_ref[g, 0:HEAD_DIM, 0:wb] = win_ref[0, 0, 1, g].astype(BF16)
    place_new(kvw_ref, wb, kw_ref, vw_ref)

    sig = jax.nn.sigmoid(gn_ref[0])
    lane = lax.broadcasted_iota(jnp.int32, (1, LANES), 1)
    rr = lax.broadcasted_iota(jnp.int32, (RANK_LANES, RANK_LANES), 0)
    cc = lax.broadcasted_iota(jnp.int32, (RANK_LANES, RANK_LANES), 1)
    ones_sq = jnp.ones((RANK_LANES, RANK_LANES), BF16)
    pad_rows = jnp.zeros((STEP_ROWS - HPG, LANES), F32)
    for g in range(KV_HEADS):
        gw = HPG * HEAD_DIM
        q4 = jnp.dot(q_ref[0][:, g * gw:(g + 1) * gw], pq_ref[...], preferred_element_type=F32)
        q8 = jnp.concatenate([q4[:, j * LANES:(j + 1) * LANES] for j in range(HPG)] + [pad_rows], axis=0)
        gates = [jnp.concatenate([sig[:, g * LANES + 3 * j + c:g * LANES + 3 * j + c + 1] for j in range(HPG)]
                                 + [pad_rows[:, 0:1]], axis=0) for c in range(3)]

        lc = _nt(q8.astype(BF16), ck_ref[0, g]) + bcs_ref[g]
        col = lax.broadcasted_iota(jnp.int32, (STEP_ROWS, ncp), 1)
        valid = col < n_cmp
        lc = jnp.where(valid, lc, NEG)
        p = jnp.where(valid, jnp.exp(lc - jnp.max(lc, axis=1, keepdims=True)), 0.0)
        pc = p / jnp.maximum(jnp.sum(p, axis=1, keepdims=True), 1e-30)
        comb = gates[0] * jnp.dot(pc.astype(BF16), cv_ref[0, g], preferred_element_type=F32)

        pcs = jnp.broadcast_to(jnp.sum(pc[0:HPG], axis=0, keepdims=True), (STEP_ROWS, ncp))
        hi = pcs.astype(BF16)
        lo = (pcs - hi.astype(F32)).astype(BF16)
        imp = (jnp.dot(hi, ovl_ref[...], preferred_element_type=F32)
               + jnp.dot(lo, ovl_ref[...], preferred_element_type=F32))[0:1]
        blk_id = lax.broadcasted_iota(jnp.int32, (1, RANK_LANES), 1)
        cur = past // SEL_BLOCK
        forced = (blk_id == 0) | (blk_id == cur) | (blk_id == cur - 1)
        score = jnp.where(blk_id > cur, -1.0, jnp.where(forced, FORCE, imp))
        s_n = jnp.broadcast_to(score, (RANK_LANES, RANK_LANES))
        diag = jnp.where(rr == cc, s_n, 0.0)
        s_m = sum(_nt(part, ones_sq) for part in _split3(diag))
        beats = jnp.where(rr < cc, jnp.where(s_m >= s_n, 1.0, 0.0), jnp.where(s_m > s_n, 1.0, 0.0))
        rank = jnp.sum(beats, axis=0, keepdims=True)
        sel_neg = jnp.where(rank < float(N_SELECT), 0.0, NEG)

        qa = jnp.concatenate([q8, jnp.broadcast_to(sel_neg[:, :LANES], (STEP_ROWS, LANES))], axis=1).astype(BF16)
        s_past = jnp.dot(qa, kc_ref[g, :, 0:past], preferred_element_type=F32) + bsel_ref[g, :, 0:past]
        s_new = (jnp.dot(qa, kc_ref[g, :, past:past + LANES], preferred_element_type=F32)
                 + bsel_ref[g, :, past:past + LANES] + sel_neg[:, LANES:LANES + 1])
        mx = jnp.maximum(jnp.max(s_past, axis=1, keepdims=True), jnp.max(s_new, axis=1, keepdims=True))
        acc = (_nt(jnp.exp(s_past - mx).astype(BF16), vs_ref[g, :, 0:past])
               + _nt(jnp.exp(s_new - mx).astype(BF16), vs_ref[g, :, past:past + LANES]))
        comb = comb + gates[1] * (acc / pltpu.roll(acc, HEAD_DIM, axis=1))

        sw = jnp.dot(q8.astype(BF16), kw_ref[g], preferred_element_type=F32) + bwin_ref[g]
        pw = jnp.exp(sw - jnp.max(sw, axis=1, keepdims=True))
        accw = _nt(pw.astype(BF16), vw_ref[g])
        comb = comb + gates[2] * (accw / pltpu.roll(accw, HEAD_DIM, axis=1))

        for half in range(HPG // 2):
            a = comb[2 * half:2 * half + 1]
            b = comb[2 * half + 1:2 * half + 2]
            o_ref[0, :, g * gw + half * LANES:g * gw + (half + 1) * LANES] = jnp.where(
                lane < HEAD_DIM, a, pltpu.roll(b, HEAD_DIM, axis=1)).astype(o_ref.dtype)


def step_bias_tables(rel_bias, past, wb, ncp):
    tab_t = _bias_by_distance(rel_bias)

    def rows(t):
        return jnp.pad(t.reshape(KV_HEADS, HPG, -1), ((0, 0), (0, STEP_ROWS - HPG), (0, 0)))

    d_first = past - (L_CMP - 1)
    cmp_t = _ext(tab_t, d_first - CMP_STRIDE * (ncp - 1), d_first + 1, 0, None)[:, ::CMP_STRIDE][:, ::-1]
    sel_t = _ext(tab_t, -(LANES - 1), past + 1, 0, None)[:, ::-1]
    win_t = _ext(tab_t, -(LANES - 1), wb + 1, 0, WINDOW)[:, ::-1]
    return rows(cmp_t), rows(sel_t), rows(win_t)


def nsa_step(q, ck, cv, kvs_new, kvw_new, sel_pool, win_buf, gn, page_table, tables, layer):
    B, n_pages = page_table.shape
    page = sel_pool.shape[-1]
    past = n_pages * page
    wb = win_buf.shape[-1]
    ncp = ck.shape[2]
    n_blk = past // SEL_BLOCK + 1
    assert n_blk <= LANES + 1 and page == LANES and wb % LANES == 0
    bcs, bsel, bwin = tables
    c0 = np.arange(ncp)[:, None] * CMP_STRIDE
    s0 = np.arange(RANK_LANES)[None, :] * SEL_BLOCK
    ovl = ((c0 <= s0 + SEL_BLOCK - 1) & (c0 + L_CMP - 1 >= s0) & (np.arange(ncp)[:, None] < ncp - 1)
           & (np.arange(RANK_LANES)[None, :] < n_blk))
    onehot = np.zeros((LANES, past), np.float32)
    onehot[np.arange(past) // SEL_BLOCK, np.arange(past)] = 1.0
    gw = HPG * HEAD_DIM
    pq = np.zeros((gw, HPG * LANES), np.float32)
    pq[np.arange(gw), (np.arange(gw) // HEAD_DIM) * LANES + np.arange(gw) % HEAD_DIM] = 1.0
    pk = np.zeros((KV_HEADS, LANES, KV_ROW), np.float32)
    pv = np.zeros((KV_HEADS, LANES, KV_ROW), np.float32)
    for g in range(KV_HEADS):
        pk[g, np.arange(HEAD_DIM), g * HEAD_DIM + np.arange(HEAD_DIM)] = 1.0
        pv[g, np.arange(HEAD_DIM), (KV_HEADS + g) * HEAD_DIM + np.arange(HEAD_DIM)] = 1.0
    consts = [jnp.asarray(ovl.astype(np.float32), BF16), jnp.asarray(onehot, BF16), jnp.asarray(pq, BF16),
              jnp.asarray(pk, BF16), jnp.asarray(pv, BF16)]

    row3 = lambda a: a.reshape(B, 1, a.shape[-1])
    per_b = lambda shape: pl.BlockSpec((1,) + shape, lambda b, pt: (b,) + (0,) * len(shape))
    full = lambda a: pl.BlockSpec(a.shape, lambda b, pt: (0,) * a.ndim)
    slab = (2, KV_HEADS, HEAD_DIM)
    page_specs = [pl.BlockSpec((1, 1) + slab + (page,), lambda b, pt, k=k: (layer, pt[b, k], 0, 0, 0, 0))
                  for k in range(n_pages)]
    win_spec = pl.BlockSpec((1, 1) + slab + (wb,), lambda b, pt: (layer, b, 0, 0, 0, 0))
    in_specs = page_specs + [per_b((1, Q_COLS)), per_b((KV_HEADS, ncp, LANES)), per_b((KV_HEADS, ncp, LANES)),
                             per_b((1, KV_ROW)), per_b((1, KV_ROW)), win_spec, per_b((1, GN_COLS)),
                             full(bcs), full(bsel), full(bwin)] + [full(c) for c in consts]
    grid_spec = pltpu.PrefetchScalarGridSpec(
        num_scalar_prefetch=1, grid=(B,), in_specs=in_specs, out_specs=per_b((1, Q_COLS)),
        scratch_shapes=[pltpu.VMEM((KV_HEADS, 2 * LANES, past + LANES), BF16), pltpu.VMEM((KV_HEADS, LANES, past + LANES), BF16),
                        pltpu.VMEM((KV_HEADS, LANES, wb + LANES), BF16), pltpu.VMEM((KV_HEADS, LANES, wb + LANES), BF16)])
    out = pl.pallas_call(
        functools.partial(_nsa_step_kernel, n_pages=n_pages, n_cmp=ncp - 1),
        grid_spec=grid_spec,
        out_shape=jax.ShapeDtypeStruct((B, 1, Q_COLS), BF16),
        compiler_params=pltpu.CompilerParams(dimension_semantics=("arbitrary",), vmem_limit_bytes=VMEM_LIMIT_BYTES),
        name="nsa_step",
    )(page_table, *([sel_pool] * n_pages), row3(q), ck, cv, row3(kvs_new), row3(kvw_new), win_buf, row3(gn),
      bcs, bsel, bwin, *consts)
    return out.reshape(B, Q_COLS)


def _merge_kernel(ya_ref, yb_ref, yc_ref, ga_ref, gb_ref, gc_ref, x_ref, wa_ref, wb_ref, wc_ref, wo_ref, o_ref):
    def branch(y_ref, g_ref, w_ref):
        return jax.nn.sigmoid(g_ref[...]) * jnp.dot(y_ref[...], w_ref[...], preferred_element_type=F32)

    merged = branch(ya_ref, ga_ref, wa_ref) + branch(yb_ref, gb_ref, wb_ref) + branch(yc_ref, gc_ref, wc_ref)
    o_ref[...] = x_ref[...] + jnp.dot(merged.astype(BF16), wo_ref[...], preferred_element_type=F32)


def merge(ya, yb, yc, za, x, wa, wb, wc, wo):
    T = x.shape[0]
    tm = _pick(T, (512, 256, 128))
    y_spec = pl.BlockSpec((tm, ya.shape[1]), lambda i: (i, 0))
    w_spec = pl.BlockSpec((ya.shape[1], D_MODEL), lambda i: (0, 0))
    return pl.pallas_call(
        _merge_kernel,
        grid=(T // tm,),
        in_specs=[y_spec, y_spec, y_spec,
                  pl.BlockSpec((tm, D_MODEL), lambda i: (i, 0)), pl.BlockSpec((tm, D_MODEL), lambda i: (i, 1)),
                  pl.BlockSpec((tm, D_MODEL), lambda i: (i, 2)),
                  pl.BlockSpec((tm, D_MODEL), lambda i: (i, 0)),
                  w_spec, w_spec, w_spec, pl.BlockSpec((D_MODEL, D_MODEL), lambda i: (0, 0))],
        out_specs=pl.BlockSpec((tm, D_MODEL), lambda i: (i, 0)),
        out_shape=jax.ShapeDtypeStruct((T, D_MODEL), F32),
        compiler_params=pltpu.CompilerParams(dimension_semantics=("parallel",), vmem_limit_bytes=VMEM_LIMIT_BYTES),
        name="merge",
    )(ya, yb, yc, za, za, za, x, wa, wb, wc, wo)


def _ffn_kernel(*refs, routed):
    if routed:
        h_ref, g_ref, wr_ref, br_ref, wg_ref, wu_ref, wd_ref, o_ref, xn_ref, gate_ref = refs
    else:
        h_ref, g_ref, wg_ref, wu_ref, wd_ref, o_ref, xn_ref = refs
    e = pl.program_id(1)
    f = pl.program_id(2)

    @pl.when((e == 0) & (f == 0))
    def _():
        h = h_ref[...]
        xb = _rms_rows(h, g_ref[...]).astype(BF16)
        xn_ref[...] = xb
        o_ref[...] = h
        if routed:
            logits = jnp.dot(xb, wr_ref[...], preferred_element_type=F32) + br_ref[...]
            lane = lax.broadcasted_iota(jnp.int32, logits.shape, 1)
            m1 = jnp.max(logits, axis=1, keepdims=True)
            i1 = jnp.min(jnp.where(logits == m1, lane, LANES), axis=1, keepdims=True)
            rest = jnp.where(lane == i1, M_INIT, logits)
            m2 = jnp.max(rest, axis=1, keepdims=True)
            i2 = jnp.min(jnp.where(rest == m2, lane, LANES), axis=1, keepdims=True)
            r = jnp.exp(m2 - m1)
            gate_ref[...] = jnp.where(lane == i1, 1.0 / (1.0 + r), 0.0) + jnp.where(lane == i2, r / (1.0 + r), 0.0)

    xb = xn_ref[...]
    a = jnp.dot(xb, wg_ref[0], preferred_element_type=F32)
    u = jnp.dot(xb, wu_ref[0], preferred_element_type=F32)
    t = (a * jax.nn.sigmoid(a)) * u
    y = jnp.dot(t.astype(BF16), wd_ref[0], preferred_element_type=F32)
    if routed:
        lane = lax.broadcasted_iota(jnp.int32, gate_ref.shape, 1)
        y = jnp.sum(jnp.where(lane == e, gate_ref[...], 0.0), axis=1, keepdims=True) * y
    o_ref[...] += y


def channel_mixer(h, norm_gain, wg, wu, wd, router=None):
    T = h.shape[0]
    E, _, F = wg.shape
    tm = _pick(T, (512, 256, 128))
    tf = _pick(F, (1408, 1024, 512, 256, 128))
    routed = router is not None
    row = lambda i, e, f: (i, 0)
    in_specs = [pl.BlockSpec((tm, D_MODEL), row), pl.BlockSpec((1, D_MODEL), lambda i, e, f: (0, 0))]
    args = [h, norm_gain.astype(F32)[None]]
    scratch = [pltpu.VMEM((tm, D_MODEL), BF16)]
    if routed:
        in_specs += [pl.BlockSpec((D_MODEL, LANES), lambda i, e, f: (0, 0)), pl.BlockSpec((1, LANES), lambda i, e, f: (0, 0))]
        args += list(router)
        scratch.append(pltpu.VMEM((tm, LANES), F32))
    in_specs += [pl.BlockSpec((1, D_MODEL, tf), lambda i, e, f: (e, 0, f)),
                 pl.BlockSpec((1, D_MODEL, tf), lambda i, e, f: (e, 0, f)),
                 pl.BlockSpec((1, tf, D_MODEL), lambda i, e, f: (e, f, 0))]
    args += [wg, wu, wd]
    return pl.pallas_call(
        functools.partial(_ffn_kernel, routed=routed),
        grid=(T // tm, E, F // tf),
        in_specs=in_specs,
        out_specs=pl.BlockSpec((tm, D_MODEL), row),
        out_shape=jax.ShapeDtypeStruct((T, D_MODEL), F32),
        scratch_shapes=scratch,
        compiler_params=pltpu.CompilerParams(dimension_semantics=("parallel", "arbitrary", "arbitrary"),
                                             vmem_limit_bytes=VMEM_LIMIT_BYTES),
        name="moe" if routed else "ffn",
    )(*args)


MOE_TM = 512
MOE_SUB = 128


def _moe_route_kernel(h_ref, g_ref, wr_ref, br_ref, tri_ref, xn_ref, gate_ref, pos_ref, cnt_ref):
    xb = _rms_rows(h_ref[...], g_ref[...]).astype(BF16)
    xn_ref[...] = xb
    logits = jnp.dot(xb, wr_ref[...], preferred_element_type=F32) + br_ref[...]
    lane = lax.broadcasted_iota(jnp.int32, logits.shape, 1)
    m1 = jnp.max(logits, axis=1, keepdims=True)
    i1 = jnp.min(jnp.where(logits == m1, lane, LANES), axis=1, keepdims=True)
    rest = jnp.where(lane == i1, M_INIT, logits)
    m2 = jnp.max(rest, axis=1, keepdims=True)
    i2 = jnp.min(jnp.where(rest == m2, lane, LANES), axis=1, keepdims=True)
    r = jnp.exp(m2 - m1)
    chosen = (lane == i1) | (lane == i2)
    gate_ref[...] = jnp.where(lane == i1, 1.0 / (1.0 + r), 0.0) + jnp.where(lane == i2, r / (1.0 + r), 0.0)
    picked = jnp.where(chosen, 1.0, 0.0)
    before = jnp.dot(tri_ref[...], picked.astype(BF16), preferred_element_type=F32)
    pos_ref[...] = jnp.where(chosen, before, -1.0)
    cnt_ref[0] = jnp.sum(picked, axis=0, keepdims=True).astype(jnp.int32)


def _moe_expert_kernel(cnt_ref, h_ref, xn_ref, gate_ref, pos_ref, eye_ref, wg_ref, wu_ref, wd_ref, o_ref):
    i = pl.program_id(0)
    e = pl.program_id(1)
    s = pl.program_id(2)

    @pl.when((e == 0) & (s == 0))
    def _():
        o_ref[...] = h_ref[...]

    sub = eye_ref.shape[0]

    @pl.when(s * sub < cnt_ref[i, e])
    def _():
        lane = lax.broadcasted_iota(jnp.int32, gate_ref.shape, 1)
        mine = lane == e
        gate_e = jnp.sum(jnp.where(mine, gate_ref[...], 0.0), axis=1, keepdims=True)
        pos_e = jnp.sum(jnp.where(mine, pos_ref[...], 0.0), axis=1, keepdims=True)
        slot = (lax.broadcasted_iota(jnp.int32, (pos_e.shape[0], sub), 1) + s * sub).astype(F32)
        pick_t = jnp.where(pos_e == slot, 1.0, 0.0).astype(BF16)
        pick = _nt(eye_ref[...], pick_t).astype(BF16)
        xg = jnp.dot(pick, xn_ref[...], preferred_element_type=F32).astype(BF16)
        a = jnp.dot(xg, wg_ref[0], preferred_element_type=F32)
        u = jnp.dot(xg, wu_ref[0], preferred_element_type=F32)
        t = (a * jax.nn.sigmoid(a)) * u
        y = jnp.dot(t.astype(BF16), wd_ref[0], preferred_element_type=F32)
        y_hi = y.astype(BF16)
        y_lo = (y - y_hi.astype(F32)).astype(BF16)
        back = (jnp.dot(pick_t, y_hi, preferred_element_type=F32) + jnp.dot(pick_t, y_lo, preferred_element_type=F32))
        o_ref[...] += gate_e * back


def moe_mixer(h, norm_gain, wg, wu, wd, router):
    T = h.shape[0]
    E, _, F = wg.shape
    tm = _pick(T, (MOE_TM, 256, 128))
    sub = min(MOE_SUB, tm)
    nt = T // tm
    tri = jnp.asarray(np.tril(np.ones((tm, tm), np.float32), -1), BF16)
    row = lambda i: (i, 0)
    xn, gate, pos, cnt = pl.pallas_call(
        _moe_route_kernel,
        grid=(nt,),
        in_specs=[pl.BlockSpec((tm, D_MODEL), row), pl.BlockSpec((1, D_MODEL), lambda i: (0, 0)),
                  pl.BlockSpec((D_MODEL, LANES), lambda i: (0, 0)), pl.BlockSpec((1, LANES), lambda i: (0, 0)),
                  pl.BlockSpec((tm, tm), lambda i: (0, 0))],
        out_specs=[pl.BlockSpec((tm, D_MODEL), row), pl.BlockSpec((tm, LANES), row), pl.BlockSpec((tm, LANES), row),
                   pl.BlockSpec((1, 1, LANES), lambda i: (i, 0, 0))],
        out_shape=[jax.ShapeDtypeStruct((T, D_MODEL), BF16), jax.ShapeDtypeStruct((T, LANES), F32),
                   jax.ShapeDtypeStruct((T, LANES), F32), jax.ShapeDtypeStruct((nt, 1, LANES), jnp.int32)],
        compiler_params=pltpu.CompilerParams(dimension_semantics=("parallel",), vmem_limit_bytes=VMEM_LIMIT_BYTES),
        name="moe_route",
    )(h, norm_gain.astype(F32)[None], router[0], router[1], tri)
    counts = cnt[:, 0, :E]
    eye = jnp.asarray(np.eye(sub, dtype=np.float32), BF16)
    rows = lambda i, e, s, c: (i, 0)
    grid_spec = pltpu.PrefetchScalarGridSpec(
        num_scalar_prefetch=1, grid=(nt, E, tm // sub),
        in_specs=[pl.BlockSpec((tm, D_MODEL), rows), pl.BlockSpec((tm, D_MODEL), rows), pl.BlockSpec((tm, LANES), rows),
                  pl.BlockSpec((tm, LANES), rows), pl.BlockSpec((sub, sub), lambda i, e, s, c: (0, 0)),
                  pl.BlockSpec((1, D_MODEL, F), lambda i, e, s, c: (e, 0, 0)),
                  pl.BlockSpec((1, D_MODEL, F), lambda i, e, s, c: (e, 0, 0)),
                  pl.BlockSpec((1, F, D_MODEL), lambda i, e, s, c: (e, 0, 0))],
        out_specs=pl.BlockSpec((tm, D_MODEL), rows))
    return pl.pallas_call(
        _moe_expert_kernel,
        grid_spec=grid_spec,
        out_shape=jax.ShapeDtypeStruct((T, D_MODEL), F32),
        compiler_params=pltpu.CompilerParams(dimension_semantics=("parallel", "arbitrary", "arbitrary"),
                                             vmem_limit_bytes=VMEM_LIMIT_BYTES),
        name="moe_experts",
    )(counts, h, xn, gate, pos, eye, wg, wu, wd)


def mixer_tail(x2, za, y_a, y_b, y_c, l, P):
    return merge(y_a, y_b, y_c, za, x2, P['w_br_rg'][l].astype(BF16), P['w_br_attn'][l].astype(BF16),
                 P['w_br_pool'][l].astype(BF16), P['w_out'][l].astype(BF16))


def prompt_mixer(x, l, P, packed, tables):
    B, T, _ = x.shape
    x2 = x.reshape(B * T, D_MODEL)
    za, q, kv_c, kv_s, kv_w, gn = projection(x2, P['attn_norm'][l], packed['proj'])
    zeros = lambda rows: jnp.zeros((B, rows, D_RNN), F32)
    y_a, y_c, conv_new, h_last, pool_new = mixer_seq(za, B, T, packed['mix'], zeros(CONV_W - 1), jnp.zeros((B, D_RNN), F32),
                                                     zeros(POOL_MAX - 1), 0)
    ck, cv = compress(kv_c.reshape(B, 2 * T, LANES), packed['cmp'])
    y_b = nsa_prompt_pallas(q.reshape(B, T, Q_COLS), ck, cv, kv_s.reshape(B, T, KV_ROW), kv_w.reshape(B, T, KV_ROW),
                            gn.reshape(B, T, GN_COLS), tables)
    out = mixer_tail(x2, za, y_a, y_b.reshape(B * T, Q_COLS), y_c, l, P)
    kv_shape = (B, T, 2, KV_HEADS, HEAD_DIM)
    state = (kv_c.reshape(kv_shape), kv_s.reshape(kv_shape), kv_w.reshape(kv_shape)[:, -min(WINDOW, T):], conv_new,
             h_last[:, 0], pool_new)
    return out.reshape(B, T, D_MODEL), state


def sample_mixer(x, l, P, packed, past_len, conv_state, h0, pool_state, caches, win_buf, page_table, step_tables):
    B, T, _ = x.shape
    x2 = x.reshape(B * T, D_MODEL)
    za, q, kv_c, kv_s, kv_w, gn = projection(x2, P['attn_norm'][l], packed['proj'])
    y_a, y_c, h_new = mixer_step(za, packed['mix'], conv_state, h0, pool_state, past_len)
    cmp_pools, sel_pools, win_bufs = caches
    ck, cv = compress(cmp_pools, packed['cmp'], page_table, layer=l)
    kv_shape = (B, T, 2, KV_HEADS, HEAD_DIM)
    wb = win_buf.shape[1]
    y_b = nsa_step(q, ck, cv, kv_s, kv_w, sel_pools, win_bufs, gn, page_table, step_tables, l)
    win_new = jnp.concatenate([win_buf[:, 1:], kv_w.reshape(kv_shape)], axis=1)
    out = mixer_tail(x2, za, y_a, y_b, y_c, l, P)
    conv_new = jnp.concatenate([conv_state[:, 1:], za[:, None, ZA_XRG:ZA_XRG + D_RNN]], axis=1)
    pool_new = jnp.concatenate([pool_state[:, 1:], za[:, None, ZA_XPOOL:ZA_XPOOL + D_POOL]], axis=1)
    state = (kv_c.reshape(kv_shape), kv_s.reshape(kv_shape), win_new, conv_new, h_new, pool_new)
    return out.reshape(B, T, D_MODEL), state


def ffn_layer(x, l, P, W):
    B, T, _ = x.shape
    i = l // 2
    if l % 2 == 0:
        y = channel_mixer(x.reshape(B * T, D_MODEL), P['ffn_norm'][l], W['ffn_g'][i], W['ffn_u'][i], W['ffn_d'][i])
    else:
        y = moe_mixer(x.reshape(B * T, D_MODEL), P['ffn_norm'][l], W['moe_g'][i], W['moe_u'][i], W['moe_d'][i],
                      W['router'][i])
    return y.reshape(B, T, D_MODEL)


def kernel(x_prompt, x_sample, cache_cmp_kv, cache_sel_kv, cache_win_kv, state_conv, state_rg_h, state_pool,
           page_table, attn_norm, w_in, conv_w, conv_b, rg_w_a, rg_b_a, rg_w_x, rg_b_x, rg_lambda, q_norm, k_norm,
           cmp_pe, w_cmp1, w_cmp2, rel_bias, w_pool, pool_scale, w_br_rg, w_br_attn, w_br_pool, w_out, ffn_norm,
           ffn_w_gate, ffn_w_up, ffn_w_down, w_router, b_router, moe_w_gate, moe_w_up, moe_w_down):
    P = dict(attn_norm=attn_norm, conv_w=conv_w, conv_b=conv_b, rg_w_a=rg_w_a, rg_b_a=rg_b_a,
             rg_w_x=rg_w_x, rg_b_x=rg_b_x, rg_lambda=rg_lambda, w_pool=w_pool,
             pool_scale=pool_scale, w_br_rg=w_br_rg, w_br_attn=w_br_attn, w_br_pool=w_br_pool, w_out=w_out,
             ffn_norm=ffn_norm)
    depth = w_in.shape[0]
    n_moe = w_router.shape[0]
    pad_e = LANES - N_EXPERTS
    W = dict(
        ffn_g=[w[None].astype(BF16) for w in ffn_w_gate], ffn_u=[w[None].astype(BF16) for w in ffn_w_up],
        ffn_d=[w[None].astype(BF16) for w in ffn_w_down],
        moe_g=[w.astype(BF16) for w in moe_w_gate], moe_u=[w.astype(BF16) for w in moe_w_up],
        moe_d=[w.astype(BF16) for w in moe_w_down],
        router=[(jnp.pad(w_router[i], ((0, 0), (0, pad_e))).astype(BF16),
                 jnp.pad(b_router[i].astype(F32), (0, pad_e), constant_values=NEG)[None]) for i in range(n_moe)])
    past_len = page_table.shape[1] * PAGE_SIZE
    y_p, y_s = x_prompt, x_sample
    tables = rel_bias_tables(rel_bias, x_prompt.shape[1])
    step_tables = step_bias_tables(rel_bias, past_len, cache_win_kv.shape[2], past_len // CMP_STRIDE)
    positions_minor = (0, 1, 3, 4, 5, 2)
    caches = tuple(jnp.transpose(c, positions_minor) for c in (cache_cmp_kv, cache_sel_kv, cache_win_kv))
    p_list, s_list = [], []
    for l in range(depth):
        packed = dict(proj=pack_projection(w_in[l], q_norm[l], k_norm[l, 1], k_norm[l, 2]),
                      cmp=pack_compress(w_cmp1[l], w_cmp2[l], cmp_pe[l], k_norm[l, 0]),
                      mix=pack_mixer(conv_w[l], conv_b[l], rg_w_a[l], rg_b_a[l], rg_w_x[l], rg_b_x[l], rg_lambda[l],
                                     w_pool[l], pool_scale[l]))
        y_p, st_p = prompt_mixer(y_p, l, P, packed, tables)
        y_p = ffn_layer(y_p, l, P, W)
        p_list.append(st_p)
        y_s, st_s = sample_mixer(y_s, l, P, packed, past_len, state_conv[l], state_rg_h[l], state_pool[l],
                                 caches, cache_win_kv[l], page_table, step_tables)
        y_s = ffn_layer(y_s, l, P, W)
        s_list.append(st_s)
    p_cmp_kv, p_sel_kv, p_win_kv, p_conv, p_h, p_pool = [jnp.stack(a) for a in zip(*p_list)]
    s_cmp_kv, s_sel_kv, s_win_kv, s_conv, s_h, s_pool = [jnp.stack(a) for a in zip(*s_list)]
    return (y_p, y_s, p_cmp_kv, p_sel_kv, p_win_kv, p_conv, p_h, p_pool,
            s_cmp_kv, s_sel_kv, s_win_kv, s_conv, s_h, s_pool)
```

```python
import math
import functools

import jax
import jax.numpy as jnp
import numpy as np
from jax import lax
from jax.experimental import pallas as pl
from jax.experimental.pallas import tpu as pltpu

D_MODEL = 1024
PAGE_SIZE = 128
F32 = jnp.float32
BF16 = jnp.bfloat16
EPS = 1e-6
NEG = -1e30
FORCE = 1e4
D_RNN = 512
CONV_W = 4
RG_C = 8.0
N_HEADS = 8
KV_HEADS = 2
HPG = N_HEADS // KV_HEADS
HEAD_DIM = 64
L_CMP = 32
CMP_STRIDE = 16
CMP_HIDDEN = 256
SEL_BLOCK = 64
N_SELECT = 16
WINDOW = 512
Q_BLOCK = 128
D_POOL = 512
POOL_WINDOWS = (2, 4, 8, 16)
POOL_GW = D_POOL // len(POOL_WINDOWS)
POOL_MAX = 16
REL_BUCKETS = 32
REL_MAX_DIST = 1024
N_EXPERTS = 8
KV_ROW = 2 * KV_HEADS * HEAD_DIM
SPLITS = (D_RNN, D_RNN, N_HEADS * HEAD_DIM, KV_ROW, KV_ROW, KV_ROW, 3 * N_HEADS, D_POOL, 3 * D_MODEL)

VMEM_LIMIT_BYTES = 52 * 1024 * 1024
LANES = 128
SUBLANES = 8
M_INIT = -3e38


def _pick(n, cands):
    for c in cands:
        if n % c == 0:
            return c
    return n


def _nt(a, b):
    return lax.dot_general(a, b, (((1,), (1,)), ((), ())), preferred_element_type=F32)


def _rms_rows(x, g):
    return x * lax.rsqrt(jnp.mean(x * x, axis=-1, keepdims=True) + EPS) * g


def rel_bucket(dist):
    n_exact = REL_BUCKETS // 2
    d = jnp.maximum(dist, 0)
    df = jnp.maximum(d, 1).astype(F32)
    large = n_exact + (jnp.log(df / n_exact) / math.log(REL_MAX_DIST / n_exact)
                       * (REL_BUCKETS - n_exact)).astype(jnp.int32)
    return jnp.where(d < n_exact, d, jnp.minimum(large, REL_BUCKETS - 1))


PROJ_TN = 512
ZA_COLS = 3 * D_MODEL + 2 * D_RNN + D_POOL
N_ZA = ZA_COLS // PROJ_TN
Q_COLS = N_HEADS * HEAD_DIM
GN_COLS = 2 * LANES
PROJ_COLS = ZA_COLS + Q_COLS + 3 * KV_ROW + GN_COLS
ZA_XRG, ZA_GRG, ZA_XPOOL = 3 * D_MODEL, 3 * D_MODEL + D_RNN, 3 * D_MODEL + 2 * D_RNN


def _proj_kernel(x_ref, g_ref, w_ref, seg_ref, ng_ref, nm_ref, za_ref, q_ref, kvc_ref, kvs_ref, kvw_ref, gn_ref, xn_ref):
    j = pl.program_id(1)

    @pl.when(j == 0)
    def _():
        xn_ref[...] = _rms_rows(x_ref[...], g_ref[...]).astype(BF16)

    acc = jnp.dot(xn_ref[...], w_ref[...], preferred_element_type=F32)

    @pl.when(j < N_ZA)
    def _():
        za_ref[...] = acc

    @pl.when(j >= N_ZA)
    def _():
        sq = acc * acc
        hi = sq.astype(BF16)
        lo = (sq - hi.astype(F32)).astype(BF16)
        ss = (jnp.dot(hi, seg_ref[...], preferred_element_type=F32)
              + jnp.dot(lo, seg_ref[...], preferred_element_type=F32))
        normed = acc * lax.rsqrt(ss * (1.0 / HEAD_DIM) + EPS) * ng_ref[...]
        y = jnp.where(nm_ref[...] > 0.5, normed, acc)

        @pl.when(j == N_ZA)
        def _():
            q_ref[...] = y.astype(BF16)

        @pl.when(j == N_ZA + 1)
        def _():
            kvc_ref[...] = y[:, :KV_ROW]
            kvs_ref[...] = y[:, KV_ROW:]

        @pl.when(j == N_ZA + 2)
        def _():
            kvw_ref[...] = y[:, :KV_ROW]
            gn_ref[...] = y[:, KV_ROW:]


def pack_projection(w_in, q_gain, ks_gain, kw_gain):
    cut = np.cumsum(SPLITS)[:-1].tolist()
    x_rg, g_rg, q, kv_c, kv_s, kv_w, g_nsa, x_pool, g_br = jnp.split(w_in, cut, axis=1)
    per_group = 3 * HPG
    gn = jnp.zeros((w_in.shape[0], GN_COLS), w_in.dtype)
    for g in range(KV_HEADS):
        gn = gn.at[:, g * LANES:g * LANES + per_group].set(g_nsa[:, g * per_group:(g + 1) * per_group])
    w = jnp.concatenate([g_br, x_rg, g_rg, x_pool, q, kv_c, kv_s, kv_w, gn], axis=1).astype(BF16)
    ones_v = jnp.ones((KV_HEADS * HEAD_DIM,), F32)
    zeros_v = jnp.zeros((KV_HEADS * HEAD_DIM,), F32)
    gain = jnp.concatenate([jnp.ones((ZA_COLS,), F32), jnp.tile(q_gain.astype(F32), N_HEADS) * HEAD_DIM ** -0.5,
                            jnp.ones((KV_ROW,), F32),
                            jnp.tile(ks_gain.astype(F32), KV_HEADS), ones_v,
                            jnp.tile(kw_gain.astype(F32), KV_HEADS), ones_v,
                            jnp.ones((GN_COLS,), F32)])
    mask = jnp.concatenate([jnp.zeros((ZA_COLS,), F32), jnp.ones((Q_COLS,), F32), jnp.zeros((KV_ROW,), F32),
                            ones_v, zeros_v, ones_v, zeros_v, jnp.zeros((GN_COLS,), F32)])
    return w, gain[None], mask[None]


def projection(x, norm_gain, packed):
    w, gain, mask = packed
    T = x.shape[0]
    tm = _pick(T, (1024, 512, 256, 128))
    seg = jnp.asarray((np.arange(PROJ_TN)[:, None] // HEAD_DIM == np.arange(PROJ_TN)[None, :] // HEAD_DIM)
                      .astype(np.float32), BF16)
    row = lambda i, j: (i, 0)
    return pl.pallas_call(
        _proj_kernel,
        grid=(T // tm, PROJ_COLS // PROJ_TN),
        in_specs=[pl.BlockSpec((tm, D_MODEL), row),
                  pl.BlockSpec((1, D_MODEL), lambda i, j: (0, 0)),
                  pl.BlockSpec((D_MODEL, PROJ_TN), lambda i, j: (0, j)),
                  pl.BlockSpec((PROJ_TN, PROJ_TN), lambda i, j: (0, 0)),
                  pl.BlockSpec((1, PROJ_TN), lambda i, j: (0, j)),
                  pl.BlockSpec((1, PROJ_TN), lambda i, j: (0, j))],
        out_specs=[pl.BlockSpec((tm, PROJ_TN), lambda i, j: (i, jnp.minimum(j, N_ZA - 1))),
                   pl.BlockSpec((tm, Q_COLS), row),
                   pl.BlockSpec((tm, KV_ROW), row), pl.BlockSpec((tm, KV_ROW), row), pl.BlockSpec((tm, KV_ROW), row),
                   pl.BlockSpec((tm, GN_COLS), row)],
        out_shape=[jax.ShapeDtypeStruct((T, ZA_COLS), F32), jax.ShapeDtypeStruct((T, Q_COLS), BF16),
                   jax.ShapeDtypeStruct((T, KV_ROW), F32), jax.ShapeDtypeStruct((T, KV_ROW), F32),
                   jax.ShapeDtypeStruct((T, KV_ROW), F32), jax.ShapeDtypeStruct((T, GN_COLS), F32)],
        scratch_shapes=[pltpu.VMEM((tm, D_MODEL), BF16)],
        compiler_params=pltpu.CompilerParams(dimension_semantics=("parallel", "arbitrary"),
                                             vmem_limit_bytes=VMEM_LIMIT_BYTES),
        name="projection",
    )(x, norm_gain.astype(F32)[None], w, seg, gain, mask)


CMP_PAIRS = CMP_STRIDE // 2
CMP_GW = KV_HEADS * CMP_HIDDEN


def _compress_kernel(*refs, n_pages, paged):
    if paged:
        pages = refs[1:1 + n_pages]
        eye_ref, w1_ref, pe_ref, w2_ref, kn_ref, ck_ref, cv_ref, half_ref = refs[1 + n_pages:]
        page = pages[0].shape[-1]
        m = n_pages * page // CMP_STRIDE
        for k, pg in enumerate(pages):
            for e in range(2):
                slab = pg[0, 0, e].reshape(KV_HEADS * HEAD_DIM, page).astype(BF16)
                half_ref[e, k * page:(k + 1) * page, :] = _nt(eye_ref[...], slab)
    else:
        rows_ref, w1_ref, pe_ref, w2_ref, kn_ref, ck_ref, cv_ref = refs
        m = rows_ref.shape[1] // (2 * CMP_STRIDE)
    for e, out_ref in enumerate((ck_ref, cv_ref)):
        acc = jnp.zeros((m, 2 * CMP_GW), F32)
        for p in range(CMP_PAIRS):
            def rows(s):
                if paged:
                    return half_ref[e, pl.ds(s, m, stride=CMP_STRIDE), :]
                return rows_ref[0, pl.ds(2 * s + e, m, stride=2 * CMP_STRIDE), :]
            a = jnp.concatenate([rows(2 * p), rows(2 * p + 1)], axis=1).astype(BF16)
            acc = acc + jnp.dot(a, w1_ref[e, p], preferred_element_type=F32)
        h = acc[:, :CMP_GW] + pltpu.roll(acc[:, CMP_GW:], m - 1, axis=0) + pe_ref[e]
        gl = jax.nn.gelu(h).astype(BF16)
        for g in range(KV_HEADS):
            c = jnp.dot(gl, w2_ref[e, g], preferred_element_type=F32)
            if e == 0:
                c = c * lax.rsqrt(jnp.sum(c * c, axis=1, keepdims=True) * (1.0 / HEAD_DIM) + EPS) * kn_ref[...]
            out_ref[0, g] = c.astype(out_ref.dtype)


def pack_compress(w1, w2, pe, kn):
    halves = w1.reshape(2, 2, CMP_STRIDE, HEAD_DIM, CMP_HIDDEN)
    eye = jnp.eye(KV_HEADS, dtype=w1.dtype)
    bd = jnp.einsum('ehsdf,gk->esgdhkf', halves, eye)
    w1p = bd.reshape(2, CMP_PAIRS, 2 * KV_HEADS * HEAD_DIM, 2 * CMP_GW).astype(BF16)
    pe_term = jnp.einsum('led,eldf->ef', pe, w1)
    pe_t = jnp.tile(pe_term, (1, KV_HEADS))[:, None, :].astype(F32)
    w2p = jnp.zeros((2, KV_HEADS, CMP_GW, LANES), w2.dtype)
    for g in range(KV_HEADS):
        w2p = w2p.at[:, g, g * CMP_HIDDEN:(g + 1) * CMP_HIDDEN, :HEAD_DIM].set(w2)
    knp = jnp.concatenate([kn.astype(F32), jnp.zeros((LANES - HEAD_DIM,), F32)])[None]
    return w1p, pe_t, w2p.astype(BF16), knp


def compress(rows, packed, page_table=None, layer=0):
    w1p, pe_t, w2p, knp = packed
    paged = page_table is not None
    if not paged:
        bx, n_pages = rows.shape[0], 1
        m = rows.shape[1] // (2 * CMP_STRIDE)
        data_specs = [pl.BlockSpec((1,) + rows.shape[1:], lambda b: (b, 0, 0))]
        const = lambda nd: (lambda b: (0,) * nd)
        out_map = lambda b: (b, 0, 0, 0)
        data, scratch = [rows], []
    else:
        bx, n_pages = page_table.shape
        page = rows.shape[-1]
        assert page == LANES and KV_HEADS * HEAD_DIM == LANES
        m = n_pages * page // CMP_STRIDE
        eye = jnp.asarray(np.eye(LANES, dtype=np.float32), BF16)
        const = lambda nd: (lambda b, pt: (0,) * nd)
        data_specs = [pl.BlockSpec((1, 1) + rows.shape[2:], lambda b, pt, k=k: (layer, pt[b, k], 0, 0, 0, 0))
                      for k in range(n_pages)] + [pl.BlockSpec(eye.shape, const(2))]
        out_map = lambda b, pt: (b, 0, 0, 0)
        data, scratch = [page_table] + [rows] * n_pages + [eye], [pltpu.VMEM((2, n_pages * page, LANES), F32)]
    in_specs = data_specs + [pl.BlockSpec(w1p.shape, const(4)), pl.BlockSpec(pe_t.shape, const(3)),
                             pl.BlockSpec(w2p.shape, const(4)), pl.BlockSpec(knp.shape, const(2))]
    out_spec = pl.BlockSpec((1, KV_HEADS, m, LANES), out_map)
    out_shape = jax.ShapeDtypeStruct((bx, KV_HEADS, m, LANES), BF16)
    grid_spec = pltpu.PrefetchScalarGridSpec(num_scalar_prefetch=int(paged), grid=(bx,), in_specs=in_specs,
                                             out_specs=[out_spec, out_spec], scratch_shapes=scratch)
    args = data + [w1p, pe_t, w2p, knp]
    return pl.pallas_call(
        functools.partial(_compress_kernel, n_pages=n_pages, paged=paged),
        grid_spec=grid_spec,
        out_shape=[out_shape, out_shape],
        compiler_params=pltpu.CompilerParams(dimension_semantics=("parallel",), vmem_limit_bytes=VMEM_LIMIT_BYTES),
        name="compress",
    )(*args)


MIX_TT = 512
CONV_HALO = SUBLANES
SCAN_UNROLL = 4


def _log1p(y):
    u = 1.0 + y
    return jnp.where(u == 1.0, y, jnp.log(u) * (y / jnp.where(u == 1.0, 1.0, u - 1.0)))


def _neg_expm1(x):
    t = jnp.tanh(0.5 * x)
    return -2.0 * t / (1.0 - t)


def _softplus(x):
    return jnp.maximum(x, 0.0) + _log1p(jnp.exp(-jnp.abs(x)))


def _rglru_coeffs(xc, wa_ref, ba_ref, wx_ref, bx_ref, lam_ref):
    xb = xc.astype(BF16)
    r = jax.nn.sigmoid(jnp.dot(xb, wa_ref[...], preferred_element_type=F32) + ba_ref[...])
    i = jax.nn.sigmoid(jnp.dot(xb, wx_ref[...], preferred_element_type=F32) + bx_ref[...])
    log_a = -RG_C * r * _softplus(-lam_ref[...])
    return jnp.exp(log_a), jnp.sqrt(_neg_expm1(2.0 * log_a)) * (i * xc)


def _pool_project(sums_minus, wp_ref, scale_ref):
    return jnp.dot(sums_minus.astype(BF16), wp_ref[...], preferred_element_type=F32) * scale_ref[...]


def _mixer_seq_kernel(xrg_ref, grg_ref, xpool_ref, conv0_ref, h0_ref, pool0_ref, cw_ref, cb_ref, wa_ref, ba_ref,
                      wx_ref, bx_ref, lam_ref, wp_ref, ps_ref, ya_ref, yc_ref, convn_ref, hn_ref, pooln_ref,
                      xe_ref, pe_ref, a_ref, u_ref, h_ref, carry_ref, *, start_pos):
    i = pl.program_id(1)
    tt = xrg_ref.shape[0]

    @pl.when(i == 0)
    def _():
        xe_ref[0:CONV_HALO, :] = conv0_ref[0]
        pe_ref[0:POOL_MAX, :] = pool0_ref[0]
        carry_ref[...] = h0_ref[0]

    xe_ref[CONV_HALO:CONV_HALO + tt, :] = xrg_ref[...]
    xc = cb_ref[...] + sum(cw_ref[k:k + 1, :] * xe_ref[CONV_HALO - (CONV_W - 1) + k:CONV_HALO - (CONV_W - 1) + k + tt, :]
                           for k in range(CONV_W))
    a, u = _rglru_coeffs(xc, wa_ref, ba_ref, wx_ref, bx_ref, lam_ref)
    a_ref[...] = a
    u_ref[...] = u

    row = lax.broadcasted_iota(jnp.int32, (SUBLANES, D_RNN), 0)

    def block(j, carry):
        r0 = pl.multiple_of(j * SUBLANES, SUBLANES)
        ab = a_ref[pl.ds(r0, SUBLANES), :]
        ub = u_ref[pl.ds(r0, SUBLANES), :]
        for d in (1, 2, 4):
            a_sh = jnp.where(row >= d, pltpu.roll(ab, d, axis=0), 1.0)
            u_sh = jnp.where(row >= d, pltpu.roll(ub, d, axis=0), 0.0)
            ub = ab * u_sh + ub
            ab = ab * a_sh
        hb = ab * carry + ub
        h_ref[pl.ds(r0, SUBLANES), :] = hb
        return jnp.broadcast_to(hb[SUBLANES - 1:SUBLANES, :], (SUBLANES, D_RNN))

    carry = lax.fori_loop(0, tt // SUBLANES, block, carry_ref[...], unroll=SCAN_UNROLL)
    carry_ref[...] = carry
    ya_ref[...] = (h_ref[...] * jax.nn.gelu(grg_ref[...])).astype(ya_ref.dtype)

    pe_ref[POOL_MAX:POOL_MAX + tt, :] = xpool_ref[...]
    pos = start_pos + i * tt + lax.broadcasted_iota(jnp.int32, (tt, 1), 0)
    parts = []
    for g, w in enumerate(POOL_WINDOWS):
        lanes = slice(g * POOL_GW, (g + 1) * POOL_GW)
        s = sum(pe_ref[POOL_MAX - k:POOL_MAX - k + tt, lanes] for k in range(w))
        cnt = jnp.minimum(pos + 1, w).astype(F32)
        parts.append(s / cnt - pe_ref[POOL_MAX:POOL_MAX + tt, lanes])
    yc_ref[...] = _pool_project(jnp.concatenate(parts, axis=1), wp_ref, ps_ref).astype(yc_ref.dtype)

    @pl.when(i == pl.num_programs(1) - 1)
    def _():
        convn_ref[0] = xe_ref[CONV_HALO + tt - (CONV_W - 1):CONV_HALO + tt, :]
        hn_ref[0] = carry[0:1, :]
        pooln_ref[0] = pe_ref[POOL_MAX + tt - (POOL_MAX - 1):POOL_MAX + tt, :]

    xe_ref[0:CONV_HALO, :] = xe_ref[tt:tt + CONV_HALO, :]
    pe_ref[0:POOL_MAX, :] = pe_ref[tt:tt + POOL_MAX, :]


def pack_mixer(conv_w, conv_b, w_a, b_a, w_x, b_x, lam, w_pool, scale):
    def block_diag(w):
        n, c, d = w.shape
        return jnp.einsum('ncd,nm->ncmd', w, jnp.eye(n, dtype=w.dtype)).reshape(n * c, n * d).astype(BF16)

    row = lambda v: v.astype(F32)[None]
    return (conv_w.astype(F32), row(conv_b), block_diag(w_a), row(b_a), block_diag(w_x), row(b_x), row(lam),
            block_diag(w_pool), row(scale))


def mixer_seq(za, batch, seq, packed, conv0, h0, pool0, start_pos):
    tt = min(MIX_TT, seq)
    nt = seq // tt
    conv_pad = jnp.pad(conv0.astype(F32), ((0, 0), (CONV_HALO - (CONV_W - 1), 0), (0, 0)))
    pool_pad = jnp.pad(pool0.astype(F32), ((0, 0), (1, 0), (0, 0)))
    h_pad = jnp.broadcast_to(h0.astype(F32)[:, None, :], (batch, SUBLANES, D_RNN))
    col = lambda c: pl.BlockSpec((tt, D_RNN), lambda b, i: (b * nt + i, c))
    state = lambda rows: pl.BlockSpec((1, rows, D_RNN), lambda b, i: (b, 0, 0))
    full = lambda a: pl.BlockSpec(a.shape, lambda b, i: (0,) * a.ndim)
    out_rows = pl.BlockSpec((tt, D_RNN), lambda b, i: (b * nt + i, 0))
    return pl.pallas_call(
        functools.partial(_mixer_seq_kernel, start_pos=start_pos),
        grid=(batch, nt),
        in_specs=[col(ZA_XRG // D_RNN), col(ZA_GRG // D_RNN), col(ZA_XPOOL // D_RNN), state(CONV_HALO), state(SUBLANES),
                  state(POOL_MAX)] + [full(a) for a in packed],
        out_specs=[out_rows, out_rows, state(CONV_W - 1), state(1), state(POOL_MAX - 1)],
        out_shape=[jax.ShapeDtypeStruct((batch * seq, D_RNN), BF16), jax.ShapeDtypeStruct((batch * seq, D_POOL), BF16),
                   jax.ShapeDtypeStruct((batch, CONV_W - 1, D_RNN), F32), jax.ShapeDtypeStruct((batch, 1, D_RNN), F32),
                   jax.ShapeDtypeStruct((batch, POOL_MAX - 1, D_POOL), F32)],
        scratch_shapes=[pltpu.VMEM((CONV_HALO + tt, D_RNN), F32), pltpu.VMEM((POOL_MAX + tt, D_POOL), F32),
                        pltpu.VMEM((tt, D_RNN), F32), pltpu.VMEM((tt, D_RNN), F32), pltpu.VMEM((tt, D_RNN), F32),
                        pltpu.VMEM((SUBLANES, D_RNN), F32)],
        compiler_params=pltpu.CompilerParams(dimension_semantics=("parallel", "arbitrary"),
                                             vmem_limit_bytes=VMEM_LIMIT_BYTES),
        name="mixer_seq",
    )(za, za, za, conv_pad, h_pad, pool_pad, *packed)


def _mixer_step_kernel(xrg_ref, grg_ref, xpool_ref, conv_ref, h0_ref, pool_ref, cw_ref, cb_ref, wa_ref, ba_ref,
                       wx_ref, bx_ref, lam_ref, wp_ref, ps_ref, ya_ref, yc_ref, hn_ref, *, start_pos):
    x = xrg_ref[...]
    xc = cb_ref[...] + cw_ref[CONV_W - 1:CONV_W, :] * x + sum(cw_ref[k:k + 1, :] * conv_ref[k] for k in range(CONV_W - 1))
    a, u = _rglru_coeffs(xc, wa_ref, ba_ref, wx_ref, bx_ref, lam_ref)
    h = a * h0_ref[...] + u
    hn_ref[...] = h
    ya_ref[...] = (h * jax.nn.gelu(grg_ref[...])).astype(ya_ref.dtype)
    xp = xpool_ref[...]
    parts = []
    for g, w in enumerate(POOL_WINDOWS):
        lanes = slice(g * POOL_GW, (g + 1) * POOL_GW)
        s = xp[:, lanes] + sum(pool_ref[POOL_MAX - 1 - k][:, lanes] for k in range(1, w))
        parts.append(s / float(min(start_pos + 1, w)) - xp[:, lanes])
    yc_ref[...] = _pool_project(jnp.concatenate(parts, axis=1), wp_ref, ps_ref).astype(yc_ref.dtype)


def mixer_step(za, packed, conv_state, h0, pool_state, start_pos):
    batch = za.shape[0]
    conv_t = jnp.swapaxes(conv_state.astype(F32), 0, 1)
    pool_t = jnp.swapaxes(pool_state.astype(F32), 0, 1)
    col = lambda c: pl.BlockSpec((batch, D_RNN), lambda i: (0, c))
    full = lambda a: pl.BlockSpec(a.shape, lambda i: (0,) * a.ndim)
    rows = pl.BlockSpec((batch, D_RNN), lambda i: (0, 0))
    return pl.pallas_call(
        functools.partial(_mixer_step_kernel, start_pos=start_pos),
        grid=(1,),
        in_specs=[col(ZA_XRG // D_RNN), col(ZA_GRG // D_RNN), col(ZA_XPOOL // D_RNN), full(conv_t), rows, full(pool_t)]
        + [full(a) for a in packed],
        out_specs=[rows, rows, rows],
        out_shape=[jax.ShapeDtypeStruct((batch, D_RNN), BF16), jax.ShapeDtypeStruct((batch, D_POOL), BF16),
                   jax.ShapeDtypeStruct((batch, D_RNN), F32)],
        compiler_params=pltpu.CompilerParams(vmem_limit_bytes=VMEM_LIMIT_BYTES),
        name="mixer_step",
    )(za, za, za, conv_t, h0.astype(F32), pool_t, *packed)


QB = Q_BLOCK
ROWS = HPG * QB
N_WIN_TILES = WINDOW // QB + 1
N_SEL_BIAS = REL_MAX_DIST // QB + 2
SEL_SPAN = 4
KV_CHUNK = 512


def _nsa_prompt_kernel(q_ref, ck_ref, cv_ref, kvs_ref, kvw_ref, gate_ref, bc_ref, tbs_ref, tbw_ref,
                       ovl_ref, eye_ref, pq_ref, pk_ref, pv_ref, onehot_ref, o_ref,
                       acc_ref, m_ref, qa_ref, comb_ref, ks_ref, vs_ref, kw_ref, vw_ref, cnt_ref, *, n_cmp, n_blk):
    qb = pl.program_id(2)
    t0 = qb * QB
    ncp = ck_ref.shape[2]
    seq = kvs_ref.shape[1]
    lane_row = lax.broadcasted_iota(jnp.int32, (1, LANES), 1)

    @pl.when(qb == 0)
    def _():
        ones_hi = jnp.where(lane_row >= HEAD_DIM, 1.0, 0.0)
        kw_ref[0:WINDOW, :] = jnp.broadcast_to(jnp.where(lane_row == HEAD_DIM, 1.0, 0.0), (WINDOW, LANES)).astype(BF16)
        vw_ref[0:WINDOW, :] = jnp.zeros((WINDOW, LANES), BF16)

        def stage(c, carry):
            r = pl.multiple_of(c * KV_CHUNK, KV_CHUNK)
            sel = kvs_ref[0, pl.ds(r, KV_CHUNK), :].astype(BF16)
            win = kvw_ref[0, pl.ds(r, KV_CHUNK), :].astype(BF16)
            ks_ref[pl.ds(r, KV_CHUNK), :] = (jnp.dot(sel, pk_ref[0], preferred_element_type=F32)
                                             + onehot_ref[pl.ds(r, KV_CHUNK), :].astype(F32)).astype(BF16)
            vs_ref[pl.ds(r, KV_CHUNK), :] = (jnp.dot(sel, pv_ref[0], preferred_element_type=F32) + ones_hi).astype(BF16)
            kw_ref[pl.ds(WINDOW + r, KV_CHUNK), :] = jnp.dot(win, pk_ref[0], preferred_element_type=F32).astype(BF16)
            vw_ref[pl.ds(WINDOW + r, KV_CHUNK), :] = (jnp.dot(win, pv_ref[0], preferred_element_type=F32)
                                                      + ones_hi).astype(BF16)
            return carry

        lax.fori_loop(0, seq // KV_CHUNK, stage, 0)

    q4 = jnp.dot(q_ref[0], pq_ref[...], preferred_element_type=F32)
    q3 = jnp.concatenate([q4[:, j * LANES:(j + 1) * LANES] for j in range(HPG)], axis=0)
    q = q3.astype(BF16)
    sig = jax.nn.sigmoid(gate_ref[0])
    gates = [jnp.concatenate([sig[:, 3 * j + c:3 * j + c + 1] for j in range(HPG)], axis=0) for c in range(3)]

    lc = _nt(q, ck_ref[0, 0]) + bc_ref[0].reshape(ROWS, ncp)
    tok = t0 + lax.broadcasted_iota(jnp.int32, (HPG, QB, ncp), 1).reshape(ROWS, ncp)
    col = lax.broadcasted_iota(jnp.int32, (ROWS, ncp), 1)
    valid = (tok >= col * CMP_STRIDE + (L_CMP - 1)) & (col < n_cmp)
    lc = jnp.where(valid, lc, NEG)
    mx = jnp.max(lc, axis=1, keepdims=True)
    p = jnp.where(valid, jnp.exp(lc - mx), 0.0)
    pc = p / jnp.maximum(jnp.sum(p, axis=1, keepdims=True), 1e-30)
    comb_ref[...] = gates[0] * jnp.dot(pc.astype(BF16), cv_ref[0, 0], preferred_element_type=F32)

    pcs = pc[0:QB] + pc[QB:2 * QB] + pc[2 * QB:3 * QB] + pc[3 * QB:4 * QB]
    hi = pcs.astype(BF16)
    lo = (pcs - hi.astype(F32)).astype(BF16)
    imp = _nt(ovl_ref[...], hi) + _nt(ovl_ref[...], lo)
    blk = lax.broadcasted_iota(jnp.int32, (n_blk, QB), 0)
    cur = (t0 + lax.broadcasted_iota(jnp.int32, (n_blk, QB), 1)) // SEL_BLOCK
    forced = (blk == 0) | (blk == cur) | (blk == cur - 1)
    score = jnp.where(blk > cur, -1.0, jnp.where(forced, FORCE, imp))
    chunks = [score[r:r + SUBLANES] for r in range(0, n_blk, SUBLANES)]
    sub = lax.broadcasted_iota(jnp.int32, (SUBLANES, QB), 0)
    cnt_ref[...] = jnp.zeros((n_blk, QB), F32)
    last_blk = (t0 + QB - 1) // SEL_BLOCK
    for mc in range(0, n_blk, SUBLANES):
        @pl.when(mc <= last_blk)
        def _(mc=mc):
            for r, ch in enumerate(chunks):
                first = r * SUBLANES
                part = jnp.zeros((SUBLANES, QB), F32)
                for m in range(mc, mc + SUBLANES):
                    row = jnp.broadcast_to(score[m:m + 1, :], (SUBLANES, QB))
                    if first > m:
                        beats = jnp.where(row >= ch, 1.0, 0.0)
                    elif first + SUBLANES - 1 < m:
                        beats = jnp.where(row > ch, 1.0, 0.0)
                    else:
                        beats = jnp.where(sub + first > m, jnp.where(row >= ch, 1.0, 0.0), jnp.where(row > ch, 1.0, 0.0))
                    part = part + beats
                cnt_ref[first:first + SUBLANES, :] += part
    sel_neg = jnp.where(cnt_ref[...] < float(min(N_SELECT, n_blk)), 0.0, NEG)
    pieces = [jnp.zeros((HEAD_DIM, QB), F32), sel_neg]
    if n_blk < HEAD_DIM:
        pieces.append(jnp.zeros((HEAD_DIM - n_blk, QB), F32))
    placed_t = jnp.concatenate(pieces, axis=0).astype(BF16)
    placed = _nt(eye_ref[...], placed_t)
    qa_ref[...] = (q3 + jnp.concatenate([placed] * HPG, axis=0)).astype(BF16)

    m_ref[...] = jnp.full((ROWS, LANES), M_INIT, F32)
    acc_ref[...] = jnp.zeros((ROWS, LANES), F32)
    n_bias = tbs_ref.shape[1]

    span = SEL_SPAN * QB
    n_spans = qb // SEL_SPAN + 1

    def body(kk, carry):
        off = pl.multiple_of(kk * span, span)
        s = _nt(qa_ref[...], ks_ref[pl.ds(off, span), :])
        parts = []
        for u in range(SEL_SPAN):
            idx = jnp.clip(qb - (kk * SEL_SPAN + u), -1, n_bias - 2) + 1
            parts.append(s[:, u * QB:(u + 1) * QB] + tbs_ref[:, pl.ds(idx, 1)].reshape(ROWS, LANES))
        tile_max = functools.reduce(jnp.maximum, parts)
        m_old = m_ref[...]
        m_new = jnp.maximum(m_old, jnp.max(tile_max, axis=1, keepdims=True))
        alpha = jnp.exp(m_old - m_new)
        pr = jnp.concatenate([jnp.exp(x - m_new).astype(BF16) for x in parts], axis=1)
        acc_ref[...] = alpha * acc_ref[...] + jnp.dot(pr, vs_ref[pl.ds(off, SEL_SPAN * QB), :],
                                                      preferred_element_type=F32)
        m_ref[...] = m_new
        return carry

    lax.fori_loop(0, n_spans, body, 0)
    acc = acc_ref[...]
    comb_ref[...] += gates[1] * (acc / pltpu.roll(acc, HEAD_DIM, axis=1))

    qw = (q3 + jnp.where(lane_row == HEAD_DIM, NEG, 0.0)).astype(BF16)
    w_off = pl.multiple_of(t0, QB)
    sw = _nt(qw, kw_ref[pl.ds(w_off, WINDOW + QB), :]) + tbw_ref[...].reshape(ROWS, WINDOW + QB)
    pw = jnp.exp(sw - jnp.max(sw, axis=1, keepdims=True))
    accw = jnp.dot(pw.astype(BF16), vw_ref[pl.ds(w_off, WINDOW + QB), :], preferred_element_type=F32)
    comb_ref[...] += gates[2] * (accw / pltpu.roll(accw, HEAD_DIM, axis=1))

    comb = comb_ref[...]
    lane = lax.broadcasted_iota(jnp.int32, (QB, LANES), 1)
    for half in range(HPG // 2):
        a = comb[(2 * half) * QB:(2 * half + 1) * QB]
        b = comb[(2 * half + 1) * QB:(2 * half + 2) * QB]
        o_ref[0, :, half * LANES:(half + 1) * LANES] = jnp.where(lane < HEAD_DIM, a,
                                                                 pltpu.roll(b, HEAD_DIM, axis=1)).astype(o_ref.dtype)


def _bias_by_distance(rel_bias):
    max_d = (N_SEL_BIAS + 1) * QB
    return jnp.transpose(rel_bias.astype(F32)[rel_bucket(jnp.arange(max_d))])


def _ext(tab_t, lo, hi, ok_lo=None, ok_hi=None):
    n_heads, depth = tab_t.shape
    y = np.arange(lo, hi)
    parts = [jnp.broadcast_to(tab_t[:, :1], (n_heads, int(np.sum(y < 0)))), tab_t[:, max(lo, 0):max(min(hi, depth), 0)],
             jnp.broadcast_to(tab_t[:, -1:], (n_heads, int(np.sum(y > depth - 1))))]
    arr = jnp.concatenate(parts, axis=1)
    ok = np.ones(y.shape, bool)
    if ok_lo is not None:
        ok &= y >= ok_lo
    if ok_hi is not None:
        ok &= y <= ok_hi
    return jnp.where(jnp.asarray(ok)[None], arr, NEG)


def _toeplitz_tiles(ext, lo, ks):
    w = jnp.stack([ext[:, QB * k - (QB - 1) - lo:QB * k - (QB - 1) - lo + 2 * QB] for k in ks], axis=1)
    skew = jnp.tile(w, (1, 1, QB + 1))[..., :QB * (2 * QB + 1)].reshape(w.shape[:2] + (QB, 2 * QB + 1))[..., :QB]
    return skew[..., ::-1]


def rel_bias_tables(rel_bias, seq):
    tab_t = _bias_by_distance(rel_bias)
    max_d = tab_t.shape[1]
    lo = -(2 * QB - 1)
    tbs = _toeplitz_tiles(_ext(tab_t, lo, max_d, 0, None), lo, range(-1, N_SEL_BIAS))
    tbw = _toeplitz_tiles(_ext(tab_t, lo, max_d, 0, WINDOW), lo, range(N_WIN_TILES - 1, -1, -1))
    tbw = jnp.transpose(tbw, (0, 2, 1, 3)).reshape(N_HEADS, QB, N_WIN_TILES * QB)
    nqb = seq // QB
    ncp = seq // CMP_STRIDE
    per_qb = QB // CMP_STRIDE
    width = ncp + per_qb * (nqb - 1)
    c0 = QB * (nqb - 1) - (L_CMP - 1)
    ext = _ext(tab_t, c0 - CMP_STRIDE * (width - 1), c0 + QB)
    n_z = width + per_qb - 1
    f = ext[:, :CMP_STRIDE * n_z].reshape(N_HEADS, n_z, CMP_STRIDE)[:, ::-1]
    v = jnp.concatenate([jnp.swapaxes(f[:, per_qb - 1 - a:per_qb - 1 - a + width], 1, 2) for a in range(per_qb)],
                        axis=1)
    bc = jnp.stack([v[:, :, per_qb * (nqb - 1 - b):per_qb * (nqb - 1 - b) + ncp] for b in range(nqb)])
    return tbs, tbw, bc


def nsa_prompt_pallas(q, ck, cv, kv_sel, kv_win, gn, tables):
    B, S = q.shape[:2]
    assert S % (SEL_SPAN * QB) == 0 and S % KV_CHUNK == 0 and S // SEL_BLOCK <= HEAD_DIM
    tbs, tbw, bc = tables
    nqb = S // QB
    ncp = S // CMP_STRIDE
    n_cmp = ncp - 1
    n_blk = S // SEL_BLOCK
    c0 = np.arange(ncp)[None, :] * CMP_STRIDE
    s0 = np.arange(n_blk)[:, None] * SEL_BLOCK
    ovl_t = ((c0 <= s0 + SEL_BLOCK - 1) & (c0 + L_CMP - 1 >= s0) & (np.arange(ncp)[None, :] < n_cmp))
    ovl_t = jnp.asarray(ovl_t.astype(np.float32), BF16)
    eye = jnp.asarray(np.eye(QB, dtype=np.float32), BF16)
    gw = HPG * HEAD_DIM
    pq = np.zeros((gw, HPG * LANES), np.float32)
    pq[np.arange(gw), (np.arange(gw) // HEAD_DIM) * LANES + np.arange(gw) % HEAD_DIM] = 1.0
    pk = np.zeros((KV_HEADS, KV_ROW, LANES), np.float32)
    pv = np.zeros((KV_HEADS, KV_ROW, LANES), np.float32)
    for g in range(KV_HEADS):
        pk[g, g * HEAD_DIM + np.arange(HEAD_DIM), np.arange(HEAD_DIM)] = 1.0
        pv[g, (KV_HEADS + g) * HEAD_DIM + np.arange(HEAD_DIM), np.arange(HEAD_DIM)] = 1.0
    onehot = np.zeros((S, LANES), np.float32)
    onehot[np.arange(S), HEAD_DIM + np.arange(S) // SEL_BLOCK] = 1.0

    kv_spec = pl.BlockSpec((1, S, KV_ROW), lambda b, g, i: (b, 0, 0))
    cmp_spec = pl.BlockSpec((1, 1, ncp, LANES), lambda b, g, i: (b, g, 0, 0))
    const2 = lambda b, g, i: (0, 0)
    return pl.pallas_call(
        functools.partial(_nsa_prompt_kernel, n_cmp=n_cmp, n_blk=n_blk),
        grid=(B, KV_HEADS, nqb),
        in_specs=[
            pl.BlockSpec((1, QB, gw), lambda b, g, i: (b, i, g)),
            cmp_spec, cmp_spec, kv_spec, kv_spec,
            pl.BlockSpec((1, QB, LANES), lambda b, g, i: (b, i, g)),
            pl.BlockSpec((1, HPG, QB, ncp), lambda b, g, i: (i, g, 0, 0)),
            pl.BlockSpec((HPG, N_SEL_BIAS + 1, QB, QB), lambda b, g, i: (g, 0, 0, 0)),
            pl.BlockSpec((HPG, QB, N_WIN_TILES * QB), lambda b, g, i: (g, 0, 0)),
            pl.BlockSpec((n_blk, ncp), const2),
            pl.BlockSpec((QB, QB), const2),
            pl.BlockSpec((gw, HPG * LANES), const2),
            pl.BlockSpec((1, KV_ROW, LANES), lambda b, g, i: (g, 0, 0)),
            pl.BlockSpec((1, KV_ROW, LANES), lambda b, g, i: (g, 0, 0)),
            pl.BlockSpec((S, LANES), const2),
        ],
        out_specs=pl.BlockSpec((1, QB, gw), lambda b, g, i: (b, i, g)),
        out_shape=jax.ShapeDtypeStruct((B, S, N_HEADS * HEAD_DIM), BF16),
        scratch_shapes=[pltpu.VMEM((ROWS, LANES), F32), pltpu.VMEM((ROWS, LANES), F32),
                        pltpu.VMEM((ROWS, LANES), BF16), pltpu.VMEM((ROWS, LANES), F32),
                        pltpu.VMEM((S, LANES), BF16), pltpu.VMEM((S, LANES), BF16),
                        pltpu.VMEM((S + WINDOW, LANES), BF16), pltpu.VMEM((S + WINDOW, LANES), BF16),
                        pltpu.VMEM((n_blk, QB), F32)],
        compiler_params=pltpu.CompilerParams(dimension_semantics=("parallel", "parallel", "arbitrary"),
                                             vmem_limit_bytes=VMEM_LIMIT_BYTES),
        name="nsa_prompt",
    )(q, ck, cv, kv_sel, kv_win, gn, bc, tbs, tbw, ovl_t, eye, jnp.asarray(pq, BF16), jnp.asarray(pk, BF16),
      jnp.asarray(pv, BF16), jnp.asarray(onehot, BF16))


STEP_ROWS = SUBLANES
RANK_LANES = 2 * LANES


def _split3(x):
    hi = x.astype(BF16)
    r1 = x - hi.astype(F32)
    mid = r1.astype(BF16)
    return hi, mid, (r1 - mid.astype(F32)).astype(BF16)


def _nsa_step_kernel(*refs, n_pages, n_cmp):
    pages = refs[1:1 + n_pages]
    (q_ref, ck_ref, cv_ref, kvs_ref, kvw_ref, win_ref, gn_ref, bcs_ref, bsel_ref, bwin_ref, ovl_ref, oh_ref,
     pq_ref, pk_ref, pv_ref, o_ref, kc_ref, vs_ref, kw_ref, vw_ref) = refs[1 + n_pages:]
    page = pages[0].shape[-1]
    past = n_pages * page
    wb = win_ref.shape[-1]
    ncp = ck_ref.shape[2]
    first_row = lax.broadcasted_iota(jnp.int32, (LANES, 1), 0) == 0

    @pl.when(pl.program_id(0) == 0)
    def _():
        for g in range(KV_HEADS):
            kc_ref[g, HEAD_DIM:LANES, :] = jnp.zeros((LANES - HEAD_DIM, past + LANES), BF16)
            kc_ref[g, LANES:2 * LANES, 0:past] = oh_ref[...]
            kc_ref[g, LANES:2 * LANES, past:past + LANES] = jnp.zeros((LANES, LANES), BF16)
            vs_ref[g, HEAD_DIM:LANES, :] = jnp.ones((LANES - HEAD_DIM, past + LANES), BF16)
            kw_ref[g, HEAD_DIM:LANES, :] = jnp.zeros((LANES - HEAD_DIM, wb + LANES), BF16)
            vw_ref[g, HEAD_DIM:LANES, :] = jnp.ones((LANES - HEAD_DIM, wb + LANES), BF16)

    def place_new(row_ref, c0, k_dst, v_dst):
        new = jnp.where(first_row, row_ref[0], 0.0).astype(BF16)
        for g in range(KV_HEADS):
            k_dst[g, 0:HEAD_DIM, c0:c0 + LANES] = _nt(pk_ref[g], new)[0:HEAD_DIM].astype(BF16)
            v_dst[g, 0:HEAD_DIM, c0:c0 + LANES] = _nt(pv_ref[g], new)[0:HEAD_DIM].astype(BF16)

    for k, pg in enumerate(pages):
        for g in range(KV_HEADS):
            kc_ref[g, 0:HEAD_DIM, k * page:(k + 1) * page] = pg[0, 0, 0, g].astype(BF16)
            vs_ref[g, 0:HEAD_DIM, k * page:(k + 1) * page] = pg[0, 0, 1, g].astype(BF16)
    place_new(kvs_ref, past, kc_ref, vs_ref)
    for g in range(KV_HEADS):
        kw_ref[g, 0:HEAD_DIM, 0:wb] = win_ref[0, 0, 0, g].astype(BF16)
        vw_ref[g, 0:HEAD_DIM, 0:wb] = win_ref[0, 0, 1, g].astype(BF16)
    place_new(kvw_ref, wb, kw_ref, vw_ref)

    sig = jax.nn.sigmoid(gn_ref[0])
    lane = lax.broadcasted_iota(jnp.int32, (1, LANES), 1)
    rr = lax.broadcasted_iota(jnp.int32, (RANK_LANES, RANK_LANES), 0)
    cc = lax.broadcasted_iota(jnp.int32, (RANK_LANES, RANK_LANES), 1)
    ones_sq = jnp.ones((RANK_LANES, RANK_LANES), BF16)
    pad_rows = jnp.zeros((STEP_ROWS - HPG, LANES), F32)
    for g in range(KV_HEADS):
        gw = HPG * HEAD_DIM
        q4 = jnp.dot(q_ref[0][:, g * gw:(g + 1) * gw], pq_ref[...], preferred_element_type=F32)
        q8 = jnp.concatenate([q4[:, j * LANES:(j + 1) * LANES] for j in range(HPG)] + [pad_rows], axis=0)
        gates = [jnp.concatenate([sig[:, g * LANES + 3 * j + c:g * LANES + 3 * j + c + 1] for j in range(HPG)]
                                 + [pad_rows[:, 0:1]], axis=0) for c in range(3)]

        lc = _nt(q8.astype(BF16), ck_ref[0, g]) + bcs_ref[g]
        col = lax.broadcasted_iota(jnp.int32, (STEP_ROWS, ncp), 1)
        valid = col < n_cmp
        lc = jnp.where(valid, lc, NEG)
        p = jnp.where(valid, jnp.exp(lc - jnp.max(lc, axis=1, keepdims=True)), 0.0)
        pc = p / jnp.maximum(jnp.sum(p, axis=1, keepdims=True), 1e-30)
        comb = gates[0] * jnp.dot(pc.astype(BF16), cv_ref[0, g], preferred_element_type=F32)

        pcs = jnp.broadcast_to(jnp.sum(pc[0:HPG], axis=0, keepdims=True), (STEP_ROWS, ncp))
        hi = pcs.astype(BF16)
        lo = (pcs - hi.astype(F32)).astype(BF16)
        imp = (jnp.dot(hi, ovl_ref[...], preferred_element_type=F32)
               + jnp.dot(lo, ovl_ref[...], preferred_element_type=F32))[0:1]
        blk_id = lax.broadcasted_iota(jnp.int32, (1, RANK_LANES), 1)
        cur = past // SEL_BLOCK
        forced = (blk_id == 0) | (blk_id == cur) | (blk_id == cur - 1)
        score = jnp.where(blk_id > cur, -1.0, jnp.where(forced, FORCE, imp))
        s_n = jnp.broadcast_to(score, (RANK_LANES, RANK_LANES))
        diag = jnp.where(rr == cc, s_n, 0.0)
        s_m = sum(_nt(part, ones_sq) for part in _split3(diag))
        beats = jnp.where(rr < cc, jnp.where(s_m >= s_n, 1.0, 0.0), jnp.where(s_m > s_n, 1.0, 0.0))
        rank = jnp.sum(beats, axis=0, keepdims=True)
        sel_neg = jnp.where(rank < float(N_SELECT), 0.0, NEG)

        qa = jnp.concatenate([q8, jnp.broadcast_to(sel_neg[:, :LANES], (STEP_ROWS, LANES))], axis=1).astype(BF16)
        s_past = jnp.dot(qa, kc_ref[g, :, 0:past], preferred_element_type=F32) + bsel_ref[g, :, 0:past]
        s_new = (jnp.dot(qa, kc_ref[g, :, past:past + LANES], preferred_element_type=F32)
                 + bsel_ref[g, :, past:past + LANES] + sel_neg[:, LANES:LANES + 1])
        mx = jnp.maximum(jnp.max(s_past, axis=1, keepdims=True), jnp.max(s_new, axis=1, keepdims=True))
        acc = (_nt(jnp.exp(s_past - mx).astype(BF16), vs_ref[g, :, 0:past])
               + _nt(jnp.exp(s_new - mx).astype(BF16), vs_ref[g, :, past:past + LANES]))
        comb = comb + gates[1] * (acc / pltpu.roll(acc, HEAD_DIM, axis=1))

        sw = jnp.dot(q8.astype(BF16), kw_ref[g], preferred_element_type=F32) + bwin_ref[g]
        pw = jnp.exp(sw - jnp.max(sw, axis=1, keepdims=True))
        accw = _nt(pw.astype(BF16), vw_ref[g])
        comb = comb + gates[2] * (accw / pltpu.roll(accw, HEAD_DIM, axis=1))

        for half in range(HPG // 2):
            a = comb[2 * half:2 * half + 1]
            b = comb[2 * half + 1:2 * half + 2]
            o_ref[0, :, g * gw + half * LANES:g * gw + (half + 1) * LANES] = jnp.where(
                lane < HEAD_DIM, a, pltpu.roll(b, HEAD_DIM, axis=1)).astype(o_ref.dtype)


def step_bias_tables(rel_bias, past, wb, ncp):
    tab_t = _bias_by_distance(rel_bias)

    def rows(t):
        return jnp.pad(t.reshape(KV_HEADS, HPG, -1), ((0, 0), (0, STEP_ROWS - HPG), (0, 0)))

    d_first = past - (L_CMP - 1)
    cmp_t = _ext(tab_t, d_first - CMP_STRIDE * (ncp - 1), d_first + 1, 0, None)[:, ::CMP_STRIDE][:, ::-1]
    sel_t = _ext(tab_t, -(LANES - 1), past + 1, 0, None)[:, ::-1]
    win_t = _ext(tab_t, -(LANES - 1), wb + 1, 0, WINDOW)[:, ::-1]
    return rows(cmp_t), rows(sel_t), rows(win_t)


def nsa_step(q, ck, cv, kvs_new, kvw_new, sel_pool, win_buf, gn, page_table, tables, layer):
    B, n_pages = page_table.shape
    page = sel_pool.shape[-1]
    past = n_pages * page
    wb = win_buf.shape[-1]
    ncp = ck.shape[2]
    n_blk = past // SEL_BLOCK + 1
    assert n_blk <= LANES + 1 and page == LANES and wb % LANES == 0
    bcs, bsel, bwin = tables
    c0 = np.arange(ncp)[:, None] * CMP_STRIDE
    s0 = np.arange(RANK_LANES)[None, :] * SEL_BLOCK
    ovl = ((c0 <= s0 + SEL_BLOCK - 1) & (c0 + L_CMP - 1 >= s0) & (np.arange(ncp)[:, None] < ncp - 1)
           & (np.arange(RANK_LANES)[None, :] < n_blk))
    onehot = np.zeros((LANES, past), np.float32)
    onehot[np.arange(past) // SEL_BLOCK, np.arange(past)] = 1.0
    gw = HPG * HEAD_DIM
    pq = np.zeros((gw, HPG * LANES), np.float32)
    pq[np.arange(gw), (np.arange(gw) // HEAD_DIM) * LANES + np.arange(gw) % HEAD_DIM] = 1.0
    pk = np.zeros((KV_HEADS, LANES, KV_ROW), np.float32)
    pv = np.zeros((KV_HEADS, LANES, KV_ROW), np.float32)
    for g in range(KV_HEADS):
        pk[g, np.arange(HEAD_DIM), g * HEAD_DIM + np.arange(HEAD_DIM)] = 1.0
        pv[g, np.arange(HEAD_DIM), (KV_HEADS + g) * HEAD_DIM + np.arange(HEAD_DIM)] = 1.0
    consts = [jnp.asarray(ovl.astype(np.float32), BF16), jnp.asarray(onehot, BF16), jnp.asarray(pq, BF16),
              jnp.asarray(pk, BF16), jnp.asarray(pv, BF16)]

    row3 = lambda a: a.reshape(B, 1, a.shape[-1])
    per_b = lambda shape: pl.BlockSpec((1,) + shape, lambda b, pt: (b,) + (0,) * len(shape))
    full = lambda a: pl.BlockSpec(a.shape, lambda b, pt: (0,) * a.ndim)
    slab = (2, KV_HEADS, HEAD_DIM)
    page_specs = [pl.BlockSpec((1, 1) + slab + (page,), lambda b, pt, k=k: (layer, pt[b, k], 0, 0, 0, 0))
                  for k in range(n_pages)]
    win_spec = pl.BlockSpec((1, 1) + slab + (wb,), lambda b, pt: (layer, b, 0, 0, 0, 0))
    in_specs = page_specs + [per_b((1, Q_COLS)), per_b((KV_HEADS, ncp, LANES)), per_b((KV_HEADS, ncp, LANES)),
                             per_b((1, KV_ROW)), per_b((1, KV_ROW)), win_spec, per_b((1, GN_COLS)),
                             full(bcs), full(bsel), full(bwin)] + [full(c) for c in consts]
    grid_spec = pltpu.PrefetchScalarGridSpec(
        num_scalar_prefetch=1, grid=(B,), in_specs=in_specs, out_specs=per_b((1, Q_COLS)),
        scratch_shapes=[pltpu.VMEM((KV_HEADS, 2 * LANES, past + LANES), BF16), pltpu.VMEM((KV_HEADS, LANES, past + LANES), BF16),
                        pltpu.VMEM((KV_HEADS, LANES, wb + LANES), BF16), pltpu.VMEM((KV_HEADS, LANES, wb + LANES), BF16)])
    out = pl.pallas_call(
        functools.partial(_nsa_step_kernel, n_pages=n_pages, n_cmp=ncp - 1),
        grid_spec=grid_spec,
        out_shape=jax.ShapeDtypeStruct((B, 1, Q_COLS), BF16),
        compiler_params=pltpu.CompilerParams(dimension_semantics=("arbitrary",), vmem_limit_bytes=VMEM_LIMIT_BYTES),
        name="nsa_step",
    )(page_table, *([sel_pool] * n_pages), row3(q), ck, cv, row3(kvs_new), row3(kvw_new), win_buf, row3(gn),
      bcs, bsel, bwin, *consts)
    return out.reshape(B, Q_COLS)


def _merge_kernel(ya_ref, yb_ref, yc_ref, ga_ref, gb_ref, gc_ref, x_ref, wa_ref, wb_ref, wc_ref, wo_ref, o_ref):
    def branch(y_ref, g_ref, w_ref):
        return jax.nn.sigmoid(g_ref[...]) * jnp.dot(y_ref[...], w_ref[...], preferred_element_type=F32)

    merged = branch(ya_ref, ga_ref, wa_ref) + branch(yb_ref, gb_ref, wb_ref) + branch(yc_ref, gc_ref, wc_ref)
    o_ref[...] = x_ref[...] + jnp.dot(merged.astype(BF16), wo_ref[...], preferred_element_type=F32)


def merge(ya, yb, yc, za, x, wa, wb, wc, wo):
    T = x.shape[0]
    tm = _pick(T, (512, 256, 128))
    y_spec = pl.BlockSpec((tm, ya.shape[1]), lambda i: (i, 0))
    w_spec = pl.BlockSpec((ya.shape[1], D_MODEL), lambda i: (0, 0))
    return pl.pallas_call(
        _merge_kernel,
        grid=(T // tm,),
        in_specs=[y_spec, y_spec, y_spec,
                  pl.BlockSpec((tm, D_MODEL), lambda i: (i, 0)), pl.BlockSpec((tm, D_MODEL), lambda i: (i, 1)),
                  pl.BlockSpec((tm, D_MODEL), lambda i: (i, 2)),
                  pl.BlockSpec((tm, D_MODEL), lambda i: (i, 0)),
                  w_spec, w_spec, w_spec, pl.BlockSpec((D_MODEL, D_MODEL), lambda i: (0, 0))],
        out_specs=pl.BlockSpec((tm, D_MODEL), lambda i: (i, 0)),
        out_shape=jax.ShapeDtypeStruct((T, D_MODEL), F32),
        compiler_params=pltpu.CompilerParams(dimension_semantics=("parallel",), vmem_limit_bytes=VMEM_LIMIT_BYTES),
        name="merge",
    )(ya, yb, yc, za, za, za, x, wa, wb, wc, wo)


def _ffn_kernel(*refs, routed):
    if routed:
        h_ref, g_ref, wr_ref, br_ref, wg_ref, wu_ref, wd_ref, o_ref, xn_ref, gate_ref = refs
    else:
        h_ref, g_ref, wg_ref, wu_ref, wd_ref, o_ref, xn_ref = refs
    e = pl.program_id(1)
    f = pl.program_id(2)

    @pl.when((e == 0) & (f == 0))
    def _():
        h = h_ref[...]
        xb = _rms_rows(h, g_ref[...]).astype(BF16)
        xn_ref[...] = xb
        o_ref[...] = h
        if routed:
            logits = jnp.dot(xb, wr_ref[...], preferred_element_type=F32) + br_ref[...]
            lane = lax.broadcasted_iota(jnp.int32, logits.shape, 1)
            m1 = jnp.max(logits, axis=1, keepdims=True)
            i1 = jnp.min(jnp.where(logits == m1, lane, LANES), axis=1, keepdims=True)
            rest = jnp.where(lane == i1, M_INIT, logits)
            m2 = jnp.max(rest, axis=1, keepdims=True)
            i2 = jnp.min(jnp.where(rest == m2, lane, LANES), axis=1, keepdims=True)
            r = jnp.exp(m2 - m1)
            gate_ref[...] = jnp.where(lane == i1, 1.0 / (1.0 + r), 0.0) + jnp.where(lane == i2, r / (1.0 + r), 0.0)

    xb = xn_ref[...]
    a = jnp.dot(xb, wg_ref[0], preferred_element_type=F32)
    u = jnp.dot(xb, wu_ref[0], preferred_element_type=F32)
    t = (a * jax.nn.sigmoid(a)) * u
    y = jnp.dot(t.astype(BF16), wd_ref[0], preferred_element_type=F32)
    if routed:
        lane = lax.broadcasted_iota(jnp.int32, gate_ref.shape, 1)
        y = jnp.sum(jnp.where(lane == e, gate_ref[...], 0.0), axis=1, keepdims=True) * y
    o_ref[...] += y


def channel_mixer(h, norm_gain, wg, wu, wd, router=None):
    T = h.shape[0]
    E, _, F = wg.shape
    tm = _pick(T, (512, 256, 128))
    tf = _pick(F, (1408, 1024, 512, 256, 128))
    routed = router is not None
    row = lambda i, e, f: (i, 0)
    in_specs = [pl.BlockSpec((tm, D_MODEL), row), pl.BlockSpec((1, D_MODEL), lambda i, e, f: (0, 0))]
    args = [h, norm_gain.astype(F32)[None]]
    scratch = [pltpu.VMEM((tm, D_MODEL), BF16)]
    if routed:
        in_specs += [pl.BlockSpec((D_MODEL, LANES), lambda i, e, f: (0, 0)), pl.BlockSpec((1, LANES), lambda i, e, f: (0, 0))]
        args += list(router)
        scratch.append(pltpu.VMEM((tm, LANES), F32))
    in_specs += [pl.BlockSpec((1, D_MODEL, tf), lambda i, e, f: (e, 0, f)),
                 pl.BlockSpec((1, D_MODEL, tf), lambda i, e, f: (e, 0, f)),
                 pl.BlockSpec((1, tf, D_MODEL), lambda i, e, f: (e, f, 0))]
    args += [wg, wu, wd]
    return pl.pallas_call(
        functools.partial(_ffn_kernel, routed=routed),
        grid=(T // tm, E, F // tf),
        in_specs=in_specs,
        out_specs=pl.BlockSpec((tm, D_MODEL), row),
        out_shape=jax.ShapeDtypeStruct((T, D_MODEL), F32),
        scratch_shapes=scratch,
        compiler_params=pltpu.CompilerParams(dimension_semantics=("parallel", "arbitrary", "arbitrary"),
                                             vmem_limit_bytes=VMEM_LIMIT_BYTES),
        name="moe" if routed else "ffn",
    )(*args)


def mixer_tail(x2, za, y_a, y_b, y_c, l, P):
    return merge(y_a, y_b, y_c, za, x2, P['w_br_rg'][l].astype(BF16), P['w_br_attn'][l].astype(BF16),
                 P['w_br_pool'][l].astype(BF16), P['w_out'][l].astype(BF16))


def prompt_mixer(x, l, P, packed, tables):
    B, T, _ = x.shape
    x2 = x.reshape(B * T, D_MODEL)
    za, q, kv_c, kv_s, kv_w, gn = projection(x2, P['attn_norm'][l], packed['proj'])
    zeros = lambda rows: jnp.zeros((B, rows, D_RNN), F32)
    y_a, y_c, conv_new, h_last, pool_new = mixer_seq(za, B, T, packed['mix'], zeros(CONV_W - 1), jnp.zeros((B, D_RNN), F32),
                                                     zeros(POOL_MAX - 1), 0)
    ck, cv = compress(kv_c.reshape(B, 2 * T, LANES), packed['cmp'])
    y_b = nsa_prompt_pallas(q.reshape(B, T, Q_COLS), ck, cv, kv_s.reshape(B, T, KV_ROW), kv_w.reshape(B, T, KV_ROW),
                            gn.reshape(B, T, GN_COLS), tables)
    out = mixer_tail(x2, za, y_a, y_b.reshape(B * T, Q_COLS), y_c, l, P)
    kv_shape = (B, T, 2, KV_HEADS, HEAD_DIM)
    state = (kv_c.reshape(kv_shape), kv_s.reshape(kv_shape), kv_w.reshape(kv_shape)[:, -min(WINDOW, T):], conv_new,
             h_last[:, 0], pool_new)
    return out.reshape(B, T, D_MODEL), state


def sample_mixer(x, l, P, packed, past_len, conv_state, h0, pool_state, caches, win_buf, page_table, step_tables):
    B, T, _ = x.shape
    x2 = x.reshape(B * T, D_MODEL)
    za, q, kv_c, kv_s, kv_w, gn = projection(x2, P['attn_norm'][l], packed['proj'])
    y_a, y_c, h_new = mixer_step(za, packed['mix'], conv_state, h0, pool_state, past_len)
    cmp_pools, sel_pools, win_bufs = caches
    ck, cv = compress(cmp_pools, packed['cmp'], page_table, layer=l)
    kv_shape = (B, T, 2, KV_HEADS, HEAD_DIM)
    wb = win_buf.shape[1]
    y_b = nsa_step(q, ck, cv, kv_s, kv_w, sel_pools, win_bufs, gn, page_table, step_tables, l)
    win_new = jnp.concatenate([win_buf[:, 1:], kv_w.reshape(kv_shape)], axis=1)
    out = mixer_tail(x2, za, y_a, y_b, y_c, l, P)
    conv_new = jnp.concatenate([conv_state[:, 1:], za[:, None, ZA_XRG:ZA_XRG + D_RNN]], axis=1)
    pool_new = jnp.concatenate([pool_state[:, 1:], za[:, None, ZA_XPOOL:ZA_XPOOL + D_POOL]], axis=1)
    state = (kv_c.reshape(kv_shape), kv_s.reshape(kv_shape), win_new, conv_new, h_new, pool_new)
    return out.reshape(B, T, D_MODEL), state


def ffn_layer(x, l, P, W):
    B, T, _ = x.shape
    i = l // 2
    if l % 2 == 0:
        y = channel_mixer(x.reshape(B * T, D_MODEL), P['ffn_norm'][l], W['ffn_g'][i], W['ffn_u'][i], W['ffn_d'][i])
    else:
        y = channel_mixer(x.reshape(B * T, D_MODEL), P['ffn_norm'][l], W['moe_g'][i], W['moe_u'][i], W['moe_d'][i],
                          router=W['router'][i])
    return y.reshape(B, T, D_MODEL)


def kernel(x_prompt, x_sample, cache_cmp_kv, cache_sel_kv, cache_win_kv, state_conv, state_rg_h, state_pool,
           page_table, attn_norm, w_in, conv_w, conv_b, rg_w_a, rg_b_a, rg_w_x, rg_b_x, rg_lambda, q_norm, k_norm,
           cmp_pe, w_cmp1, w_cmp2, rel_bias, w_pool, pool_scale, w_br_rg, w_br_attn, w_br_pool, w_out, ffn_norm,
           ffn_w_gate, ffn_w_up, ffn_w_down, w_router, b_router, moe_w_gate, moe_w_up, moe_w_down):
    P = dict(attn_norm=attn_norm, conv_w=conv_w, conv_b=conv_b, rg_w_a=rg_w_a, rg_b_a=rg_b_a,
             rg_w_x=rg_w_x, rg_b_x=rg_b_x, rg_lambda=rg_lambda, w_pool=w_pool,
             pool_scale=pool_scale, w_br_rg=w_br_rg, w_br_attn=w_br_attn, w_br_pool=w_br_pool, w_out=w_out,
             ffn_norm=ffn_norm)
    depth = w_in.shape[0]
    n_moe = w_router.shape[0]
    pad_e = LANES - N_EXPERTS
    W = dict(
        ffn_g=[w[None].astype(BF16) for w in ffn_w_gate], ffn_u=[w[None].astype(BF16) for w in ffn_w_up],
        ffn_d=[w[None].astype(BF16) for w in ffn_w_down],
        moe_g=[w.astype(BF16) for w in moe_w_gate], moe_u=[w.astype(BF16) for w in moe_w_up],
        moe_d=[w.astype(BF16) for w in moe_w_down],
        router=[(jnp.pad(w_router[i], ((0, 0), (0, pad_e))).astype(BF16),
                 jnp.pad(b_router[i].astype(F32), (0, pad_e), constant_values=NEG)[None]) for i in range(n_moe)])
    past_len = page_table.shape[1] * PAGE_SIZE
    y_p, y_s = x_prompt, x_sample
    tables = rel_bias_tables(rel_bias, x_prompt.shape[1])
    step_tables = step_bias_tables(rel_bias, past_len, cache_win_kv.shape[2], past_len // CMP_STRIDE)
    positions_minor = (0, 1, 3, 4, 5, 2)
    caches = tuple(jnp.transpose(c, positions_minor) for c in (cache_cmp_kv, cache_sel_kv, cache_win_kv))
    p_list, s_list = [], []
    for l in range(depth):
        packed = dict(proj=pack_projection(w_in[l], q_norm[l], k_norm[l, 1], k_norm[l, 2]),
                      cmp=pack_compress(w_cmp1[l], w_cmp2[l], cmp_pe[l], k_norm[l, 0]),
                      mix=pack_mixer(conv_w[l], conv_b[l], rg_w_a[l], rg_b_a[l], rg_w_x[l], rg_b_x[l], rg_lambda[l],
                                     w_pool[l], pool_scale[l]))
        y_p, st_p = prompt_mixer(y_p, l, P, packed, tables)
        y_p = ffn_layer(y_p, l, P, W)
        p_list.append(st_p)
        y_s, st_s = sample_mixer(y_s, l, P, packed, past_len, state_conv[l], state_rg_h[l], state_pool[l],
                                 caches, cache_win_kv[l], page_table, step_tables)
        y_s = ffn_layer(y_s, l, P, W)
        s_list.append(st_s)
    p_cmp_kv, p_sel_kv, p_win_kv, p_conv, p_h, p_pool = [jnp.stack(a) for a in zip(*p_list)]
    s_cmp_kv, s_sel_kv, s_win_kv, s_conv, s_h, s_pool = [jnp.stack(a) for a in zip(*s_list)]
    return (y_p, y_s, p_cmp_kv, p_sel_kv, p_win_kv, p_conv, p_h, p_pool,
            s_cmp_kv, s_sel_kv, s_win_kv, s_conv, s_h, s_pool)
```

```python
import math
import functools

import jax
import jax.numpy as jnp
import numpy as np
from jax import lax
from jax.experimental import pallas as pl
from jax.experimental.pallas import tpu as pltpu

D_MODEL = 1024
PAGE_SIZE = 128
F32 = jnp.float32
BF16 = jnp.bfloat16
EPS = 1e-6
NEG = -1e30
FORCE = 1e4
D_RNN = 512
CONV_W = 4
RG_C = 8.0
N_HEADS = 8
KV_HEADS = 2
HPG = N_HEADS // KV_HEADS
HEAD_DIM = 64
L_CMP = 32
CMP_STRIDE = 16
CMP_HIDDEN = 256
SEL_BLOCK = 64
N_SELECT = 16
WINDOW = 512
Q_BLOCK = 128
D_POOL = 512
POOL_WINDOWS = (2, 4, 8, 16)
POOL_GW = D_POOL // len(POOL_WINDOWS)
POOL_MAX = 16
REL_BUCKETS = 32
REL_MAX_DIST = 1024
N_EXPERTS = 8
KV_ROW = 2 * KV_HEADS * HEAD_DIM
SPLITS = (D_RNN, D_RNN, N_HEADS * HEAD_DIM, KV_ROW, KV_ROW, KV_ROW, 3 * N_HEADS, D_POOL, 3 * D_MODEL)

VMEM_LIMIT_BYTES = 52 * 1024 * 1024
LANES = 128
SUBLANES = 8
M_INIT = -3e38


def _pick(n, cands):
    for c in cands:
        if n % c == 0:
            return c
    return n


def _nt(a, b):
    return lax.dot_general(a, b, (((1,), (1,)), ((), ())), preferred_element_type=F32)


def _rms_rows(x, g):
    return x * lax.rsqrt(jnp.mean(x * x, axis=-1, keepdims=True) + EPS) * g


def rel_bucket(dist):
    n_exact = REL_BUCKETS // 2
    d = jnp.maximum(dist, 0)
    df = jnp.maximum(d, 1).astype(F32)
    large = n_exact + (jnp.log(df / n_exact) / math.log(REL_MAX_DIST / n_exact)
                       * (REL_BUCKETS - n_exact)).astype(jnp.int32)
    return jnp.where(d < n_exact, d, jnp.minimum(large, REL_BUCKETS - 1))


PROJ_TN = 512
ZA_COLS = 3 * D_MODEL + 2 * D_RNN + D_POOL
N_ZA = ZA_COLS // PROJ_TN
Q_COLS = N_HEADS * HEAD_DIM
GN_COLS = 2 * LANES
PROJ_COLS = ZA_COLS + Q_COLS + 3 * KV_ROW + GN_COLS
ZA_XRG, ZA_GRG, ZA_XPOOL = 3 * D_MODEL, 3 * D_MODEL + D_RNN, 3 * D_MODEL + 2 * D_RNN


def _proj_kernel(x_ref, g_ref, w_ref, seg_ref, ng_ref, nm_ref, za_ref, q_ref, kvc_ref, kvs_ref, kvw_ref, gn_ref, xn_ref):
    j = pl.program_id(1)

    @pl.when(j == 0)
    def _():
        xn_ref[...] = _rms_rows(x_ref[...], g_ref[...]).astype(BF16)

    acc = jnp.dot(xn_ref[...], w_ref[...], preferred_element_type=F32)

    @pl.when(j < N_ZA)
    def _():
        za_ref[...] = acc

    @pl.when(j >= N_ZA)
    def _():
        sq = acc * acc
        hi = sq.astype(BF16)
        lo = (sq - hi.astype(F32)).astype(BF16)
        ss = (jnp.dot(hi, seg_ref[...], preferred_element_type=F32)
              + jnp.dot(lo, seg_ref[...], preferred_element_type=F32))
        normed = acc * lax.rsqrt(ss * (1.0 / HEAD_DIM) + EPS) * ng_ref[...]
        y = jnp.where(nm_ref[...] > 0.5, normed, acc)

        @pl.when(j == N_ZA)
        def _():
            q_ref[...] = y.astype(BF16)

        @pl.when(j == N_ZA + 1)
        def _():
            kvc_ref[...] = y[:, :KV_ROW]
            kvs_ref[...] = y[:, KV_ROW:]

        @pl.when(j == N_ZA + 2)
        def _():
            kvw_ref[...] = y[:, :KV_ROW]
            gn_ref[...] = y[:, KV_ROW:]


def pack_projection(w_in, q_gain, ks_gain, kw_gain):
    cut = np.cumsum(SPLITS)[:-1].tolist()
    x_rg, g_rg, q, kv_c, kv_s, kv_w, g_nsa, x_pool, g_br = jnp.split(w_in, cut, axis=1)
    per_group = 3 * HPG
    gn = jnp.zeros((w_in.shape[0], GN_COLS), w_in.dtype)
    for g in range(KV_HEADS):
        gn = gn.at[:, g * LANES:g * LANES + per_group].set(g_nsa[:, g * per_group:(g + 1) * per_group])
    w = jnp.concatenate([g_br, x_rg, g_rg, x_pool, q, kv_c, kv_s, kv_w, gn], axis=1).astype(BF16)
    ones_v = jnp.ones((KV_HEADS * HEAD_DIM,), F32)
    zeros_v = jnp.zeros((KV_HEADS * HEAD_DIM,), F32)
    gain = jnp.concatenate([jnp.ones((ZA_COLS,), F32), jnp.tile(q_gain.astype(F32), N_HEADS) * HEAD_DIM ** -0.5,
                            jnp.ones((KV_ROW,), F32),
                            jnp.tile(ks_gain.astype(F32), KV_HEADS), ones_v,
                            jnp.tile(kw_gain.astype(F32), KV_HEADS), ones_v,
                            jnp.ones((GN_COLS,), F32)])
    mask = jnp.concatenate([jnp.zeros((ZA_COLS,), F32), jnp.ones((Q_COLS,), F32), jnp.zeros((KV_ROW,), F32),
                            ones_v, zeros_v, ones_v, zeros_v, jnp.zeros((GN_COLS,), F32)])
    return w, gain[None], mask[None]


def projection(x, norm_gain, packed):
    w, gain, mask = packed
    T = x.shape[0]
    tm = _pick(T, (1024, 512, 256, 128))
    seg = jnp.asarray((np.arange(PROJ_TN)[:, None] // HEAD_DIM == np.arange(PROJ_TN)[None, :] // HEAD_DIM)
                      .astype(np.float32), BF16)
    row = lambda i, j: (i, 0)
    return pl.pallas_call(
        _proj_kernel,
        grid=(T // tm, PROJ_COLS // PROJ_TN),
        in_specs=[pl.BlockSpec((tm, D_MODEL), row),
                  pl.BlockSpec((1, D_MODEL), lambda i, j: (0, 0)),
                  pl.BlockSpec((D_MODEL, PROJ_TN), lambda i, j: (0, j)),
                  pl.BlockSpec((PROJ_TN, PROJ_TN), lambda i, j: (0, 0)),
                  pl.BlockSpec((1, PROJ_TN), lambda i, j: (0, j)),
                  pl.BlockSpec((1, PROJ_TN), lambda i, j: (0, j))],
        out_specs=[pl.BlockSpec((tm, PROJ_TN), lambda i, j: (i, jnp.minimum(j, N_ZA - 1))),
                   pl.BlockSpec((tm, Q_COLS), row),
                   pl.BlockSpec((tm, KV_ROW), row), pl.BlockSpec((tm, KV_ROW), row), pl.BlockSpec((tm, KV_ROW), row),
                   pl.BlockSpec((tm, GN_COLS), row)],
        out_shape=[jax.ShapeDtypeStruct((T, ZA_COLS), F32), jax.ShapeDtypeStruct((T, Q_COLS), BF16),
                   jax.ShapeDtypeStruct((T, KV_ROW), F32), jax.ShapeDtypeStruct((T, KV_ROW), F32),
                   jax.ShapeDtypeStruct((T, KV_ROW), F32), jax.ShapeDtypeStruct((T, GN_COLS), F32)],
        scratch_shapes=[pltpu.VMEM((tm, D_MODEL), BF16)],
        compiler_params=pltpu.CompilerParams(dimension_semantics=("parallel", "arbitrary"),
                                             vmem_limit_bytes=VMEM_LIMIT_BYTES),
        name="projection",
    )(x, norm_gain.astype(F32)[None], w, seg, gain, mask)


CMP_PAIRS = CMP_STRIDE // 2
CMP_GW = KV_HEADS * CMP_HIDDEN


def _compress_kernel(*refs, n_pages, paged):
    if paged:
        pages = refs[1:1 + n_pages]
        eye_ref, w1_ref, pe_ref, w2_ref, kn_ref, ck_ref, cv_ref, half_ref = refs[1 + n_pages:]
        page = pages[0].shape[-1]
        m = n_pages * page // CMP_STRIDE
        for k, pg in enumerate(pages):
            for e in range(2):
                slab = pg[0, 0, e].reshape(KV_HEADS * HEAD_DIM, page).astype(BF16)
                half_ref[e, k * page:(k + 1) * page, :] = _nt(eye_ref[...], slab)
    else:
        rows_ref, w1_ref, pe_ref, w2_ref, kn_ref, ck_ref, cv_ref = refs
        m = rows_ref.shape[1] // (2 * CMP_STRIDE)
    for e, out_ref in enumerate((ck_ref, cv_ref)):
        acc = jnp.zeros((m, 2 * CMP_GW), F32)
        for p in range(CMP_PAIRS):
            def rows(s):
                if paged:
                    return half_ref[e, pl.ds(s, m, stride=CMP_STRIDE), :]
                return rows_ref[0, pl.ds(2 * s + e, m, stride=2 * CMP_STRIDE), :]
            a = jnp.concatenate([rows(2 * p), rows(2 * p + 1)], axis=1).astype(BF16)
            acc = acc + jnp.dot(a, w1_ref[e, p], preferred_element_type=F32)
        h = acc[:, :CMP_GW] + pltpu.roll(acc[:, CMP_GW:], m - 1, axis=0) + pe_ref[e]
        gl = jax.nn.gelu(h).astype(BF16)
        for g in range(KV_HEADS):
            c = jnp.dot(gl, w2_ref[e, g], preferred_element_type=F32)
            if e == 0:
                c = c * lax.rsqrt(jnp.sum(c * c, axis=1, keepdims=True) * (1.0 / HEAD_DIM) + EPS) * kn_ref[...]
            out_ref[0, g] = c.astype(out_ref.dtype)


def pack_compress(w1, w2, pe, kn):
    halves = w1.reshape(2, 2, CMP_STRIDE, HEAD_DIM, CMP_HIDDEN)
    eye = jnp.eye(KV_HEADS, dtype=w1.dtype)
    bd = jnp.einsum('ehsdf,gk->esgdhkf', halves, eye)
    w1p = bd.reshape(2, CMP_PAIRS, 2 * KV_HEADS * HEAD_DIM, 2 * CMP_GW).astype(BF16)
    pe_term = jnp.einsum('led,eldf->ef', pe, w1)
    pe_t = jnp.tile(pe_term, (1, KV_HEADS))[:, None, :].astype(F32)
    w2p = jnp.zeros((2, KV_HEADS, CMP_GW, LANES), w2.dtype)
    for g in range(KV_HEADS):
        w2p = w2p.at[:, g, g * CMP_HIDDEN:(g + 1) * CMP_HIDDEN, :HEAD_DIM].set(w2)
    knp = jnp.concatenate([kn.astype(F32), jnp.zeros((LANES - HEAD_DIM,), F32)])[None]
    return w1p, pe_t, w2p.astype(BF16), knp


def compress(rows, packed, page_table=None, layer=0):
    w1p, pe_t, w2p, knp = packed
    paged = page_table is not None
    if not paged:
        bx, n_pages = rows.shape[0], 1
        m = rows.shape[1] // (2 * CMP_STRIDE)
        data_specs = [pl.BlockSpec((1,) + rows.shape[1:], lambda b: (b, 0, 0))]
        const = lambda nd: (lambda b: (0,) * nd)
        out_map = lambda b: (b, 0, 0, 0)
        data, scratch = [rows], []
    else:
        bx, n_pages = page_table.shape
        page = rows.shape[-1]
        assert page == LANES and KV_HEADS * HEAD_DIM == LANES
        m = n_pages * page // CMP_STRIDE
        eye = jnp.asarray(np.eye(LANES, dtype=np.float32), BF16)
        const = lambda nd: (lambda b, pt: (0,) * nd)
        data_specs = [pl.BlockSpec((1, 1) + rows.shape[2:], lambda b, pt, k=k: (layer, pt[b, k], 0, 0, 0, 0))
                      for k in range(n_pages)] + [pl.BlockSpec(eye.shape, const(2))]
        out_map = lambda b, pt: (b, 0, 0, 0)
        data, scratch = [page_table] + [rows] * n_pages + [eye], [pltpu.VMEM((2, n_pages * page, LANES), F32)]
    in_specs = data_specs + [pl.BlockSpec(w1p.shape, const(4)), pl.BlockSpec(pe_t.shape, const(3)),
                             pl.BlockSpec(w2p.shape, const(4)), pl.BlockSpec(knp.shape, const(2))]
    out_spec = pl.BlockSpec((1, KV_HEADS, m, LANES), out_map)
    out_shape = jax.ShapeDtypeStruct((bx, KV_HEADS, m, LANES), BF16)
    grid_spec = pltpu.PrefetchScalarGridSpec(num_scalar_prefetch=int(paged), grid=(bx,), in_specs=in_specs,
                                             out_specs=[out_spec, out_spec], scratch_shapes=scratch)
    args = data + [w1p, pe_t, w2p, knp]
    return pl.pallas_call(
        functools.partial(_compress_kernel, n_pages=n_pages, paged=paged),
        grid_spec=grid_spec,
        out_shape=[out_shape, out_shape],
        compiler_params=pltpu.CompilerParams(dimension_semantics=("parallel",), vmem_limit_bytes=VMEM_LIMIT_BYTES),
        name="compress",
    )(*args)


MIX_TT = 512
CONV_HALO = SUBLANES
SCAN_UNROLL = 4


def _log1p(y):
    u = 1.0 + y
    return jnp.where(u == 1.0, y, jnp.log(u) * (y / jnp.where(u == 1.0, 1.0, u - 1.0)))


def _neg_expm1(x):
    t = jnp.tanh(0.5 * x)
    return -2.0 * t / (1.0 - t)


def _softplus(x):
    return jnp.maximum(x, 0.0) + _log1p(jnp.exp(-jnp.abs(x)))


def _rglru_coeffs(xc, wa_ref, ba_ref, wx_ref, bx_ref, lam_ref):
    xb = xc.astype(BF16)
    r = jax.nn.sigmoid(jnp.dot(xb, wa_ref[...], preferred_element_type=F32) + ba_ref[...])
    i = jax.nn.sigmoid(jnp.dot(xb, wx_ref[...], preferred_element_type=F32) + bx_ref[...])
    log_a = -RG_C * r * _softplus(-lam_ref[...])
    return jnp.exp(log_a), jnp.sqrt(_neg_expm1(2.0 * log_a)) * (i * xc)


def _pool_project(sums_minus, wp_ref, scale_ref):
    return jnp.dot(sums_minus.astype(BF16), wp_ref[...], preferred_element_type=F32) * scale_ref[...]


def _mixer_seq_kernel(xrg_ref, grg_ref, xpool_ref, conv0_ref, h0_ref, pool0_ref, cw_ref, cb_ref, wa_ref, ba_ref,
                      wx_ref, bx_ref, lam_ref, wp_ref, ps_ref, ya_ref, yc_ref, convn_ref, hn_ref, pooln_ref,
                      xe_ref, pe_ref, a_ref, u_ref, h_ref, carry_ref, *, start_pos):
    i = pl.program_id(1)
    tt = xrg_ref.shape[0]

    @pl.when(i == 0)
    def _():
        xe_ref[0:CONV_HALO, :] = conv0_ref[0]
        pe_ref[0:POOL_MAX, :] = pool0_ref[0]
        carry_ref[...] = h0_ref[0]

    xe_ref[CONV_HALO:CONV_HALO + tt, :] = xrg_ref[...]
    xc = cb_ref[...] + sum(cw_ref[k:k + 1, :] * xe_ref[CONV_HALO - (CONV_W - 1) + k:CONV_HALO - (CONV_W - 1) + k + tt, :]
                           for k in range(CONV_W))
    a, u = _rglru_coeffs(xc, wa_ref, ba_ref, wx_ref, bx_ref, lam_ref)
    a_ref[...] = a
    u_ref[...] = u

    row = lax.broadcasted_iota(jnp.int32, (SUBLANES, D_RNN), 0)

    def block(j, carry):
        r0 = pl.multiple_of(j * SUBLANES, SUBLANES)
        ab = a_ref[pl.ds(r0, SUBLANES), :]
        ub = u_ref[pl.ds(r0, SUBLANES), :]
        for d in (1, 2, 4):
            a_sh = jnp.where(row >= d, pltpu.roll(ab, d, axis=0), 1.0)
            u_sh = jnp.where(row >= d, pltpu.roll(ub, d, axis=0), 0.0)
            ub = ab * u_sh + ub
            ab = ab * a_sh
        hb = ab * carry + ub
        h_ref[pl.ds(r0, SUBLANES), :] = hb
        return jnp.broadcast_to(hb[SUBLANES - 1:SUBLANES, :], (SUBLANES, D_RNN))

    carry = lax.fori_loop(0, tt // SUBLANES, block, carry_ref[...], unroll=SCAN_UNROLL)
    carry_ref[...] = carry
    ya_ref[...] = (h_ref[...] * jax.nn.gelu(grg_ref[...])).astype(ya_ref.dtype)

    pe_ref[POOL_MAX:POOL_MAX + tt, :] = xpool_ref[...]
    pos = start_pos + i * tt + lax.broadcasted_iota(jnp.int32, (tt, 1), 0)
    parts = []
    for g, w in enumerate(POOL_WINDOWS):
        lanes = slice(g * POOL_GW, (g + 1) * POOL_GW)
        s = sum(pe_ref[POOL_MAX - k:POOL_MAX - k + tt, lanes] for k in range(w))
        cnt = jnp.minimum(pos + 1, w).astype(F32)
        parts.append(s / cnt - pe_ref[POOL_MAX:POOL_MAX + tt, lanes])
    yc_ref[...] = _pool_project(jnp.concatenate(parts, axis=1), wp_ref, ps_ref).astype(yc_ref.dtype)

    @pl.when(i == pl.num_programs(1) - 1)
    def _():
        convn_ref[0] = xe_ref[CONV_HALO + tt - (CONV_W - 1):CONV_HALO + tt, :]
        hn_ref[0] = carry[0:1, :]
        pooln_ref[0] = pe_ref[POOL_MAX + tt - (POOL_MAX - 1):POOL_MAX + tt, :]

    xe_ref[0:CONV_HALO, :] = xe_ref[tt:tt + CONV_HALO, :]
    pe_ref[0:POOL_MAX, :] = pe_ref[tt:tt + POOL_MAX, :]


def pack_mixer(conv_w, conv_b, w_a, b_a, w_x, b_x, lam, w_pool, scale):
    def block_diag(w):
        n, c, d = w.shape
        return jnp.einsum('ncd,nm->ncmd', w, jnp.eye(n, dtype=w.dtype)).reshape(n * c, n * d).astype(BF16)

    row = lambda v: v.astype(F32)[None]
    return (conv_w.astype(F32), row(conv_b), block_diag(w_a), row(b_a), block_diag(w_x), row(b_x), row(lam),
            block_diag(w_pool), row(scale))


def mixer_seq(za, batch, seq, packed, conv0, h0, pool0, start_pos):
    tt = min(MIX_TT, seq)
    nt = seq // tt
    conv_pad = jnp.pad(conv0.astype(F32), ((0, 0), (CONV_HALO - (CONV_W - 1), 0), (0, 0)))
    pool_pad = jnp.pad(pool0.astype(F32), ((0, 0), (1, 0), (0, 0)))
    h_pad = jnp.broadcast_to(h0.astype(F32)[:, None, :], (batch, SUBLANES, D_RNN))
    col = lambda c: pl.BlockSpec((tt, D_RNN), lambda b, i: (b * nt + i, c))
    state = lambda rows: pl.BlockSpec((1, rows, D_RNN), lambda b, i: (b, 0, 0))
    full = lambda a: pl.BlockSpec(a.shape, lambda b, i: (0,) * a.ndim)
    out_rows = pl.BlockSpec((tt, D_RNN), lambda b, i: (b * nt + i, 0))
    return pl.pallas_call(
        functools.partial(_mixer_seq_kernel, start_pos=start_pos),
        grid=(batch, nt),
        in_specs=[col(ZA_XRG // D_RNN), col(ZA_GRG // D_RNN), col(ZA_XPOOL // D_RNN), state(CONV_HALO), state(SUBLANES),
                  state(POOL_MAX)] + [full(a) for a in packed],
        out_specs=[out_rows, out_rows, state(CONV_W - 1), state(1), state(POOL_MAX - 1)],
        out_shape=[jax.ShapeDtypeStruct((batch * seq, D_RNN), BF16), jax.ShapeDtypeStruct((batch * seq, D_POOL), BF16),
                   jax.ShapeDtypeStruct((batch, CONV_W - 1, D_RNN), F32), jax.ShapeDtypeStruct((batch, 1, D_RNN), F32),
                   jax.ShapeDtypeStruct((batch, POOL_MAX - 1, D_POOL), F32)],
        scratch_shapes=[pltpu.VMEM((CONV_HALO + tt, D_RNN), F32), pltpu.VMEM((POOL_MAX + tt, D_POOL), F32),
                        pltpu.VMEM((tt, D_RNN), F32), pltpu.VMEM((tt, D_RNN), F32), pltpu.VMEM((tt, D_RNN), F32),
                        pltpu.VMEM((SUBLANES, D_RNN), F32)],
        compiler_params=pltpu.CompilerParams(dimension_semantics=("parallel", "arbitrary"),
                                             vmem_limit_bytes=VMEM_LIMIT_BYTES),
        name="mixer_seq",
    )(za, za, za, conv_pad, h_pad, pool_pad, *packed)


def _mixer_step_kernel(xrg_ref, grg_ref, xpool_ref, conv_ref, h0_ref, pool_ref, cw_ref, cb_ref, wa_ref, ba_ref,
                       wx_ref, bx_ref, lam_ref, wp_ref, ps_ref, ya_ref, yc_ref, hn_ref, *, start_pos):
    x = xrg_ref[...]
    xc = cb_ref[...] + cw_ref[CONV_W - 1:CONV_W, :] * x + sum(cw_ref[k:k + 1, :] * conv_ref[k] for k in range(CONV_W - 1))
    a, u = _rglru_coeffs(xc, wa_ref, ba_ref, wx_ref, bx_ref, lam_ref)
    h = a * h0_ref[...] + u
    hn_ref[...] = h
    ya_ref[...] = (h * jax.nn.gelu(grg_ref[...])).astype(ya_ref.dtype)
    xp = xpool_ref[...]
    parts = []
    for g, w in enumerate(POOL_WINDOWS):
        lanes = slice(g * POOL_GW, (g + 1) * POOL_GW)
        s = xp[:, lanes] + sum(pool_ref[POOL_MAX - 1 - k][:, lanes] for k in range(1, w))
        parts.append(s / float(min(start_pos + 1, w)) - xp[:, lanes])
    yc_ref[...] = _pool_project(jnp.concatenate(parts, axis=1), wp_ref, ps_ref).astype(yc_ref.dtype)


def mixer_step(za, packed, conv_state, h0, pool_state, start_pos):
    batch = za.shape[0]
    conv_t = jnp.swapaxes(conv_state.astype(F32), 0, 1)
    pool_t = jnp.swapaxes(pool_state.astype(F32), 0, 1)
    col = lambda c: pl.BlockSpec((batch, D_RNN), lambda i: (0, c))
    full = lambda a: pl.BlockSpec(a.shape, lambda i: (0,) * a.ndim)
    rows = pl.BlockSpec((batch, D_RNN), lambda i: (0, 0))
    return pl.pallas_call(
        functools.partial(_mixer_step_kernel, start_pos=start_pos),
        grid=(1,),
        in_specs=[col(ZA_XRG // D_RNN), col(ZA_GRG // D_RNN), col(ZA_XPOOL // D_RNN), full(conv_t), rows, full(pool_t)]
        + [full(a) for a in packed],
        out_specs=[rows, rows, rows],
        out_shape=[jax.ShapeDtypeStruct((batch, D_RNN), BF16), jax.ShapeDtypeStruct((batch, D_POOL), BF16),
                   jax.ShapeDtypeStruct((batch, D_RNN), F32)],
        compiler_params=pltpu.CompilerParams(vmem_limit_bytes=VMEM_LIMIT_BYTES),
        name="mixer_step",
    )(za, za, za, conv_t, h0.astype(F32), pool_t, *packed)


QB = Q_BLOCK
ROWS = HPG * QB
N_WIN_TILES = WINDOW // QB + 1
N_SEL_BIAS = REL_MAX_DIST // QB + 2
SEL_SPAN = 8
KV_CHUNK = 512


def _nsa_prompt_kernel(q_ref, ck_ref, cv_ref, kvs_ref, kvw_ref, gate_ref, bc_ref, tbs_ref, tbw_ref,
                       ovl_ref, eye_ref, pq_ref, pk_ref, pv_ref, onehot_ref, o_ref,
                       acc_ref, m_ref, qa_ref, comb_ref, ks_ref, vs_ref, kw_ref, vw_ref, cnt_ref, *, n_cmp, n_blk):
    qb = pl.program_id(2)
    t0 = qb * QB
    ncp = ck_ref.shape[2]
    seq = kvs_ref.shape[1]
    lane_row = lax.broadcasted_iota(jnp.int32, (1, LANES), 1)

    @pl.when(qb == 0)
    def _():
        ones_hi = jnp.where(lane_row >= HEAD_DIM, 1.0, 0.0)
        kw_ref[0:WINDOW, :] = jnp.broadcast_to(jnp.where(lane_row == HEAD_DIM, 1.0, 0.0), (WINDOW, LANES)).astype(BF16)
        vw_ref[0:WINDOW, :] = jnp.zeros((WINDOW, LANES), BF16)

        def stage(c, carry):
            r = pl.multiple_of(c * KV_CHUNK, KV_CHUNK)
            sel = kvs_ref[0, pl.ds(r, KV_CHUNK), :].astype(BF16)
            win = kvw_ref[0, pl.ds(r, KV_CHUNK), :].astype(BF16)
            ks_ref[pl.ds(r, KV_CHUNK), :] = (jnp.dot(sel, pk_ref[0], preferred_element_type=F32)
                                             + onehot_ref[pl.ds(r, KV_CHUNK), :].astype(F32)).astype(BF16)
            vs_ref[pl.ds(r, KV_CHUNK), :] = (jnp.dot(sel, pv_ref[0], preferred_element_type=F32) + ones_hi).astype(BF16)
            kw_ref[pl.ds(WINDOW + r, KV_CHUNK), :] = jnp.dot(win, pk_ref[0], preferred_element_type=F32).astype(BF16)
            vw_ref[pl.ds(WINDOW + r, KV_CHUNK), :] = (jnp.dot(win, pv_ref[0], preferred_element_type=F32)
                                                      + ones_hi).astype(BF16)
            return carry

        lax.fori_loop(0, seq // KV_CHUNK, stage, 0)

    q4 = jnp.dot(q_ref[0], pq_ref[...], preferred_element_type=F32)
    q3 = jnp.concatenate([q4[:, j * LANES:(j + 1) * LANES] for j in range(HPG)], axis=0)
    q = q3.astype(BF16)
    sig = jax.nn.sigmoid(gate_ref[0])
    gates = [jnp.concatenate([sig[:, 3 * j + c:3 * j + c + 1] for j in range(HPG)], axis=0) for c in range(3)]

    lc = _nt(q, ck_ref[0, 0]) + bc_ref[0].reshape(ROWS, ncp)
    tok = t0 + lax.broadcasted_iota(jnp.int32, (HPG, QB, ncp), 1).reshape(ROWS, ncp)
    col = lax.broadcasted_iota(jnp.int32, (ROWS, ncp), 1)
    valid = (tok >= col * CMP_STRIDE + (L_CMP - 1)) & (col < n_cmp)
    lc = jnp.where(valid, lc, NEG)
    mx = jnp.max(lc, axis=1, keepdims=True)
    p = jnp.where(valid, jnp.exp(lc - mx), 0.0)
    pc = p / jnp.maximum(jnp.sum(p, axis=1, keepdims=True), 1e-30)
    comb_ref[...] = gates[0] * jnp.dot(pc.astype(BF16), cv_ref[0, 0], preferred_element_type=F32)

    pcs = pc[0:QB] + pc[QB:2 * QB] + pc[2 * QB:3 * QB] + pc[3 * QB:4 * QB]
    hi = pcs.astype(BF16)
    lo = (pcs - hi.astype(F32)).astype(BF16)
    imp = _nt(ovl_ref[...], hi) + _nt(ovl_ref[...], lo)
    blk = lax.broadcasted_iota(jnp.int32, (n_blk, QB), 0)
    cur = (t0 + lax.broadcasted_iota(jnp.int32, (n_blk, QB), 1)) // SEL_BLOCK
    forced = (blk == 0) | (blk == cur) | (blk == cur - 1)
    score = jnp.where(blk > cur, -1.0, jnp.where(forced, FORCE, imp))
    chunks = [score[r:r + SUBLANES] for r in range(0, n_blk, SUBLANES)]
    sub = lax.broadcasted_iota(jnp.int32, (SUBLANES, QB), 0)
    cnt_ref[...] = jnp.zeros((n_blk, QB), F32)
    last_blk = (t0 + QB - 1) // SEL_BLOCK
    for mc in range(0, n_blk, SUBLANES):
        @pl.when(mc <= last_blk)
        def _(mc=mc):
            for r, ch in enumerate(chunks):
                first = r * SUBLANES
                part = jnp.zeros((SUBLANES, QB), F32)
                for m in range(mc, mc + SUBLANES):
                    row = jnp.broadcast_to(score[m:m + 1, :], (SUBLANES, QB))
                    if first > m:
                        beats = jnp.where(row >= ch, 1.0, 0.0)
                    elif first + SUBLANES - 1 < m:
                        beats = jnp.where(row > ch, 1.0, 0.0)
                    else:
                        beats = jnp.where(sub + first > m, jnp.where(row >= ch, 1.0, 0.0), jnp.where(row > ch, 1.0, 0.0))
                    part = part + beats
                cnt_ref[first:first + SUBLANES, :] += part
    sel_neg = jnp.where(cnt_ref[...] < float(min(N_SELECT, n_blk)), 0.0, NEG)
    pieces = [jnp.zeros((HEAD_DIM, QB), F32), sel_neg]
    if n_blk < HEAD_DIM:
        pieces.append(jnp.zeros((HEAD_DIM - n_blk, QB), F32))
    placed_t = jnp.concatenate(pieces, axis=0).astype(BF16)
    placed = _nt(eye_ref[...], placed_t)
    qa_ref[...] = (q3 + jnp.concatenate([placed] * HPG, axis=0)).astype(BF16)

    m_ref[...] = jnp.full((ROWS, LANES), M_INIT, F32)
    acc_ref[...] = jnp.zeros((ROWS, LANES), F32)
    n_bias = tbs_ref.shape[1]

    span = SEL_SPAN * QB
    n_spans = qb // SEL_SPAN + 1

    def body(kk, carry):
        off = pl.multiple_of(kk * span, span)
        s = _nt(qa_ref[...], ks_ref[pl.ds(off, span), :])
        parts = []
        for u in range(SEL_SPAN):
            idx = jnp.clip(qb - (kk * SEL_SPAN + u), -1, n_bias - 2) + 1
            parts.append(s[:, u * QB:(u + 1) * QB] + tbs_ref[:, pl.ds(idx, 1)].reshape(ROWS, LANES))
        tile_max = functools.reduce(jnp.maximum, parts)
        m_old = m_ref[...]
        m_new = jnp.maximum(m_old, jnp.max(tile_max, axis=1, keepdims=True))
        alpha = jnp.exp(m_old - m_new)
        pr = jnp.concatenate([jnp.exp(x - m_new).astype(BF16) for x in parts], axis=1)
        acc_ref[...] = alpha * acc_ref[...] + jnp.dot(pr, vs_ref[pl.ds(off, SEL_SPAN * QB), :],
                                                      preferred_element_type=F32)
        m_ref[...] = m_new
        return carry

    lax.fori_loop(0, n_spans, body, 0)
    acc = acc_ref[...]
    comb_ref[...] += gates[1] * (acc / pltpu.roll(acc, HEAD_DIM, axis=1))

    qw = (q3 + jnp.where(lane_row == HEAD_DIM, NEG, 0.0)).astype(BF16)
    w_off = pl.multiple_of(t0, QB)
    sw = _nt(qw, kw_ref[pl.ds(w_off, WINDOW + QB), :]) + tbw_ref[...].reshape(ROWS, WINDOW + QB)
    pw = jnp.exp(sw - jnp.max(sw, axis=1, keepdims=True))
    accw = jnp.dot(pw.astype(BF16), vw_ref[pl.ds(w_off, WINDOW + QB), :], preferred_element_type=F32)
    comb_ref[...] += gates[2] * (accw / pltpu.roll(accw, HEAD_DIM, axis=1))

    comb = comb_ref[...]
    lane = lax.broadcasted_iota(jnp.int32, (QB, LANES), 1)
    for half in range(HPG // 2):
        a = comb[(2 * half) * QB:(2 * half + 1) * QB]
        b = comb[(2 * half + 1) * QB:(2 * half + 2) * QB]
        o_ref[0, :, half * LANES:(half + 1) * LANES] = jnp.where(lane < HEAD_DIM, a,
                                                                 pltpu.roll(b, HEAD_DIM, axis=1)).astype(o_ref.dtype)


def _bias_by_distance(rel_bias):
    max_d = (N_SEL_BIAS + 1) * QB
    return jnp.transpose(rel_bias.astype(F32)[rel_bucket(jnp.arange(max_d))])


def _ext(tab_t, lo, hi, ok_lo=None, ok_hi=None):
    n_heads, depth = tab_t.shape
    y = np.arange(lo, hi)
    parts = [jnp.broadcast_to(tab_t[:, :1], (n_heads, int(np.sum(y < 0)))), tab_t[:, max(lo, 0):max(min(hi, depth), 0)],
             jnp.broadcast_to(tab_t[:, -1:], (n_heads, int(np.sum(y > depth - 1))))]
    arr = jnp.concatenate(parts, axis=1)
    ok = np.ones(y.shape, bool)
    if ok_lo is not None:
        ok &= y >= ok_lo
    if ok_hi is not None:
        ok &= y <= ok_hi
    return jnp.where(jnp.asarray(ok)[None], arr, NEG)


def _toeplitz_tiles(ext, lo, ks):
    w = jnp.stack([ext[:, QB * k - (QB - 1) - lo:QB * k - (QB - 1) - lo + 2 * QB] for k in ks], axis=1)
    skew = jnp.tile(w, (1, 1, QB + 1))[..., :QB * (2 * QB + 1)].reshape(w.shape[:2] + (QB, 2 * QB + 1))[..., :QB]
    return skew[..., ::-1]


def rel_bias_tables(rel_bias, seq):
    tab_t = _bias_by_distance(rel_bias)
    max_d = tab_t.shape[1]
    lo = -(2 * QB - 1)
    tbs = _toeplitz_tiles(_ext(tab_t, lo, max_d, 0, None), lo, range(-1, N_SEL_BIAS))
    tbw = _toeplitz_tiles(_ext(tab_t, lo, max_d, 0, WINDOW), lo, range(N_WIN_TILES - 1, -1, -1))
    tbw = jnp.transpose(tbw, (0, 2, 1, 3)).reshape(N_HEADS, QB, N_WIN_TILES * QB)
    nqb = seq // QB
    ncp = seq // CMP_STRIDE
    per_qb = QB // CMP_STRIDE
    width = ncp + per_qb * (nqb - 1)
    c0 = QB * (nqb - 1) - (L_CMP - 1)
    ext = _ext(tab_t, c0 - CMP_STRIDE * (width - 1), c0 + QB)
    n_z = width + per_qb - 1
    f = ext[:, :CMP_STRIDE * n_z].reshape(N_HEADS, n_z, CMP_STRIDE)[:, ::-1]
    v = jnp.concatenate([jnp.swapaxes(f[:, per_qb - 1 - a:per_qb - 1 - a + width], 1, 2) for a in range(per_qb)],
                        axis=1)
    bc = jnp.stack([v[:, :, per_qb * (nqb - 1 - b):per_qb * (nqb - 1 - b) + ncp] for b in range(nqb)])
    return tbs, tbw, bc


def nsa_prompt_pallas(q, ck, cv, kv_sel, kv_win, gn, tables):
    B, S = q.shape[:2]
    assert S % (SEL_SPAN * QB) == 0 and S % KV_CHUNK == 0 and S // SEL_BLOCK <= HEAD_DIM
    tbs, tbw, bc = tables
    nqb = S // QB
    ncp = S // CMP_STRIDE
    n_cmp = ncp - 1
    n_blk = S // SEL_BLOCK
    c0 = np.arange(ncp)[None, :] * CMP_STRIDE
    s0 = np.arange(n_blk)[:, None] * SEL_BLOCK
    ovl_t = ((c0 <= s0 + SEL_BLOCK - 1) & (c0 + L_CMP - 1 >= s0) & (np.arange(ncp)[None, :] < n_cmp))
    ovl_t = jnp.asarray(ovl_t.astype(np.float32), BF16)
    eye = jnp.asarray(np.eye(QB, dtype=np.float32), BF16)
    gw = HPG * HEAD_DIM
    pq = np.zeros((gw, HPG * LANES), np.float32)
    pq[np.arange(gw), (np.arange(gw) // HEAD_DIM) * LANES + np.arange(gw) % HEAD_DIM] = 1.0
    pk = np.zeros((KV_HEADS, KV_ROW, LANES), np.float32)
    pv = np.zeros((KV_HEADS, KV_ROW, LANES), np.float32)
    for g in range(KV_HEADS):
        pk[g, g * HEAD_DIM + np.arange(HEAD_DIM), np.arange(HEAD_DIM)] = 1.0
        pv[g, (KV_HEADS + g) * HEAD_DIM + np.arange(HEAD_DIM), np.arange(HEAD_DIM)] = 1.0
    onehot = np.zeros((S, LANES), np.float32)
    onehot[np.arange(S), HEAD_DIM + np.arange(S) // SEL_BLOCK] = 1.0

    kv_spec = pl.BlockSpec((1, S, KV_ROW), lambda b, g, i: (b, 0, 0))
    cmp_spec = pl.BlockSpec((1, 1, ncp, LANES), lambda b, g, i: (b, g, 0, 0))
    const2 = lambda b, g, i: (0, 0)
    return pl.pallas_call(
        functools.partial(_nsa_prompt_kernel, n_cmp=n_cmp, n_blk=n_blk),
        grid=(B, KV_HEADS, nqb),
        in_specs=[
            pl.BlockSpec((1, QB, gw), lambda b, g, i: (b, i, g)),
            cmp_spec, cmp_spec, kv_spec, kv_spec,
            pl.BlockSpec((1, QB, LANES), lambda b, g, i: (b, i, g)),
            pl.BlockSpec((1, HPG, QB, ncp), lambda b, g, i: (i, g, 0, 0)),
            pl.BlockSpec((HPG, N_SEL_BIAS + 1, QB, QB), lambda b, g, i: (g, 0, 0, 0)),
            pl.BlockSpec((HPG, QB, N_WIN_TILES * QB), lambda b, g, i: (g, 0, 0)),
            pl.BlockSpec((n_blk, ncp), const2),
            pl.BlockSpec((QB, QB), const2),
            pl.BlockSpec((gw, HPG * LANES), const2),
            pl.BlockSpec((1, KV_ROW, LANES), lambda b, g, i: (g, 0, 0)),
            pl.BlockSpec((1, KV_ROW, LANES), lambda b, g, i: (g, 0, 0)),
            pl.BlockSpec((S, LANES), const2),
        ],
        out_specs=pl.BlockSpec((1, QB, gw), lambda b, g, i: (b, i, g)),
        out_shape=jax.ShapeDtypeStruct((B, S, N_HEADS * HEAD_DIM), BF16),
        scratch_shapes=[pltpu.VMEM((ROWS, LANES), F32), pltpu.VMEM((ROWS, LANES), F32),
                        pltpu.VMEM((ROWS, LANES), BF16), pltpu.VMEM((ROWS, LANES), F32),
                        pltpu.VMEM((S, LANES), BF16), pltpu.VMEM((S, LANES), BF16),
                        pltpu.VMEM((S + WINDOW, LANES), BF16), pltpu.VMEM((S + WINDOW, LANES), BF16),
                        pltpu.VMEM((n_blk, QB), F32)],
        compiler_params=pltpu.CompilerParams(dimension_semantics=("parallel", "parallel", "arbitrary"),
                                             vmem_limit_bytes=VMEM_LIMIT_BYTES),
        name="nsa_prompt",
    )(q, ck, cv, kv_sel, kv_win, gn, bc, tbs, tbw, ovl_t, eye, jnp.asarray(pq, BF16), jnp.asarray(pk, BF16),
      jnp.asarray(pv, BF16), jnp.asarray(onehot, BF16))


STEP_ROWS = SUBLANES
RANK_LANES = 2 * LANES


def _split3(x):
    hi = x.astype(BF16)
    r1 = x - hi.astype(F32)
    mid = r1.astype(BF16)
    return hi, mid, (r1 - mid.astype(F32)).astype(BF16)


def _nsa_step_kernel(*refs, n_pages, n_cmp):
    pages = refs[1:1 + n_pages]
    (q_ref, ck_ref, cv_ref, kvs_ref, kvw_ref, win_ref, gn_ref, bcs_ref, bsel_ref, bwin_ref, ovl_ref, oh_ref,
     pq_ref, pk_ref, pv_ref, o_ref, kc_ref, vs_ref, kw_ref, vw_ref) = refs[1 + n_pages:]
    page = pages[0].shape[-1]
    past = n_pages * page
    wb = win_ref.shape[-1]
    ncp = ck_ref.shape[2]
    first_row = lax.broadcasted_iota(jnp.int32, (LANES, 1), 0) == 0

    @pl.when(pl.program_id(0) == 0)
    def _():
        for g in range(KV_HEADS):
            kc_ref[g, HEAD_DIM:LANES, :] = jnp.zeros((LANES - HEAD_DIM, past + LANES), BF16)
            kc_ref[g, LANES:2 * LANES, 0:past] = oh_ref[...]
            kc_ref[g, LANES:2 * LANES, past:past + LANES] = jnp.zeros((LANES, LANES), BF16)
            vs_ref[g, HEAD_DIM:LANES, :] = jnp.ones((LANES - HEAD_DIM, past + LANES), BF16)
            kw_ref[g, HEAD_DIM:LANES, :] = jnp.zeros((LANES - HEAD_DIM, wb + LANES), BF16)
            vw_ref[g, HEAD_DIM:LANES, :] = jnp.ones((LANES - HEAD_DIM, wb + LANES), BF16)

    def place_new(row_ref, c0, k_dst, v_dst):
        new = jnp.where(first_row, row_ref[0], 0.0).astype(BF16)
        for g in range(KV_HEADS):
            k_dst[g, 0:HEAD_DIM, c0:c0 + LANES] = _nt(pk_ref[g], new)[0:HEAD_DIM].astype(BF16)
            v_dst[g, 0:HEAD_DIM, c0:c0 + LANES] = _nt(pv_ref[g], new)[0:HEAD_DIM].astype(BF16)

    for k, pg in enumerate(pages):
        for g in range(KV_HEADS):
            kc_ref[g, 0:HEAD_DIM, k * page:(k + 1) * page] = pg[0, 0, 0, g].astype(BF16)
            vs_ref[g, 0:HEAD_DIM, k * page:(k + 1) * page] = pg[0, 0, 1, g].astype(BF16)
    place_new(kvs_ref, past, kc_ref, vs_ref)
    for g in range(KV_HEADS):
        kw_ref[g, 0:HEAD_DIM, 0:wb] = win_ref[0, 0, 0, g].astype(BF16)
        vw_ref[g, 0:HEAD_DIM, 0:wb] = win_ref[0, 0, 1, g].astype(BF16)
    place_new(kvw_ref, wb, kw_ref, vw_ref)

    sig = jax.nn.sigmoid(gn_ref[0])
    lane = lax.broadcasted_iota(jnp.int32, (1, LANES), 1)
    rr = lax.broadcasted_iota(jnp.int32, (RANK_LANES, RANK_LANES), 0)
    cc = lax.broadcasted_iota(jnp.int32, (RANK_LANES, RANK_LANES), 1)
    ones_sq = jnp.ones((RANK_LANES, RANK_LANES), BF16)
    pad_rows = jnp.zeros((STEP_ROWS - HPG, LANES), F32)
    for g in range(KV_HEADS):
        gw = HPG * HEAD_DIM
        q4 = jnp.dot(q_ref[0][:, g * gw:(g + 1) * gw], pq_ref[...], preferred_element_type=F32)
        q8 = jnp.concatenate([q4[:, j * LANES:(j + 1) * LANES] for j in range(HPG)] + [pad_rows], axis=0)
        gates = [jnp.concatenate([sig[:, g * LANES + 3 * j + c:g * LANES + 3 * j + c + 1] for j in range(HPG)]
                                 + [pad_rows[:, 0:1]], axis=0) for c in range(3)]

        lc = _nt(q8.astype(BF16), ck_ref[0, g]) + bcs_ref[g]
        col = lax.broadcasted_iota(jnp.int32, (STEP_ROWS, ncp), 1)
        valid = col < n_cmp
        lc = jnp.where(valid, lc, NEG)
        p = jnp.where(valid, jnp.exp(lc - jnp.max(lc, axis=1, keepdims=True)), 0.0)
        pc = p / jnp.maximum(jnp.sum(p, axis=1, keepdims=True), 1e-30)
        comb = gates[0] * jnp.dot(pc.astype(BF16), cv_ref[0, g], preferred_element_type=F32)

        pcs = jnp.broadcast_to(jnp.sum(pc[0:HPG], axis=0, keepdims=True), (STEP_ROWS, ncp))
        hi = pcs.astype(BF16)
        lo = (pcs - hi.astype(F32)).astype(BF16)
        imp = (jnp.dot(hi, ovl_ref[...], preferred_element_type=F32)
               + jnp.dot(lo, ovl_ref[...], preferred_element_type=F32))[0:1]
        blk_id = lax.broadcasted_iota(jnp.int32, (1, RANK_LANES), 1)
        cur = past // SEL_BLOCK
        forced = (blk_id == 0) | (blk_id == cur) | (blk_id == cur - 1)
        score = jnp.where(blk_id > cur, -1.0, jnp.where(forced, FORCE, imp))
        s_n = jnp.broadcast_to(score, (RANK_LANES, RANK_LANES))
        diag = jnp.where(rr == cc, s_n, 0.0)
        s_m = sum(_nt(part, ones_sq) for part in _split3(diag))
        beats = jnp.where(rr < cc, jnp.where(s_m >= s_n, 1.0, 0.0), jnp.where(s_m > s_n, 1.0, 0.0))
        rank = jnp.sum(beats, axis=0, keepdims=True)
        sel_neg = jnp.where(rank < float(N_SELECT), 0.0, NEG)

        qa = jnp.concatenate([q8, jnp.broadcast_to(sel_neg[:, :LANES], (STEP_ROWS, LANES))], axis=1).astype(BF16)
        s_past = jnp.dot(qa, kc_ref[g, :, 0:past], preferred_element_type=F32) + bsel_ref[g, :, 0:past]
        s_new = (jnp.dot(qa, kc_ref[g, :, past:past + LANES], preferred_element_type=F32)
                 + bsel_ref[g, :, past:past + LANES] + sel_neg[:, LANES:LANES + 1])
        mx = jnp.maximum(jnp.max(s_past, axis=1, keepdims=True), jnp.max(s_new, axis=1, keepdims=True))
        acc = (_nt(jnp.exp(s_past - mx).astype(BF16), vs_ref[g, :, 0:past])
               + _nt(jnp.exp(s_new - mx).astype(BF16), vs_ref[g, :, past:past + LANES]))
        comb = comb + gates[1] * (acc / pltpu.roll(acc, HEAD_DIM, axis=1))

        sw = jnp.dot(q8.astype(BF16), kw_ref[g], preferred_element_type=F32) + bwin_ref[g]
        pw = jnp.exp(sw - jnp.max(sw, axis=1, keepdims=True))
        accw = _nt(pw.astype(BF16), vw_ref[g])
        comb = comb + gates[2] * (accw / pltpu.roll(accw, HEAD_DIM, axis=1))

        for half in range(HPG // 2):
            a = comb[2 * half:2 * half + 1]
            b = comb[2 * half + 1:2 * half + 2]
            o_ref[0, :, g * gw + half * LANES:g * gw + (half + 1) * LANES] = jnp.where(
                lane < HEAD_DIM, a, pltpu.roll(b, HEAD_DIM, axis=1)).astype(o_ref.dtype)


def step_bias_tables(rel_bias, past, wb, ncp):
    tab_t = _bias_by_distance(rel_bias)

    def rows(t):
        return jnp.pad(t.reshape(KV_HEADS, HPG, -1), ((0, 0), (0, STEP_ROWS - HPG), (0, 0)))

    d_first = past - (L_CMP - 1)
    cmp_t = _ext(tab_t, d_first - CMP_STRIDE * (ncp - 1), d_first + 1, 0, None)[:, ::CMP_STRIDE][:, ::-1]
    sel_t = _ext(tab_t, -(LANES - 1), past + 1, 0, None)[:, ::-1]
    win_t = _ext(tab_t, -(LANES - 1), wb + 1, 0, WINDOW)[:, ::-1]
    return rows(cmp_t), rows(sel_t), rows(win_t)


def nsa_step(q, ck, cv, kvs_new, kvw_new, sel_pool, win_buf, gn, page_table, tables, layer):
    B, n_pages = page_table.shape
    page = sel_pool.shape[-1]
    past = n_pages * page
    wb = win_buf.shape[-1]
    ncp = ck.shape[2]
    n_blk = past // SEL_BLOCK + 1
    assert n_blk <= LANES + 1 and page == LANES and wb % LANES == 0
    bcs, bsel, bwin = tables
    c0 = np.arange(ncp)[:, None] * CMP_STRIDE
    s0 = np.arange(RANK_LANES)[None, :] * SEL_BLOCK
    ovl = ((c0 <= s0 + SEL_BLOCK - 1) & (c0 + L_CMP - 1 >= s0) & (np.arange(ncp)[:, None] < ncp - 1)
           & (np.arange(RANK_LANES)[None, :] < n_blk))
    onehot = np.zeros((LANES, past), np.float32)
    onehot[np.arange(past) // SEL_BLOCK, np.arange(past)] = 1.0
    gw = HPG * HEAD_DIM
    pq = np.zeros((gw, HPG * LANES), np.float32)
    pq[np.arange(gw), (np.arange(gw) // HEAD_DIM) * LANES + np.arange(gw) % HEAD_DIM] = 1.0
    pk = np.zeros((KV_HEADS, LANES, KV_ROW), np.float32)
    pv = np.zeros((KV_HEADS, LANES, KV_ROW), np.float32)
    for g in range(KV_HEADS):
        pk[g, np.arange(HEAD_DIM), g * HEAD_DIM + np.arange(HEAD_DIM)] = 1.0
        pv[g, np.arange(HEAD_DIM), (KV_HEADS + g) * HEAD_DIM + np.arange(HEAD_DIM)] = 1.0
    consts = [jnp.asarray(ovl.astype(np.float32), BF16), jnp.asarray(onehot, BF16), jnp.asarray(pq, BF16),
              jnp.asarray(pk, BF16), jnp.asarray(pv, BF16)]

    row3 = lambda a: a.reshape(B, 1, a.shape[-1])
    per_b = lambda shape: pl.BlockSpec((1,) + shape, lambda b, pt: (b,) + (0,) * len(shape))
    full = lambda a: pl.BlockSpec(a.shape, lambda b, pt: (0,) * a.ndim)
    slab = (2, KV_HEADS, HEAD_DIM)
    page_specs = [pl.BlockSpec((1, 1) + slab + (page,), lambda b, pt, k=k: (layer, pt[b, k], 0, 0, 0, 0))
                  for k in range(n_pages)]
    win_spec = pl.BlockSpec((1, 1) + slab + (wb,), lambda b, pt: (layer, b, 0, 0, 0, 0))
    in_specs = page_specs + [per_b((1, Q_COLS)), per_b((KV_HEADS, ncp, LANES)), per_b((KV_HEADS, ncp, LANES)),
                             per_b((1, KV_ROW)), per_b((1, KV_ROW)), win_spec, per_b((1, GN_COLS)),
                             full(bcs), full(bsel), full(bwin)] + [full(c) for c in consts]
    grid_spec = pltpu.PrefetchScalarGridSpec(
        num_scalar_prefetch=1, grid=(B,), in_specs=in_specs, out_specs=per_b((1, Q_COLS)),
        scratch_shapes=[pltpu.VMEM((KV_HEADS, 2 * LANES, past + LANES), BF16), pltpu.VMEM((KV_HEADS, LANES, past + LANES), BF16),
                        pltpu.VMEM((KV_HEADS, LANES, wb + LANES), BF16), pltpu.VMEM((KV_HEADS, LANES, wb + LANES), BF16)])
    out = pl.pallas_call(
        functools.partial(_nsa_step_kernel, n_pages=n_pages, n_cmp=ncp - 1),
        grid_spec=grid_spec,
        out_shape=jax.ShapeDtypeStruct((B, 1, Q_COLS), BF16),
        compiler_params=pltpu.CompilerParams(dimension_semantics=("arbitrary",), vmem_limit_bytes=VMEM_LIMIT_BYTES),
        name="nsa_step",
    )(page_table, *([sel_pool] * n_pages), row3(q), ck, cv, row3(kvs_new), row3(kvw_new), win_buf, row3(gn),
      bcs, bsel, bwin, *consts)
    return out.reshape(B, Q_COLS)


def _merge_kernel(ya_ref, yb_ref, yc_ref, ga_ref, gb_ref, gc_ref, x_ref, wa_ref, wb_ref, wc_ref, wo_ref, o_ref):
    def branch(y_ref, g_ref, w_ref):
        return jax.nn.sigmoid(g_ref[...]) * jnp.dot(y_ref[...], w_ref[...], preferred_element_type=F32)

    merged = branch(ya_ref, ga_ref, wa_ref) + branch(yb_ref, gb_ref, wb_ref) + branch(yc_ref, gc_ref, wc_ref)
    o_ref[...] = x_ref[...] + jnp.dot(merged.astype(BF16), wo_ref[...], preferred_element_type=F32)


def merge(ya, yb, yc, za, x, wa, wb, wc, wo):
    T = x.shape[0]
    tm = _pick(T, (512, 256, 128))
    y_spec = pl.BlockSpec((tm, ya.shape[1]), lambda i: (i, 0))
    w_spec = pl.BlockSpec((ya.shape[1], D_MODEL), lambda i: (0, 0))
    return pl.pallas_call(
        _merge_kernel,
        grid=(T // tm,),
        in_specs=[y_spec, y_spec, y_spec,
                  pl.BlockSpec((tm, D_MODEL), lambda i: (i, 0)), pl.BlockSpec((tm, D_MODEL), lambda i: (i, 1)),
                  pl.BlockSpec((tm, D_MODEL), lambda i: (i, 2)),
                  pl.BlockSpec((tm, D_MODEL), lambda i: (i, 0)),
                  w_spec, w_spec, w_spec, pl.BlockSpec((D_MODEL, D_MODEL), lambda i: (0, 0))],
        out_specs=pl.BlockSpec((tm, D_MODEL), lambda i: (i, 0)),
        out_shape=jax.ShapeDtypeStruct((T, D_MODEL), F32),
        compiler_params=pltpu.CompilerParams(dimension_semantics=("parallel",), vmem_limit_bytes=VMEM_LIMIT_BYTES),
        name="merge",
    )(ya, yb, yc, za, za, za, x, wa, wb, wc, wo)


def _ffn_kernel(*refs, routed):
    if routed:
        h_ref, g_ref, wr_ref, br_ref, wg_ref, wu_ref, wd_ref, o_ref, xn_ref, gate_ref = refs
    else:
        h_ref, g_ref, wg_ref, wu_ref, wd_ref, o_ref, xn_ref = refs
    e = pl.program_id(1)
    f = pl.program_id(2)

    @pl.when((e == 0) & (f == 0))
    def _():
        h = h_ref[...]
        xb = _rms_rows(h, g_ref[...]).astype(BF16)
        xn_ref[...] = xb
        o_ref[...] = h
        if routed:
            logits = jnp.dot(xb, wr_ref[...], preferred_element_type=F32) + br_ref[...]
            lane = lax.broadcasted_iota(jnp.int32, logits.shape, 1)
            m1 = jnp.max(logits, axis=1, keepdims=True)
            i1 = jnp.min(jnp.where(logits == m1, lane, LANES), axis=1, keepdims=True)
            rest = jnp.where(lane == i1, M_INIT, logits)
            m2 = jnp.max(rest, axis=1, keepdims=True)
            i2 = jnp.min(jnp.where(rest == m2, lane, LANES), axis=1, keepdims=True)
            r = jnp.exp(m2 - m1)
            gate_ref[...] = jnp.where(lane == i1, 1.0 / (1.0 + r), 0.0) + jnp.where(lane == i2, r / (1.0 + r), 0.0)

    xb = xn_ref[...]
    a = jnp.dot(xb, wg_ref[0], preferred_element_type=F32)
    u = jnp.dot(xb, wu_ref[0], preferred_element_type=F32)
    t = (a * jax.nn.sigmoid(a)) * u
    y = jnp.dot(t.astype(BF16), wd_ref[0], preferred_element_type=F32)
    if routed:
        lane = lax.broadcasted_iota(jnp.int32, gate_ref.shape, 1)
        y = jnp.sum(jnp.where(lane == e, gate_ref[...], 0.0), axis=1, keepdims=True) * y
    o_ref[...] += y


def channel_mixer(h, norm_gain, wg, wu, wd, router=None):
    T = h.shape[0]
    E, _, F = wg.shape
    tm = _pick(T, (512, 256, 128))
    tf = _pick(F, (1408, 1024, 512, 256, 128))
    routed = router is not None
    row = lambda i, e, f: (i, 0)
    in_specs = [pl.BlockSpec((tm, D_MODEL), row), pl.BlockSpec((1, D_MODEL), lambda i, e, f: (0, 0))]
    args = [h, norm_gain.astype(F32)[None]]
    scratch = [pltpu.VMEM((tm, D_MODEL), BF16)]
    if routed:
        in_specs += [pl.BlockSpec((D_MODEL, LANES), lambda i, e, f: (0, 0)), pl.BlockSpec((1, LANES), lambda i, e, f: (0, 0))]
        args += list(router)
        scratch.append(pltpu.VMEM((tm, LANES), F32))
    in_specs += [pl.BlockSpec((1, D_MODEL, tf), lambda i, e, f: (e, 0, f)),
                 pl.BlockSpec((1, D_MODEL, tf), lambda i, e, f: (e, 0, f)),
                 pl.BlockSpec((1, tf, D_MODEL), lambda i, e, f: (e, f, 0))]
    args += [wg, wu, wd]
    return pl.pallas_call(
        functools.partial(_ffn_kernel, routed=routed),
        grid=(T // tm, E, F // tf),
        in_specs=in_specs,
        out_specs=pl.BlockSpec((tm, D_MODEL), row),
        out_shape=jax.ShapeDtypeStruct((T, D_MODEL), F32),
        scratch_shapes=scratch,
        compiler_params=pltpu.CompilerParams(dimension_semantics=("parallel", "arbitrary", "arbitrary"),
                                             vmem_limit_bytes=VMEM_LIMIT_BYTES),
        name="moe" if routed else "ffn",
    )(*args)


def mixer_tail(x2, za, y_a, y_b, y_c, l, P):
    return merge(y_a, y_b, y_c, za, x2, P['w_br_rg'][l].astype(BF16), P['w_br_attn'][l].astype(BF16),
                 P['w_br_pool'][l].astype(BF16), P['w_out'][l].astype(BF16))


def prompt_mixer(x, l, P, packed, tables):
    B, T, _ = x.shape
    x2 = x.reshape(B * T, D_MODEL)
    za, q, kv_c, kv_s, kv_w, gn = projection(x2, P['attn_norm'][l], packed['proj'])
    zeros = lambda rows: jnp.zeros((B, rows, D_RNN), F32)
    y_a, y_c, conv_new, h_last, pool_new = mixer_seq(za, B, T, packed['mix'], zeros(CONV_W - 1), jnp.zeros((B, D_RNN), F32),
                                                     zeros(POOL_MAX - 1), 0)
    ck, cv = compress(kv_c.reshape(B, 2 * T, LANES), packed['cmp'])
    y_b = nsa_prompt_pallas(q.reshape(B, T, Q_COLS), ck, cv, kv_s.reshape(B, T, KV_ROW), kv_w.reshape(B, T, KV_ROW),
                            gn.reshape(B, T, GN_COLS), tables)
    out = mixer_tail(x2, za, y_a, y_b.reshape(B * T, Q_COLS), y_c, l, P)
    kv_shape = (B, T, 2, KV_HEADS, HEAD_DIM)
    state = (kv_c.reshape(kv_shape), kv_s.reshape(kv_shape), kv_w.reshape(kv_shape)[:, -min(WINDOW, T):], conv_new,
             h_last[:, 0], pool_new)
    return out.reshape(B, T, D_MODEL), state


def sample_mixer(x, l, P, packed, past_len, conv_state, h0, pool_state, caches, win_buf, page_table, step_tables):
    B, T, _ = x.shape
    x2 = x.reshape(B * T, D_MODEL)
    za, q, kv_c, kv_s, kv_w, gn = projection(x2, P['attn_norm'][l], packed['proj'])
    y_a, y_c, h_new = mixer_step(za, packed['mix'], conv_state, h0, pool_state, past_len)
    cmp_pools, sel_pools, win_bufs = caches
    ck, cv = compress(cmp_pools, packed['cmp'], page_table, layer=l)
    kv_shape = (B, T, 2, KV_HEADS, HEAD_DIM)
    wb = win_buf.shape[1]
    y_b = nsa_step(q, ck, cv, kv_s, kv_w, sel_pools, win_bufs, gn, page_table, step_tables, l)
    win_new = jnp.concatenate([win_buf[:, 1:], kv_w.reshape(kv_shape)], axis=1)
    out = mixer_tail(x2, za, y_a, y_b, y_c, l, P)
    conv_new = jnp.concatenate([conv_state[:, 1:], za[:, None, ZA_XRG:ZA_XRG + D_RNN]], axis=1)
    pool_new = jnp.concatenate([pool_state[:, 1:], za[:, None, ZA_XPOOL:ZA_XPOOL + D_POOL]], axis=1)
    state = (kv_c.reshape(kv_shape), kv_s.reshape(kv_shape), win_new, conv_new, h_new, pool_new)
    return out.reshape(B, T, D_MODEL), state


def ffn_layer(x, l, P, W):
    B, T, _ = x.shape
    i = l // 2
    if l % 2 == 0:
        y = channel_mixer(x.reshape(B * T, D_MODEL), P['ffn_norm'][l], W['ffn_g'][i], W['ffn_u'][i], W['ffn_d'][i])
    else:
        y = channel_mixer(x.reshape(B * T, D_MODEL), P['ffn_norm'][l], W['moe_g'][i], W['moe_u'][i], W['moe_d'][i],
                          router=W['router'][i])
    return y.reshape(B, T, D_MODEL)


def kernel(x_prompt, x_sample, cache_cmp_kv, cache_sel_kv, cache_win_kv, state_conv, state_rg_h, state_pool,
           page_table, attn_norm, w_in, conv_w, conv_b, rg_w_a, rg_b_a, rg_w_x, rg_b_x, rg_lambda, q_norm, k_norm,
           cmp_pe, w_cmp1, w_cmp2, rel_bias, w_pool, pool_scale, w_br_rg, w_br_attn, w_br_pool, w_out, ffn_norm,
           ffn_w_gate, ffn_w_up, ffn_w_down, w_router, b_router, moe_w_gate, moe_w_up, moe_w_down):
    P = dict(attn_norm=attn_norm, conv_w=conv_w, conv_b=conv_b, rg_w_a=rg_w_a, rg_b_a=rg_b_a,
             rg_w_x=rg_w_x, rg_b_x=rg_b_x, rg_lambda=rg_lambda, w_pool=w_pool,
             pool_scale=pool_scale, w_br_rg=w_br_rg, w_br_attn=w_br_attn, w_br_pool=w_br_pool, w_out=w_out,
             ffn_norm=ffn_norm)
    depth = w_in.shape[0]
    n_moe = w_router.shape[0]
    pad_e = LANES - N_EXPERTS
    W = dict(
        ffn_g=[w[None].astype(BF16) for w in ffn_w_gate], ffn_u=[w[None].astype(BF16) for w in ffn_w_up],
        ffn_d=[w[None].astype(BF16) for w in ffn_w_down],
        moe_g=[w.astype(BF16) for w in moe_w_gate], moe_u=[w.astype(BF16) for w in moe_w_up],
        moe_d=[w.astype(BF16) for w in moe_w_down],
        router=[(jnp.pad(w_router[i], ((0, 0), (0, pad_e))).astype(BF16),
                 jnp.pad(b_router[i].astype(F32), (0, pad_e), constant_values=NEG)[None]) for i in range(n_moe)])
    past_len = page_table.shape[1] * PAGE_SIZE
    y_p, y_s = x_prompt, x_sample
    tables = rel_bias_tables(rel_bias, x_prompt.shape[1])
    step_tables = step_bias_tables(rel_bias, past_len, cache_win_kv.shape[2], past_len // CMP_STRIDE)
    positions_minor = (0, 1, 3, 4, 5, 2)
    caches = tuple(jnp.transpose(c, positions_minor) for c in (cache_cmp_kv, cache_sel_kv, cache_win_kv))
    p_list, s_list = [], []
    for l in range(depth):
        packed = dict(proj=pack_projection(w_in[l], q_norm[l], k_norm[l, 1], k_norm[l, 2]),
                      cmp=pack_compress(w_cmp1[l], w_cmp2[l], cmp_pe[l], k_norm[l, 0]),
                      mix=pack_mixer(conv_w[l], conv_b[l], rg_w_a[l], rg_b_a[l], rg_w_x[l], rg_b_x[l], rg_lambda[l],
                                     w_pool[l], pool_scale[l]))
        y_p, st_p = prompt_mixer(y_p, l, P, packed, tables)
        y_p = ffn_layer(y_p, l, P, W)
        p_list.append(st_p)
        y_s, st_s = sample_mixer(y_s, l, P, packed, past_len, state_conv[l], state_rg_h[l], state_pool[l],
                                 caches, cache_win_kv[l], page_table, step_tables)
        y_s = ffn_layer(y_s, l, P, W)
        s_list.append(st_s)
    p_cmp_kv, p_sel_kv, p_win_kv, p_conv, p_h, p_pool = [jnp.stack(a) for a in zip(*p_list)]
    s_cmp_kv, s_sel_kv, s_win_kv, s_conv, s_h, s_pool = [jnp.stack(a) for a in zip(*s_list)]
    return (y_p, y_s, p_cmp_kv, p_sel_kv, p_win_kv, p_conv, p_h, p_pool,
            s_cmp_kv, s_sel_kv, s_win_kv, s_conv, s_h, s_pool)
```

```python
import math
import functools

import jax
import jax.numpy as jnp
import numpy as np
from jax import lax
from jax.experimental import pallas as pl
from jax.experimental.pallas import tpu as pltpu

D_MODEL = 1024
PAGE_SIZE = 128
F32 = jnp.float32
BF16 = jnp.bfloat16
EPS = 1e-6
NEG = -1e30
FORCE = 1e4
D_RNN = 512
CONV_W = 4
RG_C = 8.0
N_HEADS = 8
KV_HEADS = 2
HPG = N_HEADS // KV_HEADS
HEAD_DIM = 64
L_CMP = 32
CMP_STRIDE = 16
CMP_HIDDEN = 256
SEL_BLOCK = 64
N_SELECT = 16
WINDOW = 512
Q_BLOCK = 128
D_POOL = 512
POOL_WINDOWS = (2, 4, 8, 16)
POOL_GW = D_POOL // len(POOL_WINDOWS)
POOL_MAX = 16
REL_BUCKETS = 32
REL_MAX_DIST = 1024
N_EXPERTS = 8
KV_ROW = 2 * KV_HEADS * HEAD_DIM
SPLITS = (D_RNN, D_RNN, N_HEADS * HEAD_DIM, KV_ROW, KV_ROW, KV_ROW, 3 * N_HEADS, D_POOL, 3 * D_MODEL)

VMEM_LIMIT_BYTES = 52 * 1024 * 1024
LANES = 128
SUBLANES = 8
M_INIT = -3e38


def _pick(n, cands):
    for c in cands:
        if n % c == 0:
            return c
    return n


def _nt(a, b):
    return lax.dot_general(a, b, (((1,), (1,)), ((), ())), preferred_element_type=F32)


def _rms_rows(x, g):
    return x * lax.rsqrt(jnp.mean(x * x, axis=-1, keepdims=True) + EPS) * g


def rel_bucket(dist):
    n_exact = REL_BUCKETS // 2
    d = jnp.maximum(dist, 0)
    df = jnp.maximum(d, 1).astype(F32)
    large = n_exact + (jnp.log(df / n_exact) / math.log(REL_MAX_DIST / n_exact)
                       * (REL_BUCKETS - n_exact)).astype(jnp.int32)
    return jnp.where(d < n_exact, d, jnp.minimum(large, REL_BUCKETS - 1))


PROJ_TN = 1024
PROJ_HALF = PROJ_TN // 2
ZA_COLS = 3 * D_MODEL + 2 * D_RNN + D_POOL
Q_COLS = N_HEADS * HEAD_DIM
GN_COLS = 2 * LANES
PROJ_COLS = ZA_COLS + Q_COLS + 3 * KV_ROW + GN_COLS
N_ZA = (ZA_COLS + Q_COLS) // PROJ_TN
ZA_STORE = N_ZA * PROJ_TN
assert ZA_COLS + Q_COLS == ZA_STORE and PROJ_COLS == ZA_STORE + PROJ_TN and Q_COLS == PROJ_HALF == 2 * KV_ROW
ZA_XRG, ZA_GRG, ZA_XPOOL = 3 * D_MODEL, 3 * D_MODEL + D_RNN, 3 * D_MODEL + 2 * D_RNN


def _proj_kernel(x_ref, g_ref, w_ref, seg_ref, ng_ref, nm_ref, za_ref, q_ref, kvc_ref, kvs_ref, kvw_ref, gn_ref, xn_ref):
    j = pl.program_id(1)

    @pl.when(j == 0)
    def _():
        xn_ref[...] = _rms_rows(x_ref[...], g_ref[...]).astype(BF16)

    acc = jnp.dot(xn_ref[...], w_ref[...], preferred_element_type=F32)

    @pl.when(j < N_ZA)
    def _():
        za_ref[...] = acc

    def head_norm(half):
        cols = slice(half * PROJ_HALF, (half + 1) * PROJ_HALF)
        y = acc[:, cols]
        sq = y * y
        hi = sq.astype(BF16)
        lo = (sq - hi.astype(F32)).astype(BF16)
        ss = (jnp.dot(hi, seg_ref[...], preferred_element_type=F32)
              + jnp.dot(lo, seg_ref[...], preferred_element_type=F32))
        normed = y * lax.rsqrt(ss * (1.0 / HEAD_DIM) + EPS) * ng_ref[:, cols]
        return jnp.where(nm_ref[:, cols] > 0.5, normed, y)

    @pl.when(j == N_ZA - 1)
    def _():
        q_ref[...] = head_norm(1).astype(BF16)

    @pl.when(j == N_ZA)
    def _():
        a = head_norm(0)
        b = head_norm(1)
        kvc_ref[...] = a[:, :KV_ROW]
        kvs_ref[...] = a[:, KV_ROW:]
        kvw_ref[...] = b[:, :KV_ROW]
        gn_ref[...] = b[:, KV_ROW:]


def pack_projection(w_in, q_gain, ks_gain, kw_gain):
    cut = np.cumsum(SPLITS)[:-1].tolist()
    x_rg, g_rg, q, kv_c, kv_s, kv_w, g_nsa, x_pool, g_br = jnp.split(w_in, cut, axis=1)
    per_group = 3 * HPG
    gn = jnp.zeros((w_in.shape[0], GN_COLS), w_in.dtype)
    for g in range(KV_HEADS):
        gn = gn.at[:, g * LANES:g * LANES + per_group].set(g_nsa[:, g * per_group:(g + 1) * per_group])
    w = jnp.concatenate([g_br, x_rg, g_rg, x_pool, q, kv_c, kv_s, kv_w, gn], axis=1).astype(BF16)
    ones_v = jnp.ones((KV_HEADS * HEAD_DIM,), F32)
    zeros_v = jnp.zeros((KV_HEADS * HEAD_DIM,), F32)
    gain = jnp.concatenate([jnp.ones((ZA_COLS,), F32), jnp.tile(q_gain.astype(F32), N_HEADS) * HEAD_DIM ** -0.5,
                            jnp.ones((KV_ROW,), F32),
                            jnp.tile(ks_gain.astype(F32), KV_HEADS), ones_v,
                            jnp.tile(kw_gain.astype(F32), KV_HEADS), ones_v,
                            jnp.ones((GN_COLS,), F32)])
    mask = jnp.concatenate([jnp.zeros((ZA_COLS,), F32), jnp.ones((Q_COLS,), F32), jnp.zeros((KV_ROW,), F32),
                            ones_v, zeros_v, ones_v, zeros_v, jnp.zeros((GN_COLS,), F32)])
    return w, gain[None], mask[None]


def projection(x, norm_gain, packed):
    w, gain, mask = packed
    T = x.shape[0]
    tm = _pick(T, (1024, 512, 256, 128))
    seg = jnp.asarray((np.arange(PROJ_HALF)[:, None] // HEAD_DIM == np.arange(PROJ_HALF)[None, :] // HEAD_DIM)
                      .astype(np.float32), BF16)
    row = lambda i, j: (i, 0)
    return pl.pallas_call(
        _proj_kernel,
        grid=(T // tm, PROJ_COLS // PROJ_TN),
        in_specs=[pl.BlockSpec((tm, D_MODEL), row),
                  pl.BlockSpec((1, D_MODEL), lambda i, j: (0, 0)),
                  pl.BlockSpec((D_MODEL, PROJ_TN), lambda i, j: (0, j)),
                  pl.BlockSpec((PROJ_HALF, PROJ_HALF), lambda i, j: (0, 0)),
                  pl.BlockSpec((1, PROJ_TN), lambda i, j: (0, j)),
                  pl.BlockSpec((1, PROJ_TN), lambda i, j: (0, j))],
        out_specs=[pl.BlockSpec((tm, PROJ_TN), lambda i, j: (i, jnp.minimum(j, N_ZA - 1))),
                   pl.BlockSpec((tm, Q_COLS), row),
                   pl.BlockSpec((tm, KV_ROW), row), pl.BlockSpec((tm, KV_ROW), row), pl.BlockSpec((tm, KV_ROW), row),
                   pl.BlockSpec((tm, GN_COLS), row)],
        out_shape=[jax.ShapeDtypeStruct((T, ZA_STORE), F32), jax.ShapeDtypeStruct((T, Q_COLS), BF16),
                   jax.ShapeDtypeStruct((T, KV_ROW), F32), jax.ShapeDtypeStruct((T, KV_ROW), F32),
                   jax.ShapeDtypeStruct((T, KV_ROW), F32), jax.ShapeDtypeStruct((T, GN_COLS), F32)],
        scratch_shapes=[pltpu.VMEM((tm, D_MODEL), BF16)],
        compiler_params=pltpu.CompilerParams(dimension_semantics=("parallel", "arbitrary"),
                                             vmem_limit_bytes=VMEM_LIMIT_BYTES),
        name="projection",
    )(x, norm_gain.astype(F32)[None], w, seg, gain, mask)


CMP_PAIRS = CMP_STRIDE // 2
CMP_GW = KV_HEADS * CMP_HIDDEN


def _compress_kernel(*refs, n_pages, paged):
    if paged:
        pages = refs[1:1 + n_pages]
        eye_ref, w1_ref, pe_ref, w2_ref, kn_ref, ck_ref, cv_ref, half_ref = refs[1 + n_pages:]
        page = pages[0].shape[-1]
        m = n_pages * page // CMP_STRIDE
        for k, pg in enumerate(pages):
            for e in range(2):
                slab = pg[0, 0, e].reshape(KV_HEADS * HEAD_DIM, page).astype(BF16)
                half_ref[e, k * page:(k + 1) * page, :] = _nt(eye_ref[...], slab)
    else:
        rows_ref, w1_ref, pe_ref, w2_ref, kn_ref, ck_ref, cv_ref = refs
        m = rows_ref.shape[1] // (2 * CMP_STRIDE)
    for e, out_ref in enumerate((ck_ref, cv_ref)):
        acc = jnp.zeros((m, 2 * CMP_GW), F32)
        for p in range(CMP_PAIRS):
            def rows(s):
                if paged:
                    return half_ref[e, pl.ds(s, m, stride=CMP_STRIDE), :]
                return rows_ref[0, pl.ds(2 * s + e, m, stride=2 * CMP_STRIDE), :]
            a = jnp.concatenate([rows(2 * p), rows(2 * p + 1)], axis=1).astype(BF16)
            acc = acc + jnp.dot(a, w1_ref[e, p], preferred_element_type=F32)
        h = acc[:, :CMP_GW] + pltpu.roll(acc[:, CMP_GW:], m - 1, axis=0) + pe_ref[e]
        gl = jax.nn.gelu(h).astype(BF16)
        for g in range(KV_HEADS):
            c = jnp.dot(gl, w2_ref[e, g], preferred_element_type=F32)
            if e == 0:
                c = c * lax.rsqrt(jnp.sum(c * c, axis=1, keepdims=True) * (1.0 / HEAD_DIM) + EPS) * kn_ref[...]
            out_ref[0, g] = c.astype(out_ref.dtype)


def pack_compress(w1, w2, pe, kn):
    halves = w1.reshape(2, 2, CMP_STRIDE, HEAD_DIM, CMP_HIDDEN)
    eye = jnp.eye(KV_HEADS, dtype=w1.dtype)
    bd = jnp.einsum('ehsdf,gk->esgdhkf', halves, eye)
    w1p = bd.reshape(2, CMP_PAIRS, 2 * KV_HEADS * HEAD_DIM, 2 * CMP_GW).astype(BF16)
    pe_term = jnp.einsum('led,eldf->ef', pe, w1)
    pe_t = jnp.tile(pe_term, (1, KV_HEADS))[:, None, :].astype(F32)
    w2p = jnp.zeros((2, KV_HEADS, CMP_GW, LANES), w2.dtype)
    for g in range(KV_HEADS):
        w2p = w2p.at[:, g, g * CMP_HIDDEN:(g + 1) * CMP_HIDDEN, :HEAD_DIM].set(w2)
    knp = jnp.concatenate([kn.astype(F32), jnp.zeros((LANES - HEAD_DIM,), F32)])[None]
    return w1p, pe_t, w2p.astype(BF16), knp


def compress(rows, packed, page_table=None, layer=0):
    w1p, pe_t, w2p, knp = packed
    paged = page_table is not None
    if not paged:
        bx, n_pages = rows.shape[0], 1
        m = rows.shape[1] // (2 * CMP_STRIDE)
        data_specs = [pl.BlockSpec((1,) + rows.shape[1:], lambda b: (b, 0, 0))]
        const = lambda nd: (lambda b: (0,) * nd)
        out_map = lambda b: (b, 0, 0, 0)
        data, scratch = [rows], []
    else:
        bx, n_pages = page_table.shape
        page = rows.shape[-1]
        assert page == LANES and KV_HEADS * HEAD_DIM == LANES
        m = n_pages * page // CMP_STRIDE
        eye = jnp.asarray(np.eye(LANES, dtype=np.float32), BF16)
        const = lambda nd: (lambda b, pt: (0,) * nd)
        data_specs = [pl.BlockSpec((1, 1) + rows.shape[2:], lambda b, pt, k=k: (layer, pt[b, k], 0, 0, 0, 0))
                      for k in range(n_pages)] + [pl.BlockSpec(eye.shape, const(2))]
        out_map = lambda b, pt: (b, 0, 0, 0)
        data, scratch = [page_table] + [rows] * n_pages + [eye], [pltpu.VMEM((2, n_pages * page, LANES), F32)]
    in_specs = data_specs + [pl.BlockSpec(w1p.shape, const(4)), pl.BlockSpec(pe_t.shape, const(3)),
                             pl.BlockSpec(w2p.shape, const(4)), pl.BlockSpec(knp.shape, const(2))]
    out_spec = pl.BlockSpec((1, KV_HEADS, m, LANES), out_map)
    out_shape = jax.ShapeDtypeStruct((bx, KV_HEADS, m, LANES), BF16)
    grid_spec = pltpu.PrefetchScalarGridSpec(num_scalar_prefetch=int(paged), grid=(bx,), in_specs=in_specs,
                                             out_specs=[out_spec, out_spec], scratch_shapes=scratch)
    args = data + [w1p, pe_t, w2p, knp]
    return pl.pallas_call(
        functools.partial(_compress_kernel, n_pages=n_pages, paged=paged),
        grid_spec=grid_spec,
        out_shape=[out_shape, out_shape],
        compiler_params=pltpu.CompilerParams(dimension_semantics=("parallel",), vmem_limit_bytes=VMEM_LIMIT_BYTES),
        name="compress",
    )(*args)


MIX_TT = 512
CONV_HALO = SUBLANES
SCAN_UNROLL = 4


def _log1p(y):
    u = 1.0 + y
    return jnp.where(u == 1.0, y, jnp.log(u) * (y / jnp.where(u == 1.0, 1.0, u - 1.0)))


def _neg_expm1(x):
    t = jnp.tanh(0.5 * x)
    return -2.0 * t / (1.0 - t)


def _softplus(x):
    return jnp.maximum(x, 0.0) + _log1p(jnp.exp(-jnp.abs(x)))


def _rglru_coeffs(xc, wa_ref, ba_ref, wx_ref, bx_ref, lam_ref):
    xb = xc.astype(BF16)
    r = jax.nn.sigmoid(jnp.dot(xb, wa_ref[...], preferred_element_type=F32) + ba_ref[...])
    i = jax.nn.sigmoid(jnp.dot(xb, wx_ref[...], preferred_element_type=F32) + bx_ref[...])
    log_a = -RG_C * r * _softplus(-lam_ref[...])
    return jnp.exp(log_a), jnp.sqrt(_neg_expm1(2.0 * log_a)) * (i * xc)


def _pool_project(sums_minus, wp_ref, scale_ref):
    return jnp.dot(sums_minus.astype(BF16), wp_ref[...], preferred_element_type=F32) * scale_ref[...]


def _mixer_seq_kernel(xrg_ref, grg_ref, xpool_ref, conv0_ref, h0_ref, pool0_ref, cw_ref, cb_ref, wa_ref, ba_ref,
                      wx_ref, bx_ref, lam_ref, wp_ref, ps_ref, ya_ref, yc_ref, convn_ref, hn_ref, pooln_ref,
                      xe_ref, pe_ref, a_ref, u_ref, h_ref, carry_ref, *, start_pos):
    i = pl.program_id(1)
    tt = xrg_ref.shape[0]

    @pl.when(i == 0)
    def _():
        xe_ref[0:CONV_HALO, :] = conv0_ref[0]
        pe_ref[0:POOL_MAX, :] = pool0_ref[0]
        carry_ref[...] = h0_ref[0]

    xe_ref[CONV_HALO:CONV_HALO + tt, :] = xrg_ref[...]
    xc = cb_ref[...] + sum(cw_ref[k:k + 1, :] * xe_ref[CONV_HALO - (CONV_W - 1) + k:CONV_HALO - (CONV_W - 1) + k + tt, :]
                           for k in range(CONV_W))
    a, u = _rglru_coeffs(xc, wa_ref, ba_ref, wx_ref, bx_ref, lam_ref)
    a_ref[...] = a
    u_ref[...] = u

    row = lax.broadcasted_iota(jnp.int32, (SUBLANES, D_RNN), 0)

    def block(j, carry):
        r0 = pl.multiple_of(j * SUBLANES, SUBLANES)
        ab = a_ref[pl.ds(r0, SUBLANES), :]
        ub = u_ref[pl.ds(r0, SUBLANES), :]
        for d in (1, 2, 4):
            a_sh = jnp.where(row >= d, pltpu.roll(ab, d, axis=0), 1.0)
            u_sh = jnp.where(row >= d, pltpu.roll(ub, d, axis=0), 0.0)
            ub = ab * u_sh + ub
            ab = ab * a_sh
        hb = ab * carry + ub
        h_ref[pl.ds(r0, SUBLANES), :] = hb
        return jnp.broadcast_to(hb[SUBLANES - 1:SUBLANES, :], (SUBLANES, D_RNN))

    carry = lax.fori_loop(0, tt // SUBLANES, block, carry_ref[...], unroll=SCAN_UNROLL)
    carry_ref[...] = carry
    ya_ref[...] = (h_ref[...] * jax.nn.gelu(grg_ref[...])).astype(ya_ref.dtype)

    pe_ref[POOL_MAX:POOL_MAX + tt, :] = xpool_ref[...]
    pos = start_pos + i * tt + lax.broadcasted_iota(jnp.int32, (tt, 1), 0)
    parts = []
    for g, w in enumerate(POOL_WINDOWS):
        lanes = slice(g * POOL_GW, (g + 1) * POOL_GW)
        s = sum(pe_ref[POOL_MAX - k:POOL_MAX - k + tt, lanes] for k in range(w))
        cnt = jnp.minimum(pos + 1, w).astype(F32)
        parts.append(s / cnt - pe_ref[POOL_MAX:POOL_MAX + tt, lanes])
    yc_ref[...] = _pool_project(jnp.concatenate(parts, axis=1), wp_ref, ps_ref).astype(yc_ref.dtype)

    @pl.when(i == pl.num_programs(1) - 1)
    def _():
        convn_ref[0] = xe_ref[CONV_HALO + tt - (CONV_W - 1):CONV_HALO + tt, :]
        hn_ref[0] = carry[0:1, :]
        pooln_ref[0] = pe_ref[POOL_MAX + tt - (POOL_MAX - 1):POOL_MAX + tt, :]

    xe_ref[0:CONV_HALO, :] = xe_ref[tt:tt + CONV_HALO, :]
    pe_ref[0:POOL_MAX, :] = pe_ref[tt:tt + POOL_MAX, :]


def pack_mixer(conv_w, conv_b, w_a, b_a, w_x, b_x, lam, w_pool, scale):
    def block_diag(w):
        n, c, d = w.shape
        return jnp.einsum('ncd,nm->ncmd', w, jnp.eye(n, dtype=w.dtype)).reshape(n * c, n * d).astype(BF16)

    row = lambda v: v.astype(F32)[None]
    return (conv_w.astype(F32), row(conv_b), block_diag(w_a), row(b_a), block_diag(w_x), row(b_x), row(lam),
            block_diag(w_pool), row(scale))


def mixer_seq(za, batch, seq, packed, conv0, h0, pool0, start_pos):
    tt = min(MIX_TT, seq)
    nt = seq // tt
    conv_pad = jnp.pad(conv0.astype(F32), ((0, 0), (CONV_HALO - (CONV_W - 1), 0), (0, 0)))
    pool_pad = jnp.pad(pool0.astype(F32), ((0, 0), (1, 0), (0, 0)))
    h_pad = jnp.broadcast_to(h0.astype(F32)[:, None, :], (batch, SUBLANES, D_RNN))
    col = lambda c: pl.BlockSpec((tt, D_RNN), lambda b, i: (b * nt + i, c))
    state = lambda rows: pl.BlockSpec((1, rows, D_RNN), lambda b, i: (b, 0, 0))
    full = lambda a: pl.BlockSpec(a.shape, lambda b, i: (0,) * a.ndim)
    out_rows = pl.BlockSpec((tt, D_RNN), lambda b, i: (b * nt + i, 0))
    return pl.pallas_call(
        functools.partial(_mixer_seq_kernel, start_pos=start_pos),
        grid=(batch, nt),
        in_specs=[col(ZA_XRG // D_RNN), col(ZA_GRG // D_RNN), col(ZA_XPOOL // D_RNN), state(CONV_HALO), state(SUBLANES),
                  state(POOL_MAX)] + [full(a) for a in packed],
        out_specs=[out_rows, out_rows, state(CONV_W - 1), state(1), state(POOL_MAX - 1)],
        out_shape=[jax.ShapeDtypeStruct((batch * seq, D_RNN), BF16), jax.ShapeDtypeStruct((batch * seq, D_POOL), BF16),
                   jax.ShapeDtypeStruct((batch, CONV_W - 1, D_RNN), F32), jax.ShapeDtypeStruct((batch, 1, D_RNN), F32),
                   jax.ShapeDtypeStruct((batch, POOL_MAX - 1, D_POOL), F32)],
        scratch_shapes=[pltpu.VMEM((CONV_HALO + tt, D_RNN), F32), pltpu.VMEM((POOL_MAX + tt, D_POOL), F32),
                        pltpu.VMEM((tt, D_RNN), F32), pltpu.VMEM((tt, D_RNN), F32), pltpu.VMEM((tt, D_RNN), F32),
                        pltpu.VMEM((SUBLANES, D_RNN), F32)],
        compiler_params=pltpu.CompilerParams(dimension_semantics=("parallel", "arbitrary"),
                                             vmem_limit_bytes=VMEM_LIMIT_BYTES),
        name="mixer_seq",
    )(za, za, za, conv_pad, h_pad, pool_pad, *packed)


def _mixer_step_kernel(xrg_ref, grg_ref, xpool_ref, conv_ref, h0_ref, pool_ref, cw_ref, cb_ref, wa_ref, ba_ref,
                       wx_ref, bx_ref, lam_ref, wp_ref, ps_ref, ya_ref, yc_ref, hn_ref, *, start_pos):
    x = xrg_ref[...]
    xc = cb_ref[...] + cw_ref[CONV_W - 1:CONV_W, :] * x + sum(cw_ref[k:k + 1, :] * conv_ref[k] for k in range(CONV_W - 1))
    a, u = _rglru_coeffs(xc, wa_ref, ba_ref, wx_ref, bx_ref, lam_ref)
    h = a * h0_ref[...] + u
    hn_ref[...] = h
    ya_ref[...] = (h * jax.nn.gelu(grg_ref[...])).astype(ya_ref.dtype)
    xp = xpool_ref[...]
    parts = []
    for g, w in enumerate(POOL_WINDOWS):
        lanes = slice(g * POOL_GW, (g + 1) * POOL_GW)
        s = xp[:, lanes] + sum(pool_ref[POOL_MAX - 1 - k][:, lanes] for k in range(1, w))
        parts.append(s / float(min(start_pos + 1, w)) - xp[:, lanes])
    yc_ref[...] = _pool_project(jnp.concatenate(parts, axis=1), wp_ref, ps_ref).astype(yc_ref.dtype)


def mixer_step(za, packed, conv_state, h0, pool_state, start_pos):
    batch = za.shape[0]
    conv_t = jnp.swapaxes(conv_state.astype(F32), 0, 1)
    pool_t = jnp.swapaxes(pool_state.astype(F32), 0, 1)
    col = lambda c: pl.BlockSpec((batch, D_RNN), lambda i: (0, c))
    full = lambda a: pl.BlockSpec(a.shape, lambda i: (0,) * a.ndim)
    rows = pl.BlockSpec((batch, D_RNN), lambda i: (0, 0))
    return pl.pallas_call(
        functools.partial(_mixer_step_kernel, start_pos=start_pos),
        grid=(1,),
        in_specs=[col(ZA_XRG // D_RNN), col(ZA_GRG // D_RNN), col(ZA_XPOOL // D_RNN), full(conv_t), rows, full(pool_t)]
        + [full(a) for a in packed],
        out_specs=[rows, rows, rows],
        out_shape=[jax.ShapeDtypeStruct((batch, D_RNN), BF16), jax.ShapeDtypeStruct((batch, D_POOL), BF16),
                   jax.ShapeDtypeStruct((batch, D_RNN), F32)],
        compiler_params=pltpu.CompilerParams(vmem_limit_bytes=VMEM_LIMIT_BYTES),
        name="mixer_step",
    )(za, za, za, conv_t, h0.astype(F32), pool_t, *packed)


QB = Q_BLOCK
ROWS = HPG * QB
N_WIN_TILES = WINDOW // QB + 1
N_SEL_BIAS = REL_MAX_DIST // QB + 2
SEL_SPAN = 8
KV_CHUNK = 512


def _nsa_prompt_kernel(q_ref, ck_ref, cv_ref, kvs_ref, kvw_ref, gate_ref, bc_ref, tbs_ref, tbw_ref,
                       ovl_ref, eye_ref, pq_ref, pk_ref, pv_ref, onehot_ref, o_ref,
                       acc_ref, m_ref, qa_ref, comb_ref, ks_ref, vs_ref, kw_ref, vw_ref, cnt_ref, *, n_cmp, n_blk):
    qb = pl.program_id(2)
    t0 = qb * QB
    ncp = ck_ref.shape[2]
    seq = kvs_ref.shape[1]
    lane_row = lax.broadcasted_iota(jnp.int32, (1, LANES), 1)

    @pl.when(qb == 0)
    def _():
        ones_hi = jnp.where(lane_row >= HEAD_DIM, 1.0, 0.0)
        kw_ref[0:WINDOW, :] = jnp.broadcast_to(jnp.where(lane_row == HEAD_DIM, 1.0, 0.0), (WINDOW, LANES)).astype(BF16)
        vw_ref[0:WINDOW, :] = jnp.zeros((WINDOW, LANES), BF16)

        def stage(c, carry):
            r = pl.multiple_of(c * KV_CHUNK, KV_CHUNK)
            sel = kvs_ref[0, pl.ds(r, KV_CHUNK), :].astype(BF16)
            win = kvw_ref[0, pl.ds(r, KV_CHUNK), :].astype(BF16)
            ks_ref[pl.ds(r, KV_CHUNK), :] = (jnp.dot(sel, pk_ref[0], preferred_element_type=F32)
                                             + onehot_ref[pl.ds(r, KV_CHUNK), :].astype(F32)).astype(BF16)
            vs_ref[pl.ds(r, KV_CHUNK), :] = (jnp.dot(sel, pv_ref[0], preferred_element_type=F32) + ones_hi).astype(BF16)
            kw_ref[pl.ds(WINDOW + r, KV_CHUNK), :] = jnp.dot(win, pk_ref[0], preferred_element_type=F32).astype(BF16)
            vw_ref[pl.ds(WINDOW + r, KV_CHUNK), :] = (jnp.dot(win, pv_ref[0], preferred_element_type=F32)
                                                      + ones_hi).astype(BF16)
            return carry

        lax.fori_loop(0, seq // KV_CHUNK, stage, 0)

    q4 = jnp.dot(q_ref[0], pq_ref[...], preferred_element_type=F32)
    q3 = jnp.concatenate([q4[:, j * LANES:(j + 1) * LANES] for j in range(HPG)], axis=0)
    q = q3.astype(BF16)
    sig = jax.nn.sigmoid(gate_ref[0])
    gates = [jnp.concatenate([sig[:, 3 * j + c:3 * j + c + 1] for j in range(HPG)], axis=0) for c in range(3)]

    lc = _nt(q, ck_ref[0, 0]) + bc_ref[0].reshape(ROWS, ncp)
    tok = t0 + lax.broadcasted_iota(jnp.int32, (HPG, QB, ncp), 1).reshape(ROWS, ncp)
    col = lax.broadcasted_iota(jnp.int32, (ROWS, ncp), 1)
    valid = (tok >= col * CMP_STRIDE + (L_CMP - 1)) & (col < n_cmp)
    lc = jnp.where(valid, lc, NEG)
    mx = jnp.max(lc, axis=1, keepdims=True)
    p = jnp.where(valid, jnp.exp(lc - mx), 0.0)
    pc = p / jnp.maximum(jnp.sum(p, axis=1, keepdims=True), 1e-30)
    comb_ref[...] = gates[0] * jnp.dot(pc.astype(BF16), cv_ref[0, 0], preferred_element_type=F32)

    pcs = pc[0:QB] + pc[QB:2 * QB] + pc[2 * QB:3 * QB] + pc[3 * QB:4 * QB]
    hi = pcs.astype(BF16)
    lo = (pcs - hi.astype(F32)).astype(BF16)
    imp = _nt(ovl_ref[...], hi) + _nt(ovl_ref[...], lo)
    blk = lax.broadcasted_iota(jnp.int32, (n_blk, QB), 0)
    cur = (t0 + lax.broadcasted_iota(jnp.int32, (n_blk, QB), 1)) // SEL_BLOCK
    forced = (blk == 0) | (blk == cur) | (blk == cur - 1)
    score = jnp.where(blk > cur, -1.0, jnp.where(forced, FORCE, imp))
    chunks = [score[r:r + SUBLANES] for r in range(0, n_blk, SUBLANES)]
    sub = lax.broadcasted_iota(jnp.int32, (SUBLANES, QB), 0)
    cnt_ref[...] = jnp.zeros((n_blk, QB), F32)
    last_blk = (t0 + QB - 1) // SEL_BLOCK
    for mc in range(0, n_blk, SUBLANES):
        @pl.when(mc <= last_blk)
        def _(mc=mc):
            for r, ch in enumerate(chunks):
                first = r * SUBLANES
                part = jnp.zeros((SUBLANES, QB), F32)
                for m in range(mc, mc + SUBLANES):
                    row = jnp.broadcast_to(score[m:m + 1, :], (SUBLANES, QB))
                    if first > m:
                        beats = jnp.where(row >= ch, 1.0, 0.0)
                    elif first + SUBLANES - 1 < m:
                        beats = jnp.where(row > ch, 1.0, 0.0)
                    else:
                        beats = jnp.where(sub + first > m, jnp.where(row >= ch, 1.0, 0.0), jnp.where(row > ch, 1.0, 0.0))
                    part = part + beats
                cnt_ref[first:first + SUBLANES, :] += part
    sel_neg = jnp.where(cnt_ref[...] < float(min(N_SELECT, n_blk)), 0.0, NEG)
    pieces = [jnp.zeros((HEAD_DIM, QB), F32), sel_neg]
    if n_blk < HEAD_DIM:
        pieces.append(jnp.zeros((HEAD_DIM - n_blk, QB), F32))
    placed_t = jnp.concatenate(pieces, axis=0).astype(BF16)
    placed = _nt(eye_ref[...], placed_t)
    qa_ref[...] = (q3 + jnp.concatenate([placed] * HPG, axis=0)).astype(BF16)

    m_ref[...] = jnp.full((ROWS, LANES), M_INIT, F32)
    acc_ref[...] = jnp.zeros((ROWS, LANES), F32)
    n_bias = tbs_ref.shape[1]

    span = SEL_SPAN * QB
    n_spans = qb // SEL_SPAN + 1

    def body(kk, carry):
        off = pl.multiple_of(kk * span, span)
        s = _nt(qa_ref[...], ks_ref[pl.ds(off, span), :])
        parts = []
        for u in range(SEL_SPAN):
            idx = jnp.clip(qb - (kk * SEL_SPAN + u), -1, n_bias - 2) + 1
            parts.append(s[:, u * QB:(u + 1) * QB] + tbs_ref[:, pl.ds(idx, 1)].reshape(ROWS, LANES))
        tile_max = functools.reduce(jnp.maximum, parts)
        m_old = m_ref[...]
        m_new = jnp.maximum(m_old, jnp.max(tile_max, axis=1, keepdims=True))
        alpha = jnp.exp(m_old - m_new)
        pr = jnp.concatenate([jnp.exp(x - m_new).astype(BF16) for x in parts], axis=1)
        acc_ref[...] = alpha * acc_ref[...] + jnp.dot(pr, vs_ref[pl.ds(off, SEL_SPAN * QB), :],
                                                      preferred_element_type=F32)
        m_ref[...] = m_new
        return carry

    lax.fori_loop(0, n_spans, body, 0)
    acc = acc_ref[...]
    comb_ref[...] += gates[1] * (acc / pltpu.roll(acc, HEAD_DIM, axis=1))

    qw = (q3 + jnp.where(lane_row == HEAD_DIM, NEG, 0.0)).astype(BF16)
    w_off = pl.multiple_of(t0, QB)
    sw = _nt(qw, kw_ref[pl.ds(w_off, WINDOW + QB), :]) + tbw_ref[...].reshape(ROWS, WINDOW + QB)
    pw = jnp.exp(sw - jnp.max(sw, axis=1, keepdims=True))
    accw = jnp.dot(pw.astype(BF16), vw_ref[pl.ds(w_off, WINDOW + QB), :], preferred_element_type=F32)
    comb_ref[...] += gates[2] * (accw / pltpu.roll(accw, HEAD_DIM, axis=1))

    comb = comb_ref[...]
    lane = lax.broadcasted_iota(jnp.int32, (QB, LANES), 1)
    for half in range(HPG // 2):
        a = comb[(2 * half) * QB:(2 * half + 1) * QB]
        b = comb[(2 * half + 1) * QB:(2 * half + 2) * QB]
        o_ref[0, :, half * LANES:(half + 1) * LANES] = jnp.where(lane < HEAD_DIM, a,
                                                                 pltpu.roll(b, HEAD_DIM, axis=1)).astype(o_ref.dtype)


def _bias_by_distance(rel_bias):
    max_d = (N_SEL_BIAS + 1) * QB
    return jnp.transpose(rel_bias.astype(F32)[rel_bucket(jnp.arange(max_d))])


def _ext(tab_t, lo, hi, ok_lo=None, ok_hi=None):
    n_heads, depth = tab_t.shape
    y = np.arange(lo, hi)
    parts = [jnp.broadcast_to(tab_t[:, :1], (n_heads, int(np.sum(y < 0)))), tab_t[:, max(lo, 0):max(min(hi, depth), 0)],
             jnp.broadcast_to(tab_t[:, -1:], (n_heads, int(np.sum(y > depth - 1))))]
    arr = jnp.concatenate(parts, axis=1)
    ok = np.ones(y.shape, bool)
    if ok_lo is not None:
        ok &= y >= ok_lo
    if ok_hi is not None:
        ok &= y <= ok_hi
    return jnp.where(jnp.asarray(ok)[None], arr, NEG)


def _toeplitz_tiles(ext, lo, ks):
    w = jnp.stack([ext[:, QB * k - (QB - 1) - lo:QB * k - (QB - 1) - lo + 2 * QB] for k in ks], axis=1)
    skew = jnp.tile(w, (1, 1, QB + 1))[..., :QB * (2 * QB + 1)].reshape(w.shape[:2] + (QB, 2 * QB + 1))[..., :QB]
    return skew[..., ::-1]


def rel_bias_tables(rel_bias, seq):
    tab_t = _bias_by_distance(rel_bias)
    max_d = tab_t.shape[1]
    lo = -(2 * QB - 1)
    tbs = _toeplitz_tiles(_ext(tab_t, lo, max_d, 0, None), lo, range(-1, N_SEL_BIAS))
    tbw = _toeplitz_tiles(_ext(tab_t, lo, max_d, 0, WINDOW), lo, range(N_WIN_TILES - 1, -1, -1))
    tbw = jnp.transpose(tbw, (0, 2, 1, 3)).reshape(N_HEADS, QB, N_WIN_TILES * QB)
    nqb = seq // QB
    ncp = seq // CMP_STRIDE
    per_qb = QB // CMP_STRIDE
    width = ncp + per_qb * (nqb - 1)
    c0 = QB * (nqb - 1) - (L_CMP - 1)
    ext = _ext(tab_t, c0 - CMP_STRIDE * (width - 1), c0 + QB)
    n_z = width + per_qb - 1
    f = ext[:, :CMP_STRIDE * n_z].reshape(N_HEADS, n_z, CMP_STRIDE)[:, ::-1]
    v = jnp.concatenate([jnp.swapaxes(f[:, per_qb - 1 - a:per_qb - 1 - a + width], 1, 2) for a in range(per_qb)],
                        axis=1)
    bc = jnp.stack([v[:, :, per_qb * (nqb - 1 - b):per_qb * (nqb - 1 - b) + ncp] for b in range(nqb)])
    return tbs, tbw, bc


def nsa_prompt_pallas(q, ck, cv, kv_sel, kv_win, gn, tables):
    B, S = q.shape[:2]
    assert S % (SEL_SPAN * QB) == 0 and S % KV_CHUNK == 0 and S // SEL_BLOCK <= HEAD_DIM
    tbs, tbw, bc = tables
    nqb = S // QB
    ncp = S // CMP_STRIDE
    n_cmp = ncp - 1
    n_blk = S // SEL_BLOCK
    c0 = np.arange(ncp)[None, :] * CMP_STRIDE
    s0 = np.arange(n_blk)[:, None] * SEL_BLOCK
    ovl_t = ((c0 <= s0 + SEL_BLOCK - 1) & (c0 + L_CMP - 1 >= s0) & (np.arange(ncp)[None, :] < n_cmp))
    ovl_t = jnp.asarray(ovl_t.astype(np.float32), BF16)
    eye = jnp.asarray(np.eye(QB, dtype=np.float32), BF16)
    gw = HPG * HEAD_DIM
    pq = np.zeros((gw, HPG * LANES), np.float32)
    pq[np.arange(gw), (np.arange(gw) // HEAD_DIM) * LANES + np.arange(gw) % HEAD_DIM] = 1.0
    pk = np.zeros((KV_HEADS, KV_ROW, LANES), np.float32)
    pv = np.zeros((KV_HEADS, KV_ROW, LANES), np.float32)
    for g in range(KV_HEADS):
        pk[g, g * HEAD_DIM + np.arange(HEAD_DIM), np.arange(HEAD_DIM)] = 1.0
        pv[g, (KV_HEADS + g) * HEAD_DIM + np.arange(HEAD_DIM), np.arange(HEAD_DIM)] = 1.0
    onehot = np.zeros((S, LANES), np.float32)
    onehot[np.arange(S), HEAD_DIM + np.arange(S) // SEL_BLOCK] = 1.0

    kv_spec = pl.BlockSpec((1, S, KV_ROW), lambda b, g, i: (b, 0, 0))
    cmp_spec = pl.BlockSpec((1, 1, ncp, LANES), lambda b, g, i: (b, g, 0, 0))
    const2 = lambda b, g, i: (0, 0)
    return pl.pallas_call(
        functools.partial(_nsa_prompt_kernel, n_cmp=n_cmp, n_blk=n_blk),
        grid=(B, KV_HEADS, nqb),
        in_specs=[
            pl.BlockSpec((1, QB, gw), lambda b, g, i: (b, i, g)),
            cmp_spec, cmp_spec, kv_spec, kv_spec,
            pl.BlockSpec((1, QB, LANES), lambda b, g, i: (b, i, g)),
            pl.BlockSpec((1, HPG, QB, ncp), lambda b, g, i: (i, g, 0, 0)),
            pl.BlockSpec((HPG, N_SEL_BIAS + 1, QB, QB), lambda b, g, i: (g, 0, 0, 0)),
            pl.BlockSpec((HPG, QB, N_WIN_TILES * QB), lambda b, g, i: (g, 0, 0)),
            pl.BlockSpec((n_blk, ncp), const2),
            pl.BlockSpec((QB, QB), const2),
            pl.BlockSpec((gw, HPG * LANES), const2),
            pl.BlockSpec((1, KV_ROW, LANES), lambda b, g, i: (g, 0, 0)),
            pl.BlockSpec((1, KV_ROW, LANES), lambda b, g, i: (g, 0, 0)),
            pl.BlockSpec((S, LANES), const2),
        ],
        out_specs=pl.BlockSpec((1, QB, gw), lambda b, g, i: (b, i, g)),
        out_shape=jax.ShapeDtypeStruct((B, S, N_HEADS * HEAD_DIM), BF16),
        scratch_shapes=[pltpu.VMEM((ROWS, LANES), F32), pltpu.VMEM((ROWS, LANES), F32),
                        pltpu.VMEM((ROWS, LANES), BF16), pltpu.VMEM((ROWS, LANES), F32),
                        pltpu.VMEM((S, LANES), BF16), pltpu.VMEM((S, LANES), BF16),
                        pltpu.VMEM((S + WINDOW, LANES), BF16), pltpu.VMEM((S + WINDOW, LANES), BF16),
                        pltpu.VMEM((n_blk, QB), F32)],
        compiler_params=pltpu.CompilerParams(dimension_semantics=("parallel", "parallel", "arbitrary"),
                                             vmem_limit_bytes=VMEM_LIMIT_BYTES),
        name="nsa_prompt",
    )(q, ck, cv, kv_sel, kv_win, gn, bc, tbs, tbw, ovl_t, eye, jnp.asarray(pq, BF16), jnp.asarray(pk, BF16),
      jnp.asarray(pv, BF16), jnp.asarray(onehot, BF16))


STEP_ROWS = SUBLANES
RANK_LANES = 2 * LANES


def _split3(x):
    hi = x.astype(BF16)
    r1 = x - hi.astype(F32)
    mid = r1.astype(BF16)
    return hi, mid, (r1 - mid.astype(F32)).astype(BF16)


def _nsa_step_kernel(*refs, n_pages, n_cmp):
    pages = refs[1:1 + n_pages]
    (q_ref, ck_ref, cv_ref, kvs_ref, kvw_ref, win_ref, gn_ref, bcs_ref, bsel_ref, bwin_ref, ovl_ref, oh_ref,
     pq_ref, pk_ref, pv_ref, o_ref, kc_ref, vs_ref, kw_ref, vw_ref) = refs[1 + n_pages:]
    page = pages[0].shape[-1]
    past = n_pages * page
    wb = win_ref.shape[-1]
    ncp = ck_ref.shape[2]
    first_row = lax.broadcasted_iota(jnp.int32, (LANES, 1), 0) == 0

    @pl.when(pl.program_id(0) == 0)
    def _():
        for g in range(KV_HEADS):
            kc_ref[g, HEAD_DIM:LANES, :] = jnp.zeros((LANES - HEAD_DIM, past + LANES), BF16)
            kc_ref[g, LANES:2 * LANES, 0:past] = oh_ref[...]
            kc_ref[g, LANES:2 * LANES, past:past + LANES] = jnp.zeros((LANES, LANES), BF16)
            vs_ref[g, HEAD_DIM:LANES, :] = jnp.ones((LANES - HEAD_DIM, past + LANES), BF16)
            kw_ref[g, HEAD_DIM:LANES, :] = jnp.zeros((LANES - HEAD_DIM, wb + LANES), BF16)
            vw_ref[g, HEAD_DIM:LANES, :] = jnp.ones((LANES - HEAD_DIM, wb + LANES), BF16)

    def place_new(row_ref, c0, k_dst, v_dst):
        new = jnp.where(first_row, row_ref[0], 0.0).astype(BF16)
        for g in range(KV_HEADS):
            k_dst[g, 0:HEAD_DIM, c0:c0 + LANES] = _nt(pk_ref[g], new)[0:HEAD_DIM].astype(BF16)
            v_dst[g, 0:HEAD_DIM, c0:c0 + LANES] = _nt(pv_ref[g], new)[0:HEAD_DIM].astype(BF16)

    for k, pg in enumerate(pages):
        for g in range(KV_HEADS):
            kc_ref[g, 0:HEAD_DIM, k * page:(k + 1) * page] = pg[0, 0, 0, g].astype(BF16)
            vs_ref[g, 0:HEAD_DIM, k * page:(k + 1) * page] = pg[0, 0, 1, g].astype(BF16)
    place_new(kvs_ref, past, kc_ref, vs_ref)
    for g in range(KV_HEADS):
        kw_ref[g, 0:HEAD_DIM, 0:wb] = win_ref[0, 0, 0, g].astype(BF16)
        vw_ref[g, 0:HEAD_DIM, 0:wb] = win_ref[0, 0, 1, g].astype(BF16)
    place_new(kvw_ref, wb, kw_ref, vw_ref)

    sig = jax.nn.sigmoid(gn_ref[0])
    lane = lax.broadcasted_iota(jnp.int32, (1, LANES), 1)
    rr = lax.broadcasted_iota(jnp.int32, (RANK_LANES, RANK_LANES), 0)
    cc = lax.broadcasted_iota(jnp.int32, (RANK_LANES, RANK_LANES), 1)
    ones_sq = jnp.ones((RANK_LANES, RANK_LANES), BF16)
    pad_rows = jnp.zeros((STEP_ROWS - HPG, LANES), F32)
    for g in range(KV_HEADS):
        gw = HPG * HEAD_DIM
        q4 = jnp.dot(q_ref[0][:, g * gw:(g + 1) * gw], pq_ref[...], preferred_element_type=F32)
        q8 = jnp.concatenate([q4[:, j * LANES:(j + 1) * LANES] for j in range(HPG)] + [pad_rows], axis=0)
        gates = [jnp.concatenate([sig[:, g * LANES + 3 * j + c:g * LANES + 3 * j + c + 1] for j in range(HPG)]
                                 + [pad_rows[:, 0:1]], axis=0) for c in range(3)]

        lc = _nt(q8.astype(BF16), ck_ref[0, g]) + bcs_ref[g]
        col = lax.broadcasted_iota(jnp.int32, (STEP_ROWS, ncp), 1)
        valid = col < n_cmp
        lc = jnp.where(valid, lc, NEG)
        p = jnp.where(valid, jnp.exp(lc - jnp.max(lc, axis=1, keepdims=True)), 0.0)
        pc = p / jnp.maximum(jnp.sum(p, axis=1, keepdims=True), 1e-30)
        comb = gates[0] * jnp.dot(pc.astype(BF16), cv_ref[0, g], preferred_element_type=F32)

        pcs = jnp.broadcast_to(jnp.sum(pc[0:HPG], axis=0, keepdims=True), (STEP_ROWS, ncp))
        hi = pcs.astype(BF16)
        lo = (pcs - hi.astype(F32)).astype(BF16)
        imp = (jnp.dot(hi, ovl_ref[...], preferred_element_type=F32)
               + jnp.dot(lo, ovl_ref[...], preferred_element_type=F32))[0:1]
        blk_id = lax.broadcasted_iota(jnp.int32, (1, RANK_LANES), 1)
        cur = past // SEL_BLOCK
        forced = (blk_id == 0) | (blk_id == cur) | (blk_id == cur - 1)
        score = jnp.where(blk_id > cur, -1.0, jnp.where(forced, FORCE, imp))
        s_n = jnp.broadcast_to(score, (RANK_LANES, RANK_LANES))
        diag = jnp.where(rr == cc, s_n, 0.0)
        s_m = sum(_nt(part, ones_sq) for part in _split3(diag))
        beats = jnp.where(rr < cc, jnp.where(s_m >= s_n, 1.0, 0.0), jnp.where(s_m > s_n, 1.0, 0.0))
        rank = jnp.sum(beats, axis=0, keepdims=True)
        sel_neg = jnp.where(rank < float(N_SELECT), 0.0, NEG)

        qa = jnp.concatenate([q8, jnp.broadcast_to(sel_neg[:, :LANES], (STEP_ROWS, LANES))], axis=1).astype(BF16)
        s_past = jnp.dot(qa, kc_ref[g, :, 0:past], preferred_element_type=F32) + bsel_ref[g, :, 0:past]
        s_new = (jnp.dot(qa, kc_ref[g, :, past:past + LANES], preferred_element_type=F32)
                 + bsel_ref[g, :, past:past + LANES] + sel_neg[:, LANES:LANES + 1])
        mx = jnp.maximum(jnp.max(s_past, axis=1, keepdims=True), jnp.max(s_new, axis=1, keepdims=True))
        acc = (_nt(jnp.exp(s_past - mx).astype(BF16), vs_ref[g, :, 0:past])
               + _nt(jnp.exp(s_new - mx).astype(BF16), vs_ref[g, :, past:past + LANES]))
        comb = comb + gates[1] * (acc / pltpu.roll(acc, HEAD_DIM, axis=1))

        sw = jnp.dot(q8.astype(BF16), kw_ref[g], preferred_element_type=F32) + bwin_ref[g]
        pw = jnp.exp(sw - jnp.max(sw, axis=1, keepdims=True))
        accw = _nt(pw.astype(BF16), vw_ref[g])
        comb = comb + gates[2] * (accw / pltpu.roll(accw, HEAD_DIM, axis=1))

        for half in range(HPG // 2):
            a = comb[2 * half:2 * half + 1]
            b = comb[2 * half + 1:2 * half + 2]
            o_ref[0, :, g * gw + half * LANES:g * gw + (half + 1) * LANES] = jnp.where(
                lane < HEAD_DIM, a, pltpu.roll(b, HEAD_DIM, axis=1)).astype(o_ref.dtype)


def step_bias_tables(rel_bias, past, wb, ncp):
    tab_t = _bias_by_distance(rel_bias)

    def rows(t):
        return jnp.pad(t.reshape(KV_HEADS, HPG, -1), ((0, 0), (0, STEP_ROWS - HPG), (0, 0)))

    d_first = past - (L_CMP - 1)
    cmp_t = _ext(tab_t, d_first - CMP_STRIDE * (ncp - 1), d_first + 1, 0, None)[:, ::CMP_STRIDE][:, ::-1]
    sel_t = _ext(tab_t, -(LANES - 1), past + 1, 0, None)[:, ::-1]
    win_t = _ext(tab_t, -(LANES - 1), wb + 1, 0, WINDOW)[:, ::-1]
    return rows(cmp_t), rows(sel_t), rows(win_t)


def nsa_step(q, ck, cv, kvs_new, kvw_new, sel_pool, win_buf, gn, page_table, tables, layer):
    B, n_pages = page_table.shape
    page = sel_pool.shape[-1]
    past = n_pages * page
    wb = win_buf.shape[-1]
    ncp = ck.shape[2]
    n_blk = past // SEL_BLOCK + 1
    assert n_blk <= LANES + 1 and page == LANES and wb % LANES == 0
    bcs, bsel, bwin = tables
    c0 = np.arange(ncp)[:, None] * CMP_STRIDE
    s0 = np.arange(RANK_LANES)[None, :] * SEL_BLOCK
    ovl = ((c0 <= s0 + SEL_BLOCK - 1) & (c0 + L_CMP - 1 >= s0) & (np.arange(ncp)[:, None] < ncp - 1)
           & (np.arange(RANK_LANES)[None, :] < n_blk))
    onehot = np.zeros((LANES, past), np.float32)
    onehot[np.arange(past) // SEL_BLOCK, np.arange(past)] = 1.0
    gw = HPG * HEAD_DIM
    pq = np.zeros((gw, HPG * LANES), np.float32)
    pq[np.arange(gw), (np.arange(gw) // HEAD_DIM) * LANES + np.arange(gw) % HEAD_DIM] = 1.0
    pk = np.zeros((KV_HEADS, LANES, KV_ROW), np.float32)
    pv = np.zeros((KV_HEADS, LANES, KV_ROW), np.float32)
    for g in range(KV_HEADS):
        pk[g, np.arange(HEAD_DIM), g * HEAD_DIM + np.arange(HEAD_DIM)] = 1.0
        pv[g, np.arange(HEAD_DIM), (KV_HEADS + g) * HEAD_DIM + np.arange(HEAD_DIM)] = 1.0
    consts = [jnp.asarray(ovl.astype(np.float32), BF16), jnp.asarray(onehot, BF16), jnp.asarray(pq, BF16),
              jnp.asarray(pk, BF16), jnp.asarray(pv, BF16)]

    row3 = lambda a: a.reshape(B, 1, a.shape[-1])
    per_b = lambda shape: pl.BlockSpec((1,) + shape, lambda b, pt: (b,) + (0,) * len(shape))
    full = lambda a: pl.BlockSpec(a.shape, lambda b, pt: (0,) * a.ndim)
    slab = (2, KV_HEADS, HEAD_DIM)
    page_specs = [pl.BlockSpec((1, 1) + slab + (page,), lambda b, pt, k=k: (layer, pt[b, k], 0, 0, 0, 0))
                  for k in range(n_pages)]
    win_spec = pl.BlockSpec((1, 1) + slab + (wb,), lambda b, pt: (layer, b, 0, 0, 0, 0))
    in_specs = page_specs + [per_b((1, Q_COLS)), per_b((KV_HEADS, ncp, LANES)), per_b((KV_HEADS, ncp, LANES)),
                             per_b((1, KV_ROW)), per_b((1, KV_ROW)), win_spec, per_b((1, GN_COLS)),
                             full(bcs), full(bsel), full(bwin)] + [full(c) for c in consts]
    grid_spec = pltpu.PrefetchScalarGridSpec(
        num_scalar_prefetch=1, grid=(B,), in_specs=in_specs, out_specs=per_b((1, Q_COLS)),
        scratch_shapes=[pltpu.VMEM((KV_HEADS, 2 * LANES, past + LANES), BF16), pltpu.VMEM((KV_HEADS, LANES, past + LANES), BF16),
                        pltpu.VMEM((KV_HEADS, LANES, wb + LANES), BF16), pltpu.VMEM((KV_HEADS, LANES, wb + LANES), BF16)])
    out = pl.pallas_call(
        functools.partial(_nsa_step_kernel, n_pages=n_pages, n_cmp=ncp - 1),
        grid_spec=grid_spec,
        out_shape=jax.ShapeDtypeStruct((B, 1, Q_COLS), BF16),
        compiler_params=pltpu.CompilerParams(dimension_semantics=("arbitrary",), vmem_limit_bytes=VMEM_LIMIT_BYTES),
        name="nsa_step",
    )(page_table, *([sel_pool] * n_pages), row3(q), ck, cv, row3(kvs_new), row3(kvw_new), win_buf, row3(gn),
      bcs, bsel, bwin, *consts)
    return out.reshape(B, Q_COLS)


def _merge_kernel(ya_ref, yb_ref, yc_ref, ga_ref, gb_ref, gc_ref, x_ref, wa_ref, wb_ref, wc_ref, wo_ref, o_ref):
    def branch(y_ref, g_ref, w_ref):
        return jax.nn.sigmoid(g_ref[...]) * jnp.dot(y_ref[...], w_ref[...], preferred_element_type=F32)

    merged = branch(ya_ref, ga_ref, wa_ref) + branch(yb_ref, gb_ref, wb_ref) + branch(yc_ref, gc_ref, wc_ref)
    o_ref[...] = x_ref[...] + jnp.dot(merged.astype(BF16), wo_ref[...], preferred_element_type=F32)


def merge(ya, yb, yc, za, x, wa, wb, wc, wo):
    T = x.shape[0]
    tm = _pick(T, (512, 256, 128))
    y_spec = pl.BlockSpec((tm, ya.shape[1]), lambda i: (i, 0))
    w_spec = pl.BlockSpec((ya.shape[1], D_MODEL), lambda i: (0, 0))
    return pl.pallas_call(
        _merge_kernel,
        grid=(T // tm,),
        in_specs=[y_spec, y_spec, y_spec,
                  pl.BlockSpec((tm, D_MODEL), lambda i: (i, 0)), pl.BlockSpec((tm, D_MODEL), lambda i: (i, 1)),
                  pl.BlockSpec((tm, D_MODEL), lambda i: (i, 2)),
                  pl.BlockSpec((tm, D_MODEL), lambda i: (i, 0)),
                  w_spec, w_spec, w_spec, pl.BlockSpec((D_MODEL, D_MODEL), lambda i: (0, 0))],
        out_specs=pl.BlockSpec((tm, D_MODEL), lambda i: (i, 0)),
        out_shape=jax.ShapeDtypeStruct((T, D_MODEL), F32),
        compiler_params=pltpu.CompilerParams(dimension_semantics=("parallel",), vmem_limit_bytes=VMEM_LIMIT_BYTES),
        name="merge",
    )(ya, yb, yc, za, za, za, x, wa, wb, wc, wo)


def _ffn_kernel(*refs, routed):
    if routed:
        h_ref, g_ref, wr_ref, br_ref, wg_ref, wu_ref, wd_ref, o_ref, xn_ref, gate_ref = refs
    else:
        h_ref, g_ref, wg_ref, wu_ref, wd_ref, o_ref, xn_ref = refs
    e = pl.program_id(1)
    f = pl.program_id(2)

    @pl.when((e == 0) & (f == 0))
    def _():
        h = h_ref[...]
        xb = _rms_rows(h, g_ref[...]).astype(BF16)
        xn_ref[...] = xb
        o_ref[...] = h
        if routed:
            logits = jnp.dot(xb, wr_ref[...], preferred_element_type=F32) + br_ref[...]
            lane = lax.broadcasted_iota(jnp.int32, logits.shape, 1)
            m1 = jnp.max(logits, axis=1, keepdims=True)
            i1 = jnp.min(jnp.where(logits == m1, lane, LANES), axis=1, keepdims=True)
            rest = jnp.where(lane == i1, M_INIT, logits)
            m2 = jnp.max(rest, axis=1, keepdims=True)
            i2 = jnp.min(jnp.where(rest == m2, lane, LANES), axis=1, keepdims=True)
            r = jnp.exp(m2 - m1)
            gate_ref[...] = jnp.where(lane == i1, 1.0 / (1.0 + r), 0.0) + jnp.where(lane == i2, r / (1.0 + r), 0.0)

    xb = xn_ref[...]
    a = jnp.dot(xb, wg_ref[0], preferred_element_type=F32)
    u = jnp.dot(xb, wu_ref[0], preferred_element_type=F32)
    t = (a * jax.nn.sigmoid(a)) * u
    y = jnp.dot(t.astype(BF16), wd_ref[0], preferred_element_type=F32)
    if routed:
        lane = lax.broadcasted_iota(jnp.int32, gate_ref.shape, 1)
        y = jnp.sum(jnp.where(lane == e, gate_ref[...], 0.0), axis=1, keepdims=True) * y
    o_ref[...] += y


def channel_mixer(h, norm_gain, wg, wu, wd, router=None):
    T = h.shape[0]
    E, _, F = wg.shape
    tm = _pick(T, (512, 256, 128))
    tf = _pick(F, (1408, 1024, 512, 256, 128))
    routed = router is not None
    row = lambda i, e, f: (i, 0)
    in_specs = [pl.BlockSpec((tm, D_MODEL), row), pl.BlockSpec((1, D_MODEL), lambda i, e, f: (0, 0))]
    args = [h, norm_gain.astype(F32)[None]]
    scratch = [pltpu.VMEM((tm, D_MODEL), BF16)]
    if routed:
        in_specs += [pl.BlockSpec((D_MODEL, LANES), lambda i, e, f: (0, 0)), pl.BlockSpec((1, LANES), lambda i, e, f: (0, 0))]
        args += list(router)
        scratch.append(pltpu.VMEM((tm, LANES), F32))
    in_specs += [pl.BlockSpec((1, D_MODEL, tf), lambda i, e, f: (e, 0, f)),
                 pl.BlockSpec((1, D_MODEL, tf), lambda i, e, f: (e, 0, f)),
                 pl.BlockSpec((1, tf, D_MODEL), lambda i, e, f: (e, f, 0))]
    args += [wg, wu, wd]
    return pl.pallas_call(
        functools.partial(_ffn_kernel, routed=routed),
        grid=(T // tm, E, F // tf),
        in_specs=in_specs,
        out_specs=pl.BlockSpec((tm, D_MODEL), row),
        out_shape=jax.ShapeDtypeStruct((T, D_MODEL), F32),
        scratch_shapes=scratch,
        compiler_params=pltpu.CompilerParams(dimension_semantics=("parallel", "arbitrary", "arbitrary"),
                                             vmem_limit_bytes=VMEM_LIMIT_BYTES),
        name="moe" if routed else "ffn",
    )(*args)


def mixer_tail(x2, za, y_a, y_b, y_c, l, P):
    return merge(y_a, y_b, y_c, za, x2, P['w_br_rg'][l].astype(BF16), P['w_br_attn'][l].astype(BF16),
                 P['w_br_pool'][l].astype(BF16), P['w_out'][l].astype(BF16))


def prompt_mixer(x, l, P, packed, tables):
    B, T, _ = x.shape
    x2 = x.reshape(B * T, D_MODEL)
    za, q, kv_c, kv_s, kv_w, gn = projection(x2, P['attn_norm'][l], packed['proj'])
    zeros = lambda rows: jnp.zeros((B, rows, D_RNN), F32)
    y_a, y_c, conv_new, h_last, pool_new = mixer_seq(za, B, T, packed['mix'], zeros(CONV_W - 1), jnp.zeros((B, D_RNN), F32),
                                                     zeros(POOL_MAX - 1), 0)
    ck, cv = compress(kv_c.reshape(B, 2 * T, LANES), packed['cmp'])
    y_b = nsa_prompt_pallas(q.reshape(B, T, Q_COLS), ck, cv, kv_s.reshape(B, T, KV_ROW), kv_w.reshape(B, T, KV_ROW),
                            gn.reshape(B, T, GN_COLS), tables)
    out = mixer_tail(x2, za, y_a, y_b.reshape(B * T, Q_COLS), y_c, l, P)
    kv_shape = (B, T, 2, KV_HEADS, HEAD_DIM)
    state = (kv_c.reshape(kv_shape), kv_s.reshape(kv_shape), kv_w.reshape(kv_shape)[:, -min(WINDOW, T):], conv_new,
             h_last[:, 0], pool_new)
    return out.reshape(B, T, D_MODEL), state


def sample_mixer(x, l, P, packed, past_len, conv_state, h0, pool_state, caches, win_buf, page_table, step_tables):
    B, T, _ = x.shape
    x2 = x.reshape(B * T, D_MODEL)
    za, q, kv_c, kv_s, kv_w, gn = projection(x2, P['attn_norm'][l], packed['proj'])
    y_a, y_c, h_new = mixer_step(za, packed['mix'], conv_state, h0, pool_state, past_len)
    cmp_pools, sel_pools, win_bufs = caches
    ck, cv = compress(cmp_pools, packed['cmp'], page_table, layer=l)
    kv_shape = (B, T, 2, KV_HEADS, HEAD_DIM)
    wb = win_buf.shape[1]
    y_b = nsa_step(q, ck, cv, kv_s, kv_w, sel_pools, win_bufs, gn, page_table, step_tables, l)
    win_new = jnp.concatenate([win_buf[:, 1:], kv_w.reshape(kv_shape)], axis=1)
    out = mixer_tail(x2, za, y_a, y_b, y_c, l, P)
    conv_new = jnp.concatenate([conv_state[:, 1:], za[:, None, ZA_XRG:ZA_XRG + D_RNN]], axis=1)
    pool_new = jnp.concatenate([pool_state[:, 1:], za[:, None, ZA_XPOOL:ZA_XPOOL + D_POOL]], axis=1)
    state = (kv_c.reshape(kv_shape), kv_s.reshape(kv_shape), win_new, conv_new, h_new, pool_new)
    return out.reshape(B, T, D_MODEL), state


def ffn_layer(x, l, P, W):
    B, T, _ = x.shape
    i = l // 2
    if l % 2 == 0:
        y = channel_mixer(x.reshape(B * T, D_MODEL), P['ffn_norm'][l], W['ffn_g'][i], W['ffn_u'][i], W['ffn_d'][i])
    else:
        y = channel_mixer(x.reshape(B * T, D_MODEL), P['ffn_norm'][l], W['moe_g'][i], W['moe_u'][i], W['moe_d'][i],
                          router=W['router'][i])
    return y.reshape(B, T, D_MODEL)


def kernel(x_prompt, x_sample, cache_cmp_kv, cache_sel_kv, cache_win_kv, state_conv, state_rg_h, state_pool,
           page_table, attn_norm, w_in, conv_w, conv_b, rg_w_a, rg_b_a, rg_w_x, rg_b_x, rg_lambda, q_norm, k_norm,
           cmp_pe, w_cmp1, w_cmp2, rel_bias, w_pool, pool_scale, w_br_rg, w_br_attn, w_br_pool, w_out, ffn_norm,
           ffn_w_gate, ffn_w_up, ffn_w_down, w_router, b_router, moe_w_gate, moe_w_up, moe_w_down):
    P = dict(attn_norm=attn_norm, conv_w=conv_w, conv_b=conv_b, rg_w_a=rg_w_a, rg_b_a=rg_b_a,
             rg_w_x=rg_w_x, rg_b_x=rg_b_x, rg_lambda=rg_lambda, w_pool=w_pool,
             pool_scale=pool_scale, w_br_rg=w_br_rg, w_br_attn=w_br_attn, w_br_pool=w_br_pool, w_out=w_out,
             ffn_norm=ffn_norm)
    depth = w_in.shape[0]
    n_moe = w_router.shape[0]
    pad_e = LANES - N_EXPERTS
    W = dict(
        ffn_g=[w[None].astype(BF16) for w in ffn_w_gate], ffn_u=[w[None].astype(BF16) for w in ffn_w_up],
        ffn_d=[w[None].astype(BF16) for w in ffn_w_down],
        moe_g=[w.astype(BF16) for w in moe_w_gate], moe_u=[w.astype(BF16) for w in moe_w_up],
        moe_d=[w.astype(BF16) for w in moe_w_down],
        router=[(jnp.pad(w_router[i], ((0, 0), (0, pad_e))).astype(BF16),
                 jnp.pad(b_router[i].astype(F32), (0, pad_e), constant_values=NEG)[None]) for i in range(n_moe)])
    past_len = page_table.shape[1] * PAGE_SIZE
    y_p, y_s = x_prompt, x_sample
    tables = rel_bias_tables(rel_bias, x_prompt.shape[1])
    step_tables = step_bias_tables(rel_bias, past_len, cache_win_kv.shape[2], past_len // CMP_STRIDE)
    positions_minor = (0, 1, 3, 4, 5, 2)
    caches = tuple(jnp.transpose(c, positions_minor) for c in (cache_cmp_kv, cache_sel_kv, cache_win_kv))
    p_list, s_list = [], []
    for l in range(depth):
        packed = dict(proj=pack_projection(w_in[l], q_norm[l], k_norm[l, 1], k_norm[l, 2]),
                      cmp=pack_compress(w_cmp1[l], w_cmp2[l], cmp_pe[l], k_norm[l, 0]),
                      mix=pack_mixer(conv_w[l], conv_b[l], rg_w_a[l], rg_b_a[l], rg_w_x[l], rg_b_x[l], rg_lambda[l],
                                     w_pool[l], pool_scale[l]))
        y_p, st_p = prompt_mixer(y_p, l, P, packed, tables)
        y_p = ffn_layer(y_p, l, P, W)
        p_list.append(st_p)
        y_s, st_s = sample_mixer(y_s, l, P, packed, past_len, state_conv[l], state_rg_h[l], state_pool[l],
                                 caches, cache_win_kv[l], page_table, step_tables)
        y_s = ffn_layer(y_s, l, P, W)
        s_list.append(st_s)
    p_cmp_kv, p_sel_kv, p_win_kv, p_conv, p_h, p_pool = [jnp.stack(a) for a in zip(*p_list)]
    s_cmp_kv, s_sel_kv, s_win_kv, s_conv, s_h, s_pool = [jnp.stack(a) for a in zip(*s_list)]
    return (y_p, y_s, p_cmp_kv, p_sel_kv, p_win_kv, p_conv, p_h, p_pool,
            s_cmp_kv, s_sel_kv, s_win_kv, s_conv, s_h, s_pool)
```

```python
import math
import functools

import jax
import jax.numpy as jnp
import numpy as np
from jax import lax
from jax.experimental import pallas as pl
from jax.experimental.pallas import tpu as pltpu

D_MODEL = 1024
PAGE_SIZE = 128
F32 = jnp.float32
BF16 = jnp.bfloat16
EPS = 1e-6
NEG = -1e30
FORCE = 1e4
D_RNN = 512
CONV_W = 4
RG_C = 8.0
N_HEADS = 8
KV_HEADS = 2
HPG = N_HEADS // KV_HEADS
HEAD_DIM = 64
L_CMP = 32
CMP_STRIDE = 16
CMP_HIDDEN = 256
SEL_BLOCK = 64
N_SELECT = 16
WINDOW = 512
Q_BLOCK = 128
D_POOL = 512
POOL_WINDOWS = (2, 4, 8, 16)
POOL_GW = D_POOL // len(POOL_WINDOWS)
POOL_MAX = 16
REL_BUCKETS = 32
REL_MAX_DIST = 1024
N_EXPERTS = 8
KV_ROW = 2 * KV_HEADS * HEAD_DIM
SPLITS = (D_RNN, D_RNN, N_HEADS * HEAD_DIM, KV_ROW, KV_ROW, KV_ROW, 3 * N_HEADS, D_POOL, 3 * D_MODEL)

VMEM_LIMIT_BYTES = 52 * 1024 * 1024
LANES = 128
SUBLANES = 8
M_INIT = -3e38


def _pick(n, cands):
    for c in cands:
        if n % c == 0:
            return c
    return n


def _nt(a, b):
    return lax.dot_general(a, b, (((1,), (1,)), ((), ())), preferred_element_type=F32)


def _rms_rows(x, g):
    return x * lax.rsqrt(jnp.mean(x * x, axis=-1, keepdims=True) + EPS) * g


def rel_bucket(dist):
    n_exact = REL_BUCKETS // 2
    d = jnp.maximum(dist, 0)
    df = jnp.maximum(d, 1).astype(F32)
    large = n_exact + (jnp.log(df / n_exact) / math.log(REL_MAX_DIST / n_exact)
                       * (REL_BUCKETS - n_exact)).astype(jnp.int32)
    return jnp.where(d < n_exact, d, jnp.minimum(large, REL_BUCKETS - 1))


PROJ_TN = 1024
PROJ_HALF = PROJ_TN // 2
ZA_COLS = 3 * D_MODEL + 2 * D_RNN + D_POOL
Q_COLS = N_HEADS * HEAD_DIM
GN_COLS = 2 * LANES
PROJ_COLS = ZA_COLS + Q_COLS + 3 * KV_ROW + GN_COLS
N_ZA = (ZA_COLS + Q_COLS) // PROJ_TN
ZA_STORE = N_ZA * PROJ_TN
assert ZA_COLS + Q_COLS == ZA_STORE and PROJ_COLS == ZA_STORE + PROJ_TN and Q_COLS == PROJ_HALF == 2 * KV_ROW
ZA_XRG, ZA_GRG, ZA_XPOOL = 3 * D_MODEL, 3 * D_MODEL + D_RNN, 3 * D_MODEL + 2 * D_RNN


def _proj_kernel(x_ref, g_ref, w_ref, seg_ref, ng_ref, nm_ref, za_ref, q_ref, kvc_ref, kvs_ref, kvw_ref, gn_ref, xn_ref):
    j = pl.program_id(1)

    @pl.when(j == 0)
    def _():
        xn_ref[...] = _rms_rows(x_ref[...], g_ref[...]).astype(BF16)

    acc = jnp.dot(xn_ref[...], w_ref[...], preferred_element_type=F32)

    @pl.when(j < N_ZA)
    def _():
        za_ref[...] = acc

    def head_norm(half):
        cols = slice(half * PROJ_HALF, (half + 1) * PROJ_HALF)
        y = acc[:, cols]
        sq = y * y
        hi = sq.astype(BF16)
        lo = (sq - hi.astype(F32)).astype(BF16)
        ss = (jnp.dot(hi, seg_ref[...], preferred_element_type=F32)
              + jnp.dot(lo, seg_ref[...], preferred_element_type=F32))
        normed = y * lax.rsqrt(ss * (1.0 / HEAD_DIM) + EPS) * ng_ref[:, cols]
        return jnp.where(nm_ref[:, cols] > 0.5, normed, y)

    @pl.when(j == N_ZA - 1)
    def _():
        q_ref[...] = head_norm(1).astype(BF16)

    @pl.when(j == N_ZA)
    def _():
        a = head_norm(0)
        b = head_norm(1)
        kvc_ref[...] = a[:, :KV_ROW]
        kvs_ref[...] = a[:, KV_ROW:]
        kvw_ref[...] = b[:, :KV_ROW]
        gn_ref[...] = b[:, KV_ROW:]


def pack_projection(w_in, q_gain, ks_gain, kw_gain):
    cut = np.cumsum(SPLITS)[:-1].tolist()
    x_rg, g_rg, q, kv_c, kv_s, kv_w, g_nsa, x_pool, g_br = jnp.split(w_in, cut, axis=1)
    per_group = 3 * HPG
    gn = jnp.zeros((w_in.shape[0], GN_COLS), w_in.dtype)
    for g in range(KV_HEADS):
        gn = gn.at[:, g * LANES:g * LANES + per_group].set(g_nsa[:, g * per_group:(g + 1) * per_group])
    w = jnp.concatenate([g_br, x_rg, g_rg, x_pool, q, kv_c, kv_s, kv_w, gn], axis=1).astype(BF16)
    ones_v = jnp.ones((KV_HEADS * HEAD_DIM,), F32)
    zeros_v = jnp.zeros((KV_HEADS * HEAD_DIM,), F32)
    gain = jnp.concatenate([jnp.ones((ZA_COLS,), F32), jnp.tile(q_gain.astype(F32), N_HEADS) * HEAD_DIM ** -0.5,
                            jnp.ones((KV_ROW,), F32),
                            jnp.tile(ks_gain.astype(F32), KV_HEADS), ones_v,
                            jnp.tile(kw_gain.astype(F32), KV_HEADS), ones_v,
                            jnp.ones((GN_COLS,), F32)])
    mask = jnp.concatenate([jnp.zeros((ZA_COLS,), F32), jnp.ones((Q_COLS,), F32), jnp.zeros((KV_ROW,), F32),
                            ones_v, zeros_v, ones_v, zeros_v, jnp.zeros((GN_COLS,), F32)])
    return w, gain[None], mask[None]


def projection(x, norm_gain, packed):
    w, gain, mask = packed
    T = x.shape[0]
    tm = _pick(T, (1024, 512, 256, 128))
    seg = jnp.asarray((np.arange(PROJ_HALF)[:, None] // HEAD_DIM == np.arange(PROJ_HALF)[None, :] // HEAD_DIM)
                      .astype(np.float32), BF16)
    row = lambda i, j: (i, 0)
    return pl.pallas_call(
        _proj_kernel,
        grid=(T // tm, PROJ_COLS // PROJ_TN),
        in_specs=[pl.BlockSpec((tm, D_MODEL), row),
                  pl.BlockSpec((1, D_MODEL), lambda i, j: (0, 0)),
                  pl.BlockSpec((D_MODEL, PROJ_TN), lambda i, j: (0, j)),
                  pl.BlockSpec((PROJ_HALF, PROJ_HALF), lambda i, j: (0, 0)),
                  pl.BlockSpec((1, PROJ_TN), lambda i, j: (0, j)),
                  pl.BlockSpec((1, PROJ_TN), lambda i, j: (0, j))],
        out_specs=[pl.BlockSpec((tm, PROJ_TN), lambda i, j: (i, jnp.minimum(j, N_ZA - 1))),
                   pl.BlockSpec((tm, Q_COLS), row),
                   pl.BlockSpec((tm, KV_ROW), row), pl.BlockSpec((tm, KV_ROW), row), pl.BlockSpec((tm, KV_ROW), row),
                   pl.BlockSpec((tm, GN_COLS), row)],
        out_shape=[jax.ShapeDtypeStruct((T, ZA_STORE), F32), jax.ShapeDtypeStruct((T, Q_COLS), BF16),
                   jax.ShapeDtypeStruct((T, KV_ROW), F32), jax.ShapeDtypeStruct((T, KV_ROW), F32),
                   jax.ShapeDtypeStruct((T, KV_ROW), F32), jax.ShapeDtypeStruct((T, GN_COLS), F32)],
        scratch_shapes=[pltpu.VMEM((tm, D_MODEL), BF16)],
        compiler_params=pltpu.CompilerParams(dimension_semantics=("parallel", "arbitrary"),
                                             vmem_limit_bytes=VMEM_LIMIT_BYTES),
        name="projection",
    )(x, norm_gain.astype(F32)[None], w, seg, gain, mask)


CMP_PAIRS = CMP_STRIDE // 2
CMP_GW = KV_HEADS * CMP_HIDDEN


def _compress_kernel(*refs, n_pages, paged):
    if paged:
        pages = refs[1:1 + n_pages]
        eye_ref, w1_ref, pe_ref, w2_ref, kn_ref, ck_ref, cv_ref, half_ref = refs[1 + n_pages:]
        page = pages[0].shape[-1]
        m = n_pages * page // CMP_STRIDE
        for k, pg in enumerate(pages):
            for e in range(2):
                slab = pg[0, 0, e].reshape(KV_HEADS * HEAD_DIM, page).astype(BF16)
                half_ref[e, k * page:(k + 1) * page, :] = _nt(eye_ref[...], slab)
    else:
        rows_ref, w1_ref, pe_ref, w2_ref, kn_ref, ck_ref, cv_ref = refs
        m = rows_ref.shape[1] // (2 * CMP_STRIDE)
    for e, out_ref in enumerate((ck_ref, cv_ref)):
        acc = jnp.zeros((m, 2 * CMP_GW), F32)
        for p in range(CMP_PAIRS):
            def rows(s):
                if paged:
                    return half_ref[e, pl.ds(s, m, stride=CMP_STRIDE), :]
                return rows_ref[0, pl.ds(2 * s + e, m, stride=2 * CMP_STRIDE), :]
            a = jnp.concatenate([rows(2 * p), rows(2 * p + 1)], axis=1).astype(BF16)
            acc = acc + jnp.dot(a, w1_ref[e, p], preferred_element_type=F32)
        h = acc[:, :CMP_GW] + pltpu.roll(acc[:, CMP_GW:], m - 1, axis=0) + pe_ref[e]
        gl = jax.nn.gelu(h).astype(BF16)
        for g in range(KV_HEADS):
            c = jnp.dot(gl, w2_ref[e, g], preferred_element_type=F32)
            if e == 0:
                c = c * lax.rsqrt(jnp.sum(c * c, axis=1, keepdims=True) * (1.0 / HEAD_DIM) + EPS) * kn_ref[...]
            out_ref[0, g] = c.astype(out_ref.dtype)


def pack_compress(w1, w2, pe, kn):
    halves = w1.reshape(2, 2, CMP_STRIDE, HEAD_DIM, CMP_HIDDEN)
    eye = jnp.eye(KV_HEADS, dtype=w1.dtype)
    bd = jnp.einsum('ehsdf,gk->esgdhkf', halves, eye)
    w1p = bd.reshape(2, CMP_PAIRS, 2 * KV_HEADS * HEAD_DIM, 2 * CMP_GW).astype(BF16)
    pe_term = jnp.einsum('led,eldf->ef', pe, w1)
    pe_t = jnp.tile(pe_term, (1, KV_HEADS))[:, None, :].astype(F32)
    w2p = jnp.zeros((2, KV_HEADS, CMP_GW, LANES), w2.dtype)
    for g in range(KV_HEADS):
        w2p = w2p.at[:, g, g * CMP_HIDDEN:(g + 1) * CMP_HIDDEN, :HEAD_DIM].set(w2)
    knp = jnp.concatenate([kn.astype(F32), jnp.zeros((LANES - HEAD_DIM,), F32)])[None]
    return w1p, pe_t, w2p.astype(BF16), knp


def compress(rows, packed, page_table=None, layer=0):
    w1p, pe_t, w2p, knp = packed
    paged = page_table is not None
    if not paged:
        bx, n_pages = rows.shape[0], 1
        m = rows.shape[1] // (2 * CMP_STRIDE)
        data_specs = [pl.BlockSpec((1,) + rows.shape[1:], lambda b: (b, 0, 0))]
        const = lambda nd: (lambda b: (0,) * nd)
        out_map = lambda b: (b, 0, 0, 0)
        data, scratch = [rows], []
    else:
        bx, n_pages = page_table.shape
        page = rows.shape[-1]
        assert page == LANES and KV_HEADS * HEAD_DIM == LANES
        m = n_pages * page // CMP_STRIDE
        eye = jnp.asarray(np.eye(LANES, dtype=np.float32), BF16)
        const = lambda nd: (lambda b, pt: (0,) * nd)
        data_specs = [pl.BlockSpec((1, 1) + rows.shape[2:], lambda b, pt, k=k: (layer, pt[b, k], 0, 0, 0, 0))
                      for k in range(n_pages)] + [pl.BlockSpec(eye.shape, const(2))]
        out_map = lambda b, pt: (b, 0, 0, 0)
        data, scratch = [page_table] + [rows] * n_pages + [eye], [pltpu.VMEM((2, n_pages * page, LANES), F32)]
    in_specs = data_specs + [pl.BlockSpec(w1p.shape, const(4)), pl.BlockSpec(pe_t.shape, const(3)),
                             pl.BlockSpec(w2p.shape, const(4)), pl.BlockSpec(knp.shape, const(2))]
    out_spec = pl.BlockSpec((1, KV_HEADS, m, LANES), out_map)
    out_shape = jax.ShapeDtypeStruct((bx, KV_HEADS, m, LANES), BF16)
    grid_spec = pltpu.PrefetchScalarGridSpec(num_scalar_prefetch=int(paged), grid=(bx,), in_specs=in_specs,
                                             out_specs=[out_spec, out_spec], scratch_shapes=scratch)
    args = data + [w1p, pe_t, w2p, knp]
    return pl.pallas_call(
        functools.partial(_compress_kernel, n_pages=n_pages, paged=paged),
        grid_spec=grid_spec,
        out_shape=[out_shape, out_shape],
        compiler_params=pltpu.CompilerParams(dimension_semantics=("parallel",), vmem_limit_bytes=VMEM_LIMIT_BYTES),
        name="compress",
    )(*args)


MIX_TT = 1024
CONV_HALO = SUBLANES
SCAN_UNROLL = 4


def _log1p(y):
    u = 1.0 + y
    return jnp.where(u == 1.0, y, jnp.log(u) * (y / jnp.where(u == 1.0, 1.0, u - 1.0)))


def _neg_expm1(x):
    t = jnp.tanh(0.5 * x)
    return -2.0 * t / (1.0 - t)


def _softplus(x):
    return jnp.maximum(x, 0.0) + _log1p(jnp.exp(-jnp.abs(x)))


def _rglru_coeffs(xc, wa_ref, ba_ref, wx_ref, bx_ref, lam_ref):
    xb = xc.astype(BF16)
    r = jax.nn.sigmoid(jnp.dot(xb, wa_ref[...], preferred_element_type=F32) + ba_ref[...])
    i = jax.nn.sigmoid(jnp.dot(xb, wx_ref[...], preferred_element_type=F32) + bx_ref[...])
    log_a = -RG_C * r * _softplus(-lam_ref[...])
    return jnp.exp(log_a), jnp.sqrt(_neg_expm1(2.0 * log_a)) * (i * xc)


def _pool_project(sums_minus, wp_ref, scale_ref):
    return jnp.dot(sums_minus.astype(BF16), wp_ref[...], preferred_element_type=F32) * scale_ref[...]


def _mixer_seq_kernel(xrg_ref, grg_ref, xpool_ref, conv0_ref, h0_ref, pool0_ref, cw_ref, cb_ref, wa_ref, ba_ref,
                      wx_ref, bx_ref, lam_ref, wp_ref, ps_ref, ya_ref, yc_ref, convn_ref, hn_ref, pooln_ref,
                      xe_ref, pe_ref, a_ref, u_ref, h_ref, carry_ref, *, start_pos):
    i = pl.program_id(1)
    tt = xrg_ref.shape[0]

    @pl.when(i == 0)
    def _():
        xe_ref[0:CONV_HALO, :] = conv0_ref[0]
        pe_ref[0:POOL_MAX, :] = pool0_ref[0]
        carry_ref[...] = h0_ref[0]

    xe_ref[CONV_HALO:CONV_HALO + tt, :] = xrg_ref[...]
    xc = cb_ref[...] + sum(cw_ref[k:k + 1, :] * xe_ref[CONV_HALO - (CONV_W - 1) + k:CONV_HALO - (CONV_W - 1) + k + tt, :]
                           for k in range(CONV_W))
    a, u = _rglru_coeffs(xc, wa_ref, ba_ref, wx_ref, bx_ref, lam_ref)
    a_ref[...] = a
    u_ref[...] = u

    row = lax.broadcasted_iota(jnp.int32, (SUBLANES, D_RNN), 0)

    def block(j, carry):
        r0 = pl.multiple_of(j * SUBLANES, SUBLANES)
        ab = a_ref[pl.ds(r0, SUBLANES), :]
        ub = u_ref[pl.ds(r0, SUBLANES), :]
        for d in (1, 2, 4):
            a_sh = jnp.where(row >= d, pltpu.roll(ab, d, axis=0), 1.0)
            u_sh = jnp.where(row >= d, pltpu.roll(ub, d, axis=0), 0.0)
            ub = ab * u_sh + ub
            ab = ab * a_sh
        hb = ab * carry + ub
        h_ref[pl.ds(r0, SUBLANES), :] = hb
        return jnp.broadcast_to(hb[SUBLANES - 1:SUBLANES, :], (SUBLANES, D_RNN))

    carry = lax.fori_loop(0, tt // SUBLANES, block, carry_ref[...], unroll=SCAN_UNROLL)
    carry_ref[...] = carry
    ya_ref[...] = (h_ref[...] * jax.nn.gelu(grg_ref[...])).astype(ya_ref.dtype)

    pe_ref[POOL_MAX:POOL_MAX + tt, :] = xpool_ref[...]
    pos = start_pos + i * tt + lax.broadcasted_iota(jnp.int32, (tt, 1), 0)
    parts = []
    for g, w in enumerate(POOL_WINDOWS):
        lanes = slice(g * POOL_GW, (g + 1) * POOL_GW)
        s = sum(pe_ref[POOL_MAX - k:POOL_MAX - k + tt, lanes] for k in range(w))
        cnt = jnp.minimum(pos + 1, w).astype(F32)
        parts.append(s / cnt - pe_ref[POOL_MAX:POOL_MAX + tt, lanes])
    yc_ref[...] = _pool_project(jnp.concatenate(parts, axis=1), wp_ref, ps_ref).astype(yc_ref.dtype)

    @pl.when(i == pl.num_programs(1) - 1)
    def _():
        convn_ref[0] = xe_ref[CONV_HALO + tt - (CONV_W - 1):CONV_HALO + tt, :]
        hn_ref[0] = carry[0:1, :]
        pooln_ref[0] = pe_ref[POOL_MAX + tt - (POOL_MAX - 1):POOL_MAX + tt, :]

    xe_ref[0:CONV_HALO, :] = xe_ref[tt:tt + CONV_HALO, :]
    pe_ref[0:POOL_MAX, :] = pe_ref[tt:tt + POOL_MAX, :]


def pack_mixer(conv_w, conv_b, w_a, b_a, w_x, b_x, lam, w_pool, scale):
    def block_diag(w):
        n, c, d = w.shape
        return jnp.einsum('ncd,nm->ncmd', w, jnp.eye(n, dtype=w.dtype)).reshape(n * c, n * d).astype(BF16)

    row = lambda v: v.astype(F32)[None]
    return (conv_w.astype(F32), row(conv_b), block_diag(w_a), row(b_a), block_diag(w_x), row(b_x), row(lam),
            block_diag(w_pool), row(scale))


def mixer_seq(za, batch, seq, packed, conv0, h0, pool0, start_pos):
    tt = min(MIX_TT, seq)
    nt = seq // tt
    conv_pad = jnp.pad(conv0.astype(F32), ((0, 0), (CONV_HALO - (CONV_W - 1), 0), (0, 0)))
    pool_pad = jnp.pad(pool0.astype(F32), ((0, 0), (1, 0), (0, 0)))
    h_pad = jnp.broadcast_to(h0.astype(F32)[:, None, :], (batch, SUBLANES, D_RNN))
    col = lambda c: pl.BlockSpec((tt, D_RNN), lambda b, i: (b * nt + i, c))
    state = lambda rows: pl.BlockSpec((1, rows, D_RNN), lambda b, i: (b, 0, 0))
    full = lambda a: pl.BlockSpec(a.shape, lambda b, i: (0,) * a.ndim)
    out_rows = pl.BlockSpec((tt, D_RNN), lambda b, i: (b * nt + i, 0))
    return pl.pallas_call(
        functools.partial(_mixer_seq_kernel, start_pos=start_pos),
        grid=(batch, nt),
        in_specs=[col(ZA_XRG // D_RNN), col(ZA_GRG // D_RNN), col(ZA_XPOOL // D_RNN), state(CONV_HALO), state(SUBLANES),
                  state(POOL_MAX)] + [full(a) for a in packed],
        out_specs=[out_rows, out_rows, state(CONV_W - 1), state(1), state(POOL_MAX - 1)],
        out_shape=[jax.ShapeDtypeStruct((batch * seq, D_RNN), BF16), jax.ShapeDtypeStruct((batch * seq, D_POOL), BF16),
                   jax.ShapeDtypeStruct((batch, CONV_W - 1, D_RNN), F32), jax.ShapeDtypeStruct((batch, 1, D_RNN), F32),
                   jax.ShapeDtypeStruct((batch, POOL_MAX - 1, D_POOL), F32)],
        scratch_shapes=[pltpu.VMEM((CONV_HALO + tt, D_RNN), F32), pltpu.VMEM((POOL_MAX + tt, D_POOL), F32),
                        pltpu.VMEM((tt, D_RNN), F32), pltpu.VMEM((tt, D_RNN), F32), pltpu.VMEM((tt, D_RNN), F32),
                        pltpu.VMEM((SUBLANES, D_RNN), F32)],
        compiler_params=pltpu.CompilerParams(dimension_semantics=("parallel", "arbitrary"),
                                             vmem_limit_bytes=VMEM_LIMIT_BYTES),
        name="mixer_seq",
    )(za, za, za, conv_pad, h_pad, pool_pad, *packed)


def _mixer_step_kernel(xrg_ref, grg_ref, xpool_ref, conv_ref, h0_ref, pool_ref, cw_ref, cb_ref, wa_ref, ba_ref,
                       wx_ref, bx_ref, lam_ref, wp_ref, ps_ref, ya_ref, yc_ref, hn_ref, *, start_pos):
    x = xrg_ref[...]
    xc = cb_ref[...] + cw_ref[CONV_W - 1:CONV_W, :] * x + sum(cw_ref[k:k + 1, :] * conv_ref[k] for k in range(CONV_W - 1))
    a, u = _rglru_coeffs(xc, wa_ref, ba_ref, wx_ref, bx_ref, lam_ref)
    h = a * h0_ref[...] + u
    hn_ref[...] = h
    ya_ref[...] = (h * jax.nn.gelu(grg_ref[...])).astype(ya_ref.dtype)
    xp = xpool_ref[...]
    parts = []
    for g, w in enumerate(POOL_WINDOWS):
        lanes = slice(g * POOL_GW, (g + 1) * POOL_GW)
        s = xp[:, lanes] + sum(pool_ref[POOL_MAX - 1 - k][:, lanes] for k in range(1, w))
        parts.append(s / float(min(start_pos + 1, w)) - xp[:, lanes])
    yc_ref[...] = _pool_project(jnp.concatenate(parts, axis=1), wp_ref, ps_ref).astype(yc_ref.dtype)


def mixer_step(za, packed, conv_state, h0, pool_state, start_pos):
    batch = za.shape[0]
    conv_t = jnp.swapaxes(conv_state.astype(F32), 0, 1)
    pool_t = jnp.swapaxes(pool_state.astype(F32), 0, 1)
    col = lambda c: pl.BlockSpec((batch, D_RNN), lambda i: (0, c))
    full = lambda a: pl.BlockSpec(a.shape, lambda i: (0,) * a.ndim)
    rows = pl.BlockSpec((batch, D_RNN), lambda i: (0, 0))
    return pl.pallas_call(
        functools.partial(_mixer_step_kernel, start_pos=start_pos),
        grid=(1,),
        in_specs=[col(ZA_XRG // D_RNN), col(ZA_GRG // D_RNN), col(ZA_XPOOL // D_RNN), full(conv_t), rows, full(pool_t)]
        + [full(a) for a in packed],
        out_specs=[rows, rows, rows],
        out_shape=[jax.ShapeDtypeStruct((batch, D_RNN), BF16), jax.ShapeDtypeStruct((batch, D_POOL), BF16),
                   jax.ShapeDtypeStruct((batch, D_RNN), F32)],
        compiler_params=pltpu.CompilerParams(vmem_limit_bytes=VMEM_LIMIT_BYTES),
        name="mixer_step",
    )(za, za, za, conv_t, h0.astype(F32), pool_t, *packed)


QB = Q_BLOCK
ROWS = HPG * QB
N_WIN_TILES = WINDOW // QB + 1
N_SEL_BIAS = REL_MAX_DIST // QB + 2
SEL_SPAN = 8
KV_CHUNK = 512


def _nsa_prompt_kernel(q_ref, ck_ref, cv_ref, kvs_ref, kvw_ref, gate_ref, bc_ref, tbs_ref, tbw_ref,
                       ovl_ref, eye_ref, pq_ref, pk_ref, pv_ref, onehot_ref, o_ref,
                       acc_ref, m_ref, qa_ref, comb_ref, ks_ref, vs_ref, kw_ref, vw_ref, cnt_ref, *, n_cmp, n_blk):
    qb = pl.program_id(2)
    t0 = qb * QB
    ncp = ck_ref.shape[2]
    seq = kvs_ref.shape[1]
    lane_row = lax.broadcasted_iota(jnp.int32, (1, LANES), 1)

    @pl.when(qb == 0)
    def _():
        ones_hi = jnp.where(lane_row >= HEAD_DIM, 1.0, 0.0)
        kw_ref[0:WINDOW, :] = jnp.broadcast_to(jnp.where(lane_row == HEAD_DIM, 1.0, 0.0), (WINDOW, LANES)).astype(BF16)
        vw_ref[0:WINDOW, :] = jnp.zeros((WINDOW, LANES), BF16)

        def stage(c, carry):
            r = pl.multiple_of(c * KV_CHUNK, KV_CHUNK)
            sel = kvs_ref[0, pl.ds(r, KV_CHUNK), :].astype(BF16)
            win = kvw_ref[0, pl.ds(r, KV_CHUNK), :].astype(BF16)
            ks_ref[pl.ds(r, KV_CHUNK), :] = (jnp.dot(sel, pk_ref[0], preferred_element_type=F32)
                                             + onehot_ref[pl.ds(r, KV_CHUNK), :].astype(F32)).astype(BF16)
            vs_ref[pl.ds(r, KV_CHUNK), :] = (jnp.dot(sel, pv_ref[0], preferred_element_type=F32) + ones_hi).astype(BF16)
            kw_ref[pl.ds(WINDOW + r, KV_CHUNK), :] = jnp.dot(win, pk_ref[0], preferred_element_type=F32).astype(BF16)
            vw_ref[pl.ds(WINDOW + r, KV_CHUNK), :] = (jnp.dot(win, pv_ref[0], preferred_element_type=F32)
                                                      + ones_hi).astype(BF16)
            return carry

        lax.fori_loop(0, seq // KV_CHUNK, stage, 0)

    q4 = jnp.dot(q_ref[0], pq_ref[...], preferred_element_type=F32)
    q3 = jnp.concatenate([q4[:, j * LANES:(j + 1) * LANES] for j in range(HPG)], axis=0)
    q = q3.astype(BF16)
    sig = jax.nn.sigmoid(gate_ref[0])
    gates = [jnp.concatenate([sig[:, 3 * j + c:3 * j + c + 1] for j in range(HPG)], axis=0) for c in range(3)]

    lc = _nt(q, ck_ref[0, 0]) + bc_ref[0].reshape(ROWS, ncp)
    tok = t0 + lax.broadcasted_iota(jnp.int32, (HPG, QB, ncp), 1).reshape(ROWS, ncp)
    col = lax.broadcasted_iota(jnp.int32, (ROWS, ncp), 1)
    valid = (tok >= col * CMP_STRIDE + (L_CMP - 1)) & (col < n_cmp)
    lc = jnp.where(valid, lc, NEG)
    mx = jnp.max(lc, axis=1, keepdims=True)
    p = jnp.where(valid, jnp.exp(lc - mx), 0.0)
    pc = p / jnp.maximum(jnp.sum(p, axis=1, keepdims=True), 1e-30)
    comb_ref[...] = gates[0] * jnp.dot(pc.astype(BF16), cv_ref[0, 0], preferred_element_type=F32)

    pcs = pc[0:QB] + pc[QB:2 * QB] + pc[2 * QB:3 * QB] + pc[3 * QB:4 * QB]
    hi = pcs.astype(BF16)
    lo = (pcs - hi.astype(F32)).astype(BF16)
    imp = _nt(ovl_ref[...], hi) + _nt(ovl_ref[...], lo)
    blk = lax.broadcasted_iota(jnp.int32, (n_blk, QB), 0)
    cur = (t0 + lax.broadcasted_iota(jnp.int32, (n_blk, QB), 1)) // SEL_BLOCK
    forced = (blk == 0) | (blk == cur) | (blk == cur - 1)
    score = jnp.where(blk > cur, -1.0, jnp.where(forced, FORCE, imp))
    chunks = [score[r:r + SUBLANES] for r in range(0, n_blk, SUBLANES)]
    sub = lax.broadcasted_iota(jnp.int32, (SUBLANES, QB), 0)
    cnt_ref[...] = jnp.zeros((n_blk, QB), F32)
    last_blk = (t0 + QB - 1) // SEL_BLOCK
    for mc in range(0, n_blk, SUBLANES):
        @pl.when(mc <= last_blk)
        def _(mc=mc):
            for r, ch in enumerate(chunks):
                first = r * SUBLANES
                part = jnp.zeros((SUBLANES, QB), F32)
                for m in range(mc, mc + SUBLANES):
                    row = jnp.broadcast_to(score[m:m + 1, :], (SUBLANES, QB))
                    if first > m:
                        beats = jnp.where(row >= ch, 1.0, 0.0)
                    elif first + SUBLANES - 1 < m:
                        beats = jnp.where(row > ch, 1.0, 0.0)
                    else:
                        beats = jnp.where(sub + first > m, jnp.where(row >= ch, 1.0, 0.0), jnp.where(row > ch, 1.0, 0.0))
                    part = part + beats
                cnt_ref[first:first + SUBLANES, :] += part
    sel_neg = jnp.where(cnt_ref[...] < float(min(N_SELECT, n_blk)), 0.0, NEG)
    pieces = [jnp.zeros((HEAD_DIM, QB), F32), sel_neg]
    if n_blk < HEAD_DIM:
        pieces.append(jnp.zeros((HEAD_DIM - n_blk, QB), F32))
    placed_t = jnp.concatenate(pieces, axis=0).astype(BF16)
    placed = _nt(eye_ref[...], placed_t)
    qa_ref[...] = (q3 + jnp.concatenate([placed] * HPG, axis=0)).astype(BF16)

    m_ref[...] = jnp.full((ROWS, LANES), M_INIT, F32)
    acc_ref[...] = jnp.zeros((ROWS, LANES), F32)
    n_bias = tbs_ref.shape[1]

    span = SEL_SPAN * QB
    n_spans = qb // SEL_SPAN + 1

    def body(kk, carry):
        off = pl.multiple_of(kk * span, span)
        s = _nt(qa_ref[...], ks_ref[pl.ds(off, span), :])
        parts = []
        for u in range(SEL_SPAN):
            idx = jnp.clip(qb - (kk * SEL_SPAN + u), -1, n_bias - 2) + 1
            parts.append(s[:, u * QB:(u + 1) * QB] + tbs_ref[:, pl.ds(idx, 1)].reshape(ROWS, LANES))
        tile_max = functools.reduce(jnp.maximum, parts)
        m_old = m_ref[...]
        m_new = jnp.maximum(m_old, jnp.max(tile_max, axis=1, keepdims=True))
        alpha = jnp.exp(m_old - m_new)
        pr = jnp.concatenate([jnp.exp(x - m_new).astype(BF16) for x in parts], axis=1)
        acc_ref[...] = alpha * acc_ref[...] + jnp.dot(pr, vs_ref[pl.ds(off, SEL_SPAN * QB), :],
                                                      preferred_element_type=F32)
        m_ref[...] = m_new
        return carry

    lax.fori_loop(0, n_spans, body, 0)
    acc = acc_ref[...]
    comb_ref[...] += gates[1] * (acc / pltpu.roll(acc, HEAD_DIM, axis=1))

    qw = (q3 + jnp.where(lane_row == HEAD_DIM, NEG, 0.0)).astype(BF16)
    w_off = pl.multiple_of(t0, QB)
    sw = _nt(qw, kw_ref[pl.ds(w_off, WINDOW + QB), :]) + tbw_ref[...].reshape(ROWS, WINDOW + QB)
    pw = jnp.exp(sw - jnp.max(sw, axis=1, keepdims=True))
    accw = jnp.dot(pw.astype(BF16), vw_ref[pl.ds(w_off, WINDOW + QB), :], preferred_element_type=F32)
    comb_ref[...] += gates[2] * (accw / pltpu.roll(accw, HEAD_DIM, axis=1))

    comb = comb_ref[...]
    lane = lax.broadcasted_iota(jnp.int32, (QB, LANES), 1)
    for half in range(HPG // 2):
        a = comb[(2 * half) * QB:(2 * half + 1) * QB]
        b = comb[(2 * half + 1) * QB:(2 * half + 2) * QB]
        o_ref[0, :, half * LANES:(half + 1) * LANES] = jnp.where(lane < HEAD_DIM, a,
                                                                 pltpu.roll(b, HEAD_DIM, axis=1)).astype(o_ref.dtype)


def _bias_by_distance(rel_bias):
    max_d = (N_SEL_BIAS + 1) * QB
    return jnp.transpose(rel_bias.astype(F32)[rel_bucket(jnp.arange(max_d))])


def _ext(tab_t, lo, hi, ok_lo=None, ok_hi=None):
    n_heads, depth = tab_t.shape
    y = np.arange(lo, hi)
    parts = [jnp.broadcast_to(tab_t[:, :1], (n_heads, int(np.sum(y < 0)))), tab_t[:, max(lo, 0):max(min(hi, depth), 0)],
             jnp.broadcast_to(tab_t[:, -1:], (n_heads, int(np.sum(y > depth - 1))))]
    arr = jnp.concatenate(parts, axis=1)
    ok = np.ones(y.shape, bool)
    if ok_lo is not None:
        ok &= y >= ok_lo
    if ok_hi is not None:
        ok &= y <= ok_hi
    return jnp.where(jnp.asarray(ok)[None], arr, NEG)


def _toeplitz_tiles(ext, lo, ks):
    period = 2 * QB - 1
    seg = jnp.stack([ext[:, QB * k - (QB - 1) - lo:QB * k + QB - lo] for k in ks], axis=1)
    w = jnp.concatenate([seg[..., QB - 1::-1], seg[..., :QB - 1:-1]], axis=-1)
    return jnp.tile(w, (1, 1, QB))[..., :QB * (period - 1)].reshape(w.shape[:2] + (QB, period - 1))[..., :QB]


def rel_bias_tables(rel_bias, seq):
    tab_t = _bias_by_distance(rel_bias)
    max_d = tab_t.shape[1]
    lo = -(2 * QB - 1)
    tbs = _toeplitz_tiles(_ext(tab_t, lo, max_d, 0, None), lo, range(-1, N_SEL_BIAS))
    tbw = _toeplitz_tiles(_ext(tab_t, lo, max_d, 0, WINDOW), lo, range(N_WIN_TILES - 1, -1, -1))
    tbw = jnp.transpose(tbw, (0, 2, 1, 3)).reshape(N_HEADS, QB, N_WIN_TILES * QB)
    nqb = seq // QB
    ncp = seq // CMP_STRIDE
    per_qb = QB // CMP_STRIDE
    width = ncp + per_qb * (nqb - 1)
    c0 = QB * (nqb - 1) - (L_CMP - 1)
    ext = _ext(tab_t, c0 - CMP_STRIDE * (width - 1), c0 + QB)
    n_z = width + per_qb - 1
    f = ext[:, :CMP_STRIDE * n_z].reshape(N_HEADS, n_z, CMP_STRIDE)[:, ::-1]
    v = jnp.concatenate([jnp.swapaxes(f[:, per_qb - 1 - a:per_qb - 1 - a + width], 1, 2) for a in range(per_qb)],
                        axis=1)
    bc = jnp.stack([v[:, :, per_qb * (nqb - 1 - b):per_qb * (nqb - 1 - b) + ncp] for b in range(nqb)])
    return tbs, tbw, bc


def nsa_prompt_pallas(q, ck, cv, kv_sel, kv_win, gn, tables):
    B, S = q.shape[:2]
    assert S % (SEL_SPAN * QB) == 0 and S % KV_CHUNK == 0 and S // SEL_BLOCK <= HEAD_DIM
    tbs, tbw, bc = tables
    nqb = S // QB
    ncp = S // CMP_STRIDE
    n_cmp = ncp - 1
    n_blk = S // SEL_BLOCK
    c0 = np.arange(ncp)[None, :] * CMP_STRIDE
    s0 = np.arange(n_blk)[:, None] * SEL_BLOCK
    ovl_t = ((c0 <= s0 + SEL_BLOCK - 1) & (c0 + L_CMP - 1 >= s0) & (np.arange(ncp)[None, :] < n_cmp))
    ovl_t = jnp.asarray(ovl_t.astype(np.float32), BF16)
    eye = jnp.asarray(np.eye(QB, dtype=np.float32), BF16)
    gw = HPG * HEAD_DIM
    pq = np.zeros((gw, HPG * LANES), np.float32)
    pq[np.arange(gw), (np.arange(gw) // HEAD_DIM) * LANES + np.arange(gw) % HEAD_DIM] = 1.0
    pk = np.zeros((KV_HEADS, KV_ROW, LANES), np.float32)
    pv = np.zeros((KV_HEADS, KV_ROW, LANES), np.float32)
    for g in range(KV_HEADS):
        pk[g, g * HEAD_DIM + np.arange(HEAD_DIM), np.arange(HEAD_DIM)] = 1.0
        pv[g, (KV_HEADS + g) * HEAD_DIM + np.arange(HEAD_DIM), np.arange(HEAD_DIM)] = 1.0
    onehot = np.zeros((S, LANES), np.float32)
    onehot[np.arange(S), HEAD_DIM + np.arange(S) // SEL_BLOCK] = 1.0

    kv_spec = pl.BlockSpec((1, S, KV_ROW), lambda b, g, i: (b, 0, 0))
    cmp_spec = pl.BlockSpec((1, 1, ncp, LANES), lambda b, g, i: (b, g, 0, 0))
    const2 = lambda b, g, i: (0, 0)
    return pl.pallas_call(
        functools.partial(_nsa_prompt_kernel, n_cmp=n_cmp, n_blk=n_blk),
        grid=(B, KV_HEADS, nqb),
        in_specs=[
            pl.BlockSpec((1, QB, gw), lambda b, g, i: (b, i, g)),
            cmp_spec, cmp_spec, kv_spec, kv_spec,
            pl.BlockSpec((1, QB, LANES), lambda b, g, i: (b, i, g)),
            pl.BlockSpec((1, HPG, QB, ncp), lambda b, g, i: (i, g, 0, 0)),
            pl.BlockSpec((HPG, N_SEL_BIAS + 1, QB, QB), lambda b, g, i: (g, 0, 0, 0)),
            pl.BlockSpec((HPG, QB, N_WIN_TILES * QB), lambda b, g, i: (g, 0, 0)),
            pl.BlockSpec((n_blk, ncp), const2),
            pl.BlockSpec((QB, QB), const2),
            pl.BlockSpec((gw, HPG * LANES), const2),
            pl.BlockSpec((1, KV_ROW, LANES), lambda b, g, i: (g, 0, 0)),
            pl.BlockSpec((1, KV_ROW, LANES), lambda b, g, i: (g, 0, 0)),
            pl.BlockSpec((S, LANES), const2),
        ],
        out_specs=pl.BlockSpec((1, QB, gw), lambda b, g, i: (b, i, g)),
        out_shape=jax.ShapeDtypeStruct((B, S, N_HEADS * HEAD_DIM), BF16),
        scratch_shapes=[pltpu.VMEM((ROWS, LANES), F32), pltpu.VMEM((ROWS, LANES), F32),
                        pltpu.VMEM((ROWS, LANES), BF16), pltpu.VMEM((ROWS, LANES), F32),
                        pltpu.VMEM((S, LANES), BF16), pltpu.VMEM((S, LANES), BF16),
                        pltpu.VMEM((S + WINDOW, LANES), BF16), pltpu.VMEM((S + WINDOW, LANES), BF16),
                        pltpu.VMEM((n_blk, QB), F32)],
        compiler_params=pltpu.CompilerParams(dimension_semantics=("parallel", "parallel", "arbitrary"),
                                             vmem_limit_bytes=VMEM_LIMIT_BYTES),
        name="nsa_prompt",
    )(q, ck, cv, kv_sel, kv_win, gn, bc, tbs, tbw, ovl_t, eye, jnp.asarray(pq, BF16), jnp.asarray(pk, BF16),
      jnp.asarray(pv, BF16), jnp.asarray(onehot, BF16))


STEP_ROWS = SUBLANES
RANK_LANES = 2 * LANES


def _split3(x):
    hi = x.astype(BF16)
    r1 = x - hi.astype(F32)
    mid = r1.astype(BF16)
    return hi, mid, (r1 - mid.astype(F32)).astype(BF16)


def _nsa_step_kernel(*refs, n_pages, n_cmp):
    pages = refs[1:1 + n_pages]
    (q_ref, ck_ref, cv_ref, kvs_ref, kvw_ref, win_ref, gn_ref, bcs_ref, bsel_ref, bwin_ref, ovl_ref, oh_ref,
     pq_ref, pk_ref, pv_ref, o_ref, kc_ref, vs_ref, kw_ref, vw_ref) = refs[1 + n_pages:]
    page = pages[0].shape[-1]
    past = n_pages * page
    wb = win_ref.shape[-1]
    ncp = ck_ref.shape[2]
    first_row = lax.broadcasted_iota(jnp.int32, (LANES, 1), 0) == 0

    @pl.when(pl.program_id(0) == 0)
    def _():
        for g in range(KV_HEADS):
            kc_ref[g, HEAD_DIM:LANES, :] = jnp.zeros((LANES - HEAD_DIM, past + LANES), BF16)
            kc_ref[g, LANES:2 * LANES, 0:past] = oh_ref[...]
            kc_ref[g, LANES:2 * LANES, past:past + LANES] = jnp.zeros((LANES, LANES), BF16)
            vs_ref[g, HEAD_DIM:LANES, :] = jnp.ones((LANES - HEAD_DIM, past + LANES), BF16)
            kw_ref[g, HEAD_DIM:LANES, :] = jnp.zeros((LANES - HEAD_DIM, wb + LANES), BF16)
            vw_ref[g, HEAD_DIM:LANES, :] = jnp.ones((LANES - HEAD_DIM, wb + LANES), BF16)

    def place_new(row_ref, c0, k_dst, v_dst):
        new = jnp.where(first_row, row_ref[0], 0.0).astype(BF16)
        for g in range(KV_HEADS):
            k_dst[g, 0:HEAD_DIM, c0:c0 + LANES] = _nt(pk_ref[g], new)[0:HEAD_DIM].astype(BF16)
            v_dst[g, 0:HEAD_DIM, c0:c0 + LANES] = _nt(pv_ref[g], new)[0:HEAD_DIM].astype(BF16)

    for k, pg in enumerate(pages):
        for g in range(KV_HEADS):
            kc_ref[g, 0:HEAD_DIM, k * page:(k + 1) * page] = pg[0, 0, 0, g].astype(BF16)
            vs_ref[g, 0:HEAD_DIM, k * page:(k + 1) * page] = pg[0, 0, 1, g].astype(BF16)
    place_new(kvs_ref, past, kc_ref, vs_ref)
    for g in range(KV_HEADS):
        kw_ref[g, 0:HEAD_DIM, 0:wb] = win_ref[0, 0, 0, g].astype(BF16)
        vw_ref[g, 0:HEAD_DIM, 0:wb] = win_ref[0, 0, 1, g].astype(BF16)
    place_new(kvw_ref, wb, kw_ref, vw_ref)

    sig = jax.nn.sigmoid(gn_ref[0])
    lane = lax.broadcasted_iota(jnp.int32, (1, LANES), 1)
    rr = lax.broadcasted_iota(jnp.int32, (RANK_LANES, RANK_LANES), 0)
    cc = lax.broadcasted_iota(jnp.int32, (RANK_LANES, RANK_LANES), 1)
    ones_sq = jnp.ones((RANK_LANES, RANK_LANES), BF16)
    pad_rows = jnp.zeros((STEP_ROWS - HPG, LANES), F32)
    for g in range(KV_HEADS):
        gw = HPG * HEAD_DIM
        q4 = jnp.dot(q_ref[0][:, g * gw:(g + 1) * gw], pq_ref[...], preferred_element_type=F32)
        q8 = jnp.concatenate([q4[:, j * LANES:(j + 1) * LANES] for j in range(HPG)] + [pad_rows], axis=0)
        gates = [jnp.concatenate([sig[:, g * LANES + 3 * j + c:g * LANES + 3 * j + c + 1] for j in range(HPG)]
                                 + [pad_rows[:, 0:1]], axis=0) for c in range(3)]

        lc = _nt(q8.astype(BF16), ck_ref[0, g]) + bcs_ref[g]
        col = lax.broadcasted_iota(jnp.int32, (STEP_ROWS, ncp), 1)
        valid = col < n_cmp
        lc = jnp.where(valid, lc, NEG)
        p = jnp.where(valid, jnp.exp(lc - jnp.max(lc, axis=1, keepdims=True)), 0.0)
        pc = p / jnp.maximum(jnp.sum(p, axis=1, keepdims=True), 1e-30)
        comb = gates[0] * jnp.dot(pc.astype(BF16), cv_ref[0, g], preferred_element_type=F32)

        pcs = jnp.broadcast_to(jnp.sum(pc[0:HPG], axis=0, keepdims=True), (STEP_ROWS, ncp))
        hi = pcs.astype(BF16)
        lo = (pcs - hi.astype(F32)).astype(BF16)
        imp = (jnp.dot(hi, ovl_ref[...], preferred_element_type=F32)
               + jnp.dot(lo, ovl_ref[...], preferred_element_type=F32))[0:1]
        blk_id = lax.broadcasted_iota(jnp.int32, (1, RANK_LANES), 1)
        cur = past // SEL_BLOCK
        forced = (blk_id == 0) | (blk_id == cur) | (blk_id == cur - 1)
        score = jnp.where(blk_id > cur, -1.0, jnp.where(forced, FORCE, imp))
        s_n = jnp.broadcast_to(score, (RANK_LANES, RANK_LANES))
        diag = jnp.where(rr == cc, s_n, 0.0)
        s_m = sum(_nt(part, ones_sq) for part in _split3(diag))
        beats = jnp.where(rr < cc, jnp.where(s_m >= s_n, 1.0, 0.0), jnp.where(s_m > s_n, 1.0, 0.0))
        rank = jnp.sum(beats, axis=0, keepdims=True)
        sel_neg = jnp.where(rank < float(N_SELECT), 0.0, NEG)

        qa = jnp.concatenate([q8, jnp.broadcast_to(sel_neg[:, :LANES], (STEP_ROWS, LANES))], axis=1).astype(BF16)
        s_past = jnp.dot(qa, kc_ref[g, :, 0:past], preferred_element_type=F32) + bsel_ref[g, :, 0:past]
        s_new = (jnp.dot(qa, kc_ref[g, :, past:past + LANES], preferred_element_type=F32)
                 + bsel_ref[g, :, past:past + LANES] + sel_neg[:, LANES:LANES + 1])
        mx = jnp.maximum(jnp.max(s_past, axis=1, keepdims=True), jnp.max(s_new, axis=1, keepdims=True))
        acc = (_nt(jnp.exp(s_past - mx).astype(BF16), vs_ref[g, :, 0:past])
               + _nt(jnp.exp(s_new - mx).astype(BF16), vs_ref[g, :, past:past + LANES]))
        comb = comb + gates[1] * (acc / pltpu.roll(acc, HEAD_DIM, axis=1))

        sw = jnp.dot(q8.astype(BF16), kw_ref[g], preferred_element_type=F32) + bwin_ref[g]
        pw = jnp.exp(sw - jnp.max(sw, axis=1, keepdims=True))
        accw = _nt(pw.astype(BF16), vw_ref[g])
        comb = comb + gates[2] * (accw / pltpu.roll(accw, HEAD_DIM, axis=1))

        for half in range(HPG // 2):
            a = comb[2 * half:2 * half + 1]
            b = comb[2 * half + 1:2 * half + 2]
            o_ref[0, :, g * gw + half * LANES:g * gw + (half + 1) * LANES] = jnp.where(
                lane < HEAD_DIM, a, pltpu.roll(b, HEAD_DIM, axis=1)).astype(o_ref.dtype)


def step_bias_tables(rel_bias, past, wb, ncp):
    tab_t = _bias_by_distance(rel_bias)

    def rows(t):
        return jnp.pad(t.reshape(KV_HEADS, HPG, -1), ((0, 0), (0, STEP_ROWS - HPG), (0, 0)))

    d_first = past - (L_CMP - 1)
    cmp_t = _ext(tab_t, d_first - CMP_STRIDE * (ncp - 1), d_first + 1, 0, None)[:, ::CMP_STRIDE][:, ::-1]
    sel_t = _ext(tab_t, -(LANES - 1), past + 1, 0, None)[:, ::-1]
    win_t = _ext(tab_t, -(LANES - 1), wb + 1, 0, WINDOW)[:, ::-1]
    return rows(cmp_t), rows(sel_t), rows(win_t)


def nsa_step(q, ck, cv, kvs_new, kvw_new, sel_pool, win_buf, gn, page_table, tables, layer):
    B, n_pages = page_table.shape
    page = sel_pool.shape[-1]
    past = n_pages * page
    wb = win_buf.shape[-1]
    ncp = ck.shape[2]
    n_blk = past // SEL_BLOCK + 1
    assert n_blk <= LANES + 1 and page == LANES and wb % LANES == 0
    bcs, bsel, bwin = tables
    c0 = np.arange(ncp)[:, None] * CMP_STRIDE
    s0 = np.arange(RANK_LANES)[None, :] * SEL_BLOCK
    ovl = ((c0 <= s0 + SEL_BLOCK - 1) & (c0 + L_CMP - 1 >= s0) & (np.arange(ncp)[:, None] < ncp - 1)
           & (np.arange(RANK_LANES)[None, :] < n_blk))
    onehot = np.zeros((LANES, past), np.float32)
    onehot[np.arange(past) // SEL_BLOCK, np.arange(past)] = 1.0
    gw = HPG * HEAD_DIM
    pq = np.zeros((gw, HPG * LANES), np.float32)
    pq[np.arange(gw), (np.arange(gw) // HEAD_DIM) * LANES + np.arange(gw) % HEAD_DIM] = 1.0
    pk = np.zeros((KV_HEADS, LANES, KV_ROW), np.float32)
    pv = np.zeros((KV_HEADS, LANES, KV_ROW), np.float32)
    for g in range(KV_HEADS):
        pk[g, np.arange(HEAD_DIM), g * HEAD_DIM + np.arange(HEAD_DIM)] = 1.0
        pv[g, np.arange(HEAD_DIM), (KV_HEADS + g) * HEAD_DIM + np.arange(HEAD_DIM)] = 1.0
    consts = [jnp.asarray(ovl.astype(np.float32), BF16), jnp.asarray(onehot, BF16), jnp.asarray(pq, BF16),
              jnp.asarray(pk, BF16), jnp.asarray(pv, BF16)]

    row3 = lambda a: a.reshape(B, 1, a.shape[-1])
    per_b = lambda shape: pl.BlockSpec((1,) + shape, lambda b, pt: (b,) + (0,) * len(shape))
    full = lambda a: pl.BlockSpec(a.shape, lambda b, pt: (0,) * a.ndim)
    slab = (2, KV_HEADS, HEAD_DIM)
    page_specs = [pl.BlockSpec((1, 1) + slab + (page,), lambda b, pt, k=k: (layer, pt[b, k], 0, 0, 0, 0))
                  for k in range(n_pages)]
    win_spec = pl.BlockSpec((1, 1) + slab + (wb,), lambda b, pt: (layer, b, 0, 0, 0, 0))
    in_specs = page_specs + [per_b((1, Q_COLS)), per_b((KV_HEADS, ncp, LANES)), per_b((KV_HEADS, ncp, LANES)),
                             per_b((1, KV_ROW)), per_b((1, KV_ROW)), win_spec, per_b((1, GN_COLS)),
                             full(bcs), full(bsel), full(bwin)] + [full(c) for c in consts]
    grid_spec = pltpu.PrefetchScalarGridSpec(
        num_scalar_prefetch=1, grid=(B,), in_specs=in_specs, out_specs=per_b((1, Q_COLS)),
        scratch_shapes=[pltpu.VMEM((KV_HEADS, 2 * LANES, past + LANES), BF16), pltpu.VMEM((KV_HEADS, LANES, past + LANES), BF16),
                        pltpu.VMEM((KV_HEADS, LANES, wb + LANES), BF16), pltpu.VMEM((KV_HEADS, LANES, wb + LANES), BF16)])
    out = pl.pallas_call(
        functools.partial(_nsa_step_kernel, n_pages=n_pages, n_cmp=ncp - 1),
        grid_spec=grid_spec,
        out_shape=jax.ShapeDtypeStruct((B, 1, Q_COLS), BF16),
        compiler_params=pltpu.CompilerParams(dimension_semantics=("arbitrary",), vmem_limit_bytes=VMEM_LIMIT_BYTES),
        name="nsa_step",
    )(page_table, *([sel_pool] * n_pages), row3(q), ck, cv, row3(kvs_new), row3(kvw_new), win_buf, row3(gn),
      bcs, bsel, bwin, *consts)
    return out.reshape(B, Q_COLS)


def _merge_kernel(ya_ref, yb_ref, yc_ref, ga_ref, gb_ref, gc_ref, x_ref, wa_ref, wb_ref, wc_ref, wo_ref, o_ref):
    def branch(y_ref, g_ref, w_ref):
        return jax.nn.sigmoid(g_ref[...]) * jnp.dot(y_ref[...], w_ref[...], preferred_element_type=F32)

    merged = branch(ya_ref, ga_ref, wa_ref) + branch(yb_ref, gb_ref, wb_ref) + branch(yc_ref, gc_ref, wc_ref)
    o_ref[...] = x_ref[...] + jnp.dot(merged.astype(BF16), wo_ref[...], preferred_element_type=F32)


def merge(ya, yb, yc, za, x, wa, wb, wc, wo):
    T = x.shape[0]
    tm = _pick(T, (512, 256, 128))
    y_spec = pl.BlockSpec((tm, ya.shape[1]), lambda i: (i, 0))
    w_spec = pl.BlockSpec((ya.shape[1], D_MODEL), lambda i: (0, 0))
    return pl.pallas_call(
        _merge_kernel,
        grid=(T // tm,),
        in_specs=[y_spec, y_spec, y_spec,
                  pl.BlockSpec((tm, D_MODEL), lambda i: (i, 0)), pl.BlockSpec((tm, D_MODEL), lambda i: (i, 1)),
                  pl.BlockSpec((tm, D_MODEL), lambda i: (i, 2)),
                  pl.BlockSpec((tm, D_MODEL), lambda i: (i, 0)),
                  w_spec, w_spec, w_spec, pl.BlockSpec((D_MODEL, D_MODEL), lambda i: (0, 0))],
        out_specs=pl.BlockSpec((tm, D_MODEL), lambda i: (i, 0)),
        out_shape=jax.ShapeDtypeStruct((T, D_MODEL), F32),
        compiler_params=pltpu.CompilerParams(dimension_semantics=("parallel",), vmem_limit_bytes=VMEM_LIMIT_BYTES),
        name="merge",
    )(ya, yb, yc, za, za, za, x, wa, wb, wc, wo)


def _ffn_kernel(*refs, routed):
    if routed:
        h_ref, g_ref, wr_ref, br_ref, wg_ref, wu_ref, wd_ref, o_ref, xn_ref, gate_ref = refs
    else:
        h_ref, g_ref, wg_ref, wu_ref, wd_ref, o_ref, xn_ref = refs
    e = pl.program_id(1)
    f = pl.program_id(2)

    @pl.when((e == 0) & (f == 0))
    def _():
        h = h_ref[...]
        xb = _rms_rows(h, g_ref[...]).astype(BF16)
        xn_ref[...] = xb
        o_ref[...] = h
        if routed:
            logits = jnp.dot(xb, wr_ref[...], preferred_element_type=F32) + br_ref[...]
            lane = lax.broadcasted_iota(jnp.int32, logits.shape, 1)
            m1 = jnp.max(logits, axis=1, keepdims=True)
            i1 = jnp.min(jnp.where(logits == m1, lane, LANES), axis=1, keepdims=True)
            rest = jnp.where(lane == i1, M_INIT, logits)
            m2 = jnp.max(rest, axis=1, keepdims=True)
            i2 = jnp.min(jnp.where(rest == m2, lane, LANES), axis=1, keepdims=True)
            r = jnp.exp(m2 - m1)
            gate_ref[...] = jnp.where(lane == i1, 1.0 / (1.0 + r), 0.0) + jnp.where(lane == i2, r / (1.0 + r), 0.0)

    xb = xn_ref[...]
    a = jnp.dot(xb, wg_ref[0], preferred_element_type=F32)
    u = jnp.dot(xb, wu_ref[0], preferred_element_type=F32)
    t = (a * jax.nn.sigmoid(a)) * u
    y = jnp.dot(t.astype(BF16), wd_ref[0], preferred_element_type=F32)
    if routed:
        lane = lax.broadcasted_iota(jnp.int32, gate_ref.shape, 1)
        y = jnp.sum(jnp.where(lane == e, gate_ref[...], 0.0), axis=1, keepdims=True) * y
    o_ref[...] += y


def channel_mixer(h, norm_gain, wg, wu, wd, router=None):
    T = h.shape[0]
    E, _, F = wg.shape
    tm = _pick(T, (512, 256, 128))
    tf = _pick(F, (1408, 1024, 512, 256, 128))
    routed = router is not None
    row = lambda i, e, f: (i, 0)
    in_specs = [pl.BlockSpec((tm, D_MODEL), row), pl.BlockSpec((1, D_MODEL), lambda i, e, f: (0, 0))]
    args = [h, norm_gain.astype(F32)[None]]
    scratch = [pltpu.VMEM((tm, D_MODEL), BF16)]
    if routed:
        in_specs += [pl.BlockSpec((D_MODEL, LANES), lambda i, e, f: (0, 0)), pl.BlockSpec((1, LANES), lambda i, e, f: (0, 0))]
        args += list(router)
        scratch.append(pltpu.VMEM((tm, LANES), F32))
    in_specs += [pl.BlockSpec((1, D_MODEL, tf), lambda i, e, f: (e, 0, f)),
                 pl.BlockSpec((1, D_MODEL, tf), lambda i, e, f: (e, 0, f)),
                 pl.BlockSpec((1, tf, D_MODEL), lambda i, e, f: (e, f, 0))]
    args += [wg, wu, wd]
    return pl.pallas_call(
        functools.partial(_ffn_kernel, routed=routed),
        grid=(T // tm, E, F // tf),
        in_specs=in_specs,
        out_specs=pl.BlockSpec((tm, D_MODEL), row),
        out_shape=jax.ShapeDtypeStruct((T, D_MODEL), F32),
        scratch_shapes=scratch,
        compiler_params=pltpu.CompilerParams(dimension_semantics=("parallel", "arbitrary", "arbitrary"),
                                             vmem_limit_bytes=VMEM_LIMIT_BYTES),
        name="moe" if routed else "ffn",
    )(*args)


def mixer_tail(x2, za, y_a, y_b, y_c, l, P):
    return merge(y_a, y_b, y_c, za, x2, P['w_br_rg'][l].astype(BF16), P['w_br_attn'][l].astype(BF16),
                 P['w_br_pool'][l].astype(BF16), P['w_out'][l].astype(BF16))


def prompt_mixer(x, l, P, packed, tables):
    B, T, _ = x.shape
    x2 = x.reshape(B * T, D_MODEL)
    za, q, kv_c, kv_s, kv_w, gn = projection(x2, P['attn_norm'][l], packed['proj'])
    zeros = lambda rows: jnp.zeros((B, rows, D_RNN), F32)
    y_a, y_c, conv_new, h_last, pool_new = mixer_seq(za, B, T, packed['mix'], zeros(CONV_W - 1), jnp.zeros((B, D_RNN), F32),
                                                     zeros(POOL_MAX - 1), 0)
    ck, cv = compress(kv_c.reshape(B, 2 * T, LANES), packed['cmp'])
    y_b = nsa_prompt_pallas(q.reshape(B, T, Q_COLS), ck, cv, kv_s.reshape(B, T, KV_ROW), kv_w.reshape(B, T, KV_ROW),
                            gn.reshape(B, T, GN_COLS), tables)
    out = mixer_tail(x2, za, y_a, y_b.reshape(B * T, Q_COLS), y_c, l, P)
    kv_shape = (B, T, 2, KV_HEADS, HEAD_DIM)
    state = (kv_c.reshape(kv_shape), kv_s.reshape(kv_shape), kv_w.reshape(kv_shape)[:, -min(WINDOW, T):], conv_new,
             h_last[:, 0], pool_new)
    return out.reshape(B, T, D_MODEL), state


def sample_mixer(x, l, P, packed, past_len, conv_state, h0, pool_state, caches, win_buf, page_table, step_tables):
    B, T, _ = x.shape
    x2 = x.reshape(B * T, D_MODEL)
    za, q, kv_c, kv_s, kv_w, gn = projection(x2, P['attn_norm'][l], packed['proj'])
    y_a, y_c, h_new = mixer_step(za, packed['mix'], conv_state, h0, pool_state, past_len)
    cmp_pools, sel_pools, win_bufs = caches
    ck, cv = compress(cmp_pools, packed['cmp'], page_table, layer=l)
    kv_shape = (B, T, 2, KV_HEADS, HEAD_DIM)
    wb = win_buf.shape[1]
    y_b = nsa_step(q, ck, cv, kv_s, kv_w, sel_pools, win_bufs, gn, page_table, step_tables, l)
    win_new = jnp.concatenate([win_buf[:, 1:], kv_w.reshape(kv_shape)], axis=1)
    out = mixer_tail(x2, za, y_a, y_b, y_c, l, P)
    conv_new = jnp.concatenate([conv_state[:, 1:], za[:, None, ZA_XRG:ZA_XRG + D_RNN]], axis=1)
    pool_new = jnp.concatenate([pool_state[:, 1:], za[:, None, ZA_XPOOL:ZA_XPOOL + D_POOL]], axis=1)
    state = (kv_c.reshape(kv_shape), kv_s.reshape(kv_shape), win_new, conv_new, h_new, pool_new)
    return out.reshape(B, T, D_MODEL), state


def ffn_layer(x, l, P, W):
    B, T, _ = x.shape
    i = l // 2
    if l % 2 == 0:
        y = channel_mixer(x.reshape(B * T, D_MODEL), P['ffn_norm'][l], W['ffn_g'][i], W['ffn_u'][i], W['ffn_d'][i])
    else:
        y = channel_mixer(x.reshape(B * T, D_MODEL), P['ffn_norm'][l], W['moe_g'][i], W['moe_u'][i], W['moe_d'][i],
                          router=W['router'][i])
    return y.reshape(B, T, D_MODEL)


def kernel(x_prompt, x_sample, cache_cmp_kv, cache_sel_kv, cache_win_kv, state_conv, state_rg_h, state_pool,
           page_table, attn_norm, w_in, conv_w, conv_b, rg_w_a, rg_b_a, rg_w_x, rg_b_x, rg_lambda, q_norm, k_norm,
           cmp_pe, w_cmp1, w_cmp2, rel_bias, w_pool, pool_scale, w_br_rg, w_br_attn, w_br_pool, w_out, ffn_norm,
           ffn_w_gate, ffn_w_up, ffn_w_down, w_router, b_router, moe_w_gate, moe_w_up, moe_w_down):
    P = dict(attn_norm=attn_norm, conv_w=conv_w, conv_b=conv_b, rg_w_a=rg_w_a, rg_b_a=rg_b_a,
             rg_w_x=rg_w_x, rg_b_x=rg_b_x, rg_lambda=rg_lambda, w_pool=w_pool,
             pool_scale=pool_scale, w_br_rg=w_br_rg, w_br_attn=w_br_attn, w_br_pool=w_br_pool, w_out=w_out,
             ffn_norm=ffn_norm)
    depth = w_in.shape[0]
    n_moe = w_router.shape[0]
    pad_e = LANES - N_EXPERTS
    W = dict(
        ffn_g=[w[None].astype(BF16) for w in ffn_w_gate], ffn_u=[w[None].astype(BF16) for w in ffn_w_up],
        ffn_d=[w[None].astype(BF16) for w in ffn_w_down],
        moe_g=[w.astype(BF16) for w in moe_w_gate], moe_u=[w.astype(BF16) for w in moe_w_up],
        moe_d=[w.astype(BF16) for w in moe_w_down],
        router=[(jnp.pad(w_router[i], ((0, 0), (0, pad_e))).astype(BF16),
                 jnp.pad(b_router[i].astype(F32), (0, pad_e), constant_values=NEG)[None]) for i in range(n_moe)])
    past_len = page_table.shape[1] * PAGE_SIZE
    y_p, y_s = x_prompt, x_sample
    tables = rel_bias_tables(rel_bias, x_prompt.shape[1])
    step_tables = step_bias_tables(rel_bias, past_len, cache_win_kv.shape[2], past_len // CMP_STRIDE)
    positions_minor = (0, 1, 3, 4, 5, 2)
    caches = tuple(jnp.transpose(c, positions_minor) for c in (cache_cmp_kv, cache_sel_kv, cache_win_kv))
    p_list, s_list = [], []
    for l in range(depth):
        packed = dict(proj=pack_projection(w_in[l], q_norm[l], k_norm[l, 1], k_norm[l, 2]),
                      cmp=pack_compress(w_cmp1[l], w_cmp2[l], cmp_pe[l], k_norm[l, 0]),
                      mix=pack_mixer(conv_w[l], conv_b[l], rg_w_a[l], rg_b_a[l], rg_w_x[l], rg_b_x[l], rg_lambda[l],
                                     w_pool[l], pool_scale[l]))
        y_p, st_p = prompt_mixer(y_p, l, P, packed, tables)
        y_p = ffn_layer(y_p, l, P, W)
        p_list.append(st_p)
        y_s, st_s = sample_mixer(y_s, l, P, packed, past_len, state_conv[l], state_rg_h[l], state_pool[l],
                                 caches, cache_win_kv[l], page_table, step_tables)
        y_s = ffn_layer(y_s, l, P, W)
        s_list.append(st_s)
    p_cmp_kv, p_sel_kv, p_win_kv, p_conv, p_h, p_pool = [jnp.stack(a) for a in zip(*p_list)]
    s_cmp_kv, s_sel_kv, s_win_kv, s_conv, s_h, s_pool = [jnp.stack(a) for a in zip(*s_list)]
    return (y_p, y_s, p_cmp_kv, p_sel_kv, p_win_kv, p_conv, p_h, p_pool,
            s_cmp_kv, s_sel_kv, s_win_kv, s_conv, s_h, s_pool)
```

```python
import math
import functools

import jax
import jax.numpy as jnp
import numpy as np
from jax import lax
from jax.experimental import pallas as pl
from jax.experimental.pallas import tpu as pltpu

D_MODEL = 1024
PAGE_SIZE = 128
F32 = jnp.float32
BF16 = jnp.bfloat16
EPS = 1e-6
NEG = -1e30
FORCE = 1e4
D_RNN = 512
CONV_W = 4
RG_C = 8.0
N_HEADS = 8
KV_HEADS = 2
HPG = N_HEADS // KV_HEADS
HEAD_DIM = 64
L_CMP = 32
CMP_STRIDE = 16
CMP_HIDDEN = 256
SEL_BLOCK = 64
N_SELECT = 16
WINDOW = 512
Q_BLOCK = 128
D_POOL = 512
POOL_WINDOWS = (2, 4, 8, 16)
POOL_GW = D_POOL // len(POOL_WINDOWS)
POOL_MAX = 16
REL_BUCKETS = 32
REL_MAX_DIST = 1024
N_EXPERTS = 8
KV_ROW = 2 * KV_HEADS * HEAD_DIM
SPLITS = (D_RNN, D_RNN, N_HEADS * HEAD_DIM, KV_ROW, KV_ROW, KV_ROW, 3 * N_HEADS, D_POOL, 3 * D_MODEL)

VMEM_LIMIT_BYTES = 52 * 1024 * 1024
LANES = 128
SUBLANES = 8
M_INIT = -3e38


def _pick(n, cands):
    for c in cands:
        if n % c == 0:
            return c
    return n


def _nt(a, b):
    return lax.dot_general(a, b, (((1,), (1,)), ((), ())), preferred_element_type=F32)


def _rms_rows(x, g):
    return x * lax.rsqrt(jnp.mean(x * x, axis=-1, keepdims=True) + EPS) * g


def rel_bucket(dist):
    n_exact = REL_BUCKETS // 2
    d = jnp.maximum(dist, 0)
    df = jnp.maximum(d, 1).astype(F32)
    large = n_exact + (jnp.log(df / n_exact) / math.log(REL_MAX_DIST / n_exact)
                       * (REL_BUCKETS - n_exact)).astype(jnp.int32)
    return jnp.where(d < n_exact, d, jnp.minimum(large, REL_BUCKETS - 1))


PROJ_TN = 1024
PROJ_HALF = PROJ_TN // 2
ZA_COLS = 3 * D_MODEL + 2 * D_RNN + D_POOL
Q_COLS = N_HEADS * HEAD_DIM
GN_COLS = 2 * LANES
PROJ_COLS = ZA_COLS + Q_COLS + 3 * KV_ROW + GN_COLS
N_ZA = (ZA_COLS + Q_COLS) // PROJ_TN
ZA_STORE = N_ZA * PROJ_TN
assert ZA_COLS + Q_COLS == ZA_STORE and PROJ_COLS == ZA_STORE + PROJ_TN and Q_COLS == PROJ_HALF == 2 * KV_ROW
ZA_XRG, ZA_GRG, ZA_XPOOL = 3 * D_MODEL, 3 * D_MODEL + D_RNN, 3 * D_MODEL + 2 * D_RNN


def _proj_kernel(x_ref, g_ref, w_ref, seg_ref, ng_ref, nm_ref, za_ref, q_ref, kvc_ref, kvs_ref, kvw_ref, gn_ref, xn_ref):
    j = pl.program_id(1)

    @pl.when(j == 0)
    def _():
        xn_ref[...] = _rms_rows(x_ref[...], g_ref[...]).astype(BF16)

    acc = jnp.dot(xn_ref[...], w_ref[...], preferred_element_type=F32)

    @pl.when(j < N_ZA)
    def _():
        za_ref[...] = acc

    def head_norm(half):
        cols = slice(half * PROJ_HALF, (half + 1) * PROJ_HALF)
        y = acc[:, cols]
        sq = y * y
        hi = sq.astype(BF16)
        lo = (sq - hi.astype(F32)).astype(BF16)
        ss = (jnp.dot(hi, seg_ref[...], preferred_element_type=F32)
              + jnp.dot(lo, seg_ref[...], preferred_element_type=F32))
        normed = y * lax.rsqrt(ss * (1.0 / HEAD_DIM) + EPS) * ng_ref[:, cols]
        return jnp.where(nm_ref[:, cols] > 0.5, normed, y)

    @pl.when(j == N_ZA - 1)
    def _():
        q_ref[...] = head_norm(1).astype(BF16)

    @pl.when(j == N_ZA)
    def _():
        a = head_norm(0)
        b = head_norm(1)
        kvc_ref[...] = a[:, :KV_ROW]
        kvs_ref[...] = a[:, KV_ROW:]
        kvw_ref[...] = b[:, :KV_ROW]
        gn_ref[...] = b[:, KV_ROW:]


def pack_projection(w_in, q_gain, ks_gain, kw_gain):
    cut = np.cumsum(SPLITS)[:-1].tolist()
    x_rg, g_rg, q, kv_c, kv_s, kv_w, g_nsa, x_pool, g_br = jnp.split(w_in, cut, axis=1)
    per_group = 3 * HPG
    gn = jnp.zeros((w_in.shape[0], GN_COLS), w_in.dtype)
    for g in range(KV_HEADS):
        gn = gn.at[:, g * LANES:g * LANES + per_group].set(g_nsa[:, g * per_group:(g + 1) * per_group])
    w = jnp.concatenate([g_br, x_rg, g_rg, x_pool, q, kv_c, kv_s, kv_w, gn], axis=1).astype(BF16)
    ones_v = jnp.ones((KV_HEADS * HEAD_DIM,), F32)
    zeros_v = jnp.zeros((KV_HEADS * HEAD_DIM,), F32)
    gain = jnp.concatenate([jnp.ones((ZA_COLS,), F32), jnp.tile(q_gain.astype(F32), N_HEADS) * HEAD_DIM ** -0.5,
                            jnp.ones((KV_ROW,), F32),
                            jnp.tile(ks_gain.astype(F32), KV_HEADS), ones_v,
                            jnp.tile(kw_gain.astype(F32), KV_HEADS), ones_v,
                            jnp.ones((GN_COLS,), F32)])
    mask = jnp.concatenate([jnp.zeros((ZA_COLS,), F32), jnp.ones((Q_COLS,), F32), jnp.zeros((KV_ROW,), F32),
                            ones_v, zeros_v, ones_v, zeros_v, jnp.zeros((GN_COLS,), F32)])
    return w, gain[None], mask[None]


def projection(x, norm_gain, packed):
    w, gain, mask = packed
    T = x.shape[0]
    tm = _pick(T, (1024, 512, 256, 128))
    seg = jnp.asarray((np.arange(PROJ_HALF)[:, None] // HEAD_DIM == np.arange(PROJ_HALF)[None, :] // HEAD_DIM)
                      .astype(np.float32), BF16)
    row = lambda i, j: (i, 0)
    return pl.pallas_call(
        _proj_kernel,
        grid=(T // tm, PROJ_COLS // PROJ_TN),
        in_specs=[pl.BlockSpec((tm, D_MODEL), row),
                  pl.BlockSpec((1, D_MODEL), lambda i, j: (0, 0)),
                  pl.BlockSpec((D_MODEL, PROJ_TN), lambda i, j: (0, j)),
                  pl.BlockSpec((PROJ_HALF, PROJ_HALF), lambda i, j: (0, 0)),
                  pl.BlockSpec((1, PROJ_TN), lambda i, j: (0, j)),
                  pl.BlockSpec((1, PROJ_TN), lambda i, j: (0, j))],
        out_specs=[pl.BlockSpec((tm, PROJ_TN), lambda i, j: (i, jnp.minimum(j, N_ZA - 1))),
                   pl.BlockSpec((tm, Q_COLS), row),
                   pl.BlockSpec((tm, KV_ROW), row), pl.BlockSpec((tm, KV_ROW), row), pl.BlockSpec((tm, KV_ROW), row),
                   pl.BlockSpec((tm, GN_COLS), row)],
        out_shape=[jax.ShapeDtypeStruct((T, ZA_STORE), F32), jax.ShapeDtypeStruct((T, Q_COLS), BF16),
                   jax.ShapeDtypeStruct((T, KV_ROW), F32), jax.ShapeDtypeStruct((T, KV_ROW), F32),
                   jax.ShapeDtypeStruct((T, KV_ROW), F32), jax.ShapeDtypeStruct((T, GN_COLS), F32)],
        scratch_shapes=[pltpu.VMEM((tm, D_MODEL), BF16)],
        compiler_params=pltpu.CompilerParams(dimension_semantics=("parallel", "arbitrary"),
                                             vmem_limit_bytes=VMEM_LIMIT_BYTES),
        name="projection",
    )(x, norm_gain.astype(F32)[None], w, seg, gain, mask)


CMP_PAIRS = CMP_STRIDE // 2
CMP_GW = KV_HEADS * CMP_HIDDEN


def _compress_kernel(*refs, n_pages, paged):
    if paged:
        pages = refs[1:1 + n_pages]
        eye_ref, w1_ref, pe_ref, w2_ref, kn_ref, ck_ref, cv_ref, half_ref = refs[1 + n_pages:]
        page = pages[0].shape[-1]
        m = n_pages * page // CMP_STRIDE
        for k, pg in enumerate(pages):
            for e in range(2):
                slab = pg[0, 0, e].reshape(KV_HEADS * HEAD_DIM, page).astype(BF16)
                half_ref[e, k * page:(k + 1) * page, :] = _nt(eye_ref[...], slab)
    else:
        half_refs = refs[:2]
        w1_ref, pe_ref, w2_ref, kn_ref, ck_ref, cv_ref = refs[2:]
        m = half_refs[0].shape[1] // CMP_STRIDE
    for e, out_ref in enumerate((ck_ref, cv_ref)):
        acc = jnp.zeros((m, 2 * CMP_GW), F32)
        for p in range(CMP_PAIRS):
            def rows(s):
                if paged:
                    return half_ref[e, pl.ds(s, m, stride=CMP_STRIDE), :]
                return half_refs[e][0, pl.ds(s, m, stride=CMP_STRIDE), :]
            a = jnp.concatenate([rows(2 * p), rows(2 * p + 1)], axis=1).astype(BF16)
            acc = acc + jnp.dot(a, w1_ref[e, p], preferred_element_type=F32)
        h = acc[:, :CMP_GW] + pltpu.roll(acc[:, CMP_GW:], m - 1, axis=0) + pe_ref[e]
        gl = jax.nn.gelu(h).astype(BF16)
        for g in range(KV_HEADS):
            c = jnp.dot(gl, w2_ref[e, g], preferred_element_type=F32)
            if e == 0:
                c = c * lax.rsqrt(jnp.sum(c * c, axis=1, keepdims=True) * (1.0 / HEAD_DIM) + EPS) * kn_ref[...]
            out_ref[0, g] = c.astype(out_ref.dtype)


def pack_compress(w1, w2, pe, kn):
    halves = w1.reshape(2, 2, CMP_STRIDE, HEAD_DIM, CMP_HIDDEN)
    eye = jnp.eye(KV_HEADS, dtype=w1.dtype)
    bd = jnp.einsum('ehsdf,gk->esgdhkf', halves, eye)
    w1p = bd.reshape(2, CMP_PAIRS, 2 * KV_HEADS * HEAD_DIM, 2 * CMP_GW).astype(BF16)
    pe_term = jnp.einsum('led,eldf->ef', pe, w1)
    pe_t = jnp.tile(pe_term, (1, KV_HEADS))[:, None, :].astype(F32)
    w2p = jnp.zeros((2, KV_HEADS, CMP_GW, LANES), w2.dtype)
    for g in range(KV_HEADS):
        w2p = w2p.at[:, g, g * CMP_HIDDEN:(g + 1) * CMP_HIDDEN, :HEAD_DIM].set(w2)
    knp = jnp.concatenate([kn.astype(F32), jnp.zeros((LANES - HEAD_DIM,), F32)])[None]
    return w1p, pe_t, w2p.astype(BF16), knp


def compress(rows, packed, page_table=None, layer=0):
    w1p, pe_t, w2p, knp = packed
    paged = page_table is not None
    if not paged:
        bx, n_pages = rows.shape[0], 1
        m = rows.shape[1] // CMP_STRIDE
        data_specs = [pl.BlockSpec((1, rows.shape[1], LANES), lambda b, e=e: (b, 0, e)) for e in range(2)]
        const = lambda nd: (lambda b: (0,) * nd)
        out_map = lambda b: (b, 0, 0, 0)
        data, scratch = [rows, rows], []
    else:
        bx, n_pages = page_table.shape
        page = rows.shape[-1]
        assert page == LANES and KV_HEADS * HEAD_DIM == LANES
        m = n_pages * page // CMP_STRIDE
        eye = jnp.asarray(np.eye(LANES, dtype=np.float32), BF16)
        const = lambda nd: (lambda b, pt: (0,) * nd)
        data_specs = [pl.BlockSpec((1, 1) + rows.shape[2:], lambda b, pt, k=k: (layer, pt[b, k], 0, 0, 0, 0))
                      for k in range(n_pages)] + [pl.BlockSpec(eye.shape, const(2))]
        out_map = lambda b, pt: (b, 0, 0, 0)
        data, scratch = [page_table] + [rows] * n_pages + [eye], [pltpu.VMEM((2, n_pages * page, LANES), F32)]
    in_specs = data_specs + [pl.BlockSpec(w1p.shape, const(4)), pl.BlockSpec(pe_t.shape, const(3)),
                             pl.BlockSpec(w2p.shape, const(4)), pl.BlockSpec(knp.shape, const(2))]
    out_spec = pl.BlockSpec((1, KV_HEADS, m, LANES), out_map)
    out_shape = jax.ShapeDtypeStruct((bx, KV_HEADS, m, LANES), BF16)
    grid_spec = pltpu.PrefetchScalarGridSpec(num_scalar_prefetch=int(paged), grid=(bx,), in_specs=in_specs,
                                             out_specs=[out_spec, out_spec], scratch_shapes=scratch)
    args = data + [w1p, pe_t, w2p, knp]
    return pl.pallas_call(
        functools.partial(_compress_kernel, n_pages=n_pages, paged=paged),
        grid_spec=grid_spec,
        out_shape=[out_shape, out_shape],
        compiler_params=pltpu.CompilerParams(dimension_semantics=("parallel",), vmem_limit_bytes=VMEM_LIMIT_BYTES),
        name="compress",
    )(*args)


MIX_TT = 1024
CONV_HALO = SUBLANES
SCAN_UNROLL = 4


def _log1p(y):
    u = 1.0 + y
    return jnp.where(u == 1.0, y, jnp.log(u) * (y / jnp.where(u == 1.0, 1.0, u - 1.0)))


def _neg_expm1(x):
    t = jnp.tanh(0.5 * x)
    return -2.0 * t / (1.0 - t)


def _softplus(x):
    return jnp.maximum(x, 0.0) + _log1p(jnp.exp(-jnp.abs(x)))


def _rglru_coeffs(xc, wa_ref, ba_ref, wx_ref, bx_ref, lam_ref):
    xb = xc.astype(BF16)
    r = jax.nn.sigmoid(jnp.dot(xb, wa_ref[...], preferred_element_type=F32) + ba_ref[...])
    i = jax.nn.sigmoid(jnp.dot(xb, wx_ref[...], preferred_element_type=F32) + bx_ref[...])
    log_a = -RG_C * r * _softplus(-lam_ref[...])
    return jnp.exp(log_a), jnp.sqrt(_neg_expm1(2.0 * log_a)) * (i * xc)


def _pool_project(sums_minus, wp_ref, scale_ref):
    return jnp.dot(sums_minus.astype(BF16), wp_ref[...], preferred_element_type=F32) * scale_ref[...]


def _mixer_seq_kernel(xrg_ref, grg_ref, xpool_ref, conv0_ref, h0_ref, pool0_ref, cw_ref, cb_ref, wa_ref, ba_ref,
                      wx_ref, bx_ref, lam_ref, wp_ref, ps_ref, ya_ref, yc_ref, convn_ref, hn_ref, pooln_ref,
                      xe_ref, pe_ref, a_ref, u_ref, h_ref, carry_ref, *, start_pos):
    i = pl.program_id(1)
    tt = xrg_ref.shape[0]

    @pl.when(i == 0)
    def _():
        xe_ref[0:CONV_HALO, :] = conv0_ref[0]
        pe_ref[0:POOL_MAX, :] = pool0_ref[0]
        carry_ref[...] = h0_ref[0]

    xe_ref[CONV_HALO:CONV_HALO + tt, :] = xrg_ref[...]
    xc = cb_ref[...] + sum(cw_ref[k:k + 1, :] * xe_ref[CONV_HALO - (CONV_W - 1) + k:CONV_HALO - (CONV_W - 1) + k + tt, :]
                           for k in range(CONV_W))
    a, u = _rglru_coeffs(xc, wa_ref, ba_ref, wx_ref, bx_ref, lam_ref)
    a_ref[...] = a
    u_ref[...] = u

    row = lax.broadcasted_iota(jnp.int32, (SUBLANES, D_RNN), 0)

    def block(j, carry):
        r0 = pl.multiple_of(j * SUBLANES, SUBLANES)
        ab = a_ref[pl.ds(r0, SUBLANES), :]
        ub = u_ref[pl.ds(r0, SUBLANES), :]
        for d in (1, 2, 4):
            a_sh = jnp.where(row >= d, pltpu.roll(ab, d, axis=0), 1.0)
            u_sh = jnp.where(row >= d, pltpu.roll(ub, d, axis=0), 0.0)
            ub = ab * u_sh + ub
            ab = ab * a_sh
        hb = ab * carry + ub
        h_ref[pl.ds(r0, SUBLANES), :] = hb
        return jnp.broadcast_to(hb[SUBLANES - 1:SUBLANES, :], (SUBLANES, D_RNN))

    carry = lax.fori_loop(0, tt // SUBLANES, block, carry_ref[...], unroll=SCAN_UNROLL)
    carry_ref[...] = carry
    ya_ref[...] = (h_ref[...] * jax.nn.gelu(grg_ref[...])).astype(ya_ref.dtype)

    pe_ref[POOL_MAX:POOL_MAX + tt, :] = xpool_ref[...]
    pos = start_pos + i * tt + lax.broadcasted_iota(jnp.int32, (tt, 1), 0)
    parts = []
    for g, w in enumerate(POOL_WINDOWS):
        lanes = slice(g * POOL_GW, (g + 1) * POOL_GW)
        s = sum(pe_ref[POOL_MAX - k:POOL_MAX - k + tt, lanes] for k in range(w))
        cnt = jnp.minimum(pos + 1, w).astype(F32)
        parts.append(s / cnt - pe_ref[POOL_MAX:POOL_MAX + tt, lanes])
    yc_ref[...] = _pool_project(jnp.concatenate(parts, axis=1), wp_ref, ps_ref).astype(yc_ref.dtype)

    @pl.when(i == pl.num_programs(1) - 1)
    def _():
        convn_ref[0] = xe_ref[CONV_HALO + tt - (CONV_W - 1):CONV_HALO + tt, :]
        hn_ref[0] = carry[0:1, :]
        pooln_ref[0] = pe_ref[POOL_MAX + tt - (POOL_MAX - 1):POOL_MAX + tt, :]

    xe_ref[0:CONV_HALO, :] = xe_ref[tt:tt + CONV_HALO, :]
    pe_ref[0:POOL_MAX, :] = pe_ref[tt:tt + POOL_MAX, :]


def pack_mixer(conv_w, conv_b, w_a, b_a, w_x, b_x, lam, w_pool, scale):
    def block_diag(w):
        n, c, d = w.shape
        return jnp.einsum('ncd,nm->ncmd', w, jnp.eye(n, dtype=w.dtype)).reshape(n * c, n * d).astype(BF16)

    row = lambda v: v.astype(F32)[None]
    return (conv_w.astype(F32), row(conv_b), block_diag(w_a), row(b_a), block_diag(w_x), row(b_x), row(lam),
            block_diag(w_pool), row(scale))


def mixer_seq(za, batch, seq, packed, conv0, h0, pool0, start_pos):
    tt = min(MIX_TT, seq)
    nt = seq // tt
    conv_pad = jnp.pad(conv0.astype(F32), ((0, 0), (CONV_HALO - (CONV_W - 1), 0), (0, 0)))
    pool_pad = jnp.pad(pool0.astype(F32), ((0, 0), (1, 0), (0, 0)))
    h_pad = jnp.broadcast_to(h0.astype(F32)[:, None, :], (batch, SUBLANES, D_RNN))
    col = lambda c: pl.BlockSpec((tt, D_RNN), lambda b, i: (b * nt + i, c))
    state = lambda rows: pl.BlockSpec((1, rows, D_RNN), lambda b, i: (b, 0, 0))
    full = lambda a: pl.BlockSpec(a.shape, lambda b, i: (0,) * a.ndim)
    out_rows = pl.BlockSpec((tt, D_RNN), lambda b, i: (b * nt + i, 0))
    return pl.pallas_call(
        functools.partial(_mixer_seq_kernel, start_pos=start_pos),
        grid=(batch, nt),
        in_specs=[col(ZA_XRG // D_RNN), col(ZA_GRG // D_RNN), col(ZA_XPOOL // D_RNN), state(CONV_HALO), state(SUBLANES),
                  state(POOL_MAX)] + [full(a) for a in packed],
        out_specs=[out_rows, out_rows, state(CONV_W - 1), state(1), state(POOL_MAX - 1)],
        out_shape=[jax.ShapeDtypeStruct((batch * seq, D_RNN), BF16), jax.ShapeDtypeStruct((batch * seq, D_POOL), BF16),
                   jax.ShapeDtypeStruct((batch, CONV_W - 1, D_RNN), F32), jax.ShapeDtypeStruct((batch, 1, D_RNN), F32),
                   jax.ShapeDtypeStruct((batch, POOL_MAX - 1, D_POOL), F32)],
        scratch_shapes=[pltpu.VMEM((CONV_HALO + tt, D_RNN), F32), pltpu.VMEM((POOL_MAX + tt, D_POOL), F32),
                        pltpu.VMEM((tt, D_RNN), F32), pltpu.VMEM((tt, D_RNN), F32), pltpu.VMEM((tt, D_RNN), F32),
                        pltpu.VMEM((SUBLANES, D_RNN), F32)],
        compiler_params=pltpu.CompilerParams(dimension_semantics=("parallel", "arbitrary"),
                                             vmem_limit_bytes=VMEM_LIMIT_BYTES),
        name="mixer_seq",
    )(za, za, za, conv_pad, h_pad, pool_pad, *packed)


def _mixer_step_kernel(xrg_ref, grg_ref, xpool_ref, conv_ref, h0_ref, pool_ref, cw_ref, cb_ref, wa_ref, ba_ref,
                       wx_ref, bx_ref, lam_ref, wp_ref, ps_ref, ya_ref, yc_ref, hn_ref, *, start_pos):
    x = xrg_ref[...]
    xc = cb_ref[...] + cw_ref[CONV_W - 1:CONV_W, :] * x + sum(cw_ref[k:k + 1, :] * conv_ref[k] for k in range(CONV_W - 1))
    a, u = _rglru_coeffs(xc, wa_ref, ba_ref, wx_ref, bx_ref, lam_ref)
    h = a * h0_ref[...] + u
    hn_ref[...] = h
    ya_ref[...] = (h * jax.nn.gelu(grg_ref[...])).astype(ya_ref.dtype)
    xp = xpool_ref[...]
    parts = []
    for g, w in enumerate(POOL_WINDOWS):
        lanes = slice(g * POOL_GW, (g + 1) * POOL_GW)
        s = xp[:, lanes] + sum(pool_ref[POOL_MAX - 1 - k][:, lanes] for k in range(1, w))
        parts.append(s / float(min(start_pos + 1, w)) - xp[:, lanes])
    yc_ref[...] = _pool_project(jnp.concatenate(parts, axis=1), wp_ref, ps_ref).astype(yc_ref.dtype)


def mixer_step(za, packed, conv_state, h0, pool_state, start_pos):
    batch = za.shape[0]
    conv_t = jnp.swapaxes(conv_state.astype(F32), 0, 1)
    pool_t = jnp.swapaxes(pool_state.astype(F32), 0, 1)
    col = lambda c: pl.BlockSpec((batch, D_RNN), lambda i: (0, c))
    full = lambda a: pl.BlockSpec(a.shape, lambda i: (0,) * a.ndim)
    rows = pl.BlockSpec((batch, D_RNN), lambda i: (0, 0))
    return pl.pallas_call(
        functools.partial(_mixer_step_kernel, start_pos=start_pos),
        grid=(1,),
        in_specs=[col(ZA_XRG // D_RNN), col(ZA_GRG // D_RNN), col(ZA_XPOOL // D_RNN), full(conv_t), rows, full(pool_t)]
        + [full(a) for a in packed],
        out_specs=[rows, rows, rows],
        out_shape=[jax.ShapeDtypeStruct((batch, D_RNN), BF16), jax.ShapeDtypeStruct((batch, D_POOL), BF16),
                   jax.ShapeDtypeStruct((batch, D_RNN), F32)],
        compiler_params=pltpu.CompilerParams(vmem_limit_bytes=VMEM_LIMIT_BYTES),
        name="mixer_step",
    )(za, za, za, conv_t, h0.astype(F32), pool_t, *packed)


QB = Q_BLOCK
ROWS = HPG * QB
N_WIN_TILES = WINDOW // QB + 1
N_SEL_BIAS = REL_MAX_DIST // QB + 2
SEL_SPAN = 8
KV_CHUNK = 512


def _nsa_prompt_kernel(q_ref, ck_ref, cv_ref, kvs_ref, kvw_ref, gate_ref, bc_ref, tbs_ref, tbw_ref,
                       ovl_ref, eye_ref, pq_ref, pk_ref, pv_ref, onehot_ref, o_ref,
                       acc_ref, m_ref, qa_ref, comb_ref, ks_ref, vs_ref, kw_ref, vw_ref, cnt_ref, *, n_cmp, n_blk):
    qb = pl.program_id(2)
    t0 = qb * QB
    ncp = ck_ref.shape[2]
    seq = kvs_ref.shape[1]
    lane_row = lax.broadcasted_iota(jnp.int32, (1, LANES), 1)

    @pl.when(qb == 0)
    def _():
        ones_hi = jnp.where(lane_row >= HEAD_DIM, 1.0, 0.0)
        kw_ref[0:WINDOW, :] = jnp.broadcast_to(jnp.where(lane_row == HEAD_DIM, 1.0, 0.0), (WINDOW, LANES)).astype(BF16)
        vw_ref[0:WINDOW, :] = jnp.zeros((WINDOW, LANES), BF16)

        def stage(c, carry):
            r = pl.multiple_of(c * KV_CHUNK, KV_CHUNK)
            sel = kvs_ref[0, pl.ds(r, KV_CHUNK), :].astype(BF16)
            win = kvw_ref[0, pl.ds(r, KV_CHUNK), :].astype(BF16)
            ks_ref[pl.ds(r, KV_CHUNK), :] = (jnp.dot(sel, pk_ref[0], preferred_element_type=F32)
                                             + onehot_ref[pl.ds(r, KV_CHUNK), :].astype(F32)).astype(BF16)
            vs_ref[pl.ds(r, KV_CHUNK), :] = (jnp.dot(sel, pv_ref[0], preferred_element_type=F32) + ones_hi).astype(BF16)
            kw_ref[pl.ds(WINDOW + r, KV_CHUNK), :] = jnp.dot(win, pk_ref[0], preferred_element_type=F32).astype(BF16)
            vw_ref[pl.ds(WINDOW + r, KV_CHUNK), :] = (jnp.dot(win, pv_ref[0], preferred_element_type=F32)
                                                      + ones_hi).astype(BF16)
            return carry

        lax.fori_loop(0, seq // KV_CHUNK, stage, 0)

    q4 = jnp.dot(q_ref[0], pq_ref[...], preferred_element_type=F32)
    q3 = jnp.concatenate([q4[:, j * LANES:(j + 1) * LANES] for j in range(HPG)], axis=0)
    q = q3.astype(BF16)
    sig = jax.nn.sigmoid(gate_ref[0])
    gates = [jnp.concatenate([sig[:, 3 * j + c:3 * j + c + 1] for j in range(HPG)], axis=0) for c in range(3)]

    lc = _nt(q, ck_ref[0, 0]) + bc_ref[0].reshape(ROWS, ncp)
    tok = t0 + lax.broadcasted_iota(jnp.int32, (HPG, QB, ncp), 1).reshape(ROWS, ncp)
    col = lax.broadcasted_iota(jnp.int32, (ROWS, ncp), 1)
    valid = (tok >= col * CMP_STRIDE + (L_CMP - 1)) & (col < n_cmp)
    lc = jnp.where(valid, lc, NEG)
    mx = jnp.max(lc, axis=1, keepdims=True)
    p = jnp.where(valid, jnp.exp(lc - mx), 0.0)
    pc = p / jnp.maximum(jnp.sum(p, axis=1, keepdims=True), 1e-30)
    comb_ref[...] = gates[0] * jnp.dot(pc.astype(BF16), cv_ref[0, 0], preferred_element_type=F32)

    pcs = pc[0:QB] + pc[QB:2 * QB] + pc[2 * QB:3 * QB] + pc[3 * QB:4 * QB]
    hi = pcs.astype(BF16)
    lo = (pcs - hi.astype(F32)).astype(BF16)
    imp = _nt(ovl_ref[...], hi) + _nt(ovl_ref[...], lo)
    blk = lax.broadcasted_iota(jnp.int32, (n_blk, QB), 0)
    cur = (t0 + lax.broadcasted_iota(jnp.int32, (n_blk, QB), 1)) // SEL_BLOCK
    forced = (blk == 0) | (blk == cur) | (blk == cur - 1)
    score = jnp.where(blk > cur, -1.0, jnp.where(forced, FORCE, imp))
    chunks = [score[r:r + SUBLANES] for r in range(0, n_blk, SUBLANES)]
    sub = lax.broadcasted_iota(jnp.int32, (SUBLANES, QB), 0)
    cnt_ref[...] = jnp.zeros((n_blk, QB), F32)
    last_blk = (t0 + QB - 1) // SEL_BLOCK
    for mc in range(0, n_blk, SUBLANES):
        @pl.when(mc <= last_blk)
        def _(mc=mc):
            for r, ch in enumerate(chunks):
                first = r * SUBLANES
                part = jnp.zeros((SUBLANES, QB), F32)
                for m in range(mc, mc + SUBLANES):
                    row = jnp.broadcast_to(score[m:m + 1, :], (SUBLANES, QB))
                    if first > m:
                        beats = jnp.where(row >= ch, 1.0, 0.0)
                    elif first + SUBLANES - 1 < m:
                        beats = jnp.where(row > ch, 1.0, 0.0)
                    else:
                        beats = jnp.where(sub + first > m, jnp.where(row >= ch, 1.0, 0.0), jnp.where(row > ch, 1.0, 0.0))
                    part = part + beats
                cnt_ref[first:first + SUBLANES, :] += part
    sel_neg = jnp.where(cnt_ref[...] < float(min(N_SELECT, n_blk)), 0.0, NEG)
    pieces = [jnp.zeros((HEAD_DIM, QB), F32), sel_neg]
    if n_blk < HEAD_DIM:
        pieces.append(jnp.zeros((HEAD_DIM - n_blk, QB), F32))
    placed_t = jnp.concatenate(pieces, axis=0).astype(BF16)
    placed = _nt(eye_ref[...], placed_t)
    qa_ref[...] = (q3 + jnp.concatenate([placed] * HPG, axis=0)).astype(BF16)

    m_ref[...] = jnp.full((ROWS, LANES), M_INIT, F32)
    acc_ref[...] = jnp.zeros((ROWS, LANES), F32)
    n_bias = tbs_ref.shape[1]

    span = SEL_SPAN * QB
    n_spans = qb // SEL_SPAN + 1

    def body(kk, carry):
        off = pl.multiple_of(kk * span, span)
        s = _nt(qa_ref[...], ks_ref[pl.ds(off, span), :])
        parts = []
        for u in range(SEL_SPAN):
            idx = jnp.clip(qb - (kk * SEL_SPAN + u), -1, n_bias - 2) + 1
            parts.append(s[:, u * QB:(u + 1) * QB] + tbs_ref[:, pl.ds(idx, 1)].reshape(ROWS, LANES))
        tile_max = functools.reduce(jnp.maximum, parts)
        m_old = m_ref[...]
        m_new = jnp.maximum(m_old, jnp.max(tile_max, axis=1, keepdims=True))
        alpha = jnp.exp(m_old - m_new)
        pr = jnp.concatenate([jnp.exp(x - m_new).astype(BF16) for x in parts], axis=1)
        acc_ref[...] = alpha * acc_ref[...] + jnp.dot(pr, vs_ref[pl.ds(off, SEL_SPAN * QB), :],
                                                      preferred_element_type=F32)
        m_ref[...] = m_new
        return carry

    lax.fori_loop(0, n_spans, body, 0)
    acc = acc_ref[...]
    comb_ref[...] += gates[1] * (acc / pltpu.roll(acc, HEAD_DIM, axis=1))

    qw = (q3 + jnp.where(lane_row == HEAD_DIM, NEG, 0.0)).astype(BF16)
    w_off = pl.multiple_of(t0, QB)
    sw = _nt(qw, kw_ref[pl.ds(w_off, WINDOW + QB), :]) + tbw_ref[...].reshape(ROWS, WINDOW + QB)
    pw = jnp.exp(sw - jnp.max(sw, axis=1, keepdims=True))
    accw = jnp.dot(pw.astype(BF16), vw_ref[pl.ds(w_off, WINDOW + QB), :], preferred_element_type=F32)
    comb_ref[...] += gates[2] * (accw / pltpu.roll(accw, HEAD_DIM, axis=1))

    comb = comb_ref[...]
    lane = lax.broadcasted_iota(jnp.int32, (QB, LANES), 1)
    for half in range(HPG // 2):
        a = comb[(2 * half) * QB:(2 * half + 1) * QB]
        b = comb[(2 * half + 1) * QB:(2 * half + 2) * QB]
        o_ref[0, :, half * LANES:(half + 1) * LANES] = jnp.where(lane < HEAD_DIM, a,
                                                                 pltpu.roll(b, HEAD_DIM, axis=1)).astype(o_ref.dtype)


def _bias_by_distance(rel_bias):
    max_d = (N_SEL_BIAS + 1) * QB
    return jnp.transpose(rel_bias.astype(F32)[rel_bucket(jnp.arange(max_d))])


def _ext(tab_t, lo, hi, ok_lo=None, ok_hi=None):
    n_heads, depth = tab_t.shape
    y = np.arange(lo, hi)
    parts = [jnp.broadcast_to(tab_t[:, :1], (n_heads, int(np.sum(y < 0)))), tab_t[:, max(lo, 0):max(min(hi, depth), 0)],
             jnp.broadcast_to(tab_t[:, -1:], (n_heads, int(np.sum(y > depth - 1))))]
    arr = jnp.concatenate(parts, axis=1)
    ok = np.ones(y.shape, bool)
    if ok_lo is not None:
        ok &= y >= ok_lo
    if ok_hi is not None:
        ok &= y <= ok_hi
    return jnp.where(jnp.asarray(ok)[None], arr, NEG)


def _toeplitz_tiles(ext, lo, ks):
    period = 2 * QB - 1
    seg = jnp.stack([ext[:, QB * k - (QB - 1) - lo:QB * k + QB - lo] for k in ks], axis=1)
    w = jnp.concatenate([seg[..., QB - 1::-1], seg[..., :QB - 1:-1]], axis=-1)
    return jnp.tile(w, (1, 1, QB))[..., :QB * (period - 1)].reshape(w.shape[:2] + (QB, period - 1))[..., :QB]


def rel_bias_tables(rel_bias, seq):
    tab_t = _bias_by_distance(rel_bias)
    max_d = tab_t.shape[1]
    lo = -(2 * QB - 1)
    tbs = _toeplitz_tiles(_ext(tab_t, lo, max_d, 0, None), lo, range(-1, N_SEL_BIAS))
    tbw = _toeplitz_tiles(_ext(tab_t, lo, max_d, 0, WINDOW), lo, range(N_WIN_TILES - 1, -1, -1))
    tbw = jnp.transpose(tbw, (0, 2, 1, 3)).reshape(N_HEADS, QB, N_WIN_TILES * QB)
    nqb = seq // QB
    ncp = seq // CMP_STRIDE
    per_qb = QB // CMP_STRIDE
    width = ncp + per_qb * (nqb - 1)
    c0 = QB * (nqb - 1) - (L_CMP - 1)
    ext = _ext(tab_t, c0 - CMP_STRIDE * (width - 1), c0 + QB)
    n_z = width + per_qb - 1
    f = ext[:, :CMP_STRIDE * n_z].reshape(N_HEADS, n_z, CMP_STRIDE)[:, ::-1]
    v = jnp.concatenate([jnp.swapaxes(f[:, per_qb - 1 - a:per_qb - 1 - a + width], 1, 2) for a in range(per_qb)],
                        axis=1)
    bc = jnp.stack([v[:, :, per_qb * (nqb - 1 - b):per_qb * (nqb - 1 - b) + ncp] for b in range(nqb)])
    return tbs, tbw, bc


def nsa_prompt_pallas(q, ck, cv, kv_sel, kv_win, gn, tables):
    B, S = q.shape[:2]
    assert S % (SEL_SPAN * QB) == 0 and S % KV_CHUNK == 0 and S // SEL_BLOCK <= HEAD_DIM
    tbs, tbw, bc = tables
    nqb = S // QB
    ncp = S // CMP_STRIDE
    n_cmp = ncp - 1
    n_blk = S // SEL_BLOCK
    c0 = np.arange(ncp)[None, :] * CMP_STRIDE
    s0 = np.arange(n_blk)[:, None] * SEL_BLOCK
    ovl_t = ((c0 <= s0 + SEL_BLOCK - 1) & (c0 + L_CMP - 1 >= s0) & (np.arange(ncp)[None, :] < n_cmp))
    ovl_t = jnp.asarray(ovl_t.astype(np.float32), BF16)
    eye = jnp.asarray(np.eye(QB, dtype=np.float32), BF16)
    gw = HPG * HEAD_DIM
    pq = np.zeros((gw, HPG * LANES), np.float32)
    pq[np.arange(gw), (np.arange(gw) // HEAD_DIM) * LANES + np.arange(gw) % HEAD_DIM] = 1.0
    pk = np.zeros((KV_HEADS, KV_ROW, LANES), np.float32)
    pv = np.zeros((KV_HEADS, KV_ROW, LANES), np.float32)
    for g in range(KV_HEADS):
        pk[g, g * HEAD_DIM + np.arange(HEAD_DIM), np.arange(HEAD_DIM)] = 1.0
        pv[g, (KV_HEADS + g) * HEAD_DIM + np.arange(HEAD_DIM), np.arange(HEAD_DIM)] = 1.0
    onehot = np.zeros((S, LANES), np.float32)
    onehot[np.arange(S), HEAD_DIM + np.arange(S) // SEL_BLOCK] = 1.0

    kv_spec = pl.BlockSpec((1, S, KV_ROW), lambda b, g, i: (b, 0, 0))
    cmp_spec = pl.BlockSpec((1, 1, ncp, LANES), lambda b, g, i: (b, g, 0, 0))
    const2 = lambda b, g, i: (0, 0)
    return pl.pallas_call(
        functools.partial(_nsa_prompt_kernel, n_cmp=n_cmp, n_blk=n_blk),
        grid=(B, KV_HEADS, nqb),
        in_specs=[
            pl.BlockSpec((1, QB, gw), lambda b, g, i: (b, i, g)),
            cmp_spec, cmp_spec, kv_spec, kv_spec,
            pl.BlockSpec((1, QB, LANES), lambda b, g, i: (b, i, g)),
            pl.BlockSpec((1, HPG, QB, ncp), lambda b, g, i: (i, g, 0, 0)),
            pl.BlockSpec((HPG, N_SEL_BIAS + 1, QB, QB), lambda b, g, i: (g, 0, 0, 0)),
            pl.BlockSpec((HPG, QB, N_WIN_TILES * QB), lambda b, g, i: (g, 0, 0)),
            pl.BlockSpec((n_blk, ncp), const2),
            pl.BlockSpec((QB, QB), const2),
            pl.BlockSpec((gw, HPG * LANES), const2),
            pl.BlockSpec((1, KV_ROW, LANES), lambda b, g, i: (g, 0, 0)),
            pl.BlockSpec((1, KV_ROW, LANES), lambda b, g, i: (g, 0, 0)),
            pl.BlockSpec((S, LANES), const2),
        ],
        out_specs=pl.BlockSpec((1, QB, gw), lambda b, g, i: (b, i, g)),
        out_shape=jax.ShapeDtypeStruct((B, S, N_HEADS * HEAD_DIM), BF16),
        scratch_shapes=[pltpu.VMEM((ROWS, LANES), F32), pltpu.VMEM((ROWS, LANES), F32),
                        pltpu.VMEM((ROWS, LANES), BF16), pltpu.VMEM((ROWS, LANES), F32),
                        pltpu.VMEM((S, LANES), BF16), pltpu.VMEM((S, LANES), BF16),
                        pltpu.VMEM((S + WINDOW, LANES), BF16), pltpu.VMEM((S + WINDOW, LANES), BF16),
                        pltpu.VMEM((n_blk, QB), F32)],
        compiler_params=pltpu.CompilerParams(dimension_semantics=("parallel", "parallel", "arbitrary"),
                                             vmem_limit_bytes=VMEM_LIMIT_BYTES),
        name="nsa_prompt",
    )(q, ck, cv, kv_sel, kv_win, gn, bc, tbs, tbw, ovl_t, eye, jnp.asarray(pq, BF16), jnp.asarray(pk, BF16),
      jnp.asarray(pv, BF16), jnp.asarray(onehot, BF16))


STEP_ROWS = SUBLANES
RANK_LANES = 2 * LANES


def _split3(x):
    hi = x.astype(BF16)
    r1 = x - hi.astype(F32)
    mid = r1.astype(BF16)
    return hi, mid, (r1 - mid.astype(F32)).astype(BF16)


def _nsa_step_kernel(*refs, n_pages, n_cmp):
    pages = refs[1:1 + n_pages]
    (q_ref, ck_ref, cv_ref, kvs_ref, kvw_ref, win_ref, gn_ref, bcs_ref, bsel_ref, bwin_ref, ovl_ref, oh_ref,
     pq_ref, pk_ref, pv_ref, o_ref, kc_ref, vs_ref, kw_ref, vw_ref) = refs[1 + n_pages:]
    page = pages[0].shape[-1]
    past = n_pages * page
    wb = win_ref.shape[-1]
    ncp = ck_ref.shape[2]
    first_row = lax.broadcasted_iota(jnp.int32, (LANES, 1), 0) == 0

    @pl.when(pl.program_id(0) == 0)
    def _():
        for g in range(KV_HEADS):
            kc_ref[g, HEAD_DIM:LANES, :] = jnp.zeros((LANES - HEAD_DIM, past + LANES), BF16)
            kc_ref[g, LANES:2 * LANES, 0:past] = oh_ref[...]
            kc_ref[g, LANES:2 * LANES, past:past + LANES] = jnp.zeros((LANES, LANES), BF16)
            vs_ref[g, HEAD_DIM:LANES, :] = jnp.ones((LANES - HEAD_DIM, past + LANES), BF16)
            kw_ref[g, HEAD_DIM:LANES, :] = jnp.zeros((LANES - HEAD_DIM, wb + LANES), BF16)
            vw_ref[g, HEAD_DIM:LANES, :] = jnp.ones((LANES - HEAD_DIM, wb + LANES), BF16)

    def place_new(row_ref, c0, k_dst, v_dst):
        new = jnp.where(first_row, row_ref[0], 0.0).astype(BF16)
        for g in range(KV_HEADS):
            k_dst[g, 0:HEAD_DIM, c0:c0 + LANES] = _nt(pk_ref[g], new)[0:HEAD_DIM].astype(BF16)
            v_dst[g, 0:HEAD_DIM, c0:c0 + LANES] = _nt(pv_ref[g], new)[0:HEAD_DIM].astype(BF16)

    for k, pg in enumerate(pages):
        for g in range(KV_HEADS):
            kc_ref[g, 0:HEAD_DIM, k * page:(k + 1) * page] = pg[0, 0, 0, g].astype(BF16)
            vs_ref[g, 0:HEAD_DIM, k * page:(k + 1) * page] = pg[0, 0, 1, g].astype(BF16)
    place_new(kvs_ref, past, kc_ref, vs_ref)
    for g in range(KV_HEADS):
        kw_ref[g, 0:HEAD_DIM, 0:wb] = win_ref[0, 0, 0, g].astype(BF16)
        vw_ref[g, 0:HEAD_DIM, 0:wb] = win_ref[0, 0, 1, g].astype(BF16)
    place_new(kvw_ref, wb, kw_ref, vw_ref)

    sig = jax.nn.sigmoid(gn_ref[0])
    lane = lax.broadcasted_iota(jnp.int32, (1, LANES), 1)
    rr = lax.broadcasted_iota(jnp.int32, (RANK_LANES, RANK_LANES), 0)
    cc = lax.broadcasted_iota(jnp.int32, (RANK_LANES, RANK_LANES), 1)
    ones_sq = jnp.ones((RANK_LANES, RANK_LANES), BF16)
    pad_rows = jnp.zeros((STEP_ROWS - HPG, LANES), F32)
    for g in range(KV_HEADS):
        gw = HPG * HEAD_DIM
        q4 = jnp.dot(q_ref[0][:, g * gw:(g + 1) * gw], pq_ref[...], preferred_element_type=F32)
        q8 = jnp.concatenate([q4[:, j * LANES:(j + 1) * LANES] for j in range(HPG)] + [pad_rows], axis=0)
        gates = [jnp.concatenate([sig[:, g * LANES + 3 * j + c:g * LANES + 3 * j + c + 1] for j in range(HPG)]
                                 + [pad_rows[:, 0:1]], axis=0) for c in range(3)]

        lc = _nt(q8.astype(BF16), ck_ref[0, g]) + bcs_ref[g]
        col = lax.broadcasted_iota(jnp.int32, (STEP_ROWS, ncp), 1)
        valid = col < n_cmp
        lc = jnp.where(valid, lc, NEG)
        p = jnp.where(valid, jnp.exp(lc - jnp.max(lc, axis=1, keepdims=True)), 0.0)
        pc = p / jnp.maximum(jnp.sum(p, axis=1, keepdims=True), 1e-30)
        comb = gates[0] * jnp.dot(pc.astype(BF16), cv_ref[0, g], preferred_element_type=F32)

        pcs = jnp.broadcast_to(jnp.sum(pc[0:HPG], axis=0, keepdims=True), (STEP_ROWS, ncp))
        hi = pcs.astype(BF16)
        lo = (pcs - hi.astype(F32)).astype(BF16)
        imp = (jnp.dot(hi, ovl_ref[...], preferred_element_type=F32)
               + jnp.dot(lo, ovl_ref[...], preferred_element_type=F32))[0:1]
        blk_id = lax.broadcasted_iota(jnp.int32, (1, RANK_LANES), 1)
        cur = past // SEL_BLOCK
        forced = (blk_id == 0) | (blk_id == cur) | (blk_id == cur - 1)
        score = jnp.where(blk_id > cur, -1.0, jnp.where(forced, FORCE, imp))
        s_n = jnp.broadcast_to(score, (RANK_LANES, RANK_LANES))
        diag = jnp.where(rr == cc, s_n, 0.0)
        s_m = sum(_nt(part, ones_sq) for part in _split3(diag))
        beats = jnp.where(rr < cc, jnp.where(s_m >= s_n, 1.0, 0.0), jnp.where(s_m > s_n, 1.0, 0.0))
        rank = jnp.sum(beats, axis=0, keepdims=True)
        sel_neg = jnp.where(rank < float(N_SELECT), 0.0, NEG)

        qa = jnp.concatenate([q8, jnp.broadcast_to(sel_neg[:, :LANES], (STEP_ROWS, LANES))], axis=1).astype(BF16)
        s_past = jnp.dot(qa, kc_ref[g, :, 0:past], preferred_element_type=F32) + bsel_ref[g, :, 0:past]
        s_new = (jnp.dot(qa, kc_ref[g, :, past:past + LANES], preferred_element_type=F32)
                 + bsel_ref[g, :, past:past + LANES] + sel_neg[:, LANES:LANES + 1])
        mx = jnp.maximum(jnp.max(s_past, axis=1, keepdims=True), jnp.max(s_new, axis=1, keepdims=True))
        acc = (_nt(jnp.exp(s_past - mx).astype(BF16), vs_ref[g, :, 0:past])
               + _nt(jnp.exp(s_new - mx).astype(BF16), vs_ref[g, :, past:past + LANES]))
        comb = comb + gates[1] * (acc / pltpu.roll(acc, HEAD_DIM, axis=1))

        sw = jnp.dot(q8.astype(BF16), kw_ref[g], preferred_element_type=F32) + bwin_ref[g]
        pw = jnp.exp(sw - jnp.max(sw, axis=1, keepdims=True))
        accw = _nt(pw.astype(BF16), vw_ref[g])
        comb = comb + gates[2] * (accw / pltpu.roll(accw, HEAD_DIM, axis=1))

        for half in range(HPG // 2):
            a = comb[2 * half:2 * half + 1]
            b = comb[2 * half + 1:2 * half + 2]
            o_ref[0, :, g * gw + half * LANES:g * gw + (half + 1) * LANES] = jnp.where(
                lane < HEAD_DIM, a, pltpu.roll(b, HEAD_DIM, axis=1)).astype(o_ref.dtype)


def step_bias_tables(rel_bias, past, wb, ncp):
    tab_t = _bias_by_distance(rel_bias)

    def rows(t):
        return jnp.pad(t.reshape(KV_HEADS, HPG, -1), ((0, 0), (0, STEP_ROWS - HPG), (0, 0)))

    d_first = past - (L_CMP - 1)
    cmp_t = _ext(tab_t, d_first - CMP_STRIDE * (ncp - 1), d_first + 1, 0, None)[:, ::CMP_STRIDE][:, ::-1]
    sel_t = _ext(tab_t, -(LANES - 1), past + 1, 0, None)[:, ::-1]
    win_t = _ext(tab_t, -(LANES - 1), wb + 1, 0, WINDOW)[:, ::-1]
    return rows(cmp_t), rows(sel_t), rows(win_t)


def nsa_step(q, ck, cv, kvs_new, kvw_new, sel_pool, win_buf, gn, page_table, tables, layer):
    B, n_pages = page_table.shape
    page = sel_pool.shape[-1]
    past = n_pages * page
    wb = win_buf.shape[-1]
    ncp = ck.shape[2]
    n_blk = past // SEL_BLOCK + 1
    assert n_blk <= LANES + 1 and page == LANES and wb % LANES == 0
    bcs, bsel, bwin = tables
    c0 = np.arange(ncp)[:, None] * CMP_STRIDE
    s0 = np.arange(RANK_LANES)[None, :] * SEL_BLOCK
    ovl = ((c0 <= s0 + SEL_BLOCK - 1) & (c0 + L_CMP - 1 >= s0) & (np.arange(ncp)[:, None] < ncp - 1)
           & (np.arange(RANK_LANES)[None, :] < n_blk))
    onehot = np.zeros((LANES, past), np.float32)
    onehot[np.arange(past) // SEL_BLOCK, np.arange(past)] = 1.0
    gw = HPG * HEAD_DIM
    pq = np.zeros((gw, HPG * LANES), np.float32)
    pq[np.arange(gw), (np.arange(gw) // HEAD_DIM) * LANES + np.arange(gw) % HEAD_DIM] = 1.0
    pk = np.zeros((KV_HEADS, LANES, KV_ROW), np.float32)
    pv = np.zeros((KV_HEADS, LANES, KV_ROW), np.float32)
    for g in range(KV_HEADS):
        pk[g, np.arange(HEAD_DIM), g * HEAD_DIM + np.arange(HEAD_DIM)] = 1.0
        pv[g, np.arange(HEAD_DIM), (KV_HEADS + g) * HEAD_DIM + np.arange(HEAD_DIM)] = 1.0
    consts = [jnp.asarray(ovl.astype(np.float32), BF16), jnp.asarray(onehot, BF16), jnp.asarray(pq, BF16),
              jnp.asarray(pk, BF16), jnp.asarray(pv, BF16)]

    row3 = lambda a: a.reshape(B, 1, a.shape[-1])
    per_b = lambda shape: pl.BlockSpec((1,) + shape, lambda b, pt: (b,) + (0,) * len(shape))
    full = lambda a: pl.BlockSpec(a.shape, lambda b, pt: (0,) * a.ndim)
    slab = (2, KV_HEADS, HEAD_DIM)
    page_specs = [pl.BlockSpec((1, 1) + slab + (page,), lambda b, pt, k=k: (layer, pt[b, k], 0, 0, 0, 0))
                  for k in range(n_pages)]
    win_spec = pl.BlockSpec((1, 1) + slab + (wb,), lambda b, pt: (layer, b, 0, 0, 0, 0))
    in_specs = page_specs + [per_b((1, Q_COLS)), per_b((KV_HEADS, ncp, LANES)), per_b((KV_HEADS, ncp, LANES)),
                             per_b((1, KV_ROW)), per_b((1, KV_ROW)), win_spec, per_b((1, GN_COLS)),
                             full(bcs), full(bsel), full(bwin)] + [full(c) for c in consts]
    grid_spec = pltpu.PrefetchScalarGridSpec(
        num_scalar_prefetch=1, grid=(B,), in_specs=in_specs, out_specs=per_b((1, Q_COLS)),
        scratch_shapes=[pltpu.VMEM((KV_HEADS, 2 * LANES, past + LANES), BF16), pltpu.VMEM((KV_HEADS, LANES, past + LANES), BF16),
                        pltpu.VMEM((KV_HEADS, LANES, wb + LANES), BF16), pltpu.VMEM((KV_HEADS, LANES, wb + LANES), BF16)])
    out = pl.pallas_call(
        functools.partial(_nsa_step_kernel, n_pages=n_pages, n_cmp=ncp - 1),
        grid_spec=grid_spec,
        out_shape=jax.ShapeDtypeStruct((B, 1, Q_COLS), BF16),
        compiler_params=pltpu.CompilerParams(dimension_semantics=("arbitrary",), vmem_limit_bytes=VMEM_LIMIT_BYTES),
        name="nsa_step",
    )(page_table, *([sel_pool] * n_pages), row3(q), ck, cv, row3(kvs_new), row3(kvw_new), win_buf, row3(gn),
      bcs, bsel, bwin, *consts)
    return out.reshape(B, Q_COLS)


def _merge_kernel(ya_ref, yb_ref, yc_ref, ga_ref, gb_ref, gc_ref, x_ref, wa_ref, wb_ref, wc_ref, wo_ref, o_ref):
    def branch(y_ref, g_ref, w_ref):
        return jax.nn.sigmoid(g_ref[...]) * jnp.dot(y_ref[...], w_ref[...], preferred_element_type=F32)

    merged = branch(ya_ref, ga_ref, wa_ref) + branch(yb_ref, gb_ref, wb_ref) + branch(yc_ref, gc_ref, wc_ref)
    o_ref[...] = x_ref[...] + jnp.dot(merged.astype(BF16), wo_ref[...], preferred_element_type=F32)


def merge(ya, yb, yc, za, x, wa, wb, wc, wo):
    T = x.shape[0]
    tm = _pick(T, (512, 256, 128))
    y_spec = pl.BlockSpec((tm, ya.shape[1]), lambda i: (i, 0))
    w_spec = pl.BlockSpec((ya.shape[1], D_MODEL), lambda i: (0, 0))
    return pl.pallas_call(
        _merge_kernel,
        grid=(T // tm,),
        in_specs=[y_spec, y_spec, y_spec,
                  pl.BlockSpec((tm, D_MODEL), lambda i: (i, 0)), pl.BlockSpec((tm, D_MODEL), lambda i: (i, 1)),
                  pl.BlockSpec((tm, D_MODEL), lambda i: (i, 2)),
                  pl.BlockSpec((tm, D_MODEL), lambda i: (i, 0)),
                  w_spec, w_spec, w_spec, pl.BlockSpec((D_MODEL, D_MODEL), lambda i: (0, 0))],
        out_specs=pl.BlockSpec((tm, D_MODEL), lambda i: (i, 0)),
        out_shape=jax.ShapeDtypeStruct((T, D_MODEL), F32),
        compiler_params=pltpu.CompilerParams(dimension_semantics=("parallel",), vmem_limit_bytes=VMEM_LIMIT_BYTES),
        name="merge",
    )(ya, yb, yc, za, za, za, x, wa, wb, wc, wo)


def _ffn_kernel(*refs, routed):
    if routed:
        h_ref, g_ref, wr_ref, br_ref, wg_ref, wu_ref, wd_ref, o_ref, xn_ref, gate_ref = refs
    else:
        h_ref, g_ref, wg_ref, wu_ref, wd_ref, o_ref, xn_ref = refs
    e = pl.program_id(1)
    f = pl.program_id(2)

    @pl.when((e == 0) & (f == 0))
    def _():
        h = h_ref[...]
        xb = _rms_rows(h, g_ref[...]).astype(BF16)
        xn_ref[...] = xb
        o_ref[...] = h
        if routed:
            logits = jnp.dot(xb, wr_ref[...], preferred_element_type=F32) + br_ref[...]
            lane = lax.broadcasted_iota(jnp.int32, logits.shape, 1)
            m1 = jnp.max(logits, axis=1, keepdims=True)
            i1 = jnp.min(jnp.where(logits == m1, lane, LANES), axis=1, keepdims=True)
            rest = jnp.where(lane == i1, M_INIT, logits)
            m2 = jnp.max(rest, axis=1, keepdims=True)
            i2 = jnp.min(jnp.where(rest == m2, lane, LANES), axis=1, keepdims=True)
            r = jnp.exp(m2 - m1)
            gate_ref[...] = jnp.where(lane == i1, 1.0 / (1.0 + r), 0.0) + jnp.where(lane == i2, r / (1.0 + r), 0.0)

    xb = xn_ref[...]
    a = jnp.dot(xb, wg_ref[0], preferred_element_type=F32)
    u = jnp.dot(xb, wu_ref[0], preferred_element_type=F32)
    t = (a * jax.nn.sigmoid(a)) * u
    y = jnp.dot(t.astype(BF16), wd_ref[0], preferred_element_type=F32)
    if routed:
        lane = lax.broadcasted_iota(jnp.int32, gate_ref.shape, 1)
        y = jnp.sum(jnp.where(lane == e, gate_ref[...], 0.0), axis=1, keepdims=True) * y
    o_ref[...] += y


def channel_mixer(h, norm_gain, wg, wu, wd, router=None):
    T = h.shape[0]
    E, _, F = wg.shape
    tm = _pick(T, (512, 256, 128))
    tf = _pick(F, (1408, 1024, 512, 256, 128))
    routed = router is not None
    row = lambda i, e, f: (i, 0)
    in_specs = [pl.BlockSpec((tm, D_MODEL), row), pl.BlockSpec((1, D_MODEL), lambda i, e, f: (0, 0))]
    args = [h, norm_gain.astype(F32)[None]]
    scratch = [pltpu.VMEM((tm, D_MODEL), BF16)]
    if routed:
        in_specs += [pl.BlockSpec((D_MODEL, LANES), lambda i, e, f: (0, 0)), pl.BlockSpec((1, LANES), lambda i, e, f: (0, 0))]
        args += list(router)
        scratch.append(pltpu.VMEM((tm, LANES), F32))
    in_specs += [pl.BlockSpec((1, D_MODEL, tf), lambda i, e, f: (e, 0, f)),
                 pl.BlockSpec((1, D_MODEL, tf), lambda i, e, f: (e, 0, f)),
                 pl.BlockSpec((1, tf, D_MODEL), lambda i, e, f: (e, f, 0))]
    args += [wg, wu, wd]
    return pl.pallas_call(
        functools.partial(_ffn_kernel, routed=routed),
        grid=(T // tm, E, F // tf),
        in_specs=in_specs,
        out_specs=pl.BlockSpec((tm, D_MODEL), row),
        out_shape=jax.ShapeDtypeStruct((T, D_MODEL), F32),
        scratch_shapes=scratch,
        compiler_params=pltpu.CompilerParams(dimension_semantics=("parallel", "arbitrary", "arbitrary"),
                                             vmem_limit_bytes=VMEM_LIMIT_BYTES),
        name="moe" if routed else "ffn",
    )(*args)


def mixer_tail(x2, za, y_a, y_b, y_c, l, P):
    return merge(y_a, y_b, y_c, za, x2, P['w_br_rg'][l].astype(BF16), P['w_br_attn'][l].astype(BF16),
                 P['w_br_pool'][l].astype(BF16), P['w_out'][l].astype(BF16))


def prompt_mixer(x, l, P, packed, tables):
    B, T, _ = x.shape
    x2 = x.reshape(B * T, D_MODEL)
    za, q, kv_c, kv_s, kv_w, gn = projection(x2, P['attn_norm'][l], packed['proj'])
    zeros = lambda rows: jnp.zeros((B, rows, D_RNN), F32)
    y_a, y_c, conv_new, h_last, pool_new = mixer_seq(za, B, T, packed['mix'], zeros(CONV_W - 1), jnp.zeros((B, D_RNN), F32),
                                                     zeros(POOL_MAX - 1), 0)
    ck, cv = compress(kv_c.reshape(B, T, KV_ROW), packed['cmp'])
    y_b = nsa_prompt_pallas(q.reshape(B, T, Q_COLS), ck, cv, kv_s.reshape(B, T, KV_ROW), kv_w.reshape(B, T, KV_ROW),
                            gn.reshape(B, T, GN_COLS), tables)
    out = mixer_tail(x2, za, y_a, y_b.reshape(B * T, Q_COLS), y_c, l, P)
    kv_shape = (B, T, 2, KV_HEADS, HEAD_DIM)
    state = (kv_c.reshape(kv_shape), kv_s.reshape(kv_shape), kv_w.reshape(kv_shape)[:, -min(WINDOW, T):], conv_new,
             h_last[:, 0], pool_new)
    return out.reshape(B, T, D_MODEL), state


def sample_mixer(x, l, P, packed, past_len, conv_state, h0, pool_state, caches, win_buf, page_table, step_tables):
    B, T, _ = x.shape
    x2 = x.reshape(B * T, D_MODEL)
    za, q, kv_c, kv_s, kv_w, gn = projection(x2, P['attn_norm'][l], packed['proj'])
    y_a, y_c, h_new = mixer_step(za, packed['mix'], conv_state, h0, pool_state, past_len)
    cmp_pools, sel_pools, win_bufs = caches
    ck, cv = compress(cmp_pools, packed['cmp'], page_table, layer=l)
    kv_shape = (B, T, 2, KV_HEADS, HEAD_DIM)
    wb = win_buf.shape[1]
    y_b = nsa_step(q, ck, cv, kv_s, kv_w, sel_pools, win_bufs, gn, page_table, step_tables, l)
    win_new = jnp.concatenate([win_buf[:, 1:], kv_w.reshape(kv_shape)], axis=1)
    out = mixer_tail(x2, za, y_a, y_b, y_c, l, P)
    conv_new = jnp.concatenate([conv_state[:, 1:], za[:, None, ZA_XRG:ZA_XRG + D_RNN]], axis=1)
    pool_new = jnp.concatenate([pool_state[:, 1:], za[:, None, ZA_XPOOL:ZA_XPOOL + D_POOL]], axis=1)
    state = (kv_c.reshape(kv_shape), kv_s.reshape(kv_shape), win_new, conv_new, h_new, pool_new)
    return out.reshape(B, T, D_MODEL), state


def ffn_layer(x, l, P, W):
    B, T, _ = x.shape
    i = l // 2
    if l % 2 == 0:
        y = channel_mixer(x.reshape(B * T, D_MODEL), P['ffn_norm'][l], W['ffn_g'][i], W['ffn_u'][i], W['ffn_d'][i])
    else:
        y = channel_mixer(x.reshape(B * T, D_MODEL), P['ffn_norm'][l], W['moe_g'][i], W['moe_u'][i], W['moe_d'][i],
                          router=W['router'][i])
    return y.reshape(B, T, D_MODEL)


def kernel(x_prompt, x_sample, cache_cmp_kv, cache_sel_kv, cache_win_kv, state_conv, state_rg_h, state_pool,
           page_table, attn_norm, w_in, conv_w, conv_b, rg_w_a, rg_b_a, rg_w_x, rg_b_x, rg_lambda, q_norm, k_norm,
           cmp_pe, w_cmp1, w_cmp2, rel_bias, w_pool, pool_scale, w_br_rg, w_br_attn, w_br_pool, w_out, ffn_norm,
           ffn_w_gate, ffn_w_up, ffn_w_down, w_router, b_router, moe_w_gate, moe_w_up, moe_w_down):
    P = dict(attn_norm=attn_norm, conv_w=conv_w, conv_b=conv_b, rg_w_a=rg_w_a, rg_b_a=rg_b_a,
             rg_w_x=rg_w_x, rg_b_x=rg_b_x, rg_lambda=rg_lambda, w_pool=w_pool,
             pool_scale=pool_scale, w_br_rg=w_br_rg, w_br_attn=w_br_attn, w_br_pool=w_br_pool, w_out=w_out,
             ffn_norm=ffn_norm)
    depth = w_in.shape[0]
    n_moe = w_router.shape[0]
    pad_e = LANES - N_EXPERTS
    W = dict(
        ffn_g=[w[None].astype(BF16) for w in ffn_w_gate], ffn_u=[w[None].astype(BF16) for w in ffn_w_up],
        ffn_d=[w[None].astype(BF16) for w in ffn_w_down],
        moe_g=[w.astype(BF16) for w in moe_w_gate], moe_u=[w.astype(BF16) for w in moe_w_up],
        moe_d=[w.astype(BF16) for w in moe_w_down],
        router=[(jnp.pad(w_router[i], ((0, 0), (0, pad_e))).astype(BF16),
                 jnp.pad(b_router[i].astype(F32), (0, pad_e), constant_values=NEG)[None]) for i in range(n_moe)])
    past_len = page_table.shape[1] * PAGE_SIZE
    y_p, y_s = x_prompt, x_sample
    tables = rel_bias_tables(rel_bias, x_prompt.shape[1])
    step_tables = step_bias_tables(rel_bias, past_len, cache_win_kv.shape[2], past_len // CMP_STRIDE)
    positions_minor = (0, 1, 3, 4, 5, 2)
    caches = tuple(jnp.transpose(c, positions_minor) for c in (cache_cmp_kv, cache_sel_kv, cache_win_kv))
    p_list, s_list = [], []
    for l in range(depth):
        packed = dict(proj=pack_projection(w_in[l], q_norm[l], k_norm[l, 1], k_norm[l, 2]),
                      cmp=pack_compress(w_cmp1[l], w_cmp2[l], cmp_pe[l], k_norm[l, 0]),
                      mix=pack_mixer(conv_w[l], conv_b[l], rg_w_a[l], rg_b_a[l], rg_w_x[l], rg_b_x[l], rg_lambda[l],
                                     w_pool[l], pool_scale[l]))
        y_p, st_p = prompt_mixer(y_p, l, P, packed, tables)
        y_p = ffn_layer(y_p, l, P, W)
        p_list.append(st_p)
        y_s, st_s = sample_mixer(y_s, l, P, packed, past_len, state_conv[l], state_rg_h[l], state_pool[l],
                                 caches, cache_win_kv[l], page_table, step_tables)
        y_s = ffn_layer(y_s, l, P, W)
        s_list.append(st_s)
    p_cmp_kv, p_sel_kv, p_win_kv, p_conv, p_h, p_pool = [jnp.stack(a) for a in zip(*p_list)]
    s_cmp_kv, s_sel_kv, s_win_kv, s_conv, s_h, s_pool = [jnp.stack(a) for a in zip(*s_list)]
    return (y_p, y_s, p_cmp_kv, p_sel_kv, p_win_kv, p_conv, p_h, p_pool,
            s_cmp_kv, s_sel_kv, s_win_kv, s_conv, s_h, s_pool)
```
